```python
import jax, jax.numpy as jnp
from jax import lax
import numpy as np

D_MODEL = 1024
BATCH = 32
SEQ = 256
DEPTH = 1
DEC_BATCH = 2
DEC_SEQ = 1024
PAST_LEN = 512

GRID_W = 64
N_HEADS = 8
N_KV_HEADS = 2
HEAD_DIM = 128
ROPE_THETA = 10000.0
ROT_PAIRS_PER_AXIS = HEAD_DIM // 4
Q_BLOCK = 128
GLA_HEADS = 4
GLA_DK = (D_MODEL // 2) // GLA_HEADS
GLA_DV = D_MODEL // GLA_HEADS
GLA_GATE_RANK = 16
GLA_GATE_NORM = 16.0
GLA_CHUNK = 64
N_EXPERTS = 16
EC_CAPACITY_FACTOR = 2
D_FF_EXPERT = D_MODEL
EPS = 1e-6
ATTN_Q = N_HEADS * HEAD_DIM
ATTN_KV = N_KV_HEADS * HEAD_DIM
GLA_K = GLA_HEADS * GLA_DK
GLA_V = GLA_HEADS * GLA_DV
IN_WIDTHS = (ATTN_Q, ATTN_KV, ATTN_KV, GLA_K, GLA_K, GLA_V, GLA_V, GLA_GATE_RANK, GLA_GATE_RANK, 2 * D_MODEL)
D_IN = sum(IN_WIDTHS)

kernel_name = "hybrid_diffusion_gqa_gla_ec_step"


def rms_norm(x, g):
    xf = x.astype(jnp.float32)
    y = xf * lax.rsqrt(jnp.mean(xf * xf, axis=-1, keepdims=True) + EPS)
    return (y * g.astype(jnp.float32)).astype(x.dtype)


def rope_2d(T):
    rows = T // GRID_W
    r = jnp.repeat(jnp.arange(rows), GRID_W).astype(jnp.float32)
    col = jnp.tile(jnp.arange(GRID_W), rows).astype(jnp.float32)
    freqs = ROPE_THETA ** (-jnp.arange(ROT_PAIRS_PER_AXIS, dtype=jnp.float32) / ROT_PAIRS_PER_AXIS)
    ang = jnp.concatenate([r[:, None] * freqs, col[:, None] * freqs], axis=-1)
    return jnp.cos(ang)[:, None, :], jnp.sin(ang)[:, None, :]


def apply_rope(x, cos, sin):
    xf = x.astype(jnp.float32)
    x1, x2 = xf[..., 0::2], xf[..., 1::2]
    out = jnp.stack([x1 * cos - x2 * sin, x1 * sin + x2 * cos], axis=-1)
    return out.reshape(x.shape).astype(x.dtype)


def attend(q, k, v):
    B, T, H, hd = q.shape
    G = H // N_KV_HEADS
    nb = T // Q_BLOCK
    qb = q.reshape(B, nb, Q_BLOCK, N_KV_HEADS, G, hd).transpose(1, 0, 2, 3, 4, 5)
    scale = hd ** -0.5

    def one_block(qblk):
        s = jnp.einsum('bqkgd,blkd->bkgql', qblk, k).astype(jnp.float32) * scale
        p = jax.nn.softmax(s, axis=-1).astype(v.dtype)
        return jnp.einsum('bkgql,blkd->bqkgd', p, v)

    o = lax.map(one_block, qb)
    return o.transpose(1, 0, 2, 3, 4, 5).reshape(B, T, H * hd)


def gla_chunked(q, k, v, logg, s0):
    B, H, T, dk = q.shape
    dv = v.shape[-1]
    N = T // GLA_CHUNK
    f32 = jnp.float32
    rs = lambda a: a.astype(f32).reshape(B, H, N, GLA_CHUNK, a.shape[-1])
    q, k, v, logg = rs(q), rs(k), rs(v), rs(logg)
    b = jnp.cumsum(logg, axis=3)
    b_last = b[:, :, :, -1:, :]
    qe = q * jnp.exp(b)
    ke = k * jnp.exp(-b)
    kd = k * jnp.exp(b_last - b)
    A = jnp.einsum('bhnid,bhnjd->bhnij', qe, ke)
    tril = jnp.tril(jnp.ones((GLA_CHUNK, GLA_CHUNK), dtype=bool))
    A = jnp.where(tril, A, 0.0)
    o_intra = jnp.einsum('bhnij,bhnjv->bhniv', A, v)
    decay = jnp.exp(b_last[:, :, :, 0, :])
    kv_chunk = jnp.einsum('bhnjd,bhnjv->bhndv', kd, v)

    def step(S, xs):
        dec, kvc = xs
        return dec[..., None] * S + kvc, S

    s_final, s_before = lax.scan(step, s0.astype(f32),
                                 (jnp.moveaxis(decay, 2, 0), jnp.moveaxis(kv_chunk, 2, 0)))
    s_before = jnp.moveaxis(s_before, 0, 2)
    o_inter = jnp.einsum('bhnid,bhndv->bhniv', qe, s_before)
    return (o_intra + o_inter).reshape(B, H, T, dv), s_final


def ec_ffn(h, w_router, w1, w3, w2):
    B, T, D = h.shape
    n = B * T
    cap = (EC_CAPACITY_FACTOR * n) // N_EXPERTS
    xt = h.reshape(n, D)
    aff = jax.nn.softmax((xt @ w_router).astype(jnp.float32), axis=-1)
    top_v, top_i = lax.top_k(aff.T, cap)
    xs = jnp.take(xt, top_i, axis=0)
    hid = jax.nn.silu(jnp.einsum('ecd,edf->ecf', xs, w1)) * jnp.einsum('ecd,edf->ecf', xs, w3)
    ye = jnp.einsum('ecf,efd->ecd', hid, w2) * top_v[..., None].astype(h.dtype)
    out = jnp.zeros((n, D), h.dtype).at[top_i.reshape(-1)].add(ye.reshape(-1, D))
    return out.reshape(B, T, D)


def trunk_layer(x, mod, rope, ctx, lw):
    (g_pre_mix, g_post_mix, g_pre_ffn, g_post_ffn, w_in, g_q, g_k, w_gk2_f, b_gk_f,
     w_gk2_b, b_gk_b, g_gla, w_pa, w_pg, w_out, w_router, w1, w3, w2) = lw
    B, T, D = x.shape
    sh_m, sc_m, gt_m, sh_f, sc_f, gt_f = jnp.split(mod[:, None, :], 6, axis=-1)

    h = rms_norm(x, g_pre_mix) * (1.0 + sc_m) + sh_m
    proj = h @ w_in
    idx = np.cumsum(IN_WIDTHS)[:-1].tolist()
    q_a, k_a, v_a, q_g, k_g, v_g, r_g, gk_f, gk_b, gates = jnp.split(proj, idx, axis=-1)

    q_a = rms_norm(q_a.reshape(B, T, N_HEADS, HEAD_DIM), g_q)
    k_a = rms_norm(k_a.reshape(B, T, N_KV_HEADS, HEAD_DIM), g_k)
    v_a = v_a.reshape(B, T, N_KV_HEADS, HEAD_DIM)
    if rope is not None:
        q_a = apply_rope(q_a, *rope)
        k_a = apply_rope(k_a, *rope)
    if ctx is None:
        k_all, v_all = k_a, v_a
        s_f0 = jnp.zeros((B, GLA_HEADS, GLA_DK, GLA_DV), x.dtype)
        s_b0 = s_f0
    else:
        ck, cv, s_f0, s_b0 = ctx
        k_all = jnp.concatenate([ck.transpose(0, 2, 1, 3), k_a], axis=1)
        v_all = jnp.concatenate([cv.transpose(0, 2, 1, 3), v_a], axis=1)
    o_a = attend(q_a, k_all, v_all) @ w_pa

    heads = lambda a, d: a.reshape(B, T, GLA_HEADS, d).transpose(0, 2, 1, 3)
    qg = heads(q_g, GLA_DK) * (GLA_DK ** -0.5)
    kg = heads(k_g, GLA_DK)
    vg = heads(v_g, GLA_DV)
    lg_f = heads(jax.nn.log_sigmoid((gk_f @ w_gk2_f + b_gk_f).astype(jnp.float32)) / GLA_GATE_NORM, GLA_DK)
    lg_b = heads(jax.nn.log_sigmoid((gk_b @ w_gk2_b + b_gk_b).astype(jnp.float32)) / GLA_GATE_NORM, GLA_DK)
    o_f, s_f = gla_chunked(qg, kg, vg, lg_f, s_f0)
    flip = lambda a: jnp.flip(a, axis=2)
    o_b, s_b = gla_chunked(flip(qg), flip(kg), flip(vg), flip(lg_b), s_b0)
    o_g = (o_f + flip(o_b)).astype(x.dtype).transpose(0, 2, 1, 3)
    o_g = rms_norm(o_g, g_gla).reshape(B, T, GLA_V) * jax.nn.silu(r_g)
    o_g = o_g @ w_pg

    g_a, g_b = jnp.split(jax.nn.sigmoid(gates), 2, axis=-1)
    mix = (g_a * o_a + g_b * o_g) @ w_out
    x = x + gt_m * rms_norm(mix, g_post_mix)

    h = rms_norm(x, g_pre_ffn) * (1.0 + sc_f) + sh_f
    x = x + gt_f * rms_norm(ec_ffn(h, w_router, w1, w3, w2), g_post_ffn)

    ctx_out = (k_a.transpose(0, 2, 1, 3), v_a.transpose(0, 2, 1, 3),
               s_f.astype(x.dtype), s_b.astype(x.dtype))
    return x, ctx_out


def setup_inputs(seed: int = 0) -> dict:
    key = jax.random.key(seed)
    ks = jax.random.split(key, 40)
    nrm = lambda i, shape, s: jax.random.normal(ks[i], shape, jnp.float32) * s
    gain = lambda i, shape: 1.0 + nrm(i, shape, 0.05)
    L = DEPTH
    return {
        "x_prompt": nrm(0, (BATCH, SEQ, D_MODEL), 1.0),
        "x_sample": nrm(1, (DEC_BATCH, DEC_SEQ, D_MODEL), 1.0),
        "cache_k": nrm(2, (DEC_BATCH, L, N_KV_HEADS, PAST_LEN, HEAD_DIM), 1.0),
        "cache_v": nrm(3, (DEC_BATCH, L, N_KV_HEADS, PAST_LEN, HEAD_DIM), 1.0),
        "state_gla_fwd": nrm(4, (DEC_BATCH, L, GLA_HEADS, GLA_DK, GLA_DV), 0.5),
        "state_gla_bwd": nrm(5, (DEC_BATCH, L, GLA_HEADS, GLA_DK, GLA_DV), 0.5),
        "c": nrm(6, (DEC_BATCH, D_MODEL), 1.0),
        "c_ctx": nrm(7, (D_MODEL,), 1.0),
        "g_pre_mix": gain(8, (L, D_MODEL)),
        "g_post_mix": gain(9, (L, D_MODEL)),
        "g_pre_ffn": gain(10, (L, D_MODEL)),
        "g_post_ffn": gain(11, (L, D_MODEL)),
        "w_mod": nrm(12, (L, D_MODEL, 6 * D_MODEL), 0.5 * D_MODEL ** -0.5),
        "b_mod": nrm(13, (L, 6 * D_MODEL), 0.02),
        "w_in": nrm(14, (L, D_MODEL, D_IN), D_MODEL ** -0.5),
        "g_q": gain(15, (L, HEAD_DIM)),
        "g_k": gain(16, (L, HEAD_DIM)),
        "w_gk2_f": nrm(17, (L, GLA_GATE_RANK, GLA_K), GLA_GATE_RANK ** -0.5),
        "b_gk_f": nrm(18, (L, GLA_K), 0.1),
        "w_gk2_b": nrm(19, (L, GLA_GATE_RANK, GLA_K), GLA_GATE_RANK ** -0.5),
        "b_gk_b": nrm(20, (L, GLA_K), 0.1),
        "g_gla": gain(21, (L, GLA_DV)),
        "w_pa": nrm(22, (L, ATTN_Q, D_MODEL), ATTN_Q ** -0.5),
        "w_pg": nrm(23, (L, GLA_V, D_MODEL), GLA_V ** -0.5),
        "w_out": nrm(24, (L, D_MODEL, D_MODEL), D_MODEL ** -0.5),
        "w_router": nrm(25, (L, D_MODEL, N_EXPERTS), D_MODEL ** -0.5),
        "w1": nrm(26, (L, N_EXPERTS, D_MODEL, D_FF_EXPERT), D_MODEL ** -0.5),
        "w3": nrm(27, (L, N_EXPERTS, D_MODEL, D_FF_EXPERT), D_MODEL ** -0.5),
        "w2": nrm(28, (L, N_EXPERTS, D_FF_EXPERT, D_MODEL), D_FF_EXPERT ** -0.5),
    }


def reference(x_prompt, x_sample, cache_k, cache_v, state_gla_fwd, state_gla_bwd, c, c_ctx,
              g_pre_mix, g_post_mix, g_pre_ffn, g_post_ffn, w_mod, b_mod, w_in, g_q, g_k,
              w_gk2_f, b_gk_f, w_gk2_b, b_gk_b, g_gla, w_pa, w_pg, w_out, w_router, w1, w3, w2):
    rope = rope_2d(x_sample.shape[1])
    xp, xs = x_prompt, x_sample
    ks, vs, sfs, sbs = [], [], [], []
    for l in range(DEPTH):
        lw = (g_pre_mix[l], g_post_mix[l], g_pre_ffn[l], g_post_ffn[l], w_in[l], g_q[l], g_k[l],
              w_gk2_f[l], b_gk_f[l], w_gk2_b[l], b_gk_b[l], g_gla[l], w_pa[l], w_pg[l], w_out[l],
              w_router[l], w1[l], w3[l], w2[l])
        mod_ctx = jax.nn.silu(c_ctx)[None, :] @ w_mod[l] + b_mod[l]
        mod_lat = jax.nn.silu(c) @ w_mod[l] + b_mod[l]
        xp, (nk, nv, nsf, nsb) = trunk_layer(xp, mod_ctx, None, None, lw)
        ks.append(nk); vs.append(nv); sfs.append(nsf); sbs.append(nsb)
        ctx = (cache_k[:, l], cache_v[:, l], state_gla_fwd[:, l], state_gla_bwd[:, l])
        xs, _ = trunk_layer(xs, mod_lat, rope, ctx, lw)
    new_cache_k = jnp.stack(ks, axis=1)
    new_cache_v = jnp.stack(vs, axis=1)
    new_state_gla_fwd = jnp.stack(sfs, axis=1)
    new_state_gla_bwd = jnp.stack(sbs, axis=1)
    return (xp, xs, new_cache_k, new_cache_v, new_state_gla_fwd, new_state_gla_bwd)
```

```python
import functools

import jax
import jax.numpy as jnp
from jax import lax
from jax.experimental import pallas as pl
from jax.experimental.pallas import tpu as pltpu

F32 = jnp.float32
BF16 = jnp.bfloat16
I32 = jnp.int32

N_HEADS = 8
N_KV_HEADS = 2
HEAD_DIM = 128
GRID_W = 64
ROPE_THETA = 10000.0
GLA_HEADS = 4
GLA_GATE_RANK = 16
GLA_GATE_NORM = 16.0
GLA_CHUNK = 64
N_EXPERTS = 16
EC_CAPACITY_FACTOR = 2
EPS = 1e-6

V7X_LANES = 128
V7X_VMEM_BYTES = 64 * 1024 * 1024
V7X_VMEM_RESERVE_BYTES = 6 * 1024 * 1024
BF16_ROWS_PER_TILE = 16

TOKEN_TILE = 512
ATTN_Q_TILE = 256
GLA_BLOCK = 256
ROUTE_BLOCK = 256
FFN_ROW_TILE = 256
FFN_F_TILES = 2
COMBINE_WINDOW = 256
MOD_ROWS = 8
MOD_N_TILE = 1536


def _cparams(semantics, vmem_bytes):
    limit = min(max(int(vmem_bytes), 32 * 1024 * 1024), V7X_VMEM_BYTES - V7X_VMEM_RESERVE_BYTES)
    return pltpu.CompilerParams(dimension_semantics=semantics, vmem_limit_bytes=limit)


def _sigmoid(x):
    return 1.0 / (1.0 + jnp.exp(-x))


def _silu(x):
    return x * _sigmoid(x)


def _log_sigmoid(x):
    return jnp.minimum(x, 0.0) - jnp.log1p(jnp.exp(-jnp.abs(x)))


def _rms(x, g):
    ms = jnp.mean(x * x, axis=-1, keepdims=True)
    return x * lax.rsqrt(ms + EPS) * g


def _dot(a, b):
    return jnp.dot(a, b, preferred_element_type=F32)


def _dot_nt(a, b):
    return lax.dot_general(a, b, (((1,), (1,)), ((), ())), preferred_element_type=F32)


def _mod_kernel(c_ref, w_ref, b_ref, o_ref):
    s = _silu(c_ref[...]).astype(BF16)
    o_ref[...] = _dot(s, w_ref[...].astype(BF16)) + b_ref[...]


def _mod_call(cc, w_mod, b_mod):
    d, n6 = w_mod.shape
    tn = MOD_N_TILE
    return pl.pallas_call(
        _mod_kernel,
        out_shape=jax.ShapeDtypeStruct((MOD_ROWS, n6), F32),
        grid=(n6 // tn,),
        in_specs=[pl.BlockSpec((MOD_ROWS, d), lambda j: (0, 0)),
                  pl.BlockSpec((d, tn), lambda j: (0, j)),
                  pl.BlockSpec((1, tn), lambda j: (0, j))],
        out_specs=pl.BlockSpec((MOD_ROWS, tn), lambda j: (0, j)),
        compiler_params=_cparams(("arbitrary",), 3 * d * tn * 4),
        name="mod",
    )(cc, w_mod, b_mod)


def _inproj_layout(d):
    aq, akv = N_HEADS * HEAD_DIM, N_KV_HEADS * HEAD_DIM
    gk, gv = d // 2, d
    names = ("q_a", "k_a", "v_a", "q_g", "k_g", "v_g", "r_g", "gk", "gates")
    widths = (aq, akv, akv, gk, gk, gv, gv, V7X_LANES, 2 * d)
    off, o = {}, 0
    for nme, w in zip(names, widths):
        off[nme] = (o, o + w)
        o += w
    return off, o


def _inproj_kernel(*refs, rope, d):
    if rope:
        (x_ref, mod_ref, gpre_ref, w_ref, gq_ref, gk_ref, wgf_ref, bgf_ref, wgb_ref, bgb_ref,
         cos_ref, se_ref, so_ref, *outs) = refs
    else:
        (x_ref, mod_ref, gpre_ref, w_ref, gq_ref, gk_ref, wgf_ref, bgf_ref, wgb_ref, bgb_ref,
         *outs) = refs
    qa_ref, k_ref, v_ref, qg_ref, kg_ref, vg_ref, rg_ref, lgf_ref, lgb_ref, gate_ref = outs
    off, _ = _inproj_layout(d)
    m = mod_ref[0]
    h = _rms(x_ref[...], gpre_ref[...]) * (1.0 + m[:, d:2 * d]) + m[:, 0:d]
    hb = h.astype(BF16)

    def proj(name):
        a, b = off[name]
        return _dot(hb, w_ref[:, a:b])

    def qk_norm(y, g_ref):
        y = _rms(y, g_ref[...])
        if rope:
            nxt = pltpu.roll(y, HEAD_DIM - 1, axis=1)
            prv = pltpu.roll(y, 1, axis=1)
            y = y * cos_ref[...] + nxt * se_ref[...] + prv * so_ref[...]
        return y

    q = proj("q_a")
    scale = HEAD_DIM ** -0.5
    for hd in range(N_HEADS):
        sl = slice(hd * HEAD_DIM, (hd + 1) * HEAD_DIM)
        qa_ref[:, sl] = (qk_norm(q[:, sl], gq_ref) * scale).astype(BF16)

    k = proj("k_a")
    v = proj("v_a")
    tb, _, _, ts, _ = k_ref.shape
    for kv in range(N_KV_HEADS):
        sl = slice(kv * HEAD_DIM, (kv + 1) * HEAD_DIM)
        k_ref[:, 0, kv] = qk_norm(k[:, sl], gk_ref).reshape(tb, ts, HEAD_DIM)
        v_ref[:, 0, kv] = v[:, sl].reshape(tb, ts, HEAD_DIM)

    dk = (d // 2) // GLA_HEADS
    qg_ref[...] = (proj("q_g") * (dk ** -0.5)).astype(BF16)
    kg_ref[...] = proj("k_g").astype(BF16)
    vg_ref[...] = proj("v_g").astype(BF16)
    rg_ref[...] = _silu(proj("r_g")).astype(BF16)

    gk = proj("gk").astype(BF16)
    lgf_ref[...] = _log_sigmoid(_dot(gk, wgf_ref[...]) + bgf_ref[...]) * (1.0 / GLA_GATE_NORM)
    lgb_ref[...] = _log_sigmoid(_dot(gk, wgb_ref[...]) + bgb_ref[...]) * (1.0 / GLA_GATE_NORM)

    gate_ref[...] = _sigmoid(proj("gates")).astype(BF16)


def _inproj_call(x2, mod3, mod_row_of_tile, g_pre, w_in_p, g_q, g_k, wgf, bgf, wgb, bgb, rope_tabs,
                 batch, seq):
    n, d = x2.shape
    tm = TOKEN_TILE
    _, dinp = _inproj_layout(d)
    rope = rope_tabs is not None
    gk_w = d // 2
    if seq >= tm:
        tb, ts, per = 1, tm, seq // tm
        kv_map = lambda i: (i // per, 0, 0, i % per, 0)
    else:
        tb, ts, per = tm // seq, seq, 1
        kv_map = lambda i: (i, 0, 0, 0, 0)
    row = lambda i: (i, 0)
    const = lambda i: (0, 0)
    in_specs = [
        pl.BlockSpec((tm, d), row),
        pl.BlockSpec((1, 1, 6 * d), lambda i: (mod_row_of_tile(i), 0, 0)),
        pl.BlockSpec((1, d), const),
        pl.BlockSpec((d, dinp), const, pipeline_mode=pl.Buffered(1)),
        pl.BlockSpec((1, HEAD_DIM), const),
        pl.BlockSpec((1, HEAD_DIM), const),
        pl.BlockSpec((V7X_LANES, gk_w), const),
        pl.BlockSpec((1, gk_w), const),
        pl.BlockSpec((V7X_LANES, gk_w), const),
        pl.BlockSpec((1, gk_w), const),
    ]
    args = [x2, mod3, g_pre, w_in_p, g_q, g_k, wgf, bgf, wgb, bgb]
    if rope:
        tab = pl.BlockSpec((tm, HEAD_DIM), lambda i: (i % per, 0))
        in_specs += [tab, tab, tab]
        args += list(rope_tabs)
    kv_shape = jax.ShapeDtypeStruct((batch, 1, N_KV_HEADS, seq, HEAD_DIM), F32)
    kv_spec = pl.BlockSpec((tb, 1, N_KV_HEADS, ts, HEAD_DIM), kv_map)
    out_shape = (
        jax.ShapeDtypeStruct((n, N_HEADS * HEAD_DIM), BF16), kv_shape, kv_shape,
        jax.ShapeDtypeStruct((n, gk_w), BF16), jax.ShapeDtypeStruct((n, gk_w), BF16),
        jax.ShapeDtypeStruct((n, d), BF16), jax.ShapeDtypeStruct((n, d), BF16),
        jax.ShapeDtypeStruct((n, gk_w), F32), jax.ShapeDtypeStruct((n, gk_w), F32),
        jax.ShapeDtypeStruct((n, 2 * d), BF16),
    )
    out_specs = (
        pl.BlockSpec((tm, N_HEADS * HEAD_DIM), row), kv_spec, kv_spec,
        pl.BlockSpec((tm, gk_w), row), pl.BlockSpec((tm, gk_w), row),
        pl.BlockSpec((tm, d), row), pl.BlockSpec((tm, d), row),
        pl.BlockSpec((tm, gk_w), row), pl.BlockSpec((tm, gk_w), row),
        pl.BlockSpec((tm, 2 * d), row),
    )
    out_row_bytes = 2 * (N_HEADS * HEAD_DIM + 2 * gk_w + 2 * d + 2 * d) + 4 * (4 * HEAD_DIM + 2 * gk_w)
    vmem = d * dinp * 2 + 2 * tm * (d * 4 + out_row_bytes) + 6 * tm * 2 * d * 4
    return pl.pallas_call(
        functools.partial(_inproj_kernel, rope=rope, d=d),
        out_shape=out_shape,
        grid=(n // tm,),
        in_specs=in_specs,
        out_specs=out_specs,
        compiler_params=_cparams(("parallel",), vmem),
        name="inproj_lat" if rope else "inproj_ctx",
    )(*args)


def _attn_kernel(*refs, cached):
    if cached:
        q_ref, k_ref, v_ref, ck_ref, cv_ref, o_ref = refs
    else:
        q_ref, k_ref, v_ref, o_ref = refs
    tq = q_ref.shape[0]
    grp = N_HEADS // N_KV_HEADS
    for kv in range(N_KV_HEADS):
        kk = k_ref[0, 0, kv].astype(BF16)
        vv = v_ref[0, 0, kv].astype(BF16)
        if cached:
            kk = jnp.concatenate([ck_ref[0, 0, kv].astype(BF16), kk], axis=0)
            vv = jnp.concatenate([cv_ref[0, 0, kv].astype(BF16), vv], axis=0)
        heads = [q_ref[:, (kv * grp + g) * HEAD_DIM:(kv * grp + g + 1) * HEAD_DIM] for g in range(grp)]
        q4 = jnp.concatenate(heads, axis=0)
        s = _dot_nt(q4, kk)
        p = jnp.exp(s - jnp.max(s, axis=-1, keepdims=True))
        l = jnp.sum(p, axis=-1, keepdims=True)
        o = _dot(p.astype(BF16), vv) / l
        for g in range(grp):
            hd = kv * grp + g
            o_ref[:, hd * HEAD_DIM:(hd + 1) * HEAD_DIM] = o[g * tq:(g + 1) * tq].astype(BF16)


def _attn_call(q_a, k_a, v_a, cache_k, cache_v):
    batch, _, _, seq, _ = k_a.shape
    n, aq = q_a.shape
    tq = ATTN_Q_TILE
    per = seq // tq
    cached = cache_k is not None
    own = pl.BlockSpec((1, 1, N_KV_HEADS, seq, HEAD_DIM), lambda b, j: (b, 0, 0, 0, 0))
    in_specs = [pl.BlockSpec((tq, aq), lambda b, j: (b * per + j, 0)), own, own]
    args = [q_a, k_a, v_a]
    klen = seq
    if cached:
        past = cache_k.shape[3]
        cspec = pl.BlockSpec((1, 1, N_KV_HEADS, past, HEAD_DIM), lambda b, j: (b, 0, 0, 0, 0))
        in_specs += [cspec, cspec]
        args += [cache_k, cache_v]
        klen += past
    grp = N_HEADS // N_KV_HEADS
    vmem = 4 * grp * tq * klen * 4 + 8 * N_KV_HEADS * klen * HEAD_DIM * 4 + 8 * tq * aq * 2
    return pl.pallas_call(
        functools.partial(_attn_kernel, cached=cached),
        out_shape=jax.ShapeDtypeStruct((n, aq), BF16),
        grid=(batch, per),
        in_specs=in_specs,
        out_specs=pl.BlockSpec((tq, aq), lambda b, j: (b * per + j, 0)),
        compiler_params=_cparams(("parallel", "parallel"), vmem),
        name="attn_lat" if cached else "attn_ctx",
    )(*args)


def _gla_kernel(*refs, nblk, has_state, emit_state):
    refs = list(refs)
    q_ref, k_ref, v_ref, lgf_ref, lgb_ref, rg_ref, gg_ref = refs[:7]
    pos = 7
    if has_state:
        s0f_ref, s0b_ref = refs[pos:pos + 2]
        pos += 2
    og_ref = refs[pos]
    pos += 1
    if emit_state:
        sf_ref, sb_ref = refs[pos:pos + 2]
        pos += 2
    of_scr, ob_scr = refs[pos:pos + 2]

    blk = GLA_BLOCK
    ch = GLA_CHUNK
    nch = blk // ch
    dk = q_ref.shape[1]
    dv = v_ref.shape[1]
    shift = ch.bit_length() - 1
    row_in_chunk = lax.broadcasted_iota(I32, (blk, dk), 0) & (ch - 1)
    ri = lax.broadcasted_iota(I32, (blk, blk), 0)
    ci = lax.broadcasted_iota(I32, (blk, blk), 1)
    same = (ri >> shift) == (ci >> shift)
    col_chunk = lax.broadcasted_iota(I32, (dk, blk), 1) >> shift

    def one_block(b0, state, reverse):
        rows = pl.ds(b0, blk)
        q = q_ref[rows, :].astype(F32)
        k = k_ref[rows, :].astype(F32)
        v = v_ref[rows, :]
        b = (lgb_ref if reverse else lgf_ref)[rows, :]
        s = 1
        while s < ch:
            if reverse:
                sh = pltpu.roll(b, blk - s, axis=0)
                b = b + jnp.where(row_in_chunk < ch - s, sh, 0.0)
            else:
                sh = pltpu.roll(b, s, axis=0)
                b = b + jnp.where(row_in_chunk >= s, sh, 0.0)
            s *= 2
        qe = (q * jnp.exp(b)).astype(BF16)
        ke = (k * jnp.exp(-b)).astype(BF16)
        tri = (ci >= ri) if reverse else (ci <= ri)
        a = jnp.where(same & tri, _dot_nt(qe, ke), 0.0)
        o = _dot(a.astype(BF16), v)

        kt = k.T
        bt = b.T
        order = range(nch - 1, -1, -1) if reverse else range(nch)
        ends = [bt[:, (c * ch + (0 if reverse else ch - 1)):(c * ch + (0 if reverse else ch - 1)) + 1]
                for c in range(nch)]
        b_end = ends[nch - 1]
        for c in range(nch - 2, -1, -1):
            b_end = jnp.where(col_chunk == c, ends[c], b_end)
        kdt = kt * jnp.exp(b_end - bt)
        inter = [None] * nch
        for c in order:
            inter[c] = _dot(qe[c * ch:(c + 1) * ch], state.astype(BF16))
            kv_c = _dot(jnp.where(col_chunk == c, kdt, 0.0).astype(BF16), v)
            state = jnp.exp(ends[c]) * state + kv_c
        return o + jnp.concatenate(inter, axis=0), state

    sf = s0f_ref[0, 0, 0] if has_state else jnp.zeros((dk, dv), F32)
    sb = s0b_ref[0, 0, 0] if has_state else jnp.zeros((dk, dv), F32)
    for i in range(nblk):
        j = nblk - 1 - i
        o_f, sf = one_block(i * blk, sf, False)
        o_b, sb = one_block(j * blk, sb, True)
        of_scr[pl.ds(i * blk, blk), :] = o_f
        ob_scr[pl.ds(j * blk, blk), :] = o_b
    if emit_state:
        sf_ref[0, 0, 0] = sf
        sb_ref[0, 0, 0] = sb
    o = of_scr[...] + ob_scr[...]
    og_ref[...] = (_rms(o, gg_ref[...]) * rg_ref[...].astype(F32)).astype(BF16)


def _gla_call(q_g, k_g, v_g, lg_f, lg_b, r_g, g_gla, state_f, state_b, batch, seq, emit_state):
    n, gkw = q_g.shape
    d = v_g.shape[1]
    dk, dv = gkw // GLA_HEADS, d // GLA_HEADS
    has_state = state_f is not None
    kspec = pl.BlockSpec((seq, dk), lambda b, h: (b, h))
    vspec = pl.BlockSpec((seq, dv), lambda b, h: (b, h))
    sspec = pl.BlockSpec((1, 1, 1, dk, dv), lambda b, h: (b, 0, h, 0, 0))
    in_specs = [kspec, kspec, vspec, kspec, kspec, vspec, pl.BlockSpec((1, dv), lambda b, h: (0, 0))]
    args = [q_g, k_g, v_g, lg_f, lg_b, r_g, g_gla]
    if has_state:
        in_specs += [sspec, sspec]
        args += [state_f, state_b]
    out_shape = [jax.ShapeDtypeStruct((n, d), BF16)]
    out_specs = [vspec]
    if emit_state:
        st = jax.ShapeDtypeStruct((batch, 1, GLA_HEADS, dk, dv), F32)
        out_shape += [st, st]
        out_specs += [sspec, sspec]
    vmem = 6 * seq * dv * 4 + 40 * GLA_BLOCK * GLA_BLOCK * 4
    return pl.pallas_call(
        functools.partial(_gla_kernel, nblk=seq // GLA_BLOCK, has_state=has_state, emit_state=emit_state),
        out_shape=tuple(out_shape),
        grid=(batch, GLA_HEADS),
        in_specs=in_specs,
        out_specs=tuple(out_specs),
        scratch_shapes=[pltpu.VMEM((seq, dv), F32), pltpu.VMEM((seq, dv), F32)],
        compiler_params=_cparams(("parallel", "parallel"), vmem),
        name="gla_ctx" if emit_state else "gla_lat",
    )(*args)


def _outproj_kernel(oa_ref, og_ref, gate_ref, x_ref, mod_ref, wpa_ref, wpg_ref, wout_ref, gpm_ref,
                    gpf_ref, wr_ref, x1_ref, h_ref, aff_ref):
    d = x_ref.shape[1]
    m = mod_ref[0]
    oa = _dot(oa_ref[...], wpa_ref[...])
    og = _dot(og_ref[...], wpg_ref[...])
    mix = gate_ref[:, 0:d].astype(F32) * oa + gate_ref[:, d:2 * d].astype(F32) * og
    mo = _dot(mix.astype(BF16), wout_ref[...])
    x1 = x_ref[...] + m[:, 2 * d:3 * d] * _rms(mo, gpm_ref[...])
    x1_ref[...] = x1
    hb = (_rms(x1, gpf_ref[...]) * (1.0 + m[:, 4 * d:5 * d]) + m[:, 3 * d:4 * d]).astype(BF16)
    h_ref[...] = hb
    logits = _dot(hb, wr_ref[...])
    valid = lax.broadcasted_iota(I32, logits.shape, 1) < N_EXPERTS
    mx = jnp.max(jnp.where(valid, logits, -jnp.inf), axis=-1, keepdims=True)
    ex = jnp.where(valid, jnp.exp(logits - mx), 0.0)
    aff_ref[...] = ex / jnp.sum(ex, axis=-1, keepdims=True)


def _outproj_call(o_a, o_g, gates, x2, mod3, mod_row_of_tile, w_pa, w_pg, w_out, g_pm, g_pf, w_r, tag):
    n, d = x2.shape
    tm = TOKEN_TILE
    row = lambda i: (i, 0)
    const = lambda i: (0, 0)
    wspec = pl.BlockSpec((d, d), const)
    vmem = 2 * 3 * d * d * 2 + 2 * tm * d * (2 + 2 + 4 + 4 + 4 + 2) + 8 * tm * d * 4
    return pl.pallas_call(
        _outproj_kernel,
        out_shape=(jax.ShapeDtypeStruct((n, d), F32), jax.ShapeDtypeStruct((n, d), BF16),
                   jax.ShapeDtypeStruct((n, V7X_LANES), F32)),
        grid=(n // tm,),
        in_specs=[pl.BlockSpec((tm, d), row), pl.BlockSpec((tm, d), row), pl.BlockSpec((tm, 2 * d), row),
                  pl.BlockSpec((tm, d), row),
                  pl.BlockSpec((1, 1, 6 * d), lambda i: (mod_row_of_tile(i), 0, 0)),
                  wspec, wspec, wspec, pl.BlockSpec((1, d), const), pl.BlockSpec((1, d), const),
                  pl.BlockSpec((d, V7X_LANES), const)],
        out_specs=(pl.BlockSpec((tm, d), row), pl.BlockSpec((tm, d), row),
                   pl.BlockSpec((tm, V7X_LANES), row)),
        compiler_params=_cparams(("parallel",), vmem),
        name="outproj_" + tag,
    )(o_a, o_g, gates, x2, mod3, w_pa, w_pg, w_out, g_pm, g_pf, w_r)


def _route_kernel(aff_ref, pos_ref, post_ref, tbl_ref, bnd_ref, *, cap, n_row_tiles):
    n = aff_ref.shape[0]
    rb = ROUTE_BLOCK
    nb = n // rb
    lanes = aff_ref.shape[1]
    lane = lax.broadcasted_iota(I32, (1, lanes), 1)
    expert_lane = lane < N_EXPERTS
    tbl_ref[...] = jnp.zeros(tbl_ref.shape, I32)
    bnd_ref[...] = jnp.zeros(bnd_ref.shape, I32)

    def bits_of(c):
        start = pl.multiple_of(c * rb, rb)
        return lax.bitcast_convert_type(aff_ref[pl.ds(start, rb), :], I32)

    def count_ge(t):
        def body(c, acc):
            return acc + jnp.sum((bits_of(c) >= t).astype(I32), axis=0, keepdims=True)
        return lax.fori_loop(0, nb, body, jnp.zeros((1, lanes), I32))

    def bit_step(i, lo):
        t = lo | jnp.left_shift(jnp.int32(1), 30 - i)
        return jnp.where(count_ge(t) >= cap, t, lo)

    thr = lax.fori_loop(0, 31, bit_step, jnp.zeros((1, lanes), I32))
    need = (cap - count_ge(thr + 1)).astype(F32)

    r = lax.broadcasted_iota(I32, (rb, rb), 0)
    c_ = lax.broadcasted_iota(I32, (rb, rb), 1)
    tril = jnp.where(c_ <= r, 1.0, 0.0).astype(BF16)

    def blk_step(c, carry):
        eq_before, sel_before, lo_cnt, hi_cnt = carry
        bits = bits_of(c)
        gt = bits > thr
        eq = bits == thr
        eq_incl = _dot(tril, jnp.where(eq, 1.0, 0.0).astype(BF16)) + eq_before
        sel = (gt | (eq & (eq_incl <= need))) & expert_lane
        self_ = jnp.where(sel, 1.0, 0.0)
        incl = _dot(tril, self_.astype(BF16)) + sel_before
        excl = incl - self_
        posb = jnp.where(sel, excl, -1.0).astype(I32)
        start = pl.multiple_of(c * rb, rb)
        pos_ref[pl.ds(start, rb), :] = posb
        post_ref[c] = posb.T[0:N_EXPERTS, :]
        tbl_ref[pl.ds(c, 1), :] = sel_before.astype(I32)
        lo_new, hi_new = [], []
        for j in range(n_row_tiles):
            lo_new.append(lo_cnt[j] + jnp.sum(jnp.where(incl <= j * FFN_ROW_TILE, 1.0, 0.0), axis=0, keepdims=True))
            hi_new.append(hi_cnt[j] + jnp.sum(jnp.where(excl < (j + 1) * FFN_ROW_TILE, 1.0, 0.0), axis=0, keepdims=True))
        return (eq_incl[rb - 1:rb, :], incl[rb - 1:rb, :], tuple(lo_new), tuple(hi_new))

    zero = jnp.zeros((1, lanes), F32)
    zeros_j = tuple(zero for _ in range(n_row_tiles))
    _, total, lo_cnt, hi_cnt = lax.fori_loop(0, nb, blk_step, (zero, zero, zeros_j, zeros_j))
    tbl_ref[pl.ds(nb, 1), :] = total.astype(I32)
    for j in range(n_row_tiles):
        bnd_ref[pl.ds(j, 1), :] = lo_cnt[j].astype(I32)
        bnd_ref[pl.ds(n_row_tiles + j, 1), :] = hi_cnt[j].astype(I32)


def _route_call(aff, cap, tag):
    n, lanes = aff.shape
    nb = n // ROUTE_BLOCK
    n_row_tiles = cap // FFN_ROW_TILE
    tbl_rows = -(-(nb + 1) // 8) * 8
    bnd_rows = -(-(2 * n_row_tiles) // 8) * 8
    full = lambda *shape: pl.BlockSpec(shape, lambda: tuple(0 for _ in shape))
    return pl.pallas_call(
        functools.partial(_route_kernel, cap=cap, n_row_tiles=n_row_tiles),
        out_shape=(jax.ShapeDtypeStruct((n, lanes), I32),
                   jax.ShapeDtypeStruct((nb, N_EXPERTS, ROUTE_BLOCK), I32),
                   jax.ShapeDtypeStruct((tbl_rows, lanes), I32),
                   jax.ShapeDtypeStruct((bnd_rows, lanes), I32)),
        in_specs=[full(n, lanes)],
        out_specs=(full(n, lanes), full(nb, N_EXPERTS, ROUTE_BLOCK), full(tbl_rows, lanes),
                   full(bnd_rows, lanes)),
        compiler_params=_cparams((), 8 * n * lanes * 4),
        name="route_" + tag,
    )(aff)


def _ffn_kernel(bnd_ref, h_ref, post_ref, w1_ref, w3_ref, w2_ref, ye_ref, xs_scr, acc_scr, *,
                n_row_tiles):
    e = pl.program_id(0)
    f = pl.program_id(1)
    tr = FFN_ROW_TILE
    kc = ROUTE_BLOCK

    @pl.when(f == 0)
    def _gather():
        rid = lax.broadcasted_iota(I32, (tr, kc), 0)
        for j in range(n_row_tiles):
            lo = bnd_ref[j, e]
            hi = bnd_ref[n_row_tiles + j, e]
            rows = pl.ds(j * tr, tr)
            acc_scr[rows, :] = jnp.zeros((tr, acc_scr.shape[1]), F32)

            def chunk(c, carry):
                onehot = jnp.where(post_ref[c] == rid + j * tr, 1.0, 0.0).astype(BF16)
                start = pl.multiple_of(c * kc, kc)
                acc_scr[rows, :] += _dot(onehot, h_ref[pl.ds(start, kc), :])
                return carry

            lax.fori_loop(lo // kc, (hi + kc - 1) // kc, chunk, 0)
            xs_scr[rows, :] = acc_scr[rows, :].astype(BF16)

    xs = xs_scr[...]
    hid = _silu(_dot(xs, w1_ref[...].astype(BF16))) * _dot(xs, w3_ref[...].astype(BF16))
    part = _dot(hid.astype(BF16), w2_ref[...].astype(BF16))

    @pl.when(f == 0)
    def _first():
        acc_scr[...] = part

    @pl.when(f > 0)
    def _rest():
        acc_scr[...] += part

    @pl.when(f == pl.num_programs(1) - 1)
    def _emit():
        ye_ref[...] = acc_scr[...].astype(BF16)


def _ffn_call(bnd, h, post4, w1, w3, w2, cap, tag):
    n, d = h.shape
    n_exp, _, dff = w1.shape
    nb = n // ROUTE_BLOCK
    tf = dff // FFN_F_TILES
    grid_spec = pltpu.PrefetchScalarGridSpec(
        num_scalar_prefetch=1,
        grid=(n_exp, FFN_F_TILES),
        in_specs=[pl.BlockSpec((n, d), lambda e, f, b: (0, 0), pipeline_mode=pl.Buffered(1)),
                  pl.BlockSpec((nb, None, 1, ROUTE_BLOCK), lambda e, f, b: (0, e, 0, 0)),
                  pl.BlockSpec((None, d, tf), lambda e, f, b: (e, 0, f)),
                  pl.BlockSpec((None, d, tf), lambda e, f, b: (e, 0, f)),
                  pl.BlockSpec((None, tf, d), lambda e, f, b: (e, f, 0))],
        out_specs=pl.BlockSpec((cap, d), lambda e, f, b: (e, 0)),
        scratch_shapes=[pltpu.VMEM((cap, d), BF16), pltpu.VMEM((cap, d), F32)],
    )
    vmem = n * d * 2 + 2 * 3 * d * tf * 4 + 3 * d * tf * 2 + cap * d * (2 + 4 + 2 * 2) + 4 * cap * tf * 4
    return pl.pallas_call(
        functools.partial(_ffn_kernel, n_row_tiles=cap // FFN_ROW_TILE),
        out_shape=jax.ShapeDtypeStruct((n_exp * cap, d), BF16),
        grid_spec=grid_spec,
        compiler_params=_cparams(("arbitrary", "arbitrary"), vmem),
        name="ffn_" + tag,
    )(bnd, h, post4, w1, w3, w2)


def _combine_kernel(tbl_ref, ye_hbm, pos_ref, aff_ref, x1_ref, mod_ref, gpo_ref, y_ref, buf, sem, acc_scr,
                    *, cap, blocks_per_tile):
    i = pl.program_id(0)
    d = x1_ref.shape[1]
    win = COMBINE_WINDOW
    total_rows = ye_hbm.shape[0]
    last_start = total_rows - win

    def window_start(first_row, k):
        unclamped = (first_row // BF16_ROWS_PER_TILE) * BF16_ROWS_PER_TILE + k * win
        return unclamped, jnp.minimum(unclamped, last_start)

    def fetch(e, start):
        return pltpu.make_async_copy(ye_hbm.at[pl.ds(pl.multiple_of(start, BF16_ROWS_PER_TILE), win), :],
                                     buf.at[e], sem.at[e])

    first, end = [], []
    for e in range(N_EXPERTS):
        first.append(tbl_ref[i * blocks_per_tile, e] + e * cap)
        end.append(tbl_ref[(i + 1) * blocks_per_tile, e] + e * cap)
        fetch(e, window_start(first[e], 0)[1]).start()

    acc_scr[...] = jnp.zeros(acc_scr.shape, F32)
    lane_row = lax.broadcasted_iota(I32, (1, win), 1)
    for e in range(N_EXPERTS):
        pcol = pos_ref[:, e:e + 1]
        grow = jnp.where(pcol >= 0, pcol + e * cap, -1)
        wcol = aff_ref[:, e:e + 1]

        def accumulate(k, grow=grow, wcol=wcol, e=e):
            unclamped, start = window_start(first[e], k)
            hit = (grow == start + lane_row) & (grow >= unclamped)
            acc_scr[...] += _dot(jnp.where(hit, wcol, 0.0).astype(BF16), buf[e])

        fetch(e, window_start(first[e], 0)[1]).wait()
        accumulate(0)
        covered = window_start(first[e], 1)[0]
        extra = jnp.maximum(end[e] - covered + win - 1, 0) // win

        def more(k, carry, e=e, accumulate=accumulate):
            cp = fetch(e, window_start(first[e], k)[1])
            cp.start()
            cp.wait()
            accumulate(k)
            return carry

        lax.fori_loop(1, 1 + extra, more, 0)

    m = mod_ref[0]
    y_ref[...] = x1_ref[...] + m[:, 5 * d:6 * d] * _rms(acc_scr[...], gpo_ref[...])


def _combine_call(tbl, ye, pos, aff, x1, mod3, mod_row_of_tile, g_po, cap, tag):
    n, d = x1.shape
    tm = TOKEN_TILE
    lanes = pos.shape[1]
    grid_spec = pltpu.PrefetchScalarGridSpec(
        num_scalar_prefetch=1,
        grid=(n // tm,),
        in_specs=[pl.BlockSpec(memory_space=pl.ANY),
                  pl.BlockSpec((tm, lanes), lambda i, t: (i, 0)),
                  pl.BlockSpec((tm, lanes), lambda i, t: (i, 0)),
                  pl.BlockSpec((tm, d), lambda i, t: (i, 0)),
                  pl.BlockSpec((1, 1, 6 * d), lambda i, t: (mod_row_of_tile(i), 0, 0)),
                  pl.BlockSpec((1, d), lambda i, t: (0, 0))],
        out_specs=pl.BlockSpec((tm, d), lambda i, t: (i, 0)),
        scratch_shapes=[pltpu.VMEM((N_EXPERTS, COMBINE_WINDOW, d), BF16),
                        pltpu.SemaphoreType.DMA((N_EXPERTS,)),
                        pltpu.VMEM((tm, d), F32)],
    )
    vmem = N_EXPERTS * COMBINE_WINDOW * d * 2 + 8 * tm * d * 4
    return pl.pallas_call(
        functools.partial(_combine_kernel, cap=cap, blocks_per_tile=tm // ROUTE_BLOCK),
        out_shape=jax.ShapeDtypeStruct((n, d), F32),
        grid_spec=grid_spec,
        compiler_params=_cparams(("arbitrary",), vmem),
        name="combine_" + tag,
    )(tbl, ye, pos, aff, x1, mod3, g_po)


def _rope_tables(seq):
    rows = seq // GRID_W
    r = jnp.repeat(jnp.arange(rows), GRID_W).astype(F32)
    col = jnp.tile(jnp.arange(GRID_W), rows).astype(F32)
    pairs = HEAD_DIM // 4
    freqs = ROPE_THETA ** (-jnp.arange(pairs, dtype=F32) / pairs)
    ang = jnp.concatenate([r[:, None] * freqs, col[:, None] * freqs], axis=-1)
    cos = jnp.repeat(jnp.cos(ang), 2, axis=-1)
    sin = jnp.repeat(jnp.sin(ang), 2, axis=-1)
    even = (jnp.arange(HEAD_DIM) % 2) == 0
    return cos, jnp.where(even, -sin, 0.0), jnp.where(even, 0.0, sin)


def _trunk(x, mod3, mod_row_of_tile, rope_tabs, ctx, lw, tag):
    (g_pre_mix, g_post_mix, g_pre_ffn, g_post_ffn, w_in_p, g_q, g_k, wgf, bgf, wgb, bgb, g_gla,
     w_pa, w_pg, w_out, w_r, w1, w3, w2) = lw
    batch, seq, d = x.shape
    n = batch * seq
    x2 = x.reshape(n, d)
    (q_a, k_a, v_a, q_g, k_g, v_g, r_g, lg_f, lg_b, gates) = _inproj_call(
        x2, mod3, mod_row_of_tile, g_pre_mix, w_in_p, g_q, g_k, wgf, bgf, wgb, bgb, rope_tabs, batch, seq)
    if ctx is None:
        o_a = _attn_call(q_a, k_a, v_a, None, None)
        o_g, s_f, s_b = _gla_call(q_g, k_g, v_g, lg_f, lg_b, r_g, g_gla, None, None, batch, seq, True)
    else:
        ck, cv, s_f0, s_b0 = ctx
        o_a = _attn_call(q_a, k_a, v_a, ck, cv)
        (o_g,) = _gla_call(q_g, k_g, v_g, lg_f, lg_b, r_g, g_gla, s_f0, s_b0, batch, seq, False)
        s_f = s_b = None
    x1, h, aff = _outproj_call(o_a, o_g, gates, x2, mod3, mod_row_of_tile, w_pa, w_pg, w_out,
                               g_post_mix, g_pre_ffn, w_r, tag)
    cap = (EC_CAPACITY_FACTOR * n) // N_EXPERTS
    pos, post, tbl, bnd = _route_call(aff, cap, tag)
    post4 = post.reshape(post.shape[0], N_EXPERTS, 1, ROUTE_BLOCK)
    ye = _ffn_call(bnd, h, post4, w1, w3, w2, cap, tag)
    y = _combine_call(tbl, ye, pos, aff, x1, mod3, mod_row_of_tile, g_post_ffn, cap, tag)
    return y.reshape(batch, seq, d), (k_a, v_a, s_f, s_b)


def kernel(x_prompt, x_sample, cache_k, cache_v, state_gla_fwd, state_gla_bwd, c, c_ctx, g_pre_mix, g_post_mix, g_pre_ffn, g_post_ffn, w_mod, b_mod, w_in, g_q, g_k, w_gk2_f, b_gk_f, w_gk2_b, b_gk_b, g_gla, w_pa, w_pg, w_out, w_router, w1, w3, w2):
    depth = w_in.shape[0]
    assert depth == 1, "single trunk layer"
    d = x_prompt.shape[-1]
    dec_batch, dec_seq, _ = x_sample.shape
    assert dec_batch + 1 <= MOD_ROWS
    l = 0
    off, dinp = _inproj_layout(d)
    gk0 = off["gk"][0]
    rank = GLA_GATE_RANK
    w_in_p = jnp.concatenate(
        [w_in[l][:, :gk0], w_in[l][:, gk0:gk0 + 2 * rank],
         jnp.zeros((d, V7X_LANES - 2 * rank), F32), w_in[l][:, gk0 + 2 * rank:]], axis=1).astype(BF16)
    gkw = w_gk2_f.shape[-1]
    wgf = jnp.zeros((V7X_LANES, gkw), F32).at[0:rank].set(w_gk2_f[l]).astype(BF16)
    wgb = jnp.zeros((V7X_LANES, gkw), F32).at[rank:2 * rank].set(w_gk2_b[l]).astype(BF16)
    w_r = jnp.zeros((d, V7X_LANES), F32).at[:, :N_EXPERTS].set(w_router[l]).astype(BF16)
    row = lambda a: a[l].reshape(1, -1)
    lw = (row(g_pre_mix), row(g_post_mix), row(g_pre_ffn), row(g_post_ffn), w_in_p, row(g_q), row(g_k),
          wgf, row(b_gk_f), wgb, row(b_gk_b), row(g_gla),
          w_pa[l].astype(BF16), w_pg[l].astype(BF16), w_out[l].astype(BF16), w_r, w1[l], w3[l], w2[l])

    cc = jnp.concatenate([c_ctx[None, :], c, jnp.zeros((MOD_ROWS - 1 - dec_batch, d), F32)], axis=0)
    mod = _mod_call(cc, w_mod[l], b_mod[l].reshape(1, -1))
    mod3 = mod.reshape(MOD_ROWS, 1, 6 * d)

    yp, (nk, nv, nsf, nsb) = _trunk(x_prompt, mod3, lambda i: 0, None, None, lw, "ctx")
    tiles_per_seq = dec_seq // TOKEN_TILE
    ctx = (cache_k[:, l:l + 1], cache_v[:, l:l + 1], state_gla_fwd[:, l:l + 1], state_gla_bwd[:, l:l + 1])
    ys, _ = _trunk(x_sample, mod3, lambda i: 1 + i // tiles_per_seq, _rope_tables(dec_seq), ctx, lw, "lat")
    return (yp, ys, nk, nv, nsf, nsb)
```

```python
import functools

import jax
import jax.numpy as jnp
from jax import lax
from jax.experimental import pallas as pl
from jax.experimental.pallas import tpu as pltpu

F32 = jnp.float32
BF16 = jnp.bfloat16
I32 = jnp.int32

N_HEADS = 8
N_KV_HEADS = 2
HEAD_DIM = 128
GRID_W = 64
ROPE_THETA = 10000.0
GLA_HEADS = 4
GLA_GATE_RANK = 16
GLA_GATE_NORM = 16.0
GLA_CHUNK = 64
N_EXPERTS = 16
EC_CAPACITY_FACTOR = 2
EPS = 1e-6

V7X_LANES = 128
V7X_VMEM_BYTES = 64 * 1024 * 1024
V7X_VMEM_RESERVE_BYTES = 6 * 1024 * 1024
BF16_ROWS_PER_TILE = 16

TOKEN_TILE = 512
ATTN_Q_TILE = 256
GLA_BLOCK = 256
GLA_CTX_HEADS_PER_STEP = 4
GLA_LAT_HEADS_PER_STEP = 1
ROUTE_BLOCK = 256
FFN_ROW_TILE = 256
FFN_GATHER_TOKENS = 512
FFN_F_TILES = 4
COMBINE_TILE = 256
COMBINE_WINDOW = 128
MOD_ROWS = 8
MOD_N_TILE = 1536


def _cparams(semantics, vmem_bytes):
    limit = min(max(int(vmem_bytes), 32 * 1024 * 1024), V7X_VMEM_BYTES - V7X_VMEM_RESERVE_BYTES)
    return pltpu.CompilerParams(dimension_semantics=semantics, vmem_limit_bytes=limit)


def _sigmoid(x):
    return 1.0 / (1.0 + jnp.exp(-x))


def _silu(x):
    return x * _sigmoid(x)


def _log_sigmoid(x):
    return jnp.minimum(x, 0.0) - jnp.log1p(jnp.exp(-jnp.abs(x)))


def _rms(x, g):
    ms = jnp.mean(x * x, axis=-1, keepdims=True)
    return x * lax.rsqrt(ms + EPS) * g


def _dot(a, b):
    return jnp.dot(a, b, preferred_element_type=F32)


def _dot_nt(a, b):
    return lax.dot_general(a, b, (((1,), (1,)), ((), ())), preferred_element_type=F32)


def _mod_kernel(c_ref, w_ref, b_ref, o_ref):
    s = _silu(c_ref[...]).astype(BF16)
    o_ref[...] = _dot(s, w_ref[...].astype(BF16)) + b_ref[...]


def _mod_call(cc, w_mod, b_mod):
    d, n6 = w_mod.shape
    tn = MOD_N_TILE
    return pl.pallas_call(
        _mod_kernel,
        out_shape=jax.ShapeDtypeStruct((MOD_ROWS, n6), F32),
        grid=(n6 // tn,),
        in_specs=[pl.BlockSpec((MOD_ROWS, d), lambda j: (0, 0)),
                  pl.BlockSpec((d, tn), lambda j: (0, j)),
                  pl.BlockSpec((1, tn), lambda j: (0, j))],
        out_specs=pl.BlockSpec((MOD_ROWS, tn), lambda j: (0, j)),
        compiler_params=_cparams(("arbitrary",), 3 * d * tn * 4),
        name="mod",
    )(cc, w_mod, b_mod)


def _inproj_layout(d):
    aq, akv = N_HEADS * HEAD_DIM, N_KV_HEADS * HEAD_DIM
    gk, gv = d // 2, d
    names = ("q_a", "k_a", "v_a", "q_g", "k_g", "v_g", "r_g", "gk", "gates")
    widths = (aq, akv, akv, gk, gk, gv, gv, V7X_LANES, 2 * d)
    off, o = {}, 0
    for nme, w in zip(names, widths):
        off[nme] = (o, o + w)
        o += w
    return off, o


def _inproj_kernel(*refs, rope, d):
    if rope:
        (x_ref, mod_ref, gpre_ref, w_ref, gq_ref, gk_ref, wgf_ref, bgf_ref, wgb_ref, bgb_ref,
         cos_ref, se_ref, so_ref, *outs) = refs
    else:
        (x_ref, mod_ref, gpre_ref, w_ref, gq_ref, gk_ref, wgf_ref, bgf_ref, wgb_ref, bgb_ref,
         *outs) = refs
    qa_ref, k_ref, v_ref, qg_ref, kg_ref, vg_ref, rg_ref, lgf_ref, lgb_ref, gate_ref = outs
    off, _ = _inproj_layout(d)
    m = mod_ref[0]
    h = _rms(x_ref[...], gpre_ref[...]) * (1.0 + m[:, d:2 * d]) + m[:, 0:d]
    hb = h.astype(BF16)

    def proj(name):
        a, b = off[name]
        return _dot(hb, w_ref[:, a:b])

    def qk_norm(y, g_ref):
        y = _rms(y, g_ref[...])
        if rope:
            nxt = pltpu.roll(y, HEAD_DIM - 1, axis=1)
            prv = pltpu.roll(y, 1, axis=1)
            y = y * cos_ref[...] + nxt * se_ref[...] + prv * so_ref[...]
        return y

    q = proj("q_a")
    scale = HEAD_DIM ** -0.5
    for hd in range(N_HEADS):
        sl = slice(hd * HEAD_DIM, (hd + 1) * HEAD_DIM)
        qa_ref[:, sl] = (qk_norm(q[:, sl], gq_ref) * scale).astype(BF16)

    k = proj("k_a")
    v = proj("v_a")
    tb, _, _, ts, _ = k_ref.shape
    for kv in range(N_KV_HEADS):
        sl = slice(kv * HEAD_DIM, (kv + 1) * HEAD_DIM)
        k_ref[:, 0, kv] = qk_norm(k[:, sl], gk_ref).reshape(tb, ts, HEAD_DIM)
        v_ref[:, 0, kv] = v[:, sl].reshape(tb, ts, HEAD_DIM)

    dk = (d // 2) // GLA_HEADS
    qg_ref[...] = (proj("q_g") * (dk ** -0.5)).astype(BF16)
    kg_ref[...] = proj("k_g").astype(BF16)
    vg_ref[...] = proj("v_g").astype(BF16)
    rg_ref[...] = _silu(proj("r_g")).astype(BF16)

    gk = proj("gk").astype(BF16)
    lgf_ref[...] = _log_sigmoid(_dot(gk, wgf_ref[...]) + bgf_ref[...]) * (1.0 / GLA_GATE_NORM)
    lgb_ref[...] = _log_sigmoid(_dot(gk, wgb_ref[...]) + bgb_ref[...]) * (1.0 / GLA_GATE_NORM)

    gate_ref[...] = _sigmoid(proj("gates")).astype(BF16)


def _inproj_call(x2, mod3, mod_row_of_tile, g_pre, w_in_p, g_q, g_k, wgf, bgf, wgb, bgb, rope_tabs,
                 batch, seq):
    n, d = x2.shape
    tm = TOKEN_TILE
    _, dinp = _inproj_layout(d)
    rope = rope_tabs is not None
    gk_w = d // 2
    if seq >= tm:
        tb, ts, per = 1, tm, seq // tm
        kv_map = lambda i: (i // per, 0, 0, i % per, 0)
    else:
        tb, ts, per = tm // seq, seq, 1
        kv_map = lambda i: (i, 0, 0, 0, 0)
    row = lambda i: (i, 0)
    const = lambda i: (0, 0)
    in_specs = [
        pl.BlockSpec((tm, d), row),
        pl.BlockSpec((1, 1, 6 * d), lambda i: (mod_row_of_tile(i, tm), 0, 0)),
        pl.BlockSpec((1, d), const),
        pl.BlockSpec((d, dinp), const, pipeline_mode=pl.Buffered(1)),
        pl.BlockSpec((1, HEAD_DIM), const),
        pl.BlockSpec((1, HEAD_DIM), const),
        pl.BlockSpec((V7X_LANES, gk_w), const),
        pl.BlockSpec((1, gk_w), const),
        pl.BlockSpec((V7X_LANES, gk_w), const),
        pl.BlockSpec((1, gk_w), const),
    ]
    args = [x2, mod3, g_pre, w_in_p, g_q, g_k, wgf, bgf, wgb, bgb]
    if rope:
        tab = pl.BlockSpec((tm, HEAD_DIM), lambda i: (i % per, 0))
        in_specs += [tab, tab, tab]
        args += list(rope_tabs)
    kv_shape = jax.ShapeDtypeStruct((batch, 1, N_KV_HEADS, seq, HEAD_DIM), F32)
    kv_spec = pl.BlockSpec((tb, 1, N_KV_HEADS, ts, HEAD_DIM), kv_map)
    out_shape = (
        jax.ShapeDtypeStruct((n, N_HEADS * HEAD_DIM), BF16), kv_shape, kv_shape,
        jax.ShapeDtypeStruct((n, gk_w), BF16), jax.ShapeDtypeStruct((n, gk_w), BF16),
        jax.ShapeDtypeStruct((n, d), BF16), jax.ShapeDtypeStruct((n, d), BF16),
        jax.ShapeDtypeStruct((n, gk_w), F32), jax.ShapeDtypeStruct((n, gk_w), F32),
        jax.ShapeDtypeStruct((n, 2 * d), BF16),
    )
    out_specs = (
        pl.BlockSpec((tm, N_HEADS * HEAD_DIM), row), kv_spec, kv_spec,
        pl.BlockSpec((tm, gk_w), row), pl.BlockSpec((tm, gk_w), row),
        pl.BlockSpec((tm, d), row), pl.BlockSpec((tm, d), row),
        pl.BlockSpec((tm, gk_w), row), pl.BlockSpec((tm, gk_w), row),
        pl.BlockSpec((tm, 2 * d), row),
    )
    out_row_bytes = 2 * (N_HEADS * HEAD_DIM + 2 * gk_w + 2 * d + 2 * d) + 4 * (4 * HEAD_DIM + 2 * gk_w)
    vmem = d * dinp * 2 + 2 * tm * (d * 4 + out_row_bytes) + 6 * tm * 2 * d * 4
    return pl.pallas_call(
        functools.partial(_inproj_kernel, rope=rope, d=d),
        out_shape=out_shape,
        grid=(n // tm,),
        in_specs=in_specs,
        out_specs=out_specs,
        compiler_params=_cparams(("parallel",), vmem),
        name="inproj_lat" if rope else "inproj_ctx",
    )(*args)


def _attn_kernel(*refs, cached):
    if cached:
        q_ref, k_ref, v_ref, ck_ref, cv_ref, o_ref = refs
    else:
        q_ref, k_ref, v_ref, o_ref = refs
    tq = q_ref.shape[0]
    grp = N_HEADS // N_KV_HEADS
    for kv in range(N_KV_HEADS):
        kk = k_ref[0, 0, kv].astype(BF16)
        vv = v_ref[0, 0, kv].astype(BF16)
        if cached:
            kk = jnp.concatenate([ck_ref[0, 0, kv].astype(BF16), kk], axis=0)
            vv = jnp.concatenate([cv_ref[0, 0, kv].astype(BF16), vv], axis=0)
        heads = [q_ref[:, (kv * grp + g) * HEAD_DIM:(kv * grp + g + 1) * HEAD_DIM] for g in range(grp)]
        q4 = jnp.concatenate(heads, axis=0)
        s = _dot_nt(q4, kk)
        p = jnp.exp(s - jnp.max(s, axis=-1, keepdims=True))
        l = jnp.sum(p, axis=-1, keepdims=True)
        o = _dot(p.astype(BF16), vv) / l
        for g in range(grp):
            hd = kv * grp + g
            o_ref[:, hd * HEAD_DIM:(hd + 1) * HEAD_DIM] = o[g * tq:(g + 1) * tq].astype(BF16)


def _attn_call(q_a, k_a, v_a, cache_k, cache_v):
    batch, _, _, seq, _ = k_a.shape
    n, aq = q_a.shape
    tq = ATTN_Q_TILE
    per = seq // tq
    cached = cache_k is not None
    own = pl.BlockSpec((1, 1, N_KV_HEADS, seq, HEAD_DIM), lambda b, j: (b, 0, 0, 0, 0))
    in_specs = [pl.BlockSpec((tq, aq), lambda b, j: (b * per + j, 0)), own, own]
    args = [q_a, k_a, v_a]
    klen = seq
    if cached:
        past = cache_k.shape[3]
        cspec = pl.BlockSpec((1, 1, N_KV_HEADS, past, HEAD_DIM), lambda b, j: (b, 0, 0, 0, 0))
        in_specs += [cspec, cspec]
        args += [cache_k, cache_v]
        klen += past
    grp = N_HEADS // N_KV_HEADS
    vmem = 4 * grp * tq * klen * 4 + 8 * N_KV_HEADS * klen * HEAD_DIM * 4 + 8 * tq * aq * 2
    return pl.pallas_call(
        functools.partial(_attn_kernel, cached=cached),
        out_shape=jax.ShapeDtypeStruct((n, aq), BF16),
        grid=(batch, per),
        in_specs=in_specs,
        out_specs=pl.BlockSpec((tq, aq), lambda b, j: (b * per + j, 0)),
        compiler_params=_cparams(("parallel", "parallel"), vmem),
        name="attn_lat" if cached else "attn_ctx",
    )(*args)


def _gla_kernel(*refs, nblk, heads, has_state, emit_state):
    refs = list(refs)
    q_ref, k_ref, v_ref, lgf_ref, lgb_ref, rg_ref, gg_ref = refs[:7]
    pos = 7
    if has_state:
        s0f_ref, s0b_ref = refs[pos:pos + 2]
        pos += 2
    og_ref = refs[pos]
    pos += 1
    if emit_state:
        sf_ref, sb_ref = refs[pos:pos + 2]
        pos += 2
    of_scr, ob_scr = refs[pos:pos + 2]

    blk = GLA_BLOCK
    ch = GLA_CHUNK
    nch = blk // ch
    dk = q_ref.shape[1] // heads
    dv = v_ref.shape[1] // heads
    shift = ch.bit_length() - 1
    row_in_chunk = lax.broadcasted_iota(I32, (blk, dk), 0) & (ch - 1)
    ri = lax.broadcasted_iota(I32, (blk, blk), 0)
    ci = lax.broadcasted_iota(I32, (blk, blk), 1)
    same = (ri >> shift) == (ci >> shift)
    col_chunk = lax.broadcasted_iota(I32, (dk, blk), 1) >> shift

    def one_block(b0, hd, state, reverse):
        rows = pl.ds(b0, blk)
        kcols = slice(hd * dk, (hd + 1) * dk)
        q = q_ref[rows, kcols].astype(F32)
        k = k_ref[rows, kcols].astype(F32)
        v = v_ref[rows, hd * dv:(hd + 1) * dv]
        b = (lgb_ref if reverse else lgf_ref)[rows, kcols]
        s = 1
        while s < ch:
            if reverse:
                sh = pltpu.roll(b, blk - s, axis=0)
                b = b + jnp.where(row_in_chunk < ch - s, sh, 0.0)
            else:
                sh = pltpu.roll(b, s, axis=0)
                b = b + jnp.where(row_in_chunk >= s, sh, 0.0)
            s *= 2
        qe = (q * jnp.exp(b)).astype(BF16)
        ke = (k * jnp.exp(-b)).astype(BF16)
        tri = (ci >= ri) if reverse else (ci <= ri)
        a = jnp.where(same & tri, _dot_nt(qe, ke), 0.0)
        o = _dot(a.astype(BF16), v)

        kt = k.T
        bt = b.T
        order = range(nch - 1, -1, -1) if reverse else range(nch)
        end_col = [c * ch + (0 if reverse else ch - 1) for c in range(nch)]
        ends = [bt[:, e:e + 1] for e in end_col]
        b_end = ends[nch - 1]
        for c in range(nch - 2, -1, -1):
            b_end = jnp.where(col_chunk == c, ends[c], b_end)
        kdt = kt * jnp.exp(b_end - bt)
        inter = [None] * nch
        for c in order:
            inter[c] = _dot(qe[c * ch:(c + 1) * ch], state.astype(BF16))
            kv_c = _dot(jnp.where(col_chunk == c, kdt, 0.0).astype(BF16), v)
            state = jnp.exp(ends[c]) * state + kv_c
        return o + jnp.concatenate(inter, axis=0), state

    for hd in range(heads):
        vcols = slice(hd * dv, (hd + 1) * dv)
        sf = s0f_ref[0, 0, hd] if has_state else jnp.zeros((dk, dv), F32)
        sb = s0b_ref[0, 0, hd] if has_state else jnp.zeros((dk, dv), F32)
        for i in range(nblk):
            j = nblk - 1 - i
            o_f, sf = one_block(i * blk, hd, sf, False)
            o_b, sb = one_block(j * blk, hd, sb, True)
            of_scr[pl.ds(i * blk, blk), vcols] = o_f
            ob_scr[pl.ds(j * blk, blk), vcols] = o_b
        if emit_state:
            sf_ref[0, 0, hd] = sf
            sb_ref[0, 0, hd] = sb
        o = of_scr[:, vcols] + ob_scr[:, vcols]
        og_ref[:, vcols] = (_rms(o, gg_ref[...]) * rg_ref[:, vcols].astype(F32)).astype(BF16)


def _gla_call(q_g, k_g, v_g, lg_f, lg_b, r_g, g_gla, state_f, state_b, batch, seq, emit_state, heads):
    n, gkw = q_g.shape
    d = v_g.shape[1]
    dk, dv = gkw // GLA_HEADS, d // GLA_HEADS
    has_state = state_f is not None
    kspec = pl.BlockSpec((seq, heads * dk), lambda b, h: (b, h))
    vspec = pl.BlockSpec((seq, heads * dv), lambda b, h: (b, h))
    sspec = pl.BlockSpec((1, 1, heads, dk, dv), lambda b, h: (b, 0, h, 0, 0))
    in_specs = [kspec, kspec, vspec, kspec, kspec, vspec, pl.BlockSpec((1, dv), lambda b, h: (0, 0))]
    args = [q_g, k_g, v_g, lg_f, lg_b, r_g, g_gla]
    if has_state:
        in_specs += [sspec, sspec]
        args += [state_f, state_b]
    out_shape = [jax.ShapeDtypeStruct((n, d), BF16)]
    out_specs = [vspec]
    if emit_state:
        st = jax.ShapeDtypeStruct((batch, 1, GLA_HEADS, dk, dv), F32)
        out_shape += [st, st]
        out_specs += [sspec, sspec]
    vmem = 12 * seq * heads * dv * 4 + 40 * GLA_BLOCK * GLA_BLOCK * 4
    return pl.pallas_call(
        functools.partial(_gla_kernel, nblk=seq // GLA_BLOCK, heads=heads, has_state=has_state,
                          emit_state=emit_state),
        out_shape=tuple(out_shape),
        grid=(batch, GLA_HEADS // heads),
        in_specs=in_specs,
        out_specs=tuple(out_specs),
        scratch_shapes=[pltpu.VMEM((seq, heads * dv), F32), pltpu.VMEM((seq, heads * dv), F32)],
        compiler_params=_cparams(("parallel", "parallel"), vmem),
        name="gla_ctx" if emit_state else "gla_lat",
    )(*args)


def _outproj_kernel(oa_ref, og_ref, gate_ref, x_ref, mod_ref, wpa_ref, wpg_ref, wout_ref, gpm_ref,
                    gpf_ref, wr_ref, x1_ref, h_ref, aff_ref):
    d = x_ref.shape[1]
    m = mod_ref[0]
    oa = _dot(oa_ref[...], wpa_ref[...])
    og = _dot(og_ref[...], wpg_ref[...])
    mix = gate_ref[:, 0:d].astype(F32) * oa + gate_ref[:, d:2 * d].astype(F32) * og
    mo = _dot(mix.astype(BF16), wout_ref[...])
    x1 = x_ref[...] + m[:, 2 * d:3 * d] * _rms(mo, gpm_ref[...])
    x1_ref[...] = x1
    hb = (_rms(x1, gpf_ref[...]) * (1.0 + m[:, 4 * d:5 * d]) + m[:, 3 * d:4 * d]).astype(BF16)
    h_ref[...] = hb
    logits = _dot(hb, wr_ref[...])
    valid = lax.broadcasted_iota(I32, logits.shape, 1) < N_EXPERTS
    mx = jnp.max(jnp.where(valid, logits, -jnp.inf), axis=-1, keepdims=True)
    ex = jnp.where(valid, jnp.exp(logits - mx), 0.0)
    aff_ref[...] = ex / jnp.sum(ex, axis=-1, keepdims=True)


def _outproj_call(o_a, o_g, gates, x2, mod3, mod_row_of_tile, w_pa, w_pg, w_out, g_pm, g_pf, w_r, tag):
    n, d = x2.shape
    tm = TOKEN_TILE
    row = lambda i: (i, 0)
    const = lambda i: (0, 0)
    wspec = pl.BlockSpec((d, d), const)
    vmem = 2 * 3 * d * d * 2 + 2 * tm * d * (2 + 2 + 4 + 4 + 4 + 2) + 8 * tm * d * 4
    return pl.pallas_call(
        _outproj_kernel,
        out_shape=(jax.ShapeDtypeStruct((n, d), F32), jax.ShapeDtypeStruct((n, d), BF16),
                   jax.ShapeDtypeStruct((n, V7X_LANES), F32)),
        grid=(n // tm,),
        in_specs=[pl.BlockSpec((tm, d), row), pl.BlockSpec((tm, d), row), pl.BlockSpec((tm, 2 * d), row),
                  pl.BlockSpec((tm, d), row),
                  pl.BlockSpec((1, 1, 6 * d), lambda i: (mod_row_of_tile(i, tm), 0, 0)),
                  wspec, wspec, wspec, pl.BlockSpec((1, d), const), pl.BlockSpec((1, d), const),
                  pl.BlockSpec((d, V7X_LANES), const)],
        out_specs=(pl.BlockSpec((tm, d), row), pl.BlockSpec((tm, d), row),
                   pl.BlockSpec((tm, V7X_LANES), row)),
        compiler_params=_cparams(("parallel",), vmem),
        name="outproj_" + tag,
    )(o_a, o_g, gates, x2, mod3, w_pa, w_pg, w_out, g_pm, g_pf, w_r)


def _route_kernel(aff_ref, pos_ref, post_ref, tbl_ref, bnd_ref, *, cap, n_row_tiles):
    n = aff_ref.shape[0]
    rb = ROUTE_BLOCK
    nb = n // rb
    lanes = aff_ref.shape[1]
    lane = lax.broadcasted_iota(I32, (1, lanes), 1)
    expert_lane = lane < N_EXPERTS
    tbl_ref[...] = jnp.zeros(tbl_ref.shape, I32)
    bnd_ref[...] = jnp.zeros(bnd_ref.shape, I32)

    def bits_of(c):
        start = pl.multiple_of(c * rb, rb)
        return lax.bitcast_convert_type(aff_ref[pl.ds(start, rb), :], I32)

    def count_ge(t):
        def body(c, acc):
            return acc + jnp.sum((bits_of(c) >= t).astype(I32), axis=0, keepdims=True)
        return lax.fori_loop(0, nb, body, jnp.zeros((1, lanes), I32))

    def bit_step(i, lo):
        t = lo | jnp.left_shift(jnp.int32(1), 30 - i)
        return jnp.where(count_ge(t) >= cap, t, lo)

    thr = lax.fori_loop(0, 31, bit_step, jnp.zeros((1, lanes), I32))
    need = (cap - count_ge(thr + 1)).astype(F32)

    r = lax.broadcasted_iota(I32, (rb, rb), 0)
    c_ = lax.broadcasted_iota(I32, (rb, rb), 1)
    tril = jnp.where(c_ <= r, 1.0, 0.0).astype(BF16)

    def blk_step(c, carry):
        eq_before, sel_before, lo_cnt, hi_cnt = carry
        bits = bits_of(c)
        gt = bits > thr
        eq = bits == thr
        eq_incl = _dot(tril, jnp.where(eq, 1.0, 0.0).astype(BF16)) + eq_before
        sel = (gt | (eq & (eq_incl <= need))) & expert_lane
        self_ = jnp.where(sel, 1.0, 0.0)
        incl = _dot(tril, self_.astype(BF16)) + sel_before
        excl = incl - self_
        posb = jnp.where(sel, excl, -1.0).astype(I32)
        start = pl.multiple_of(c * rb, rb)
        pos_ref[pl.ds(start, rb), :] = posb
        post_ref[c] = posb.T[0:N_EXPERTS, :]
        tbl_ref[pl.ds(c, 1), :] = sel_before.astype(I32)
        lo_new, hi_new = [], []
        for j in range(n_row_tiles):
            lo_new.append(lo_cnt[j] + jnp.sum(jnp.where(incl <= j * FFN_ROW_TILE, 1.0, 0.0), axis=0, keepdims=True))
            hi_new.append(hi_cnt[j] + jnp.sum(jnp.where(excl < (j + 1) * FFN_ROW_TILE, 1.0, 0.0), axis=0, keepdims=True))
        return (eq_incl[rb - 1:rb, :], incl[rb - 1:rb, :], tuple(lo_new), tuple(hi_new))

    zero = jnp.zeros((1, lanes), F32)
    zeros_j = tuple(zero for _ in range(n_row_tiles))
    _, total, lo_cnt, hi_cnt = lax.fori_loop(0, nb, blk_step, (zero, zero, zeros_j, zeros_j))
    tbl_ref[pl.ds(nb, 1), :] = total.astype(I32)
    for j in range(n_row_tiles):
        bnd_ref[pl.ds(j, 1), :] = lo_cnt[j].astype(I32)
        bnd_ref[pl.ds(n_row_tiles + j, 1), :] = hi_cnt[j].astype(I32)


def _route_call(aff, cap, tag):
    n, lanes = aff.shape
    nb = n // ROUTE_BLOCK
    n_row_tiles = cap // FFN_ROW_TILE
    tbl_rows = -(-(nb + 1) // 8) * 8
    bnd_rows = -(-(2 * n_row_tiles) // 8) * 8
    full = lambda *shape: pl.BlockSpec(shape, lambda: tuple(0 for _ in shape))
    return pl.pallas_call(
        functools.partial(_route_kernel, cap=cap, n_row_tiles=n_row_tiles),
        out_shape=(jax.ShapeDtypeStruct((n, lanes), I32),
                   jax.ShapeDtypeStruct((nb, N_EXPERTS, ROUTE_BLOCK), I32),
                   jax.ShapeDtypeStruct((tbl_rows, lanes), I32),
                   jax.ShapeDtypeStruct((bnd_rows, lanes), I32)),
        in_specs=[full(n, lanes)],
        out_specs=(full(n, lanes), full(nb, N_EXPERTS, ROUTE_BLOCK), full(tbl_rows, lanes),
                   full(bnd_rows, lanes)),
        compiler_params=_cparams((), 8 * n * lanes * 4),
        name="route_" + tag,
    )(aff)


def _ffn_kernel(*refs, caps):
    ng = len(caps)
    bnd_refs = refs[:ng]
    h_refs = refs[ng:3 * ng:2]
    post_refs = refs[ng + 1:3 * ng:2]
    w1_ref, w3_ref, w2_ref = refs[3 * ng:3 * ng + 3]
    ye_refs = refs[3 * ng + 3:4 * ng + 3]
    xs_scr, acc_scr = refs[4 * ng + 3:]
    row_off = [sum(caps[:g]) for g in range(ng)]
    e = pl.program_id(0)
    f = pl.program_id(1)
    tr = FFN_ROW_TILE
    kc = FFN_GATHER_TOKENS
    per = kc // ROUTE_BLOCK

    @pl.when(f == 0)
    def _gather():
        rid = lax.broadcasted_iota(I32, (tr, kc), 0)
        for g in range(ng):
            n_row_tiles = caps[g] // tr
            for j in range(n_row_tiles):
                lo = bnd_refs[g][j, e]
                hi = bnd_refs[g][n_row_tiles + j, e]
                rows = pl.ds(row_off[g] + j * tr, tr)
                acc_scr[rows, :] = jnp.zeros((tr, acc_scr.shape[1]), F32)

                def chunk(c, carry, g=g, j=j, rows=rows):
                    p = jnp.concatenate([post_refs[g][per * c + s] for s in range(per)], axis=1)
                    onehot = jnp.where(p == rid + j * tr, 1.0, 0.0).astype(BF16)
                    start = pl.multiple_of(c * kc, kc)
                    acc_scr[rows, :] += _dot(onehot, h_refs[g][pl.ds(start, kc), :])
                    return carry

                lax.fori_loop(lo // kc, (hi + kc - 1) // kc, chunk, 0)
                xs_scr[rows, :] = acc_scr[rows, :].astype(BF16)

    xs = xs_scr[...]
    hid = _silu(_dot(xs, w1_ref[...].astype(BF16))) * _dot(xs, w3_ref[...].astype(BF16))
    part = _dot(hid.astype(BF16), w2_ref[...].astype(BF16))

    @pl.when(f == 0)
    def _first():
        acc_scr[...] = part

    @pl.when(f > 0)
    def _rest():
        acc_scr[...] += part

    @pl.when(f == pl.num_programs(1) - 1)
    def _emit():
        for g in range(ng):
            ye_refs[g][...] = acc_scr[pl.ds(row_off[g], caps[g]), :].astype(BF16)


def _ffn_call(groups, w1, w3, w2):
    ng = len(groups)
    caps = tuple(g[3] for g in groups)
    d = groups[0][1].shape[1]
    n_exp, _, dff = w1.shape
    tf = dff // FFN_F_TILES
    rows = sum(caps)
    idx = lambda fn: (lambda e, f, *_: fn(e, f))
    in_specs, args, h_bytes = [], [], 0
    for bnd, h, post4, cap in groups:
        n = h.shape[0]
        in_specs += [pl.BlockSpec((n, d), idx(lambda e, f: (0, 0)), pipeline_mode=pl.Buffered(1)),
                     pl.BlockSpec((n // ROUTE_BLOCK, None, 1, ROUTE_BLOCK), idx(lambda e, f: (0, e, 0, 0)))]
        args += [h, post4]
        h_bytes += n * d * 2
    in_specs += [pl.BlockSpec((None, d, tf), idx(lambda e, f: (e, 0, f))),
                 pl.BlockSpec((None, d, tf), idx(lambda e, f: (e, 0, f))),
                 pl.BlockSpec((None, tf, d), idx(lambda e, f: (e, f, 0)))]
    grid_spec = pltpu.PrefetchScalarGridSpec(
        num_scalar_prefetch=ng,
        grid=(n_exp, FFN_F_TILES),
        in_specs=in_specs,
        out_specs=tuple(pl.BlockSpec((cap, d), idx(lambda e, f: (e, 0))) for cap in caps),
        scratch_shapes=[pltpu.VMEM((rows, d), BF16), pltpu.VMEM((rows, d), F32)],
    )
    vmem = h_bytes + 3 * d * tf * (2 * 4 + 2) + rows * d * (2 + 4 + 2 * 2 + 4) + 4 * rows * tf * 4
    return pl.pallas_call(
        functools.partial(_ffn_kernel, caps=caps),
        out_shape=tuple(jax.ShapeDtypeStruct((n_exp * cap, d), BF16) for cap in caps),
        grid_spec=grid_spec,
        compiler_params=_cparams(("arbitrary", "arbitrary"), vmem),
        name="ffn",
    )(*[g[0] for g in groups], *args, w1, w3, w2)


def _combine_kernel(tbl_ref, ye_hbm, pos_ref, aff_ref, x1_ref, mod_ref, gpo_ref, y_ref, buf, sem, xbuf, xsem,
                    acc_scr, *, cap):
    i = pl.program_id(0)
    nsteps = pl.num_programs(0)
    d = x1_ref.shape[1]
    lanes = pos_ref.shape[1]
    win = COMBINE_WINDOW
    last_start = ye_hbm.shape[0] - win
    slot = i % 2

    def first_row(step, e):
        return tbl_ref[step, e] + e * cap

    def window_start(first, k):
        unclamped = (first // BF16_ROWS_PER_TILE) * BF16_ROWS_PER_TILE + k * win
        return unclamped, jnp.minimum(unclamped, last_start)

    def fetch(step, to_slot, e):
        start = window_start(first_row(step, e), 0)[1]
        return pltpu.make_async_copy(ye_hbm.at[pl.ds(pl.multiple_of(start, BF16_ROWS_PER_TILE), win), :],
                                     buf.at[to_slot, pl.ds(e * win, win), :], sem.at[to_slot, e])

    @pl.when(i == 0)
    def _prime():
        for e in range(N_EXPERTS):
            fetch(0, 0, e).start()

    @pl.when(i + 1 < nsteps)
    def _ahead():
        for e in range(N_EXPERTS):
            fetch(i + 1, 1 - slot, e).start()

    lane_row = lax.broadcasted_iota(I32, (1, win), 1)
    pieces = []
    for e in range(N_EXPERTS):
        pcol = pos_ref[:, e:e + 1]
        grow = jnp.where(pcol >= 0, pcol + e * cap, -1)
        start = window_start(first_row(i, e), 0)[1]
        pieces.append(jnp.where(grow == start + lane_row, aff_ref[:, e:e + 1], 0.0).astype(BF16))
    onehot = jnp.concatenate(pieces, axis=1)
    for e in range(N_EXPERTS):
        fetch(i, slot, e).wait()
    acc_scr[...] = _dot(onehot, buf[slot])

    def expert_extra(e, carry):
        first = first_row(i, e)
        end = first_row(i + 1, e)
        covered = window_start(first, 1)[0]
        extra = jnp.maximum(end - covered + win - 1, 0) // win

        def more(k, c):
            unclamped, start = window_start(first, k)
            cp = pltpu.make_async_copy(ye_hbm.at[pl.ds(pl.multiple_of(start, BF16_ROWS_PER_TILE), win), :],
                                       xbuf, xsem)
            cp.start()
            cp.wait()
            at_e = lax.broadcasted_iota(I32, (1, lanes), 1) == e
            pcol = jnp.sum(jnp.where(at_e, pos_ref[...].astype(F32), 0.0), axis=1, keepdims=True).astype(I32)
            wcol = jnp.sum(jnp.where(at_e, aff_ref[...], 0.0), axis=1, keepdims=True)
            grow = jnp.where(pcol >= 0, pcol + e * cap, -1)
            hit = (grow == start + lane_row) & (grow >= unclamped)
            acc_scr[...] += _dot(jnp.where(hit, wcol, 0.0).astype(BF16), xbuf[...])
            return c

        lax.fori_loop(1, 1 + extra, more, 0)
        return carry

    lax.fori_loop(0, N_EXPERTS, expert_extra, 0)

    m = mod_ref[0]
    y_ref[...] = x1_ref[...] + m[:, 5 * d:6 * d] * _rms(acc_scr[...], gpo_ref[...])


def _combine_call(tbl, ye, pos, aff, x1, mod3, mod_row_of_tile, g_po, cap, tag):
    n, d = x1.shape
    tm = COMBINE_TILE
    lanes = pos.shape[1]
    grid_spec = pltpu.PrefetchScalarGridSpec(
        num_scalar_prefetch=1,
        grid=(n // tm,),
        in_specs=[pl.BlockSpec(memory_space=pl.ANY),
                  pl.BlockSpec((tm, lanes), lambda i, t: (i, 0)),
                  pl.BlockSpec((tm, lanes), lambda i, t: (i, 0)),
                  pl.BlockSpec((tm, d), lambda i, t: (i, 0)),
                  pl.BlockSpec((1, 1, 6 * d), lambda i, t: (mod_row_of_tile(i, tm), 0, 0)),
                  pl.BlockSpec((1, d), lambda i, t: (0, 0))],
        out_specs=pl.BlockSpec((tm, d), lambda i, t: (i, 0)),
        scratch_shapes=[pltpu.VMEM((2, N_EXPERTS * COMBINE_WINDOW, d), BF16),
                        pltpu.SemaphoreType.DMA((2, N_EXPERTS)),
                        pltpu.VMEM((COMBINE_WINDOW, d), BF16),
                        pltpu.SemaphoreType.DMA(()),
                        pltpu.VMEM((tm, d), F32)],
    )
    vmem = 2 * N_EXPERTS * COMBINE_WINDOW * d * 2 + 10 * tm * d * 4 + 2 * tm * N_EXPERTS * COMBINE_WINDOW * 4
    return pl.pallas_call(
        functools.partial(_combine_kernel, cap=cap),
        out_shape=jax.ShapeDtypeStruct((n, d), F32),
        grid_spec=grid_spec,
        compiler_params=_cparams(("arbitrary",), vmem),
        name="combine_" + tag,
    )(tbl, ye, pos, aff, x1, mod3, g_po)


def _rope_tables(seq):
    rows = seq // GRID_W
    r = jnp.repeat(jnp.arange(rows), GRID_W).astype(F32)
    col = jnp.tile(jnp.arange(GRID_W), rows).astype(F32)
    pairs = HEAD_DIM // 4
    freqs = ROPE_THETA ** (-jnp.arange(pairs, dtype=F32) / pairs)
    ang = jnp.concatenate([r[:, None] * freqs, col[:, None] * freqs], axis=-1)
    cos = jnp.repeat(jnp.cos(ang), 2, axis=-1)
    sin = jnp.repeat(jnp.sin(ang), 2, axis=-1)
    even = (jnp.arange(HEAD_DIM) % 2) == 0
    return cos, jnp.where(even, -sin, 0.0), jnp.where(even, 0.0, sin)


def _trunk_to_routing(x, mod3, mod_row_of_tile, rope_tabs, ctx, lw, tag):
    (g_pre_mix, g_post_mix, g_pre_ffn, g_post_ffn, w_in_p, g_q, g_k, wgf, bgf, wgb, bgb, g_gla,
     w_pa, w_pg, w_out, w_r, w1, w3, w2) = lw
    batch, seq, d = x.shape
    n = batch * seq
    x2 = x.reshape(n, d)
    (q_a, k_a, v_a, q_g, k_g, v_g, r_g, lg_f, lg_b, gates) = _inproj_call(
        x2, mod3, mod_row_of_tile, g_pre_mix, w_in_p, g_q, g_k, wgf, bgf, wgb, bgb, rope_tabs, batch, seq)
    if ctx is None:
        o_a = _attn_call(q_a, k_a, v_a, None, None)
        o_g, s_f, s_b = _gla_call(q_g, k_g, v_g, lg_f, lg_b, r_g, g_gla, None, None, batch, seq, True,
                                  GLA_CTX_HEADS_PER_STEP)
    else:
        ck, cv, s_f0, s_b0 = ctx
        o_a = _attn_call(q_a, k_a, v_a, ck, cv)
        (o_g,) = _gla_call(q_g, k_g, v_g, lg_f, lg_b, r_g, g_gla, s_f0, s_b0, batch, seq, False,
                           GLA_LAT_HEADS_PER_STEP)
        s_f = s_b = None
    x1, h, aff = _outproj_call(o_a, o_g, gates, x2, mod3, mod_row_of_tile, w_pa, w_pg, w_out,
                               g_post_mix, g_pre_ffn, w_r, tag)
    cap = (EC_CAPACITY_FACTOR * n) // N_EXPERTS
    pos, post, tbl, bnd = _route_call(aff, cap, tag)
    post4 = post.reshape(post.shape[0], N_EXPERTS, 1, ROUTE_BLOCK)
    return dict(x1=x1, h=h, aff=aff, pos=pos, post4=post4, tbl=tbl, bnd=bnd, cap=cap,
                mod_row_of_tile=mod_row_of_tile, tag=tag, shape=(batch, seq, d)), (k_a, v_a, s_f, s_b)


def _expert_ffn(groups, mod3, g_post_ffn, w1, w3, w2):
    yes = _ffn_call([(g["bnd"], g["h"], g["post4"], g["cap"]) for g in groups], w1, w3, w2)
    outs = []
    for g, ye in zip(groups, yes):
        y = _combine_call(g["tbl"], ye, g["pos"], g["aff"], g["x1"], mod3, g["mod_row_of_tile"], g_post_ffn,
                          g["cap"], g["tag"])
        outs.append(y.reshape(g["shape"]))
    return outs


def kernel(x_prompt, x_sample, cache_k, cache_v, state_gla_fwd, state_gla_bwd, c, c_ctx, g_pre_mix, g_post_mix, g_pre_ffn, g_post_ffn, w_mod, b_mod, w_in, g_q, g_k, w_gk2_f, b_gk_f, w_gk2_b, b_gk_b, g_gla, w_pa, w_pg, w_out, w_router, w1, w3, w2):
    depth = w_in.shape[0]
    assert depth == 1, "single trunk layer"
    d = x_prompt.shape[-1]
    dec_batch, dec_seq, _ = x_sample.shape
    assert dec_batch + 1 <= MOD_ROWS
    l = 0
    off, dinp = _inproj_layout(d)
    gk0 = off["gk"][0]
    rank = GLA_GATE_RANK
    w_in_p = jnp.concatenate(
        [w_in[l][:, :gk0], w_in[l][:, gk0:gk0 + 2 * rank],
         jnp.zeros((d, V7X_LANES - 2 * rank), F32), w_in[l][:, gk0 + 2 * rank:]], axis=1).astype(BF16)
    gkw = w_gk2_f.shape[-1]
    wgf = jnp.zeros((V7X_LANES, gkw), F32).at[0:rank].set(w_gk2_f[l]).astype(BF16)
    wgb = jnp.zeros((V7X_LANES, gkw), F32).at[rank:2 * rank].set(w_gk2_b[l]).astype(BF16)
    w_r = jnp.zeros((d, V7X_LANES), F32).at[:, :N_EXPERTS].set(w_router[l]).astype(BF16)
    row = lambda a: a[l].reshape(1, -1)
    lw = (row(g_pre_mix), row(g_post_mix), row(g_pre_ffn), row(g_post_ffn), w_in_p, row(g_q), row(g_k),
          wgf, row(b_gk_f), wgb, row(b_gk_b), row(g_gla),
          w_pa[l].astype(BF16), w_pg[l].astype(BF16), w_out[l].astype(BF16), w_r, w1[l], w3[l], w2[l])

    cc = jnp.concatenate([c_ctx[None, :], c, jnp.zeros((MOD_ROWS - 1 - dec_batch, d), F32)], axis=0)
    mod = _mod_call(cc, w_mod[l], b_mod[l].reshape(1, -1))
    mod3 = mod.reshape(MOD_ROWS, 1, 6 * d)

    gp, (nk, nv, nsf, nsb) = _trunk_to_routing(x_prompt, mod3, lambda i, tm: 0, None, None, lw, "ctx")
    ctx = (cache_k[:, l:l + 1], cache_v[:, l:l + 1], state_gla_fwd[:, l:l + 1], state_gla_bwd[:, l:l + 1])
    gs, _ = _trunk_to_routing(x_sample, mod3, lambda i, tm: 1 + (i * tm) // dec_seq, _rope_tables(dec_seq),
                              ctx, lw, "lat")
    yp, ys = _expert_ffn([gp, gs], mod3, lw[3], lw[16], lw[17], lw[18])
    return (yp, ys, nk, nv, nsf, nsb)
```

```python
import functools

import jax
import jax.numpy as jnp
from jax import lax
from jax.experimental import pallas as pl
from jax.experimental.pallas import tpu as pltpu

F32 = jnp.float32
BF16 = jnp.bfloat16
I32 = jnp.int32

N_HEADS = 8
N_KV_HEADS = 2
HEAD_DIM = 128
GRID_W = 64
ROPE_THETA = 10000.0
GLA_HEADS = 4
GLA_GATE_RANK = 16
GLA_GATE_NORM = 16.0
GLA_CHUNK = 64
N_EXPERTS = 16
EC_CAPACITY_FACTOR = 2
EPS = 1e-6

V7X_LANES = 128
V7X_VMEM_BYTES = 64 * 1024 * 1024
V7X_VMEM_RESERVE_BYTES = 6 * 1024 * 1024
BF16_ROWS_PER_TILE = 16

TOKEN_TILE = 512
ATTN_Q_TILE = 256
GLA_BLOCK = 256
GLA_CTX_HEADS_PER_STEP = 4
GLA_LAT_HEADS_PER_STEP = 1
ROUTE_BLOCK = 256
FFN_ROW_TILE = 256
FFN_GATHER_TOKENS = 1024
FFN_F_TILES = 4
COMBINE_TILE = 256
COMBINE_WINDOW = 128
MOD_ROWS = 8
MOD_N_TILE = 1536


def _cparams(semantics, vmem_bytes):
    limit = min(max(int(vmem_bytes), 32 * 1024 * 1024), V7X_VMEM_BYTES - V7X_VMEM_RESERVE_BYTES)
    return pltpu.CompilerParams(dimension_semantics=semantics, vmem_limit_bytes=limit)


def _sigmoid(x):
    return 1.0 / (1.0 + jnp.exp(-x))


def _silu(x):
    return x * _sigmoid(x)


def _log_sigmoid(x):
    return jnp.minimum(x, 0.0) - jnp.log1p(jnp.exp(-jnp.abs(x)))


def _rms(x, g):
    ms = jnp.mean(x * x, axis=-1, keepdims=True)
    return x * lax.rsqrt(ms + EPS) * g


def _dot(a, b):
    return jnp.dot(a, b, preferred_element_type=F32)


def _dot_nt(a, b):
    return lax.dot_general(a, b, (((1,), (1,)), ((), ())), preferred_element_type=F32)


def _mod_kernel(c_ref, w_ref, b_ref, o_ref):
    s = _silu(c_ref[...]).astype(BF16)
    o_ref[...] = _dot(s, w_ref[...].astype(BF16)) + b_ref[...]


def _mod_call(cc, w_mod, b_mod):
    d, n6 = w_mod.shape
    tn = MOD_N_TILE
    return pl.pallas_call(
        _mod_kernel,
        out_shape=jax.ShapeDtypeStruct((MOD_ROWS, n6), F32),
        grid=(n6 // tn,),
        in_specs=[pl.BlockSpec((MOD_ROWS, d), lambda j: (0, 0)),
                  pl.BlockSpec((d, tn), lambda j: (0, j)),
                  pl.BlockSpec((1, tn), lambda j: (0, j))],
        out_specs=pl.BlockSpec((MOD_ROWS, tn), lambda j: (0, j)),
        compiler_params=_cparams(("arbitrary",), 3 * d * tn * 4),
        name="mod",
    )(cc, w_mod, b_mod)


def _inproj_layout(d):
    aq, akv = N_HEADS * HEAD_DIM, N_KV_HEADS * HEAD_DIM
    gk, gv = d // 2, d
    names = ("q_a", "k_a", "v_a", "q_g", "k_g", "v_g", "r_g", "gk_f", "gk_b", "gates")
    widths = (aq, akv, akv, gk, gk, gv, gv, GLA_GATE_RANK, GLA_GATE_RANK, 2 * d)
    off, o = {}, 0
    for nme, w in zip(names, widths):
        off[nme] = (o, o + w)
        o += w
    off["gk"] = (off["gk_f"][0], off["gk_f"][0] + V7X_LANES)
    return off, o


def _inproj_kernel(*refs, rope, d):
    if rope:
        (x_ref, mod_ref, gpre_ref, w_ref, gq_ref, gk_ref, wgf_ref, bgf_ref, wgb_ref, bgb_ref,
         cos_ref, se_ref, so_ref, *outs) = refs
    else:
        (x_ref, mod_ref, gpre_ref, w_ref, gq_ref, gk_ref, wgf_ref, bgf_ref, wgb_ref, bgb_ref,
         *outs) = refs
    qa_ref, k_ref, v_ref, qg_ref, kg_ref, vg_ref, rg_ref, lgf_ref, lgb_ref, gate_ref = outs
    off, _ = _inproj_layout(d)
    m = mod_ref[0]
    h = _rms(x_ref[...], gpre_ref[...]) * (1.0 + m[:, d:2 * d]) + m[:, 0:d]
    hb = h.astype(BF16)

    def proj(name):
        a, b = off[name]
        return _dot_nt(hb, w_ref[a:b, :])

    def qk_norm(y, g_ref):
        y = _rms(y, g_ref[...])
        if rope:
            nxt = pltpu.roll(y, HEAD_DIM - 1, axis=1)
            prv = pltpu.roll(y, 1, axis=1)
            y = y * cos_ref[...] + nxt * se_ref[...] + prv * so_ref[...]
        return y

    q = proj("q_a")
    scale = HEAD_DIM ** -0.5
    for hd in range(N_HEADS):
        sl = slice(hd * HEAD_DIM, (hd + 1) * HEAD_DIM)
        qa_ref[:, sl] = (qk_norm(q[:, sl], gq_ref) * scale).astype(BF16)

    k = proj("k_a")
    v = proj("v_a")
    tb, _, _, ts, _ = k_ref.shape
    for kv in range(N_KV_HEADS):
        sl = slice(kv * HEAD_DIM, (kv + 1) * HEAD_DIM)
        k_ref[:, 0, kv] = qk_norm(k[:, sl], gk_ref).reshape(tb, ts, HEAD_DIM)
        v_ref[:, 0, kv] = v[:, sl].reshape(tb, ts, HEAD_DIM)

    dk = (d // 2) // GLA_HEADS
    qg_ref[...] = (proj("q_g") * (dk ** -0.5)).astype(BF16)
    kg_ref[...] = proj("k_g").astype(BF16)
    vg_ref[...] = proj("v_g").astype(BF16)
    rg_ref[...] = _silu(proj("r_g")).astype(BF16)

    gk = proj("gk").astype(BF16)
    lgf_ref[...] = _log_sigmoid(_dot(gk, wgf_ref[...]) + bgf_ref[...]) * (1.0 / GLA_GATE_NORM)
    lgb_ref[...] = _log_sigmoid(_dot(gk, wgb_ref[...]) + bgb_ref[...]) * (1.0 / GLA_GATE_NORM)

    gate_ref[...] = _sigmoid(proj("gates")).astype(BF16)


def _inproj_call(x2, mod3, mod_row_of_tile, g_pre, w_in_p, g_q, g_k, wgf, bgf, wgb, bgb, rope_tabs,
                 batch, seq):
    n, d = x2.shape
    tm = TOKEN_TILE
    _, dinp = _inproj_layout(d)
    rope = rope_tabs is not None
    gk_w = d // 2
    if seq >= tm:
        tb, ts, per = 1, tm, seq // tm
        kv_map = lambda i: (i // per, 0, 0, i % per, 0)
    else:
        tb, ts, per = tm // seq, seq, 1
        kv_map = lambda i: (i, 0, 0, 0, 0)
    row = lambda i: (i, 0)
    const = lambda i: (0, 0)
    in_specs = [
        pl.BlockSpec((tm, d), row),
        pl.BlockSpec((1, 1, 6 * d), lambda i: (mod_row_of_tile(i, tm), 0, 0)),
        pl.BlockSpec((1, d), const),
        pl.BlockSpec((dinp, d), const, pipeline_mode=pl.Buffered(1)),
        pl.BlockSpec((1, HEAD_DIM), const),
        pl.BlockSpec((1, HEAD_DIM), const),
        pl.BlockSpec((V7X_LANES, gk_w), const),
        pl.BlockSpec((1, gk_w), const),
        pl.BlockSpec((V7X_LANES, gk_w), const),
        pl.BlockSpec((1, gk_w), const),
    ]
    args = [x2, mod3, g_pre, w_in_p, g_q, g_k, wgf, bgf, wgb, bgb]
    if rope:
        tab = pl.BlockSpec((tm, HEAD_DIM), lambda i: (i % per, 0))
        in_specs += [tab, tab, tab]
        args += list(rope_tabs)
    kv_shape = jax.ShapeDtypeStruct((batch, 1, N_KV_HEADS, seq, HEAD_DIM), F32)
    kv_spec = pl.BlockSpec((tb, 1, N_KV_HEADS, ts, HEAD_DIM), kv_map)
    out_shape = (
        jax.ShapeDtypeStruct((n, N_HEADS * HEAD_DIM), BF16), kv_shape, kv_shape,
        jax.ShapeDtypeStruct((n, gk_w), BF16), jax.ShapeDtypeStruct((n, gk_w), BF16),
        jax.ShapeDtypeStruct((n, d), BF16), jax.ShapeDtypeStruct((n, d), BF16),
        jax.ShapeDtypeStruct((n, gk_w), F32), jax.ShapeDtypeStruct((n, gk_w), F32),
        jax.ShapeDtypeStruct((n, 2 * d), BF16),
    )
    out_specs = (
        pl.BlockSpec((tm, N_HEADS * HEAD_DIM), row), kv_spec, kv_spec,
        pl.BlockSpec((tm, gk_w), row), pl.BlockSpec((tm, gk_w), row),
        pl.BlockSpec((tm, d), row), pl.BlockSpec((tm, d), row),
        pl.BlockSpec((tm, gk_w), row), pl.BlockSpec((tm, gk_w), row),
        pl.BlockSpec((tm, 2 * d), row),
    )
    out_row_bytes = 2 * (N_HEADS * HEAD_DIM + 2 * gk_w + 2 * d + 2 * d) + 4 * (4 * HEAD_DIM + 2 * gk_w)
    vmem = d * dinp * 2 + 2 * tm * (d * 4 + out_row_bytes) + 6 * tm * 2 * d * 4
    return pl.pallas_call(
        functools.partial(_inproj_kernel, rope=rope, d=d),
        out_shape=out_shape,
        grid=(n // tm,),
        in_specs=in_specs,
        out_specs=out_specs,
        compiler_params=_cparams(("parallel",), vmem),
        name="inproj_lat" if rope else "inproj_ctx",
    )(*args)


def _attn_kernel(*refs, cached):
    if cached:
        q_ref, k_ref, v_ref, ck_ref, cv_ref, o_ref = refs
    else:
        q_ref, k_ref, v_ref, o_ref = refs
    tq = q_ref.shape[0]
    grp = N_HEADS // N_KV_HEADS
    for kv in range(N_KV_HEADS):
        kk = k_ref[0, 0, kv].astype(BF16)
        vv = v_ref[0, 0, kv].astype(BF16)
        if cached:
            kk = jnp.concatenate([ck_ref[0, 0, kv].astype(BF16), kk], axis=0)
            vv = jnp.concatenate([cv_ref[0, 0, kv].astype(BF16), vv], axis=0)
        heads = [q_ref[:, (kv * grp + g) * HEAD_DIM:(kv * grp + g + 1) * HEAD_DIM] for g in range(grp)]
        q4 = jnp.concatenate(heads, axis=0)
        s = _dot_nt(q4, kk)
        p = jnp.exp(s - jnp.max(s, axis=-1, keepdims=True))
        l = jnp.sum(p, axis=-1, keepdims=True)
        o = _dot(p.astype(BF16), vv) / l
        for g in range(grp):
            hd = kv * grp + g
            o_ref[:, hd * HEAD_DIM:(hd + 1) * HEAD_DIM] = o[g * tq:(g + 1) * tq].astype(BF16)


def _attn_call(q_a, k_a, v_a, cache_k, cache_v):
    batch, _, _, seq, _ = k_a.shape
    n, aq = q_a.shape
    tq = ATTN_Q_TILE
    per = seq // tq
    cached = cache_k is not None
    own = pl.BlockSpec((1, 1, N_KV_HEADS, seq, HEAD_DIM), lambda b, j: (b, 0, 0, 0, 0))
    in_specs = [pl.BlockSpec((tq, aq), lambda b, j: (b * per + j, 0)), own, own]
    args = [q_a, k_a, v_a]
    klen = seq
    if cached:
        past = cache_k.shape[3]
        cspec = pl.BlockSpec((1, 1, N_KV_HEADS, past, HEAD_DIM), lambda b, j: (b, 0, 0, 0, 0))
        in_specs += [cspec, cspec]
        args += [cache_k, cache_v]
        klen += past
    grp = N_HEADS // N_KV_HEADS
    vmem = 4 * grp * tq * klen * 4 + 8 * N_KV_HEADS * klen * HEAD_DIM * 4 + 8 * tq * aq * 2
    return pl.pallas_call(
        functools.partial(_attn_kernel, cached=cached),
        out_shape=jax.ShapeDtypeStruct((n, aq), BF16),
        grid=(batch, per),
        in_specs=in_specs,
        out_specs=pl.BlockSpec((tq, aq), lambda b, j: (b * per + j, 0)),
        compiler_params=_cparams(("parallel", "parallel"), vmem),
        name="attn_lat" if cached else "attn_ctx",
    )(*args)


def _gla_kernel(*refs, nblk, heads, has_state, emit_state):
    refs = list(refs)
    q_ref, k_ref, v_ref, lgf_ref, lgb_ref, rg_ref, gg_ref = refs[:7]
    pos = 7
    if has_state:
        s0f_ref, s0b_ref = refs[pos:pos + 2]
        pos += 2
    og_ref = refs[pos]
    pos += 1
    if emit_state:
        sf_ref, sb_ref = refs[pos:pos + 2]
        pos += 2
    of_scr, ob_scr = refs[pos:pos + 2]

    blk = GLA_BLOCK
    ch = GLA_CHUNK
    nch = blk // ch
    dk = q_ref.shape[1] // heads
    dv = v_ref.shape[1] // heads
    shift = ch.bit_length() - 1
    row_in_chunk = lax.broadcasted_iota(I32, (blk, dk), 0) & (ch - 1)
    ri = lax.broadcasted_iota(I32, (blk, blk), 0)
    ci = lax.broadcasted_iota(I32, (blk, blk), 1)
    same = (ri >> shift) == (ci >> shift)
    col_chunk = lax.broadcasted_iota(I32, (dk, blk), 1) >> shift

    def one_block(b0, hd, state, reverse):
        rows = pl.ds(b0, blk)
        kcols = slice(hd * dk, (hd + 1) * dk)
        q = q_ref[rows, kcols].astype(F32)
        k = k_ref[rows, kcols].astype(F32)
        v = v_ref[rows, hd * dv:(hd + 1) * dv]
        b = (lgb_ref if reverse else lgf_ref)[rows, kcols]
        s = 1
        while s < ch:
            if reverse:
                sh = pltpu.roll(b, blk - s, axis=0)
                b = b + jnp.where(row_in_chunk < ch - s, sh, 0.0)
            else:
                sh = pltpu.roll(b, s, axis=0)
                b = b + jnp.where(row_in_chunk >= s, sh, 0.0)
            s *= 2
        qe = (q * jnp.exp(b)).astype(BF16)
        ke = (k * jnp.exp(-b)).astype(BF16)
        tri = (ci >= ri) if reverse else (ci <= ri)
        a = jnp.where(same & tri, _dot_nt(qe, ke), 0.0).astype(BF16)

        order = range(nch - 1, -1, -1) if reverse else range(nch)
        end_row = [c * ch + (0 if reverse else ch - 1) for c in range(nch)]
        ends = [b[r:r + 1, :] for r in end_row]
        b_end = jnp.concatenate([jnp.broadcast_to(e, (ch, dk)) for e in ends], axis=0)
        kdt = (k * jnp.exp(b_end - b)).T
        stacked = jnp.concatenate(
            [a] + [jnp.where(col_chunk == c, kdt, 0.0).astype(BF16) for c in range(nch)], axis=0)
        big = _dot(stacked, v)
        decay = jnp.exp(jnp.concatenate(ends + [jnp.zeros((dk - nch, dk), F32)], axis=0)).T
        inter = [None] * nch
        for c in order:
            inter[c] = _dot(qe[c * ch:(c + 1) * ch], state.astype(BF16))
            state = decay[:, c:c + 1] * state + big[blk + c * dk:blk + (c + 1) * dk]
        return big[0:blk] + jnp.concatenate(inter, axis=0), state

    for hd in range(heads):
        vcols = slice(hd * dv, (hd + 1) * dv)
        sf = s0f_ref[0, 0, hd] if has_state else jnp.zeros((dk, dv), F32)
        sb = s0b_ref[0, 0, hd] if has_state else jnp.zeros((dk, dv), F32)
        for i in range(nblk):
            j = nblk - 1 - i
            o_f, sf = one_block(i * blk, hd, sf, False)
            o_b, sb = one_block(j * blk, hd, sb, True)
            of_scr[pl.ds(i * blk, blk), vcols] = o_f
            ob_scr[pl.ds(j * blk, blk), vcols] = o_b
        if emit_state:
            sf_ref[0, 0, hd] = sf
            sb_ref[0, 0, hd] = sb
        o = of_scr[:, vcols] + ob_scr[:, vcols]
        og_ref[:, vcols] = (_rms(o, gg_ref[...]) * rg_ref[:, vcols].astype(F32)).astype(BF16)


def _gla_call(q_g, k_g, v_g, lg_f, lg_b, r_g, g_gla, state_f, state_b, batch, seq, emit_state, heads):
    n, gkw = q_g.shape
    d = v_g.shape[1]
    dk, dv = gkw // GLA_HEADS, d // GLA_HEADS
    has_state = state_f is not None
    kspec = pl.BlockSpec((seq, heads * dk), lambda b, h: (b, h))
    vspec = pl.BlockSpec((seq, heads * dv), lambda b, h: (b, h))
    sspec = pl.BlockSpec((1, 1, heads, dk, dv), lambda b, h: (b, 0, h, 0, 0))
    in_specs = [kspec, kspec, vspec, kspec, kspec, vspec, pl.BlockSpec((1, dv), lambda b, h: (0, 0))]
    args = [q_g, k_g, v_g, lg_f, lg_b, r_g, g_gla]
    if has_state:
        in_specs += [sspec, sspec]
        args += [state_f, state_b]
    out_shape = [jax.ShapeDtypeStruct((n, d), BF16)]
    out_specs = [vspec]
    if emit_state:
        st = jax.ShapeDtypeStruct((batch, 1, GLA_HEADS, dk, dv), F32)
        out_shape += [st, st]
        out_specs += [sspec, sspec]
    vmem = 12 * seq * heads * dv * 4 + 40 * GLA_BLOCK * GLA_BLOCK * 4
    return pl.pallas_call(
        functools.partial(_gla_kernel, nblk=seq // GLA_BLOCK, heads=heads, has_state=has_state,
                          emit_state=emit_state),
        out_shape=tuple(out_shape),
        grid=(batch, GLA_HEADS // heads),
        in_specs=in_specs,
        out_specs=tuple(out_specs),
        scratch_shapes=[pltpu.VMEM((seq, heads * dv), F32), pltpu.VMEM((seq, heads * dv), F32)],
        compiler_params=_cparams(("parallel", "parallel"), vmem),
        name="gla_ctx" if emit_state else "gla_lat",
    )(*args)


def _outproj_kernel(oa_ref, og_ref, gate_ref, x_ref, mod_ref, wpa_ref, wpg_ref, wout_ref, gpm_ref,
                    gpf_ref, wr_ref, x1_ref, h_ref, aff_ref):
    d = x_ref.shape[1]
    m = mod_ref[0]
    oa = _dot(oa_ref[...], wpa_ref[...])
    og = _dot(og_ref[...], wpg_ref[...])
    mix = gate_ref[:, 0:d].astype(F32) * oa + gate_ref[:, d:2 * d].astype(F32) * og
    mo = _dot(mix.astype(BF16), wout_ref[...])
    x1 = x_ref[...] + m[:, 2 * d:3 * d] * _rms(mo, gpm_ref[...])
    x1_ref[...] = x1
    hb = (_rms(x1, gpf_ref[...]) * (1.0 + m[:, 4 * d:5 * d]) + m[:, 3 * d:4 * d]).astype(BF16)
    h_ref[...] = hb
    logits = _dot(hb, wr_ref[...])
    valid = lax.broadcasted_iota(I32, logits.shape, 1) < N_EXPERTS
    mx = jnp.max(jnp.where(valid, logits, -jnp.inf), axis=-1, keepdims=True)
    ex = jnp.where(valid, jnp.exp(logits - mx), 0.0)
    aff_ref[...] = ex / jnp.sum(ex, axis=-1, keepdims=True)


def _outproj_call(o_a, o_g, gates, x2, mod3, mod_row_of_tile, w_pa, w_pg, w_out, g_pm, g_pf, w_r, tag):
    n, d = x2.shape
    tm = TOKEN_TILE
    row = lambda i: (i, 0)
    const = lambda i: (0, 0)
    wspec = pl.BlockSpec((d, d), const)
    vmem = 2 * 3 * d * d * 2 + 2 * tm * d * (2 + 2 + 4 + 4 + 4 + 2) + 8 * tm * d * 4
    return pl.pallas_call(
        _outproj_kernel,
        out_shape=(jax.ShapeDtypeStruct((n, d), F32), jax.ShapeDtypeStruct((n, d), BF16),
                   jax.ShapeDtypeStruct((n, V7X_LANES), F32)),
        grid=(n // tm,),
        in_specs=[pl.BlockSpec((tm, d), row), pl.BlockSpec((tm, d), row), pl.BlockSpec((tm, 2 * d), row),
                  pl.BlockSpec((tm, d), row),
                  pl.BlockSpec((1, 1, 6 * d), lambda i: (mod_row_of_tile(i, tm), 0, 0)),
                  wspec, wspec, wspec, pl.BlockSpec((1, d), const), pl.BlockSpec((1, d), const),
                  pl.BlockSpec((d, V7X_LANES), const)],
        out_specs=(pl.BlockSpec((tm, d), row), pl.BlockSpec((tm, d), row),
                   pl.BlockSpec((tm, V7X_LANES), row)),
        compiler_params=_cparams(("parallel",), vmem),
        name="outproj_" + tag,
    )(o_a, o_g, gates, x2, mod3, w_pa, w_pg, w_out, g_pm, g_pf, w_r)


def _route_kernel(aff_ref, pos_ref, post_ref, tbl_ref, bnd_ref, afft_scr, *, cap, n_row_tiles):
    n = aff_ref.shape[0]
    rb = ROUTE_BLOCK
    nb = n // rb
    lanes = aff_ref.shape[1]
    lane = lax.broadcasted_iota(I32, (1, lanes), 1)
    expert_lane = lane < N_EXPERTS
    tbl_ref[...] = jnp.zeros(tbl_ref.shape, I32)
    bnd_ref[...] = jnp.zeros(bnd_ref.shape, I32)

    def to_token_lanes(c, carry):
        start = pl.multiple_of(c * rb, rb)
        afft_scr[c] = aff_ref[pl.ds(start, rb), :].T[0:N_EXPERTS, :]
        return carry

    lax.fori_loop(0, nb, to_token_lanes, 0)
    aff_t = afft_scr[...]

    def count(hit):
        return jnp.sum(jnp.sum(hit.astype(I32), axis=0), axis=1, keepdims=True)

    def bit_step(i, lo):
        t = lo | jnp.left_shift(jnp.int32(1), 30 - i)
        ge = aff_t >= lax.bitcast_convert_type(t, F32)[None]
        return jnp.where(count(ge) >= cap, t, lo)

    thr_bits = lax.fori_loop(0, 31, bit_step, jnp.zeros((N_EXPERTS, 1), I32))
    need_t = cap - count(aff_t > lax.bitcast_convert_type(thr_bits, F32)[None])

    def to_expert_lanes(col):
        full = jnp.concatenate([jnp.broadcast_to(col, (N_EXPERTS, lanes)),
                                jnp.zeros((lanes - N_EXPERTS, lanes), I32)], axis=0)
        return full.T[0:1, :]

    thr = lax.bitcast_convert_type(to_expert_lanes(thr_bits), F32)
    need = to_expert_lanes(need_t).astype(F32)
    capf = float(cap)

    r = lax.broadcasted_iota(I32, (rb, rb), 0)
    c_ = lax.broadcasted_iota(I32, (rb, rb), 1)
    tril = jnp.where(c_ <= r, 1.0, 0.0).astype(BF16)

    def blk_step(c, carry):
        eq_before, raw_before, lo_cnt, hi_cnt = carry
        start = pl.multiple_of(c * rb, rb)
        a = aff_ref[pl.ds(start, rb), :]
        gt = a > thr
        eq = a == thr
        eq_incl = _dot(tril, jnp.where(eq, 1.0, 0.0).astype(BF16)) + eq_before
        raw = (gt | (eq & (eq_incl <= need))) & expert_lane
        raw_incl = _dot(tril, jnp.where(raw, 1.0, 0.0).astype(BF16)) + raw_before
        sel = raw & (raw_incl <= capf)
        self_ = jnp.where(sel, 1.0, 0.0)
        incl = jnp.minimum(raw_incl, capf)
        sel_before = jnp.minimum(raw_before, capf)
        excl = incl - self_
        posb = jnp.where(sel, excl, -1.0).astype(I32)
        pos_ref[pl.ds(start, rb), :] = posb
        post_ref[c] = posb.T[0:N_EXPERTS, :]
        tbl_ref[pl.ds(c, 1), :] = sel_before.astype(I32)
        lo_new, hi_new = [], []
        for j in range(n_row_tiles):
            lo_new.append(lo_cnt[j] + jnp.sum(jnp.where(incl <= j * FFN_ROW_TILE, 1.0, 0.0), axis=0, keepdims=True))
            hi_new.append(hi_cnt[j] + jnp.sum(jnp.where(excl < (j + 1) * FFN_ROW_TILE, 1.0, 0.0), axis=0, keepdims=True))
        return (eq_incl[rb - 1:rb, :], raw_incl[rb - 1:rb, :], tuple(lo_new), tuple(hi_new))

    zero = jnp.zeros((1, lanes), F32)
    zeros_j = tuple(zero for _ in range(n_row_tiles))
    _, total, lo_cnt, hi_cnt = lax.fori_loop(0, nb, blk_step, (zero, zero, zeros_j, zeros_j))
    tbl_ref[pl.ds(nb, 1), :] = jnp.minimum(total, capf).astype(I32)
    for j in range(n_row_tiles):
        bnd_ref[pl.ds(j, 1), :] = lo_cnt[j].astype(I32)
        bnd_ref[pl.ds(n_row_tiles + j, 1), :] = hi_cnt[j].astype(I32)


def _route_call(aff, cap, tag):
    n, lanes = aff.shape
    nb = n // ROUTE_BLOCK
    n_row_tiles = cap // FFN_ROW_TILE
    tbl_rows = -(-(nb + 1) // 8) * 8
    bnd_rows = -(-(2 * n_row_tiles) // 8) * 8
    full = lambda *shape: pl.BlockSpec(shape, lambda: tuple(0 for _ in shape))
    return pl.pallas_call(
        functools.partial(_route_kernel, cap=cap, n_row_tiles=n_row_tiles),
        out_shape=(jax.ShapeDtypeStruct((n, lanes), I32),
                   jax.ShapeDtypeStruct((nb, N_EXPERTS, ROUTE_BLOCK), I32),
                   jax.ShapeDtypeStruct((tbl_rows, lanes), I32),
                   jax.ShapeDtypeStruct((bnd_rows, lanes), I32)),
        in_specs=[full(n, lanes)],
        out_specs=(full(n, lanes), full(nb, N_EXPERTS, ROUTE_BLOCK), full(tbl_rows, lanes),
                   full(bnd_rows, lanes)),
        scratch_shapes=[pltpu.VMEM((nb, N_EXPERTS, ROUTE_BLOCK), F32)],
        compiler_params=_cparams((), 8 * n * lanes * 4),
        name="route_" + tag,
    )(aff)


def _ffn_kernel(*refs, caps):
    ng = len(caps)
    bnd_refs = refs[:ng]
    h_refs = refs[ng:3 * ng:2]
    post_refs = refs[ng + 1:3 * ng:2]
    w1_ref, w3_ref, w2_ref = refs[3 * ng:3 * ng + 3]
    ye_refs = refs[3 * ng + 3:4 * ng + 3]
    xs_scr, acc_scr = refs[4 * ng + 3:]
    row_off = [sum(caps[:g]) for g in range(ng)]
    e = pl.program_id(0)
    f = pl.program_id(1)
    tr = FFN_ROW_TILE
    kc = FFN_GATHER_TOKENS
    per = kc // ROUTE_BLOCK

    @pl.when(f == 0)
    def _gather():
        rid = lax.broadcasted_iota(I32, (tr, kc), 0)
        for g in range(ng):
            n_row_tiles = caps[g] // tr
            for j in range(n_row_tiles):
                lo = bnd_refs[g][j, e]
                hi = bnd_refs[g][n_row_tiles + j, e]
                rows = pl.ds(row_off[g] + j * tr, tr)
                acc_scr[rows, :] = jnp.zeros((tr, acc_scr.shape[1]), F32)

                def chunk(c, carry, g=g, j=j, rows=rows):
                    p = jnp.concatenate([post_refs[g][per * c + s] for s in range(per)], axis=1)
                    onehot = jnp.where(p == rid + j * tr, 1.0, 0.0).astype(BF16)
                    start = pl.multiple_of(c * kc, kc)
                    acc_scr[rows, :] += _dot(onehot, h_refs[g][pl.ds(start, kc), :])
                    return carry

                lax.fori_loop(lo // kc, (hi + kc - 1) // kc, chunk, 0)
                xs_scr[rows, :] = acc_scr[rows, :].astype(BF16)

    xs = xs_scr[...]
    hid = _silu(_dot(xs, w1_ref[...].astype(BF16))) * _dot(xs, w3_ref[...].astype(BF16))
    part = _dot(hid.astype(BF16), w2_ref[...].astype(BF16))

    @pl.when(f == 0)
    def _first():
        acc_scr[...] = part

    @pl.when(f > 0)
    def _rest():
        acc_scr[...] += part

    @pl.when(f == pl.num_programs(1) - 1)
    def _emit():
        for g in range(ng):
            ye_refs[g][...] = acc_scr[pl.ds(row_off[g], caps[g]), :].astype(BF16)


def _ffn_call(groups, w1, w3, w2):
    ng = len(groups)
    caps = tuple(g[3] for g in groups)
    d = groups[0][1].shape[1]
    n_exp, _, dff = w1.shape
    tf = dff // FFN_F_TILES
    rows = sum(caps)
    idx = lambda fn: (lambda e, f, *_: fn(e, f))
    in_specs, args, h_bytes = [], [], 0
    for bnd, h, post4, cap in groups:
        n = h.shape[0]
        in_specs += [pl.BlockSpec((n, d), idx(lambda e, f: (0, 0)), pipeline_mode=pl.Buffered(1)),
                     pl.BlockSpec((n // ROUTE_BLOCK, None, 1, ROUTE_BLOCK), idx(lambda e, f: (0, e, 0, 0)))]
        args += [h, post4]
        h_bytes += n * d * 2
    in_specs += [pl.BlockSpec((None, d, tf), idx(lambda e, f: (e, 0, f))),
                 pl.BlockSpec((None, d, tf), idx(lambda e, f: (e, 0, f))),
                 pl.BlockSpec((None, tf, d), idx(lambda e, f: (e, f, 0)))]
    grid_spec = pltpu.PrefetchScalarGridSpec(
        num_scalar_prefetch=ng,
        grid=(n_exp, FFN_F_TILES),
        in_specs=in_specs,
        out_specs=tuple(pl.BlockSpec((cap, d), idx(lambda e, f: (e, 0))) for cap in caps),
        scratch_shapes=[pltpu.VMEM((rows, d), BF16), pltpu.VMEM((rows, d), F32)],
    )
    vmem = h_bytes + 3 * d * tf * (2 * 4 + 2) + rows * d * (2 + 4 + 2 * 2 + 4) + 4 * rows * tf * 4
    return pl.pallas_call(
        functools.partial(_ffn_kernel, caps=caps),
        out_shape=tuple(jax.ShapeDtypeStruct((n_exp * cap, d), BF16) for cap in caps),
        grid_spec=grid_spec,
        compiler_params=_cparams(("arbitrary", "arbitrary"), vmem),
        name="ffn",
    )(*[g[0] for g in groups], *args, w1, w3, w2)


def _combine_kernel(tbl_ref, ye_hbm, pos_ref, aff_ref, x1_ref, mod_ref, gpo_ref, y_ref, buf, sem, xbuf, xsem,
                    acc_scr, *, cap):
    i = pl.program_id(0)
    nsteps = pl.num_programs(0)
    d = x1_ref.shape[1]
    lanes = pos_ref.shape[1]
    win = COMBINE_WINDOW
    last_start = ye_hbm.shape[0] - win
    slot = i % 2

    def first_row(step, e):
        return tbl_ref[step, e] + e * cap

    def window_start(first, k):
        unclamped = (first // BF16_ROWS_PER_TILE) * BF16_ROWS_PER_TILE + k * win
        return unclamped, jnp.minimum(unclamped, last_start)

    def fetch(step, to_slot, e):
        start = window_start(first_row(step, e), 0)[1]
        return pltpu.make_async_copy(ye_hbm.at[pl.ds(pl.multiple_of(start, BF16_ROWS_PER_TILE), win), :],
                                     buf.at[to_slot, pl.ds(e * win, win), :], sem.at[to_slot, e])

    @pl.when(i == 0)
    def _prime():
        for e in range(N_EXPERTS):
            fetch(0, 0, e).start()

    @pl.when(i + 1 < nsteps)
    def _ahead():
        for e in range(N_EXPERTS):
            fetch(i + 1, 1 - slot, e).start()

    lane_row = lax.broadcasted_iota(I32, (1, win), 1)
    pieces = []
    for e in range(N_EXPERTS):
        pcol = pos_ref[:, e:e + 1]
        grow = jnp.where(pcol >= 0, pcol + e * cap, -1)
        start = window_start(first_row(i, e), 0)[1]
        pieces.append(jnp.where(grow == start + lane_row, aff_ref[:, e:e + 1], 0.0).astype(BF16))
    onehot = jnp.concatenate(pieces, axis=1)
    for e in range(N_EXPERTS):
        fetch(i, slot, e).wait()
    acc_scr[...] = _dot(onehot, buf[slot])

    def expert_extra(e, carry):
        first = first_row(i, e)
        end = first_row(i + 1, e)
        covered = window_start(first, 1)[0]
        extra = jnp.maximum(end - covered + win - 1, 0) // win

        def more(k, c):
            unclamped, start = window_start(first, k)
            cp = pltpu.make_async_copy(ye_hbm.at[pl.ds(pl.multiple_of(start, BF16_ROWS_PER_TILE), win), :],
                                       xbuf, xsem)
            cp.start()
            cp.wait()
            at_e = lax.broadcasted_iota(I32, (1, lanes), 1) == e
            pcol = jnp.sum(jnp.where(at_e, pos_ref[...].astype(F32), 0.0), axis=1, keepdims=True).astype(I32)
            wcol = jnp.sum(jnp.where(at_e, aff_ref[...], 0.0), axis=1, keepdims=True)
            grow = jnp.where(pcol >= 0, pcol + e * cap, -1)
            hit = (grow == start + lane_row) & (grow >= unclamped)
            acc_scr[...] += _dot(jnp.where(hit, wcol, 0.0).astype(BF16), xbuf[...])
            return c

        lax.fori_loop(1, 1 + extra, more, 0)
        return carry

    lax.fori_loop(0, N_EXPERTS, expert_extra, 0)

    m = mod_ref[0]
    y_ref[...] = x1_ref[...] + m[:, 5 * d:6 * d] * _rms(acc_scr[...], gpo_ref[...])


def _combine_call(tbl, ye, pos, aff, x1, mod3, mod_row_of_tile, g_po, cap, tag):
    n, d = x1.shape
    tm = COMBINE_TILE
    lanes = pos.shape[1]
    grid_spec = pltpu.PrefetchScalarGridSpec(
        num_scalar_prefetch=1,
        grid=(n // tm,),
        in_specs=[pl.BlockSpec(memory_space=pl.ANY),
                  pl.BlockSpec((tm, lanes), lambda i, t: (i, 0)),
                  pl.BlockSpec((tm, lanes), lambda i, t: (i, 0)),
                  pl.BlockSpec((tm, d), lambda i, t: (i, 0)),
                  pl.BlockSpec((1, 1, 6 * d), lambda i, t: (mod_row_of_tile(i, tm), 0, 0)),
                  pl.BlockSpec((1, d), lambda i, t: (0, 0))],
        out_specs=pl.BlockSpec((tm, d), lambda i, t: (i, 0)),
        scratch_shapes=[pltpu.VMEM((2, N_EXPERTS * COMBINE_WINDOW, d), BF16),
                        pltpu.SemaphoreType.DMA((2, N_EXPERTS)),
                        pltpu.VMEM((COMBINE_WINDOW, d), BF16),
                        pltpu.SemaphoreType.DMA(()),
                        pltpu.VMEM((tm, d), F32)],
    )
    vmem = 2 * N_EXPERTS * COMBINE_WINDOW * d * 2 + 10 * tm * d * 4 + 2 * tm * N_EXPERTS * COMBINE_WINDOW * 4
    return pl.pallas_call(
        functools.partial(_combine_kernel, cap=cap),
        out_shape=jax.ShapeDtypeStruct((n, d), F32),
        grid_spec=grid_spec,
        compiler_params=_cparams(("arbitrary",), vmem),
        name="combine_" + tag,
    )(tbl, ye, pos, aff, x1, mod3, g_po)


def _rope_tables(seq):
    rows = seq // GRID_W
    r = jnp.repeat(jnp.arange(rows), GRID_W).astype(F32)
    col = jnp.tile(jnp.arange(GRID_W), rows).astype(F32)
    pairs = HEAD_DIM // 4
    freqs = ROPE_THETA ** (-jnp.arange(pairs, dtype=F32) / pairs)
    ang = jnp.concatenate([r[:, None] * freqs, col[:, None] * freqs], axis=-1)
    cos = jnp.repeat(jnp.cos(ang), 2, axis=-1)
    sin = jnp.repeat(jnp.sin(ang), 2, axis=-1)
    even = (jnp.arange(HEAD_DIM) % 2) == 0
    return cos, jnp.where(even, -sin, 0.0), jnp.where(even, 0.0, sin)


def _trunk_to_routing(x, mod3, mod_row_of_tile, rope_tabs, ctx, lw, tag):
    (g_pre_mix, g_post_mix, g_pre_ffn, g_post_ffn, w_in_p, g_q, g_k, wgf, bgf, wgb, bgb, g_gla,
     w_pa, w_pg, w_out, w_r, w1, w3, w2) = lw
    batch, seq, d = x.shape
    n = batch * seq
    x2 = x.reshape(n, d)
    (q_a, k_a, v_a, q_g, k_g, v_g, r_g, lg_f, lg_b, gates) = _inproj_call(
        x2, mod3, mod_row_of_tile, g_pre_mix, w_in_p, g_q, g_k, wgf, bgf, wgb, bgb, rope_tabs, batch, seq)
    if ctx is None:
        o_a = _attn_call(q_a, k_a, v_a, None, None)
        o_g, s_f, s_b = _gla_call(q_g, k_g, v_g, lg_f, lg_b, r_g, g_gla, None, None, batch, seq, True,
                                  GLA_CTX_HEADS_PER_STEP)
    else:
        ck, cv, s_f0, s_b0 = ctx
        o_a = _attn_call(q_a, k_a, v_a, ck, cv)
        (o_g,) = _gla_call(q_g, k_g, v_g, lg_f, lg_b, r_g, g_gla, s_f0, s_b0, batch, seq, False,
                           GLA_LAT_HEADS_PER_STEP)
        s_f = s_b = None
    x1, h, aff = _outproj_call(o_a, o_g, gates, x2, mod3, mod_row_of_tile, w_pa, w_pg, w_out,
                               g_post_mix, g_pre_ffn, w_r, tag)
    cap = (EC_CAPACITY_FACTOR * n) // N_EXPERTS
    pos, post, tbl, bnd = _route_call(aff, cap, tag)
    post4 = post.reshape(post.shape[0], N_EXPERTS, 1, ROUTE_BLOCK)
    return dict(x1=x1, h=h, aff=aff, pos=pos, post4=post4, tbl=tbl, bnd=bnd, cap=cap,
                mod_row_of_tile=mod_row_of_tile, tag=tag, shape=(batch, seq, d)), (k_a, v_a, s_f, s_b)


def _expert_ffn(groups, mod3, g_post_ffn, w1, w3, w2):
    yes = _ffn_call([(g["bnd"], g["h"], g["post4"], g["cap"]) for g in groups], w1, w3, w2)
    outs = []
    for g, ye in zip(groups, yes):
        y = _combine_call(g["tbl"], ye, g["pos"], g["aff"], g["x1"], mod3, g["mod_row_of_tile"], g_post_ffn,
                          g["cap"], g["tag"])
        outs.append(y.reshape(g["shape"]))
    return outs


def kernel(x_prompt, x_sample, cache_k, cache_v, state_gla_fwd, state_gla_bwd, c, c_ctx, g_pre_mix, g_post_mix, g_pre_ffn, g_post_ffn, w_mod, b_mod, w_in, g_q, g_k, w_gk2_f, b_gk_f, w_gk2_b, b_gk_b, g_gla, w_pa, w_pg, w_out, w_router, w1, w3, w2):
    depth = w_in.shape[0]
    assert depth == 1, "single trunk layer"
    d = x_prompt.shape[-1]
    dec_batch, dec_seq, _ = x_sample.shape
    assert dec_batch + 1 <= MOD_ROWS
    l = 0
    rank = GLA_GATE_RANK
    w_in_p = jnp.swapaxes(w_in[l], 0, 1).astype(BF16)
    gkw = w_gk2_f.shape[-1]
    wgf = jnp.zeros((V7X_LANES, gkw), F32).at[0:rank].set(w_gk2_f[l]).astype(BF16)
    wgb = jnp.zeros((V7X_LANES, gkw), F32).at[rank:2 * rank].set(w_gk2_b[l]).astype(BF16)
    w_r = jnp.zeros((d, V7X_LANES), F32).at[:, :N_EXPERTS].set(w_router[l]).astype(BF16)
    row = lambda a: a[l].reshape(1, -1)
    lw = (row(g_pre_mix), row(g_post_mix), row(g_pre_ffn), row(g_post_ffn), w_in_p, row(g_q), row(g_k),
          wgf, row(b_gk_f), wgb, row(b_gk_b), row(g_gla),
          w_pa[l].astype(BF16), w_pg[l].astype(BF16), w_out[l].astype(BF16), w_r, w1[l], w3[l], w2[l])

    cc = jnp.concatenate([c_ctx[None, :], c, jnp.zeros((MOD_ROWS - 1 - dec_batch, d), F32)], axis=0)
    mod = _mod_call(cc, w_mod[l], b_mod[l].reshape(1, -1))
    mod3 = mod.reshape(MOD_ROWS, 1, 6 * d)

    gp, (nk, nv, nsf, nsb) = _trunk_to_routing(x_prompt, mod3, lambda i, tm: 0, None, None, lw, "ctx")
    ctx = (cache_k, cache_v, state_gla_fwd, state_gla_bwd)
    gs, _ = _trunk_to_routing(x_sample, mod3, lambda i, tm: 1 + (i * tm) // dec_seq, _rope_tables(dec_seq),
                              ctx, lw, "lat")
    yp, ys = _expert_ffn([gp, gs], mod3, lw[3], lw[16], lw[17], lw[18])
    return (yp, ys, nk, nv, nsf, nsb)
```

```python
import functools

import jax
import jax.numpy as jnp
from jax import lax
from jax.experimental import pallas as pl
from jax.experimental.pallas import tpu as pltpu

F32 = jnp.float32
BF16 = jnp.bfloat16
I32 = jnp.int32

N_HEADS = 8
N_KV_HEADS = 2
HEAD_DIM = 128
GRID_W = 64
ROPE_THETA = 10000.0
GLA_HEADS = 4
GLA_GATE_RANK = 16
GLA_GATE_NORM = 16.0
GLA_CHUNK = 64
N_EXPERTS = 16
EC_CAPACITY_FACTOR = 2
EPS = 1e-6

V7X_LANES = 128
V7X_VMEM_BYTES = 64 * 1024 * 1024
V7X_VMEM_RESERVE_BYTES = 6 * 1024 * 1024
BF16_ROWS_PER_TILE = 16

TOKEN_TILE = 512
OUTPROJ_TILE = 1024
ATTN_Q_TILE = 256
GLA_BLOCK = 256
GLA_CTX_HEADS_PER_STEP = 4
GLA_LAT_HEADS_PER_STEP = 1
ROUTE_BLOCK = 256
FFN_ROW_TILE = 256
FFN_GATHER_TOKENS = 1024
FFN_F_TILES = 2
COMBINE_TILE = 512
COMBINE_WINDOW = 128
MOD_ROWS = 8
MOD_N_TILE = 1536


def _cparams(semantics, vmem_bytes):
    limit = min(max(int(vmem_bytes), 32 * 1024 * 1024), V7X_VMEM_BYTES - V7X_VMEM_RESERVE_BYTES)
    return pltpu.CompilerParams(dimension_semantics=semantics, vmem_limit_bytes=limit)


def _sigmoid(x):
    return 1.0 / (1.0 + jnp.exp(-x))


def _silu(x):
    return x * _sigmoid(x)


def _log_sigmoid(x):
    return jnp.minimum(x, 0.0) - jnp.log1p(jnp.exp(-jnp.abs(x)))


def _rms(x, g):
    ms = jnp.mean(x * x, axis=-1, keepdims=True)
    return x * lax.rsqrt(ms + EPS) * g


def _dot(a, b):
    return jnp.dot(a, b, preferred_element_type=F32)


def _dot_nt(a, b):
    return lax.dot_general(a, b, (((1,), (1,)), ((), ())), preferred_element_type=F32)


def _mod_kernel(c_ref, w_ref, b_ref, o_ref):
    s = _silu(c_ref[...]).astype(BF16)
    o_ref[...] = _dot(s, w_ref[...].astype(BF16)) + b_ref[...]


def _mod_call(cc, w_mod, b_mod):
    d, n6 = w_mod.shape
    tn = MOD_N_TILE
    return pl.pallas_call(
        _mod_kernel,
        out_shape=jax.ShapeDtypeStruct((MOD_ROWS, n6), F32),
        grid=(n6 // tn,),
        in_specs=[pl.BlockSpec((MOD_ROWS, d), lambda j: (0, 0)),
                  pl.BlockSpec((d, tn), lambda j: (0, j)),
                  pl.BlockSpec((1, tn), lambda j: (0, j))],
        out_specs=pl.BlockSpec((MOD_ROWS, tn), lambda j: (0, j)),
        compiler_params=_cparams(("arbitrary",), 3 * d * tn * 4),
        name="mod",
    )(cc, w_mod, b_mod)


def _inproj_layout(d):
    aq, akv = N_HEADS * HEAD_DIM, N_KV_HEADS * HEAD_DIM
    gk, gv = d // 2, d
    names = ("q_a", "k_a", "v_a", "q_g", "k_g", "v_g", "r_g", "gk_f", "gk_b", "gates")
    widths = (aq, akv, akv, gk, gk, gv, gv, GLA_GATE_RANK, GLA_GATE_RANK, 2 * d)
    off, o = {}, 0
    for nme, w in zip(names, widths):
        off[nme] = (o, o + w)
        o += w
    off["gk"] = (off["gk_f"][0], off["gk_f"][0] + V7X_LANES)
    return off, o


def _inproj_kernel(*refs, rope, d):
    if rope:
        (x_ref, mod_ref, gpre_ref, w_ref, gq_ref, gk_ref, wgf_ref, bgf_ref, wgb_ref, bgb_ref,
         cos_ref, se_ref, so_ref, *outs) = refs
    else:
        (x_ref, mod_ref, gpre_ref, w_ref, gq_ref, gk_ref, wgf_ref, bgf_ref, wgb_ref, bgb_ref,
         *outs) = refs
    qa_ref, k_ref, v_ref, qg_ref, kg_ref, vg_ref, rg_ref, lgf_ref, lgb_ref, gate_ref = outs
    off, _ = _inproj_layout(d)
    m = mod_ref[0]
    h = _rms(x_ref[...], gpre_ref[...]) * (1.0 + m[:, d:2 * d]) + m[:, 0:d]
    hb = h.astype(BF16)

    def proj(name):
        a, b = off[name]
        return _dot_nt(hb, w_ref[a:b, :])

    def qk_norm(y, g_ref):
        y = _rms(y, g_ref[...])
        if rope:
            nxt = pltpu.roll(y, HEAD_DIM - 1, axis=1)
            prv = pltpu.roll(y, 1, axis=1)
            y = y * cos_ref[...] + nxt * se_ref[...] + prv * so_ref[...]
        return y

    q = proj("q_a")
    scale = HEAD_DIM ** -0.5
    for hd in range(N_HEADS):
        sl = slice(hd * HEAD_DIM, (hd + 1) * HEAD_DIM)
        qa_ref[:, sl] = (qk_norm(q[:, sl], gq_ref) * scale).astype(BF16)

    k = proj("k_a")
    v = proj("v_a")
    tb, _, _, ts, _ = k_ref.shape
    for kv in range(N_KV_HEADS):
        sl = slice(kv * HEAD_DIM, (kv + 1) * HEAD_DIM)
        k_ref[:, 0, kv] = qk_norm(k[:, sl], gk_ref).reshape(tb, ts, HEAD_DIM)
        v_ref[:, 0, kv] = v[:, sl].reshape(tb, ts, HEAD_DIM)

    dk = (d // 2) // GLA_HEADS
    qg_ref[...] = (proj("q_g") * (dk ** -0.5)).astype(BF16)
    kg_ref[...] = proj("k_g").astype(BF16)
    vg_ref[...] = proj("v_g").astype(BF16)
    rg_ref[...] = _silu(proj("r_g")).astype(BF16)

    gk = proj("gk").astype(BF16)
    lgf_ref[...] = _log_sigmoid(_dot(gk, wgf_ref[...]) + bgf_ref[...]) * (1.0 / GLA_GATE_NORM)
    lgb_ref[...] = _log_sigmoid(_dot(gk, wgb_ref[...]) + bgb_ref[...]) * (1.0 / GLA_GATE_NORM)

    gate_ref[...] = _sigmoid(proj("gates")).astype(BF16)


def _inproj_call(x2, mod3, mod_row_of_tile, g_pre, w_in_p, g_q, g_k, wgf, bgf, wgb, bgb, rope_tabs,
                 batch, seq):
    n, d = x2.shape
    tm = TOKEN_TILE
    _, dinp = _inproj_layout(d)
    rope = rope_tabs is not None
    gk_w = d // 2
    if seq >= tm:
        tb, ts, per = 1, tm, seq // tm
        kv_map = lambda i: (i // per, 0, 0, i % per, 0)
    else:
        tb, ts, per = tm // seq, seq, 1
        kv_map = lambda i: (i, 0, 0, 0, 0)
    row = lambda i: (i, 0)
    const = lambda i: (0, 0)
    in_specs = [
        pl.BlockSpec((tm, d), row),
        pl.BlockSpec((1, 1, 6 * d), lambda i: (mod_row_of_tile(i, tm), 0, 0)),
        pl.BlockSpec((1, d), const),
        pl.BlockSpec((dinp, d), const, pipeline_mode=pl.Buffered(1)),
        pl.BlockSpec((1, HEAD_DIM), const),
        pl.BlockSpec((1, HEAD_DIM), const),
        pl.BlockSpec((V7X_LANES, gk_w), const),
        pl.BlockSpec((1, gk_w), const),
        pl.BlockSpec((V7X_LANES, gk_w), const),
        pl.BlockSpec((1, gk_w), const),
    ]
    args = [x2, mod3, g_pre, w_in_p, g_q, g_k, wgf, bgf, wgb, bgb]
    if rope:
        tab = pl.BlockSpec((tm, HEAD_DIM), lambda i: (i % per, 0))
        in_specs += [tab, tab, tab]
        args += list(rope_tabs)
    kv_shape = jax.ShapeDtypeStruct((batch, 1, N_KV_HEADS, seq, HEAD_DIM), F32)
    kv_spec = pl.BlockSpec((tb, 1, N_KV_HEADS, ts, HEAD_DIM), kv_map)
    out_shape = (
        jax.ShapeDtypeStruct((n, N_HEADS * HEAD_DIM), BF16), kv_shape, kv_shape,
        jax.ShapeDtypeStruct((n, gk_w), BF16), jax.ShapeDtypeStruct((n, gk_w), BF16),
        jax.ShapeDtypeStruct((n, d), BF16), jax.ShapeDtypeStruct((n, d), BF16),
        jax.ShapeDtypeStruct((n, gk_w), F32), jax.ShapeDtypeStruct((n, gk_w), F32),
        jax.ShapeDtypeStruct((n, 2 * d), BF16),
    )
    out_specs = (
        pl.BlockSpec((tm, N_HEADS * HEAD_DIM), row), kv_spec, kv_spec,
        pl.BlockSpec((tm, gk_w), row), pl.BlockSpec((tm, gk_w), row),
        pl.BlockSpec((tm, d), row), pl.BlockSpec((tm, d), row),
        pl.BlockSpec((tm, gk_w), row), pl.BlockSpec((tm, gk_w), row),
        pl.BlockSpec((tm, 2 * d), row),
    )
    out_row_bytes = 2 * (N_HEADS * HEAD_DIM + 2 * gk_w + 2 * d + 2 * d) + 4 * (4 * HEAD_DIM + 2 * gk_w)
    vmem = d * dinp * 2 + 2 * tm * (d * 4 + out_row_bytes) + 6 * tm * 2 * d * 4
    return pl.pallas_call(
        functools.partial(_inproj_kernel, rope=rope, d=d),
        out_shape=out_shape,
        grid=(n // tm,),
        in_specs=in_specs,
        out_specs=out_specs,
        compiler_params=_cparams(("parallel",), vmem),
        name="inproj_lat" if rope else "inproj_ctx",
    )(*args)


def _attn_kernel(*refs, cached):
    if cached:
        q_ref, k_ref, v_ref, ck_ref, cv_ref, o_ref = refs
    else:
        q_ref, k_ref, v_ref, o_ref = refs
    tq = q_ref.shape[0]
    grp = N_HEADS // N_KV_HEADS
    for kv in range(N_KV_HEADS):
        kk = k_ref[0, 0, kv].astype(BF16)
        vv = v_ref[0, 0, kv].astype(BF16)
        if cached:
            kk = jnp.concatenate([ck_ref[0, 0, kv].astype(BF16), kk], axis=0)
            vv = jnp.concatenate([cv_ref[0, 0, kv].astype(BF16), vv], axis=0)
        heads = [q_ref[:, (kv * grp + g) * HEAD_DIM:(kv * grp + g + 1) * HEAD_DIM] for g in range(grp)]
        q4 = jnp.concatenate(heads, axis=0)
        s = _dot_nt(q4, kk)
        p = jnp.exp(s - jnp.max(s, axis=-1, keepdims=True))
        l = jnp.sum(p, axis=-1, keepdims=True)
        o = _dot(p.astype(BF16), vv) / l
        for g in range(grp):
            hd = kv * grp + g
            o_ref[:, hd * HEAD_DIM:(hd + 1) * HEAD_DIM] = o[g * tq:(g + 1) * tq].astype(BF16)


def _attn_call(q_a, k_a, v_a, cache_k, cache_v):
    batch, _, _, seq, _ = k_a.shape
    n, aq = q_a.shape
    tq = ATTN_Q_TILE
    per = seq // tq
    cached = cache_k is not None
    own = pl.BlockSpec((1, 1, N_KV_HEADS, seq, HEAD_DIM), lambda b, j: (b, 0, 0, 0, 0))
    in_specs = [pl.BlockSpec((tq, aq), lambda b, j: (b * per + j, 0)), own, own]
    args = [q_a, k_a, v_a]
    klen = seq
    if cached:
        past = cache_k.shape[3]
        cspec = pl.BlockSpec((1, 1, N_KV_HEADS, past, HEAD_DIM), lambda b, j: (b, 0, 0, 0, 0))
        in_specs += [cspec, cspec]
        args += [cache_k, cache_v]
        klen += past
    grp = N_HEADS // N_KV_HEADS
    vmem = 4 * grp * tq * klen * 4 + 8 * N_KV_HEADS * klen * HEAD_DIM * 4 + 8 * tq * aq * 2
    return pl.pallas_call(
        functools.partial(_attn_kernel, cached=cached),
        out_shape=jax.ShapeDtypeStruct((n, aq), BF16),
        grid=(batch, per),
        in_specs=in_specs,
        out_specs=pl.BlockSpec((tq, aq), lambda b, j: (b * per + j, 0)),
        compiler_params=_cparams(("parallel", "parallel"), vmem),
        name="attn_lat" if cached else "attn_ctx",
    )(*args)


def _gla_kernel(*refs, nblk, heads, has_state, emit_state):
    refs = list(refs)
    q_ref, k_ref, v_ref, lgf_ref, lgb_ref, rg_ref, gg_ref = refs[:7]
    pos = 7
    if has_state:
        s0f_ref, s0b_ref = refs[pos:pos + 2]
        pos += 2
    og_ref = refs[pos]
    pos += 1
    if emit_state:
        sf_ref, sb_ref = refs[pos:pos + 2]
        pos += 2
    of_scr, ob_scr = refs[pos:pos + 2]

    blk = GLA_BLOCK
    ch = GLA_CHUNK
    nch = blk // ch
    dk = q_ref.shape[1] // heads
    dv = v_ref.shape[1] // heads
    shift = ch.bit_length() - 1
    row_in_chunk = lax.broadcasted_iota(I32, (blk, dk), 0) & (ch - 1)
    ri = lax.broadcasted_iota(I32, (blk, blk), 0)
    ci = lax.broadcasted_iota(I32, (blk, blk), 1)
    same = (ri >> shift) == (ci >> shift)
    col_chunk = lax.broadcasted_iota(I32, (dk, blk), 1) >> shift

    def one_block(b0, hd, state, reverse):
        rows = pl.ds(b0, blk)
        kcols = slice(hd * dk, (hd + 1) * dk)
        q = q_ref[rows, kcols].astype(F32)
        k = k_ref[rows, kcols].astype(F32)
        v = v_ref[rows, hd * dv:(hd + 1) * dv]
        b = (lgb_ref if reverse else lgf_ref)[rows, kcols]
        s = 1
        while s < ch:
            if reverse:
                sh = pltpu.roll(b, blk - s, axis=0)
                b = b + jnp.where(row_in_chunk < ch - s, sh, 0.0)
            else:
                sh = pltpu.roll(b, s, axis=0)
                b = b + jnp.where(row_in_chunk >= s, sh, 0.0)
            s *= 2
        qe = (q * jnp.exp(b)).astype(BF16)
        ke = (k * jnp.exp(-b)).astype(BF16)
        tri = (ci >= ri) if reverse else (ci <= ri)
        a = jnp.where(same & tri, _dot_nt(qe, ke), 0.0).astype(BF16)

        order = range(nch - 1, -1, -1) if reverse else range(nch)
        end_row = [c * ch + (0 if reverse else ch - 1) for c in range(nch)]
        ends = [b[r:r + 1, :] for r in end_row]
        b_end = jnp.concatenate([jnp.broadcast_to(e, (ch, dk)) for e in ends], axis=0)
        kdt = (k * jnp.exp(b_end - b)).T
        stacked = jnp.concatenate(
            [a] + [jnp.where(col_chunk == c, kdt, 0.0).astype(BF16) for c in range(nch)], axis=0)
        big = _dot(stacked, v)
        decay = jnp.exp(jnp.concatenate(ends + [jnp.zeros((dk - nch, dk), F32)], axis=0)).T
        inter = [None] * nch
        for c in order:
            inter[c] = _dot(qe[c * ch:(c + 1) * ch], state.astype(BF16))
            state = decay[:, c:c + 1] * state + big[blk + c * dk:blk + (c + 1) * dk]
        return big[0:blk] + jnp.concatenate(inter, axis=0), state

    for hd in range(heads):
        vcols = slice(hd * dv, (hd + 1) * dv)
        sf = s0f_ref[0, 0, hd] if has_state else jnp.zeros((dk, dv), F32)
        sb = s0b_ref[0, 0, hd] if has_state else jnp.zeros((dk, dv), F32)
        for i in range(nblk):
            j = nblk - 1 - i
            o_f, sf = one_block(i * blk, hd, sf, False)
            o_b, sb = one_block(j * blk, hd, sb, True)
            of_scr[pl.ds(i * blk, blk), vcols] = o_f
            ob_scr[pl.ds(j * blk, blk), vcols] = o_b
        if emit_state:
            sf_ref[0, 0, hd] = sf
            sb_ref[0, 0, hd] = sb
        o = of_scr[:, vcols] + ob_scr[:, vcols]
        og_ref[:, vcols] = (_rms(o, gg_ref[...]) * rg_ref[:, vcols].astype(F32)).astype(BF16)


def _gla_call(q_g, k_g, v_g, lg_f, lg_b, r_g, g_gla, state_f, state_b, batch, seq, emit_state, heads):
    n, gkw = q_g.shape
    d = v_g.shape[1]
    dk, dv = gkw // GLA_HEADS, d // GLA_HEADS
    has_state = state_f is not None
    kspec = pl.BlockSpec((seq, heads * dk), lambda b, h: (b, h))
    vspec = pl.BlockSpec((seq, heads * dv), lambda b, h: (b, h))
    sspec = pl.BlockSpec((1, 1, heads, dk, dv), lambda b, h: (b, 0, h, 0, 0))
    in_specs = [kspec, kspec, vspec, kspec, kspec, vspec, pl.BlockSpec((1, dv), lambda b, h: (0, 0))]
    args = [q_g, k_g, v_g, lg_f, lg_b, r_g, g_gla]
    if has_state:
        in_specs += [sspec, sspec]
        args += [state_f, state_b]
    out_shape = [jax.ShapeDtypeStruct((n, d), BF16)]
    out_specs = [vspec]
    if emit_state:
        st = jax.ShapeDtypeStruct((batch, 1, GLA_HEADS, dk, dv), F32)
        out_shape += [st, st]
        out_specs += [sspec, sspec]
    vmem = 12 * seq * heads * dv * 4 + 40 * GLA_BLOCK * GLA_BLOCK * 4
    return pl.pallas_call(
        functools.partial(_gla_kernel, nblk=seq // GLA_BLOCK, heads=heads, has_state=has_state,
                          emit_state=emit_state),
        out_shape=tuple(out_shape),
        grid=(batch, GLA_HEADS // heads),
        in_specs=in_specs,
        out_specs=tuple(out_specs),
        scratch_shapes=[pltpu.VMEM((seq, heads * dv), F32), pltpu.VMEM((seq, heads * dv), F32)],
        compiler_params=_cparams(("parallel", "parallel"), vmem),
        name="gla_ctx" if emit_state else "gla_lat",
    )(*args)


def _outproj_kernel(oa_ref, og_ref, gate_ref, x_ref, mod_ref, wpa_ref, wpg_ref, wout_ref, gpm_ref,
                    gpf_ref, wr_ref, x1_ref, h_ref, aff_ref):
    d = x_ref.shape[1]
    m = mod_ref[0]
    oa = _dot(oa_ref[...], wpa_ref[...])
    og = _dot(og_ref[...], wpg_ref[...])
    mix = gate_ref[:, 0:d].astype(F32) * oa + gate_ref[:, d:2 * d].astype(F32) * og
    mo = _dot(mix.astype(BF16), wout_ref[...])
    x1 = x_ref[...] + m[:, 2 * d:3 * d] * _rms(mo, gpm_ref[...])
    x1_ref[...] = x1
    hb = (_rms(x1, gpf_ref[...]) * (1.0 + m[:, 4 * d:5 * d]) + m[:, 3 * d:4 * d]).astype(BF16)
    h_ref[...] = hb
    logits = _dot(hb, wr_ref[...])
    valid = lax.broadcasted_iota(I32, logits.shape, 1) < N_EXPERTS
    mx = jnp.max(jnp.where(valid, logits, -jnp.inf), axis=-1, keepdims=True)
    ex = jnp.where(valid, jnp.exp(logits - mx), 0.0)
    aff_ref[...] = ex / jnp.sum(ex, axis=-1, keepdims=True)


def _outproj_call(o_a, o_g, gates, x2, mod3, mod_row_of_tile, w_pa, w_pg, w_out, g_pm, g_pf, w_r, tag):
    n, d = x2.shape
    tm = OUTPROJ_TILE
    row = lambda i: (i, 0)
    const = lambda i: (0, 0)
    wspec = pl.BlockSpec((d, d), const, pipeline_mode=pl.Buffered(1))
    vmem = 3 * d * d * 2 + 2 * tm * d * (2 + 2 + 4 + 4 + 4 + 2) + 8 * tm * d * 4
    return pl.pallas_call(
        _outproj_kernel,
        out_shape=(jax.ShapeDtypeStruct((n, d), F32), jax.ShapeDtypeStruct((n, d), BF16),
                   jax.ShapeDtypeStruct((n, V7X_LANES), F32)),
        grid=(n // tm,),
        in_specs=[pl.BlockSpec((tm, d), row), pl.BlockSpec((tm, d), row), pl.BlockSpec((tm, 2 * d), row),
                  pl.BlockSpec((tm, d), row),
                  pl.BlockSpec((1, 1, 6 * d), lambda i: (mod_row_of_tile(i, tm), 0, 0)),
                  wspec, wspec, wspec, pl.BlockSpec((1, d), const), pl.BlockSpec((1, d), const),
                  pl.BlockSpec((d, V7X_LANES), const)],
        out_specs=(pl.BlockSpec((tm, d), row), pl.BlockSpec((tm, d), row),
                   pl.BlockSpec((tm, V7X_LANES), row)),
        compiler_params=_cparams(("parallel",), vmem),
        name="outproj_" + tag,
    )(o_a, o_g, gates, x2, mod3, w_pa, w_pg, w_out, g_pm, g_pf, w_r)


def _route_kernel(aff_ref, pos_ref, post_ref, tbl_ref, bnd_ref, afft_scr, *, cap, n_row_tiles):
    n = aff_ref.shape[0]
    rb = ROUTE_BLOCK
    nb = n // rb
    lanes = aff_ref.shape[1]
    lane = lax.broadcasted_iota(I32, (1, lanes), 1)
    expert_lane = lane < N_EXPERTS
    tbl_ref[...] = jnp.zeros(tbl_ref.shape, I32)
    bnd_ref[...] = jnp.zeros(bnd_ref.shape, I32)

    def to_token_lanes(c, carry):
        start = pl.multiple_of(c * rb, rb)
        afft_scr[c] = aff_ref[pl.ds(start, rb), :].T[0:N_EXPERTS, :]
        return carry

    lax.fori_loop(0, nb, to_token_lanes, 0)
    aff_t = afft_scr[...]

    def count(hit):
        return jnp.sum(jnp.sum(hit.astype(I32), axis=0), axis=1, keepdims=True)

    def bit_step(i, lo):
        t = lo | jnp.left_shift(jnp.int32(1), 30 - i)
        ge = aff_t >= lax.bitcast_convert_type(t, F32)[None]
        return jnp.where(count(ge) >= cap, t, lo)

    thr_bits = lax.fori_loop(0, 31, bit_step, jnp.zeros((N_EXPERTS, 1), I32))
    need_t = cap - count(aff_t > lax.bitcast_convert_type(thr_bits, F32)[None])

    def to_expert_lanes(col):
        full = jnp.concatenate([jnp.broadcast_to(col, (N_EXPERTS, lanes)),
                                jnp.zeros((lanes - N_EXPERTS, lanes), I32)], axis=0)
        return full.T[0:1, :]

    thr = lax.bitcast_convert_type(to_expert_lanes(thr_bits), F32)
    need = to_expert_lanes(need_t).astype(F32)
    capf = float(cap)

    r = lax.broadcasted_iota(I32, (rb, rb), 0)
    c_ = lax.broadcasted_iota(I32, (rb, rb), 1)
    tril = jnp.where(c_ <= r, 1.0, 0.0).astype(BF16)

    def blk_step(c, carry):
        eq_before, raw_before, lo_cnt, hi_cnt = carry
        start = pl.multiple_of(c * rb, rb)
        a = aff_ref[pl.ds(start, rb), :]
        gt = a > thr
        eq = a == thr
        eq_incl = _dot(tril, jnp.where(eq, 1.0, 0.0).astype(BF16)) + eq_before
        raw = (gt | (eq & (eq_incl <= need))) & expert_lane
        raw_incl = _dot(tril, jnp.where(raw, 1.0, 0.0).astype(BF16)) + raw_before
        sel = raw & (raw_incl <= capf)
        self_ = jnp.where(sel, 1.0, 0.0)
        incl = jnp.minimum(raw_incl, capf)
        sel_before = jnp.minimum(raw_before, capf)
        excl = incl - self_
        posb = jnp.where(sel, excl, -1.0).astype(I32)
        pos_ref[pl.ds(start, rb), :] = posb
        post_ref[c] = posb.T[0:N_EXPERTS, :]
        tbl_ref[pl.ds(c, 1), :] = sel_before.astype(I32)
        lo_new, hi_new = [], []
        for j in range(n_row_tiles):
            lo_new.append(lo_cnt[j] + jnp.sum(jnp.where(incl <= j * FFN_ROW_TILE, 1.0, 0.0), axis=0, keepdims=True))
            hi_new.append(hi_cnt[j] + jnp.sum(jnp.where(excl < (j + 1) * FFN_ROW_TILE, 1.0, 0.0), axis=0, keepdims=True))
        return (eq_incl[rb - 1:rb, :], raw_incl[rb - 1:rb, :], tuple(lo_new), tuple(hi_new))

    zero = jnp.zeros((1, lanes), F32)
    zeros_j = tuple(zero for _ in range(n_row_tiles))
    _, total, lo_cnt, hi_cnt = lax.fori_loop(0, nb, blk_step, (zero, zero, zeros_j, zeros_j))
    tbl_ref[pl.ds(nb, 1), :] = jnp.minimum(total, capf).astype(I32)
    for j in range(n_row_tiles):
        bnd_ref[pl.ds(j, 1), :] = lo_cnt[j].astype(I32)
        bnd_ref[pl.ds(n_row_tiles + j, 1), :] = hi_cnt[j].astype(I32)


def _route_call(aff, cap, tag):
    n, lanes = aff.shape
    nb = n // ROUTE_BLOCK
    n_row_tiles = cap // FFN_ROW_TILE
    tbl_rows = -(-(nb + 1) // 8) * 8
    bnd_rows = -(-(2 * n_row_tiles) // 8) * 8
    full = lambda *shape: pl.BlockSpec(shape, lambda: tuple(0 for _ in shape))
    return pl.pallas_call(
        functools.partial(_route_kernel, cap=cap, n_row_tiles=n_row_tiles),
        out_shape=(jax.ShapeDtypeStruct((n, lanes), I32),
                   jax.ShapeDtypeStruct((nb, N_EXPERTS, ROUTE_BLOCK), I32),
                   jax.ShapeDtypeStruct((tbl_rows, lanes), I32),
                   jax.ShapeDtypeStruct((bnd_rows, lanes), I32)),
        in_specs=[full(n, lanes)],
        out_specs=(full(n, lanes), full(nb, N_EXPERTS, ROUTE_BLOCK), full(tbl_rows, lanes),
                   full(bnd_rows, lanes)),
        scratch_shapes=[pltpu.VMEM((nb, N_EXPERTS, ROUTE_BLOCK), F32)],
        compiler_params=_cparams((), 8 * n * lanes * 4),
        name="route_" + tag,
    )(aff)


def _ffn_kernel(*refs, caps):
    ng = len(caps)
    bnd_refs = refs[:ng]
    h_refs = refs[ng:3 * ng:2]
    post_refs = refs[ng + 1:3 * ng:2]
    w1_ref, w3_ref, w2_ref = refs[3 * ng:3 * ng + 3]
    ye_refs = refs[3 * ng + 3:4 * ng + 3]
    xs_scr, acc_scr = refs[4 * ng + 3:]
    row_off = [sum(caps[:g]) for g in range(ng)]
    e = pl.program_id(0)
    f = pl.program_id(1)
    tr = FFN_ROW_TILE
    kc = FFN_GATHER_TOKENS
    per = kc // ROUTE_BLOCK

    @pl.when(f == 0)
    def _gather():
        rid = lax.broadcasted_iota(I32, (tr, kc), 0)
        for g in range(ng):
            n_row_tiles = caps[g] // tr
            for j in range(n_row_tiles):
                lo = bnd_refs[g][j, e]
                hi = bnd_refs[g][n_row_tiles + j, e]
                rows = pl.ds(row_off[g] + j * tr, tr)
                acc_scr[rows, :] = jnp.zeros((tr, acc_scr.shape[1]), F32)

                def chunk(c, carry, g=g, j=j, rows=rows):
                    p = jnp.concatenate([post_refs[g][per * c + s] for s in range(per)], axis=1)
                    onehot = jnp.where(p == rid + j * tr, 1.0, 0.0).astype(BF16)
                    start = pl.multiple_of(c * kc, kc)
                    acc_scr[rows, :] += _dot(onehot, h_refs[g][pl.ds(start, kc), :])
                    return carry

                lax.fori_loop(lo // kc, (hi + kc - 1) // kc, chunk, 0)
                xs_scr[rows, :] = acc_scr[rows, :].astype(BF16)

    xs = xs_scr[...]
    hid = _silu(_dot(xs, w1_ref[...].astype(BF16))) * _dot(xs, w3_ref[...].astype(BF16))
    part = _dot(hid.astype(BF16), w2_ref[...].astype(BF16))

    @pl.when(f == 0)
    def _first():
        acc_scr[...] = part

    @pl.when(f > 0)
    def _rest():
        acc_scr[...] += part

    @pl.when(f == pl.num_programs(1) - 1)
    def _emit():
        for g in range(ng):
            ye_refs[g][...] = acc_scr[pl.ds(row_off[g], caps[g]), :].astype(BF16)


def _ffn_call(groups, w1, w3, w2):
    ng = len(groups)
    caps = tuple(g[3] for g in groups)
    d = groups[0][1].shape[1]
    n_exp, _, dff = w1.shape
    tf = dff // FFN_F_TILES
    rows = sum(caps)
    idx = lambda fn: (lambda e, f, *_: fn(e, f))
    in_specs, args, h_bytes = [], [], 0
    for bnd, h, post4, cap in groups:
        n = h.shape[0]
        in_specs += [pl.BlockSpec((n, d), idx(lambda e, f: (0, 0)), pipeline_mode=pl.Buffered(1)),
                     pl.BlockSpec((n // ROUTE_BLOCK, None, 1, ROUTE_BLOCK), idx(lambda e, f: (0, e, 0, 0)))]
        args += [h, post4]
        h_bytes += n * d * 2
    in_specs += [pl.BlockSpec((None, d, tf), idx(lambda e, f: (e, 0, f))),
                 pl.BlockSpec((None, d, tf), idx(lambda e, f: (e, 0, f))),
                 pl.BlockSpec((None, tf, d), idx(lambda e, f: (e, f, 0)))]
    grid_spec = pltpu.PrefetchScalarGridSpec(
        num_scalar_prefetch=ng,
        grid=(n_exp, FFN_F_TILES),
        in_specs=in_specs,
        out_specs=tuple(pl.BlockSpec((cap, d), idx(lambda e, f: (e, 0))) for cap in caps),
        scratch_shapes=[pltpu.VMEM((rows, d), BF16), pltpu.VMEM((rows, d), F32)],
    )
    vmem = h_bytes + 3 * d * tf * (2 * 4 + 2) + rows * d * (2 + 4 + 2 * 2 + 4) + 4 * rows * tf * 4
    return pl.pallas_call(
        functools.partial(_ffn_kernel, caps=caps),
        out_shape=tuple(jax.ShapeDtypeStruct((n_exp * cap, d), BF16) for cap in caps),
        grid_spec=grid_spec,
        compiler_params=_cparams(("arbitrary", "arbitrary"), vmem),
        name="ffn",
    )(*[g[0] for g in groups], *args, w1, w3, w2)


def _combine_kernel(tbl_ref, ye_hbm, pos_ref, aff_ref, x1_ref, mod_ref, gpo_ref, y_ref, buf, sem, xbuf, xsem,
                    acc_scr, *, cap, blocks_per_tile):
    i = pl.program_id(0)
    nsteps = pl.num_programs(0)
    d = x1_ref.shape[1]
    lanes = pos_ref.shape[1]
    win = COMBINE_WINDOW
    last_start = ye_hbm.shape[0] - win
    slot = i % 2

    def first_row(step, e):
        return tbl_ref[step * blocks_per_tile, e] + e * cap

    def window_start(first, k):
        unclamped = (first // BF16_ROWS_PER_TILE) * BF16_ROWS_PER_TILE + k * win
        return unclamped, jnp.minimum(unclamped, last_start)

    def fetch(step, to_slot, e):
        start = window_start(first_row(step, e), 0)[1]
        return pltpu.make_async_copy(ye_hbm.at[pl.ds(pl.multiple_of(start, BF16_ROWS_PER_TILE), win), :],
                                     buf.at[to_slot, pl.ds(e * win, win), :], sem.at[to_slot, e])

    @pl.when(i == 0)
    def _prime():
        for e in range(N_EXPERTS):
            fetch(0, 0, e).start()

    @pl.when(i + 1 < nsteps)
    def _ahead():
        for e in range(N_EXPERTS):
            fetch(i + 1, 1 - slot, e).start()

    lane_row = lax.broadcasted_iota(I32, (1, win), 1)
    pieces = []
    for e in range(N_EXPERTS):
        pcol = pos_ref[:, e:e + 1]
        grow = jnp.where(pcol >= 0, pcol + e * cap, -1)
        start = window_start(first_row(i, e), 0)[1]
        pieces.append(jnp.where(grow == start + lane_row, aff_ref[:, e:e + 1], 0.0).astype(BF16))
    onehot = jnp.concatenate(pieces, axis=1)
    for e in range(N_EXPERTS):
        fetch(i, slot, e).wait()
    acc_scr[...] = _dot(onehot, buf[slot])

    def extra_windows(e):
        covered = window_start(first_row(i, e), 1)[0]
        return jnp.maximum(first_row(i + 1, e) - covered + win - 1, 0) // win

    def expert_extra(e, carry):
        first = first_row(i, e)
        extra = extra_windows(e)

        def more(k, c):
            unclamped, start = window_start(first, k)
            cp = pltpu.make_async_copy(ye_hbm.at[pl.ds(pl.multiple_of(start, BF16_ROWS_PER_TILE), win), :],
                                       xbuf, xsem)
            cp.start()
            cp.wait()
            at_e = lax.broadcasted_iota(I32, (1, lanes), 1) == e
            pcol = jnp.sum(jnp.where(at_e, pos_ref[...].astype(F32), 0.0), axis=1, keepdims=True).astype(I32)
            wcol = jnp.sum(jnp.where(at_e, aff_ref[...], 0.0), axis=1, keepdims=True)
            grow = jnp.where(pcol >= 0, pcol + e * cap, -1)
            hit = (grow == start + lane_row) & (grow >= unclamped)
            acc_scr[...] += _dot(jnp.where(hit, wcol, 0.0).astype(BF16), xbuf[...])
            return c

        lax.fori_loop(1, 1 + extra, more, 0)
        return carry

    any_extra = extra_windows(0)
    for e in range(1, N_EXPERTS):
        any_extra = any_extra + extra_windows(e)

    @pl.when(any_extra > 0)
    def _overflow():
        lax.fori_loop(0, N_EXPERTS, expert_extra, 0)

    m = mod_ref[0]
    y_ref[...] = x1_ref[...] + m[:, 5 * d:6 * d] * _rms(acc_scr[...], gpo_ref[...])


def _combine_call(tbl, ye, pos, aff, x1, mod3, mod_row_of_tile, g_po, cap, tag):
    n, d = x1.shape
    tm = COMBINE_TILE
    lanes = pos.shape[1]
    grid_spec = pltpu.PrefetchScalarGridSpec(
        num_scalar_prefetch=1,
        grid=(n // tm,),
        in_specs=[pl.BlockSpec(memory_space=pl.ANY),
                  pl.BlockSpec((tm, lanes), lambda i, t: (i, 0)),
                  pl.BlockSpec((tm, lanes), lambda i, t: (i, 0)),
                  pl.BlockSpec((tm, d), lambda i, t: (i, 0)),
                  pl.BlockSpec((1, 1, 6 * d), lambda i, t: (mod_row_of_tile(i, tm), 0, 0)),
                  pl.BlockSpec((1, d), lambda i, t: (0, 0))],
        out_specs=pl.BlockSpec((tm, d), lambda i, t: (i, 0)),
        scratch_shapes=[pltpu.VMEM((2, N_EXPERTS * COMBINE_WINDOW, d), BF16),
                        pltpu.SemaphoreType.DMA((2, N_EXPERTS)),
                        pltpu.VMEM((COMBINE_WINDOW, d), BF16),
                        pltpu.SemaphoreType.DMA(()),
                        pltpu.VMEM((tm, d), F32)],
    )
    vmem = 2 * N_EXPERTS * COMBINE_WINDOW * d * 2 + 10 * tm * d * 4 + 2 * tm * N_EXPERTS * COMBINE_WINDOW * 4
    return pl.pallas_call(
        functools.partial(_combine_kernel, cap=cap, blocks_per_tile=tm // ROUTE_BLOCK),
        out_shape=jax.ShapeDtypeStruct((n, d), F32),
        grid_spec=grid_spec,
        compiler_params=_cparams(("arbitrary",), vmem),
        name="combine_" + tag,
    )(tbl, ye, pos, aff, x1, mod3, g_po)


def _rope_tables(seq):
    rows = seq // GRID_W
    r = jnp.repeat(jnp.arange(rows), GRID_W).astype(F32)
    col = jnp.tile(jnp.arange(GRID_W), rows).astype(F32)
    pairs = HEAD_DIM // 4
    freqs = ROPE_THETA ** (-jnp.arange(pairs, dtype=F32) / pairs)
    ang = jnp.concatenate([r[:, None] * freqs, col[:, None] * freqs], axis=-1)
    cos = jnp.repeat(jnp.cos(ang), 2, axis=-1)
    sin = jnp.repeat(jnp.sin(ang), 2, axis=-1)
    even = (jnp.arange(HEAD_DIM) % 2) == 0
    return cos, jnp.where(even, -sin, 0.0), jnp.where(even, 0.0, sin)


def _trunk_to_routing(x, mod3, mod_row_of_tile, rope_tabs, ctx, lw, tag):
    (g_pre_mix, g_post_mix, g_pre_ffn, g_post_ffn, w_in_p, g_q, g_k, wgf, bgf, wgb, bgb, g_gla,
     w_pa, w_pg, w_out, w_r, w1, w3, w2) = lw
    batch, seq, d = x.shape
    n = batch * seq
    x2 = x.reshape(n, d)
    (q_a, k_a, v_a, q_g, k_g, v_g, r_g, lg_f, lg_b, gates) = _inproj_call(
        x2, mod3, mod_row_of_tile, g_pre_mix, w_in_p, g_q, g_k, wgf, bgf, wgb, bgb, rope_tabs, batch, seq)
    if ctx is None:
        o_a = _attn_call(q_a, k_a, v_a, None, None)
        o_g, s_f, s_b = _gla_call(q_g, k_g, v_g, lg_f, lg_b, r_g, g_gla, None, None, batch, seq, True,
                                  GLA_CTX_HEADS_PER_STEP)
    else:
        ck, cv, s_f0, s_b0 = ctx
        o_a = _attn_call(q_a, k_a, v_a, ck, cv)
        (o_g,) = _gla_call(q_g, k_g, v_g, lg_f, lg_b, r_g, g_gla, s_f0, s_b0, batch, seq, False,
                           GLA_LAT_HEADS_PER_STEP)
        s_f = s_b = None
    x1, h, aff = _outproj_call(o_a, o_g, gates, x2, mod3, mod_row_of_tile, w_pa, w_pg, w_out,
                               g_post_mix, g_pre_ffn, w_r, tag)
    cap = (EC_CAPACITY_FACTOR * n) // N_EXPERTS
    pos, post, tbl, bnd = _route_call(aff, cap, tag)
    post4 = post.reshape(post.shape[0], N_EXPERTS, 1, ROUTE_BLOCK)
    return dict(x1=x1, h=h, aff=aff, pos=pos, post4=post4, tbl=tbl, bnd=bnd, cap=cap,
                mod_row_of_tile=mod_row_of_tile, tag=tag, shape=(batch, seq, d)), (k_a, v_a, s_f, s_b)


def _expert_ffn(groups, mod3, g_post_ffn, w1, w3, w2):
    yes = _ffn_call([(g["bnd"], g["h"], g["post4"], g["cap"]) for g in groups], w1, w3, w2)
    outs = []
    for g, ye in zip(groups, yes):
        y = _combine_call(g["tbl"], ye, g["pos"], g["aff"], g["x1"], mod3, g["mod_row_of_tile"], g_post_ffn,
                          g["cap"], g["tag"])
        outs.append(y.reshape(g["shape"]))
    return outs


def kernel(x_prompt, x_sample, cache_k, cache_v, state_gla_fwd, state_gla_bwd, c, c_ctx, g_pre_mix, g_post_mix, g_pre_ffn, g_post_ffn, w_mod, b_mod, w_in, g_q, g_k, w_gk2_f, b_gk_f, w_gk2_b, b_gk_b, g_gla, w_pa, w_pg, w_out, w_router, w1, w3, w2):
    depth = w_in.shape[0]
    assert depth == 1, "single trunk layer"
    d = x_prompt.shape[-1]
    dec_batch, dec_seq, _ = x_sample.shape
    assert dec_batch + 1 <= MOD_ROWS
    l = 0
    rank = GLA_GATE_RANK
    w_in_p = jnp.swapaxes(w_in[l], 0, 1).astype(BF16)
    gkw = w_gk2_f.shape[-1]
    wgf = jnp.zeros((V7X_LANES, gkw), F32).at[0:rank].set(w_gk2_f[l]).astype(BF16)
    wgb = jnp.zeros((V7X_LANES, gkw), F32).at[rank:2 * rank].set(w_gk2_b[l]).astype(BF16)
    w_r = jnp.zeros((d, V7X_LANES), F32).at[:, :N_EXPERTS].set(w_router[l]).astype(BF16)
    row = lambda a: a[l].reshape(1, -1)
    lw = (row(g_pre_mix), row(g_post_mix), row(g_pre_ffn), row(g_post_ffn), w_in_p, row(g_q), row(g_k),
          wgf, row(b_gk_f), wgb, row(b_gk_b), row(g_gla),
          w_pa[l].astype(BF16), w_pg[l].astype(BF16), w_out[l].astype(BF16), w_r, w1[l], w3[l], w2[l])

    cc = jnp.concatenate([c_ctx[None, :], c, jnp.zeros((MOD_ROWS - 1 - dec_batch, d), F32)], axis=0)
    mod = _mod_call(cc, w_mod[l], b_mod[l].reshape(1, -1))
    mod3 = mod.reshape(MOD_ROWS, 1, 6 * d)

    gp, (nk, nv, nsf, nsb) = _trunk_to_routing(x_prompt, mod3, lambda i, tm: 0, None, None, lw, "ctx")
    ctx = (cache_k, cache_v, state_gla_fwd, state_gla_bwd)
    gs, _ = _trunk_to_routing(x_sample, mod3, lambda i, tm: 1 + (i * tm) // dec_seq, _rope_tables(dec_seq),
                              ctx, lw, "lat")
    yp, ys = _expert_ffn([gp, gs], mod3, lw[3], lw[16], lw[17], lw[18])
    return (yp, ys, nk, nv, nsf, nsb)
```

```python
import functools

import jax
import jax.numpy as jnp
from jax import lax
from jax.experimental import pallas as pl
from jax.experimental.pallas import tpu as pltpu
from jax.experimental.pallas import tpu_sc as plsc

F32 = jnp.float32
BF16 = jnp.bfloat16
I32 = jnp.int32

N_HEADS = 8
N_KV_HEADS = 2
HEAD_DIM = 128
GRID_W = 64
ROPE_THETA = 10000.0
GLA_HEADS = 4
GLA_GATE_RANK = 16
GLA_GATE_NORM = 16.0
GLA_CHUNK = 64
N_EXPERTS = 16
EC_CAPACITY_FACTOR = 2
EPS = 1e-6

V7X_LANES = 128
V7X_VMEM_BYTES = 64 * 1024 * 1024
V7X_VMEM_RESERVE_BYTES = 6 * 1024 * 1024
BF16_ROWS_PER_TILE = 16
V7X_SC_CORES = 2
V7X_SC_SUBCORES = 16

TOKEN_TILE = 512
OUTPROJ_TILE = 512
ATTN_Q_TILE = 256
GLA_BLOCK = 256
GLA_CTX_HEADS_PER_STEP = 4
GLA_LAT_HEADS_PER_STEP = 1
ROUTE_BLOCK = 256
FFN_ROW_TILE = 256
FFN_F_TILES = 2
SC_GATHER_CHUNK = 64
COMBINE_TILE = 512
COMBINE_WINDOW = 128
MOD_ROWS = 8
MOD_N_TILE = 1536


def _cparams(semantics, vmem_bytes):
    limit = min(max(int(vmem_bytes), 32 * 1024 * 1024), V7X_VMEM_BYTES - V7X_VMEM_RESERVE_BYTES)
    return pltpu.CompilerParams(dimension_semantics=semantics, vmem_limit_bytes=limit)


def _sigmoid(x):
    return 1.0 / (1.0 + jnp.exp(-x))


def _silu(x):
    return x * _sigmoid(x)


def _log_sigmoid(x):
    return jnp.minimum(x, 0.0) - jnp.log1p(jnp.exp(-jnp.abs(x)))


def _rms(x, g):
    ms = jnp.mean(x * x, axis=-1, keepdims=True)
    return x * lax.rsqrt(ms + EPS) * g


def _dot(a, b):
    return jnp.dot(a, b, preferred_element_type=F32)


def _dot_nt(a, b):
    return lax.dot_general(a, b, (((1,), (1,)), ((), ())), preferred_element_type=F32)


def _mod_kernel(c_ref, w_ref, b_ref, o_ref):
    s = _silu(c_ref[...]).astype(BF16)
    o_ref[...] = _dot(s, w_ref[...].astype(BF16)) + b_ref[...]


def _mod_call(cc, w_mod, b_mod):
    d, n6 = w_mod.shape
    tn = MOD_N_TILE
    return pl.pallas_call(
        _mod_kernel,
        out_shape=jax.ShapeDtypeStruct((MOD_ROWS, n6), F32),
        grid=(n6 // tn,),
        in_specs=[pl.BlockSpec((MOD_ROWS, d), lambda j: (0, 0)),
                  pl.BlockSpec((d, tn), lambda j: (0, j)),
                  pl.BlockSpec((1, tn), lambda j: (0, j))],
        out_specs=pl.BlockSpec((MOD_ROWS, tn), lambda j: (0, j)),
        compiler_params=_cparams(("arbitrary",), 3 * d * tn * 4),
        name="mod",
    )(cc, w_mod, b_mod)


def _inproj_layout(d):
    aq, akv = N_HEADS * HEAD_DIM, N_KV_HEADS * HEAD_DIM
    gk, gv = d // 2, d
    names = ("q_a", "k_a", "v_a", "q_g", "k_g", "v_g", "r_g", "gk_f", "gk_b", "gates")
    widths = (aq, akv, akv, gk, gk, gv, gv, GLA_GATE_RANK, GLA_GATE_RANK, 2 * d)
    off, o = {}, 0
    for nme, w in zip(names, widths):
        off[nme] = (o, o + w)
        o += w
    off["gk"] = (off["gk_f"][0], off["gk_f"][0] + V7X_LANES)
    return off, o


def _inproj_kernel(*refs, rope, d):
    if rope:
        (x_ref, mod_ref, gpre_ref, w_ref, gq_ref, gk_ref, wgf_ref, bgf_ref, wgb_ref, bgb_ref,
         cos_ref, se_ref, so_ref, *outs) = refs
    else:
        (x_ref, mod_ref, gpre_ref, w_ref, gq_ref, gk_ref, wgf_ref, bgf_ref, wgb_ref, bgb_ref,
         *outs) = refs
    qa_ref, k_ref, v_ref, qg_ref, kg_ref, vg_ref, rg_ref, lgf_ref, lgb_ref, gate_ref = outs
    off, _ = _inproj_layout(d)
    m = mod_ref[0]
    h = _rms(x_ref[...], gpre_ref[...]) * (1.0 + m[:, d:2 * d]) + m[:, 0:d]
    hb = h.astype(BF16)

    def proj(name):
        a, b = off[name]
        return _dot_nt(hb, w_ref[a:b, :])

    def qk_norm(y, g_ref):
        y = _rms(y, g_ref[...])
        if rope:
            nxt = pltpu.roll(y, HEAD_DIM - 1, axis=1)
            prv = pltpu.roll(y, 1, axis=1)
            y = y * cos_ref[...] + nxt * se_ref[...] + prv * so_ref[...]
        return y

    q = proj("q_a")
    scale = HEAD_DIM ** -0.5
    for hd in range(N_HEADS):
        sl = slice(hd * HEAD_DIM, (hd + 1) * HEAD_DIM)
        qa_ref[:, sl] = (qk_norm(q[:, sl], gq_ref) * scale).astype(BF16)

    k = proj("k_a")
    v = proj("v_a")
    tb, _, _, ts, _ = k_ref.shape
    for kv in range(N_KV_HEADS):
        sl = slice(kv * HEAD_DIM, (kv + 1) * HEAD_DIM)
        k_ref[:, 0, kv] = qk_norm(k[:, sl], gk_ref).reshape(tb, ts, HEAD_DIM)
        v_ref[:, 0, kv] = v[:, sl].reshape(tb, ts, HEAD_DIM)

    dk = (d // 2) // GLA_HEADS
    qg_ref[...] = (proj("q_g") * (dk ** -0.5)).astype(BF16)
    kg_ref[...] = proj("k_g").astype(BF16)
    vg_ref[...] = proj("v_g").astype(BF16)
    rg_ref[...] = _silu(proj("r_g")).astype(BF16)

    gk = proj("gk").astype(BF16)
    lgf_ref[...] = _log_sigmoid(_dot(gk, wgf_ref[...]) + bgf_ref[...]) * (1.0 / GLA_GATE_NORM)
    lgb_ref[...] = _log_sigmoid(_dot(gk, wgb_ref[...]) + bgb_ref[...]) * (1.0 / GLA_GATE_NORM)

    gate_ref[...] = _sigmoid(proj("gates")).astype(BF16)


def _inproj_call(x2, mod3, mod_row_of_tile, g_pre, w_in_p, g_q, g_k, wgf, bgf, wgb, bgb, rope_tabs,
                 batch, seq):
    n, d = x2.shape
    tm = TOKEN_TILE
    _, dinp = _inproj_layout(d)
    rope = rope_tabs is not None
    gk_w = d // 2
    if seq >= tm:
        tb, ts, per = 1, tm, seq // tm
        kv_map = lambda i: (i // per, 0, 0, i % per, 0)
    else:
        tb, ts, per = tm // seq, seq, 1
        kv_map = lambda i: (i, 0, 0, 0, 0)
    row = lambda i: (i, 0)
    const = lambda i: (0, 0)
    in_specs = [
        pl.BlockSpec((tm, d), row),
        pl.BlockSpec((1, 1, 6 * d), lambda i: (mod_row_of_tile(i, tm), 0, 0)),
        pl.BlockSpec((1, d), const),
        pl.BlockSpec((dinp, d), const, pipeline_mode=pl.Buffered(1)),
        pl.BlockSpec((1, HEAD_DIM), const),
        pl.BlockSpec((1, HEAD_DIM), const),
        pl.BlockSpec((V7X_LANES, gk_w), const),
        pl.BlockSpec((1, gk_w), const),
        pl.BlockSpec((V7X_LANES, gk_w), const),
        pl.BlockSpec((1, gk_w), const),
    ]
    args = [x2, mod3, g_pre, w_in_p, g_q, g_k, wgf, bgf, wgb, bgb]
    if rope:
        tab = pl.BlockSpec((tm, HEAD_DIM), lambda i: (i % per, 0))
        in_specs += [tab, tab, tab]
        args += list(rope_tabs)
    kv_shape = jax.ShapeDtypeStruct((batch, 1, N_KV_HEADS, seq, HEAD_DIM), F32)
    kv_spec = pl.BlockSpec((tb, 1, N_KV_HEADS, ts, HEAD_DIM), kv_map)
    out_shape = (
        jax.ShapeDtypeStruct((n, N_HEADS * HEAD_DIM), BF16), kv_shape, kv_shape,
        jax.ShapeDtypeStruct((n, gk_w), BF16), jax.ShapeDtypeStruct((n, gk_w), BF16),
        jax.ShapeDtypeStruct((n, d), BF16), jax.ShapeDtypeStruct((n, d), BF16),
        jax.ShapeDtypeStruct((n, gk_w), F32), jax.ShapeDtypeStruct((n, gk_w), F32),
        jax.ShapeDtypeStruct((n, 2 * d), BF16),
    )
    out_specs = (
        pl.BlockSpec((tm, N_HEADS * HEAD_DIM), row), kv_spec, kv_spec,
        pl.BlockSpec((tm, gk_w), row), pl.BlockSpec((tm, gk_w), row),
        pl.BlockSpec((tm, d), row), pl.BlockSpec((tm, d), row),
        pl.BlockSpec((tm, gk_w), row), pl.BlockSpec((tm, gk_w), row),
        pl.BlockSpec((tm, 2 * d), row),
    )
    out_row_bytes = 2 * (N_HEADS * HEAD_DIM + 2 * gk_w + 2 * d + 2 * d) + 4 * (4 * HEAD_DIM + 2 * gk_w)
    vmem = d * dinp * 2 + 2 * tm * (d * 4 + out_row_bytes) + 6 * tm * 2 * d * 4
    return pl.pallas_call(
        functools.partial(_inproj_kernel, rope=rope, d=d),
        out_shape=out_shape,
        grid=(n // tm,),
        in_specs=in_specs,
        out_specs=out_specs,
        compiler_params=_cparams(("parallel",), vmem),
        name="inproj_lat" if rope else "inproj_ctx",
    )(*args)


def _attn_kernel(*refs, cached):
    if cached:
        q_ref, k_ref, v_ref, ck_ref, cv_ref, o_ref = refs
    else:
        q_ref, k_ref, v_ref, o_ref = refs
    tq = q_ref.shape[0]
    grp = N_HEADS // N_KV_HEADS
    for kv in range(N_KV_HEADS):
        kk = k_ref[0, 0, kv].astype(BF16)
        vv = v_ref[0, 0, kv].astype(BF16)
        if cached:
            kk = jnp.concatenate([ck_ref[0, 0, kv].astype(BF16), kk], axis=0)
            vv = jnp.concatenate([cv_ref[0, 0, kv].astype(BF16), vv], axis=0)
        heads = [q_ref[:, (kv * grp + g) * HEAD_DIM:(kv * grp + g + 1) * HEAD_DIM] for g in range(grp)]
        q4 = jnp.concatenate(heads, axis=0)
        s = _dot_nt(q4, kk)
        p = jnp.exp(s - jnp.max(s, axis=-1, keepdims=True))
        l = jnp.sum(p, axis=-1, keepdims=True)
        o = _dot(p.astype(BF16), vv) / l
        for g in range(grp):
            hd = kv * grp + g
            o_ref[:, hd * HEAD_DIM:(hd + 1) * HEAD_DIM] = o[g * tq:(g + 1) * tq].astype(BF16)


def _attn_call(q_a, k_a, v_a, cache_k, cache_v):
    batch, _, _, seq, _ = k_a.shape
    n, aq = q_a.shape
    tq = ATTN_Q_TILE
    per = seq // tq
    cached = cache_k is not None
    own = pl.BlockSpec((1, 1, N_KV_HEADS, seq, HEAD_DIM), lambda b, j: (b, 0, 0, 0, 0))
    in_specs = [pl.BlockSpec((tq, aq), lambda b, j: (b * per + j, 0)), own, own]
    args = [q_a, k_a, v_a]
    klen = seq
    if cached:
        past = cache_k.shape[3]
        cspec = pl.BlockSpec((1, 1, N_KV_HEADS, past, HEAD_DIM), lambda b, j: (b, 0, 0, 0, 0))
        in_specs += [cspec, cspec]
        args += [cache_k, cache_v]
        klen += past
    grp = N_HEADS // N_KV_HEADS
    vmem = 4 * grp * tq * klen * 4 + 8 * N_KV_HEADS * klen * HEAD_DIM * 4 + 8 * tq * aq * 2
    return pl.pallas_call(
        functools.partial(_attn_kernel, cached=cached),
        out_shape=jax.ShapeDtypeStruct((n, aq), BF16),
        grid=(batch, per),
        in_specs=in_specs,
        out_specs=pl.BlockSpec((tq, aq), lambda b, j: (b * per + j, 0)),
        compiler_params=_cparams(("parallel", "parallel"), vmem),
        name="attn_lat" if cached else "attn_ctx",
    )(*args)


def _gla_kernel(*refs, nblk, heads, has_state, emit_state):
    refs = list(refs)
    q_ref, k_ref, v_ref, lgf_ref, lgb_ref, rg_ref, gg_ref = refs[:7]
    pos = 7
    if has_state:
        s0f_ref, s0b_ref = refs[pos:pos + 2]
        pos += 2
    og_ref = refs[pos]
    pos += 1
    if emit_state:
        sf_ref, sb_ref = refs[pos:pos + 2]
        pos += 2
    of_scr, ob_scr = refs[pos:pos + 2]

    blk = GLA_BLOCK
    ch = GLA_CHUNK
    nch = blk // ch
    dk = q_ref.shape[1] // heads
    dv = v_ref.shape[1] // heads
    shift = ch.bit_length() - 1
    row_in_chunk = lax.broadcasted_iota(I32, (blk, dk), 0) & (ch - 1)
    ri = lax.broadcasted_iota(I32, (blk, blk), 0)
    ci = lax.broadcasted_iota(I32, (blk, blk), 1)
    same = (ri >> shift) == (ci >> shift)
    col_chunk = lax.broadcasted_iota(I32, (dk, blk), 1) >> shift

    def one_block(b0, hd, state, reverse):
        rows = pl.ds(b0, blk)
        kcols = slice(hd * dk, (hd + 1) * dk)
        q = q_ref[rows, kcols].astype(F32)
        k = k_ref[rows, kcols].astype(F32)
        v = v_ref[rows, hd * dv:(hd + 1) * dv]
        b = (lgb_ref if reverse else lgf_ref)[rows, kcols]
        s = 1
        while s < ch:
            if reverse:
                sh = pltpu.roll(b, blk - s, axis=0)
                b = b + jnp.where(row_in_chunk < ch - s, sh, 0.0)
            else:
                sh = pltpu.roll(b, s, axis=0)
                b = b + jnp.where(row_in_chunk >= s, sh, 0.0)
            s *= 2
        qe = (q * jnp.exp(b)).astype(BF16)
        ke = (k * jnp.exp(-b)).astype(BF16)
        tri = (ci >= ri) if reverse else (ci <= ri)
        a = jnp.where(same & tri, _dot_nt(qe, ke), 0.0).astype(BF16)

        order = range(nch - 1, -1, -1) if reverse else range(nch)
        end_row = [c * ch + (0 if reverse else ch - 1) for c in range(nch)]
        ends = [b[r:r + 1, :] for r in end_row]
        b_end = jnp.concatenate([jnp.broadcast_to(e, (ch, dk)) for e in ends], axis=0)
        kdt = (k * jnp.exp(b_end - b)).T
        stacked = jnp.concatenate(
            [a] + [jnp.where(col_chunk == c, kdt, 0.0).astype(BF16) for c in range(nch)], axis=0)
        big = _dot(stacked, v)
        decay = jnp.exp(jnp.concatenate(ends + [jnp.zeros((dk - nch, dk), F32)], axis=0)).T
        inter = [None] * nch
        for c in order:
            inter[c] = _dot(qe[c * ch:(c + 1) * ch], state.astype(BF16))
            state = decay[:, c:c + 1] * state + big[blk + c * dk:blk + (c + 1) * dk]
        return big[0:blk] + jnp.concatenate(inter, axis=0), state

    for hd in range(heads):
        vcols = slice(hd * dv, (hd + 1) * dv)
        sf = s0f_ref[0, 0, hd] if has_state else jnp.zeros((dk, dv), F32)
        sb = s0b_ref[0, 0, hd] if has_state else jnp.zeros((dk, dv), F32)
        for i in range(nblk):
            j = nblk - 1 - i
            o_f, sf = one_block(i * blk, hd, sf, False)
            o_b, sb = one_block(j * blk, hd, sb, True)
            of_scr[pl.ds(i * blk, blk), vcols] = o_f
            ob_scr[pl.ds(j * blk, blk), vcols] = o_b
        if emit_state:
            sf_ref[0, 0, hd] = sf
            sb_ref[0, 0, hd] = sb
        o = of_scr[:, vcols] + ob_scr[:, vcols]
        og_ref[:, vcols] = (_rms(o, gg_ref[...]) * rg_ref[:, vcols].astype(F32)).astype(BF16)


def _gla_call(q_g, k_g, v_g, lg_f, lg_b, r_g, g_gla, state_f, state_b, batch, seq, emit_state, heads):
    n, gkw = q_g.shape
    d = v_g.shape[1]
    dk, dv = gkw // GLA_HEADS, d // GLA_HEADS
    has_state = state_f is not None
    kspec = pl.BlockSpec((seq, heads * dk), lambda b, h: (b, h))
    vspec = pl.BlockSpec((seq, heads * dv), lambda b, h: (b, h))
    sspec = pl.BlockSpec((1, 1, heads, dk, dv), lambda b, h: (b, 0, h, 0, 0))
    in_specs = [kspec, kspec, vspec, kspec, kspec, vspec, pl.BlockSpec((1, dv), lambda b, h: (0, 0))]
    args = [q_g, k_g, v_g, lg_f, lg_b, r_g, g_gla]
    if has_state:
        in_specs += [sspec, sspec]
        args += [state_f, state_b]
    out_shape = [jax.ShapeDtypeStruct((n, d), BF16)]
    out_specs = [vspec]
    if emit_state:
        st = jax.ShapeDtypeStruct((batch, 1, GLA_HEADS, dk, dv), F32)
        out_shape += [st, st]
        out_specs += [sspec, sspec]
    vmem = 12 * seq * heads * dv * 4 + 40 * GLA_BLOCK * GLA_BLOCK * 4
    return pl.pallas_call(
        functools.partial(_gla_kernel, nblk=seq // GLA_BLOCK, heads=heads, has_state=has_state,
                          emit_state=emit_state),
        out_shape=tuple(out_shape),
        grid=(batch, GLA_HEADS // heads),
        in_specs=in_specs,
        out_specs=tuple(out_specs),
        scratch_shapes=[pltpu.VMEM((seq, heads * dv), F32), pltpu.VMEM((seq, heads * dv), F32)],
        compiler_params=_cparams(("parallel", "parallel"), vmem),
        name="gla_ctx" if emit_state else "gla_lat",
    )(*args)


def _outproj_kernel(oa_ref, og_ref, gate_ref, x_ref, mod_ref, wpa_ref, wpg_ref, wout_ref, gpm_ref,
                    gpf_ref, wr_ref, x1_ref, h_ref, aff_ref):
    d = x_ref.shape[1]
    m = mod_ref[0]
    oa = _dot(oa_ref[...], wpa_ref[...])
    og = _dot(og_ref[...], wpg_ref[...])
    mix = gate_ref[:, 0:d].astype(F32) * oa + gate_ref[:, d:2 * d].astype(F32) * og
    mo = _dot(mix.astype(BF16), wout_ref[...])
    x1 = x_ref[...] + m[:, 2 * d:3 * d] * _rms(mo, gpm_ref[...])
    x1_ref[...] = x1
    hb = (_rms(x1, gpf_ref[...]) * (1.0 + m[:, 4 * d:5 * d]) + m[:, 3 * d:4 * d]).astype(BF16)
    bits = lax.bitcast_convert_type(hb.astype(F32), jnp.uint32)
    packed = (bits[:, 0:d // 2] >> 16) | (bits[:, d // 2:d] & jnp.uint32(0xFFFF0000))
    h_ref[...] = lax.bitcast_convert_type(packed, I32)
    logits = _dot(hb, wr_ref[...])
    valid = lax.broadcasted_iota(I32, logits.shape, 1) < N_EXPERTS
    mx = jnp.max(jnp.where(valid, logits, -jnp.inf), axis=-1, keepdims=True)
    ex = jnp.where(valid, jnp.exp(logits - mx), 0.0)
    aff_ref[...] = ex / jnp.sum(ex, axis=-1, keepdims=True)


def _outproj_call(o_a, o_g, gates, x2, mod3, mod_row_of_tile, w_pa, w_pg, w_out, g_pm, g_pf, w_r, tag):
    n, d = x2.shape
    tm = OUTPROJ_TILE
    row = lambda i: (i, 0)
    const = lambda i: (0, 0)
    wspec = pl.BlockSpec((d, d), const, pipeline_mode=pl.Buffered(1))
    vmem = 3 * d * d * 2 + 2 * tm * d * (2 + 2 + 4 + 4 + 4 + 2) + 8 * tm * d * 4
    return pl.pallas_call(
        _outproj_kernel,
        out_shape=(jax.ShapeDtypeStruct((n, d), F32), jax.ShapeDtypeStruct((n, d // 2), I32),
                   jax.ShapeDtypeStruct((n, V7X_LANES), F32)),
        grid=(n // tm,),
        in_specs=[pl.BlockSpec((tm, d), row), pl.BlockSpec((tm, d), row), pl.BlockSpec((tm, 2 * d), row),
                  pl.BlockSpec((tm, d), row),
                  pl.BlockSpec((1, 1, 6 * d), lambda i: (mod_row_of_tile(i, tm), 0, 0)),
                  wspec, wspec, wspec, pl.BlockSpec((1, d), const), pl.BlockSpec((1, d), const),
                  pl.BlockSpec((d, V7X_LANES), const)],
        out_specs=(pl.BlockSpec((tm, d), row), pl.BlockSpec((tm, d // 2), row),
                   pl.BlockSpec((tm, V7X_LANES), row)),
        compiler_params=_cparams(("parallel",), vmem),
        name="outproj_" + tag,
    )(o_a, o_g, gates, x2, mod3, w_pa, w_pg, w_out, g_pm, g_pf, w_r)


def _route_kernel(aff_ref, pos_ref, cumt_ref, tbl_ref, bnd_ref, afft_scr, *, cap, n_row_tiles):
    n = aff_ref.shape[0]
    rb = ROUTE_BLOCK
    nb = n // rb
    lanes = aff_ref.shape[1]
    lane = lax.broadcasted_iota(I32, (1, lanes), 1)
    expert_lane = lane < N_EXPERTS
    tbl_ref[...] = jnp.zeros(tbl_ref.shape, I32)
    bnd_ref[...] = jnp.zeros(bnd_ref.shape, I32)

    def to_token_lanes(c, carry):
        start = pl.multiple_of(c * rb, rb)
        afft_scr[c] = aff_ref[pl.ds(start, rb), :].T[0:N_EXPERTS, :]
        return carry

    lax.fori_loop(0, nb, to_token_lanes, 0)
    aff_t = afft_scr[...]

    def count(hit):
        return jnp.sum(jnp.sum(hit.astype(I32), axis=0), axis=1, keepdims=True)

    def bit_step(i, lo):
        t = lo | jnp.left_shift(jnp.int32(1), 30 - i)
        ge = aff_t >= lax.bitcast_convert_type(t, F32)[None]
        return jnp.where(count(ge) >= cap, t, lo)

    thr_bits = lax.fori_loop(0, 31, bit_step, jnp.zeros((N_EXPERTS, 1), I32))
    need_t = cap - count(aff_t > lax.bitcast_convert_type(thr_bits, F32)[None])

    def to_expert_lanes(col):
        full = jnp.concatenate([jnp.broadcast_to(col, (N_EXPERTS, lanes)),
                                jnp.zeros((lanes - N_EXPERTS, lanes), I32)], axis=0)
        return full.T[0:1, :]

    thr = lax.bitcast_convert_type(to_expert_lanes(thr_bits), F32)
    need = to_expert_lanes(need_t).astype(F32)
    capf = float(cap)

    r = lax.broadcasted_iota(I32, (rb, rb), 0)
    c_ = lax.broadcasted_iota(I32, (rb, rb), 1)
    tril = jnp.where(c_ <= r, 1.0, 0.0).astype(BF16)

    def blk_step(c, carry):
        eq_before, raw_before, lo_cnt, hi_cnt = carry
        start = pl.multiple_of(c * rb, rb)
        a = aff_ref[pl.ds(start, rb), :]
        gt = a > thr
        eq = a == thr
        eq_incl = _dot(tril, jnp.where(eq, 1.0, 0.0).astype(BF16)) + eq_before
        raw = (gt | (eq & (eq_incl <= need))) & expert_lane
        raw_incl = _dot(tril, jnp.where(raw, 1.0, 0.0).astype(BF16)) + raw_before
        sel = raw & (raw_incl <= capf)
        self_ = jnp.where(sel, 1.0, 0.0)
        incl = jnp.minimum(raw_incl, capf)
        sel_before = jnp.minimum(raw_before, capf)
        excl = incl - self_
        posb = jnp.where(sel, excl, -1.0).astype(I32)
        pos_ref[pl.ds(start, rb), :] = posb
        cumt_ref[c] = incl.T[0:N_EXPERTS, :]
        tbl_ref[pl.ds(c, 1), :] = sel_before.astype(I32)
        lo_new, hi_new = [], []
        for j in range(n_row_tiles):
            lo_new.append(lo_cnt[j] + jnp.sum(jnp.where(incl <= j * FFN_ROW_TILE, 1.0, 0.0), axis=0, keepdims=True))
            hi_new.append(hi_cnt[j] + jnp.sum(jnp.where(excl < (j + 1) * FFN_ROW_TILE, 1.0, 0.0), axis=0, keepdims=True))
        return (eq_incl[rb - 1:rb, :], raw_incl[rb - 1:rb, :], tuple(lo_new), tuple(hi_new))

    zero = jnp.zeros((1, lanes), F32)
    zeros_j = tuple(zero for _ in range(n_row_tiles))
    _, total, lo_cnt, hi_cnt = lax.fori_loop(0, nb, blk_step, (zero, zero, zeros_j, zeros_j))
    tbl_ref[pl.ds(nb, 1), :] = jnp.minimum(total, capf).astype(I32)
    for j in range(n_row_tiles):
        bnd_ref[pl.ds(j, 1), :] = lo_cnt[j].astype(I32)
        bnd_ref[pl.ds(n_row_tiles + j, 1), :] = hi_cnt[j].astype(I32)


def _route_call(aff, cap, tag):
    n, lanes = aff.shape
    nb = n // ROUTE_BLOCK
    n_row_tiles = cap // FFN_ROW_TILE
    tbl_rows = -(-(nb + 1) // 8) * 8
    bnd_rows = -(-(2 * n_row_tiles) // 8) * 8
    full = lambda *shape: pl.BlockSpec(shape, lambda: tuple(0 for _ in shape))
    return pl.pallas_call(
        functools.partial(_route_kernel, cap=cap, n_row_tiles=n_row_tiles),
        out_shape=(jax.ShapeDtypeStruct((n, lanes), I32),
                   jax.ShapeDtypeStruct((nb, N_EXPERTS, ROUTE_BLOCK), F32),
                   jax.ShapeDtypeStruct((tbl_rows, lanes), I32),
                   jax.ShapeDtypeStruct((bnd_rows, lanes), I32)),
        in_specs=[full(n, lanes)],
        out_specs=(full(n, lanes), full(nb, N_EXPERTS, ROUTE_BLOCK), full(tbl_rows, lanes),
                   full(bnd_rows, lanes)),
        scratch_shapes=[pltpu.VMEM((nb, N_EXPERTS, ROUTE_BLOCK), F32)],
        compiler_params=_cparams((), 8 * n * lanes * 4),
        name="route_" + tag,
    )(aff)


def _compact_kernel(bnd_ref, cumt_ref, idx_ref, *, n_row_tiles):
    rb = ROUTE_BLOCK
    tr = FFN_ROW_TILE
    r_col = lax.broadcasted_iota(I32, (tr, rb), 0).astype(F32)

    def per_expert(e, carry):
        for j in range(n_row_tiles):
            lo = bnd_ref[j, e]
            hi = bnd_ref[n_row_tiles + j, e]
            c0 = lo // rb

            def chunk(c, acc, j=j):
                cum = cumt_ref[c, pl.ds(e, 1), :]
                return acc + jnp.where(cum <= r_col + float(j * tr), 1.0, 0.0)

            acc = lax.fori_loop(c0, (hi + rb - 1) // rb, chunk, jnp.zeros((tr, rb), F32))
            row = jnp.sum(acc.T, axis=0, keepdims=True)
            idx_ref[pl.ds(e, 1), pl.ds(j * tr, tr)] = (row + (c0 * rb).astype(F32)).astype(I32)
        return carry

    lax.fori_loop(0, N_EXPERTS, per_expert, 0)


def _compact_call(bnd, cumt, cap, tag):
    nb = cumt.shape[0]
    grid_spec = pltpu.PrefetchScalarGridSpec(
        num_scalar_prefetch=1,
        grid=(1,),
        in_specs=[pl.BlockSpec((nb, N_EXPERTS, ROUTE_BLOCK), lambda i, b: (0, 0, 0))],
        out_specs=pl.BlockSpec((N_EXPERTS, cap), lambda i, b: (0, 0)),
    )
    return pl.pallas_call(
        functools.partial(_compact_kernel, n_row_tiles=cap // FFN_ROW_TILE),
        out_shape=jax.ShapeDtypeStruct((N_EXPERTS, cap), I32),
        grid_spec=grid_spec,
        compiler_params=_cparams(("arbitrary",), 8 * nb * N_EXPERTS * ROUTE_BLOCK * 4),
        name="compact_" + tag,
    )(bnd, cumt)


def _sc_gather_call(table, idx_flat):
    rows = idx_flat.shape[0]
    words = table.shape[1]
    workers = V7X_SC_CORES * V7X_SC_SUBCORES
    chunk = SC_GATHER_CHUNK
    per_worker = rows // workers
    assert per_worker * workers == rows and per_worker % chunk == 0
    mesh = plsc.VectorSubcoreMesh(core_axis_name="c", subcore_axis_name="s",
                                  num_cores=V7X_SC_CORES, num_subcores=V7X_SC_SUBCORES)

    def body(table_hbm, idx_hbm, out_hbm, idx_v, rows_v, sem):
        wid = lax.axis_index("s") * V7X_SC_CORES + lax.axis_index("c")
        base = wid * per_worker

        @pl.loop(0, per_worker // chunk)
        def _(j):
            off = pl.multiple_of(base + j * chunk, chunk)
            pltpu.sync_copy(idx_hbm.at[pl.ds(off, chunk)], idx_v)
            pltpu.async_copy(table_hbm.at[idx_v], rows_v, sem).wait()
            pltpu.sync_copy(rows_v, out_hbm.at[pl.ds(off, chunk)])

    return pl.kernel(
        body,
        out_type=jax.ShapeDtypeStruct((rows, words), table.dtype),
        mesh=mesh,
        scratch_types=[pltpu.VMEM((chunk,), I32), pltpu.VMEM((chunk, words), table.dtype),
                       pltpu.SemaphoreType.DMA],
        name="sc_gather",
    )(table, idx_flat)


def _ffn_kernel(*refs, caps):
    ng = len(caps)
    xs_refs = refs[:ng]
    w1_ref, w3_ref, w2_ref = refs[ng:ng + 3]
    ye_refs = refs[ng + 3:2 * ng + 3]
    acc_scr = refs[2 * ng + 3]
    row_off = [sum(caps[:g]) for g in range(ng)]
    f = pl.program_id(1)

    def unpack(words):
        w = lax.bitcast_convert_type(words, jnp.uint32)
        lo = lax.bitcast_convert_type(w << 16, F32).astype(BF16)
        hi = lax.bitcast_convert_type(w & jnp.uint32(0xFFFF0000), F32).astype(BF16)
        return jnp.concatenate([lo, hi], axis=1)

    xs = jnp.concatenate([unpack(r[...]) for r in xs_refs], axis=0)
    hid = _silu(_dot(xs, w1_ref[...].astype(BF16))) * _dot(xs, w3_ref[...].astype(BF16))
    part = _dot(hid.astype(BF16), w2_ref[...].astype(BF16))

    @pl.when(f == 0)
    def _first():
        acc_scr[...] = part

    @pl.when(f > 0)
    def _rest():
        acc_scr[...] += part

    @pl.when(f == pl.num_programs(1) - 1)
    def _emit():
        for g in range(ng):
            ye_refs[g][...] = acc_scr[pl.ds(row_off[g], caps[g]), :].astype(BF16)


def _ffn_call(groups, w1, w3, w2):
    caps = tuple(g[1] for g in groups)
    n_exp, d, dff = w1.shape
    tf = dff // FFN_F_TILES
    rows = sum(caps)
    in_specs = [pl.BlockSpec((cap, d // 2), lambda e, f: (e, 0)) for cap in caps]
    in_specs += [pl.BlockSpec((None, d, tf), lambda e, f: (e, 0, f)),
                 pl.BlockSpec((None, d, tf), lambda e, f: (e, 0, f)),
                 pl.BlockSpec((None, tf, d), lambda e, f: (e, f, 0))]
    vmem = 3 * d * tf * (2 * 4 + 2) + rows * d * (2 * 2 + 2 + 4 + 2 * 2 + 4) + 4 * rows * tf * 4
    return pl.pallas_call(
        functools.partial(_ffn_kernel, caps=caps),
        out_shape=tuple(jax.ShapeDtypeStruct((n_exp * cap, d), BF16) for cap in caps),
        grid=(n_exp, FFN_F_TILES),
        in_specs=in_specs,
        out_specs=tuple(pl.BlockSpec((cap, d), lambda e, f: (e, 0)) for cap in caps),
        scratch_shapes=[pltpu.VMEM((rows, d), F32)],
        compiler_params=_cparams(("arbitrary", "arbitrary"), vmem),
        name="ffn",
    )(*[g[0] for g in groups], w1, w3, w2)


def _combine_kernel(tbl_ref, ye_hbm, pos_ref, aff_ref, x1_ref, mod_ref, gpo_ref, y_ref, buf, sem, xbuf, xsem,
                    acc_scr, *, cap, blocks_per_tile):
    i = pl.program_id(0)
    nsteps = pl.num_programs(0)
    d = x1_ref.shape[1]
    lanes = pos_ref.shape[1]
    win = COMBINE_WINDOW
    last_start = ye_hbm.shape[0] - win
    slot = i % 2

    def first_row(step, e):
        return tbl_ref[step * blocks_per_tile, e] + e * cap

    def window_start(first, k):
        unclamped = (first // BF16_ROWS_PER_TILE) * BF16_ROWS_PER_TILE + k * win
        return unclamped, jnp.minimum(unclamped, last_start)

    def fetch(step, to_slot, e):
        start = window_start(first_row(step, e), 0)[1]
        return pltpu.make_async_copy(ye_hbm.at[pl.ds(pl.multiple_of(start, BF16_ROWS_PER_TILE), win), :],
                                     buf.at[to_slot, pl.ds(e * win, win), :], sem.at[to_slot, e])

    @pl.when(i == 0)
    def _prime():
        for e in range(N_EXPERTS):
            fetch(0, 0, e).start()

    @pl.when(i + 1 < nsteps)
    def _ahead():
        for e in range(N_EXPERTS):
            fetch(i + 1, 1 - slot, e).start()

    lane_row = lax.broadcasted_iota(I32, (1, win), 1)
    pieces = []
    for e in range(N_EXPERTS):
        pcol = pos_ref[:, e:e + 1]
        grow = jnp.where(pcol >= 0, pcol + e * cap, -1)
        start = window_start(first_row(i, e), 0)[1]
        pieces.append(jnp.where(grow == start + lane_row, aff_ref[:, e:e + 1], 0.0).astype(BF16))
    onehot = jnp.concatenate(pieces, axis=1)
    for e in range(N_EXPERTS):
        fetch(i, slot, e).wait()
    acc_scr[...] = _dot(onehot, buf[slot])

    def extra_windows(e):
        covered = window_start(first_row(i, e), 1)[0]
        return jnp.maximum(first_row(i + 1, e) - covered + win - 1, 0) // win

    def expert_extra(e, carry):
        first = first_row(i, e)
        extra = extra_windows(e)

        def more(k, c):
            unclamped, start = window_start(first, k)
            cp = pltpu.make_async_copy(ye_hbm.at[pl.ds(pl.multiple_of(start, BF16_ROWS_PER_TILE), win), :],
                                       xbuf, xsem)
            cp.start()
            cp.wait()
            at_e = lax.broadcasted_iota(I32, (1, lanes), 1) == e
            pcol = jnp.sum(jnp.where(at_e, pos_ref[...].astype(F32), 0.0), axis=1, keepdims=True).astype(I32)
            wcol = jnp.sum(jnp.where(at_e, aff_ref[...], 0.0), axis=1, keepdims=True)
            grow = jnp.where(pcol >= 0, pcol + e * cap, -1)
            hit = (grow == start + lane_row) & (grow >= unclamped)
            acc_scr[...] += _dot(jnp.where(hit, wcol, 0.0).astype(BF16), xbuf[...])
            return c

        lax.fori_loop(1, 1 + extra, more, 0)
        return carry

    any_extra = extra_windows(0)
    for e in range(1, N_EXPERTS):
        any_extra = any_extra + extra_windows(e)

    @pl.when(any_extra > 0)
    def _overflow():
        lax.fori_loop(0, N_EXPERTS, expert_extra, 0)

    m = mod_ref[0]
    y_ref[...] = x1_ref[...] + m[:, 5 * d:6 * d] * _rms(acc_scr[...], gpo_ref[...])


def _combine_call(tbl, ye, pos, aff, x1, mod3, mod_row_of_tile, g_po, cap, tag):
    n, d = x1.shape
    tm = COMBINE_TILE
    lanes = pos.shape[1]
    grid_spec = pltpu.PrefetchScalarGridSpec(
        num_scalar_prefetch=1,
        grid=(n // tm,),
        in_specs=[pl.BlockSpec(memory_space=pl.ANY),
                  pl.BlockSpec((tm, lanes), lambda i, t: (i, 0)),
                  pl.BlockSpec((tm, lanes), lambda i, t: (i, 0)),
                  pl.BlockSpec((tm, d), lambda i, t: (i, 0)),
                  pl.BlockSpec((1, 1, 6 * d), lambda i, t: (mod_row_of_tile(i, tm), 0, 0)),
                  pl.BlockSpec((1, d), lambda i, t: (0, 0))],
        out_specs=pl.BlockSpec((tm, d), lambda i, t: (i, 0)),
        scratch_shapes=[pltpu.VMEM((2, N_EXPERTS * COMBINE_WINDOW, d), BF16),
                        pltpu.SemaphoreType.DMA((2, N_EXPERTS)),
                        pltpu.VMEM((COMBINE_WINDOW, d), BF16),
                        pltpu.SemaphoreType.DMA(()),
                        pltpu.VMEM((tm, d), F32)],
    )
    vmem = 2 * N_EXPERTS * COMBINE_WINDOW * d * 2 + 10 * tm * d * 4 + 2 * tm * N_EXPERTS * COMBINE_WINDOW * 4
    return pl.pallas_call(
        functools.partial(_combine_kernel, cap=cap, blocks_per_tile=tm // ROUTE_BLOCK),
        out_shape=jax.ShapeDtypeStruct((n, d), F32),
        grid_spec=grid_spec,
        compiler_params=_cparams(("arbitrary",), vmem),
        name="combine_" + tag,
    )(tbl, ye, pos, aff, x1, mod3, g_po)


def _rope_tables(seq):
    rows = seq // GRID_W
    r = jnp.repeat(jnp.arange(rows), GRID_W).astype(F32)
    col = jnp.tile(jnp.arange(GRID_W), rows).astype(F32)
    pairs = HEAD_DIM // 4
    freqs = ROPE_THETA ** (-jnp.arange(pairs, dtype=F32) / pairs)
    ang = jnp.concatenate([r[:, None] * freqs, col[:, None] * freqs], axis=-1)
    cos = jnp.repeat(jnp.cos(ang), 2, axis=-1)
    sin = jnp.repeat(jnp.sin(ang), 2, axis=-1)
    even = (jnp.arange(HEAD_DIM) % 2) == 0
    return cos, jnp.where(even, -sin, 0.0), jnp.where(even, 0.0, sin)


def _trunk_to_routing(x, mod3, mod_row_of_tile, rope_tabs, ctx, lw, tag):
    (g_pre_mix, g_post_mix, g_pre_ffn, g_post_ffn, w_in_p, g_q, g_k, wgf, bgf, wgb, bgb, g_gla,
     w_pa, w_pg, w_out, w_r, w1, w3, w2) = lw
    batch, seq, d = x.shape
    n = batch * seq
    x2 = x.reshape(n, d)
    (q_a, k_a, v_a, q_g, k_g, v_g, r_g, lg_f, lg_b, gates) = _inproj_call(
        x2, mod3, mod_row_of_tile, g_pre_mix, w_in_p, g_q, g_k, wgf, bgf, wgb, bgb, rope_tabs, batch, seq)
    if ctx is None:
        o_a = _attn_call(q_a, k_a, v_a, None, None)
        o_g, s_f, s_b = _gla_call(q_g, k_g, v_g, lg_f, lg_b, r_g, g_gla, None, None, batch, seq, True,
                                  GLA_CTX_HEADS_PER_STEP)
    else:
        ck, cv, s_f0, s_b0 = ctx
        o_a = _attn_call(q_a, k_a, v_a, ck, cv)
        (o_g,) = _gla_call(q_g, k_g, v_g, lg_f, lg_b, r_g, g_gla, s_f0, s_b0, batch, seq, False,
                           GLA_LAT_HEADS_PER_STEP)
        s_f = s_b = None
    x1, h, aff = _outproj_call(o_a, o_g, gates, x2, mod3, mod_row_of_tile, w_pa, w_pg, w_out,
                               g_post_mix, g_pre_ffn, w_r, tag)
    cap = (EC_CAPACITY_FACTOR * n) // N_EXPERTS
    pos, cumt, tbl, bnd = _route_call(aff, cap, tag)
    idx = _compact_call(bnd, cumt, cap, tag)
    xs = _sc_gather_call(h, idx.reshape(N_EXPERTS * cap))
    return dict(x1=x1, xs=xs, aff=aff, pos=pos, tbl=tbl, cap=cap,
                mod_row_of_tile=mod_row_of_tile, tag=tag, shape=(batch, seq, d)), (k_a, v_a, s_f, s_b)


def _expert_ffn(groups, mod3, g_post_ffn, w1, w3, w2):
    yes = _ffn_call([(g["xs"], g["cap"]) for g in groups], w1, w3, w2)
    outs = []
    for g, ye in zip(groups, yes):
        y = _combine_call(g["tbl"], ye, g["pos"], g["aff"], g["x1"], mod3, g["mod_row_of_tile"], g_post_ffn,
                          g["cap"], g["tag"])
        outs.append(y.reshape(g["shape"]))
    return outs


def kernel(x_prompt, x_sample, cache_k, cache_v, state_gla_fwd, state_gla_bwd, c, c_ctx, g_pre_mix, g_post_mix, g_pre_ffn, g_post_ffn, w_mod, b_mod, w_in, g_q, g_k, w_gk2_f, b_gk_f, w_gk2_b, b_gk_b, g_gla, w_pa, w_pg, w_out, w_router, w1, w3, w2):
    depth = w_in.shape[0]
    assert depth == 1, "single trunk layer"
    d = x_prompt.shape[-1]
    dec_batch, dec_seq, _ = x_sample.shape
    assert dec_batch + 1 <= MOD_ROWS
    l = 0
    rank = GLA_GATE_RANK
    w_in_p = jnp.swapaxes(w_in[l], 0, 1).astype(BF16)
    gkw = w_gk2_f.shape[-1]
    wgf = jnp.zeros((V7X_LANES, gkw), F32).at[0:rank].set(w_gk2_f[l]).astype(BF16)
    wgb = jnp.zeros((V7X_LANES, gkw), F32).at[rank:2 * rank].set(w_gk2_b[l]).astype(BF16)
    w_r = jnp.zeros((d, V7X_LANES), F32).at[:, :N_EXPERTS].set(w_router[l]).astype(BF16)
    row = lambda a: a[l].reshape(1, -1)
    lw = (row(g_pre_mix), row(g_post_mix), row(g_pre_ffn), row(g_post_ffn), w_in_p, row(g_q), row(g_k),
          wgf, row(b_gk_f), wgb, row(b_gk_b), row(g_gla),
          w_pa[l].astype(BF16), w_pg[l].astype(BF16), w_out[l].astype(BF16), w_r, w1[l], w3[l], w2[l])

    cc = jnp.concatenate([c_ctx[None, :], c, jnp.zeros((MOD_ROWS - 1 - dec_batch, d), F32)], axis=0)
    mod = _mod_call(cc, w_mod[l], b_mod[l].reshape(1, -1))
    mod3 = mod.reshape(MOD_ROWS, 1, 6 * d)

    gp, (nk, nv, nsf, nsb) = _trunk_to_routing(x_prompt, mod3, lambda i, tm: 0, None, None, lw, "ctx")
    ctx = (cache_k, cache_v, state_gla_fwd, state_gla_bwd)
    gs, _ = _trunk_to_routing(x_sample, mod3, lambda i, tm: 1 + (i * tm) // dec_seq, _rope_tables(dec_seq),
                              ctx, lw, "lat")
    yp, ys = _expert_ffn([gp, gs], mod3, lw[3], lw[16], lw[17], lw[18])
    return (yp, ys, nk, nv, nsf, nsb)
```

```python
import functools

import jax
import jax.numpy as jnp
from jax import lax
from jax.experimental import pallas as pl
from jax.experimental.pallas import tpu as pltpu
from jax.experimental.pallas import tpu_sc as plsc

F32 = jnp.float32
BF16 = jnp.bfloat16
I32 = jnp.int32

N_HEADS = 8
N_KV_HEADS = 2
HEAD_DIM = 128
GRID_W = 64
ROPE_THETA = 10000.0
GLA_HEADS = 4
GLA_GATE_RANK = 16
GLA_GATE_NORM = 16.0
GLA_CHUNK = 64
N_EXPERTS = 16
EC_CAPACITY_FACTOR = 2
EPS = 1e-6

V7X_LANES = 128
V7X_VMEM_BYTES = 64 * 1024 * 1024
V7X_VMEM_RESERVE_BYTES = 6 * 1024 * 1024
BF16_ROWS_PER_TILE = 16
V7X_SC_CORES = 2
V7X_SC_SUBCORES = 16
V7X_SC_LANES = 16

TOKEN_TILE = 512
OUTPROJ_TILE = 512
ATTN_Q_TILE = 256
GLA_BLOCK = 256
GLA_CTX_HEADS_PER_STEP = 4
GLA_LAT_HEADS_PER_STEP = 1
ROUTE_BLOCK = 256
FFN_ROW_TILE = 256
FFN_F_TILES = 2
SC_GATHER_CHUNK = 64
COMBINE_TILE = 512
COMBINE_WINDOW = 128
MOD_ROWS = 8
MOD_N_TILE = 1536


def _cparams(semantics, vmem_bytes):
    limit = min(max(int(vmem_bytes), 32 * 1024 * 1024), V7X_VMEM_BYTES - V7X_VMEM_RESERVE_BYTES)
    return pltpu.CompilerParams(dimension_semantics=semantics, vmem_limit_bytes=limit)


def _in_hbm(arrays):
    return [pltpu.with_memory_space_constraint(a, pltpu.HBM) for a in arrays]


def _sigmoid(x):
    return 1.0 / (1.0 + jnp.exp(-x))


def _silu(x):
    return x * _sigmoid(x)


def _log_sigmoid(x):
    return jnp.minimum(x, 0.0) - jnp.log1p(jnp.exp(-jnp.abs(x)))


def _rms(x, g):
    ms = jnp.mean(x * x, axis=-1, keepdims=True)
    return x * lax.rsqrt(ms + EPS) * g


def _dot(a, b):
    return jnp.dot(a, b, preferred_element_type=F32)


def _dot_nt(a, b):
    return lax.dot_general(a, b, (((1,), (1,)), ((), ())), preferred_element_type=F32)


def _mod_kernel(c_ref, w_ref, b_ref, o_ref):
    s = _silu(c_ref[...]).astype(BF16)
    o_ref[...] = _dot(s, w_ref[...].astype(BF16)) + b_ref[...]


def _mod_call(cc, w_mod, b_mod):
    d, n6 = w_mod.shape
    tn = MOD_N_TILE
    return pl.pallas_call(
        _mod_kernel,
        out_shape=jax.ShapeDtypeStruct((MOD_ROWS, n6), F32),
        grid=(n6 // tn,),
        in_specs=[pl.BlockSpec((MOD_ROWS, d), lambda j: (0, 0)),
                  pl.BlockSpec((d, tn), lambda j: (0, j)),
                  pl.BlockSpec((1, tn), lambda j: (0, j))],
        out_specs=pl.BlockSpec((MOD_ROWS, tn), lambda j: (0, j)),
        compiler_params=_cparams(("arbitrary",), 3 * d * tn * 4),
        name="mod",
    )(cc, w_mod, b_mod)


def _inproj_layout(d):
    aq, akv = N_HEADS * HEAD_DIM, N_KV_HEADS * HEAD_DIM
    gk, gv = d // 2, d
    names = ("q_a", "k_a", "v_a", "q_g", "k_g", "v_g", "r_g", "gk_f", "gk_b", "gates")
    widths = (aq, akv, akv, gk, gk, gv, gv, GLA_GATE_RANK, GLA_GATE_RANK, 2 * d)
    off, o = {}, 0
    for nme, w in zip(names, widths):
        off[nme] = (o, o + w)
        o += w
    off["gk"] = (off["gk_f"][0], off["gk_f"][0] + V7X_LANES)
    return off, o


def _inproj_kernel(*refs, rope, d):
    if rope:
        (x_ref, mod_ref, gpre_ref, w_ref, gq_ref, gk_ref, wgf_ref, bgf_ref, wgb_ref, bgb_ref,
         cos_ref, se_ref, so_ref, *outs) = refs
    else:
        (x_ref, mod_ref, gpre_ref, w_ref, gq_ref, gk_ref, wgf_ref, bgf_ref, wgb_ref, bgb_ref,
         *outs) = refs
    qa_ref, k_ref, v_ref, qg_ref, kg_ref, vg_ref, rg_ref, lgf_ref, lgb_ref, gate_ref = outs
    off, _ = _inproj_layout(d)
    m = mod_ref[0]
    h = _rms(x_ref[...], gpre_ref[...]) * (1.0 + m[:, d:2 * d]) + m[:, 0:d]
    hb = h.astype(BF16)

    def proj(name):
        a, b = off[name]
        return _dot_nt(hb, w_ref[a:b, :])

    def qk_norm(y, g_ref):
        y = _rms(y, g_ref[...])
        if rope:
            nxt = pltpu.roll(y, HEAD_DIM - 1, axis=1)
            prv = pltpu.roll(y, 1, axis=1)
            y = y * cos_ref[...] + nxt * se_ref[...] + prv * so_ref[...]
        return y

    q = proj("q_a")
    scale = HEAD_DIM ** -0.5
    for hd in range(N_HEADS):
        sl = slice(hd * HEAD_DIM, (hd + 1) * HEAD_DIM)
        qa_ref[:, sl] = (qk_norm(q[:, sl], gq_ref) * scale).astype(BF16)

    k = proj("k_a")
    v = proj("v_a")
    tb, _, _, ts, _ = k_ref.shape
    for kv in range(N_KV_HEADS):
        sl = slice(kv * HEAD_DIM, (kv + 1) * HEAD_DIM)
        k_ref[:, 0, kv] = qk_norm(k[:, sl], gk_ref).reshape(tb, ts, HEAD_DIM)
        v_ref[:, 0, kv] = v[:, sl].reshape(tb, ts, HEAD_DIM)

    dk = (d // 2) // GLA_HEADS
    qg_ref[...] = (proj("q_g") * (dk ** -0.5)).astype(BF16)
    kg_ref[...] = proj("k_g").astype(BF16)
    vg_ref[...] = proj("v_g").astype(BF16)
    rg_ref[...] = _silu(proj("r_g")).astype(BF16)

    gk = proj("gk").astype(BF16)
    lgf_ref[...] = _log_sigmoid(_dot(gk, wgf_ref[...]) + bgf_ref[...]) * (1.0 / GLA_GATE_NORM)
    lgb_ref[...] = _log_sigmoid(_dot(gk, wgb_ref[...]) + bgb_ref[...]) * (1.0 / GLA_GATE_NORM)

    gate_ref[...] = _sigmoid(proj("gates")).astype(BF16)


def _inproj_call(x2, mod3, mod_row_of_tile, g_pre, w_in_p, g_q, g_k, wgf, bgf, wgb, bgb, rope_tabs,
                 batch, seq):
    n, d = x2.shape
    tm = TOKEN_TILE
    _, dinp = _inproj_layout(d)
    rope = rope_tabs is not None
    gk_w = d // 2
    if seq >= tm:
        tb, ts, per = 1, tm, seq // tm
        kv_map = lambda i: (i // per, 0, 0, i % per, 0)
    else:
        tb, ts, per = tm // seq, seq, 1
        kv_map = lambda i: (i, 0, 0, 0, 0)
    row = lambda i: (i, 0)
    const = lambda i: (0, 0)
    in_specs = [
        pl.BlockSpec((tm, d), row),
        pl.BlockSpec((1, 1, 6 * d), lambda i: (mod_row_of_tile(i, tm), 0, 0)),
        pl.BlockSpec((1, d), const),
        pl.BlockSpec((dinp, d), const, pipeline_mode=pl.Buffered(1)),
        pl.BlockSpec((1, HEAD_DIM), const),
        pl.BlockSpec((1, HEAD_DIM), const),
        pl.BlockSpec((V7X_LANES, gk_w), const),
        pl.BlockSpec((1, gk_w), const),
        pl.BlockSpec((V7X_LANES, gk_w), const),
        pl.BlockSpec((1, gk_w), const),
    ]
    args = [x2, mod3, g_pre, w_in_p, g_q, g_k, wgf, bgf, wgb, bgb]
    if rope:
        tab = pl.BlockSpec((tm, HEAD_DIM), lambda i: (i % per, 0))
        in_specs += [tab, tab, tab]
        args += list(rope_tabs)
    kv_shape = jax.ShapeDtypeStruct((batch, 1, N_KV_HEADS, seq, HEAD_DIM), F32)
    kv_spec = pl.BlockSpec((tb, 1, N_KV_HEADS, ts, HEAD_DIM), kv_map)
    out_shape = (
        jax.ShapeDtypeStruct((n, N_HEADS * HEAD_DIM), BF16), kv_shape, kv_shape,
        jax.ShapeDtypeStruct((n, gk_w), BF16), jax.ShapeDtypeStruct((n, gk_w), BF16),
        jax.ShapeDtypeStruct((n, d), BF16), jax.ShapeDtypeStruct((n, d), BF16),
        jax.ShapeDtypeStruct((n, gk_w), F32), jax.ShapeDtypeStruct((n, gk_w), F32),
        jax.ShapeDtypeStruct((n, 2 * d), BF16),
    )
    out_specs = (
        pl.BlockSpec((tm, N_HEADS * HEAD_DIM), row), kv_spec, kv_spec,
        pl.BlockSpec((tm, gk_w), row), pl.BlockSpec((tm, gk_w), row),
        pl.BlockSpec((tm, d), row), pl.BlockSpec((tm, d), row),
        pl.BlockSpec((tm, gk_w), row), pl.BlockSpec((tm, gk_w), row),
        pl.BlockSpec((tm, 2 * d), row),
    )
    out_row_bytes = 2 * (N_HEADS * HEAD_DIM + 2 * gk_w + 2 * d + 2 * d) + 4 * (4 * HEAD_DIM + 2 * gk_w)
    vmem = d * dinp * 2 + 2 * tm * (d * 4 + out_row_bytes) + 6 * tm * 2 * d * 4
    return pl.pallas_call(
        functools.partial(_inproj_kernel, rope=rope, d=d),
        out_shape=out_shape,
        grid=(n // tm,),
        in_specs=in_specs,
        out_specs=out_specs,
        compiler_params=_cparams(("parallel",), vmem),
        name="inproj_lat" if rope else "inproj_ctx",
    )(*_in_hbm(args))


def _attn_kernel(*refs, cached):
    if cached:
        q_ref, k_ref, v_ref, ck_ref, cv_ref, o_ref = refs
    else:
        q_ref, k_ref, v_ref, o_ref = refs
    tq = q_ref.shape[0]
    grp = N_HEADS // N_KV_HEADS
    for kv in range(N_KV_HEADS):
        kk = k_ref[0, 0, kv].astype(BF16)
        vv = v_ref[0, 0, kv].astype(BF16)
        if cached:
            kk = jnp.concatenate([ck_ref[0, 0, kv].astype(BF16), kk], axis=0)
            vv = jnp.concatenate([cv_ref[0, 0, kv].astype(BF16), vv], axis=0)
        heads = [q_ref[:, (kv * grp + g) * HEAD_DIM:(kv * grp + g + 1) * HEAD_DIM] for g in range(grp)]
        q4 = jnp.concatenate(heads, axis=0)
        s = _dot_nt(q4, kk)
        p = jnp.exp(s - jnp.max(s, axis=-1, keepdims=True))
        l = jnp.sum(p, axis=-1, keepdims=True)
        o = _dot(p.astype(BF16), vv) / l
        for g in range(grp):
            hd = kv * grp + g
            o_ref[:, hd * HEAD_DIM:(hd + 1) * HEAD_DIM] = o[g * tq:(g + 1) * tq].astype(BF16)


def _attn_call(q_a, k_a, v_a, cache_k, cache_v):
    batch, _, _, seq, _ = k_a.shape
    n, aq = q_a.shape
    tq = ATTN_Q_TILE
    per = seq // tq
    cached = cache_k is not None
    own = pl.BlockSpec((1, 1, N_KV_HEADS, seq, HEAD_DIM), lambda b, j: (b, 0, 0, 0, 0))
    in_specs = [pl.BlockSpec((tq, aq), lambda b, j: (b * per + j, 0)), own, own]
    args = [q_a, k_a, v_a]
    klen = seq
    if cached:
        past = cache_k.shape[3]
        cspec = pl.BlockSpec((1, 1, N_KV_HEADS, past, HEAD_DIM), lambda b, j: (b, 0, 0, 0, 0))
        in_specs += [cspec, cspec]
        args += [cache_k, cache_v]
        klen += past
    grp = N_HEADS // N_KV_HEADS
    vmem = 4 * grp * tq * klen * 4 + 8 * N_KV_HEADS * klen * HEAD_DIM * 4 + 8 * tq * aq * 2
    return pl.pallas_call(
        functools.partial(_attn_kernel, cached=cached),
        out_shape=jax.ShapeDtypeStruct((n, aq), BF16),
        grid=(batch, per),
        in_specs=in_specs,
        out_specs=pl.BlockSpec((tq, aq), lambda b, j: (b * per + j, 0)),
        compiler_params=_cparams(("parallel", "parallel"), vmem),
        name="attn_lat" if cached else "attn_ctx",
    )(*_in_hbm(args))


def _gla_kernel(*refs, nblk, heads, has_state, emit_state):
    refs = list(refs)
    q_ref, k_ref, v_ref, lgf_ref, lgb_ref, rg_ref, gg_ref = refs[:7]
    pos = 7
    if has_state:
        s0f_ref, s0b_ref = refs[pos:pos + 2]
        pos += 2
    og_ref = refs[pos]
    pos += 1
    if emit_state:
        sf_ref, sb_ref = refs[pos:pos + 2]
        pos += 2
    of_scr, ob_scr = refs[pos:pos + 2]

    blk = GLA_BLOCK
    ch = GLA_CHUNK
    nch = blk // ch
    dk = q_ref.shape[1] // heads
    dv = v_ref.shape[1] // heads
    shift = ch.bit_length() - 1
    row_in_chunk = lax.broadcasted_iota(I32, (blk, dk), 0) & (ch - 1)
    ri = lax.broadcasted_iota(I32, (blk, blk), 0)
    ci = lax.broadcasted_iota(I32, (blk, blk), 1)
    same = (ri >> shift) == (ci >> shift)
    col_chunk = lax.broadcasted_iota(I32, (dk, blk), 1) >> shift

    def one_block(b0, hd, state, reverse):
        rows = pl.ds(b0, blk)
        kcols = slice(hd * dk, (hd + 1) * dk)
        q = q_ref[rows, kcols].astype(F32)
        k = k_ref[rows, kcols].astype(F32)
        v = v_ref[rows, hd * dv:(hd + 1) * dv]
        b = (lgb_ref if reverse else lgf_ref)[rows, kcols]
        s = 1
        while s < ch:
            if reverse:
                sh = pltpu.roll(b, blk - s, axis=0)
                b = b + jnp.where(row_in_chunk < ch - s, sh, 0.0)
            else:
                sh = pltpu.roll(b, s, axis=0)
                b = b + jnp.where(row_in_chunk >= s, sh, 0.0)
            s *= 2
        qe = (q * jnp.exp(b)).astype(BF16)
        ke = (k * jnp.exp(-b)).astype(BF16)
        tri = (ci >= ri) if reverse else (ci <= ri)
        a = jnp.where(same & tri, _dot_nt(qe, ke), 0.0).astype(BF16)

        order = range(nch - 1, -1, -1) if reverse else range(nch)
        end_row = [c * ch + (0 if reverse else ch - 1) for c in range(nch)]
        ends = [b[r:r + 1, :] for r in end_row]
        b_end = jnp.concatenate([jnp.broadcast_to(e, (ch, dk)) for e in ends], axis=0)
        kdt = (k * jnp.exp(b_end - b)).T
        stacked = jnp.concatenate(
            [a] + [jnp.where(col_chunk == c, kdt, 0.0).astype(BF16) for c in range(nch)], axis=0)
        big = _dot(stacked, v)
        decay = jnp.exp(jnp.concatenate(ends + [jnp.zeros((dk - nch, dk), F32)], axis=0)).T
        inter = [None] * nch
        for c in order:
            inter[c] = _dot(qe[c * ch:(c + 1) * ch], state.astype(BF16))
            state = decay[:, c:c + 1] * state + big[blk + c * dk:blk + (c + 1) * dk]
        return big[0:blk] + jnp.concatenate(inter, axis=0), state

    for hd in range(heads):
        vcols = slice(hd * dv, (hd + 1) * dv)
        sf = s0f_ref[0, 0, hd] if has_state else jnp.zeros((dk, dv), F32)
        sb = s0b_ref[0, 0, hd] if has_state else jnp.zeros((dk, dv), F32)
        for i in range(nblk):
            j = nblk - 1 - i
            o_f, sf = one_block(i * blk, hd, sf, False)
            o_b, sb = one_block(j * blk, hd, sb, True)
            of_scr[pl.ds(i * blk, blk), vcols] = o_f
            ob_scr[pl.ds(j * blk, blk), vcols] = o_b
        if emit_state:
            sf_ref[0, 0, hd] = sf
            sb_ref[0, 0, hd] = sb
        o = of_scr[:, vcols] + ob_scr[:, vcols]
        og_ref[:, vcols] = (_rms(o, gg_ref[...]) * rg_ref[:, vcols].astype(F32)).astype(BF16)


def _gla_call(q_g, k_g, v_g, lg_f, lg_b, r_g, g_gla, state_f, state_b, batch, seq, emit_state, heads):
    n, gkw = q_g.shape
    d = v_g.shape[1]
    dk, dv = gkw // GLA_HEADS, d // GLA_HEADS
    has_state = state_f is not None
    kspec = pl.BlockSpec((seq, heads * dk), lambda b, h: (b, h))
    vspec = pl.BlockSpec((seq, heads * dv), lambda b, h: (b, h))
    sspec = pl.BlockSpec((1, 1, heads, dk, dv), lambda b, h: (b, 0, h, 0, 0))
    in_specs = [kspec, kspec, vspec, kspec, kspec, vspec, pl.BlockSpec((1, dv), lambda b, h: (0, 0))]
    args = [q_g, k_g, v_g, lg_f, lg_b, r_g, g_gla]
    if has_state:
        in_specs += [sspec, sspec]
        args += [state_f, state_b]
    out_shape = [jax.ShapeDtypeStruct((n, d), BF16)]
    out_specs = [vspec]
    if emit_state:
        st = jax.ShapeDtypeStruct((batch, 1, GLA_HEADS, dk, dv), F32)
        out_shape += [st, st]
        out_specs += [sspec, sspec]
    vmem = 12 * seq * heads * dv * 4 + 40 * GLA_BLOCK * GLA_BLOCK * 4
    return pl.pallas_call(
        functools.partial(_gla_kernel, nblk=seq // GLA_BLOCK, heads=heads, has_state=has_state,
                          emit_state=emit_state),
        out_shape=tuple(out_shape),
        grid=(batch, GLA_HEADS // heads),
        in_specs=in_specs,
        out_specs=tuple(out_specs),
        scratch_shapes=[pltpu.VMEM((seq, heads * dv), F32), pltpu.VMEM((seq, heads * dv), F32)],
        compiler_params=_cparams(("parallel", "parallel"), vmem),
        name="gla_ctx" if emit_state else "gla_lat",
    )(*_in_hbm(args))


def _outproj_kernel(oa_ref, og_ref, gate_ref, x_ref, mod_ref, wpa_ref, wpg_ref, wout_ref, gpm_ref,
                    gpf_ref, wr_ref, x1_ref, h_ref, aff_ref):
    d = x_ref.shape[1]
    m = mod_ref[0]
    oa = _dot(oa_ref[...], wpa_ref[...])
    og = _dot(og_ref[...], wpg_ref[...])
    mix = gate_ref[:, 0:d].astype(F32) * oa + gate_ref[:, d:2 * d].astype(F32) * og
    mo = _dot(mix.astype(BF16), wout_ref[...])
    x1 = x_ref[...] + m[:, 2 * d:3 * d] * _rms(mo, gpm_ref[...])
    x1_ref[...] = x1
    hb = (_rms(x1, gpf_ref[...]) * (1.0 + m[:, 4 * d:5 * d]) + m[:, 3 * d:4 * d]).astype(BF16)
    bits = lax.bitcast_convert_type(hb.astype(F32), jnp.uint32)
    packed = (bits[:, 0:d // 2] >> 16) | (bits[:, d // 2:d] & jnp.uint32(0xFFFF0000))
    h_ref[...] = lax.bitcast_convert_type(packed, I32)
    logits = _dot(hb, wr_ref[...])
    valid = lax.broadcasted_iota(I32, logits.shape, 1) < N_EXPERTS
    mx = jnp.max(jnp.where(valid, logits, -jnp.inf), axis=-1, keepdims=True)
    ex = jnp.where(valid, jnp.exp(logits - mx), 0.0)
    aff_ref[...] = ex / jnp.sum(ex, axis=-1, keepdims=True)


def _outproj_call(o_a, o_g, gates, x2, mod3, mod_row_of_tile, w_pa, w_pg, w_out, g_pm, g_pf, w_r, tag):
    n, d = x2.shape
    tm = OUTPROJ_TILE
    row = lambda i: (i, 0)
    const = lambda i: (0, 0)
    wspec = pl.BlockSpec((d, d), const, pipeline_mode=pl.Buffered(1))
    vmem = 3 * d * d * 2 + 2 * tm * d * (2 + 2 + 4 + 4 + 4 + 2) + 8 * tm * d * 4
    return pl.pallas_call(
        _outproj_kernel,
        out_shape=(jax.ShapeDtypeStruct((n, d), F32), jax.ShapeDtypeStruct((n, d // 2), I32),
                   jax.ShapeDtypeStruct((n, V7X_LANES), F32)),
        grid=(n // tm,),
        in_specs=[pl.BlockSpec((tm, d), row), pl.BlockSpec((tm, d), row), pl.BlockSpec((tm, 2 * d), row),
                  pl.BlockSpec((tm, d), row),
                  pl.BlockSpec((1, 1, 6 * d), lambda i: (mod_row_of_tile(i, tm), 0, 0)),
                  wspec, wspec, wspec, pl.BlockSpec((1, d), const), pl.BlockSpec((1, d), const),
                  pl.BlockSpec((d, V7X_LANES), const)],
        out_specs=(pl.BlockSpec((tm, d), row), pl.BlockSpec((tm, d // 2), row),
                   pl.BlockSpec((tm, V7X_LANES), row)),
        compiler_params=_cparams(("parallel",), vmem),
        name="outproj_" + tag,
    )(*_in_hbm([o_a, o_g, gates, x2, mod3, w_pa, w_pg, w_out, g_pm, g_pf, w_r]))


def _route_kernel(aff_ref, pos_ref, post_ref, tbl_ref, afft_scr, *, cap):
    n = aff_ref.shape[0]
    rb = ROUTE_BLOCK
    nb = n // rb
    lanes = aff_ref.shape[1]
    lane = lax.broadcasted_iota(I32, (1, lanes), 1)
    expert_lane = lane < N_EXPERTS
    tbl_ref[...] = jnp.zeros(tbl_ref.shape, I32)

    def to_token_lanes(c, carry):
        start = pl.multiple_of(c * rb, rb)
        afft_scr[c] = aff_ref[pl.ds(start, rb), :].T[0:N_EXPERTS, :]
        return carry

    lax.fori_loop(0, nb, to_token_lanes, 0)
    aff_t = afft_scr[...]

    def count(hit):
        return jnp.sum(jnp.sum(hit.astype(I32), axis=0), axis=1, keepdims=True)

    def bit_step(i, lo):
        t = lo | jnp.left_shift(jnp.int32(1), 30 - i)
        ge = aff_t >= lax.bitcast_convert_type(t, F32)[None]
        return jnp.where(count(ge) >= cap, t, lo)

    thr_bits = lax.fori_loop(0, 31, bit_step, jnp.zeros((N_EXPERTS, 1), I32))
    need_t = cap - count(aff_t > lax.bitcast_convert_type(thr_bits, F32)[None])

    def to_expert_lanes(col):
        full = jnp.concatenate([jnp.broadcast_to(col, (N_EXPERTS, lanes)),
                                jnp.zeros((lanes - N_EXPERTS, lanes), I32)], axis=0)
        return full.T[0:1, :]

    thr = lax.bitcast_convert_type(to_expert_lanes(thr_bits), F32)
    need = to_expert_lanes(need_t).astype(F32)
    capf = float(cap)

    r = lax.broadcasted_iota(I32, (rb, rb), 0)
    c_ = lax.broadcasted_iota(I32, (rb, rb), 1)
    tril = jnp.where(c_ <= r, 1.0, 0.0).astype(BF16)

    def blk_step(c, carry):
        eq_before, raw_before = carry
        start = pl.multiple_of(c * rb, rb)
        a = aff_ref[pl.ds(start, rb), :]
        gt = a > thr
        eq = a == thr
        eq_incl = _dot(tril, jnp.where(eq, 1.0, 0.0).astype(BF16)) + eq_before
        raw = (gt | (eq & (eq_incl <= need))) & expert_lane
        raw_incl = _dot(tril, jnp.where(raw, 1.0, 0.0).astype(BF16)) + raw_before
        sel = raw & (raw_incl <= capf)
        self_ = jnp.where(sel, 1.0, 0.0)
        incl = jnp.minimum(raw_incl, capf)
        sel_before = jnp.minimum(raw_before, capf)
        excl = incl - self_
        posb = jnp.where(sel, excl, -1.0).astype(I32)
        pos_ref[pl.ds(start, rb), :] = posb
        post_ref[c] = posb.T[0:N_EXPERTS, :]
        tbl_ref[pl.ds(c, 1), :] = sel_before.astype(I32)
        return (eq_incl[rb - 1:rb, :], raw_incl[rb - 1:rb, :])

    zero = jnp.zeros((1, lanes), F32)
    _, total = lax.fori_loop(0, nb, blk_step, (zero, zero))
    tbl_ref[pl.ds(nb, 1), :] = jnp.minimum(total, capf).astype(I32)


def _route_call(aff, cap, tag):
    n, lanes = aff.shape
    nb = n // ROUTE_BLOCK
    tbl_rows = -(-(nb + 1) // 8) * 8
    full = lambda *shape: pl.BlockSpec(shape, lambda: tuple(0 for _ in shape))
    return pl.pallas_call(
        functools.partial(_route_kernel, cap=cap),
        out_shape=(jax.ShapeDtypeStruct((n, lanes), I32),
                   jax.ShapeDtypeStruct((nb, N_EXPERTS, ROUTE_BLOCK), I32),
                   jax.ShapeDtypeStruct((tbl_rows, lanes), I32)),
        in_specs=[full(n, lanes)],
        out_specs=(full(n, lanes), full(nb, N_EXPERTS, ROUTE_BLOCK), full(tbl_rows, lanes)),
        scratch_shapes=[pltpu.VMEM((nb, N_EXPERTS, ROUTE_BLOCK), F32)],
        compiler_params=_cparams((), 8 * n * lanes * 4),
        name="route_" + tag,
    )(aff)


def _sc_gather_call(table, post, cap):
    n_exp, n = post.shape
    words = table.shape[1]
    workers = V7X_SC_CORES * V7X_SC_SUBCORES
    parts = workers // n_exp
    chunk = SC_GATHER_CHUNK
    lanes = V7X_SC_LANES
    per_part = cap // parts
    assert parts * n_exp == workers and per_part % chunk == 0 and n % lanes == 0
    mesh = plsc.VectorSubcoreMesh(core_axis_name="c", subcore_axis_name="s",
                                  num_cores=V7X_SC_CORES, num_subcores=V7X_SC_SUBCORES)

    def body(table_hbm, post_hbm, out_hbm, pos_v, idx_v, rows_v, sem):
        wid = lax.axis_index("s") * V7X_SC_CORES + lax.axis_index("c")
        e = wid // parts
        part = wid % parts
        pltpu.sync_copy(post_hbm.at[e], pos_v)
        lane = lax.iota(I32, lanes)

        @pl.loop(0, n, step=lanes)
        def _(t0):
            p = pos_v[pl.ds(t0, lanes)]
            plsc.store_scatter(idx_v, [p], lane + t0, mask=p >= 0)

        @pl.loop(0, per_part // chunk)
        def _(j):
            off = pl.multiple_of(part * per_part + j * chunk, chunk)
            pltpu.async_copy(table_hbm.at[idx_v.at[pl.ds(off, chunk)]], rows_v, sem).wait()
            pltpu.sync_copy(rows_v, out_hbm.at[pl.ds(e * cap + off, chunk)])

    return pl.kernel(
        body,
        out_type=jax.ShapeDtypeStruct((n_exp * cap, words), table.dtype),
        mesh=mesh,
        scratch_types=[pltpu.VMEM((n,), I32), pltpu.VMEM((cap,), I32),
                       pltpu.VMEM((chunk, words), table.dtype), pltpu.SemaphoreType.DMA],
        compiler_params=pltpu.CompilerParams(needs_layout_passes=False),
        name="sc_gather",
    )(table, post)


def _ffn_kernel(*refs, caps):
    ng = len(caps)
    xs_refs = refs[:ng]
    w1_ref, w3_ref, w2_ref = refs[ng:ng + 3]
    ye_refs = refs[ng + 3:2 * ng + 3]
    acc_scr = refs[2 * ng + 3]
    row_off = [sum(caps[:g]) for g in range(ng)]
    f = pl.program_id(1)

    def unpack(words):
        w = lax.bitcast_convert_type(words, jnp.uint32)
        lo = lax.bitcast_convert_type(w << 16, F32).astype(BF16)
        hi = lax.bitcast_convert_type(w & jnp.uint32(0xFFFF0000), F32).astype(BF16)
        return jnp.concatenate([lo, hi], axis=1)

    xs = jnp.concatenate([unpack(r[...]) for r in xs_refs], axis=0)
    hid = _silu(_dot(xs, w1_ref[...].astype(BF16))) * _dot(xs, w3_ref[...].astype(BF16))
    part = _dot(hid.astype(BF16), w2_ref[...].astype(BF16))

    @pl.when(f == 0)
    def _first():
        acc_scr[...] = part

    @pl.when(f > 0)
    def _rest():
        acc_scr[...] += part

    @pl.when(f == pl.num_programs(1) - 1)
    def _emit():
        for g in range(ng):
            ye_refs[g][...] = acc_scr[pl.ds(row_off[g], caps[g]), :].astype(BF16)


def _ffn_call(groups, w1, w3, w2):
    caps = tuple(g[1] for g in groups)
    n_exp, d, dff = w1.shape
    tf = dff // FFN_F_TILES
    rows = sum(caps)
    in_specs = [pl.BlockSpec((cap, d // 2), lambda e, f: (e, 0)) for cap in caps]
    in_specs += [pl.BlockSpec((None, d, tf), lambda e, f: (e, 0, f)),
                 pl.BlockSpec((None, d, tf), lambda e, f: (e, 0, f)),
                 pl.BlockSpec((None, tf, d), lambda e, f: (e, f, 0))]
    vmem = 3 * d * tf * (2 * 4 + 2) + rows * d * (2 * 2 + 2 + 4 + 2 * 2 + 4) + 4 * rows * tf * 4
    return pl.pallas_call(
        functools.partial(_ffn_kernel, caps=caps),
        out_shape=tuple(jax.ShapeDtypeStruct((n_exp * cap, d), BF16) for cap in caps),
        grid=(n_exp, FFN_F_TILES),
        in_specs=in_specs,
        out_specs=tuple(pl.BlockSpec((cap, d), lambda e, f: (e, 0)) for cap in caps),
        scratch_shapes=[pltpu.VMEM((rows, d), F32)],
        compiler_params=_cparams(("arbitrary", "arbitrary"), vmem),
        name="ffn",
    )(*_in_hbm([g[0] for g in groups] + [w1, w3, w2]))


def _combine_kernel(tbl_ref, ye_hbm, pos_ref, aff_ref, x1_ref, mod_ref, gpo_ref, y_ref, buf, sem, xbuf, xsem,
                    acc_scr, *, cap, blocks_per_tile):
    i = pl.program_id(0)
    nsteps = pl.num_programs(0)
    d = x1_ref.shape[1]
    lanes = pos_ref.shape[1]
    win = COMBINE_WINDOW
    last_start = ye_hbm.shape[0] - win
    slot = i % 2

    def first_row(step, e):
        return tbl_ref[step * blocks_per_tile, e] + e * cap

    def window_start(first, k):
        unclamped = (first // BF16_ROWS_PER_TILE) * BF16_ROWS_PER_TILE + k * win
        return unclamped, jnp.minimum(unclamped, last_start)

    def fetch(step, to_slot, e):
        start = window_start(first_row(step, e), 0)[1]
        return pltpu.make_async_copy(ye_hbm.at[pl.ds(pl.multiple_of(start, BF16_ROWS_PER_TILE), win), :],
                                     buf.at[to_slot, pl.ds(e * win, win), :], sem.at[to_slot, e])

    @pl.when(i == 0)
    def _prime():
        for e in range(N_EXPERTS):
            fetch(0, 0, e).start()

    @pl.when(i + 1 < nsteps)
    def _ahead():
        for e in range(N_EXPERTS):
            fetch(i + 1, 1 - slot, e).start()

    lane_row = lax.broadcasted_iota(I32, (1, win), 1)
    pieces = []
    for e in range(N_EXPERTS):
        pcol = pos_ref[:, e:e + 1]
        grow = jnp.where(pcol >= 0, pcol + e * cap, -1)
        start = window_start(first_row(i, e), 0)[1]
        pieces.append(jnp.where(grow == start + lane_row, aff_ref[:, e:e + 1], 0.0).astype(BF16))
    onehot = jnp.concatenate(pieces, axis=1)
    for e in range(N_EXPERTS):
        fetch(i, slot, e).wait()
    acc_scr[...] = _dot(onehot, buf[slot])

    def extra_windows(e):
        covered = window_start(first_row(i, e), 1)[0]
        return jnp.maximum(first_row(i + 1, e) - covered + win - 1, 0) // win

    def expert_extra(e, carry):
        first = first_row(i, e)
        extra = extra_windows(e)

        def more(k, c):
            unclamped, start = window_start(first, k)
            cp = pltpu.make_async_copy(ye_hbm.at[pl.ds(pl.multiple_of(start, BF16_ROWS_PER_TILE), win), :],
                                       xbuf, xsem)
            cp.start()
            cp.wait()
            at_e = lax.broadcasted_iota(I32, (1, lanes), 1) == e
            pcol = jnp.sum(jnp.where(at_e, pos_ref[...].astype(F32), 0.0), axis=1, keepdims=True).astype(I32)
            wcol = jnp.sum(jnp.where(at_e, aff_ref[...], 0.0), axis=1, keepdims=True)
            grow = jnp.where(pcol >= 0, pcol + e * cap, -1)
            hit = (grow == start + lane_row) & (grow >= unclamped)
            acc_scr[...] += _dot(jnp.where(hit, wcol, 0.0).astype(BF16), xbuf[...])
            return c

        lax.fori_loop(1, 1 + extra, more, 0)
        return carry

    any_extra = extra_windows(0)
    for e in range(1, N_EXPERTS):
        any_extra = any_extra + extra_windows(e)

    @pl.when(any_extra > 0)
    def _overflow():
        lax.fori_loop(0, N_EXPERTS, expert_extra, 0)

    m = mod_ref[0]
    y_ref[...] = x1_ref[...] + m[:, 5 * d:6 * d] * _rms(acc_scr[...], gpo_ref[...])


def _combine_call(tbl, ye, pos, aff, x1, mod3, mod_row_of_tile, g_po, cap, tag):
    n, d = x1.shape
    tm = COMBINE_TILE
    lanes = pos.shape[1]
    grid_spec = pltpu.PrefetchScalarGridSpec(
        num_scalar_prefetch=1,
        grid=(n // tm,),
        in_specs=[pl.BlockSpec(memory_space=pl.ANY),
                  pl.BlockSpec((tm, lanes), lambda i, t: (i, 0)),
                  pl.BlockSpec((tm, lanes), lambda i, t: (i, 0)),
                  pl.BlockSpec((tm, d), lambda i, t: (i, 0)),
                  pl.BlockSpec((1, 1, 6 * d), lambda i, t: (mod_row_of_tile(i, tm), 0, 0)),
                  pl.BlockSpec((1, d), lambda i, t: (0, 0))],
        out_specs=pl.BlockSpec((tm, d), lambda i, t: (i, 0)),
        scratch_shapes=[pltpu.VMEM((2, N_EXPERTS * COMBINE_WINDOW, d), BF16),
                        pltpu.SemaphoreType.DMA((2, N_EXPERTS)),
                        pltpu.VMEM((COMBINE_WINDOW, d), BF16),
                        pltpu.SemaphoreType.DMA(()),
                        pltpu.VMEM((tm, d), F32)],
    )
    vmem = 2 * N_EXPERTS * COMBINE_WINDOW * d * 2 + 10 * tm * d * 4 + 2 * tm * N_EXPERTS * COMBINE_WINDOW * 4
    return pl.pallas_call(
        functools.partial(_combine_kernel, cap=cap, blocks_per_tile=tm // ROUTE_BLOCK),
        out_shape=jax.ShapeDtypeStruct((n, d), F32),
        grid_spec=grid_spec,
        compiler_params=_cparams(("arbitrary",), vmem),
        name="combine_" + tag,
    )(tbl, *_in_hbm([ye, pos, aff, x1, mod3, g_po]))


def _rope_tables(seq):
    rows = seq // GRID_W
    r = jnp.repeat(jnp.arange(rows), GRID_W).astype(F32)
    col = jnp.tile(jnp.arange(GRID_W), rows).astype(F32)
    pairs = HEAD_DIM // 4
    freqs = ROPE_THETA ** (-jnp.arange(pairs, dtype=F32) / pairs)
    ang = jnp.concatenate([r[:, None] * freqs, col[:, None] * freqs], axis=-1)
    cos = jnp.repeat(jnp.cos(ang), 2, axis=-1)
    sin = jnp.repeat(jnp.sin(ang), 2, axis=-1)
    even = (jnp.arange(HEAD_DIM) % 2) == 0
    return cos, jnp.where(even, -sin, 0.0), jnp.where(even, 0.0, sin)


def _trunk_to_routing(x, mod3, mod_row_of_tile, rope_tabs, ctx, lw, tag):
    (g_pre_mix, g_post_mix, g_pre_ffn, g_post_ffn, w_in_p, g_q, g_k, wgf, bgf, wgb, bgb, g_gla,
     w_pa, w_pg, w_out, w_r, w1, w3, w2) = lw
    batch, seq, d = x.shape
    n = batch * seq
    x2 = x.reshape(n, d)
    (q_a, k_a, v_a, q_g, k_g, v_g, r_g, lg_f, lg_b, gates) = _inproj_call(
        x2, mod3, mod_row_of_tile, g_pre_mix, w_in_p, g_q, g_k, wgf, bgf, wgb, bgb, rope_tabs, batch, seq)
    if ctx is None:
        o_a = _attn_call(q_a, k_a, v_a, None, None)
        o_g, s_f, s_b = _gla_call(q_g, k_g, v_g, lg_f, lg_b, r_g, g_gla, None, None, batch, seq, True,
                                  GLA_CTX_HEADS_PER_STEP)
    else:
        ck, cv, s_f0, s_b0 = ctx
        o_a = _attn_call(q_a, k_a, v_a, ck, cv)
        (o_g,) = _gla_call(q_g, k_g, v_g, lg_f, lg_b, r_g, g_gla, s_f0, s_b0, batch, seq, False,
                           GLA_LAT_HEADS_PER_STEP)
        s_f = s_b = None
    x1, h, aff = _outproj_call(o_a, o_g, gates, x2, mod3, mod_row_of_tile, w_pa, w_pg, w_out,
                               g_post_mix, g_pre_ffn, w_r, tag)
    cap = (EC_CAPACITY_FACTOR * n) // N_EXPERTS
    pos, post, tbl = _route_call(aff, cap, tag)
    xs = _sc_gather_call(h, post.transpose(1, 0, 2).reshape(N_EXPERTS, n), cap)
    return dict(x1=x1, xs=xs, aff=aff, pos=pos, tbl=tbl, cap=cap,
                mod_row_of_tile=mod_row_of_tile, tag=tag, shape=(batch, seq, d)), (k_a, v_a, s_f, s_b)


def _expert_ffn(groups, mod3, g_post_ffn, w1, w3, w2):
    yes = _ffn_call([(g["xs"], g["cap"]) for g in groups], w1, w3, w2)
    outs = []
    for g, ye in zip(groups, yes):
        y = _combine_call(g["tbl"], ye, g["pos"], g["aff"], g["x1"], mod3, g["mod_row_of_tile"], g_post_ffn,
                          g["cap"], g["tag"])
        outs.append(y.reshape(g["shape"]))
    return outs


def kernel(x_prompt, x_sample, cache_k, cache_v, state_gla_fwd, state_gla_bwd, c, c_ctx, g_pre_mix, g_post_mix, g_pre_ffn, g_post_ffn, w_mod, b_mod, w_in, g_q, g_k, w_gk2_f, b_gk_f, w_gk2_b, b_gk_b, g_gla, w_pa, w_pg, w_out, w_router, w1, w3, w2):
    depth = w_in.shape[0]
    assert depth == 1, "single trunk layer"
    d = x_prompt.shape[-1]
    dec_batch, dec_seq, _ = x_sample.shape
    assert dec_batch + 1 <= MOD_ROWS
    l = 0
    rank = GLA_GATE_RANK
    w_in_p = jnp.swapaxes(w_in[l], 0, 1).astype(BF16)
    gkw = w_gk2_f.shape[-1]
    wgf = jnp.zeros((V7X_LANES, gkw), F32).at[0:rank].set(w_gk2_f[l]).astype(BF16)
    wgb = jnp.zeros((V7X_LANES, gkw), F32).at[rank:2 * rank].set(w_gk2_b[l]).astype(BF16)
    w_r = jnp.zeros((d, V7X_LANES), F32).at[:, :N_EXPERTS].set(w_router[l]).astype(BF16)
    row = lambda a: a[l].reshape(1, -1)
    lw = (row(g_pre_mix), row(g_post_mix), row(g_pre_ffn), row(g_post_ffn), w_in_p, row(g_q), row(g_k),
          wgf, row(b_gk_f), wgb, row(b_gk_b), row(g_gla),
          w_pa[l].astype(BF16), w_pg[l].astype(BF16), w_out[l].astype(BF16), w_r, w1[l], w3[l], w2[l])

    cc = jnp.concatenate([c_ctx[None, :], c, jnp.zeros((MOD_ROWS - 1 - dec_batch, d), F32)], axis=0)
    mod = _mod_call(cc, w_mod[l], b_mod[l].reshape(1, -1))
    mod3 = mod.reshape(MOD_ROWS, 1, 6 * d)

    gp, (nk, nv, nsf, nsb) = _trunk_to_routing(x_prompt, mod3, lambda i, tm: 0, None, None, lw, "ctx")
    ctx = (cache_k, cache_v, state_gla_fwd, state_gla_bwd)
    gs, _ = _trunk_to_routing(x_sample, mod3, lambda i, tm: 1 + (i * tm) // dec_seq, _rope_tables(dec_seq),
                              ctx, lw, "lat")
    yp, ys = _expert_ffn([gp, gs], mod3, lw[3], lw[16], lw[17], lw[18])
    return (yp, ys, nk, nv, nsf, nsb)
```

```python
import functools

import jax
import jax.numpy as jnp
from jax import lax
from jax.experimental import pallas as pl
from jax.experimental.pallas import tpu as pltpu
from jax.experimental.pallas import tpu_sc as plsc

F32 = jnp.float32
BF16 = jnp.bfloat16
I32 = jnp.int32

N_HEADS = 8
N_KV_HEADS = 2
HEAD_DIM = 128
GRID_W = 64
ROPE_THETA = 10000.0
GLA_HEADS = 4
GLA_GATE_RANK = 16
GLA_GATE_NORM = 16.0
GLA_CHUNK = 64
N_EXPERTS = 16
EC_CAPACITY_FACTOR = 2
EPS = 1e-6

V7X_LANES = 128
V7X_VMEM_BYTES = 64 * 1024 * 1024
V7X_VMEM_RESERVE_BYTES = 6 * 1024 * 1024
BF16_ROWS_PER_TILE = 16
V7X_SC_CORES = 2
V7X_SC_SUBCORES = 16
V7X_SC_LANES = 16

TOKEN_TILE = 512
OUTPROJ_TILE = 512
ATTN_Q_TILE = 256
GLA_BLOCK = 256
GLA_CTX_HEADS_PER_STEP = 4
GLA_LAT_HEADS_PER_STEP = 1
ROUTE_BLOCK = 256
SC_GATHER_CHUNK = 64
COMBINE_TILE = 512
COMBINE_WINDOW = 128
MOD_ROWS = 8
MOD_N_TILE = 1536


def _cparams(semantics, vmem_bytes):
    limit = min(max(int(vmem_bytes), 32 * 1024 * 1024), V7X_VMEM_BYTES - V7X_VMEM_RESERVE_BYTES)
    return pltpu.CompilerParams(dimension_semantics=semantics, vmem_limit_bytes=limit)


def _in_hbm(arrays):
    return [pltpu.with_memory_space_constraint(a, pltpu.HBM) if isinstance(a, jax.core.Tracer) else a
            for a in arrays]


def _sigmoid(x):
    return 1.0 / (1.0 + jnp.exp(-x))


def _silu(x):
    return x * _sigmoid(x)


def _log_sigmoid(x):
    return jnp.minimum(x, 0.0) - jnp.log1p(jnp.exp(-jnp.abs(x)))


def _rms(x, g):
    ms = jnp.mean(x * x, axis=-1, keepdims=True)
    return x * lax.rsqrt(ms + EPS) * g


def _dot(a, b):
    return jnp.dot(a, b, preferred_element_type=F32)


def _dot_nt(a, b):
    return lax.dot_general(a, b, (((1,), (1,)), ((), ())), preferred_element_type=F32)


def _mod_kernel(c_ref, w_ref, b_ref, o_ref):
    s = _silu(c_ref[...]).astype(BF16)
    o_ref[...] = _dot(s, w_ref[...].astype(BF16)) + b_ref[...]


def _mod_call(cc, w_mod, b_mod):
    d, n6 = w_mod.shape
    tn = MOD_N_TILE
    return pl.pallas_call(
        _mod_kernel,
        out_shape=jax.ShapeDtypeStruct((MOD_ROWS, n6), F32),
        grid=(n6 // tn,),
        in_specs=[pl.BlockSpec((MOD_ROWS, d), lambda j: (0, 0)),
                  pl.BlockSpec((d, tn), lambda j: (0, j)),
                  pl.BlockSpec((1, tn), lambda j: (0, j))],
        out_specs=pl.BlockSpec((MOD_ROWS, tn), lambda j: (0, j)),
        compiler_params=_cparams(("arbitrary",), 3 * d * tn * 4),
        name="mod",
    )(cc, w_mod, b_mod)


def _inproj_layout(d):
    aq, akv = N_HEADS * HEAD_DIM, N_KV_HEADS * HEAD_DIM
    gk, gv = d // 2, d
    names = ("q_a", "k_a", "v_a", "q_g", "k_g", "v_g", "r_g", "gk_f", "gk_b", "gates")
    widths = (aq, akv, akv, gk, gk, gv, gv, GLA_GATE_RANK, GLA_GATE_RANK, 2 * d)
    off, o = {}, 0
    for nme, w in zip(names, widths):
        off[nme] = (o, o + w)
        o += w
    off["gk"] = (off["gk_f"][0], off["gk_f"][0] + V7X_LANES)
    return off, o


def _inproj_kernel(*refs, rope, d):
    if rope:
        (x_ref, mod_ref, gpre_ref, w_ref, gq_ref, gk_ref, wgf_ref, bgf_ref, wgb_ref, bgb_ref,
         cos_ref, se_ref, so_ref, *outs) = refs
    else:
        (x_ref, mod_ref, gpre_ref, w_ref, gq_ref, gk_ref, wgf_ref, bgf_ref, wgb_ref, bgb_ref,
         *outs) = refs
    qa_ref, k_ref, v_ref, qg_ref, kg_ref, vg_ref, rg_ref, lgf_ref, lgb_ref, gate_ref = outs
    off, _ = _inproj_layout(d)
    m = mod_ref[0]
    h = _rms(x_ref[...], gpre_ref[...]) * (1.0 + m[:, d:2 * d]) + m[:, 0:d]
    hb = h.astype(BF16)

    def proj(name):
        a, b = off[name]
        return _dot_nt(hb, w_ref[a:b, :])

    def qk_norm(y, g_ref):
        y = _rms(y, g_ref[...])
        if rope:
            nxt = pltpu.roll(y, HEAD_DIM - 1, axis=1)
            prv = pltpu.roll(y, 1, axis=1)
            y = y * cos_ref[...] + nxt * se_ref[...] + prv * so_ref[...]
        return y

    q = proj("q_a")
    scale = HEAD_DIM ** -0.5
    for hd in range(N_HEADS):
        sl = slice(hd * HEAD_DIM, (hd + 1) * HEAD_DIM)
        qa_ref[:, sl] = (qk_norm(q[:, sl], gq_ref) * scale).astype(BF16)

    k = proj("k_a")
    v = proj("v_a")
    tb, _, _, ts, _ = k_ref.shape
    for kv in range(N_KV_HEADS):
        sl = slice(kv * HEAD_DIM, (kv + 1) * HEAD_DIM)
        k_ref[:, 0, kv] = qk_norm(k[:, sl], gk_ref).reshape(tb, ts, HEAD_DIM)
        v_ref[:, 0, kv] = v[:, sl].reshape(tb, ts, HEAD_DIM)

    dk = (d // 2) // GLA_HEADS
    qg_ref[...] = (proj("q_g") * (dk ** -0.5)).astype(BF16)
    kg_ref[...] = proj("k_g").astype(BF16)
    vg_ref[...] = proj("v_g").astype(BF16)
    rg_ref[...] = _silu(proj("r_g")).astype(BF16)

    gk = proj("gk").astype(BF16)
    lgf_ref[...] = _log_sigmoid(_dot(gk, wgf_ref[...]) + bgf_ref[...]) * (1.0 / GLA_GATE_NORM)
    lgb_ref[...] = _log_sigmoid(_dot(gk, wgb_ref[...]) + bgb_ref[...]) * (1.0 / GLA_GATE_NORM)

    gate_ref[...] = _sigmoid(proj("gates")).astype(BF16)


def _inproj_call(x2, mod3, mod_row_of_tile, g_pre, w_in_p, g_q, g_k, wgf, bgf, wgb, bgb, rope_tabs,
                 batch, seq):
    n, d = x2.shape
    tm = TOKEN_TILE
    _, dinp = _inproj_layout(d)
    rope = rope_tabs is not None
    gk_w = d // 2
    if seq >= tm:
        tb, ts, per = 1, tm, seq // tm
        kv_map = lambda i: (i // per, 0, 0, i % per, 0)
    else:
        tb, ts, per = tm // seq, seq, 1
        kv_map = lambda i: (i, 0, 0, 0, 0)
    row = lambda i: (i, 0)
    const = lambda i: (0, 0)
    in_specs = [
        pl.BlockSpec((tm, d), row),
        pl.BlockSpec((1, 1, 6 * d), lambda i: (mod_row_of_tile(i, tm), 0, 0)),
        pl.BlockSpec((1, d), const),
        pl.BlockSpec((dinp, d), const, pipeline_mode=pl.Buffered(1)),
        pl.BlockSpec((1, HEAD_DIM), const),
        pl.BlockSpec((1, HEAD_DIM), const),
        pl.BlockSpec((V7X_LANES, gk_w), const),
        pl.BlockSpec((1, gk_w), const),
        pl.BlockSpec((V7X_LANES, gk_w), const),
        pl.BlockSpec((1, gk_w), const),
    ]
    args = [x2, mod3, g_pre, w_in_p, g_q, g_k, wgf, bgf, wgb, bgb]
    if rope:
        tab = pl.BlockSpec((tm, HEAD_DIM), lambda i: (i % per, 0))
        in_specs += [tab, tab, tab]
        args += list(rope_tabs)
    kv_shape = jax.ShapeDtypeStruct((batch, 1, N_KV_HEADS, seq, HEAD_DIM), F32)
    kv_spec = pl.BlockSpec((tb, 1, N_KV_HEADS, ts, HEAD_DIM), kv_map)
    out_shape = (
        jax.ShapeDtypeStruct((n, N_HEADS * HEAD_DIM), BF16), kv_shape, kv_shape,
        jax.ShapeDtypeStruct((n, gk_w), BF16), jax.ShapeDtypeStruct((n, gk_w), BF16),
        jax.ShapeDtypeStruct((n, d), BF16), jax.ShapeDtypeStruct((n, d), BF16),
        jax.ShapeDtypeStruct((n, gk_w), F32), jax.ShapeDtypeStruct((n, gk_w), F32),
        jax.ShapeDtypeStruct((n, 2 * d), BF16),
    )
    out_specs = (
        pl.BlockSpec((tm, N_HEADS * HEAD_DIM), row), kv_spec, kv_spec,
        pl.BlockSpec((tm, gk_w), row), pl.BlockSpec((tm, gk_w), row),
        pl.BlockSpec((tm, d), row), pl.BlockSpec((tm, d), row),
        pl.BlockSpec((tm, gk_w), row), pl.BlockSpec((tm, gk_w), row),
        pl.BlockSpec((tm, 2 * d), row),
    )
    out_row_bytes = 2 * (N_HEADS * HEAD_DIM + 2 * gk_w + 2 * d + 2 * d) + 4 * (4 * HEAD_DIM + 2 * gk_w)
    vmem = d * dinp * 2 + 2 * tm * (d * 4 + out_row_bytes) + 6 * tm * 2 * d * 4
    return pl.pallas_call(
        functools.partial(_inproj_kernel, rope=rope, d=d),
        out_shape=out_shape,
        grid=(n // tm,),
        in_specs=in_specs,
        out_specs=out_specs,
        compiler_params=_cparams(("parallel",), vmem),
        name="inproj_lat" if rope else "inproj_ctx",
    )(*_in_hbm(args))


def _attn_kernel(*refs, cached):
    if cached:
        q_ref, k_ref, v_ref, ck_ref, cv_ref, o_ref = refs
    else:
        q_ref, k_ref, v_ref, o_ref = refs
    tq = q_ref.shape[0]
    grp = N_HEADS // N_KV_HEADS
    for kv in range(N_KV_HEADS):
        kk = k_ref[0, 0, kv].astype(BF16)
        vv = v_ref[0, 0, kv].astype(BF16)
        if cached:
            kk = jnp.concatenate([ck_ref[0, 0, kv].astype(BF16), kk], axis=0)
            vv = jnp.concatenate([cv_ref[0, 0, kv].astype(BF16), vv], axis=0)
        heads = [q_ref[:, (kv * grp + g) * HEAD_DIM:(kv * grp + g + 1) * HEAD_DIM] for g in range(grp)]
        q4 = jnp.concatenate(heads, axis=0)
        s = _dot_nt(q4, kk)
        p = jnp.exp(s - jnp.max(s, axis=-1, keepdims=True))
        l = jnp.sum(p, axis=-1, keepdims=True)
        o = _dot(p.astype(BF16), vv) / l
        for g in range(grp):
            hd = kv * grp + g
            o_ref[:, hd * HEAD_DIM:(hd + 1) * HEAD_DIM] = o[g * tq:(g + 1) * tq].astype(BF16)


def _attn_call(q_a, k_a, v_a, cache_k, cache_v):
    batch, _, _, seq, _ = k_a.shape
    n, aq = q_a.shape
    tq = ATTN_Q_TILE
    per = seq // tq
    cached = cache_k is not None
    own = pl.BlockSpec((1, 1, N_KV_HEADS, seq, HEAD_DIM), lambda b, j: (b, 0, 0, 0, 0))
    in_specs = [pl.BlockSpec((tq, aq), lambda b, j: (b * per + j, 0)), own, own]
    args = [q_a, k_a, v_a]
    klen = seq
    if cached:
        past = cache_k.shape[3]
        cspec = pl.BlockSpec((1, 1, N_KV_HEADS, past, HEAD_DIM), lambda b, j: (b, 0, 0, 0, 0))
        in_specs += [cspec, cspec]
        args += [cache_k, cache_v]
        klen += past
    grp = N_HEADS // N_KV_HEADS
    vmem = 4 * grp * tq * klen * 4 + 8 * N_KV_HEADS * klen * HEAD_DIM * 4 + 8 * tq * aq * 2
    return pl.pallas_call(
        functools.partial(_attn_kernel, cached=cached),
        out_shape=jax.ShapeDtypeStruct((n, aq), BF16),
        grid=(batch, per),
        in_specs=in_specs,
        out_specs=pl.BlockSpec((tq, aq), lambda b, j: (b * per + j, 0)),
        compiler_params=_cparams(("parallel", "parallel"), vmem),
        name="attn_lat" if cached else "attn_ctx",
    )(*_in_hbm(args))


def _gla_kernel(*refs, nblk, heads, has_state, emit_state):
    refs = list(refs)
    q_ref, k_ref, v_ref, lgf_ref, lgb_ref, rg_ref, gg_ref = refs[:7]
    pos = 7
    if has_state:
        s0f_ref, s0b_ref = refs[pos:pos + 2]
        pos += 2
    og_ref = refs[pos]
    pos += 1
    if emit_state:
        sf_ref, sb_ref = refs[pos:pos + 2]
        pos += 2
    of_scr, ob_scr = refs[pos:pos + 2]

    blk = GLA_BLOCK
    ch = GLA_CHUNK
    nch = blk // ch
    dk = q_ref.shape[1] // heads
    dv = v_ref.shape[1] // heads
    shift = ch.bit_length() - 1
    row_in_chunk = lax.broadcasted_iota(I32, (blk, dk), 0) & (ch - 1)
    ri = lax.broadcasted_iota(I32, (blk, blk), 0)
    ci = lax.broadcasted_iota(I32, (blk, blk), 1)
    same = (ri >> shift) == (ci >> shift)
    mask_f = same & (ci <= ri)
    mask_b = same & (ci >= ri)

    def one_block(b0, hd, state, reverse):
        rows = pl.ds(b0, blk)
        kcols = slice(hd * dk, (hd + 1) * dk)
        q = q_ref[rows, kcols].astype(F32)
        k = k_ref[rows, kcols].astype(F32)
        v = v_ref[rows, hd * dv:(hd + 1) * dv]
        b = (lgb_ref if reverse else lgf_ref)[rows, kcols]
        mask = mask_b if reverse else mask_f
        s = 1
        while s < ch:
            if reverse:
                sh = pltpu.roll(b, blk - s, axis=0)
                b = b + jnp.where(row_in_chunk < ch - s, sh, 0.0)
            else:
                sh = pltpu.roll(b, s, axis=0)
                b = b + jnp.where(row_in_chunk >= s, sh, 0.0)
            s *= 2
        qe = (q * jnp.exp(b)).astype(BF16)
        ke = (k * jnp.exp(-b)).astype(BF16)
        a = jnp.where(mask, _dot_nt(qe, ke), 0.0).astype(BF16)

        order = range(nch - 1, -1, -1) if reverse else range(nch)
        end_row = [c * ch + (0 if reverse else ch - 1) for c in range(nch)]
        ends = [b[r:r + 1, :] for r in end_row]
        b_end = jnp.concatenate([jnp.broadcast_to(e, (ch, dk)) for e in ends], axis=0)
        kd = (k * jnp.exp(b_end - b)).astype(BF16)
        intra = _dot(a, v)
        decay = jnp.exp(jnp.concatenate(ends + [jnp.zeros((dk - nch, dk), F32)], axis=0)).T
        inter = [None] * nch
        for c in order:
            crow = slice(c * ch, (c + 1) * ch)
            kv_c = lax.dot_general(kd[crow], v[crow], (((0,), (0,)), ((), ())), preferred_element_type=F32)
            if state is None:
                inter[c] = jnp.zeros((ch, dv), F32)
                state = kv_c
            else:
                inter[c] = _dot(qe[c * ch:(c + 1) * ch], state.astype(BF16))
                state = decay[:, c:c + 1] * state + kv_c
        return intra + jnp.concatenate(inter, axis=0), state

    for hd in range(heads):
        vcols = slice(hd * dv, (hd + 1) * dv)
        sf = s0f_ref[0, 0, hd] if has_state else None
        sb = s0b_ref[0, 0, hd] if has_state else None
        for i in range(nblk):
            j = nblk - 1 - i
            o_f, sf = one_block(i * blk, hd, sf, False)
            o_b, sb = one_block(j * blk, hd, sb, True)
            of_scr[pl.ds(i * blk, blk), vcols] = o_f
            ob_scr[pl.ds(j * blk, blk), vcols] = o_b
        if emit_state:
            sf_ref[0, 0, hd] = sf
            sb_ref[0, 0, hd] = sb
        o = of_scr[:, vcols] + ob_scr[:, vcols]
        og_ref[:, vcols] = (_rms(o, gg_ref[...]) * rg_ref[:, vcols].astype(F32)).astype(BF16)


def _gla_call(q_g, k_g, v_g, lg_f, lg_b, r_g, g_gla, state_f, state_b, batch, seq, emit_state, heads):
    n, gkw = q_g.shape
    d = v_g.shape[1]
    dk, dv = gkw // GLA_HEADS, d // GLA_HEADS
    has_state = state_f is not None
    kspec = pl.BlockSpec((seq, heads * dk), lambda b, h: (b, h))
    vspec = pl.BlockSpec((seq, heads * dv), lambda b, h: (b, h))
    sspec = pl.BlockSpec((1, 1, heads, dk, dv), lambda b, h: (b, 0, h, 0, 0))
    in_specs = [kspec, kspec, vspec, kspec, kspec, vspec, pl.BlockSpec((1, dv), lambda b, h: (0, 0))]
    args = [q_g, k_g, v_g, lg_f, lg_b, r_g, g_gla]
    if has_state:
        in_specs += [sspec, sspec]
        args += [state_f, state_b]
    out_shape = [jax.ShapeDtypeStruct((n, d), BF16)]
    out_specs = [vspec]
    if emit_state:
        st = jax.ShapeDtypeStruct((batch, 1, GLA_HEADS, dk, dv), F32)
        out_shape += [st, st]
        out_specs += [sspec, sspec]
    vmem = 12 * seq * heads * dv * 4 + 40 * GLA_BLOCK * GLA_BLOCK * 4
    return pl.pallas_call(
        functools.partial(_gla_kernel, nblk=seq // GLA_BLOCK, heads=heads, has_state=has_state,
                          emit_state=emit_state),
        out_shape=tuple(out_shape),
        grid=(batch, GLA_HEADS // heads),
        in_specs=in_specs,
        out_specs=tuple(out_specs),
        scratch_shapes=[pltpu.VMEM((seq, heads * dv), F32), pltpu.VMEM((seq, heads * dv), F32)],
        compiler_params=_cparams(("parallel", "parallel"), vmem),
        name="gla_ctx" if emit_state else "gla_lat",
    )(*_in_hbm(args))


def _outproj_kernel(oa_ref, og_ref, gate_ref, x_ref, mod_ref, wpa_ref, wpg_ref, wout_ref, gpm_ref,
                    gpf_ref, wr_ref, x1_ref, h_ref, aff_ref):
    d = x_ref.shape[1]
    m = mod_ref[0]
    oa = _dot(oa_ref[...], wpa_ref[...])
    og = _dot(og_ref[...], wpg_ref[...])
    mix = gate_ref[:, 0:d].astype(F32) * oa + gate_ref[:, d:2 * d].astype(F32) * og
    mo = _dot(mix.astype(BF16), wout_ref[...])
    x1 = x_ref[...] + m[:, 2 * d:3 * d] * _rms(mo, gpm_ref[...])
    x1_ref[...] = x1
    hb = (_rms(x1, gpf_ref[...]) * (1.0 + m[:, 4 * d:5 * d]) + m[:, 3 * d:4 * d]).astype(BF16)
    bits = lax.bitcast_convert_type(hb.astype(F32), jnp.uint32)
    packed = (bits[:, 0:d // 2] >> 16) | (bits[:, d // 2:d] & jnp.uint32(0xFFFF0000))
    h_ref[...] = lax.bitcast_convert_type(packed, I32)
    logits = _dot(hb, wr_ref[...])
    valid = lax.broadcasted_iota(I32, logits.shape, 1) < N_EXPERTS
    mx = jnp.max(jnp.where(valid, logits, -jnp.inf), axis=-1, keepdims=True)
    ex = jnp.where(valid, jnp.exp(logits - mx), 0.0)
    aff_ref[...] = ex / jnp.sum(ex, axis=-1, keepdims=True)


def _outproj_call(o_a, o_g, gates, x2, mod3, mod_row_of_tile, w_pa, w_pg, w_out, g_pm, g_pf, w_r, tag):
    n, d = x2.shape
    tm = OUTPROJ_TILE
    row = lambda i: (i, 0)
    const = lambda i: (0, 0)
    wspec = pl.BlockSpec((d, d), const, pipeline_mode=pl.Buffered(1))
    vmem = 3 * d * d * 2 + 2 * tm * d * (2 + 2 + 4 + 4 + 4 + 2) + 8 * tm * d * 4
    return pl.pallas_call(
        _outproj_kernel,
        out_shape=(jax.ShapeDtypeStruct((n, d), F32), jax.ShapeDtypeStruct((n, d // 2), I32),
                   jax.ShapeDtypeStruct((n, V7X_LANES), F32)),
        grid=(n // tm,),
        in_specs=[pl.BlockSpec((tm, d), row), pl.BlockSpec((tm, d), row), pl.BlockSpec((tm, 2 * d), row),
                  pl.BlockSpec((tm, d), row),
                  pl.BlockSpec((1, 1, 6 * d), lambda i: (mod_row_of_tile(i, tm), 0, 0)),
                  wspec, wspec, wspec, pl.BlockSpec((1, d), const), pl.BlockSpec((1, d), const),
                  pl.BlockSpec((d, V7X_LANES), const)],
        out_specs=(pl.BlockSpec((tm, d), row), pl.BlockSpec((tm, d // 2), row),
                   pl.BlockSpec((tm, V7X_LANES), row)),
        compiler_params=_cparams(("parallel",), vmem),
        name="outproj_" + tag,
    )(*_in_hbm([o_a, o_g, gates, x2, mod3, w_pa, w_pg, w_out, g_pm, g_pf, w_r]))


def _route_kernel(aff_ref, pos_ref, post_ref, tbl_ref, afft_scr, *, cap):
    n = aff_ref.shape[0]
    rb = ROUTE_BLOCK
    nb = n // rb
    lanes = aff_ref.shape[1]
    lane = lax.broadcasted_iota(I32, (1, lanes), 1)
    expert_lane = lane < N_EXPERTS
    tbl_ref[...] = jnp.zeros(tbl_ref.shape, I32)

    def to_token_lanes(c, carry):
        start = pl.multiple_of(c * rb, rb)
        afft_scr[c] = aff_ref[pl.ds(start, rb), :].T[0:N_EXPERTS, :]
        return carry

    lax.fori_loop(0, nb, to_token_lanes, 0)
    aff_t = afft_scr[...]

    def count(hit):
        return jnp.sum(jnp.sum(hit.astype(I32), axis=0), axis=1, keepdims=True)

    def bit_step(i, lo):
        t = lo | jnp.left_shift(jnp.int32(1), 30 - i)
        ge = aff_t >= lax.bitcast_convert_type(t, F32)[None]
        return jnp.where(count(ge) >= cap, t, lo)

    thr_bits = lax.fori_loop(0, 31, bit_step, jnp.zeros((N_EXPERTS, 1), I32))
    need_t = cap - count(aff_t > lax.bitcast_convert_type(thr_bits, F32)[None])

    def to_expert_lanes(col):
        full = jnp.concatenate([jnp.broadcast_to(col, (N_EXPERTS, lanes)),
                                jnp.zeros((lanes - N_EXPERTS, lanes), I32)], axis=0)
        return full.T[0:1, :]

    thr = lax.bitcast_convert_type(to_expert_lanes(thr_bits), F32)
    need = to_expert_lanes(need_t).astype(F32)
    capf = float(cap)

    r = lax.broadcasted_iota(I32, (rb, rb), 0)
    c_ = lax.broadcasted_iota(I32, (rb, rb), 1)
    tril = jnp.where(c_ <= r, 1.0, 0.0).astype(BF16)

    def blk_step(c, carry):
        eq_before, raw_before = carry
        start = pl.multiple_of(c * rb, rb)
        a = aff_ref[pl.ds(start, rb), :]
        gt = a > thr
        eq = a == thr
        eq_incl = _dot(tril, jnp.where(eq, 1.0, 0.0).astype(BF16)) + eq_before
        raw = (gt | (eq & (eq_incl <= need))) & expert_lane
        raw_incl = _dot(tril, jnp.where(raw, 1.0, 0.0).astype(BF16)) + raw_before
        sel = raw & (raw_incl <= capf)
        self_ = jnp.where(sel, 1.0, 0.0)
        incl = jnp.minimum(raw_incl, capf)
        sel_before = jnp.minimum(raw_before, capf)
        excl = incl - self_
        posb = jnp.where(sel, excl, -1.0).astype(I32)
        pos_ref[pl.ds(start, rb), :] = posb
        post_ref[c] = posb.T[0:N_EXPERTS, :]
        tbl_ref[pl.ds(c, 1), :] = sel_before.astype(I32)
        return (eq_incl[rb - 1:rb, :], raw_incl[rb - 1:rb, :])

    zero = jnp.zeros((1, lanes), F32)
    _, total = lax.fori_loop(0, nb, blk_step, (zero, zero))
    tbl_ref[pl.ds(nb, 1), :] = jnp.minimum(total, capf).astype(I32)


def _route_call(aff, cap, tag):
    n, lanes = aff.shape
    nb = n // ROUTE_BLOCK
    tbl_rows = -(-(nb + 1) // 8) * 8
    full = lambda *shape: pl.BlockSpec(shape, lambda: tuple(0 for _ in shape))
    return pl.pallas_call(
        functools.partial(_route_kernel, cap=cap),
        out_shape=(jax.ShapeDtypeStruct((n, lanes), I32),
                   jax.ShapeDtypeStruct((nb, N_EXPERTS, ROUTE_BLOCK), I32),
                   jax.ShapeDtypeStruct((tbl_rows, lanes), I32)),
        in_specs=[full(n, lanes)],
        out_specs=(full(n, lanes), full(nb, N_EXPERTS, ROUTE_BLOCK), full(tbl_rows, lanes)),
        scratch_shapes=[pltpu.VMEM((nb, N_EXPERTS, ROUTE_BLOCK), F32)],
        compiler_params=_cparams((), 8 * n * lanes * 4),
        name="route_" + tag,
    )(aff)


def _sc_gather_call(table, post, cap):
    n_exp, n = post.shape
    words = table.shape[1]
    workers = V7X_SC_CORES * V7X_SC_SUBCORES
    parts = workers // n_exp
    chunk = SC_GATHER_CHUNK
    lanes = V7X_SC_LANES
    per_part = cap // parts
    assert parts * n_exp == workers and per_part % chunk == 0 and n % lanes == 0
    mesh = plsc.VectorSubcoreMesh(core_axis_name="c", subcore_axis_name="s",
                                  num_cores=V7X_SC_CORES, num_subcores=V7X_SC_SUBCORES)

    def body(table_hbm, post_hbm, out_hbm, pos_v, idx_v, rows_v, sem):
        wid = lax.axis_index("s") * V7X_SC_CORES + lax.axis_index("c")
        e = wid // parts
        part = wid % parts
        pltpu.sync_copy(post_hbm.at[e], pos_v)
        lane = lax.iota(I32, lanes)

        @pl.loop(0, n, step=lanes)
        def _(t0):
            p = pos_v[pl.ds(t0, lanes)]
            plsc.store_scatter(idx_v, [p], lane + t0, mask=p >= 0)

        @pl.loop(0, per_part // chunk)
        def _(j):
            off = pl.multiple_of(part * per_part + j * chunk, chunk)
            pltpu.async_copy(table_hbm.at[idx_v.at[pl.ds(off, chunk)]], rows_v, sem).wait()
            pltpu.sync_copy(rows_v, out_hbm.at[pl.ds(e * cap + off, chunk)])

    return pl.kernel(
        body,
        out_type=jax.ShapeDtypeStruct((n_exp * cap, words), table.dtype),
        mesh=mesh,
        scratch_types=[pltpu.VMEM((n,), I32), pltpu.VMEM((cap,), I32),
                       pltpu.VMEM((chunk, words), table.dtype), pltpu.SemaphoreType.DMA],
        compiler_params=pltpu.CompilerParams(needs_layout_passes=False),
        name="sc_gather",
    )(table, post)


def _ffn_kernel(*refs, caps):
    ng = len(caps)
    xs_refs = refs[:ng]
    w1_ref, w3_ref, w2_ref = refs[ng:ng + 3]
    ye_refs = refs[ng + 3:2 * ng + 3]
    row_off = [sum(caps[:g]) for g in range(ng)]

    def unpack(words):
        w = lax.bitcast_convert_type(words, jnp.uint32)
        lo = lax.bitcast_convert_type(w << 16, F32).astype(BF16)
        hi = lax.bitcast_convert_type(w & jnp.uint32(0xFFFF0000), F32).astype(BF16)
        return jnp.concatenate([lo, hi], axis=1)

    xs = jnp.concatenate([unpack(r[...]) for r in xs_refs], axis=0)
    hid = _silu(_dot(xs, w1_ref[...].astype(BF16))) * _dot(xs, w3_ref[...].astype(BF16))
    ye = _dot(hid.astype(BF16), w2_ref[...].astype(BF16)).astype(BF16)
    for g in range(ng):
        ye_refs[g][...] = ye[row_off[g]:row_off[g] + caps[g]]


def _ffn_call(groups, w1, w3, w2):
    caps = tuple(g[1] for g in groups)
    n_exp, d, dff = w1.shape
    rows = sum(caps)
    in_specs = [pl.BlockSpec((cap, d // 2), lambda e: (e, 0)) for cap in caps]
    in_specs += [pl.BlockSpec((None, d, dff), lambda e: (e, 0, 0)),
                 pl.BlockSpec((None, d, dff), lambda e: (e, 0, 0)),
                 pl.BlockSpec((None, dff, d), lambda e: (e, 0, 0))]
    vmem = 3 * d * dff * (2 * 4 + 2) + rows * d * (2 * 2 + 2 + 2 * 2) + 5 * rows * dff * 4
    return pl.pallas_call(
        functools.partial(_ffn_kernel, caps=caps),
        out_shape=tuple(jax.ShapeDtypeStruct((n_exp * cap, d), BF16) for cap in caps),
        grid=(n_exp,),
        in_specs=in_specs,
        out_specs=tuple(pl.BlockSpec((cap, d), lambda e: (e, 0)) for cap in caps),
        compiler_params=_cparams(("arbitrary",), vmem),
        name="ffn",
    )(*_in_hbm([g[0] for g in groups] + [w1, w3, w2]))


def _combine_kernel(tbl_ref, ye_hbm, pos_ref, aff_ref, x1_ref, mod_ref, gpo_ref, y_ref, buf, sem, xbuf, xsem,
                    acc_scr, *, cap, blocks_per_tile):
    i = pl.program_id(0)
    nsteps = pl.num_programs(0)
    d = x1_ref.shape[1]
    lanes = pos_ref.shape[1]
    win = COMBINE_WINDOW
    last_start = ye_hbm.shape[0] - win
    slot = i % 2

    def first_row(step, e):
        return tbl_ref[step * blocks_per_tile, e] + e * cap

    def window_start(first, k):
        unclamped = (first // BF16_ROWS_PER_TILE) * BF16_ROWS_PER_TILE + k * win
        return unclamped, jnp.minimum(unclamped, last_start)

    def fetch(step, to_slot, e):
        start = window_start(first_row(step, e), 0)[1]
        return pltpu.make_async_copy(ye_hbm.at[pl.ds(pl.multiple_of(start, BF16_ROWS_PER_TILE), win), :],
                                     buf.at[to_slot, pl.ds(e * win, win), :], sem.at[to_slot, e])

    @pl.when(i == 0)
    def _prime():
        for e in range(N_EXPERTS):
            fetch(0, 0, e).start()

    @pl.when(i + 1 < nsteps)
    def _ahead():
        for e in range(N_EXPERTS):
            fetch(i + 1, 1 - slot, e).start()

    lane_row = lax.broadcasted_iota(I32, (1, win), 1)
    pieces = []
    for e in range(N_EXPERTS):
        pcol = pos_ref[:, e:e + 1]
        grow = jnp.where(pcol >= 0, pcol + e * cap, -1)
        start = window_start(first_row(i, e), 0)[1]
        pieces.append(jnp.where(grow == start + lane_row, aff_ref[:, e:e + 1], 0.0).astype(BF16))
    onehot = jnp.concatenate(pieces, axis=1)
    for e in range(N_EXPERTS):
        fetch(i, slot, e).wait()
    acc_scr[...] = _dot(onehot, buf[slot])

    def extra_windows(e):
        covered = window_start(first_row(i, e), 1)[0]
        return jnp.maximum(first_row(i + 1, e) - covered + win - 1, 0) // win

    def expert_extra(e, carry):
        first = first_row(i, e)
        extra = extra_windows(e)

        def more(k, c):
            unclamped, start = window_start(first, k)
            cp = pltpu.make_async_copy(ye_hbm.at[pl.ds(pl.multiple_of(start, BF16_ROWS_PER_TILE), win), :],
                                       xbuf, xsem)
            cp.start()
            cp.wait()
            at_e = lax.broadcasted_iota(I32, (1, lanes), 1) == e
            pcol = jnp.sum(jnp.where(at_e, pos_ref[...].astype(F32), 0.0), axis=1, keepdims=True).astype(I32)
            wcol = jnp.sum(jnp.where(at_e, aff_ref[...], 0.0), axis=1, keepdims=True)
            grow = jnp.where(pcol >= 0, pcol + e * cap, -1)
            hit = (grow == start + lane_row) & (grow >= unclamped)
            acc_scr[...] += _dot(jnp.where(hit, wcol, 0.0).astype(BF16), xbuf[...])
            return c

        lax.fori_loop(1, 1 + extra, more, 0)
        return carry

    any_extra = extra_windows(0)
    for e in range(1, N_EXPERTS):
        any_extra = any_extra + extra_windows(e)

    @pl.when(any_extra > 0)
    def _overflow():
        lax.fori_loop(0, N_EXPERTS, expert_extra, 0)

    m = mod_ref[0]
    y_ref[...] = x1_ref[...] + m[:, 5 * d:6 * d] * _rms(acc_scr[...], gpo_ref[...])


def _combine_call(tbl, ye, pos, aff, x1, mod3, mod_row_of_tile, g_po, cap, tag):
    n, d = x1.shape
    tm = COMBINE_TILE
    lanes = pos.shape[1]
    grid_spec = pltpu.PrefetchScalarGridSpec(
        num_scalar_prefetch=1,
        grid=(n // tm,),
        in_specs=[pl.BlockSpec(memory_space=pl.ANY),
                  pl.BlockSpec((tm, lanes), lambda i, t: (i, 0)),
                  pl.BlockSpec((tm, lanes), lambda i, t: (i, 0)),
                  pl.BlockSpec((tm, d), lambda i, t: (i, 0)),
                  pl.BlockSpec((1, 1, 6 * d), lambda i, t: (mod_row_of_tile(i, tm), 0, 0)),
                  pl.BlockSpec((1, d), lambda i, t: (0, 0))],
        out_specs=pl.BlockSpec((tm, d), lambda i, t: (i, 0)),
        scratch_shapes=[pltpu.VMEM((2, N_EXPERTS * COMBINE_WINDOW, d), BF16),
                        pltpu.SemaphoreType.DMA((2, N_EXPERTS)),
                        pltpu.VMEM((COMBINE_WINDOW, d), BF16),
                        pltpu.SemaphoreType.DMA(()),
                        pltpu.VMEM((tm, d), F32)],
    )
    vmem = 2 * N_EXPERTS * COMBINE_WINDOW * d * 2 + 10 * tm * d * 4 + 2 * tm * N_EXPERTS * COMBINE_WINDOW * 4
    return pl.pallas_call(
        functools.partial(_combine_kernel, cap=cap, blocks_per_tile=tm // ROUTE_BLOCK),
        out_shape=jax.ShapeDtypeStruct((n, d), F32),
        grid_spec=grid_spec,
        compiler_params=_cparams(("arbitrary",), vmem),
        name="combine_" + tag,
    )(tbl, *_in_hbm([ye, pos, aff, x1, mod3, g_po]))


def _rope_tables(seq):
    rows = seq // GRID_W
    r = jnp.repeat(jnp.arange(rows), GRID_W).astype(F32)
    col = jnp.tile(jnp.arange(GRID_W), rows).astype(F32)
    pairs = HEAD_DIM // 4
    freqs = ROPE_THETA ** (-jnp.arange(pairs, dtype=F32) / pairs)
    ang = jnp.concatenate([r[:, None] * freqs, col[:, None] * freqs], axis=-1)
    cos = jnp.repeat(jnp.cos(ang), 2, axis=-1)
    sin = jnp.repeat(jnp.sin(ang), 2, axis=-1)
    even = (jnp.arange(HEAD_DIM) % 2) == 0
    return cos, jnp.where(even, -sin, 0.0), jnp.where(even, 0.0, sin)


def _trunk_to_routing(x, mod3, mod_row_of_tile, rope_tabs, ctx, lw, tag):
    (g_pre_mix, g_post_mix, g_pre_ffn, g_post_ffn, w_in_p, g_q, g_k, wgf, bgf, wgb, bgb, g_gla,
     w_pa, w_pg, w_out, w_r, w1, w3, w2) = lw
    batch, seq, d = x.shape
    n = batch * seq
    x2 = x.reshape(n, d)
    (q_a, k_a, v_a, q_g, k_g, v_g, r_g, lg_f, lg_b, gates) = _inproj_call(
        x2, mod3, mod_row_of_tile, g_pre_mix, w_in_p, g_q, g_k, wgf, bgf, wgb, bgb, rope_tabs, batch, seq)
    if ctx is None:
        o_a = _attn_call(q_a, k_a, v_a, None, None)
        o_g, s_f, s_b = _gla_call(q_g, k_g, v_g, lg_f, lg_b, r_g, g_gla, None, None, batch, seq, True,
                                  GLA_CTX_HEADS_PER_STEP)
    else:
        ck, cv, s_f0, s_b0 = ctx
        o_a = _attn_call(q_a, k_a, v_a, ck, cv)
        (o_g,) = _gla_call(q_g, k_g, v_g, lg_f, lg_b, r_g, g_gla, s_f0, s_b0, batch, seq, False,
                           GLA_LAT_HEADS_PER_STEP)
        s_f = s_b = None
    x1, h, aff = _outproj_call(o_a, o_g, gates, x2, mod3, mod_row_of_tile, w_pa, w_pg, w_out,
                               g_post_mix, g_pre_ffn, w_r, tag)
    cap = (EC_CAPACITY_FACTOR * n) // N_EXPERTS
    pos, post, tbl = _route_call(aff, cap, tag)
    xs = _sc_gather_call(h, post.transpose(1, 0, 2).reshape(N_EXPERTS, n), cap)
    return dict(x1=x1, xs=xs, aff=aff, pos=pos, tbl=tbl, cap=cap,
                mod_row_of_tile=mod_row_of_tile, tag=tag, shape=(batch, seq, d)), (k_a, v_a, s_f, s_b)


def _expert_ffn(groups, mod3, g_post_ffn, w1, w3, w2):
    yes = _ffn_call([(g["xs"], g["cap"]) for g in groups], w1, w3, w2)
    outs = []
    for g, ye in zip(groups, yes):
        y = _combine_call(g["tbl"], ye, g["pos"], g["aff"], g["x1"], mod3, g["mod_row_of_tile"], g_post_ffn,
                          g["cap"], g["tag"])
        outs.append(y.reshape(g["shape"]))
    return outs


def kernel(x_prompt, x_sample, cache_k, cache_v, state_gla_fwd, state_gla_bwd, c, c_ctx, g_pre_mix, g_post_mix, g_pre_ffn, g_post_ffn, w_mod, b_mod, w_in, g_q, g_k, w_gk2_f, b_gk_f, w_gk2_b, b_gk_b, g_gla, w_pa, w_pg, w_out, w_router, w1, w3, w2):
    depth = w_in.shape[0]
    assert depth == 1, "single trunk layer"
    d = x_prompt.shape[-1]
    dec_batch, dec_seq, _ = x_sample.shape
    assert dec_batch + 1 <= MOD_ROWS
    l = 0
    rank = GLA_GATE_RANK
    w_in_p = jnp.swapaxes(w_in[l], 0, 1).astype(BF16)
    gkw = w_gk2_f.shape[-1]
    wgf = jnp.zeros((V7X_LANES, gkw), F32).at[0:rank].set(w_gk2_f[l]).astype(BF16)
    wgb = jnp.zeros((V7X_LANES, gkw), F32).at[rank:2 * rank].set(w_gk2_b[l]).astype(BF16)
    w_r = jnp.zeros((d, V7X_LANES), F32).at[:, :N_EXPERTS].set(w_router[l]).astype(BF16)
    row = lambda a: a[l].reshape(1, -1)
    lw = (row(g_pre_mix), row(g_post_mix), row(g_pre_ffn), row(g_post_ffn), w_in_p, row(g_q), row(g_k),
          wgf, row(b_gk_f), wgb, row(b_gk_b), row(g_gla),
          w_pa[l].astype(BF16), w_pg[l].astype(BF16), w_out[l].astype(BF16), w_r, w1[l], w3[l], w2[l])

    cc = jnp.concatenate([c_ctx[None, :], c, jnp.zeros((MOD_ROWS - 1 - dec_batch, d), F32)], axis=0)
    mod = _mod_call(cc, w_mod[l], b_mod[l].reshape(1, -1))
    mod3 = mod.reshape(MOD_ROWS, 1, 6 * d)

    gp, (nk, nv, nsf, nsb) = _trunk_to_routing(x_prompt, mod3, lambda i, tm: 0, None, None, lw, "ctx")
    ctx = (cache_k, cache_v, state_gla_fwd, state_gla_bwd)
    gs, _ = _trunk_to_routing(x_sample, mod3, lambda i, tm: 1 + (i * tm) // dec_seq, _rope_tables(dec_seq),
                              ctx, lw, "lat")
    yp, ys = _expert_ffn([gp, gs], mod3, lw[3], lw[16], lw[17], lw[18])
    return (yp, ys, nk, nv, nsf, nsb)
```

```python
import functools

import jax
import jax.numpy as jnp
from jax import lax
from jax.experimental import pallas as pl
from jax.experimental.pallas import tpu as pltpu
from jax.experimental.pallas import tpu_sc as plsc

F32 = jnp.float32
BF16 = jnp.bfloat16
I32 = jnp.int32

N_HEADS = 8
N_KV_HEADS = 2
HEAD_DIM = 128
GRID_W = 64
ROPE_THETA = 10000.0
GLA_HEADS = 4
GLA_GATE_RANK = 16
GLA_GATE_NORM = 16.0
GLA_CHUNK = 64
N_EXPERTS = 16
EC_CAPACITY_FACTOR = 2
EPS = 1e-6

V7X_LANES = 128
V7X_VMEM_BYTES = 64 * 1024 * 1024
V7X_VMEM_RESERVE_BYTES = 6 * 1024 * 1024
BF16_ROWS_PER_TILE = 16
V7X_SC_CORES = 2
V7X_SC_SUBCORES = 16
V7X_SC_LANES = 16

TOKEN_TILE = 512
OUTPROJ_TILE = 512
ATTN_Q_TILE = 256
GLA_BLOCK = 256
GLA_CTX_HEADS_PER_STEP = 4
GLA_LAT_HEADS_PER_STEP = 2
ROUTE_BLOCK = 256
SC_GATHER_CHUNK = 64
COMBINE_TILE = 512
COMBINE_WINDOW = 128
MOD_ROWS = 8
MOD_N_TILE = 1536


def _cparams(semantics, vmem_bytes):
    del vmem_bytes
    return pltpu.CompilerParams(dimension_semantics=semantics,
                                vmem_limit_bytes=V7X_VMEM_BYTES - V7X_VMEM_RESERVE_BYTES)


def _sigmoid(x):
    return 1.0 / (1.0 + jnp.exp(-x))


def _silu(x):
    return x * _sigmoid(x)


def _log_sigmoid(x):
    return jnp.minimum(x, 0.0) - jnp.log1p(jnp.exp(-jnp.abs(x)))


def _rms(x, g):
    ms = jnp.mean(x * x, axis=-1, keepdims=True)
    return x * lax.rsqrt(ms + EPS) * g


def _dot(a, b):
    return jnp.dot(a, b, preferred_element_type=F32)


def _dot_nt(a, b):
    return lax.dot_general(a, b, (((1,), (1,)), ((), ())), preferred_element_type=F32)


def _mod_kernel(c_ref, w_ref, b_ref, o_ref):
    s = _silu(c_ref[...]).astype(BF16)
    o_ref[...] = _dot(s, w_ref[...].astype(BF16)) + b_ref[...]


def _mod_call(cc, w_mod, b_mod):
    d, n6 = w_mod.shape
    tn = MOD_N_TILE
    return pl.pallas_call(
        _mod_kernel,
        out_shape=jax.ShapeDtypeStruct((MOD_ROWS, n6), F32),
        grid=(n6 // tn,),
        in_specs=[pl.BlockSpec((MOD_ROWS, d), lambda j: (0, 0)),
                  pl.BlockSpec((d, tn), lambda j: (0, j)),
                  pl.BlockSpec((1, tn), lambda j: (0, j))],
        out_specs=pl.BlockSpec((MOD_ROWS, tn), lambda j: (0, j)),
        compiler_params=_cparams(("arbitrary",), 3 * d * tn * 4),
        name="mod",
    )(cc, w_mod, b_mod)


def _inproj_layout(d):
    aq, akv = N_HEADS * HEAD_DIM, N_KV_HEADS * HEAD_DIM
    gk, gv = d // 2, d
    names = ("q_a", "k_a", "v_a", "q_g", "k_g", "v_g", "r_g", "gk_f", "gk_b", "gates")
    widths = (aq, akv, akv, gk, gk, gv, gv, GLA_GATE_RANK, GLA_GATE_RANK, 2 * d)
    off, o = {}, 0
    for nme, w in zip(names, widths):
        off[nme] = (o, o + w)
        o += w
    off["gk"] = (off["gk_f"][0], off["gk_f"][0] + V7X_LANES)
    return off, o


def _inproj_kernel(*refs, rope, d):
    if rope:
        (x_ref, mod_ref, gpre_ref, w_ref, gq_ref, gk_ref, wgf_ref, bgf_ref, wgb_ref, bgb_ref,
         cos_ref, se_ref, so_ref, *outs) = refs
    else:
        (x_ref, mod_ref, gpre_ref, w_ref, gq_ref, gk_ref, wgf_ref, bgf_ref, wgb_ref, bgb_ref,
         *outs) = refs
    qa_ref, k_ref, v_ref, qg_ref, kg_ref, vg_ref, rg_ref, lgf_ref, lgb_ref, gate_ref = outs
    off, _ = _inproj_layout(d)
    m = mod_ref[0]
    h = _rms(x_ref[...], gpre_ref[...]) * (1.0 + m[:, d:2 * d]) + m[:, 0:d]
    hb = h.astype(BF16)

    def proj(name):
        a, b = off[name]
        return _dot_nt(hb, w_ref[a:b, :])

    def qk_norm(y, g_ref):
        y = _rms(y, g_ref[...])
        if rope:
            nxt = pltpu.roll(y, HEAD_DIM - 1, axis=1)
            prv = pltpu.roll(y, 1, axis=1)
            y = y * cos_ref[...] + nxt * se_ref[...] + prv * so_ref[...]
        return y

    q = proj("q_a")
    scale = HEAD_DIM ** -0.5
    for hd in range(N_HEADS):
        sl = slice(hd * HEAD_DIM, (hd + 1) * HEAD_DIM)
        qa_ref[:, sl] = (qk_norm(q[:, sl], gq_ref) * scale).astype(BF16)

    k = proj("k_a")
    v = proj("v_a")
    tb, _, _, ts, _ = k_ref.shape
    for kv in range(N_KV_HEADS):
        sl = slice(kv * HEAD_DIM, (kv + 1) * HEAD_DIM)
        k_ref[:, 0, kv] = qk_norm(k[:, sl], gk_ref).reshape(tb, ts, HEAD_DIM)
        v_ref[:, 0, kv] = v[:, sl].reshape(tb, ts, HEAD_DIM)

    dk = (d // 2) // GLA_HEADS
    qg_ref[...] = (proj("q_g") * (dk ** -0.5)).astype(BF16)
    kg_ref[...] = proj("k_g").astype(BF16)
    vg_ref[...] = proj("v_g").astype(BF16)
    rg_ref[...] = _silu(proj("r_g")).astype(BF16)

    gk = proj("gk").astype(BF16)
    lgf_ref[...] = _log_sigmoid(_dot(gk, wgf_ref[...]) + bgf_ref[...]) * (1.0 / GLA_GATE_NORM)
    lgb_ref[...] = _log_sigmoid(_dot(gk, wgb_ref[...]) + bgb_ref[...]) * (1.0 / GLA_GATE_NORM)

    gate_ref[...] = _sigmoid(proj("gates")).astype(BF16)


def _inproj_call(x2, mod3, mod_row_of_tile, g_pre, w_in_p, g_q, g_k, wgf, bgf, wgb, bgb, rope_tabs,
                 batch, seq):
    n, d = x2.shape
    tm = TOKEN_TILE
    _, dinp = _inproj_layout(d)
    rope = rope_tabs is not None
    gk_w = d // 2
    if seq >= tm:
        tb, ts, per = 1, tm, seq // tm
        kv_map = lambda i: (i // per, 0, 0, i % per, 0)
    else:
        tb, ts, per = tm // seq, seq, 1
        kv_map = lambda i: (i, 0, 0, 0, 0)
    row = lambda i: (i, 0)
    const = lambda i: (0, 0)
    in_specs = [
        pl.BlockSpec((tm, d), row),
        pl.BlockSpec((1, 1, 6 * d), lambda i: (mod_row_of_tile(i, tm), 0, 0)),
        pl.BlockSpec((1, d), const),
        pl.BlockSpec((dinp, d), const, pipeline_mode=pl.Buffered(1)),
        pl.BlockSpec((1, HEAD_DIM), const),
        pl.BlockSpec((1, HEAD_DIM), const),
        pl.BlockSpec((V7X_LANES, gk_w), const),
        pl.BlockSpec((1, gk_w), const),
        pl.BlockSpec((V7X_LANES, gk_w), const),
        pl.BlockSpec((1, gk_w), const),
    ]
    args = [x2, mod3, g_pre, w_in_p, g_q, g_k, wgf, bgf, wgb, bgb]
    if rope:
        tab = pl.BlockSpec((tm, HEAD_DIM), lambda i: (i % per, 0))
        in_specs += [tab, tab, tab]
        args += list(rope_tabs)
    kv_shape = jax.ShapeDtypeStruct((batch, 1, N_KV_HEADS, seq, HEAD_DIM), F32)
    kv_spec = pl.BlockSpec((tb, 1, N_KV_HEADS, ts, HEAD_DIM), kv_map)
    out_shape = (
        jax.ShapeDtypeStruct((n, N_HEADS * HEAD_DIM), BF16), kv_shape, kv_shape,
        jax.ShapeDtypeStruct((n, gk_w), BF16), jax.ShapeDtypeStruct((n, gk_w), BF16),
        jax.ShapeDtypeStruct((n, d), BF16), jax.ShapeDtypeStruct((n, d), BF16),
        jax.ShapeDtypeStruct((n, gk_w), F32), jax.ShapeDtypeStruct((n, gk_w), F32),
        jax.ShapeDtypeStruct((n, 2 * d), BF16),
    )
    out_specs = (
        pl.BlockSpec((tm, N_HEADS * HEAD_DIM), row), kv_spec, kv_spec,
        pl.BlockSpec((tm, gk_w), row), pl.BlockSpec((tm, gk_w), row),
        pl.BlockSpec((tm, d), row), pl.BlockSpec((tm, d), row),
        pl.BlockSpec((tm, gk_w), row), pl.BlockSpec((tm, gk_w), row),
        pl.BlockSpec((tm, 2 * d), row),
    )
    out_row_bytes = 2 * (N_HEADS * HEAD_DIM + 2 * gk_w + 2 * d + 2 * d) + 4 * (4 * HEAD_DIM + 2 * gk_w)
    vmem = d * dinp * 2 + 2 * tm * (d * 4 + out_row_bytes) + 6 * tm * 2 * d * 4
    return pl.pallas_call(
        functools.partial(_inproj_kernel, rope=rope, d=d),
        out_shape=out_shape,
        grid=(n // tm,),
        in_specs=in_specs,
        out_specs=out_specs,
        compiler_params=_cparams(("parallel",), vmem),
        name="inproj_lat" if rope else "inproj_ctx",
    )(*args)


def _attn_kernel(*refs, cached):
    if cached:
        q_ref, k_ref, v_ref, ck_ref, cv_ref, o_ref = refs
    else:
        q_ref, k_ref, v_ref, o_ref = refs
    tq = q_ref.shape[0]
    grp = N_HEADS // N_KV_HEADS
    for kv in range(N_KV_HEADS):
        kk = k_ref[0, 0, kv].astype(BF16)
        vv = v_ref[0, 0, kv].astype(BF16)
        if cached:
            kk = jnp.concatenate([ck_ref[0, 0, kv].astype(BF16), kk], axis=0)
            vv = jnp.concatenate([cv_ref[0, 0, kv].astype(BF16), vv], axis=0)
        heads = [q_ref[:, (kv * grp + g) * HEAD_DIM:(kv * grp + g + 1) * HEAD_DIM] for g in range(grp)]
        q4 = jnp.concatenate(heads, axis=0)
        s = _dot_nt(q4, kk)
        p = jnp.exp(s - jnp.max(s, axis=-1, keepdims=True))
        l = jnp.sum(p, axis=-1, keepdims=True)
        o = _dot(p.astype(BF16), vv) / l
        for g in range(grp):
            hd = kv * grp + g
            o_ref[:, hd * HEAD_DIM:(hd + 1) * HEAD_DIM] = o[g * tq:(g + 1) * tq].astype(BF16)


def _attn_call(q_a, k_a, v_a, cache_k, cache_v):
    batch, _, _, seq, _ = k_a.shape
    n, aq = q_a.shape
    tq = ATTN_Q_TILE
    per = seq // tq
    cached = cache_k is not None
    own = pl.BlockSpec((1, 1, N_KV_HEADS, seq, HEAD_DIM), lambda b, j: (b, 0, 0, 0, 0))
    in_specs = [pl.BlockSpec((tq, aq), lambda b, j: (b * per + j, 0)), own, own]
    args = [q_a, k_a, v_a]
    klen = seq
    if cached:
        past = cache_k.shape[3]
        cspec = pl.BlockSpec((1, 1, N_KV_HEADS, past, HEAD_DIM), lambda b, j: (b, 0, 0, 0, 0))
        in_specs += [cspec, cspec]
        args += [cache_k, cache_v]
        klen += past
    grp = N_HEADS // N_KV_HEADS
    vmem = 4 * grp * tq * klen * 4 + 8 * N_KV_HEADS * klen * HEAD_DIM * 4 + 8 * tq * aq * 2
    return pl.pallas_call(
        functools.partial(_attn_kernel, cached=cached),
        out_shape=jax.ShapeDtypeStruct((n, aq), BF16),
        grid=(batch, per),
        in_specs=in_specs,
        out_specs=pl.BlockSpec((tq, aq), lambda b, j: (b * per + j, 0)),
        compiler_params=_cparams(("parallel", "parallel"), vmem),
        name="attn_lat" if cached else "attn_ctx",
    )(*args)


def _gla_kernel(*refs, nblk, heads, has_state, emit_state):
    refs = list(refs)
    q_ref, k_ref, v_ref, lgf_ref, lgb_ref, rg_ref, gg_ref = refs[:7]
    pos = 7
    if has_state:
        s0f_ref, s0b_ref = refs[pos:pos + 2]
        pos += 2
    og_ref = refs[pos]
    pos += 1
    if emit_state:
        sf_ref, sb_ref = refs[pos:pos + 2]
        pos += 2
    of_scr, ob_scr = refs[pos:pos + 2]

    blk = GLA_BLOCK
    ch = GLA_CHUNK
    nch = blk // ch
    dk = q_ref.shape[1] // heads
    dv = v_ref.shape[1] // heads
    shift = ch.bit_length() - 1
    row_in_chunk = lax.broadcasted_iota(I32, (blk, dk), 0) & (ch - 1)
    ri = lax.broadcasted_iota(I32, (blk, blk), 0)
    ci = lax.broadcasted_iota(I32, (blk, blk), 1)
    same = (ri >> shift) == (ci >> shift)
    mask_f = same & (ci <= ri)
    mask_b = same & (ci >= ri)

    def one_block(b0, hd, state, reverse):
        rows = pl.ds(b0, blk)
        kcols = slice(hd * dk, (hd + 1) * dk)
        q = q_ref[rows, kcols].astype(F32)
        k = k_ref[rows, kcols].astype(F32)
        v = v_ref[rows, hd * dv:(hd + 1) * dv]
        b = (lgb_ref if reverse else lgf_ref)[rows, kcols]
        mask = mask_b if reverse else mask_f
        s = 1
        while s < ch:
            if reverse:
                sh = pltpu.roll(b, blk - s, axis=0)
                b = b + jnp.where(row_in_chunk < ch - s, sh, 0.0)
            else:
                sh = pltpu.roll(b, s, axis=0)
                b = b + jnp.where(row_in_chunk >= s, sh, 0.0)
            s *= 2
        qe = (q * jnp.exp(b)).astype(BF16)
        ke = (k * jnp.exp(-b)).astype(BF16)
        a = jnp.where(mask, _dot_nt(qe, ke), 0.0).astype(BF16)

        order = range(nch - 1, -1, -1) if reverse else range(nch)
        end_row = [c * ch + (0 if reverse else ch - 1) for c in range(nch)]
        ends = [b[r:r + 1, :] for r in end_row]
        b_end = jnp.concatenate([jnp.broadcast_to(e, (ch, dk)) for e in ends], axis=0)
        kd = (k * jnp.exp(b_end - b)).astype(BF16)
        intra = _dot(a, v)
        decay = jnp.exp(jnp.concatenate(ends + [jnp.zeros((dk - nch, dk), F32)], axis=0)).T
        inter = [None] * nch
        for c in order:
            crow = slice(c * ch, (c + 1) * ch)
            kv_c = lax.dot_general(kd[crow], v[crow], (((0,), (0,)), ((), ())), preferred_element_type=F32)
            if state is None:
                inter[c] = jnp.zeros((ch, dv), F32)
                state = kv_c
            else:
                inter[c] = _dot(qe[c * ch:(c + 1) * ch], state.astype(BF16))
                state = decay[:, c:c + 1] * state + kv_c
        return intra + jnp.concatenate(inter, axis=0), state

    for hd in range(heads):
        vcols = slice(hd * dv, (hd + 1) * dv)
        sf = s0f_ref[0, 0, hd] if has_state else None
        sb = s0b_ref[0, 0, hd] if has_state else None
        for i in range(nblk):
            j = nblk - 1 - i
            o_f, sf = one_block(i * blk, hd, sf, False)
            o_b, sb = one_block(j * blk, hd, sb, True)
            of_scr[pl.ds(i * blk, blk), vcols] = o_f
            ob_scr[pl.ds(j * blk, blk), vcols] = o_b
        if emit_state:
            sf_ref[0, 0, hd] = sf
            sb_ref[0, 0, hd] = sb
        o = of_scr[:, vcols] + ob_scr[:, vcols]
        og_ref[:, vcols] = (_rms(o, gg_ref[...]) * rg_ref[:, vcols].astype(F32)).astype(BF16)


def _gla_call(q_g, k_g, v_g, lg_f, lg_b, r_g, g_gla, state_f, state_b, batch, seq, emit_state, heads):
    n, gkw = q_g.shape
    d = v_g.shape[1]
    dk, dv = gkw // GLA_HEADS, d // GLA_HEADS
    has_state = state_f is not None
    kspec = pl.BlockSpec((seq, heads * dk), lambda b, h: (b, h))
    vspec = pl.BlockSpec((seq, heads * dv), lambda b, h: (b, h))
    sspec = pl.BlockSpec((1, 1, heads, dk, dv), lambda b, h: (b, 0, h, 0, 0))
    in_specs = [kspec, kspec, vspec, kspec, kspec, vspec, pl.BlockSpec((1, dv), lambda b, h: (0, 0))]
    args = [q_g, k_g, v_g, lg_f, lg_b, r_g, g_gla]
    if has_state:
        in_specs += [sspec, sspec]
        args += [state_f, state_b]
    out_shape = [jax.ShapeDtypeStruct((n, d), BF16)]
    out_specs = [vspec]
    if emit_state:
        st = jax.ShapeDtypeStruct((batch, 1, GLA_HEADS, dk, dv), F32)
        out_shape += [st, st]
        out_specs += [sspec, sspec]
    vmem = 12 * seq * heads * dv * 4 + 40 * GLA_BLOCK * GLA_BLOCK * 4
    return pl.pallas_call(
        functools.partial(_gla_kernel, nblk=seq // GLA_BLOCK, heads=heads, has_state=has_state,
                          emit_state=emit_state),
        out_shape=tuple(out_shape),
        grid=(batch, GLA_HEADS // heads),
        in_specs=in_specs,
        out_specs=tuple(out_specs),
        scratch_shapes=[pltpu.VMEM((seq, heads * dv), F32), pltpu.VMEM((seq, heads * dv), F32)],
        compiler_params=_cparams(("parallel", "parallel"), vmem),
        name="gla_ctx" if emit_state else "gla_lat",
    )(*args)


def _outproj_kernel(oa_ref, og_ref, gate_ref, x_ref, mod_ref, wpa_ref, wpg_ref, wout_ref, gpm_ref,
                    gpf_ref, wr_ref, x1_ref, h_ref, aff_ref):
    d = x_ref.shape[1]
    m = mod_ref[0]
    oa = _dot(oa_ref[...], wpa_ref[...])
    og = _dot(og_ref[...], wpg_ref[...])
    mix = gate_ref[:, 0:d].astype(F32) * oa + gate_ref[:, d:2 * d].astype(F32) * og
    mo = _dot(mix.astype(BF16), wout_ref[...])
    x1 = x_ref[...] + m[:, 2 * d:3 * d] * _rms(mo, gpm_ref[...])
    x1_ref[...] = x1
    hb = (_rms(x1, gpf_ref[...]) * (1.0 + m[:, 4 * d:5 * d]) + m[:, 3 * d:4 * d]).astype(BF16)
    bits = lax.bitcast_convert_type(hb.astype(F32), jnp.uint32)
    packed = (bits[:, 0:d // 2] >> 16) | (bits[:, d // 2:d] & jnp.uint32(0xFFFF0000))
    h_ref[...] = lax.bitcast_convert_type(packed, I32)
    logits = _dot(hb, wr_ref[...])
    valid = lax.broadcasted_iota(I32, logits.shape, 1) < N_EXPERTS
    mx = jnp.max(jnp.where(valid, logits, -jnp.inf), axis=-1, keepdims=True)
    ex = jnp.where(valid, jnp.exp(logits - mx), 0.0)
    aff_ref[...] = ex / jnp.sum(ex, axis=-1, keepdims=True)


def _outproj_call(o_a, o_g, gates, x2, mod3, mod_row_of_tile, w_pa, w_pg, w_out, g_pm, g_pf, w_r, tag):
    n, d = x2.shape
    tm = OUTPROJ_TILE
    row = lambda i: (i, 0)
    const = lambda i: (0, 0)
    wspec = pl.BlockSpec((d, d), const, pipeline_mode=pl.Buffered(1))
    vmem = 3 * d * d * 2 + 2 * tm * d * (2 + 2 + 4 + 4 + 4 + 2) + 8 * tm * d * 4
    return pl.pallas_call(
        _outproj_kernel,
        out_shape=(jax.ShapeDtypeStruct((n, d), F32), jax.ShapeDtypeStruct((n, d // 2), I32),
                   jax.ShapeDtypeStruct((n, V7X_LANES), F32)),
        grid=(n // tm,),
        in_specs=[pl.BlockSpec((tm, d), row), pl.BlockSpec((tm, d), row), pl.BlockSpec((tm, 2 * d), row),
                  pl.BlockSpec((tm, d), row),
                  pl.BlockSpec((1, 1, 6 * d), lambda i: (mod_row_of_tile(i, tm), 0, 0)),
                  wspec, wspec, wspec, pl.BlockSpec((1, d), const), pl.BlockSpec((1, d), const),
                  pl.BlockSpec((d, V7X_LANES), const)],
        out_specs=(pl.BlockSpec((tm, d), row), pl.BlockSpec((tm, d // 2), row),
                   pl.BlockSpec((tm, V7X_LANES), row)),
        compiler_params=_cparams(("parallel",), vmem),
        name="outproj_" + tag,
    )(o_a, o_g, gates, x2, mod3, w_pa, w_pg, w_out, g_pm, g_pf, w_r)


def _route_kernel(aff_ref, pos_ref, post_ref, tbl_ref, afft_scr, *, cap):
    n = aff_ref.shape[0]
    rb = ROUTE_BLOCK
    nb = n // rb
    lanes = aff_ref.shape[1]
    lane = lax.broadcasted_iota(I32, (1, lanes), 1)
    expert_lane = lane < N_EXPERTS
    tbl_ref[...] = jnp.zeros(tbl_ref.shape, I32)

    def to_token_lanes(c, carry):
        start = pl.multiple_of(c * rb, rb)
        afft_scr[c] = aff_ref[pl.ds(start, rb), :].T[0:N_EXPERTS, :]
        return carry

    lax.fori_loop(0, nb, to_token_lanes, 0)
    aff_t = afft_scr[...]

    def count(hit):
        return jnp.sum(jnp.sum(hit.astype(I32), axis=0), axis=1, keepdims=True)

    def bit_step(i, lo):
        t = lo | jnp.left_shift(jnp.int32(1), 30 - i)
        ge = aff_t >= lax.bitcast_convert_type(t, F32)[None]
        return jnp.where(count(ge) >= cap, t, lo)

    thr_bits = lax.fori_loop(0, 31, bit_step, jnp.zeros((N_EXPERTS, 1), I32))
    need_t = cap - count(aff_t > lax.bitcast_convert_type(thr_bits, F32)[None])

    def to_expert_lanes(col):
        full = jnp.concatenate([jnp.broadcast_to(col, (N_EXPERTS, lanes)),
                                jnp.zeros((lanes - N_EXPERTS, lanes), I32)], axis=0)
        return full.T[0:1, :]

    thr = lax.bitcast_convert_type(to_expert_lanes(thr_bits), F32)
    need = to_expert_lanes(need_t).astype(F32)
    capf = float(cap)

    r = lax.broadcasted_iota(I32, (rb, rb), 0)
    c_ = lax.broadcasted_iota(I32, (rb, rb), 1)
    tril = jnp.where(c_ <= r, 1.0, 0.0).astype(BF16)

    def blk_step(c, carry):
        eq_before, raw_before = carry
        start = pl.multiple_of(c * rb, rb)
        a = aff_ref[pl.ds(start, rb), :]
        gt = a > thr
        eq = a == thr
        eq_incl = _dot(tril, jnp.where(eq, 1.0, 0.0).astype(BF16)) + eq_before
        raw = (gt | (eq & (eq_incl <= need))) & expert_lane
        raw_incl = _dot(tril, jnp.where(raw, 1.0, 0.0).astype(BF16)) + raw_before
        sel = raw & (raw_incl <= capf)
        self_ = jnp.where(sel, 1.0, 0.0)
        incl = jnp.minimum(raw_incl, capf)
        sel_before = jnp.minimum(raw_before, capf)
        excl = incl - self_
        posb = jnp.where(sel, excl, -1.0).astype(I32)
        pos_ref[pl.ds(start, rb), :] = posb
        post_ref[c] = posb.T[0:N_EXPERTS, :]
        tbl_ref[pl.ds(c, 1), :] = sel_before.astype(I32)
        return (eq_incl[rb - 1:rb, :], raw_incl[rb - 1:rb, :])

    zero = jnp.zeros((1, lanes), F32)
    _, total = lax.fori_loop(0, nb, blk_step, (zero, zero))
    tbl_ref[pl.ds(nb, 1), :] = jnp.minimum(total, capf).astype(I32)


def _route_call(aff, cap, tag):
    n, lanes = aff.shape
    nb = n // ROUTE_BLOCK
    tbl_rows = -(-(nb + 1) // 8) * 8
    full = lambda *shape: pl.BlockSpec(shape, lambda: tuple(0 for _ in shape))
    return pl.pallas_call(
        functools.partial(_route_kernel, cap=cap),
        out_shape=(jax.ShapeDtypeStruct((n, lanes), I32),
                   jax.ShapeDtypeStruct((nb, N_EXPERTS, ROUTE_BLOCK), I32),
                   jax.ShapeDtypeStruct((tbl_rows, lanes), I32)),
        in_specs=[full(n, lanes)],
        out_specs=(full(n, lanes), full(nb, N_EXPERTS, ROUTE_BLOCK), full(tbl_rows, lanes)),
        scratch_shapes=[pltpu.VMEM((nb, N_EXPERTS, ROUTE_BLOCK), F32)],
        compiler_params=_cparams((), 8 * n * lanes * 4),
        name="route_" + tag,
    )(aff)


def _sc_gather_call(table, post, cap):
    n_exp, n = post.shape
    words = table.shape[1]
    workers = V7X_SC_CORES * V7X_SC_SUBCORES
    parts = workers // n_exp
    chunk = SC_GATHER_CHUNK
    lanes = V7X_SC_LANES
    per_part = cap // parts
    assert parts * n_exp == workers and per_part % chunk == 0 and n % lanes == 0
    mesh = plsc.VectorSubcoreMesh(core_axis_name="c", subcore_axis_name="s",
                                  num_cores=V7X_SC_CORES, num_subcores=V7X_SC_SUBCORES)

    def body(table_hbm, post_hbm, out_hbm, pos_v, idx_v, rows_v, sem):
        wid = lax.axis_index("s") * V7X_SC_CORES + lax.axis_index("c")
        e = wid // parts
        part = wid % parts
        pltpu.sync_copy(post_hbm.at[e], pos_v)
        lane = lax.iota(I32, lanes)

        @pl.loop(0, n, step=lanes)
        def _(t0):
            p = pos_v[pl.ds(t0, lanes)]
            plsc.store_scatter(idx_v, [p], lane + t0, mask=p >= 0)

        @pl.loop(0, per_part // chunk)
        def _(j):
            off = pl.multiple_of(part * per_part + j * chunk, chunk)
            pltpu.async_copy(table_hbm.at[idx_v.at[pl.ds(off, chunk)]], rows_v, sem).wait()
            pltpu.sync_copy(rows_v, out_hbm.at[pl.ds(e * cap + off, chunk)])

    return pl.kernel(
        body,
        out_type=jax.ShapeDtypeStruct((n_exp * cap, words), table.dtype),
        mesh=mesh,
        scratch_types=[pltpu.VMEM((n,), I32), pltpu.VMEM((cap,), I32),
                       pltpu.VMEM((chunk, words), table.dtype), pltpu.SemaphoreType.DMA],
        compiler_params=pltpu.CompilerParams(needs_layout_passes=False),
        name="sc_gather",
    )(table, post)


def _ffn_kernel(*refs, caps):
    ng = len(caps)
    xs_refs = refs[:ng]
    w1_ref, w3_ref, w2_ref = refs[ng:ng + 3]
    ye_refs = refs[ng + 3:2 * ng + 3]
    row_off = [sum(caps[:g]) for g in range(ng)]

    def unpack(words):
        w = lax.bitcast_convert_type(words, jnp.uint32)
        lo = lax.bitcast_convert_type(w << 16, F32).astype(BF16)
        hi = lax.bitcast_convert_type(w & jnp.uint32(0xFFFF0000), F32).astype(BF16)
        return jnp.concatenate([lo, hi], axis=1)

    xs = jnp.concatenate([unpack(r[...]) for r in xs_refs], axis=0)
    hid = _silu(_dot(xs, w1_ref[...].astype(BF16))) * _dot(xs, w3_ref[...].astype(BF16))
    ye = _dot(hid.astype(BF16), w2_ref[...].astype(BF16)).astype(BF16)
    for g in range(ng):
        ye_refs[g][...] = ye[row_off[g]:row_off[g] + caps[g]]


def _ffn_call(groups, w1, w3, w2):
    caps = tuple(g[1] for g in groups)
    n_exp, d, dff = w1.shape
    rows = sum(caps)
    in_specs = [pl.BlockSpec((cap, d // 2), lambda e: (e, 0)) for cap in caps]
    in_specs += [pl.BlockSpec((None, d, dff), lambda e: (e, 0, 0)),
                 pl.BlockSpec((None, d, dff), lambda e: (e, 0, 0)),
                 pl.BlockSpec((None, dff, d), lambda e: (e, 0, 0))]
    vmem = 3 * d * dff * (2 * 4 + 2) + rows * d * (2 * 2 + 2 + 2 * 2) + 5 * rows * dff * 4
    return pl.pallas_call(
        functools.partial(_ffn_kernel, caps=caps),
        out_shape=tuple(jax.ShapeDtypeStruct((n_exp * cap, d), BF16) for cap in caps),
        grid=(n_exp,),
        in_specs=in_specs,
        out_specs=tuple(pl.BlockSpec((cap, d), lambda e: (e, 0)) for cap in caps),
        compiler_params=_cparams(("arbitrary",), vmem),
        name="ffn",
    )(*[g[0] for g in groups], w1, w3, w2)


def _combine_kernel(tbl_ref, ye_hbm, pos_ref, aff_ref, x1_ref, mod_ref, gpo_ref, y_ref, buf, sem, xbuf, xsem,
                    acc_scr, *, cap, blocks_per_tile):
    i = pl.program_id(0)
    nsteps = pl.num_programs(0)
    d = x1_ref.shape[1]
    lanes = pos_ref.shape[1]
    win = COMBINE_WINDOW
    last_start = ye_hbm.shape[0] - win
    slot = i % 2

    def first_row(step, e):
        return tbl_ref[step * blocks_per_tile, e] + e * cap

    def window_start(first, k):
        unclamped = (first // BF16_ROWS_PER_TILE) * BF16_ROWS_PER_TILE + k * win
        return unclamped, jnp.minimum(unclamped, last_start)

    def fetch(step, to_slot, e):
        start = window_start(first_row(step, e), 0)[1]
        return pltpu.make_async_copy(ye_hbm.at[pl.ds(pl.multiple_of(start, BF16_ROWS_PER_TILE), win), :],
                                     buf.at[to_slot, pl.ds(e * win, win), :], sem.at[to_slot, e])

    @pl.when(i == 0)
    def _prime():
        for e in range(N_EXPERTS):
            fetch(0, 0, e).start()

    @pl.when(i + 1 < nsteps)
    def _ahead():
        for e in range(N_EXPERTS):
            fetch(i + 1, 1 - slot, e).start()

    lane_row = lax.broadcasted_iota(I32, (1, win), 1)
    pieces = []
    for e in range(N_EXPERTS):
        pcol = pos_ref[:, e:e + 1]
        grow = jnp.where(pcol >= 0, pcol + e * cap, -1)
        start = window_start(first_row(i, e), 0)[1]
        pieces.append(jnp.where(grow == start + lane_row, aff_ref[:, e:e + 1], 0.0).astype(BF16))
    onehot = jnp.concatenate(pieces, axis=1)
    for e in range(N_EXPERTS):
        fetch(i, slot, e).wait()
    acc_scr[...] = _dot(onehot, buf[slot])

    def extra_windows(e):
        covered = window_start(first_row(i, e), 1)[0]
        return jnp.maximum(first_row(i + 1, e) - covered + win - 1, 0) // win

    def expert_extra(e, carry):
        first = first_row(i, e)
        extra = extra_windows(e)

        def more(k, c):
            unclamped, start = window_start(first, k)
            cp = pltpu.make_async_copy(ye_hbm.at[pl.ds(pl.multiple_of(start, BF16_ROWS_PER_TILE), win), :],
                                       xbuf, xsem)
            cp.start()
            cp.wait()
            at_e = lax.broadcasted_iota(I32, (1, lanes), 1) == e
            pcol = jnp.sum(jnp.where(at_e, pos_ref[...].astype(F32), 0.0), axis=1, keepdims=True).astype(I32)
            wcol = jnp.sum(jnp.where(at_e, aff_ref[...], 0.0), axis=1, keepdims=True)
            grow = jnp.where(pcol >= 0, pcol + e * cap, -1)
            hit = (grow == start + lane_row) & (grow >= unclamped)
            acc_scr[...] += _dot(jnp.where(hit, wcol, 0.0).astype(BF16), xbuf[...])
            return c

        lax.fori_loop(1, 1 + extra, more, 0)
        return carry

    any_extra = extra_windows(0)
    for e in range(1, N_EXPERTS):
        any_extra = any_extra + extra_windows(e)

    @pl.when(any_extra > 0)
    def _overflow():
        lax.fori_loop(0, N_EXPERTS, expert_extra, 0)

    m = mod_ref[0]
    y_ref[...] = x1_ref[...] + m[:, 5 * d:6 * d] * _rms(acc_scr[...], gpo_ref[...])


def _combine_call(tbl, ye, pos, aff, x1, mod3, mod_row_of_tile, g_po, cap, tag):
    n, d = x1.shape
    tm = COMBINE_TILE
    lanes = pos.shape[1]
    grid_spec = pltpu.PrefetchScalarGridSpec(
        num_scalar_prefetch=1,
        grid=(n // tm,),
        in_specs=[pl.BlockSpec(memory_space=pl.ANY),
                  pl.BlockSpec((tm, lanes), lambda i, t: (i, 0)),
                  pl.BlockSpec((tm, lanes), lambda i, t: (i, 0)),
                  pl.BlockSpec((tm, d), lambda i, t: (i, 0)),
                  pl.BlockSpec((1, 1, 6 * d), lambda i, t: (mod_row_of_tile(i, tm), 0, 0)),
                  pl.BlockSpec((1, d), lambda i, t: (0, 0))],
        out_specs=pl.BlockSpec((tm, d), lambda i, t: (i, 0)),
        scratch_shapes=[pltpu.VMEM((2, N_EXPERTS * COMBINE_WINDOW, d), BF16),
                        pltpu.SemaphoreType.DMA((2, N_EXPERTS)),
                        pltpu.VMEM((COMBINE_WINDOW, d), BF16),
                        pltpu.SemaphoreType.DMA(()),
                        pltpu.VMEM((tm, d), F32)],
    )
    vmem = 2 * N_EXPERTS * COMBINE_WINDOW * d * 2 + 10 * tm * d * 4 + 2 * tm * N_EXPERTS * COMBINE_WINDOW * 4
    return pl.pallas_call(
        functools.partial(_combine_kernel, cap=cap, blocks_per_tile=tm // ROUTE_BLOCK),
        out_shape=jax.ShapeDtypeStruct((n, d), F32),
        grid_spec=grid_spec,
        compiler_params=_cparams(("arbitrary",), vmem),
        name="combine_" + tag,
    )(tbl, ye, pos, aff, x1, mod3, g_po)


def _rope_tables(seq):
    rows = seq // GRID_W
    r = jnp.repeat(jnp.arange(rows), GRID_W).astype(F32)
    col = jnp.tile(jnp.arange(GRID_W), rows).astype(F32)
    pairs = HEAD_DIM // 4
    freqs = ROPE_THETA ** (-jnp.arange(pairs, dtype=F32) / pairs)
    ang = jnp.concatenate([r[:, None] * freqs, col[:, None] * freqs], axis=-1)
    cos = jnp.repeat(jnp.cos(ang), 2, axis=-1)
    sin = jnp.repeat(jnp.sin(ang), 2, axis=-1)
    even = (jnp.arange(HEAD_DIM) % 2) == 0
    return cos, jnp.where(even, -sin, 0.0), jnp.where(even, 0.0, sin)


def _trunk_to_routing(x, mod3, mod_row_of_tile, rope_tabs, ctx, lw, tag):
    (g_pre_mix, g_post_mix, g_pre_ffn, g_post_ffn, w_in_p, g_q, g_k, wgf, bgf, wgb, bgb, g_gla,
     w_pa, w_pg, w_out, w_r, w1, w3, w2) = lw
    batch, seq, d = x.shape
    n = batch * seq
    x2 = x.reshape(n, d)
    (q_a, k_a, v_a, q_g, k_g, v_g, r_g, lg_f, lg_b, gates) = _inproj_call(
        x2, mod3, mod_row_of_tile, g_pre_mix, w_in_p, g_q, g_k, wgf, bgf, wgb, bgb, rope_tabs, batch, seq)
    if ctx is None:
        o_a = _attn_call(q_a, k_a, v_a, None, None)
        o_g, s_f, s_b = _gla_call(q_g, k_g, v_g, lg_f, lg_b, r_g, g_gla, None, None, batch, seq, True,
                                  GLA_CTX_HEADS_PER_STEP)
    else:
        ck, cv, s_f0, s_b0 = ctx
        o_a = _attn_call(q_a, k_a, v_a, ck, cv)
        (o_g,) = _gla_call(q_g, k_g, v_g, lg_f, lg_b, r_g, g_gla, s_f0, s_b0, batch, seq, False,
                           GLA_LAT_HEADS_PER_STEP)
        s_f = s_b = None
    x1, h, aff = _outproj_call(o_a, o_g, gates, x2, mod3, mod_row_of_tile, w_pa, w_pg, w_out,
                               g_post_mix, g_pre_ffn, w_r, tag)
    cap = (EC_CAPACITY_FACTOR * n) // N_EXPERTS
    pos, post, tbl = _route_call(aff, cap, tag)
    xs = _sc_gather_call(h, post.transpose(1, 0, 2).reshape(N_EXPERTS, n), cap)
    return dict(x1=x1, xs=xs, aff=aff, pos=pos, tbl=tbl, cap=cap,
                mod_row_of_tile=mod_row_of_tile, tag=tag, shape=(batch, seq, d)), (k_a, v_a, s_f, s_b)


def _expert_ffn(groups, mod3, g_post_ffn, w1, w3, w2):
    yes = _ffn_call([(g["xs"], g["cap"]) for g in groups], w1, w3, w2)
    outs = []
    for g, ye in zip(groups, yes):
        y = _combine_call(g["tbl"], ye, g["pos"], g["aff"], g["x1"], mod3, g["mod_row_of_tile"], g_post_ffn,
                          g["cap"], g["tag"])
        outs.append(y.reshape(g["shape"]))
    return outs


def kernel(x_prompt, x_sample, cache_k, cache_v, state_gla_fwd, state_gla_bwd, c, c_ctx, g_pre_mix, g_post_mix, g_pre_ffn, g_post_ffn, w_mod, b_mod, w_in, g_q, g_k, w_gk2_f, b_gk_f, w_gk2_b, b_gk_b, g_gla, w_pa, w_pg, w_out, w_router, w1, w3, w2):
    depth = w_in.shape[0]
    assert depth == 1, "single trunk layer"
    d = x_prompt.shape[-1]
    dec_batch, dec_seq, _ = x_sample.shape
    assert dec_batch + 1 <= MOD_ROWS
    l = 0
    rank = GLA_GATE_RANK
    w_in_p = jnp.swapaxes(w_in[l], 0, 1).astype(BF16)
    gkw = w_gk2_f.shape[-1]
    wgf = jnp.zeros((V7X_LANES, gkw), F32).at[0:rank].set(w_gk2_f[l]).astype(BF16)
    wgb = jnp.zeros((V7X_LANES, gkw), F32).at[rank:2 * rank].set(w_gk2_b[l]).astype(BF16)
    w_r = jnp.zeros((d, V7X_LANES), F32).at[:, :N_EXPERTS].set(w_router[l]).astype(BF16)
    row = lambda a: a[l].reshape(1, -1)
    lw = (row(g_pre_mix), row(g_post_mix), row(g_pre_ffn), row(g_post_ffn), w_in_p, row(g_q), row(g_k),
          wgf, row(b_gk_f), wgb, row(b_gk_b), row(g_gla),
          w_pa[l].astype(BF16), w_pg[l].astype(BF16), w_out[l].astype(BF16), w_r, w1[l], w3[l], w2[l])

    cc = jnp.concatenate([c_ctx[None, :], c, jnp.zeros((MOD_ROWS - 1 - dec_batch, d), F32)], axis=0)
    mod = _mod_call(cc, w_mod[l], b_mod[l].reshape(1, -1))
    mod3 = mod.reshape(MOD_ROWS, 1, 6 * d)

    gp, (nk, nv, nsf, nsb) = _trunk_to_routing(x_prompt, mod3, lambda i, tm: 0, None, None, lw, "ctx")
    ctx = (cache_k, cache_v, state_gla_fwd, state_gla_bwd)
    gs, _ = _trunk_to_routing(x_sample, mod3, lambda i, tm: 1 + (i * tm) // dec_seq, _rope_tables(dec_seq),
                              ctx, lw, "lat")
    yp, ys = _expert_ffn([gp, gs], mod3, lw[3], lw[16], lw[17], lw[18])
    return (yp, ys, nk, nv, nsf, nsb)
```

```python
import functools

import jax
import jax.numpy as jnp
from jax import lax
from jax.experimental import pallas as pl
from jax.experimental.pallas import tpu as pltpu
from jax.experimental.pallas import tpu_sc as plsc

F32 = jnp.float32
BF16 = jnp.bfloat16
I32 = jnp.int32

N_HEADS = 8
N_KV_HEADS = 2
HEAD_DIM = 128
GRID_W = 64
ROPE_THETA = 10000.0
GLA_HEADS = 4
GLA_GATE_RANK = 16
GLA_GATE_NORM = 16.0
GLA_CHUNK = 64
N_EXPERTS = 16
EC_CAPACITY_FACTOR = 2
EPS = 1e-6

V7X_LANES = 128
V7X_VMEM_BYTES = 64 * 1024 * 1024
V7X_VMEM_RESERVE_BYTES = 6 * 1024 * 1024
BF16_ROWS_PER_TILE = 16
V7X_SC_CORES = 2
V7X_SC_SUBCORES = 16
V7X_SC_LANES = 16

TOKEN_TILE = 512
OUTPROJ_TILE = 512
ATTN_Q_TILE = 256
ATTN_SEQS_PER_STEP = 4
GLA_BLOCK = 256
GLA_CTX_HEADS_PER_STEP = 4
GLA_CTX_SEQS_PER_STEP = 2
GLA_LAT_HEADS_PER_STEP = 2
ROUTE_BLOCK = 256
SC_GATHER_CHUNK = 64
COMBINE_TILE = 512
COMBINE_WINDOW = 128
COMBINE_EXPERT_GROUP = 2
MOD_ROWS = 8
MOD_N_TILE = 1536


def _cparams(semantics, vmem_bytes):
    del vmem_bytes
    return pltpu.CompilerParams(dimension_semantics=semantics,
                                vmem_limit_bytes=V7X_VMEM_BYTES - V7X_VMEM_RESERVE_BYTES)


def _sigmoid(x):
    return 1.0 / (1.0 + jnp.exp(-x))


def _silu(x):
    return x * _sigmoid(x)


def _log_sigmoid(x):
    return jnp.minimum(x, 0.0) - jnp.log1p(jnp.exp(-jnp.abs(x)))


def _rms(x, g):
    ms = jnp.mean(x * x, axis=-1, keepdims=True)
    return x * lax.rsqrt(ms + EPS) * g


def _dot(a, b):
    return jnp.dot(a, b, preferred_element_type=F32)


def _dot_nt(a, b):
    return lax.dot_general(a, b, (((1,), (1,)), ((), ())), preferred_element_type=F32)


def _mod_kernel(c_ref, w_ref, b_ref, o_ref):
    s = _silu(c_ref[...]).astype(BF16)
    o_ref[...] = _dot(s, w_ref[...].astype(BF16)) + b_ref[...]


def _mod_call(cc, w_mod, b_mod):
    d, n6 = w_mod.shape
    tn = MOD_N_TILE
    return pl.pallas_call(
        _mod_kernel,
        out_shape=jax.ShapeDtypeStruct((MOD_ROWS, n6), F32),
        grid=(n6 // tn,),
        in_specs=[pl.BlockSpec((MOD_ROWS, d), lambda j: (0, 0)),
                  pl.BlockSpec((d, tn), lambda j: (0, j)),
                  pl.BlockSpec((1, tn), lambda j: (0, j))],
        out_specs=pl.BlockSpec((MOD_ROWS, tn), lambda j: (0, j)),
        compiler_params=_cparams(("arbitrary",), 3 * d * tn * 4),
        name="mod",
    )(cc, w_mod, b_mod)


def _inproj_layout(d):
    aq, akv = N_HEADS * HEAD_DIM, N_KV_HEADS * HEAD_DIM
    gk, gv = d // 2, d
    names = ("q_a", "k_a", "v_a", "q_g", "k_g", "v_g", "r_g", "gk_f", "gk_b", "gates")
    widths = (aq, akv, akv, gk, gk, gv, gv, GLA_GATE_RANK, GLA_GATE_RANK, 2 * d)
    off, o = {}, 0
    for nme, w in zip(names, widths):
        off[nme] = (o, o + w)
        o += w
    off["gk"] = (off["gk_f"][0], off["gk_f"][0] + V7X_LANES)
    return off, o


def _inproj_kernel(*refs, rope, d):
    if rope:
        (x_ref, mod_ref, gpre_ref, w_ref, gq_ref, gk_ref, wgf_ref, bgf_ref, wgb_ref, bgb_ref,
         cos_ref, se_ref, so_ref, *outs) = refs
    else:
        (x_ref, mod_ref, gpre_ref, w_ref, gq_ref, gk_ref, wgf_ref, bgf_ref, wgb_ref, bgb_ref,
         *outs) = refs
    qa_ref, k_ref, v_ref, qg_ref, kg_ref, vg_ref, rg_ref, lgf_ref, lgb_ref, gate_ref = outs
    off, _ = _inproj_layout(d)
    m = mod_ref[0]
    h = _rms(x_ref[...], gpre_ref[...]) * (1.0 + m[:, d:2 * d]) + m[:, 0:d]
    hb = h.astype(BF16)

    def proj(name):
        a, b = off[name]
        return _dot_nt(hb, w_ref[a:b, :])

    def qk_norm(y, g_ref):
        y = _rms(y, g_ref[...])
        if rope:
            nxt = pltpu.roll(y, HEAD_DIM - 1, axis=1)
            prv = pltpu.roll(y, 1, axis=1)
            y = y * cos_ref[...] + nxt * se_ref[...] + prv * so_ref[...]
        return y

    q = proj("q_a")
    scale = HEAD_DIM ** -0.5
    for hd in range(N_HEADS):
        sl = slice(hd * HEAD_DIM, (hd + 1) * HEAD_DIM)
        qa_ref[:, sl] = (qk_norm(q[:, sl], gq_ref) * scale).astype(BF16)

    k = proj("k_a")
    v = proj("v_a")
    tb, _, _, ts, _ = k_ref.shape
    for kv in range(N_KV_HEADS):
        sl = slice(kv * HEAD_DIM, (kv + 1) * HEAD_DIM)
        k_ref[:, 0, kv] = qk_norm(k[:, sl], gk_ref).reshape(tb, ts, HEAD_DIM)
        v_ref[:, 0, kv] = v[:, sl].reshape(tb, ts, HEAD_DIM)

    dk = (d // 2) // GLA_HEADS
    qg_ref[...] = (proj("q_g") * (dk ** -0.5)).astype(BF16)
    kg_ref[...] = proj("k_g").astype(BF16)
    vg_ref[...] = proj("v_g").astype(BF16)
    rg_ref[...] = _silu(proj("r_g")).astype(BF16)

    gk = proj("gk").astype(BF16)
    lgf_ref[...] = _log_sigmoid(_dot(gk, wgf_ref[...]) + bgf_ref[...]) * (1.0 / GLA_GATE_NORM)
    lgb_ref[...] = _log_sigmoid(_dot(gk, wgb_ref[...]) + bgb_ref[...]) * (1.0 / GLA_GATE_NORM)

    gate_ref[...] = _sigmoid(proj("gates")).astype(BF16)


def _inproj_call(x2, mod3, mod_row_of_tile, g_pre, w_in_p, g_q, g_k, wgf, bgf, wgb, bgb, rope_tabs,
                 batch, seq):
    n, d = x2.shape
    tm = TOKEN_TILE
    _, dinp = _inproj_layout(d)
    rope = rope_tabs is not None
    gk_w = d // 2
    if seq >= tm:
        tb, ts, per = 1, tm, seq // tm
        kv_map = lambda i: (i // per, 0, 0, i % per, 0)
    else:
        tb, ts, per = tm // seq, seq, 1
        kv_map = lambda i: (i, 0, 0, 0, 0)
    row = lambda i: (i, 0)
    const = lambda i: (0, 0)
    in_specs = [
        pl.BlockSpec((tm, d), row),
        pl.BlockSpec((1, 1, 6 * d), lambda i: (mod_row_of_tile(i, tm), 0, 0)),
        pl.BlockSpec((1, d), const),
        pl.BlockSpec((dinp, d), const, pipeline_mode=pl.Buffered(1)),
        pl.BlockSpec((1, HEAD_DIM), const),
        pl.BlockSpec((1, HEAD_DIM), const),
        pl.BlockSpec((V7X_LANES, gk_w), const),
        pl.BlockSpec((1, gk_w), const),
        pl.BlockSpec((V7X_LANES, gk_w), const),
        pl.BlockSpec((1, gk_w), const),
    ]
    args = [x2, mod3, g_pre, w_in_p, g_q, g_k, wgf, bgf, wgb, bgb]
    if rope:
        tab = pl.BlockSpec((tm, HEAD_DIM), lambda i: (i % per, 0))
        in_specs += [tab, tab, tab]
        args += list(rope_tabs)
    kv_shape = jax.ShapeDtypeStruct((batch, 1, N_KV_HEADS, seq, HEAD_DIM), F32)
    kv_spec = pl.BlockSpec((tb, 1, N_KV_HEADS, ts, HEAD_DIM), kv_map)
    out_shape = (
        jax.ShapeDtypeStruct((n, N_HEADS * HEAD_DIM), BF16), kv_shape, kv_shape,
        jax.ShapeDtypeStruct((n, gk_w), BF16), jax.ShapeDtypeStruct((n, gk_w), BF16),
        jax.ShapeDtypeStruct((n, d), BF16), jax.ShapeDtypeStruct((n, d), BF16),
        jax.ShapeDtypeStruct((n, gk_w), F32), jax.ShapeDtypeStruct((n, gk_w), F32),
        jax.ShapeDtypeStruct((n, 2 * d), BF16),
    )
    out_specs = (
        pl.BlockSpec((tm, N_HEADS * HEAD_DIM), row), kv_spec, kv_spec,
        pl.BlockSpec((tm, gk_w), row), pl.BlockSpec((tm, gk_w), row),
        pl.BlockSpec((tm, d), row), pl.BlockSpec((tm, d), row),
        pl.BlockSpec((tm, gk_w), row), pl.BlockSpec((tm, gk_w), row),
        pl.BlockSpec((tm, 2 * d), row),
    )
    out_row_bytes = 2 * (N_HEADS * HEAD_DIM + 2 * gk_w + 2 * d + 2 * d) + 4 * (4 * HEAD_DIM + 2 * gk_w)
    vmem = d * dinp * 2 + 2 * tm * (d * 4 + out_row_bytes) + 6 * tm * 2 * d * 4
    return pl.pallas_call(
        functools.partial(_inproj_kernel, rope=rope, d=d),
        out_shape=out_shape,
        grid=(n // tm,),
        in_specs=in_specs,
        out_specs=out_specs,
        compiler_params=_cparams(("parallel",), vmem),
        name="inproj_lat" if rope else "inproj_ctx",
    )(*args)


def _attn_kernel(*refs, cached):
    if cached:
        q_ref, k_ref, v_ref, ck_ref, cv_ref, o_ref = refs
    else:
        q_ref, k_ref, v_ref, o_ref = refs
    seqs = k_ref.shape[0]
    tq = q_ref.shape[0] // seqs
    grp = N_HEADS // N_KV_HEADS
    for sq in range(seqs):
        rows = pl.ds(sq * tq, tq)
        for kv in range(N_KV_HEADS):
            kk = k_ref[sq, 0, kv].astype(BF16)
            vv = v_ref[sq, 0, kv].astype(BF16)
            if cached:
                kk = jnp.concatenate([ck_ref[sq, 0, kv].astype(BF16), kk], axis=0)
                vv = jnp.concatenate([cv_ref[sq, 0, kv].astype(BF16), vv], axis=0)
            heads = [q_ref[rows, (kv * grp + g) * HEAD_DIM:(kv * grp + g + 1) * HEAD_DIM] for g in range(grp)]
            q4 = jnp.concatenate(heads, axis=0)
            s = _dot_nt(q4, kk)
            p = jnp.exp(s - jnp.max(s, axis=-1, keepdims=True))
            l = jnp.sum(p, axis=-1, keepdims=True)
            o = _dot(p.astype(BF16), vv) / l
            for g in range(grp):
                hd = kv * grp + g
                o_ref[rows, hd * HEAD_DIM:(hd + 1) * HEAD_DIM] = o[g * tq:(g + 1) * tq].astype(BF16)


def _attn_call(q_a, k_a, v_a, cache_k, cache_v):
    batch, _, _, seq, _ = k_a.shape
    n, aq = q_a.shape
    tq = ATTN_Q_TILE
    per = seq // tq
    cached = cache_k is not None
    seqs = ATTN_SEQS_PER_STEP if (per == 1 and not cached) else 1
    batch = batch // seqs
    own = pl.BlockSpec((seqs, 1, N_KV_HEADS, seq, HEAD_DIM), lambda b, j: (b, 0, 0, 0, 0))
    tq = tq * seqs
    in_specs = [pl.BlockSpec((tq, aq), lambda b, j: (b * per + j, 0)), own, own]
    args = [q_a, k_a, v_a]
    klen = seq
    if cached:
        past = cache_k.shape[3]
        cspec = pl.BlockSpec((1, 1, N_KV_HEADS, past, HEAD_DIM), lambda b, j: (b, 0, 0, 0, 0))
        in_specs += [cspec, cspec]
        args += [cache_k, cache_v]
        klen += past
    grp = N_HEADS // N_KV_HEADS
    vmem = 4 * grp * tq * klen * 4 + 8 * N_KV_HEADS * klen * HEAD_DIM * 4 + 8 * tq * aq * 2
    return pl.pallas_call(
        functools.partial(_attn_kernel, cached=cached),
        out_shape=jax.ShapeDtypeStruct((n, aq), BF16),
        grid=(batch, per),
        in_specs=in_specs,
        out_specs=pl.BlockSpec((tq, aq), lambda b, j: (b * per + j, 0)),
        compiler_params=_cparams(("parallel", "parallel"), vmem),
        name="attn_lat" if cached else "attn_ctx",
    )(*args)


def _gla_kernel(*refs, nblk, heads, seqs, has_state, emit_state):
    refs = list(refs)
    q_ref, k_ref, v_ref, lgf_ref, lgb_ref, rg_ref, gg_ref = refs[:7]
    pos = 7
    if has_state:
        s0f_ref, s0b_ref = refs[pos:pos + 2]
        pos += 2
    og_ref = refs[pos]
    pos += 1
    if emit_state:
        sf_ref, sb_ref = refs[pos:pos + 2]
        pos += 2
    of_scr, ob_scr = refs[pos:pos + 2]

    blk = GLA_BLOCK
    ch = GLA_CHUNK
    nch = blk // ch
    dk = q_ref.shape[1] // heads
    dv = v_ref.shape[1] // heads
    shift = ch.bit_length() - 1
    row_in_chunk = lax.broadcasted_iota(I32, (blk, dk), 0) & (ch - 1)
    ri = lax.broadcasted_iota(I32, (blk, blk), 0)
    ci = lax.broadcasted_iota(I32, (blk, blk), 1)
    same = (ri >> shift) == (ci >> shift)
    mask_f = same & (ci <= ri)
    mask_b = same & (ci >= ri)

    def one_block(b0, hd, state, reverse):
        rows = pl.ds(b0, blk)
        kcols = slice(hd * dk, (hd + 1) * dk)
        q = q_ref[rows, kcols].astype(F32)
        k = k_ref[rows, kcols].astype(F32)
        v = v_ref[rows, hd * dv:(hd + 1) * dv]
        b = (lgb_ref if reverse else lgf_ref)[rows, kcols]
        mask = mask_b if reverse else mask_f
        s = 1
        while s < ch:
            if reverse:
                sh = pltpu.roll(b, blk - s, axis=0)
                b = b + jnp.where(row_in_chunk < ch - s, sh, 0.0)
            else:
                sh = pltpu.roll(b, s, axis=0)
                b = b + jnp.where(row_in_chunk >= s, sh, 0.0)
            s *= 2
        qe = (q * jnp.exp(b)).astype(BF16)
        ke = (k * jnp.exp(-b)).astype(BF16)
        a = jnp.where(mask, _dot_nt(qe, ke), 0.0).astype(BF16)

        order = range(nch - 1, -1, -1) if reverse else range(nch)
        end_row = [c * ch + (0 if reverse else ch - 1) for c in range(nch)]
        ends = [b[r:r + 1, :] for r in end_row]
        b_end = jnp.concatenate([jnp.broadcast_to(e, (ch, dk)) for e in ends], axis=0)
        kd = (k * jnp.exp(b_end - b)).astype(BF16)
        intra = _dot(a, v)
        decay = jnp.exp(jnp.concatenate(ends + [jnp.zeros((dk - nch, dk), F32)], axis=0)).T
        inter = [None] * nch
        for c in order:
            crow = slice(c * ch, (c + 1) * ch)
            kv_c = lax.dot_general(kd[crow], v[crow], (((0,), (0,)), ((), ())), preferred_element_type=F32)
            if state is None:
                inter[c] = jnp.zeros((ch, dv), F32)
                state = kv_c
            else:
                inter[c] = _dot(qe[c * ch:(c + 1) * ch], state.astype(BF16))
                state = decay[:, c:c + 1] * state + kv_c
        return intra + jnp.concatenate(inter, axis=0), state

    for sq in range(seqs):
        for hd in range(heads):
            vcols = slice(hd * dv, (hd + 1) * dv)
            srows = pl.ds(sq * nblk * blk, nblk * blk)
            sf = s0f_ref[sq, 0, hd] if has_state else None
            sb = s0b_ref[sq, 0, hd] if has_state else None
            for i in range(nblk):
                j = nblk - 1 - i
                o_f, sf = one_block((sq * nblk + i) * blk, hd, sf, False)
                o_b, sb = one_block((sq * nblk + j) * blk, hd, sb, True)
                of_scr[pl.ds((sq * nblk + i) * blk, blk), vcols] = o_f
                ob_scr[pl.ds((sq * nblk + j) * blk, blk), vcols] = o_b
            if emit_state:
                sf_ref[sq, 0, hd] = sf
                sb_ref[sq, 0, hd] = sb
            o = of_scr[srows, vcols] + ob_scr[srows, vcols]
            og_ref[srows, vcols] = (_rms(o, gg_ref[...]) * rg_ref[srows, vcols].astype(F32)).astype(BF16)


def _gla_call(q_g, k_g, v_g, lg_f, lg_b, r_g, g_gla, state_f, state_b, batch, seq, emit_state, heads, seqs):
    n, gkw = q_g.shape
    d = v_g.shape[1]
    dk, dv = gkw // GLA_HEADS, d // GLA_HEADS
    has_state = state_f is not None
    nblk = seq // GLA_BLOCK
    batch = batch // seqs
    seq = seq * seqs
    kspec = pl.BlockSpec((seq, heads * dk), lambda b, h: (b, h))
    vspec = pl.BlockSpec((seq, heads * dv), lambda b, h: (b, h))
    sspec = pl.BlockSpec((seqs, 1, heads, dk, dv), lambda b, h: (b, 0, h, 0, 0))
    in_specs = [kspec, kspec, vspec, kspec, kspec, vspec, pl.BlockSpec((1, dv), lambda b, h: (0, 0))]
    args = [q_g, k_g, v_g, lg_f, lg_b, r_g, g_gla]
    if has_state:
        in_specs += [sspec, sspec]
        args += [state_f, state_b]
    out_shape = [jax.ShapeDtypeStruct((n, d), BF16)]
    out_specs = [vspec]
    if emit_state:
        st = jax.ShapeDtypeStruct((batch * seqs, 1, GLA_HEADS, dk, dv), F32)
        out_shape += [st, st]
        out_specs += [sspec, sspec]
    vmem = 12 * seq * heads * dv * 4 + 40 * GLA_BLOCK * GLA_BLOCK * 4
    return pl.pallas_call(
        functools.partial(_gla_kernel, nblk=nblk, heads=heads, seqs=seqs, has_state=has_state,
                          emit_state=emit_state),
        out_shape=tuple(out_shape),
        grid=(batch, GLA_HEADS // heads),
        in_specs=in_specs,
        out_specs=tuple(out_specs),
        scratch_shapes=[pltpu.VMEM((seq, heads * dv), F32), pltpu.VMEM((seq, heads * dv), F32)],
        compiler_params=_cparams(("parallel", "parallel"), vmem),
        name="gla_ctx" if emit_state else "gla_lat",
    )(*args)


def _outproj_kernel(oa_ref, og_ref, gate_ref, x_ref, mod_ref, wpa_ref, wpg_ref, wout_ref, gpm_ref,
                    gpf_ref, wr_ref, x1_ref, h_ref, aff_ref):
    d = x_ref.shape[1]
    m = mod_ref[0]
    oa = _dot(oa_ref[...], wpa_ref[...])
    og = _dot(og_ref[...], wpg_ref[...])
    mix = gate_ref[:, 0:d].astype(F32) * oa + gate_ref[:, d:2 * d].astype(F32) * og
    mo = _dot(mix.astype(BF16), wout_ref[...])
    x1 = x_ref[...] + m[:, 2 * d:3 * d] * _rms(mo, gpm_ref[...])
    x1_ref[...] = x1
    hb = (_rms(x1, gpf_ref[...]) * (1.0 + m[:, 4 * d:5 * d]) + m[:, 3 * d:4 * d]).astype(BF16)
    bits = lax.bitcast_convert_type(hb.astype(F32), jnp.uint32)
    packed = (bits[:, 0:d // 2] >> 16) | (bits[:, d // 2:d] & jnp.uint32(0xFFFF0000))
    h_ref[...] = lax.bitcast_convert_type(packed, I32)
    logits = _dot(hb, wr_ref[...])
    valid = lax.broadcasted_iota(I32, logits.shape, 1) < N_EXPERTS
    mx = jnp.max(jnp.where(valid, logits, -jnp.inf), axis=-1, keepdims=True)
    ex = jnp.where(valid, jnp.exp(logits - mx), 0.0)
    aff_ref[...] = ex / jnp.sum(ex, axis=-1, keepdims=True)


def _outproj_call(o_a, o_g, gates, x2, mod3, mod_row_of_tile, w_pa, w_pg, w_out, g_pm, g_pf, w_r, tag):
    n, d = x2.shape
    tm = OUTPROJ_TILE
    row = lambda i: (i, 0)
    const = lambda i: (0, 0)
    wspec = pl.BlockSpec((d, d), const, pipeline_mode=pl.Buffered(1))
    vmem = 3 * d * d * 2 + 2 * tm * d * (2 + 2 + 4 + 4 + 4 + 2) + 8 * tm * d * 4
    return pl.pallas_call(
        _outproj_kernel,
        out_shape=(jax.ShapeDtypeStruct((n, d), F32), jax.ShapeDtypeStruct((n, d // 2), I32),
                   jax.ShapeDtypeStruct((n, V7X_LANES), F32)),
        grid=(n // tm,),
        in_specs=[pl.BlockSpec((tm, d), row), pl.BlockSpec((tm, d), row), pl.BlockSpec((tm, 2 * d), row),
                  pl.BlockSpec((tm, d), row),
                  pl.BlockSpec((1, 1, 6 * d), lambda i: (mod_row_of_tile(i, tm), 0, 0)),
                  wspec, wspec, wspec, pl.BlockSpec((1, d), const), pl.BlockSpec((1, d), const),
                  pl.BlockSpec((d, V7X_LANES), const)],
        out_specs=(pl.BlockSpec((tm, d), row), pl.BlockSpec((tm, d // 2), row),
                   pl.BlockSpec((tm, V7X_LANES), row)),
        compiler_params=_cparams(("parallel",), vmem),
        name="outproj_" + tag,
    )(o_a, o_g, gates, x2, mod3, w_pa, w_pg, w_out, g_pm, g_pf, w_r)


def _route_kernel(aff_ref, pos_ref, post_ref, tbl_ref, afft_scr, *, cap):
    n = aff_ref.shape[0]
    rb = ROUTE_BLOCK
    nb = n // rb
    lanes = aff_ref.shape[1]
    lane = lax.broadcasted_iota(I32, (1, lanes), 1)
    expert_lane = lane < N_EXPERTS
    tbl_ref[...] = jnp.zeros(tbl_ref.shape, I32)

    def to_token_lanes(c, carry):
        start = pl.multiple_of(c * rb, rb)
        afft_scr[c] = aff_ref[pl.ds(start, rb), :].T[0:N_EXPERTS, :]
        return carry

    lax.fori_loop(0, nb, to_token_lanes, 0)
    aff_t = afft_scr[...]

    def count(hit):
        return jnp.sum(jnp.sum(hit.astype(I32), axis=0), axis=1, keepdims=True)

    def bit_step(i, lo):
        t = lo | jnp.left_shift(jnp.int32(1), 30 - i)
        ge = aff_t >= lax.bitcast_convert_type(t, F32)[None]
        return jnp.where(count(ge) >= cap, t, lo)

    thr_bits = lax.fori_loop(0, 31, bit_step, jnp.zeros((N_EXPERTS, 1), I32))
    need_t = cap - count(aff_t > lax.bitcast_convert_type(thr_bits, F32)[None])

    def to_expert_lanes(col):
        full = jnp.concatenate([jnp.broadcast_to(col, (N_EXPERTS, lanes)),
                                jnp.zeros((lanes - N_EXPERTS, lanes), I32)], axis=0)
        return full.T[0:1, :]

    thr = lax.bitcast_convert_type(to_expert_lanes(thr_bits), F32)
    need = to_expert_lanes(need_t).astype(F32)
    capf = float(cap)

    r = lax.broadcasted_iota(I32, (rb, rb), 0)
    c_ = lax.broadcasted_iota(I32, (rb, rb), 1)
    tril = jnp.where(c_ <= r, 1.0, 0.0).astype(BF16)

    def blk_step(c, carry):
        eq_before, raw_before = carry
        start = pl.multiple_of(c * rb, rb)
        a = aff_ref[pl.ds(start, rb), :]
        gt = a > thr
        eq = a == thr
        eq_incl = _dot(tril, jnp.where(eq, 1.0, 0.0).astype(BF16)) + eq_before
        raw = (gt | (eq & (eq_incl <= need))) & expert_lane
        raw_incl = _dot(tril, jnp.where(raw, 1.0, 0.0).astype(BF16)) + raw_before
        sel = raw & (raw_incl <= capf)
        self_ = jnp.where(sel, 1.0, 0.0)
        incl = jnp.minimum(raw_incl, capf)
        sel_before = jnp.minimum(raw_before, capf)
        excl = incl - self_
        posb = jnp.where(sel, excl, -1.0).astype(I32)
        pos_ref[pl.ds(start, rb), :] = posb
        post_ref[c] = posb.T[0:N_EXPERTS, :]
        tbl_ref[pl.ds(c, 1), :] = sel_before.astype(I32)
        return (eq_incl[rb - 1:rb, :], raw_incl[rb - 1:rb, :])

    zero = jnp.zeros((1, lanes), F32)
    _, total = lax.fori_loop(0, nb, blk_step, (zero, zero), unroll=2)
    tbl_ref[pl.ds(nb, 1), :] = jnp.minimum(total, capf).astype(I32)


def _route_call(aff, cap, tag):
    n, lanes = aff.shape
    nb = n // ROUTE_BLOCK
    tbl_rows = -(-(nb + 1) // 8) * 8
    full = lambda *shape: pl.BlockSpec(shape, lambda: tuple(0 for _ in shape))
    return pl.pallas_call(
        functools.partial(_route_kernel, cap=cap),
        out_shape=(jax.ShapeDtypeStruct((n, lanes), I32),
                   jax.ShapeDtypeStruct((nb, N_EXPERTS, ROUTE_BLOCK), I32),
                   jax.ShapeDtypeStruct((tbl_rows, lanes), I32)),
        in_specs=[full(n, lanes)],
        out_specs=(full(n, lanes), full(nb, N_EXPERTS, ROUTE_BLOCK), full(tbl_rows, lanes)),
        scratch_shapes=[pltpu.VMEM((nb, N_EXPERTS, ROUTE_BLOCK), F32)],
        compiler_params=_cparams((), 8 * n * lanes * 4),
        name="route_" + tag,
    )(aff)


def _sc_gather_call(table, post, cap):
    n_exp, n = post.shape
    words = table.shape[1]
    workers = V7X_SC_CORES * V7X_SC_SUBCORES
    parts = workers // n_exp
    chunk = SC_GATHER_CHUNK
    lanes = V7X_SC_LANES
    per_part = cap // parts
    assert parts * n_exp == workers and per_part % chunk == 0 and n % lanes == 0
    mesh = plsc.VectorSubcoreMesh(core_axis_name="c", subcore_axis_name="s",
                                  num_cores=V7X_SC_CORES, num_subcores=V7X_SC_SUBCORES)

    def body(table_hbm, post_hbm, out_hbm, pos_v, idx_v, rows_v, sem):
        wid = lax.axis_index("s") * V7X_SC_CORES + lax.axis_index("c")
        e = wid // parts
        part = wid % parts
        pltpu.sync_copy(post_hbm.at[e], pos_v)
        lane = lax.iota(I32, lanes)

        @pl.loop(0, n, step=lanes)
        def _(t0):
            p = pos_v[pl.ds(t0, lanes)]
            plsc.store_scatter(idx_v, [p], lane + t0, mask=p >= 0)

        @pl.loop(0, per_part // chunk)
        def _(j):
            off = pl.multiple_of(part * per_part + j * chunk, chunk)
            pltpu.async_copy(table_hbm.at[idx_v.at[pl.ds(off, chunk)]], rows_v, sem).wait()
            pltpu.sync_copy(rows_v, out_hbm.at[pl.ds(e * cap + off, chunk)])

    return pl.kernel(
        body,
        out_type=jax.ShapeDtypeStruct((n_exp * cap, words), table.dtype),
        mesh=mesh,
        scratch_types=[pltpu.VMEM((n,), I32), pltpu.VMEM((cap,), I32),
                       pltpu.VMEM((chunk, words), table.dtype), pltpu.SemaphoreType.DMA],
        compiler_params=pltpu.CompilerParams(needs_layout_passes=False),
        name="sc_gather",
    )(table, post)


def _ffn_kernel(*refs, caps):
    ng = len(caps)
    xs_refs = refs[:ng]
    w1_ref, w3_ref, w2_ref = refs[ng:ng + 3]
    ye_refs = refs[ng + 3:2 * ng + 3]
    row_off = [sum(caps[:g]) for g in range(ng)]

    def unpack(words):
        w = lax.bitcast_convert_type(words, jnp.uint32)
        lo = lax.bitcast_convert_type(w << 16, F32).astype(BF16)
        hi = lax.bitcast_convert_type(w & jnp.uint32(0xFFFF0000), F32).astype(BF16)
        return jnp.concatenate([lo, hi], axis=1)

    xs = jnp.concatenate([unpack(r[...]) for r in xs_refs], axis=0)
    hid = _silu(_dot(xs, w1_ref[...].astype(BF16))) * _dot(xs, w3_ref[...].astype(BF16))
    ye = _dot(hid.astype(BF16), w2_ref[...].astype(BF16)).astype(BF16)
    for g in range(ng):
        ye_refs[g][...] = ye[row_off[g]:row_off[g] + caps[g]]


def _ffn_call(groups, w1, w3, w2):
    caps = tuple(g[1] for g in groups)
    n_exp, d, dff = w1.shape
    rows = sum(caps)
    in_specs = [pl.BlockSpec((cap, d // 2), lambda e: (e, 0)) for cap in caps]
    in_specs += [pl.BlockSpec((None, d, dff), lambda e: (e, 0, 0)),
                 pl.BlockSpec((None, d, dff), lambda e: (e, 0, 0)),
                 pl.BlockSpec((None, dff, d), lambda e: (e, 0, 0))]
    vmem = 3 * d * dff * (2 * 4 + 2) + rows * d * (2 * 2 + 2 + 2 * 2) + 5 * rows * dff * 4
    return pl.pallas_call(
        functools.partial(_ffn_kernel, caps=caps),
        out_shape=tuple(jax.ShapeDtypeStruct((n_exp * cap, d), BF16) for cap in caps),
        grid=(n_exp,),
        in_specs=in_specs,
        out_specs=tuple(pl.BlockSpec((cap, d), lambda e: (e, 0)) for cap in caps),
        compiler_params=_cparams(("arbitrary",), vmem),
        name="ffn",
    )(*[g[0] for g in groups], w1, w3, w2)


def _combine_kernel(tbl_ref, ye_hbm, pos_ref, aff_ref, x1_ref, mod_ref, gpo_ref, y_ref, buf, sem, xbuf, xsem,
                    acc_scr, *, cap, blocks_per_tile):
    i = pl.program_id(0)
    nsteps = pl.num_programs(0)
    d = x1_ref.shape[1]
    lanes = pos_ref.shape[1]
    win = COMBINE_WINDOW
    last_start = ye_hbm.shape[0] - win
    slot = i % 2

    def first_row(step, e):
        return tbl_ref[step * blocks_per_tile, e] + e * cap

    def window_start(first, k):
        unclamped = (first // BF16_ROWS_PER_TILE) * BF16_ROWS_PER_TILE + k * win
        return unclamped, jnp.minimum(unclamped, last_start)

    def fetch(step, to_slot, e):
        start = window_start(first_row(step, e), 0)[1]
        return pltpu.make_async_copy(ye_hbm.at[pl.ds(pl.multiple_of(start, BF16_ROWS_PER_TILE), win), :],
                                     buf.at[to_slot, pl.ds(e * win, win), :], sem.at[to_slot, e])

    @pl.when(i == 0)
    def _prime():
        for e in range(N_EXPERTS):
            fetch(0, 0, e).start()

    @pl.when(i + 1 < nsteps)
    def _ahead():
        for e in range(N_EXPERTS):
            fetch(i + 1, 1 - slot, e).start()

    for e in range(N_EXPERTS):
        fetch(i, slot, e).wait()
    lane_row = lax.broadcasted_iota(I32, (1, win), 1)
    pieces = []
    for e in range(N_EXPERTS):
        pcol = pos_ref[:, e:e + 1]
        grow = jnp.where(pcol >= 0, pcol + e * cap, -1)
        start = window_start(first_row(i, e), 0)[1]
        pieces.append(jnp.where(grow == start + lane_row, aff_ref[:, e:e + 1], 0.0).astype(BF16))
    total = None
    for e0 in range(0, N_EXPERTS, COMBINE_EXPERT_GROUP):
        grp = jnp.concatenate(pieces[e0:e0 + COMBINE_EXPERT_GROUP], axis=1)
        part = _dot(grp, buf[slot, pl.ds(e0 * win, COMBINE_EXPERT_GROUP * win), :])
        total = part if total is None else total + part
    acc_scr[...] = total

    def extra_windows(e):
        covered = window_start(first_row(i, e), 1)[0]
        return jnp.maximum(first_row(i + 1, e) - covered + win - 1, 0) // win

    def expert_extra(e, carry):
        first = first_row(i, e)
        extra = extra_windows(e)

        def more(k, c):
            unclamped, start = window_start(first, k)
            cp = pltpu.make_async_copy(ye_hbm.at[pl.ds(pl.multiple_of(start, BF16_ROWS_PER_TILE), win), :],
                                       xbuf, xsem)
            cp.start()
            cp.wait()
            at_e = lax.broadcasted_iota(I32, (1, lanes), 1) == e
            pcol = jnp.sum(jnp.where(at_e, pos_ref[...].astype(F32), 0.0), axis=1, keepdims=True).astype(I32)
            wcol = jnp.sum(jnp.where(at_e, aff_ref[...], 0.0), axis=1, keepdims=True)
            grow = jnp.where(pcol >= 0, pcol + e * cap, -1)
            hit = (grow == start + lane_row) & (grow >= unclamped)
            acc_scr[...] += _dot(jnp.where(hit, wcol, 0.0).astype(BF16), xbuf[...])
            return c

        lax.fori_loop(1, 1 + extra, more, 0)
        return carry

    any_extra = extra_windows(0)
    for e in range(1, N_EXPERTS):
        any_extra = any_extra + extra_windows(e)

    @pl.when(any_extra > 0)
    def _overflow():
        lax.fori_loop(0, N_EXPERTS, expert_extra, 0)

    m = mod_ref[0]
    y_ref[...] = x1_ref[...] + m[:, 5 * d:6 * d] * _rms(acc_scr[...], gpo_ref[...])


def _combine_call(tbl, ye, pos, aff, x1, mod3, mod_row_of_tile, g_po, cap, tag):
    n, d = x1.shape
    tm = COMBINE_TILE
    lanes = pos.shape[1]
    grid_spec = pltpu.PrefetchScalarGridSpec(
        num_scalar_prefetch=1,
        grid=(n // tm,),
        in_specs=[pl.BlockSpec(memory_space=pl.ANY),
                  pl.BlockSpec((tm, lanes), lambda i, t: (i, 0)),
                  pl.BlockSpec((tm, lanes), lambda i, t: (i, 0)),
                  pl.BlockSpec((tm, d), lambda i, t: (i, 0)),
                  pl.BlockSpec((1, 1, 6 * d), lambda i, t: (mod_row_of_tile(i, tm), 0, 0)),
                  pl.BlockSpec((1, d), lambda i, t: (0, 0))],
        out_specs=pl.BlockSpec((tm, d), lambda i, t: (i, 0)),
        scratch_shapes=[pltpu.VMEM((2, N_EXPERTS * COMBINE_WINDOW, d), BF16),
                        pltpu.SemaphoreType.DMA((2, N_EXPERTS)),
                        pltpu.VMEM((COMBINE_WINDOW, d), BF16),
                        pltpu.SemaphoreType.DMA(()),
                        pltpu.VMEM((tm, d), F32)],
    )
    vmem = 2 * N_EXPERTS * COMBINE_WINDOW * d * 2 + 10 * tm * d * 4 + 2 * tm * N_EXPERTS * COMBINE_WINDOW * 4
    return pl.pallas_call(
        functools.partial(_combine_kernel, cap=cap, blocks_per_tile=tm // ROUTE_BLOCK),
        out_shape=jax.ShapeDtypeStruct((n, d), F32),
        grid_spec=grid_spec,
        compiler_params=_cparams(("arbitrary",), vmem),
        name="combine_" + tag,
    )(tbl, ye, pos, aff, x1, mod3, g_po)


def _rope_tables(seq):
    rows = seq // GRID_W
    r = jnp.repeat(jnp.arange(rows), GRID_W).astype(F32)
    col = jnp.tile(jnp.arange(GRID_W), rows).astype(F32)
    pairs = HEAD_DIM // 4
    freqs = ROPE_THETA ** (-jnp.arange(pairs, dtype=F32) / pairs)
    ang = jnp.concatenate([r[:, None] * freqs, col[:, None] * freqs], axis=-1)
    cos = jnp.repeat(jnp.cos(ang), 2, axis=-1)
    sin = jnp.repeat(jnp.sin(ang), 2, axis=-1)
    even = (jnp.arange(HEAD_DIM) % 2) == 0
    return cos, jnp.where(even, -sin, 0.0), jnp.where(even, 0.0, sin)


def _trunk_to_routing(x, mod3, mod_row_of_tile, rope_tabs, ctx, lw, tag):
    (g_pre_mix, g_post_mix, g_pre_ffn, g_post_ffn, w_in_p, g_q, g_k, wgf, bgf, wgb, bgb, g_gla,
     w_pa, w_pg, w_out, w_r, w1, w3, w2) = lw
    batch, seq, d = x.shape
    n = batch * seq
    x2 = x.reshape(n, d)
    (q_a, k_a, v_a, q_g, k_g, v_g, r_g, lg_f, lg_b, gates) = _inproj_call(
        x2, mod3, mod_row_of_tile, g_pre_mix, w_in_p, g_q, g_k, wgf, bgf, wgb, bgb, rope_tabs, batch, seq)
    if ctx is None:
        o_a = _attn_call(q_a, k_a, v_a, None, None)
        o_g, s_f, s_b = _gla_call(q_g, k_g, v_g, lg_f, lg_b, r_g, g_gla, None, None, batch, seq, True,
                                  GLA_CTX_HEADS_PER_STEP, GLA_CTX_SEQS_PER_STEP)
    else:
        ck, cv, s_f0, s_b0 = ctx
        o_a = _attn_call(q_a, k_a, v_a, ck, cv)
        (o_g,) = _gla_call(q_g, k_g, v_g, lg_f, lg_b, r_g, g_gla, s_f0, s_b0, batch, seq, False,
                           GLA_LAT_HEADS_PER_STEP, 1)
        s_f = s_b = None
    x1, h, aff = _outproj_call(o_a, o_g, gates, x2, mod3, mod_row_of_tile, w_pa, w_pg, w_out,
                               g_post_mix, g_pre_ffn, w_r, tag)
    cap = (EC_CAPACITY_FACTOR * n) // N_EXPERTS
    pos, post, tbl = _route_call(aff, cap, tag)
    xs = _sc_gather_call(h, post.transpose(1, 0, 2).reshape(N_EXPERTS, n), cap)
    return dict(x1=x1, xs=xs, aff=aff, pos=pos, tbl=tbl, cap=cap,
                mod_row_of_tile=mod_row_of_tile, tag=tag, shape=(batch, seq, d)), (k_a, v_a, s_f, s_b)


def _expert_ffn(groups, mod3, g_post_ffn, w1, w3, w2):
    yes = _ffn_call([(g["xs"], g["cap"]) for g in groups], w1, w3, w2)
    outs = []
    for g, ye in zip(groups, yes):
        y = _combine_call(g["tbl"], ye, g["pos"], g["aff"], g["x1"], mod3, g["mod_row_of_tile"], g_post_ffn,
                          g["cap"], g["tag"])
        outs.append(y.reshape(g["shape"]))
    return outs


def kernel(x_prompt, x_sample, cache_k, cache_v, state_gla_fwd, state_gla_bwd, c, c_ctx, g_pre_mix, g_post_mix, g_pre_ffn, g_post_ffn, w_mod, b_mod, w_in, g_q, g_k, w_gk2_f, b_gk_f, w_gk2_b, b_gk_b, g_gla, w_pa, w_pg, w_out, w_router, w1, w3, w2):
    depth = w_in.shape[0]
    assert depth == 1, "single trunk layer"
    d = x_prompt.shape[-1]
    dec_batch, dec_seq, _ = x_sample.shape
    assert dec_batch + 1 <= MOD_ROWS
    l = 0
    rank = GLA_GATE_RANK
    w_in_p = jnp.swapaxes(w_in[l], 0, 1).astype(BF16)
    gkw = w_gk2_f.shape[-1]
    wgf = jnp.zeros((V7X_LANES, gkw), F32).at[0:rank].set(w_gk2_f[l]).astype(BF16)
    wgb = jnp.zeros((V7X_LANES, gkw), F32).at[rank:2 * rank].set(w_gk2_b[l]).astype(BF16)
    w_r = jnp.zeros((d, V7X_LANES), F32).at[:, :N_EXPERTS].set(w_router[l]).astype(BF16)
    row = lambda a: a[l].reshape(1, -1)
    lw = (row(g_pre_mix), row(g_post_mix), row(g_pre_ffn), row(g_post_ffn), w_in_p, row(g_q), row(g_k),
          wgf, row(b_gk_f), wgb, row(b_gk_b), row(g_gla),
          w_pa[l].astype(BF16), w_pg[l].astype(BF16), w_out[l].astype(BF16), w_r, w1[l], w3[l], w2[l])

    cc = jnp.concatenate([c_ctx[None, :], c, jnp.zeros((MOD_ROWS - 1 - dec_batch, d), F32)], axis=0)
    mod = _mod_call(cc, w_mod[l], b_mod[l].reshape(1, -1))
    mod3 = mod.reshape(MOD_ROWS, 1, 6 * d)

    gp, (nk, nv, nsf, nsb) = _trunk_to_routing(x_prompt, mod3, lambda i, tm: 0, None, None, lw, "ctx")
    ctx = (cache_k, cache_v, state_gla_fwd, state_gla_bwd)
    gs, _ = _trunk_to_routing(x_sample, mod3, lambda i, tm: 1 + (i * tm) // dec_seq, _rope_tables(dec_seq),
                              ctx, lw, "lat")
    yp, ys = _expert_ffn([gp, gs], mod3, lw[3], lw[16], lw[17], lw[18])
    return (yp, ys, nk, nv, nsf, nsb)
```

```python
import functools

import jax
import jax.numpy as jnp
from jax import lax
from jax.experimental import pallas as pl
from jax.experimental.pallas import tpu as pltpu
from jax.experimental.pallas import tpu_sc as plsc

F32 = jnp.float32
BF16 = jnp.bfloat16
I32 = jnp.int32

N_HEADS = 8
N_KV_HEADS = 2
HEAD_DIM = 128
GRID_W = 64
ROPE_THETA = 10000.0
GLA_HEADS = 4
GLA_GATE_RANK = 16
GLA_GATE_NORM = 16.0
GLA_CHUNK = 64
N_EXPERTS = 16
EC_CAPACITY_FACTOR = 2
EPS = 1e-6

V7X_LANES = 128
V7X_VMEM_BYTES = 64 * 1024 * 1024
V7X_VMEM_RESERVE_BYTES = 6 * 1024 * 1024
BF16_ROWS_PER_TILE = 16
V7X_SC_CORES = 2
V7X_SC_SUBCORES = 16
V7X_SC_LANES = 16

TOKEN_TILE = 512
OUTPROJ_TILE = 512
ATTN_Q_TILE = 256
ATTN_SEQS_PER_STEP = 4
GLA_BLOCK = 256
GLA_CTX_HEADS_PER_STEP = 4
GLA_CTX_SEQS_PER_STEP = 4
GLA_LAT_HEADS_PER_STEP = 2
ROUTE_BLOCK = 256
SC_GATHER_CHUNK = 64
COMBINE_TILE = 512
COMBINE_WINDOW = 128
COMBINE_EXPERT_GROUP = 2
MOD_ROWS = 8
MOD_N_TILE = 1536


def _cparams(semantics, vmem_bytes):
    del vmem_bytes
    return pltpu.CompilerParams(dimension_semantics=semantics,
                                vmem_limit_bytes=V7X_VMEM_BYTES - V7X_VMEM_RESERVE_BYTES)


def _sigmoid(x):
    return 0.5 * (jnp.tanh(0.5 * x) + 1.0)


def _silu(x):
    return x * _sigmoid(x)


def _log_sigmoid(x):
    return jnp.minimum(x, 0.0) - jnp.log1p(jnp.exp(-jnp.abs(x)))


def _rms(x, g):
    ms = jnp.mean(x * x, axis=-1, keepdims=True)
    return x * lax.rsqrt(ms + EPS) * g


def _dot(a, b):
    return jnp.dot(a, b, preferred_element_type=F32)


def _dot_nt(a, b):
    return lax.dot_general(a, b, (((1,), (1,)), ((), ())), preferred_element_type=F32)


def _mod_kernel(c_ref, w_ref, b_ref, o_ref):
    s = _silu(c_ref[...]).astype(BF16)
    o_ref[...] = _dot(s, w_ref[...].astype(BF16)) + b_ref[...]


def _mod_call(cc, w_mod, b_mod):
    d, n6 = w_mod.shape
    tn = MOD_N_TILE
    return pl.pallas_call(
        _mod_kernel,
        out_shape=jax.ShapeDtypeStruct((MOD_ROWS, n6), F32),
        grid=(n6 // tn,),
        in_specs=[pl.BlockSpec((MOD_ROWS, d), lambda j: (0, 0)),
                  pl.BlockSpec((d, tn), lambda j: (0, j)),
                  pl.BlockSpec((1, tn), lambda j: (0, j))],
        out_specs=pl.BlockSpec((MOD_ROWS, tn), lambda j: (0, j)),
        compiler_params=_cparams(("arbitrary",), 3 * d * tn * 4),
        name="mod",
    )(cc, w_mod, b_mod)


def _inproj_layout(d):
    aq, akv = N_HEADS * HEAD_DIM, N_KV_HEADS * HEAD_DIM
    gk, gv = d // 2, d
    names = ("q_a", "k_a", "v_a", "q_g", "k_g", "v_g", "r_g", "gk_f", "gk_b", "gates")
    widths = (aq, akv, akv, gk, gk, gv, gv, GLA_GATE_RANK, GLA_GATE_RANK, 2 * d)
    off, o = {}, 0
    for nme, w in zip(names, widths):
        off[nme] = (o, o + w)
        o += w
    off["gk"] = (off["gk_f"][0], off["gk_f"][0] + V7X_LANES)
    return off, o


def _inproj_kernel(*refs, rope, d):
    if rope:
        (x_ref, mod_ref, gpre_ref, w_ref, gq_ref, gk_ref, wgf_ref, bgf_ref, wgb_ref, bgb_ref,
         cos_ref, se_ref, so_ref, *outs) = refs
    else:
        (x_ref, mod_ref, gpre_ref, w_ref, gq_ref, gk_ref, wgf_ref, bgf_ref, wgb_ref, bgb_ref,
         *outs) = refs
    qa_ref, k_ref, v_ref, qg_ref, kg_ref, vg_ref, rg_ref, lgf_ref, lgb_ref, gate_ref = outs
    off, _ = _inproj_layout(d)
    m = mod_ref[0]
    h = _rms(x_ref[...], gpre_ref[...]) * (1.0 + m[:, d:2 * d]) + m[:, 0:d]
    hb = h.astype(BF16)

    def proj(name):
        a, b = off[name]
        return _dot_nt(hb, w_ref[a:b, :])

    def qk_norm(y, g_ref):
        y = _rms(y, g_ref[...])
        if rope:
            nxt = pltpu.roll(y, HEAD_DIM - 1, axis=1)
            prv = pltpu.roll(y, 1, axis=1)
            y = y * cos_ref[...] + nxt * se_ref[...] + prv * so_ref[...]
        return y

    gk = proj("gk").astype(BF16)

    q = proj("q_a")
    scale = HEAD_DIM ** -0.5
    for hd in range(N_HEADS):
        sl = slice(hd * HEAD_DIM, (hd + 1) * HEAD_DIM)
        qa_ref[:, sl] = (qk_norm(q[:, sl], gq_ref) * scale).astype(BF16)

    k = proj("k_a")
    v = proj("v_a")
    tb, _, _, ts, _ = k_ref.shape
    for kv in range(N_KV_HEADS):
        sl = slice(kv * HEAD_DIM, (kv + 1) * HEAD_DIM)
        k_ref[:, 0, kv] = qk_norm(k[:, sl], gk_ref).reshape(tb, ts, HEAD_DIM)
        v_ref[:, 0, kv] = v[:, sl].reshape(tb, ts, HEAD_DIM)

    dk = (d // 2) // GLA_HEADS
    qg_ref[...] = (proj("q_g") * (dk ** -0.5)).astype(BF16)
    kg_ref[...] = proj("k_g").astype(BF16)
    vg_ref[...] = proj("v_g").astype(BF16)
    rg_ref[...] = _silu(proj("r_g")).astype(BF16)

    gate_ref[...] = _sigmoid(proj("gates")).astype(BF16)

    lgf_ref[...] = _log_sigmoid(_dot(gk, wgf_ref[...]) + bgf_ref[...]) * (1.0 / GLA_GATE_NORM)
    lgb_ref[...] = _log_sigmoid(_dot(gk, wgb_ref[...]) + bgb_ref[...]) * (1.0 / GLA_GATE_NORM)


def _inproj_call(x2, mod3, mod_row_of_tile, g_pre, w_in_p, g_q, g_k, wgf, bgf, wgb, bgb, rope_tabs,
                 batch, seq):
    n, d = x2.shape
    tm = TOKEN_TILE
    _, dinp = _inproj_layout(d)
    rope = rope_tabs is not None
    gk_w = d // 2
    if seq >= tm:
        tb, ts, per = 1, tm, seq // tm
        kv_map = lambda i: (i // per, 0, 0, i % per, 0)
    else:
        tb, ts, per = tm // seq, seq, 1
        kv_map = lambda i: (i, 0, 0, 0, 0)
    row = lambda i: (i, 0)
    const = lambda i: (0, 0)
    in_specs = [
        pl.BlockSpec((tm, d), row),
        pl.BlockSpec((1, 1, 6 * d), lambda i: (mod_row_of_tile(i, tm), 0, 0)),
        pl.BlockSpec((1, d), const),
        pl.BlockSpec((dinp, d), const, pipeline_mode=pl.Buffered(1)),
        pl.BlockSpec((1, HEAD_DIM), const),
        pl.BlockSpec((1, HEAD_DIM), const),
        pl.BlockSpec((V7X_LANES, gk_w), const),
        pl.BlockSpec((1, gk_w), const),
        pl.BlockSpec((V7X_LANES, gk_w), const),
        pl.BlockSpec((1, gk_w), const),
    ]
    args = [x2, mod3, g_pre, w_in_p, g_q, g_k, wgf, bgf, wgb, bgb]
    if rope:
        tab = pl.BlockSpec((tm, HEAD_DIM), lambda i: (i % per, 0))
        in_specs += [tab, tab, tab]
        args += list(rope_tabs)
    kv_shape = jax.ShapeDtypeStruct((batch, 1, N_KV_HEADS, seq, HEAD_DIM), F32)
    kv_spec = pl.BlockSpec((tb, 1, N_KV_HEADS, ts, HEAD_DIM), kv_map)
    out_shape = (
        jax.ShapeDtypeStruct((n, N_HEADS * HEAD_DIM), BF16), kv_shape, kv_shape,
        jax.ShapeDtypeStruct((n, gk_w), BF16), jax.ShapeDtypeStruct((n, gk_w), BF16),
        jax.ShapeDtypeStruct((n, d), BF16), jax.ShapeDtypeStruct((n, d), BF16),
        jax.ShapeDtypeStruct((n, gk_w), F32), jax.ShapeDtypeStruct((n, gk_w), F32),
        jax.ShapeDtypeStruct((n, 2 * d), BF16),
    )
    out_specs = (
        pl.BlockSpec((tm, N_HEADS * HEAD_DIM), row), kv_spec, kv_spec,
        pl.BlockSpec((tm, gk_w), row), pl.BlockSpec((tm, gk_w), row),
        pl.BlockSpec((tm, d), row), pl.BlockSpec((tm, d), row),
        pl.BlockSpec((tm, gk_w), row), pl.BlockSpec((tm, gk_w), row),
        pl.BlockSpec((tm, 2 * d), row),
    )
    out_row_bytes = 2 * (N_HEADS * HEAD_DIM + 2 * gk_w + 2 * d + 2 * d) + 4 * (4 * HEAD_DIM + 2 * gk_w)
    vmem = d * dinp * 2 + 2 * tm * (d * 4 + out_row_bytes) + 6 * tm * 2 * d * 4
    return pl.pallas_call(
        functools.partial(_inproj_kernel, rope=rope, d=d),
        out_shape=out_shape,
        grid=(n // tm,),
        in_specs=in_specs,
        out_specs=out_specs,
        compiler_params=_cparams(("parallel",), vmem),
        name="inproj_lat" if rope else "inproj_ctx",
    )(*args)


def _attn_kernel(*refs, cached):
    if cached:
        q_ref, k_ref, v_ref, ck_ref, cv_ref, o_ref = refs
    else:
        q_ref, k_ref, v_ref, o_ref = refs
    seqs = k_ref.shape[0]
    tq = q_ref.shape[0] // seqs
    grp = N_HEADS // N_KV_HEADS
    for sq in range(seqs):
        rows = pl.ds(sq * tq, tq)
        for kv in range(N_KV_HEADS):
            kk = k_ref[sq, 0, kv].astype(BF16)
            vv = v_ref[sq, 0, kv].astype(BF16)
            if cached:
                kk = jnp.concatenate([ck_ref[sq, 0, kv].astype(BF16), kk], axis=0)
                vv = jnp.concatenate([cv_ref[sq, 0, kv].astype(BF16), vv], axis=0)
            heads = [q_ref[rows, (kv * grp + g) * HEAD_DIM:(kv * grp + g + 1) * HEAD_DIM] for g in range(grp)]
            q4 = jnp.concatenate(heads, axis=0)
            s = _dot_nt(q4, kk)
            p = jnp.exp(s - jnp.max(s, axis=-1, keepdims=True))
            l = jnp.sum(p, axis=-1, keepdims=True)
            o = _dot(p.astype(BF16), vv) / l
            for g in range(grp):
                hd = kv * grp + g
                o_ref[rows, hd * HEAD_DIM:(hd + 1) * HEAD_DIM] = o[g * tq:(g + 1) * tq].astype(BF16)


def _attn_call(q_a, k_a, v_a, cache_k, cache_v):
    batch, _, _, seq, _ = k_a.shape
    n, aq = q_a.shape
    tq = min(ATTN_Q_TILE, seq)
    per = seq // tq
    cached = cache_k is not None
    seqs = ATTN_SEQS_PER_STEP if (per == 1 and not cached) else 1
    batch = batch // seqs
    own = pl.BlockSpec((seqs, 1, N_KV_HEADS, seq, HEAD_DIM), lambda b, j: (b, 0, 0, 0, 0))
    tq = tq * seqs
    in_specs = [pl.BlockSpec((tq, aq), lambda b, j: (b * per + j, 0)), own, own]
    args = [q_a, k_a, v_a]
    klen = seq
    if cached:
        past = cache_k.shape[3]
        cspec = pl.BlockSpec((1, 1, N_KV_HEADS, past, HEAD_DIM), lambda b, j: (b, 0, 0, 0, 0))
        in_specs += [cspec, cspec]
        args += [cache_k, cache_v]
        klen += past
    grp = N_HEADS // N_KV_HEADS
    vmem = 4 * grp * tq * klen * 4 + 8 * N_KV_HEADS * klen * HEAD_DIM * 4 + 8 * tq * aq * 2
    return pl.pallas_call(
        functools.partial(_attn_kernel, cached=cached),
        out_shape=jax.ShapeDtypeStruct((n, aq), BF16),
        grid=(batch, per),
        in_specs=in_specs,
        out_specs=pl.BlockSpec((tq, aq), lambda b, j: (b * per + j, 0)),
        compiler_params=_cparams(("parallel", "parallel"), vmem),
        name="attn_lat" if cached else "attn_ctx",
    )(*args)


def _gla_kernel(*refs, nblk, heads, seqs, has_state, emit_state):
    refs = list(refs)
    q_ref, k_ref, v_ref, lgf_ref, lgb_ref, rg_ref, gg_ref = refs[:7]
    pos = 7
    if has_state:
        s0f_ref, s0b_ref = refs[pos:pos + 2]
        pos += 2
    og_ref = refs[pos]
    pos += 1
    if emit_state:
        sf_ref, sb_ref = refs[pos:pos + 2]
        pos += 2
    of_scr, ob_scr = refs[pos:pos + 2]

    blk = GLA_BLOCK
    ch = GLA_CHUNK
    nch = blk // ch
    dk = q_ref.shape[1] // heads
    dv = v_ref.shape[1] // heads
    shift = ch.bit_length() - 1
    row_in_chunk = lax.broadcasted_iota(I32, (blk, dk), 0) & (ch - 1)
    ri = lax.broadcasted_iota(I32, (blk, blk), 0)
    ci = lax.broadcasted_iota(I32, (blk, blk), 1)
    same = (ri >> shift) == (ci >> shift)
    mask_f = same & (ci <= ri)
    mask_b = same & (ci >= ri)

    def one_block(b0, hd, state, reverse):
        rows = pl.ds(b0, blk)
        kcols = slice(hd * dk, (hd + 1) * dk)
        q = q_ref[rows, kcols].astype(F32)
        k = k_ref[rows, kcols].astype(F32)
        v = v_ref[rows, hd * dv:(hd + 1) * dv]
        b = (lgb_ref if reverse else lgf_ref)[rows, kcols]
        mask = mask_b if reverse else mask_f
        s = 1
        while s < ch:
            if reverse:
                sh = pltpu.roll(b, blk - s, axis=0)
                b = b + jnp.where(row_in_chunk < ch - s, sh, 0.0)
            else:
                sh = pltpu.roll(b, s, axis=0)
                b = b + jnp.where(row_in_chunk >= s, sh, 0.0)
            s *= 2
        qe = (q * jnp.exp(b)).astype(BF16)
        ke = (k * jnp.exp(-b)).astype(BF16)
        a = jnp.where(mask, _dot_nt(qe, ke), 0.0).astype(BF16)

        order = range(nch - 1, -1, -1) if reverse else range(nch)
        end_row = [c * ch + (0 if reverse else ch - 1) for c in range(nch)]
        ends = [b[r:r + 1, :] for r in end_row]
        b_end = jnp.concatenate([jnp.broadcast_to(e, (ch, dk)) for e in ends], axis=0)
        kd = (k * jnp.exp(b_end - b)).astype(BF16)
        intra = _dot(a, v)
        decay = jnp.exp(jnp.concatenate(ends + [jnp.zeros((dk - nch, dk), F32)], axis=0)).T
        inter = [None] * nch
        for c in order:
            crow = slice(c * ch, (c + 1) * ch)
            kv_c = lax.dot_general(kd[crow], v[crow], (((0,), (0,)), ((), ())), preferred_element_type=F32)
            if state is None:
                inter[c] = jnp.zeros((ch, dv), F32)
                state = kv_c
            else:
                inter[c] = _dot(qe[c * ch:(c + 1) * ch], state.astype(BF16))
                state = decay[:, c:c + 1] * state + kv_c
        return intra + jnp.concatenate(inter, axis=0), state

    for sq in range(seqs):
        for hd in range(heads):
            vcols = slice(hd * dv, (hd + 1) * dv)
            srows = pl.ds(sq * nblk * blk, nblk * blk)
            sf = s0f_ref[sq, 0, hd] if has_state else None
            sb = s0b_ref[sq, 0, hd] if has_state else None
            for i in range(nblk):
                j = nblk - 1 - i
                o_f, sf = one_block((sq * nblk + i) * blk, hd, sf, False)
                o_b, sb = one_block((sq * nblk + j) * blk, hd, sb, True)
                of_scr[pl.ds((sq * nblk + i) * blk, blk), vcols] = o_f
                ob_scr[pl.ds((sq * nblk + j) * blk, blk), vcols] = o_b
            if emit_state:
                sf_ref[sq, 0, hd] = sf
                sb_ref[sq, 0, hd] = sb
            o = of_scr[srows, vcols] + ob_scr[srows, vcols]
            og_ref[srows, vcols] = (_rms(o, gg_ref[...]) * rg_ref[srows, vcols].astype(F32)).astype(BF16)


def _gla_call(q_g, k_g, v_g, lg_f, lg_b, r_g, g_gla, state_f, state_b, batch, seq, emit_state, heads, seqs):
    n, gkw = q_g.shape
    d = v_g.shape[1]
    dk, dv = gkw // GLA_HEADS, d // GLA_HEADS
    has_state = state_f is not None
    nblk = seq // GLA_BLOCK
    batch = batch // seqs
    seq = seq * seqs
    kspec = pl.BlockSpec((seq, heads * dk), lambda b, h: (b, h))
    vspec = pl.BlockSpec((seq, heads * dv), lambda b, h: (b, h))
    sspec = pl.BlockSpec((seqs, 1, heads, dk, dv), lambda b, h: (b, 0, h, 0, 0))
    in_specs = [kspec, kspec, vspec, kspec, kspec, vspec, pl.BlockSpec((1, dv), lambda b, h: (0, 0))]
    args = [q_g, k_g, v_g, lg_f, lg_b, r_g, g_gla]
    if has_state:
        in_specs += [sspec, sspec]
        args += [state_f, state_b]
    out_shape = [jax.ShapeDtypeStruct((n, d), BF16)]
    out_specs = [vspec]
    if emit_state:
        st = jax.ShapeDtypeStruct((batch * seqs, 1, GLA_HEADS, dk, dv), F32)
        out_shape += [st, st]
        out_specs += [sspec, sspec]
    vmem = 12 * seq * heads * dv * 4 + 40 * GLA_BLOCK * GLA_BLOCK * 4
    return pl.pallas_call(
        functools.partial(_gla_kernel, nblk=nblk, heads=heads, seqs=seqs, has_state=has_state,
                          emit_state=emit_state),
        out_shape=tuple(out_shape),
        grid=(batch, GLA_HEADS // heads),
        in_specs=in_specs,
        out_specs=tuple(out_specs),
        scratch_shapes=[pltpu.VMEM((seq, heads * dv), F32), pltpu.VMEM((seq, heads * dv), F32)],
        compiler_params=_cparams(("parallel", "parallel"), vmem),
        name="gla_ctx" if emit_state else "gla_lat",
    )(*args)


def _outproj_kernel(oa_ref, og_ref, gate_ref, x_ref, mod_ref, wpa_ref, wpg_ref, wout_ref, gpm_ref,
                    gpf_ref, wr_ref, x1_ref, h_ref, aff_ref):
    d = x_ref.shape[1]
    m = mod_ref[0]
    oa = _dot(oa_ref[...], wpa_ref[...])
    og = _dot(og_ref[...], wpg_ref[...])
    mix = gate_ref[:, 0:d].astype(F32) * oa + gate_ref[:, d:2 * d].astype(F32) * og
    mo = _dot(mix.astype(BF16), wout_ref[...])
    x1 = x_ref[...] + m[:, 2 * d:3 * d] * _rms(mo, gpm_ref[...])
    x1_ref[...] = x1
    hb = (_rms(x1, gpf_ref[...]) * (1.0 + m[:, 4 * d:5 * d]) + m[:, 3 * d:4 * d]).astype(BF16)
    bits = lax.bitcast_convert_type(hb.astype(F32), jnp.uint32)
    packed = (bits[:, 0:d // 2] >> 16) | (bits[:, d // 2:d] & jnp.uint32(0xFFFF0000))
    h_ref[...] = lax.bitcast_convert_type(packed, I32)
    logits = _dot(hb, wr_ref[...])
    valid = lax.broadcasted_iota(I32, logits.shape, 1) < N_EXPERTS
    mx = jnp.max(jnp.where(valid, logits, -jnp.inf), axis=-1, keepdims=True)
    ex = jnp.where(valid, jnp.exp(logits - mx), 0.0)
    aff_ref[...] = ex / jnp.sum(ex, axis=-1, keepdims=True)


def _outproj_call(o_a, o_g, gates, x2, mod3, mod_row_of_tile, w_pa, w_pg, w_out, g_pm, g_pf, w_r, tag):
    n, d = x2.shape
    tm = OUTPROJ_TILE
    row = lambda i: (i, 0)
    const = lambda i: (0, 0)
    wspec = pl.BlockSpec((d, d), const, pipeline_mode=pl.Buffered(1))
    vmem = 3 * d * d * 2 + 2 * tm * d * (2 + 2 + 4 + 4 + 4 + 2) + 8 * tm * d * 4
    return pl.pallas_call(
        _outproj_kernel,
        out_shape=(jax.ShapeDtypeStruct((n, d), F32), jax.ShapeDtypeStruct((n, d // 2), I32),
                   jax.ShapeDtypeStruct((n, V7X_LANES), F32)),
        grid=(n // tm,),
        in_specs=[pl.BlockSpec((tm, d), row), pl.BlockSpec((tm, d), row), pl.BlockSpec((tm, 2 * d), row),
                  pl.BlockSpec((tm, d), row),
                  pl.BlockSpec((1, 1, 6 * d), lambda i: (mod_row_of_tile(i, tm), 0, 0)),
                  wspec, wspec, wspec, pl.BlockSpec((1, d), const), pl.BlockSpec((1, d), const),
                  pl.BlockSpec((d, V7X_LANES), const)],
        out_specs=(pl.BlockSpec((tm, d), row), pl.BlockSpec((tm, d // 2), row),
                   pl.BlockSpec((tm, V7X_LANES), row)),
        compiler_params=_cparams(("parallel",), vmem),
        name="outproj_" + tag,
    )(o_a, o_g, gates, x2, mod3, w_pa, w_pg, w_out, g_pm, g_pf, w_r)


def _route_kernel(aff_ref, pos_ref, post_ref, tbl_ref, afft_scr, *, cap):
    n = aff_ref.shape[0]
    rb = ROUTE_BLOCK
    nb = n // rb
    lanes = aff_ref.shape[1]
    lane = lax.broadcasted_iota(I32, (1, lanes), 1)
    expert_lane = lane < N_EXPERTS
    tbl_ref[...] = jnp.zeros(tbl_ref.shape, I32)

    def to_token_lanes(c, carry):
        start = pl.multiple_of(c * rb, rb)
        afft_scr[c] = aff_ref[pl.ds(start, rb), :].T[0:N_EXPERTS, :]
        return carry

    lax.fori_loop(0, nb, to_token_lanes, 0)
    aff_t = afft_scr[...]

    def count(hit):
        return jnp.sum(jnp.sum(hit.astype(I32), axis=0), axis=1, keepdims=True)

    def bit_step(i, lo):
        t = lo | jnp.left_shift(jnp.int32(1), 30 - i)
        ge = aff_t >= lax.bitcast_convert_type(t, F32)[None]
        return jnp.where(count(ge) >= cap, t, lo)

    thr_bits = lax.fori_loop(0, 31, bit_step, jnp.zeros((N_EXPERTS, 1), I32))
    need_t = cap - count(aff_t > lax.bitcast_convert_type(thr_bits, F32)[None])

    def to_expert_lanes(col):
        full = jnp.concatenate([jnp.broadcast_to(col, (N_EXPERTS, lanes)),
                                jnp.zeros((lanes - N_EXPERTS, lanes), I32)], axis=0)
        return full.T[0:1, :]

    thr = lax.bitcast_convert_type(to_expert_lanes(thr_bits), F32)
    need = to_expert_lanes(need_t).astype(F32)
    capf = float(cap)

    r = lax.broadcasted_iota(I32, (rb, rb), 0)
    c_ = lax.broadcasted_iota(I32, (rb, rb), 1)
    tril = jnp.where(c_ <= r, 1.0, 0.0).astype(BF16)

    def blk_step(c, carry):
        eq_before, raw_before = carry
        start = pl.multiple_of(c * rb, rb)
        a = aff_ref[pl.ds(start, rb), :]
        gt = a > thr
        eq = a == thr
        eq_incl = _dot(tril, jnp.where(eq, 1.0, 0.0).astype(BF16)) + eq_before
        raw = (gt | (eq & (eq_incl <= need))) & expert_lane
        raw_incl = _dot(tril, jnp.where(raw, 1.0, 0.0).astype(BF16)) + raw_before
        sel = raw & (raw_incl <= capf)
        self_ = jnp.where(sel, 1.0, 0.0)
        incl = jnp.minimum(raw_incl, capf)
        sel_before = jnp.minimum(raw_before, capf)
        excl = incl - self_
        posb = jnp.where(sel, excl, -1.0).astype(I32)
        pos_ref[pl.ds(start, rb), :] = posb
        post_ref[c] = posb.T[0:N_EXPERTS, :]
        tbl_ref[pl.ds(c, 1), :] = sel_before.astype(I32)
        return (eq_incl[rb - 1:rb, :], raw_incl[rb - 1:rb, :])

    zero = jnp.zeros((1, lanes), F32)
    _, total = lax.fori_loop(0, nb, blk_step, (zero, zero), unroll=2)
    tbl_ref[pl.ds(nb, 1), :] = jnp.minimum(total, capf).astype(I32)


def _route_call(aff, cap, tag):
    n, lanes = aff.shape
    nb = n // ROUTE_BLOCK
    tbl_rows = -(-(nb + 1) // 8) * 8
    full = lambda *shape: pl.BlockSpec(shape, lambda: tuple(0 for _ in shape))
    return pl.pallas_call(
        functools.partial(_route_kernel, cap=cap),
        out_shape=(jax.ShapeDtypeStruct((n, lanes), I32),
                   jax.ShapeDtypeStruct((nb, N_EXPERTS, ROUTE_BLOCK), I32),
                   jax.ShapeDtypeStruct((tbl_rows, lanes), I32)),
        in_specs=[full(n, lanes)],
        out_specs=(full(n, lanes), full(nb, N_EXPERTS, ROUTE_BLOCK), full(tbl_rows, lanes)),
        scratch_shapes=[pltpu.VMEM((nb, N_EXPERTS, ROUTE_BLOCK), F32)],
        compiler_params=_cparams((), 8 * n * lanes * 4),
        name="route_" + tag,
    )(aff)


def _sc_gather_call(table, post, cap):
    n_exp, n = post.shape
    words = table.shape[1]
    workers = V7X_SC_CORES * V7X_SC_SUBCORES
    parts = workers // n_exp
    chunk = SC_GATHER_CHUNK
    lanes = V7X_SC_LANES
    per_part = cap // parts
    assert parts * n_exp == workers and per_part % chunk == 0 and n % lanes == 0
    mesh = plsc.VectorSubcoreMesh(core_axis_name="c", subcore_axis_name="s",
                                  num_cores=V7X_SC_CORES, num_subcores=V7X_SC_SUBCORES)

    def body(table_hbm, post_hbm, out_hbm, pos_v, idx_v, rows_v, sem):
        wid = lax.axis_index("s") * V7X_SC_CORES + lax.axis_index("c")
        e = wid // parts
        part = wid % parts
        pltpu.sync_copy(post_hbm.at[e], pos_v)
        lane = lax.iota(I32, lanes)

        @pl.loop(0, n, step=lanes)
        def _(t0):
            p = pos_v[pl.ds(t0, lanes)]
            plsc.store_scatter(idx_v, [p], lane + t0, mask=p >= 0)

        @pl.loop(0, per_part // chunk)
        def _(j):
            off = pl.multiple_of(part * per_part + j * chunk, chunk)
            pltpu.async_copy(table_hbm.at[idx_v.at[pl.ds(off, chunk)]], rows_v, sem).wait()
            pltpu.sync_copy(rows_v, out_hbm.at[pl.ds(e * cap + off, chunk)])

    return pl.kernel(
        body,
        out_type=jax.ShapeDtypeStruct((n_exp * cap, words), table.dtype),
        mesh=mesh,
        scratch_types=[pltpu.VMEM((n,), I32), pltpu.VMEM((cap,), I32),
                       pltpu.VMEM((chunk, words), table.dtype), pltpu.SemaphoreType.DMA],
        compiler_params=pltpu.CompilerParams(needs_layout_passes=False),
        name="sc_gather",
    )(table, post)


def _ffn_kernel(*refs, caps):
    ng = len(caps)
    xs_refs = refs[:ng]
    w1_ref, w3_ref, w2_ref = refs[ng:ng + 3]
    ye_refs = refs[ng + 3:2 * ng + 3]
    row_off = [sum(caps[:g]) for g in range(ng)]

    def unpack(words):
        w = lax.bitcast_convert_type(words, jnp.uint32)
        lo = lax.bitcast_convert_type(w << 16, F32).astype(BF16)
        hi = lax.bitcast_convert_type(w & jnp.uint32(0xFFFF0000), F32).astype(BF16)
        return jnp.concatenate([lo, hi], axis=1)

    xs = jnp.concatenate([unpack(r[...]) for r in xs_refs], axis=0)
    hid = _silu(_dot(xs, w1_ref[...].astype(BF16))) * _dot(xs, w3_ref[...].astype(BF16))
    ye = _dot(hid.astype(BF16), w2_ref[...].astype(BF16)).astype(BF16)
    for g in range(ng):
        ye_refs[g][...] = ye[row_off[g]:row_off[g] + caps[g]]


def _ffn_call(groups, w1, w3, w2):
    caps = tuple(g[1] for g in groups)
    n_exp, d, dff = w1.shape
    rows = sum(caps)
    in_specs = [pl.BlockSpec((cap, d // 2), lambda e: (e, 0)) for cap in caps]
    in_specs += [pl.BlockSpec((None, d, dff), lambda e: (e, 0, 0)),
                 pl.BlockSpec((None, d, dff), lambda e: (e, 0, 0)),
                 pl.BlockSpec((None, dff, d), lambda e: (e, 0, 0))]
    vmem = 3 * d * dff * (2 * 4 + 2) + rows * d * (2 * 2 + 2 + 2 * 2) + 5 * rows * dff * 4
    return pl.pallas_call(
        functools.partial(_ffn_kernel, caps=caps),
        out_shape=tuple(jax.ShapeDtypeStruct((n_exp * cap, d), BF16) for cap in caps),
        grid=(n_exp,),
        in_specs=in_specs,
        out_specs=tuple(pl.BlockSpec((cap, d), lambda e: (e, 0)) for cap in caps),
        compiler_params=_cparams(("arbitrary",), vmem),
        name="ffn",
    )(*[g[0] for g in groups], w1, w3, w2)


def _combine_kernel(tbl_ref, ye_hbm, pos_ref, aff_ref, x1_ref, mod_ref, gpo_ref, y_ref, buf, sem, xbuf, xsem,
                    acc_scr, *, cap, blocks_per_tile):
    i = pl.program_id(0)
    nsteps = pl.num_programs(0)
    d = x1_ref.shape[1]
    lanes = pos_ref.shape[1]
    win = COMBINE_WINDOW
    last_start = ye_hbm.shape[0] - win
    slot = i % 2

    def first_row(step, e):
        return tbl_ref[step * blocks_per_tile, e] + e * cap

    def window_start(first, k):
        unclamped = (first // BF16_ROWS_PER_TILE) * BF16_ROWS_PER_TILE + k * win
        return unclamped, jnp.minimum(unclamped, last_start)

    def fetch(step, to_slot, e):
        start = window_start(first_row(step, e), 0)[1]
        return pltpu.make_async_copy(ye_hbm.at[pl.ds(pl.multiple_of(start, BF16_ROWS_PER_TILE), win), :],
                                     buf.at[to_slot, pl.ds(e * win, win), :], sem.at[to_slot, e])

    @pl.when(i == 0)
    def _prime():
        for e in range(N_EXPERTS):
            fetch(0, 0, e).start()

    @pl.when(i + 1 < nsteps)
    def _ahead():
        for e in range(N_EXPERTS):
            fetch(i + 1, 1 - slot, e).start()

    for e in range(N_EXPERTS):
        fetch(i, slot, e).wait()
    lane_row = lax.broadcasted_iota(I32, (1, win), 1)
    pieces = []
    for e in range(N_EXPERTS):
        pcol = pos_ref[:, e:e + 1]
        grow = jnp.where(pcol >= 0, pcol + e * cap, -1)
        start = window_start(first_row(i, e), 0)[1]
        pieces.append(jnp.where(grow == start + lane_row, aff_ref[:, e:e + 1], 0.0).astype(BF16))
    total = None
    for e0 in range(0, N_EXPERTS, COMBINE_EXPERT_GROUP):
        grp = jnp.concatenate(pieces[e0:e0 + COMBINE_EXPERT_GROUP], axis=1)
        part = _dot(grp, buf[slot, pl.ds(e0 * win, COMBINE_EXPERT_GROUP * win), :])
        total = part if total is None else total + part
    acc_scr[...] = total

    def extra_windows(e):
        covered = window_start(first_row(i, e), 1)[0]
        return jnp.maximum(first_row(i + 1, e) - covered + win - 1, 0) // win

    def expert_extra(e, carry):
        first = first_row(i, e)
        extra = extra_windows(e)

        def more(k, c):
            unclamped, start = window_start(first, k)
            cp = pltpu.make_async_copy(ye_hbm.at[pl.ds(pl.multiple_of(start, BF16_ROWS_PER_TILE), win), :],
                                       xbuf, xsem)
            cp.start()
            cp.wait()
            at_e = lax.broadcasted_iota(I32, (1, lanes), 1) == e
            pcol = jnp.sum(jnp.where(at_e, pos_ref[...].astype(F32), 0.0), axis=1, keepdims=True).astype(I32)
            wcol = jnp.sum(jnp.where(at_e, aff_ref[...], 0.0), axis=1, keepdims=True)
            grow = jnp.where(pcol >= 0, pcol + e * cap, -1)
            hit = (grow == start + lane_row) & (grow >= unclamped)
            acc_scr[...] += _dot(jnp.where(hit, wcol, 0.0).astype(BF16), xbuf[...])
            return c

        lax.fori_loop(1, 1 + extra, more, 0)
        return carry

    any_extra = extra_windows(0)
    for e in range(1, N_EXPERTS):
        any_extra = any_extra + extra_windows(e)

    @pl.when(any_extra > 0)
    def _overflow():
        lax.fori_loop(0, N_EXPERTS, expert_extra, 0)

    m = mod_ref[0]
    y_ref[...] = x1_ref[...] + m[:, 5 * d:6 * d] * _rms(acc_scr[...], gpo_ref[...])


def _combine_call(tbl, ye, pos, aff, x1, mod3, mod_row_of_tile, g_po, cap, tag):
    n, d = x1.shape
    tm = COMBINE_TILE
    lanes = pos.shape[1]
    grid_spec = pltpu.PrefetchScalarGridSpec(
        num_scalar_prefetch=1,
        grid=(n // tm,),
        in_specs=[pl.BlockSpec(memory_space=pl.ANY),
                  pl.BlockSpec((tm, lanes), lambda i, t: (i, 0)),
                  pl.BlockSpec((tm, lanes), lambda i, t: (i, 0)),
                  pl.BlockSpec((tm, d), lambda i, t: (i, 0)),
                  pl.BlockSpec((1, 1, 6 * d), lambda i, t: (mod_row_of_tile(i, tm), 0, 0)),
                  pl.BlockSpec((1, d), lambda i, t: (0, 0))],
        out_specs=pl.BlockSpec((tm, d), lambda i, t: (i, 0)),
        scratch_shapes=[pltpu.VMEM((2, N_EXPERTS * COMBINE_WINDOW, d), BF16),
                        pltpu.SemaphoreType.DMA((2, N_EXPERTS)),
                        pltpu.VMEM((COMBINE_WINDOW, d), BF16),
                        pltpu.SemaphoreType.DMA(()),
                        pltpu.VMEM((tm, d), F32)],
    )
    vmem = 2 * N_EXPERTS * COMBINE_WINDOW * d * 2 + 10 * tm * d * 4 + 2 * tm * N_EXPERTS * COMBINE_WINDOW * 4
    return pl.pallas_call(
        functools.partial(_combine_kernel, cap=cap, blocks_per_tile=tm // ROUTE_BLOCK),
        out_shape=jax.ShapeDtypeStruct((n, d), F32),
        grid_spec=grid_spec,
        compiler_params=_cparams(("arbitrary",), vmem),
        name="combine_" + tag,
    )(tbl, ye, pos, aff, x1, mod3, g_po)


def _rope_tables(seq):
    rows = seq // GRID_W
    r = jnp.repeat(jnp.arange(rows), GRID_W).astype(F32)
    col = jnp.tile(jnp.arange(GRID_W), rows).astype(F32)
    pairs = HEAD_DIM // 4
    freqs = ROPE_THETA ** (-jnp.arange(pairs, dtype=F32) / pairs)
    ang = jnp.concatenate([r[:, None] * freqs, col[:, None] * freqs], axis=-1)
    cos = jnp.repeat(jnp.cos(ang), 2, axis=-1)
    sin = jnp.repeat(jnp.sin(ang), 2, axis=-1)
    even = (jnp.arange(HEAD_DIM) % 2) == 0
    return cos, jnp.where(even, -sin, 0.0), jnp.where(even, 0.0, sin)


def _trunk_to_routing(x, mod3, mod_row_of_tile, rope_tabs, ctx, lw, tag):
    (g_pre_mix, g_post_mix, g_pre_ffn, g_post_ffn, w_in_p, g_q, g_k, wgf, bgf, wgb, bgb, g_gla,
     w_pa, w_pg, w_out, w_r, w1, w3, w2) = lw
    batch, seq, d = x.shape
    n = batch * seq
    x2 = x.reshape(n, d)
    (q_a, k_a, v_a, q_g, k_g, v_g, r_g, lg_f, lg_b, gates) = _inproj_call(
        x2, mod3, mod_row_of_tile, g_pre_mix, w_in_p, g_q, g_k, wgf, bgf, wgb, bgb, rope_tabs, batch, seq)
    if ctx is None:
        o_a = _attn_call(q_a, k_a, v_a, None, None)
        o_g, s_f, s_b = _gla_call(q_g, k_g, v_g, lg_f, lg_b, r_g, g_gla, None, None, batch, seq, True,
                                  GLA_CTX_HEADS_PER_STEP, GLA_CTX_SEQS_PER_STEP)
    else:
        ck, cv, s_f0, s_b0 = ctx
        o_a = _attn_call(q_a, k_a, v_a, ck, cv)
        (o_g,) = _gla_call(q_g, k_g, v_g, lg_f, lg_b, r_g, g_gla, s_f0, s_b0, batch, seq, False,
                           GLA_LAT_HEADS_PER_STEP, 1)
        s_f = s_b = None
    x1, h, aff = _outproj_call(o_a, o_g, gates, x2, mod3, mod_row_of_tile, w_pa, w_pg, w_out,
                               g_post_mix, g_pre_ffn, w_r, tag)
    cap = (EC_CAPACITY_FACTOR * n) // N_EXPERTS
    pos, post, tbl = _route_call(aff, cap, tag)
    xs = _sc_gather_call(h, post.transpose(1, 0, 2).reshape(N_EXPERTS, n), cap)
    return dict(x1=x1, xs=xs, aff=aff, pos=pos, tbl=tbl, cap=cap,
                mod_row_of_tile=mod_row_of_tile, tag=tag, shape=(batch, seq, d)), (k_a, v_a, s_f, s_b)


def _expert_ffn(groups, mod3, g_post_ffn, w1, w3, w2):
    yes = _ffn_call([(g["xs"], g["cap"]) for g in groups], w1, w3, w2)
    outs = []
    for g, ye in zip(groups, yes):
        y = _combine_call(g["tbl"], ye, g["pos"], g["aff"], g["x1"], mod3, g["mod_row_of_tile"], g_post_ffn,
                          g["cap"], g["tag"])
        outs.append(y.reshape(g["shape"]))
    return outs


def kernel(x_prompt, x_sample, cache_k, cache_v, state_gla_fwd, state_gla_bwd, c, c_ctx, g_pre_mix, g_post_mix, g_pre_ffn, g_post_ffn, w_mod, b_mod, w_in, g_q, g_k, w_gk2_f, b_gk_f, w_gk2_b, b_gk_b, g_gla, w_pa, w_pg, w_out, w_router, w1, w3, w2):
    depth = w_in.shape[0]
    assert depth == 1, "single trunk layer"
    d = x_prompt.shape[-1]
    dec_batch, dec_seq, _ = x_sample.shape
    assert dec_batch + 1 <= MOD_ROWS
    l = 0
    rank = GLA_GATE_RANK
    w_in_p = jnp.swapaxes(w_in[l], 0, 1).astype(BF16)
    gkw = w_gk2_f.shape[-1]
    wgf = jnp.zeros((V7X_LANES, gkw), F32).at[0:rank].set(w_gk2_f[l]).astype(BF16)
    wgb = jnp.zeros((V7X_LANES, gkw), F32).at[rank:2 * rank].set(w_gk2_b[l]).astype(BF16)
    w_r = jnp.zeros((d, V7X_LANES), F32).at[:, :N_EXPERTS].set(w_router[l]).astype(BF16)
    row = lambda a: a[l].reshape(1, -1)
    lw = (row(g_pre_mix), row(g_post_mix), row(g_pre_ffn), row(g_post_ffn), w_in_p, row(g_q), row(g_k),
          wgf, row(b_gk_f), wgb, row(b_gk_b), row(g_gla),
          w_pa[l].astype(BF16), w_pg[l].astype(BF16), w_out[l].astype(BF16), w_r, w1[l], w3[l], w2[l])

    cc = jnp.concatenate([c_ctx[None, :], c, jnp.zeros((MOD_ROWS - 1 - dec_batch, d), F32)], axis=0)
    mod = _mod_call(cc, w_mod[l], b_mod[l].reshape(1, -1))
    mod3 = mod.reshape(MOD_ROWS, 1, 6 * d)

    gp, (nk, nv, nsf, nsb) = _trunk_to_routing(x_prompt, mod3, lambda i, tm: 0, None, None, lw, "ctx")
    ctx = (cache_k, cache_v, state_gla_fwd, state_gla_bwd)
    gs, _ = _trunk_to_routing(x_sample, mod3, lambda i, tm: 1 + (i * tm) // dec_seq, _rope_tables(dec_seq),
                              ctx, lw, "lat")
    yp, ys = _expert_ffn([gp, gs], mod3, lw[3], lw[16], lw[17], lw[18])
    return (yp, ys, nk, nv, nsf, nsb)
```

```python
import functools

import jax
import jax.numpy as jnp
from jax import lax
from jax.experimental import pallas as pl
from jax.experimental.pallas import tpu as pltpu
from jax.experimental.pallas import tpu_sc as plsc

F32 = jnp.float32
BF16 = jnp.bfloat16
I32 = jnp.int32

N_HEADS = 8
N_KV_HEADS = 2
HEAD_DIM = 128
GRID_W = 64
ROPE_THETA = 10000.0
GLA_HEADS = 4
GLA_GATE_RANK = 16
GLA_GATE_NORM = 16.0
GLA_CHUNK = 64
N_EXPERTS = 16
EC_CAPACITY_FACTOR = 2
EPS = 1e-6

V7X_LANES = 128
V7X_VMEM_BYTES = 64 * 1024 * 1024
V7X_VMEM_RESERVE_BYTES = 6 * 1024 * 1024
BF16_ROWS_PER_TILE = 16
V7X_SC_CORES = 2
V7X_SC_SUBCORES = 16
V7X_SC_LANES = 16

TOKEN_TILE = 512
OUTPROJ_TILE = 512
OUTPROJ_ROW_GROUPS = 4
ATTN_Q_TILE = 256
ATTN_SEQS_PER_STEP = 4
GLA_BLOCK = 256
GLA_CTX_HEADS_PER_STEP = 4
GLA_CTX_SEQS_PER_STEP = 4
GLA_LAT_HEADS_PER_STEP = 2
ROUTE_BLOCK = 256
SC_GATHER_CHUNK = 128
COMBINE_TILE = 512
COMBINE_WINDOW = 128
COMBINE_EXPERT_GROUP = 2
MOD_ROWS = 8
MOD_N_TILE = 1536


def _cparams(semantics, vmem_bytes):
    del vmem_bytes
    return pltpu.CompilerParams(dimension_semantics=semantics,
                                vmem_limit_bytes=V7X_VMEM_BYTES - V7X_VMEM_RESERVE_BYTES)


def _sigmoid(x):
    return 0.5 * (jnp.tanh(0.5 * x) + 1.0)


def _silu(x):
    return x * _sigmoid(x)


def _log_sigmoid(x):
    return jnp.minimum(x, 0.0) - jnp.log1p(jnp.exp(-jnp.abs(x)))


def _rms(x, g):
    ms = jnp.mean(x * x, axis=-1, keepdims=True)
    return x * lax.rsqrt(ms + EPS) * g


def _dot(a, b):
    return jnp.dot(a, b, preferred_element_type=F32)


def _dot_nt(a, b):
    return lax.dot_general(a, b, (((1,), (1,)), ((), ())), preferred_element_type=F32)


def _mod_kernel(c_ref, w_ref, b_ref, o_ref):
    s = _silu(c_ref[...]).astype(BF16)
    o_ref[...] = _dot(s, w_ref[...].astype(BF16)) + b_ref[...]


def _mod_call(cc, w_mod, b_mod):
    d, n6 = w_mod.shape
    tn = MOD_N_TILE
    return pl.pallas_call(
        _mod_kernel,
        out_shape=jax.ShapeDtypeStruct((MOD_ROWS, n6), F32),
        grid=(n6 // tn,),
        in_specs=[pl.BlockSpec((MOD_ROWS, d), lambda j: (0, 0)),
                  pl.BlockSpec((d, tn), lambda j: (0, j)),
                  pl.BlockSpec((1, tn), lambda j: (0, j))],
        out_specs=pl.BlockSpec((MOD_ROWS, tn), lambda j: (0, j)),
        compiler_params=_cparams(("arbitrary",), 3 * d * tn * 4),
        name="mod",
    )(cc, w_mod, b_mod)


def _inproj_layout(d):
    aq, akv = N_HEADS * HEAD_DIM, N_KV_HEADS * HEAD_DIM
    gk, gv = d // 2, d
    names = ("q_a", "k_a", "v_a", "q_g", "k_g", "v_g", "r_g", "gk_f", "gk_b", "gates")
    widths = (aq, akv, akv, gk, gk, gv, gv, GLA_GATE_RANK, GLA_GATE_RANK, 2 * d)
    off, o = {}, 0
    for nme, w in zip(names, widths):
        off[nme] = (o, o + w)
        o += w
    off["gk"] = (off["gk_f"][0], off["gk_f"][0] + V7X_LANES)
    return off, o


def _inproj_kernel(*refs, rope, d):
    if rope:
        (x_ref, mod_ref, gpre_ref, w_ref, gq_ref, gk_ref, wgf_ref, bgf_ref, wgb_ref, bgb_ref,
         cos_ref, se_ref, so_ref, *outs) = refs
    else:
        (x_ref, mod_ref, gpre_ref, w_ref, gq_ref, gk_ref, wgf_ref, bgf_ref, wgb_ref, bgb_ref,
         *outs) = refs
    qa_ref, k_ref, v_ref, qg_ref, kg_ref, vg_ref, rg_ref, lgf_ref, lgb_ref, gate_ref = outs
    off, _ = _inproj_layout(d)
    m = mod_ref[0]
    h = _rms(x_ref[...], gpre_ref[...]) * (1.0 + m[:, d:2 * d]) + m[:, 0:d]
    hb = h.astype(BF16)

    def proj(name):
        a, b = off[name]
        return _dot_nt(hb, w_ref[a:b, :])

    def qk_norm(y, g_ref):
        y = _rms(y, g_ref[...])
        if rope:
            nxt = pltpu.roll(y, HEAD_DIM - 1, axis=1)
            prv = pltpu.roll(y, 1, axis=1)
            y = y * cos_ref[...] + nxt * se_ref[...] + prv * so_ref[...]
        return y

    gk = proj("gk").astype(BF16)

    q = proj("q_a")
    scale = HEAD_DIM ** -0.5
    for hd in range(N_HEADS):
        sl = slice(hd * HEAD_DIM, (hd + 1) * HEAD_DIM)
        qa_ref[:, sl] = (qk_norm(q[:, sl], gq_ref) * scale).astype(BF16)

    k = proj("k_a")
    v = proj("v_a")
    tb, _, _, ts, _ = k_ref.shape
    for kv in range(N_KV_HEADS):
        sl = slice(kv * HEAD_DIM, (kv + 1) * HEAD_DIM)
        k_ref[:, 0, kv] = qk_norm(k[:, sl], gk_ref).reshape(tb, ts, HEAD_DIM)
        v_ref[:, 0, kv] = v[:, sl].reshape(tb, ts, HEAD_DIM)

    dk = (d // 2) // GLA_HEADS
    qg_ref[...] = (proj("q_g") * (dk ** -0.5)).astype(BF16)
    kg_ref[...] = proj("k_g").astype(BF16)
    vg_ref[...] = proj("v_g").astype(BF16)
    rg_ref[...] = _silu(proj("r_g")).astype(BF16)

    gate_ref[...] = _sigmoid(proj("gates")).astype(BF16)

    lgf_ref[...] = _log_sigmoid(_dot(gk, wgf_ref[...]) + bgf_ref[...]) * (1.0 / GLA_GATE_NORM)
    lgb_ref[...] = _log_sigmoid(_dot(gk, wgb_ref[...]) + bgb_ref[...]) * (1.0 / GLA_GATE_NORM)


def _inproj_call(x2, mod3, mod_row_of_tile, g_pre, w_in_p, g_q, g_k, wgf, bgf, wgb, bgb, rope_tabs,
                 batch, seq):
    n, d = x2.shape
    tm = TOKEN_TILE
    _, dinp = _inproj_layout(d)
    rope = rope_tabs is not None
    gk_w = d // 2
    if seq >= tm:
        tb, ts, per = 1, tm, seq // tm
        kv_map = lambda i: (i // per, 0, 0, i % per, 0)
    else:
        tb, ts, per = tm // seq, seq, 1
        kv_map = lambda i: (i, 0, 0, 0, 0)
    row = lambda i: (i, 0)
    const = lambda i: (0, 0)
    in_specs = [
        pl.BlockSpec((tm, d), row),
        pl.BlockSpec((1, 1, 6 * d), lambda i: (mod_row_of_tile(i, tm), 0, 0)),
        pl.BlockSpec((1, d), const),
        pl.BlockSpec((dinp, d), const, pipeline_mode=pl.Buffered(1)),
        pl.BlockSpec((1, HEAD_DIM), const),
        pl.BlockSpec((1, HEAD_DIM), const),
        pl.BlockSpec((V7X_LANES, gk_w), const),
        pl.BlockSpec((1, gk_w), const),
        pl.BlockSpec((V7X_LANES, gk_w), const),
        pl.BlockSpec((1, gk_w), const),
    ]
    args = [x2, mod3, g_pre, w_in_p, g_q, g_k, wgf, bgf, wgb, bgb]
    if rope:
        tab = pl.BlockSpec((tm, HEAD_DIM), lambda i: (i % per, 0))
        in_specs += [tab, tab, tab]
        args += list(rope_tabs)
    kv_shape = jax.ShapeDtypeStruct((batch, 1, N_KV_HEADS, seq, HEAD_DIM), F32)
    kv_spec = pl.BlockSpec((tb, 1, N_KV_HEADS, ts, HEAD_DIM), kv_map)
    out_shape = (
        jax.ShapeDtypeStruct((n, N_HEADS * HEAD_DIM), BF16), kv_shape, kv_shape,
        jax.ShapeDtypeStruct((n, gk_w), BF16), jax.ShapeDtypeStruct((n, gk_w), BF16),
        jax.ShapeDtypeStruct((n, d), BF16), jax.ShapeDtypeStruct((n, d), BF16),
        jax.ShapeDtypeStruct((n, gk_w), F32), jax.ShapeDtypeStruct((n, gk_w), F32),
        jax.ShapeDtypeStruct((n, 2 * d), BF16),
    )
    out_specs = (
        pl.BlockSpec((tm, N_HEADS * HEAD_DIM), row), kv_spec, kv_spec,
        pl.BlockSpec((tm, gk_w), row), pl.BlockSpec((tm, gk_w), row),
        pl.BlockSpec((tm, d), row), pl.BlockSpec((tm, d), row),
        pl.BlockSpec((tm, gk_w), row), pl.BlockSpec((tm, gk_w), row),
        pl.BlockSpec((tm, 2 * d), row),
    )
    out_row_bytes = 2 * (N_HEADS * HEAD_DIM + 2 * gk_w + 2 * d + 2 * d) + 4 * (4 * HEAD_DIM + 2 * gk_w)
    vmem = d * dinp * 2 + 2 * tm * (d * 4 + out_row_bytes) + 6 * tm * 2 * d * 4
    return pl.pallas_call(
        functools.partial(_inproj_kernel, rope=rope, d=d),
        out_shape=out_shape,
        grid=(n // tm,),
        in_specs=in_specs,
        out_specs=out_specs,
        compiler_params=_cparams(("parallel",), vmem),
        name="inproj_lat" if rope else "inproj_ctx",
    )(*args)


def _attn_kernel(*refs, cached):
    if cached:
        q_ref, k_ref, v_ref, ck_ref, cv_ref, o_ref = refs
    else:
        q_ref, k_ref, v_ref, o_ref = refs
    seqs = k_ref.shape[0]
    tq = q_ref.shape[0] // seqs
    grp = N_HEADS // N_KV_HEADS
    for sq in range(seqs):
        rows = pl.ds(sq * tq, tq)
        for kv in range(N_KV_HEADS):
            kk = k_ref[sq, 0, kv].astype(BF16)
            vv = v_ref[sq, 0, kv].astype(BF16)
            if cached:
                kk = jnp.concatenate([ck_ref[sq, 0, kv].astype(BF16), kk], axis=0)
                vv = jnp.concatenate([cv_ref[sq, 0, kv].astype(BF16), vv], axis=0)
            heads = [q_ref[rows, (kv * grp + g) * HEAD_DIM:(kv * grp + g + 1) * HEAD_DIM] for g in range(grp)]
            q4 = jnp.concatenate(heads, axis=0)
            s = _dot_nt(q4, kk)
            p = jnp.exp(s - jnp.max(s, axis=-1, keepdims=True))
            l = jnp.sum(p, axis=-1, keepdims=True)
            o = _dot(p.astype(BF16), vv) / l
            for g in range(grp):
                hd = kv * grp + g
                o_ref[rows, hd * HEAD_DIM:(hd + 1) * HEAD_DIM] = o[g * tq:(g + 1) * tq].astype(BF16)


def _attn_call(q_a, k_a, v_a, cache_k, cache_v):
    batch, _, _, seq, _ = k_a.shape
    n, aq = q_a.shape
    tq = min(ATTN_Q_TILE, seq)
    per = seq // tq
    cached = cache_k is not None
    seqs = ATTN_SEQS_PER_STEP if (per == 1 and not cached) else 1
    batch = batch // seqs
    own = pl.BlockSpec((seqs, 1, N_KV_HEADS, seq, HEAD_DIM), lambda b, j: (b, 0, 0, 0, 0))
    tq = tq * seqs
    in_specs = [pl.BlockSpec((tq, aq), lambda b, j: (b * per + j, 0)), own, own]
    args = [q_a, k_a, v_a]
    klen = seq
    if cached:
        past = cache_k.shape[3]
        cspec = pl.BlockSpec((1, 1, N_KV_HEADS, past, HEAD_DIM), lambda b, j: (b, 0, 0, 0, 0))
        in_specs += [cspec, cspec]
        args += [cache_k, cache_v]
        klen += past
    grp = N_HEADS // N_KV_HEADS
    vmem = 4 * grp * tq * klen * 4 + 8 * N_KV_HEADS * klen * HEAD_DIM * 4 + 8 * tq * aq * 2
    return pl.pallas_call(
        functools.partial(_attn_kernel, cached=cached),
        out_shape=jax.ShapeDtypeStruct((n, aq), BF16),
        grid=(batch, per),
        in_specs=in_specs,
        out_specs=pl.BlockSpec((tq, aq), lambda b, j: (b * per + j, 0)),
        compiler_params=_cparams(("parallel", "parallel"), vmem),
        name="attn_lat" if cached else "attn_ctx",
    )(*args)


def _gla_kernel(*refs, nblk, heads, seqs, has_state, emit_state):
    refs = list(refs)
    q_ref, k_ref, v_ref, lgf_ref, lgb_ref, rg_ref, gg_ref = refs[:7]
    pos = 7
    if has_state:
        s0f_ref, s0b_ref = refs[pos:pos + 2]
        pos += 2
    og_ref = refs[pos]
    pos += 1
    if emit_state:
        sf_ref, sb_ref = refs[pos:pos + 2]
        pos += 2
    of_scr, ob_scr = refs[pos:pos + 2]

    blk = GLA_BLOCK
    ch = GLA_CHUNK
    nch = blk // ch
    dk = q_ref.shape[1] // heads
    dv = v_ref.shape[1] // heads
    shift = ch.bit_length() - 1
    row_in_chunk = lax.broadcasted_iota(I32, (blk, dk), 0) & (ch - 1)
    ri = lax.broadcasted_iota(I32, (blk, blk), 0)
    ci = lax.broadcasted_iota(I32, (blk, blk), 1)
    same = (ri >> shift) == (ci >> shift)
    mask_f = same & (ci <= ri)
    mask_b = same & (ci >= ri)

    def one_block(b0, hd, state, reverse):
        rows = pl.ds(b0, blk)
        kcols = slice(hd * dk, (hd + 1) * dk)
        q = q_ref[rows, kcols].astype(F32)
        k = k_ref[rows, kcols].astype(F32)
        v = v_ref[rows, hd * dv:(hd + 1) * dv]
        b = (lgb_ref if reverse else lgf_ref)[rows, kcols]
        mask = mask_b if reverse else mask_f
        s = 1
        while s < ch:
            if reverse:
                sh = pltpu.roll(b, blk - s, axis=0)
                b = b + jnp.where(row_in_chunk < ch - s, sh, 0.0)
            else:
                sh = pltpu.roll(b, s, axis=0)
                b = b + jnp.where(row_in_chunk >= s, sh, 0.0)
            s *= 2
        qe = (q * jnp.exp(b)).astype(BF16)
        ke = (k * jnp.exp(-b)).astype(BF16)
        a = jnp.where(mask, _dot_nt(qe, ke), 0.0).astype(BF16)

        order = range(nch - 1, -1, -1) if reverse else range(nch)
        end_row = [c * ch + (0 if reverse else ch - 1) for c in range(nch)]
        ends = [b[r:r + 1, :] for r in end_row]
        b_end = jnp.concatenate([jnp.broadcast_to(e, (ch, dk)) for e in ends], axis=0)
        kd = (k * jnp.exp(b_end - b)).astype(BF16)
        intra = _dot(a, v)
        decay = jnp.exp(jnp.concatenate(ends + [jnp.zeros((dk - nch, dk), F32)], axis=0)).T
        inter = [None] * nch
        for c in order:
            crow = slice(c * ch, (c + 1) * ch)
            kv_c = lax.dot_general(kd[crow], v[crow], (((0,), (0,)), ((), ())), preferred_element_type=F32)
            if state is None:
                inter[c] = jnp.zeros((ch, dv), F32)
                state = kv_c
            else:
                inter[c] = _dot(qe[c * ch:(c + 1) * ch], state.astype(BF16))
                state = decay[:, c:c + 1] * state + kv_c
        return intra + jnp.concatenate(inter, axis=0), state

    for sq in range(seqs):
        for hd in range(heads):
            vcols = slice(hd * dv, (hd + 1) * dv)
            srows = pl.ds(sq * nblk * blk, nblk * blk)
            sf = s0f_ref[sq, 0, hd] if has_state else None
            sb = s0b_ref[sq, 0, hd] if has_state else None
            for i in range(nblk):
                j = nblk - 1 - i
                o_f, sf = one_block((sq * nblk + i) * blk, hd, sf, False)
                o_b, sb = one_block((sq * nblk + j) * blk, hd, sb, True)
                of_scr[pl.ds((sq * nblk + i) * blk, blk), vcols] = o_f
                ob_scr[pl.ds((sq * nblk + j) * blk, blk), vcols] = o_b
            if emit_state:
                sf_ref[sq, 0, hd] = sf
                sb_ref[sq, 0, hd] = sb
            o = of_scr[srows, vcols] + ob_scr[srows, vcols]
            og_ref[srows, vcols] = (_rms(o, gg_ref[...]) * rg_ref[srows, vcols].astype(F32)).astype(BF16)


def _gla_call(q_g, k_g, v_g, lg_f, lg_b, r_g, g_gla, state_f, state_b, batch, seq, emit_state, heads, seqs):
    n, gkw = q_g.shape
    d = v_g.shape[1]
    dk, dv = gkw // GLA_HEADS, d // GLA_HEADS
    has_state = state_f is not None
    nblk = seq // GLA_BLOCK
    batch = batch // seqs
    seq = seq * seqs
    kspec = pl.BlockSpec((seq, heads * dk), lambda b, h: (b, h))
    vspec = pl.BlockSpec((seq, heads * dv), lambda b, h: (b, h))
    sspec = pl.BlockSpec((seqs, 1, heads, dk, dv), lambda b, h: (b, 0, h, 0, 0))
    in_specs = [kspec, kspec, vspec, kspec, kspec, vspec, pl.BlockSpec((1, dv), lambda b, h: (0, 0))]
    args = [q_g, k_g, v_g, lg_f, lg_b, r_g, g_gla]
    if has_state:
        in_specs += [sspec, sspec]
        args += [state_f, state_b]
    out_shape = [jax.ShapeDtypeStruct((n, d), BF16)]
    out_specs = [vspec]
    if emit_state:
        st = jax.ShapeDtypeStruct((batch * seqs, 1, GLA_HEADS, dk, dv), F32)
        out_shape += [st, st]
        out_specs += [sspec, sspec]
    vmem = 12 * seq * heads * dv * 4 + 40 * GLA_BLOCK * GLA_BLOCK * 4
    return pl.pallas_call(
        functools.partial(_gla_kernel, nblk=nblk, heads=heads, seqs=seqs, has_state=has_state,
                          emit_state=emit_state),
        out_shape=tuple(out_shape),
        grid=(batch, GLA_HEADS // heads),
        in_specs=in_specs,
        out_specs=tuple(out_specs),
        scratch_shapes=[pltpu.VMEM((seq, heads * dv), F32), pltpu.VMEM((seq, heads * dv), F32)],
        compiler_params=_cparams(("parallel", "parallel"), vmem),
        name="gla_ctx" if emit_state else "gla_lat",
    )(*args)


def _outproj_kernel(oa_ref, og_ref, gate_ref, x_ref, mod_ref, wpa_ref, wpg_ref, wout_ref, gpm_ref,
                    gpf_ref, wr_ref, x1_ref, h_ref, aff_ref):
    d = x_ref.shape[1]
    m = mod_ref[0]
    sub = x_ref.shape[0] // OUTPROJ_ROW_GROUPS
    groups = [pl.ds(g * sub, sub) for g in range(OUTPROJ_ROW_GROUPS)]
    branch = [(_dot(oa_ref[r, :], wpa_ref[...]), _dot(og_ref[r, :], wpg_ref[...])) for r in groups]
    mo = jnp.concatenate(
        [_dot((gate_ref[r, 0:d].astype(F32) * oa + gate_ref[r, d:2 * d].astype(F32) * og).astype(BF16),
              wout_ref[...]) for r, (oa, og) in zip(groups, branch)], axis=0)
    x1 = x_ref[...] + m[:, 2 * d:3 * d] * _rms(mo, gpm_ref[...])
    x1_ref[...] = x1
    hb = (_rms(x1, gpf_ref[...]) * (1.0 + m[:, 4 * d:5 * d]) + m[:, 3 * d:4 * d]).astype(BF16)
    bits = lax.bitcast_convert_type(hb.astype(F32), jnp.uint32)
    packed = (bits[:, 0:d // 2] >> 16) | (bits[:, d // 2:d] & jnp.uint32(0xFFFF0000))
    h_ref[...] = lax.bitcast_convert_type(packed, I32)
    logits = _dot(hb, wr_ref[...])
    valid = lax.broadcasted_iota(I32, logits.shape, 1) < N_EXPERTS
    mx = jnp.max(jnp.where(valid, logits, -jnp.inf), axis=-1, keepdims=True)
    ex = jnp.where(valid, jnp.exp(logits - mx), 0.0)
    aff_ref[...] = ex / jnp.sum(ex, axis=-1, keepdims=True)


def _outproj_call(o_a, o_g, gates, x2, mod3, mod_row_of_tile, w_pa, w_pg, w_out, g_pm, g_pf, w_r, tag):
    n, d = x2.shape
    tm = OUTPROJ_TILE
    row = lambda i: (i, 0)
    const = lambda i: (0, 0)
    wspec = pl.BlockSpec((d, d), const, pipeline_mode=pl.Buffered(1))
    vmem = 3 * d * d * 2 + 2 * tm * d * (2 + 2 + 4 + 4 + 4 + 2) + 8 * tm * d * 4
    return pl.pallas_call(
        _outproj_kernel,
        out_shape=(jax.ShapeDtypeStruct((n, d), F32), jax.ShapeDtypeStruct((n, d // 2), I32),
                   jax.ShapeDtypeStruct((n, V7X_LANES), F32)),
        grid=(n // tm,),
        in_specs=[pl.BlockSpec((tm, d), row), pl.BlockSpec((tm, d), row), pl.BlockSpec((tm, 2 * d), row),
                  pl.BlockSpec((tm, d), row),
                  pl.BlockSpec((1, 1, 6 * d), lambda i: (mod_row_of_tile(i, tm), 0, 0)),
                  wspec, wspec, wspec, pl.BlockSpec((1, d), const), pl.BlockSpec((1, d), const),
                  pl.BlockSpec((d, V7X_LANES), const)],
        out_specs=(pl.BlockSpec((tm, d), row), pl.BlockSpec((tm, d // 2), row),
                   pl.BlockSpec((tm, V7X_LANES), row)),
        compiler_params=_cparams(("parallel",), vmem),
        name="outproj_" + tag,
    )(o_a, o_g, gates, x2, mod3, w_pa, w_pg, w_out, g_pm, g_pf, w_r)


def _route_kernel(aff_ref, pos_ref, post_ref, tbl_ref, afft_scr, *, cap):
    n = aff_ref.shape[0]
    rb = ROUTE_BLOCK
    nb = n // rb
    lanes = aff_ref.shape[1]
    lane = lax.broadcasted_iota(I32, (1, lanes), 1)
    expert_lane = lane < N_EXPERTS
    tbl_ref[...] = jnp.zeros(tbl_ref.shape, I32)

    def to_token_lanes(c, carry):
        start = pl.multiple_of(c * rb, rb)
        afft_scr[c] = aff_ref[pl.ds(start, rb), :].T[0:N_EXPERTS, :]
        return carry

    lax.fori_loop(0, nb, to_token_lanes, 0)
    aff_t = afft_scr[...]

    def count(hit):
        return jnp.sum(jnp.sum(hit.astype(I32), axis=0), axis=1, keepdims=True)

    def bit_step(i, lo):
        t = lo | jnp.left_shift(jnp.int32(1), 30 - i)
        ge = aff_t >= lax.bitcast_convert_type(t, F32)[None]
        return jnp.where(count(ge) >= cap, t, lo)

    thr_bits = lax.fori_loop(0, 31, bit_step, jnp.zeros((N_EXPERTS, 1), I32))
    need_t = cap - count(aff_t > lax.bitcast_convert_type(thr_bits, F32)[None])

    def to_expert_lanes(col):
        full = jnp.concatenate([jnp.broadcast_to(col, (N_EXPERTS, lanes)),
                                jnp.zeros((lanes - N_EXPERTS, lanes), I32)], axis=0)
        return full.T[0:1, :]

    thr = lax.bitcast_convert_type(to_expert_lanes(thr_bits), F32)
    need = to_expert_lanes(need_t).astype(F32)
    capf = float(cap)

    r = lax.broadcasted_iota(I32, (rb, rb), 0)
    c_ = lax.broadcasted_iota(I32, (rb, rb), 1)
    tril = jnp.where(c_ <= r, 1.0, 0.0).astype(BF16)

    def blk_step(c, carry):
        eq_before, raw_before = carry
        start = pl.multiple_of(c * rb, rb)
        a = aff_ref[pl.ds(start, rb), :]
        gt = a > thr
        eq = a == thr
        eq_incl = _dot(tril, jnp.where(eq, 1.0, 0.0).astype(BF16)) + eq_before
        raw = (gt | (eq & (eq_incl <= need))) & expert_lane
        raw_incl = _dot(tril, jnp.where(raw, 1.0, 0.0).astype(BF16)) + raw_before
        sel = raw & (raw_incl <= capf)
        self_ = jnp.where(sel, 1.0, 0.0)
        incl = jnp.minimum(raw_incl, capf)
        sel_before = jnp.minimum(raw_before, capf)
        excl = incl - self_
        posb = jnp.where(sel, excl, -1.0).astype(I32)
        pos_ref[pl.ds(start, rb), :] = posb
        post_ref[c] = posb.T[0:N_EXPERTS, :]
        tbl_ref[pl.ds(c, 1), :] = sel_before.astype(I32)
        return (eq_incl[rb - 1:rb, :], raw_incl[rb - 1:rb, :])

    zero = jnp.zeros((1, lanes), F32)
    _, total = lax.fori_loop(0, nb, blk_step, (zero, zero), unroll=2)
    tbl_ref[pl.ds(nb, 1), :] = jnp.minimum(total, capf).astype(I32)


def _route_call(aff, cap, tag):
    n, lanes = aff.shape
    nb = n // ROUTE_BLOCK
    tbl_rows = -(-(nb + 1) // 8) * 8
    full = lambda *shape: pl.BlockSpec(shape, lambda: tuple(0 for _ in shape))
    return pl.pallas_call(
        functools.partial(_route_kernel, cap=cap),
        out_shape=(jax.ShapeDtypeStruct((n, lanes), I32),
                   jax.ShapeDtypeStruct((nb, N_EXPERTS, ROUTE_BLOCK), I32),
                   jax.ShapeDtypeStruct((tbl_rows, lanes), I32)),
        in_specs=[full(n, lanes)],
        out_specs=(full(n, lanes), full(nb, N_EXPERTS, ROUTE_BLOCK), full(tbl_rows, lanes)),
        scratch_shapes=[pltpu.VMEM((nb, N_EXPERTS, ROUTE_BLOCK), F32)],
        compiler_params=_cparams((), 8 * n * lanes * 4),
        name="route_" + tag,
    )(aff)


def _sc_gather_call(table, post, cap):
    n_exp, n = post.shape
    words = table.shape[1]
    workers = V7X_SC_CORES * V7X_SC_SUBCORES
    parts = workers // n_exp
    chunk = SC_GATHER_CHUNK
    lanes = V7X_SC_LANES
    per_part = cap // parts
    assert parts * n_exp == workers and per_part % chunk == 0 and n % lanes == 0
    mesh = plsc.VectorSubcoreMesh(core_axis_name="c", subcore_axis_name="s",
                                  num_cores=V7X_SC_CORES, num_subcores=V7X_SC_SUBCORES)

    def body(table_hbm, post_hbm, out_hbm, pos_v, idx_v, rows_v, sem):
        wid = lax.axis_index("s") * V7X_SC_CORES + lax.axis_index("c")
        e = wid // parts
        part = wid % parts
        pltpu.sync_copy(post_hbm.at[e], pos_v)
        lane = lax.iota(I32, lanes)

        @pl.loop(0, n, step=lanes)
        def _(t0):
            p = pos_v[pl.ds(t0, lanes)]
            plsc.store_scatter(idx_v, [p], lane + t0, mask=p >= 0)

        @pl.loop(0, per_part // chunk)
        def _(j):
            off = pl.multiple_of(part * per_part + j * chunk, chunk)
            pltpu.async_copy(table_hbm.at[idx_v.at[pl.ds(off, chunk)]], rows_v, sem).wait()
            pltpu.sync_copy(rows_v, out_hbm.at[pl.ds(e * cap + off, chunk)])

    return pl.kernel(
        body,
        out_type=jax.ShapeDtypeStruct((n_exp * cap, words), table.dtype),
        mesh=mesh,
        scratch_types=[pltpu.VMEM((n,), I32), pltpu.VMEM((cap,), I32),
                       pltpu.VMEM((chunk, words), table.dtype), pltpu.SemaphoreType.DMA],
        compiler_params=pltpu.CompilerParams(needs_layout_passes=False),
        name="sc_gather",
    )(table, post)


def _ffn_kernel(*refs, caps):
    ng = len(caps)
    xs_refs = refs[:ng]
    w1_ref, w3_ref, w2_ref = refs[ng:ng + 3]
    ye_refs = refs[ng + 3:2 * ng + 3]
    row_off = [sum(caps[:g]) for g in range(ng)]

    def unpack(words):
        w = lax.bitcast_convert_type(words, jnp.uint32)
        lo = lax.bitcast_convert_type(w << 16, F32).astype(BF16)
        hi = lax.bitcast_convert_type(w & jnp.uint32(0xFFFF0000), F32).astype(BF16)
        return jnp.concatenate([lo, hi], axis=1)

    xs = jnp.concatenate([unpack(r[...]) for r in xs_refs], axis=0)
    hid = _silu(_dot(xs, w1_ref[...].astype(BF16))) * _dot(xs, w3_ref[...].astype(BF16))
    ye = _dot(hid.astype(BF16), w2_ref[...].astype(BF16)).astype(BF16)
    for g in range(ng):
        ye_refs[g][...] = ye[row_off[g]:row_off[g] + caps[g]]


def _ffn_call(groups, w1, w3, w2):
    caps = tuple(g[1] for g in groups)
    n_exp, d, dff = w1.shape
    rows = sum(caps)
    in_specs = [pl.BlockSpec((cap, d // 2), lambda e: (e, 0)) for cap in caps]
    in_specs += [pl.BlockSpec((None, d, dff), lambda e: (e, 0, 0)),
                 pl.BlockSpec((None, d, dff), lambda e: (e, 0, 0)),
                 pl.BlockSpec((None, dff, d), lambda e: (e, 0, 0))]
    vmem = 3 * d * dff * (2 * 4 + 2) + rows * d * (2 * 2 + 2 + 2 * 2) + 5 * rows * dff * 4
    return pl.pallas_call(
        functools.partial(_ffn_kernel, caps=caps),
        out_shape=tuple(jax.ShapeDtypeStruct((n_exp * cap, d), BF16) for cap in caps),
        grid=(n_exp,),
        in_specs=in_specs,
        out_specs=tuple(pl.BlockSpec((cap, d), lambda e: (e, 0)) for cap in caps),
        compiler_params=_cparams(("arbitrary",), vmem),
        name="ffn",
    )(*[g[0] for g in groups], w1, w3, w2)


def _combine_kernel(tbl_ref, ye_hbm, pos_ref, aff_ref, x1_ref, mod_ref, gpo_ref, y_ref, buf, sem, xbuf, xsem,
                    acc_scr, *, cap, blocks_per_tile):
    i = pl.program_id(0)
    nsteps = pl.num_programs(0)
    d = x1_ref.shape[1]
    lanes = pos_ref.shape[1]
    win = COMBINE_WINDOW
    last_start = ye_hbm.shape[0] - win
    slot = i % 2

    def first_row(step, e):
        return tbl_ref[step * blocks_per_tile, e] + e * cap

    def window_start(first, k):
        unclamped = (first // BF16_ROWS_PER_TILE) * BF16_ROWS_PER_TILE + k * win
        return unclamped, jnp.minimum(unclamped, last_start)

    def fetch(step, to_slot, e):
        start = window_start(first_row(step, e), 0)[1]
        return pltpu.make_async_copy(ye_hbm.at[pl.ds(pl.multiple_of(start, BF16_ROWS_PER_TILE), win), :],
                                     buf.at[to_slot, pl.ds(e * win, win), :], sem.at[to_slot, e])

    @pl.when(i == 0)
    def _prime():
        for e in range(N_EXPERTS):
            fetch(0, 0, e).start()

    @pl.when(i + 1 < nsteps)
    def _ahead():
        for e in range(N_EXPERTS):
            fetch(i + 1, 1 - slot, e).start()

    for e in range(N_EXPERTS):
        fetch(i, slot, e).wait()
    lane_row = lax.broadcasted_iota(I32, (1, win), 1)
    pieces = []
    for e in range(N_EXPERTS):
        pcol = pos_ref[:, e:e + 1]
        grow = jnp.where(pcol >= 0, pcol + e * cap, -1)
        start = window_start(first_row(i, e), 0)[1]
        pieces.append(jnp.where(grow == start + lane_row, aff_ref[:, e:e + 1], 0.0).astype(BF16))
    total = None
    for e0 in range(0, N_EXPERTS, COMBINE_EXPERT_GROUP):
        grp = jnp.concatenate(pieces[e0:e0 + COMBINE_EXPERT_GROUP], axis=1)
        part = _dot(grp, buf[slot, pl.ds(e0 * win, COMBINE_EXPERT_GROUP * win), :])
        total = part if total is None else total + part
    acc_scr[...] = total

    def extra_windows(e):
        covered = window_start(first_row(i, e), 1)[0]
        return jnp.maximum(first_row(i + 1, e) - covered + win - 1, 0) // win

    def expert_extra(e, carry):
        first = first_row(i, e)
        extra = extra_windows(e)

        def more(k, c):
            unclamped, start = window_start(first, k)
            cp = pltpu.make_async_copy(ye_hbm.at[pl.ds(pl.multiple_of(start, BF16_ROWS_PER_TILE), win), :],
                                       xbuf, xsem)
            cp.start()
            cp.wait()
            at_e = lax.broadcasted_iota(I32, (1, lanes), 1) == e
            pcol = jnp.sum(jnp.where(at_e, pos_ref[...].astype(F32), 0.0), axis=1, keepdims=True).astype(I32)
            wcol = jnp.sum(jnp.where(at_e, aff_ref[...], 0.0), axis=1, keepdims=True)
            grow = jnp.where(pcol >= 0, pcol + e * cap, -1)
            hit = (grow == start + lane_row) & (grow >= unclamped)
            acc_scr[...] += _dot(jnp.where(hit, wcol, 0.0).astype(BF16), xbuf[...])
            return c

        lax.fori_loop(1, 1 + extra, more, 0)
        return carry

    any_extra = extra_windows(0)
    for e in range(1, N_EXPERTS):
        any_extra = any_extra + extra_windows(e)

    @pl.when(any_extra > 0)
    def _overflow():
        lax.fori_loop(0, N_EXPERTS, expert_extra, 0)

    m = mod_ref[0]
    y_ref[...] = x1_ref[...] + m[:, 5 * d:6 * d] * _rms(acc_scr[...], gpo_ref[...])


def _combine_call(tbl, ye, pos, aff, x1, mod3, mod_row_of_tile, g_po, cap, tag):
    n, d = x1.shape
    tm = COMBINE_TILE
    lanes = pos.shape[1]
    grid_spec = pltpu.PrefetchScalarGridSpec(
        num_scalar_prefetch=1,
        grid=(n // tm,),
        in_specs=[pl.BlockSpec(memory_space=pl.ANY),
                  pl.BlockSpec((tm, lanes), lambda i, t: (i, 0)),
                  pl.BlockSpec((tm, lanes), lambda i, t: (i, 0)),
                  pl.BlockSpec((tm, d), lambda i, t: (i, 0)),
                  pl.BlockSpec((1, 1, 6 * d), lambda i, t: (mod_row_of_tile(i, tm), 0, 0)),
                  pl.BlockSpec((1, d), lambda i, t: (0, 0))],
        out_specs=pl.BlockSpec((tm, d), lambda i, t: (i, 0)),
        scratch_shapes=[pltpu.VMEM((2, N_EXPERTS * COMBINE_WINDOW, d), BF16),
                        pltpu.SemaphoreType.DMA((2, N_EXPERTS)),
                        pltpu.VMEM((COMBINE_WINDOW, d), BF16),
                        pltpu.SemaphoreType.DMA(()),
                        pltpu.VMEM((tm, d), F32)],
    )
    vmem = 2 * N_EXPERTS * COMBINE_WINDOW * d * 2 + 10 * tm * d * 4 + 2 * tm * N_EXPERTS * COMBINE_WINDOW * 4
    return pl.pallas_call(
        functools.partial(_combine_kernel, cap=cap, blocks_per_tile=tm // ROUTE_BLOCK),
        out_shape=jax.ShapeDtypeStruct((n, d), F32),
        grid_spec=grid_spec,
        compiler_params=_cparams(("arbitrary",), vmem),
        name="combine_" + tag,
    )(tbl, ye, pos, aff, x1, mod3, g_po)


def _rope_tables(seq):
    rows = seq // GRID_W
    r = jnp.repeat(jnp.arange(rows), GRID_W).astype(F32)
    col = jnp.tile(jnp.arange(GRID_W), rows).astype(F32)
    pairs = HEAD_DIM // 4
    freqs = ROPE_THETA ** (-jnp.arange(pairs, dtype=F32) / pairs)
    ang = jnp.concatenate([r[:, None] * freqs, col[:, None] * freqs], axis=-1)
    cos = jnp.repeat(jnp.cos(ang), 2, axis=-1)
    sin = jnp.repeat(jnp.sin(ang), 2, axis=-1)
    even = (jnp.arange(HEAD_DIM) % 2) == 0
    return cos, jnp.where(even, -sin, 0.0), jnp.where(even, 0.0, sin)


def _trunk_to_routing(x, mod3, mod_row_of_tile, rope_tabs, ctx, lw, tag):
    (g_pre_mix, g_post_mix, g_pre_ffn, g_post_ffn, w_in_p, g_q, g_k, wgf, bgf, wgb, bgb, g_gla,
     w_pa, w_pg, w_out, w_r, w1, w3, w2) = lw
    batch, seq, d = x.shape
    n = batch * seq
    x2 = x.reshape(n, d)
    (q_a, k_a, v_a, q_g, k_g, v_g, r_g, lg_f, lg_b, gates) = _inproj_call(
        x2, mod3, mod_row_of_tile, g_pre_mix, w_in_p, g_q, g_k, wgf, bgf, wgb, bgb, rope_tabs, batch, seq)
    if ctx is None:
        o_a = _attn_call(q_a, k_a, v_a, None, None)
        o_g, s_f, s_b = _gla_call(q_g, k_g, v_g, lg_f, lg_b, r_g, g_gla, None, None, batch, seq, True,
                                  GLA_CTX_HEADS_PER_STEP, GLA_CTX_SEQS_PER_STEP)
    else:
        ck, cv, s_f0, s_b0 = ctx
        o_a = _attn_call(q_a, k_a, v_a, ck, cv)
        (o_g,) = _gla_call(q_g, k_g, v_g, lg_f, lg_b, r_g, g_gla, s_f0, s_b0, batch, seq, False,
                           GLA_LAT_HEADS_PER_STEP, 1)
        s_f = s_b = None
    x1, h, aff = _outproj_call(o_a, o_g, gates, x2, mod3, mod_row_of_tile, w_pa, w_pg, w_out,
                               g_post_mix, g_pre_ffn, w_r, tag)
    cap = (EC_CAPACITY_FACTOR * n) // N_EXPERTS
    pos, post, tbl = _route_call(aff, cap, tag)
    xs = _sc_gather_call(h, post.transpose(1, 0, 2).reshape(N_EXPERTS, n), cap)
    return dict(x1=x1, xs=xs, aff=aff, pos=pos, tbl=tbl, cap=cap,
                mod_row_of_tile=mod_row_of_tile, tag=tag, shape=(batch, seq, d)), (k_a, v_a, s_f, s_b)


def _expert_ffn(groups, mod3, g_post_ffn, w1, w3, w2):
    yes = _ffn_call([(g["xs"], g["cap"]) for g in groups], w1, w3, w2)
    outs = []
    for g, ye in zip(groups, yes):
        y = _combine_call(g["tbl"], ye, g["pos"], g["aff"], g["x1"], mod3, g["mod_row_of_tile"], g_post_ffn,
                          g["cap"], g["tag"])
        outs.append(y.reshape(g["shape"]))
    return outs


def kernel(x_prompt, x_sample, cache_k, cache_v, state_gla_fwd, state_gla_bwd, c, c_ctx, g_pre_mix, g_post_mix, g_pre_ffn, g_post_ffn, w_mod, b_mod, w_in, g_q, g_k, w_gk2_f, b_gk_f, w_gk2_b, b_gk_b, g_gla, w_pa, w_pg, w_out, w_router, w1, w3, w2):
    depth = w_in.shape[0]
    assert depth == 1, "single trunk layer"
    d = x_prompt.shape[-1]
    dec_batch, dec_seq, _ = x_sample.shape
    assert dec_batch + 1 <= MOD_ROWS
    l = 0
    rank = GLA_GATE_RANK
    w_in_p = jnp.swapaxes(w_in[l], 0, 1).astype(BF16)
    gkw = w_gk2_f.shape[-1]
    wgf = jnp.zeros((V7X_LANES, gkw), F32).at[0:rank].set(w_gk2_f[l]).astype(BF16)
    wgb = jnp.zeros((V7X_LANES, gkw), F32).at[rank:2 * rank].set(w_gk2_b[l]).astype(BF16)
    w_r = jnp.zeros((d, V7X_LANES), F32).at[:, :N_EXPERTS].set(w_router[l]).astype(BF16)
    row = lambda a: a[l].reshape(1, -1)
    lw = (row(g_pre_mix), row(g_post_mix), row(g_pre_ffn), row(g_post_ffn), w_in_p, row(g_q), row(g_k),
          wgf, row(b_gk_f), wgb, row(b_gk_b), row(g_gla),
          w_pa[l].astype(BF16), w_pg[l].astype(BF16), w_out[l].astype(BF16), w_r, w1[l], w3[l], w2[l])

    cc = jnp.concatenate([c_ctx[None, :], c, jnp.zeros((MOD_ROWS - 1 - dec_batch, d), F32)], axis=0)
    mod = _mod_call(cc, w_mod[l], b_mod[l].reshape(1, -1))
    mod3 = mod.reshape(MOD_ROWS, 1, 6 * d)

    gp, (nk, nv, nsf, nsb) = _trunk_to_routing(x_prompt, mod3, lambda i, tm: 0, None, None, lw, "ctx")
    ctx = (cache_k, cache_v, state_gla_fwd, state_gla_bwd)
    gs, _ = _trunk_to_routing(x_sample, mod3, lambda i, tm: 1 + (i * tm) // dec_seq, _rope_tables(dec_seq),
                              ctx, lw, "lat")
    yp, ys = _expert_ffn([gp, gs], mod3, lw[3], lw[16], lw[17], lw[18])
    return (yp, ys, nk, nv, nsf, nsb)
```

```python
import functools

import jax
import jax.numpy as jnp
from jax import lax
from jax.experimental import pallas as pl
from jax.experimental.pallas import tpu as pltpu
from jax.experimental.pallas import tpu_sc as plsc

F32 = jnp.float32
BF16 = jnp.bfloat16
I32 = jnp.int32

N_HEADS = 8
N_KV_HEADS = 2
HEAD_DIM = 128
GRID_W = 64
ROPE_THETA = 10000.0
GLA_HEADS = 4
GLA_GATE_RANK = 16
GLA_GATE_NORM = 16.0
GLA_CHUNK = 64
N_EXPERTS = 16
EC_CAPACITY_FACTOR = 2
EPS = 1e-6

V7X_LANES = 128
V7X_VMEM_BYTES = 64 * 1024 * 1024
V7X_VMEM_RESERVE_BYTES = 6 * 1024 * 1024
BF16_ROWS_PER_TILE = 16
V7X_SC_CORES = 2
V7X_SC_SUBCORES = 16
V7X_SC_LANES = 16

TOKEN_TILE = 512
OUTPROJ_TILE = 512
OUTPROJ_ROW_GROUPS = 4
ATTN_Q_TILE = 256
ATTN_SEQS_PER_STEP = 4
GLA_BLOCK = 256
GLA_HEADS_PER_STAGE_GROUP = 2
GLA_CTX_HEADS_PER_STEP = 4
GLA_CTX_SEQS_PER_STEP = 4
GLA_LAT_HEADS_PER_STEP = 2
ROUTE_BLOCK = 256
SC_GATHER_CHUNK = 128
COMBINE_TILE = 512
COMBINE_WINDOW = 128
COMBINE_EXPERT_GROUP = 2
MOD_ROWS = 8
MOD_N_TILE = 1536


def _cparams(semantics, vmem_bytes):
    del vmem_bytes
    return pltpu.CompilerParams(dimension_semantics=semantics,
                                vmem_limit_bytes=V7X_VMEM_BYTES - V7X_VMEM_RESERVE_BYTES)


def _sigmoid(x):
    return 0.5 * (jnp.tanh(0.5 * x) + 1.0)


def _silu(x):
    return x * _sigmoid(x)


def _log_sigmoid(x):
    return jnp.minimum(x, 0.0) - jnp.log1p(jnp.exp(-jnp.abs(x)))


def _rms(x, g):
    ms = jnp.mean(x * x, axis=-1, keepdims=True)
    return x * lax.rsqrt(ms + EPS) * g


def _dot(a, b):
    return jnp.dot(a, b, preferred_element_type=F32)


def _dot_nt(a, b):
    return lax.dot_general(a, b, (((1,), (1,)), ((), ())), preferred_element_type=F32)


def _mod_kernel(c_ref, w_ref, b_ref, o_ref):
    s = _silu(c_ref[...]).astype(BF16)
    o_ref[...] = _dot(s, w_ref[...].astype(BF16)) + b_ref[...]


def _mod_call(cc, w_mod, b_mod):
    d, n6 = w_mod.shape
    tn = MOD_N_TILE
    return pl.pallas_call(
        _mod_kernel,
        out_shape=jax.ShapeDtypeStruct((MOD_ROWS, n6), F32),
        grid=(n6 // tn,),
        in_specs=[pl.BlockSpec((MOD_ROWS, d), lambda j: (0, 0)),
                  pl.BlockSpec((d, tn), lambda j: (0, j)),
                  pl.BlockSpec((1, tn), lambda j: (0, j))],
        out_specs=pl.BlockSpec((MOD_ROWS, tn), lambda j: (0, j)),
        compiler_params=_cparams(("arbitrary",), 3 * d * tn * 4),
        name="mod",
    )(cc, w_mod, b_mod)


def _inproj_layout(d):
    aq, akv = N_HEADS * HEAD_DIM, N_KV_HEADS * HEAD_DIM
    gk, gv = d // 2, d
    names = ("q_a", "k_a", "v_a", "q_g", "k_g", "v_g", "r_g", "gk_f", "gk_b", "gates")
    widths = (aq, akv, akv, gk, gk, gv, gv, GLA_GATE_RANK, GLA_GATE_RANK, 2 * d)
    off, o = {}, 0
    for nme, w in zip(names, widths):
        off[nme] = (o, o + w)
        o += w
    off["gk"] = (off["gk_f"][0], off["gk_f"][0] + V7X_LANES)
    return off, o


def _inproj_kernel(*refs, rope, d):
    if rope:
        (x_ref, mod_ref, gpre_ref, w_ref, gq_ref, gk_ref, wgf_ref, bgf_ref, wgb_ref, bgb_ref,
         cos_ref, se_ref, so_ref, *outs) = refs
    else:
        (x_ref, mod_ref, gpre_ref, w_ref, gq_ref, gk_ref, wgf_ref, bgf_ref, wgb_ref, bgb_ref,
         *outs) = refs
    qa_ref, k_ref, v_ref, qg_ref, kg_ref, vg_ref, rg_ref, lgf_ref, lgb_ref, gate_ref = outs
    off, _ = _inproj_layout(d)
    m = mod_ref[0]
    h = _rms(x_ref[...], gpre_ref[...]) * (1.0 + m[:, d:2 * d]) + m[:, 0:d]
    hb = h.astype(BF16)

    def proj(name):
        a, b = off[name]
        return _dot_nt(hb, w_ref[a:b, :])

    def qk_norm(y, g_ref):
        y = _rms(y, g_ref[...])
        if rope:
            nxt = pltpu.roll(y, HEAD_DIM - 1, axis=1)
            prv = pltpu.roll(y, 1, axis=1)
            y = y * cos_ref[...] + nxt * se_ref[...] + prv * so_ref[...]
        return y

    gk = proj("gk").astype(BF16)

    q = proj("q_a")
    scale = HEAD_DIM ** -0.5
    for hd in range(N_HEADS):
        sl = slice(hd * HEAD_DIM, (hd + 1) * HEAD_DIM)
        qa_ref[:, sl] = (qk_norm(q[:, sl], gq_ref) * scale).astype(BF16)

    k = proj("k_a")
    v = proj("v_a")
    tb, _, _, ts, _ = k_ref.shape
    for kv in range(N_KV_HEADS):
        sl = slice(kv * HEAD_DIM, (kv + 1) * HEAD_DIM)
        k_ref[:, 0, kv] = qk_norm(k[:, sl], gk_ref).reshape(tb, ts, HEAD_DIM)
        v_ref[:, 0, kv] = v[:, sl].reshape(tb, ts, HEAD_DIM)

    dk = (d // 2) // GLA_HEADS
    qg_ref[...] = (proj("q_g") * (dk ** -0.5)).astype(BF16)
    kg_ref[...] = proj("k_g").astype(BF16)
    vg_ref[...] = proj("v_g").astype(BF16)
    rg_ref[...] = _silu(proj("r_g")).astype(BF16)

    gate_ref[...] = _sigmoid(proj("gates")).astype(BF16)

    lgf_ref[...] = _log_sigmoid(_dot(gk, wgf_ref[...]) + bgf_ref[...]) * (1.0 / GLA_GATE_NORM)
    lgb_ref[...] = _log_sigmoid(_dot(gk, wgb_ref[...]) + bgb_ref[...]) * (1.0 / GLA_GATE_NORM)


def _inproj_call(x2, mod3, mod_row_of_tile, g_pre, w_in_p, g_q, g_k, wgf, bgf, wgb, bgb, rope_tabs,
                 batch, seq):
    n, d = x2.shape
    tm = TOKEN_TILE
    _, dinp = _inproj_layout(d)
    rope = rope_tabs is not None
    gk_w = d // 2
    if seq >= tm:
        tb, ts, per = 1, tm, seq // tm
        kv_map = lambda i: (i // per, 0, 0, i % per, 0)
    else:
        tb, ts, per = tm // seq, seq, 1
        kv_map = lambda i: (i, 0, 0, 0, 0)
    row = lambda i: (i, 0)
    const = lambda i: (0, 0)
    in_specs = [
        pl.BlockSpec((tm, d), row),
        pl.BlockSpec((1, 1, 6 * d), lambda i: (mod_row_of_tile(i, tm), 0, 0)),
        pl.BlockSpec((1, d), const),
        pl.BlockSpec((dinp, d), const, pipeline_mode=pl.Buffered(1)),
        pl.BlockSpec((1, HEAD_DIM), const),
        pl.BlockSpec((1, HEAD_DIM), const),
        pl.BlockSpec((V7X_LANES, gk_w), const),
        pl.BlockSpec((1, gk_w), const),
        pl.BlockSpec((V7X_LANES, gk_w), const),
        pl.BlockSpec((1, gk_w), const),
    ]
    args = [x2, mod3, g_pre, w_in_p, g_q, g_k, wgf, bgf, wgb, bgb]
    if rope:
        tab = pl.BlockSpec((tm, HEAD_DIM), lambda i: (i % per, 0))
        in_specs += [tab, tab, tab]
        args += list(rope_tabs)
    kv_shape = jax.ShapeDtypeStruct((batch, 1, N_KV_HEADS, seq, HEAD_DIM), F32)
    kv_spec = pl.BlockSpec((tb, 1, N_KV_HEADS, ts, HEAD_DIM), kv_map)
    out_shape = (
        jax.ShapeDtypeStruct((n, N_HEADS * HEAD_DIM), BF16), kv_shape, kv_shape,
        jax.ShapeDtypeStruct((n, gk_w), BF16), jax.ShapeDtypeStruct((n, gk_w), BF16),
        jax.ShapeDtypeStruct((n, d), BF16), jax.ShapeDtypeStruct((n, d), BF16),
        jax.ShapeDtypeStruct((n, gk_w), F32), jax.ShapeDtypeStruct((n, gk_w), F32),
        jax.ShapeDtypeStruct((n, 2 * d), BF16),
    )
    out_specs = (
        pl.BlockSpec((tm, N_HEADS * HEAD_DIM), row), kv_spec, kv_spec,
        pl.BlockSpec((tm, gk_w), row), pl.BlockSpec((tm, gk_w), row),
        pl.BlockSpec((tm, d), row), pl.BlockSpec((tm, d), row),
        pl.BlockSpec((tm, gk_w), row), pl.BlockSpec((tm, gk_w), row),
        pl.BlockSpec((tm, 2 * d), row),
    )
    out_row_bytes = 2 * (N_HEADS * HEAD_DIM + 2 * gk_w + 2 * d + 2 * d) + 4 * (4 * HEAD_DIM + 2 * gk_w)
    vmem = d * dinp * 2 + 2 * tm * (d * 4 + out_row_bytes) + 6 * tm * 2 * d * 4
    return pl.pallas_call(
        functools.partial(_inproj_kernel, rope=rope, d=d),
        out_shape=out_shape,
        grid=(n // tm,),
        in_specs=in_specs,
        out_specs=out_specs,
        compiler_params=_cparams(("parallel",), vmem),
        name="inproj_lat" if rope else "inproj_ctx",
    )(*args)


def _attn_kernel(*refs, cached):
    if cached:
        q_ref, k_ref, v_ref, ck_ref, cv_ref, o_ref = refs
    else:
        q_ref, k_ref, v_ref, o_ref = refs
    seqs = k_ref.shape[0]
    tq = q_ref.shape[0] // seqs
    grp = N_HEADS // N_KV_HEADS
    chains = [(sq, kv) for sq in range(seqs) for kv in range(N_KV_HEADS)]

    def scores_of(sq, kv):
        rows = pl.ds(sq * tq, tq)
        kk = k_ref[sq, 0, kv].astype(BF16)
        vv = v_ref[sq, 0, kv].astype(BF16)
        if cached:
            kk = jnp.concatenate([ck_ref[sq, 0, kv].astype(BF16), kk], axis=0)
            vv = jnp.concatenate([cv_ref[sq, 0, kv].astype(BF16), vv], axis=0)
        heads = [q_ref[rows, (kv * grp + g) * HEAD_DIM:(kv * grp + g + 1) * HEAD_DIM] for g in range(grp)]
        return _dot_nt(jnp.concatenate(heads, axis=0), kk), vv

    look_ahead = cached
    ahead = scores_of(*chains[0]) if look_ahead else None
    for n, (sq, kv) in enumerate(chains):
        s, vv = ahead if look_ahead else scores_of(sq, kv)
        if look_ahead and n + 1 < len(chains):
            ahead = scores_of(*chains[n + 1])
        rows = pl.ds(sq * tq, tq)
        p = jnp.exp(s - jnp.max(s, axis=-1, keepdims=True))
        l = jnp.sum(p, axis=-1, keepdims=True)
        o = _dot(p.astype(BF16), vv) / l
        for g in range(grp):
            hd = kv * grp + g
            o_ref[rows, hd * HEAD_DIM:(hd + 1) * HEAD_DIM] = o[g * tq:(g + 1) * tq].astype(BF16)


def _attn_call(q_a, k_a, v_a, cache_k, cache_v):
    batch, _, _, seq, _ = k_a.shape
    n, aq = q_a.shape
    tq = min(ATTN_Q_TILE, seq)
    per = seq // tq
    cached = cache_k is not None
    seqs = ATTN_SEQS_PER_STEP if (per == 1 and not cached) else 1
    batch = batch // seqs
    own = pl.BlockSpec((seqs, 1, N_KV_HEADS, seq, HEAD_DIM), lambda b, j: (b, 0, 0, 0, 0))
    tq = tq * seqs
    in_specs = [pl.BlockSpec((tq, aq), lambda b, j: (b * per + j, 0)), own, own]
    args = [q_a, k_a, v_a]
    klen = seq
    if cached:
        past = cache_k.shape[3]
        cspec = pl.BlockSpec((1, 1, N_KV_HEADS, past, HEAD_DIM), lambda b, j: (b, 0, 0, 0, 0))
        in_specs += [cspec, cspec]
        args += [cache_k, cache_v]
        klen += past
    grp = N_HEADS // N_KV_HEADS
    vmem = 4 * grp * tq * klen * 4 + 8 * N_KV_HEADS * klen * HEAD_DIM * 4 + 8 * tq * aq * 2
    return pl.pallas_call(
        functools.partial(_attn_kernel, cached=cached),
        out_shape=jax.ShapeDtypeStruct((n, aq), BF16),
        grid=(batch, per),
        in_specs=in_specs,
        out_specs=pl.BlockSpec((tq, aq), lambda b, j: (b * per + j, 0)),
        compiler_params=_cparams(("parallel", "parallel"), vmem),
        name="attn_lat" if cached else "attn_ctx",
    )(*args)


def _gla_kernel(*refs, nblk, heads, seqs, has_state, emit_state):
    refs = list(refs)
    q_ref, k_ref, v_ref, lgf_ref, lgb_ref, rg_ref, gg_ref = refs[:7]
    pos = 7
    if has_state:
        s0f_ref, s0b_ref = refs[pos:pos + 2]
        pos += 2
    og_ref = refs[pos]
    pos += 1
    if emit_state:
        sf_ref, sb_ref = refs[pos:pos + 2]
        pos += 2
    of_scr, ob_scr = refs[pos:pos + 2]

    blk = GLA_BLOCK
    ch = GLA_CHUNK
    nch = blk // ch
    dk = q_ref.shape[1] // heads
    dv = v_ref.shape[1] // heads
    shift = ch.bit_length() - 1
    row_in_chunk = lax.broadcasted_iota(I32, (blk, dk), 0) & (ch - 1)
    ri = lax.broadcasted_iota(I32, (blk, blk), 0)
    ci = lax.broadcasted_iota(I32, (blk, blk), 1)
    same = (ri >> shift) == (ci >> shift)
    mask_f = same & (ci <= ri)
    mask_b = same & (ci >= ri)

    def one_block(b0, hd, reverse):
        rows = pl.ds(b0, blk)
        kcols = slice(hd * dk, (hd + 1) * dk)
        q = q_ref[rows, kcols].astype(F32)
        k = k_ref[rows, kcols].astype(F32)
        v = v_ref[rows, hd * dv:(hd + 1) * dv]
        b = (lgb_ref if reverse else lgf_ref)[rows, kcols]
        mask = mask_b if reverse else mask_f
        s = 1
        while s < ch:
            if reverse:
                sh = pltpu.roll(b, blk - s, axis=0)
                b = b + jnp.where(row_in_chunk < ch - s, sh, 0.0)
            else:
                sh = pltpu.roll(b, s, axis=0)
                b = b + jnp.where(row_in_chunk >= s, sh, 0.0)
            s *= 2
        qe = (q * jnp.exp(b)).astype(BF16)
        ke = (k * jnp.exp(-b)).astype(BF16)
        scores = _dot_nt(qe, ke)
        end_row = [c * ch + (0 if reverse else ch - 1) for c in range(nch)]
        ends = [b[r:r + 1, :] for r in end_row]
        b_end = jnp.concatenate([jnp.broadcast_to(e, (ch, dk)) for e in ends], axis=0)
        kd = (k * jnp.exp(b_end - b)).astype(BF16)
        return dict(scores=scores, mask=mask, qe=qe, kd=kd, v=v, ends=ends, reverse=reverse)

    def chunk_products(w):
        return [lax.dot_general(w["kd"][c * ch:(c + 1) * ch], w["v"][c * ch:(c + 1) * ch],
                                (((0,), (0,)), ((), ())), preferred_element_type=F32) for c in range(nch)]

    def intra_chunk(w):
        return _dot(jnp.where(w["mask"], w["scores"], 0.0).astype(BF16), w["v"])

    def across_chunks(w, kv, state):
        decay = jnp.exp(jnp.concatenate(w["ends"] + [jnp.zeros((dk - nch, dk), F32)], axis=0)).T
        inter = [None] * nch
        for c in (range(nch - 1, -1, -1) if w["reverse"] else range(nch)):
            if state is None:
                inter[c] = jnp.zeros((ch, dv), F32)
                state = kv[c]
            else:
                inter[c] = _dot(w["qe"][c * ch:(c + 1) * ch], state.astype(BF16))
                state = decay[:, c:c + 1] * state + kv[c]
        return jnp.concatenate(inter, axis=0), state

    group = min(GLA_HEADS_PER_STAGE_GROUP, heads)
    for sq in range(seqs):
        srows = pl.ds(sq * nblk * blk, nblk * blk)
        for h0 in range(0, heads, group):
            hds = list(range(h0, h0 + group))
            states = {(hd, rev): ((s0b_ref if rev else s0f_ref)[sq, 0, hd] if has_state else None)
                      for hd in hds for rev in (False, True)}
            for i in range(nblk):
                block_of = {False: sq * nblk + i, True: sq * nblk + nblk - 1 - i}
                chains = [(hd, rev) for hd in hds for rev in (False, True)]
                work = {c: one_block(block_of[c[1]] * blk, c[0], c[1]) for c in chains}
                kvs = {c: chunk_products(work[c]) for c in chains}
                intra = {c: intra_chunk(work[c]) for c in chains}
                for c in chains:
                    hd, rev = c
                    inter, states[c] = across_chunks(work[c], kvs[c], states[c])
                    scr = ob_scr if rev else of_scr
                    scr[pl.ds(block_of[rev] * blk, blk), hd * dv:(hd + 1) * dv] = intra[c] + inter
            for hd in hds:
                vcols = slice(hd * dv, (hd + 1) * dv)
                if emit_state:
                    sf_ref[sq, 0, hd] = states[(hd, False)]
                    sb_ref[sq, 0, hd] = states[(hd, True)]
                o = of_scr[srows, vcols] + ob_scr[srows, vcols]
                og_ref[srows, vcols] = (_rms(o, gg_ref[...]) * rg_ref[srows, vcols].astype(F32)).astype(BF16)


def _gla_call(q_g, k_g, v_g, lg_f, lg_b, r_g, g_gla, state_f, state_b, batch, seq, emit_state, heads, seqs):
    n, gkw = q_g.shape
    d = v_g.shape[1]
    dk, dv = gkw // GLA_HEADS, d // GLA_HEADS
    has_state = state_f is not None
    nblk = seq // GLA_BLOCK
    batch = batch // seqs
    seq = seq * seqs
    kspec = pl.BlockSpec((seq, heads * dk), lambda b, h: (b, h))
    vspec = pl.BlockSpec((seq, heads * dv), lambda b, h: (b, h))
    sspec = pl.BlockSpec((seqs, 1, heads, dk, dv), lambda b, h: (b, 0, h, 0, 0))
    in_specs = [kspec, kspec, vspec, kspec, kspec, vspec, pl.BlockSpec((1, dv), lambda b, h: (0, 0))]
    args = [q_g, k_g, v_g, lg_f, lg_b, r_g, g_gla]
    if has_state:
        in_specs += [sspec, sspec]
        args += [state_f, state_b]
    out_shape = [jax.ShapeDtypeStruct((n, d), BF16)]
    out_specs = [vspec]
    if emit_state:
        st = jax.ShapeDtypeStruct((batch * seqs, 1, GLA_HEADS, dk, dv), F32)
        out_shape += [st, st]
        out_specs += [sspec, sspec]
    vmem = 12 * seq * heads * dv * 4 + 40 * GLA_BLOCK * GLA_BLOCK * 4
    return pl.pallas_call(
        functools.partial(_gla_kernel, nblk=nblk, heads=heads, seqs=seqs, has_state=has_state,
                          emit_state=emit_state),
        out_shape=tuple(out_shape),
        grid=(batch, GLA_HEADS // heads),
        in_specs=in_specs,
        out_specs=tuple(out_specs),
        scratch_shapes=[pltpu.VMEM((seq, heads * dv), F32), pltpu.VMEM((seq, heads * dv), F32)],
        compiler_params=_cparams(("parallel", "parallel"), vmem),
        name="gla_ctx" if emit_state else "gla_lat",
    )(*args)


def _outproj_kernel(oa_ref, og_ref, gate_ref, x_ref, mod_ref, wpa_ref, wpg_ref, wout_ref, gpm_ref,
                    gpf_ref, wr_ref, x1_ref, h_ref, aff_ref):
    d = x_ref.shape[1]
    m = mod_ref[0]
    sub = x_ref.shape[0] // OUTPROJ_ROW_GROUPS
    groups = [pl.ds(g * sub, sub) for g in range(OUTPROJ_ROW_GROUPS)]
    branch = [(_dot(oa_ref[r, :], wpa_ref[...]), _dot(og_ref[r, :], wpg_ref[...])) for r in groups]
    mo = jnp.concatenate(
        [_dot((gate_ref[r, 0:d].astype(F32) * oa + gate_ref[r, d:2 * d].astype(F32) * og).astype(BF16),
              wout_ref[...]) for r, (oa, og) in zip(groups, branch)], axis=0)
    x1 = x_ref[...] + m[:, 2 * d:3 * d] * _rms(mo, gpm_ref[...])
    x1_ref[...] = x1
    hb = (_rms(x1, gpf_ref[...]) * (1.0 + m[:, 4 * d:5 * d]) + m[:, 3 * d:4 * d]).astype(BF16)
    bits = lax.bitcast_convert_type(hb.astype(F32), jnp.uint32)
    packed = (bits[:, 0:d // 2] >> 16) | (bits[:, d // 2:d] & jnp.uint32(0xFFFF0000))
    h_ref[...] = lax.bitcast_convert_type(packed, I32)
    logits = _dot(hb, wr_ref[...])
    valid = lax.broadcasted_iota(I32, logits.shape, 1) < N_EXPERTS
    mx = jnp.max(jnp.where(valid, logits, -jnp.inf), axis=-1, keepdims=True)
    ex = jnp.where(valid, jnp.exp(logits - mx), 0.0)
    aff_ref[...] = ex / jnp.sum(ex, axis=-1, keepdims=True)


def _outproj_call(o_a, o_g, gates, x2, mod3, mod_row_of_tile, w_pa, w_pg, w_out, g_pm, g_pf, w_r, tag):
    n, d = x2.shape
    tm = OUTPROJ_TILE
    row = lambda i: (i, 0)
    const = lambda i: (0, 0)
    wspec = pl.BlockSpec((d, d), const, pipeline_mode=pl.Buffered(1))
    vmem = 3 * d * d * 2 + 2 * tm * d * (2 + 2 + 4 + 4 + 4 + 2) + 8 * tm * d * 4
    return pl.pallas_call(
        _outproj_kernel,
        out_shape=(jax.ShapeDtypeStruct((n, d), F32), jax.ShapeDtypeStruct((n, d // 2), I32),
                   jax.ShapeDtypeStruct((n, V7X_LANES), F32)),
        grid=(n // tm,),
        in_specs=[pl.BlockSpec((tm, d), row), pl.BlockSpec((tm, d), row), pl.BlockSpec((tm, 2 * d), row),
                  pl.BlockSpec((tm, d), row),
                  pl.BlockSpec((1, 1, 6 * d), lambda i: (mod_row_of_tile(i, tm), 0, 0)),
                  wspec, wspec, wspec, pl.BlockSpec((1, d), const), pl.BlockSpec((1, d), const),
                  pl.BlockSpec((d, V7X_LANES), const)],
        out_specs=(pl.BlockSpec((tm, d), row), pl.BlockSpec((tm, d // 2), row),
                   pl.BlockSpec((tm, V7X_LANES), row)),
        compiler_params=_cparams(("parallel",), vmem),
        name="outproj_" + tag,
    )(o_a, o_g, gates, x2, mod3, w_pa, w_pg, w_out, g_pm, g_pf, w_r)


def _route_kernel(aff_ref, pos_ref, post_ref, tbl_ref, afft_scr, *, cap):
    n = aff_ref.shape[0]
    rb = ROUTE_BLOCK
    nb = n // rb
    lanes = aff_ref.shape[1]
    lane = lax.broadcasted_iota(I32, (1, lanes), 1)
    expert_lane = lane < N_EXPERTS
    tbl_ref[...] = jnp.zeros(tbl_ref.shape, I32)

    def to_token_lanes(c, carry):
        start = pl.multiple_of(c * rb, rb)
        afft_scr[c] = aff_ref[pl.ds(start, rb), :].T[0:N_EXPERTS, :]
        return carry

    lax.fori_loop(0, nb, to_token_lanes, 0)
    aff_t = afft_scr[...]

    def count(hit):
        return jnp.sum(jnp.sum(hit.astype(I32), axis=0), axis=1, keepdims=True)

    def bit_step(i, lo):
        t = lo | jnp.left_shift(jnp.int32(1), 30 - i)
        ge = aff_t >= lax.bitcast_convert_type(t, F32)[None]
        return jnp.where(count(ge) >= cap, t, lo)

    thr_bits = lax.fori_loop(0, 31, bit_step, jnp.zeros((N_EXPERTS, 1), I32))
    need_t = cap - count(aff_t > lax.bitcast_convert_type(thr_bits, F32)[None])

    def to_expert_lanes(col):
        full = jnp.concatenate([jnp.broadcast_to(col, (N_EXPERTS, lanes)),
                                jnp.zeros((lanes - N_EXPERTS, lanes), I32)], axis=0)
        return full.T[0:1, :]

    thr = lax.bitcast_convert_type(to_expert_lanes(thr_bits), F32)
    need = to_expert_lanes(need_t).astype(F32)
    capf = float(cap)

    r = lax.broadcasted_iota(I32, (rb, rb), 0)
    c_ = lax.broadcasted_iota(I32, (rb, rb), 1)
    tril = jnp.where(c_ <= r, 1.0, 0.0).astype(BF16)

    def blk_step(c, carry):
        eq_before, raw_before = carry
        start = pl.multiple_of(c * rb, rb)
        a = aff_ref[pl.ds(start, rb), :]
        gt = a > thr
        eq = a == thr
        eq_incl = _dot(tril, jnp.where(eq, 1.0, 0.0).astype(BF16)) + eq_before
        raw = (gt | (eq & (eq_incl <= need))) & expert_lane
        raw_incl = _dot(tril, jnp.where(raw, 1.0, 0.0).astype(BF16)) + raw_before
        sel = raw & (raw_incl <= capf)
        self_ = jnp.where(sel, 1.0, 0.0)
        incl = jnp.minimum(raw_incl, capf)
        sel_before = jnp.minimum(raw_before, capf)
        excl = incl - self_
        posb = jnp.where(sel, excl, -1.0).astype(I32)
        pos_ref[pl.ds(start, rb), :] = posb
        post_ref[c] = posb.T[0:N_EXPERTS, :]
        tbl_ref[pl.ds(c, 1), :] = sel_before.astype(I32)
        return (eq_incl[rb - 1:rb, :], raw_incl[rb - 1:rb, :])

    zero = jnp.zeros((1, lanes), F32)
    _, total = lax.fori_loop(0, nb, blk_step, (zero, zero), unroll=2)
    tbl_ref[pl.ds(nb, 1), :] = jnp.minimum(total, capf).astype(I32)


def _route_call(aff, cap, tag):
    n, lanes = aff.shape
    nb = n // ROUTE_BLOCK
    tbl_rows = -(-(nb + 1) // 8) * 8
    full = lambda *shape: pl.BlockSpec(shape, lambda: tuple(0 for _ in shape))
    return pl.pallas_call(
        functools.partial(_route_kernel, cap=cap),
        out_shape=(jax.ShapeDtypeStruct((n, lanes), I32),
                   jax.ShapeDtypeStruct((nb, N_EXPERTS, ROUTE_BLOCK), I32),
                   jax.ShapeDtypeStruct((tbl_rows, lanes), I32)),
        in_specs=[full(n, lanes)],
        out_specs=(full(n, lanes), full(nb, N_EXPERTS, ROUTE_BLOCK), full(tbl_rows, lanes)),
        scratch_shapes=[pltpu.VMEM((nb, N_EXPERTS, ROUTE_BLOCK), F32)],
        compiler_params=_cparams((), 8 * n * lanes * 4),
        name="route_" + tag,
    )(aff)


def _sc_gather_call(table, post, cap):
    n_exp, n = post.shape
    words = table.shape[1]
    workers = V7X_SC_CORES * V7X_SC_SUBCORES
    parts = workers // n_exp
    chunk = SC_GATHER_CHUNK
    lanes = V7X_SC_LANES
    per_part = cap // parts
    assert parts * n_exp == workers and per_part % chunk == 0 and n % lanes == 0
    mesh = plsc.VectorSubcoreMesh(core_axis_name="c", subcore_axis_name="s",
                                  num_cores=V7X_SC_CORES, num_subcores=V7X_SC_SUBCORES)

    def body(table_hbm, post_hbm, out_hbm, pos_v, idx_v, rows_v, sem):
        wid = lax.axis_index("s") * V7X_SC_CORES + lax.axis_index("c")
        e = wid // parts
        part = wid % parts
        pltpu.sync_copy(post_hbm.at[e], pos_v)
        lane = lax.iota(I32, lanes)

        @pl.loop(0, n, step=lanes)
        def _(t0):
            p = pos_v[pl.ds(t0, lanes)]
            plsc.store_scatter(idx_v, [p], lane + t0, mask=p >= 0)

        @pl.loop(0, per_part // chunk)
        def _(j):
            off = pl.multiple_of(part * per_part + j * chunk, chunk)
            pltpu.async_copy(table_hbm.at[idx_v.at[pl.ds(off, chunk)]], rows_v, sem).wait()
            pltpu.sync_copy(rows_v, out_hbm.at[pl.ds(e * cap + off, chunk)])

    return pl.kernel(
        body,
        out_type=jax.ShapeDtypeStruct((n_exp * cap, words), table.dtype),
        mesh=mesh,
        scratch_types=[pltpu.VMEM((n,), I32), pltpu.VMEM((cap,), I32),
                       pltpu.VMEM((chunk, words), table.dtype), pltpu.SemaphoreType.DMA],
        compiler_params=pltpu.CompilerParams(needs_layout_passes=False),
        name="sc_gather",
    )(table, post)


def _ffn_kernel(*refs, caps):
    ng = len(caps)
    xs_refs = refs[:ng]
    w1_ref, w3_ref, w2_ref = refs[ng:ng + 3]
    ye_refs = refs[ng + 3:2 * ng + 3]
    row_off = [sum(caps[:g]) for g in range(ng)]

    def unpack(words):
        w = lax.bitcast_convert_type(words, jnp.uint32)
        lo = lax.bitcast_convert_type(w << 16, F32).astype(BF16)
        hi = lax.bitcast_convert_type(w & jnp.uint32(0xFFFF0000), F32).astype(BF16)
        return jnp.concatenate([lo, hi], axis=1)

    xs = jnp.concatenate([unpack(r[...]) for r in xs_refs], axis=0)
    hid = _silu(_dot(xs, w1_ref[...].astype(BF16))) * _dot(xs, w3_ref[...].astype(BF16))
    ye = _dot(hid.astype(BF16), w2_ref[...].astype(BF16)).astype(BF16)
    for g in range(ng):
        ye_refs[g][...] = ye[row_off[g]:row_off[g] + caps[g]]


def _ffn_call(groups, w1, w3, w2):
    caps = tuple(g[1] for g in groups)
    n_exp, d, dff = w1.shape
    rows = sum(caps)
    in_specs = [pl.BlockSpec((cap, d // 2), lambda e: (e, 0)) for cap in caps]
    in_specs += [pl.BlockSpec((None, d, dff), lambda e: (e, 0, 0)),
                 pl.BlockSpec((None, d, dff), lambda e: (e, 0, 0)),
                 pl.BlockSpec((None, dff, d), lambda e: (e, 0, 0))]
    vmem = 3 * d * dff * (2 * 4 + 2) + rows * d * (2 * 2 + 2 + 2 * 2) + 5 * rows * dff * 4
    return pl.pallas_call(
        functools.partial(_ffn_kernel, caps=caps),
        out_shape=tuple(jax.ShapeDtypeStruct((n_exp * cap, d), BF16) for cap in caps),
        grid=(n_exp,),
        in_specs=in_specs,
        out_specs=tuple(pl.BlockSpec((cap, d), lambda e: (e, 0)) for cap in caps),
        compiler_params=_cparams(("arbitrary",), vmem),
        name="ffn",
    )(*[g[0] for g in groups], w1, w3, w2)


def _combine_kernel(tbl_ref, ye_hbm, pos_ref, aff_ref, x1_ref, mod_ref, gpo_ref, y_ref, buf, sem, xbuf, xsem,
                    acc_scr, *, cap, blocks_per_tile):
    i = pl.program_id(0)
    nsteps = pl.num_programs(0)
    d = x1_ref.shape[1]
    lanes = pos_ref.shape[1]
    win = COMBINE_WINDOW
    last_start = ye_hbm.shape[0] - win
    slot = i % 2

    def first_row(step, e):
        return tbl_ref[step * blocks_per_tile, e] + e * cap

    def window_start(first, k):
        unclamped = (first // BF16_ROWS_PER_TILE) * BF16_ROWS_PER_TILE + k * win
        return unclamped, jnp.minimum(unclamped, last_start)

    def fetch(step, to_slot, e):
        start = window_start(first_row(step, e), 0)[1]
        return pltpu.make_async_copy(ye_hbm.at[pl.ds(pl.multiple_of(start, BF16_ROWS_PER_TILE), win), :],
                                     buf.at[to_slot, pl.ds(e * win, win), :], sem.at[to_slot, e])

    @pl.when(i == 0)
    def _prime():
        for e in range(N_EXPERTS):
            fetch(0, 0, e).start()

    @pl.when(i + 1 < nsteps)
    def _ahead():
        for e in range(N_EXPERTS):
            fetch(i + 1, 1 - slot, e).start()

    for e in range(N_EXPERTS):
        fetch(i, slot, e).wait()
    lane_row = lax.broadcasted_iota(I32, (1, win), 1)
    pieces = []
    for e in range(N_EXPERTS):
        pcol = pos_ref[:, e:e + 1]
        grow = jnp.where(pcol >= 0, pcol + e * cap, -1)
        start = window_start(first_row(i, e), 0)[1]
        pieces.append(jnp.where(grow == start + lane_row, aff_ref[:, e:e + 1], 0.0).astype(BF16))
    total = None
    for e0 in range(0, N_EXPERTS, COMBINE_EXPERT_GROUP):
        grp = jnp.concatenate(pieces[e0:e0 + COMBINE_EXPERT_GROUP], axis=1)
        part = _dot(grp, buf[slot, pl.ds(e0 * win, COMBINE_EXPERT_GROUP * win), :])
        total = part if total is None else total + part
    acc_scr[...] = total

    def extra_windows(e):
        covered = window_start(first_row(i, e), 1)[0]
        return jnp.maximum(first_row(i + 1, e) - covered + win - 1, 0) // win

    def expert_extra(e, carry):
        first = first_row(i, e)
        extra = extra_windows(e)

        def more(k, c):
            unclamped, start = window_start(first, k)
            cp = pltpu.make_async_copy(ye_hbm.at[pl.ds(pl.multiple_of(start, BF16_ROWS_PER_TILE), win), :],
                                       xbuf, xsem)
            cp.start()
            cp.wait()
            at_e = lax.broadcasted_iota(I32, (1, lanes), 1) == e
            pcol = jnp.sum(jnp.where(at_e, pos_ref[...].astype(F32), 0.0), axis=1, keepdims=True).astype(I32)
            wcol = jnp.sum(jnp.where(at_e, aff_ref[...], 0.0), axis=1, keepdims=True)
            grow = jnp.where(pcol >= 0, pcol + e * cap, -1)
            hit = (grow == start + lane_row) & (grow >= unclamped)
            acc_scr[...] += _dot(jnp.where(hit, wcol, 0.0).astype(BF16), xbuf[...])
            return c

        lax.fori_loop(1, 1 + extra, more, 0)
        return carry

    any_extra = extra_windows(0)
    for e in range(1, N_EXPERTS):
        any_extra = any_extra + extra_windows(e)

    @pl.when(any_extra > 0)
    def _overflow():
        lax.fori_loop(0, N_EXPERTS, expert_extra, 0)

    m = mod_ref[0]
    y_ref[...] = x1_ref[...] + m[:, 5 * d:6 * d] * _rms(acc_scr[...], gpo_ref[...])


def _combine_call(tbl, ye, pos, aff, x1, mod3, mod_row_of_tile, g_po, cap, tag):
    n, d = x1.shape
    tm = COMBINE_TILE
    lanes = pos.shape[1]
    grid_spec = pltpu.PrefetchScalarGridSpec(
        num_scalar_prefetch=1,
        grid=(n // tm,),
        in_specs=[pl.BlockSpec(memory_space=pl.ANY),
                  pl.BlockSpec((tm, lanes), lambda i, t: (i, 0)),
                  pl.BlockSpec((tm, lanes), lambda i, t: (i, 0)),
                  pl.BlockSpec((tm, d), lambda i, t: (i, 0)),
                  pl.BlockSpec((1, 1, 6 * d), lambda i, t: (mod_row_of_tile(i, tm), 0, 0)),
                  pl.BlockSpec((1, d), lambda i, t: (0, 0))],
        out_specs=pl.BlockSpec((tm, d), lambda i, t: (i, 0)),
        scratch_shapes=[pltpu.VMEM((2, N_EXPERTS * COMBINE_WINDOW, d), BF16),
                        pltpu.SemaphoreType.DMA((2, N_EXPERTS)),
                        pltpu.VMEM((COMBINE_WINDOW, d), BF16),
                        pltpu.SemaphoreType.DMA(()),
                        pltpu.VMEM((tm, d), F32)],
    )
    vmem = 2 * N_EXPERTS * COMBINE_WINDOW * d * 2 + 10 * tm * d * 4 + 2 * tm * N_EXPERTS * COMBINE_WINDOW * 4
    return pl.pallas_call(
        functools.partial(_combine_kernel, cap=cap, blocks_per_tile=tm // ROUTE_BLOCK),
        out_shape=jax.ShapeDtypeStruct((n, d), F32),
        grid_spec=grid_spec,
        compiler_params=_cparams(("arbitrary",), vmem),
        name="combine_" + tag,
    )(tbl, ye, pos, aff, x1, mod3, g_po)


def _rope_tables(seq):
    rows = seq // GRID_W
    r = jnp.repeat(jnp.arange(rows), GRID_W).astype(F32)
    col = jnp.tile(jnp.arange(GRID_W), rows).astype(F32)
    pairs = HEAD_DIM // 4
    freqs = ROPE_THETA ** (-jnp.arange(pairs, dtype=F32) / pairs)
    ang = jnp.concatenate([r[:, None] * freqs, col[:, None] * freqs], axis=-1)
    cos = jnp.repeat(jnp.cos(ang), 2, axis=-1)
    sin = jnp.repeat(jnp.sin(ang), 2, axis=-1)
    even = (jnp.arange(HEAD_DIM) % 2) == 0
    return cos, jnp.where(even, -sin, 0.0), jnp.where(even, 0.0, sin)


def _trunk_to_routing(x, mod3, mod_row_of_tile, rope_tabs, ctx, lw, tag):
    (g_pre_mix, g_post_mix, g_pre_ffn, g_post_ffn, w_in_p, g_q, g_k, wgf, bgf, wgb, bgb, g_gla,
     w_pa, w_pg, w_out, w_r, w1, w3, w2) = lw
    batch, seq, d = x.shape
    n = batch * seq
    x2 = x.reshape(n, d)
    (q_a, k_a, v_a, q_g, k_g, v_g, r_g, lg_f, lg_b, gates) = _inproj_call(
        x2, mod3, mod_row_of_tile, g_pre_mix, w_in_p, g_q, g_k, wgf, bgf, wgb, bgb, rope_tabs, batch, seq)
    if ctx is None:
        o_a = _attn_call(q_a, k_a, v_a, None, None)
        o_g, s_f, s_b = _gla_call(q_g, k_g, v_g, lg_f, lg_b, r_g, g_gla, None, None, batch, seq, True,
                                  GLA_CTX_HEADS_PER_STEP, GLA_CTX_SEQS_PER_STEP)
    else:
        ck, cv, s_f0, s_b0 = ctx
        o_a = _attn_call(q_a, k_a, v_a, ck, cv)
        (o_g,) = _gla_call(q_g, k_g, v_g, lg_f, lg_b, r_g, g_gla, s_f0, s_b0, batch, seq, False,
                           GLA_LAT_HEADS_PER_STEP, 1)
        s_f = s_b = None
    x1, h, aff = _outproj_call(o_a, o_g, gates, x2, mod3, mod_row_of_tile, w_pa, w_pg, w_out,
                               g_post_mix, g_pre_ffn, w_r, tag)
    cap = (EC_CAPACITY_FACTOR * n) // N_EXPERTS
    pos, post, tbl = _route_call(aff, cap, tag)
    xs = _sc_gather_call(h, post.transpose(1, 0, 2).reshape(N_EXPERTS, n), cap)
    return dict(x1=x1, xs=xs, aff=aff, pos=pos, tbl=tbl, cap=cap,
                mod_row_of_tile=mod_row_of_tile, tag=tag, shape=(batch, seq, d)), (k_a, v_a, s_f, s_b)


def _expert_ffn(groups, mod3, g_post_ffn, w1, w3, w2):
    yes = _ffn_call([(g["xs"], g["cap"]) for g in groups], w1, w3, w2)
    outs = []
    for g, ye in zip(groups, yes):
        y = _combine_call(g["tbl"], ye, g["pos"], g["aff"], g["x1"], mod3, g["mod_row_of_tile"], g_post_ffn,
                          g["cap"], g["tag"])
        outs.append(y.reshape(g["shape"]))
    return outs


def kernel(x_prompt, x_sample, cache_k, cache_v, state_gla_fwd, state_gla_bwd, c, c_ctx, g_pre_mix, g_post_mix, g_pre_ffn, g_post_ffn, w_mod, b_mod, w_in, g_q, g_k, w_gk2_f, b_gk_f, w_gk2_b, b_gk_b, g_gla, w_pa, w_pg, w_out, w_router, w1, w3, w2):
    depth = w_in.shape[0]
    assert depth == 1, "single trunk layer"
    d = x_prompt.shape[-1]
    dec_batch, dec_seq, _ = x_sample.shape
    assert dec_batch + 1 <= MOD_ROWS
    l = 0
    rank = GLA_GATE_RANK
    w_in_p = jnp.swapaxes(w_in[l], 0, 1).astype(BF16)
    gkw = w_gk2_f.shape[-1]
    wgf = jnp.zeros((V7X_LANES, gkw), F32).at[0:rank].set(w_gk2_f[l]).astype(BF16)
    wgb = jnp.zeros((V7X_LANES, gkw), F32).at[rank:2 * rank].set(w_gk2_b[l]).astype(BF16)
    w_r = jnp.zeros((d, V7X_LANES), F32).at[:, :N_EXPERTS].set(w_router[l]).astype(BF16)
    row = lambda a: a[l].reshape(1, -1)
    lw = (row(g_pre_mix), row(g_post_mix), row(g_pre_ffn), row(g_post_ffn), w_in_p, row(g_q), row(g_k),
          wgf, row(b_gk_f), wgb, row(b_gk_b), row(g_gla),
          w_pa[l].astype(BF16), w_pg[l].astype(BF16), w_out[l].astype(BF16), w_r, w1[l], w3[l], w2[l])

    cc = jnp.concatenate([c_ctx[None, :], c, jnp.zeros((MOD_ROWS - 1 - dec_batch, d), F32)], axis=0)
    mod = _mod_call(cc, w_mod[l], b_mod[l].reshape(1, -1))
    mod3 = mod.reshape(MOD_ROWS, 1, 6 * d)

    gp, (nk, nv, nsf, nsb) = _trunk_to_routing(x_prompt, mod3, lambda i, tm: 0, None, None, lw, "ctx")
    ctx = (cache_k, cache_v, state_gla_fwd, state_gla_bwd)
    gs, _ = _trunk_to_routing(x_sample, mod3, lambda i, tm: 1 + (i * tm) // dec_seq, _rope_tables(dec_seq),
                              ctx, lw, "lat")
    yp, ys = _expert_ffn([gp, gs], mod3, lw[3], lw[16], lw[17], lw[18])
    return (yp, ys, nk, nv, nsf, nsb)
```

```python
import functools

import jax
import jax.numpy as jnp
from jax import lax
from jax.experimental import pallas as pl
from jax.experimental.pallas import tpu as pltpu
from jax.experimental.pallas import tpu_sc as plsc

F32 = jnp.float32
BF16 = jnp.bfloat16
I32 = jnp.int32

N_HEADS = 8
N_KV_HEADS = 2
HEAD_DIM = 128
GRID_W = 64
ROPE_THETA = 10000.0
GLA_HEADS = 4
GLA_GATE_RANK = 16
GLA_GATE_NORM = 16.0
GLA_CHUNK = 64
N_EXPERTS = 16
EC_CAPACITY_FACTOR = 2
EPS = 1e-6

V7X_LANES = 128
V7X_VMEM_BYTES = 64 * 1024 * 1024
V7X_VMEM_RESERVE_BYTES = 6 * 1024 * 1024
BF16_ROWS_PER_TILE = 16
V7X_SC_CORES = 2
V7X_SC_SUBCORES = 16
V7X_SC_LANES = 16

TOKEN_TILE = 512
OUTPROJ_TILE = 512
OUTPROJ_ROW_GROUPS = 4
ATTN_Q_TILE = 256
ATTN_SEQS_PER_STEP = 4
GLA_BLOCK = 256
GLA_HEADS_PER_STAGE_GROUP = 2
GLA_CTX_HEADS_PER_STEP = 4
GLA_CTX_SEQS_PER_STEP = 4
GLA_LAT_HEADS_PER_STEP = 2
ROUTE_BLOCK = 256
SC_GATHER_CHUNK = 128
COMBINE_TILE = 512
COMBINE_WINDOW = 128
COMBINE_EXPERT_GROUP = 2
MOD_ROWS = 8
MOD_N_TILE = 1536


def _cparams(semantics):
    return pltpu.CompilerParams(dimension_semantics=semantics,
                                vmem_limit_bytes=V7X_VMEM_BYTES - V7X_VMEM_RESERVE_BYTES)


def _sigmoid(x):
    return 0.5 * (jnp.tanh(0.5 * x) + 1.0)


def _silu(x):
    return x * _sigmoid(x)


def _log_sigmoid(x):
    return jnp.minimum(x, 0.0) - jnp.log1p(jnp.exp(-jnp.abs(x)))


def _rms(x, g):
    ms = jnp.mean(x * x, axis=-1, keepdims=True)
    return x * lax.rsqrt(ms + EPS) * g


def _dot(a, b):
    return jnp.dot(a, b, preferred_element_type=F32)


def _dot_nt(a, b):
    return lax.dot_general(a, b, (((1,), (1,)), ((), ())), preferred_element_type=F32)


def _mod_kernel(c_ref, w_ref, b_ref, o_ref):
    s = _silu(c_ref[...]).astype(BF16)
    o_ref[...] = _dot(s, w_ref[...].astype(BF16)) + b_ref[...]


def _mod_call(cc, w_mod, b_mod):
    d, n6 = w_mod.shape
    tn = MOD_N_TILE
    return pl.pallas_call(
        _mod_kernel,
        out_shape=jax.ShapeDtypeStruct((MOD_ROWS, n6), F32),
        grid=(n6 // tn,),
        in_specs=[pl.BlockSpec((MOD_ROWS, d), lambda j: (0, 0)),
                  pl.BlockSpec((d, tn), lambda j: (0, j)),
                  pl.BlockSpec((1, tn), lambda j: (0, j))],
        out_specs=pl.BlockSpec((MOD_ROWS, tn), lambda j: (0, j)),
        compiler_params=_cparams(("arbitrary",)),
        name="mod",
    )(cc, w_mod, b_mod)


def _inproj_layout(d):
    aq, akv = N_HEADS * HEAD_DIM, N_KV_HEADS * HEAD_DIM
    gk, gv = d // 2, d
    names = ("q_a", "k_a", "v_a", "q_g", "k_g", "v_g", "r_g", "gk_f", "gk_b", "gates")
    widths = (aq, akv, akv, gk, gk, gv, gv, GLA_GATE_RANK, GLA_GATE_RANK, 2 * d)
    off, o = {}, 0
    for nme, w in zip(names, widths):
        off[nme] = (o, o + w)
        o += w
    off["gk"] = (off["gk_f"][0], off["gk_f"][0] + V7X_LANES)
    return off, o


def _inproj_kernel(*refs, rope, d):
    if rope:
        (x_ref, mod_ref, gpre_ref, w_ref, gq_ref, gk_ref, wgf_ref, bgf_ref, wgb_ref, bgb_ref,
         cos_ref, se_ref, so_ref, *outs) = refs
    else:
        (x_ref, mod_ref, gpre_ref, w_ref, gq_ref, gk_ref, wgf_ref, bgf_ref, wgb_ref, bgb_ref,
         *outs) = refs
    qa_ref, k_ref, v_ref, qg_ref, kg_ref, vg_ref, rg_ref, lgf_ref, lgb_ref, gate_ref = outs
    off, _ = _inproj_layout(d)
    m = mod_ref[0]
    h = _rms(x_ref[...], gpre_ref[...]) * (1.0 + m[:, d:2 * d]) + m[:, 0:d]
    hb = h.astype(BF16)

    def proj(name):
        a, b = off[name]
        return _dot_nt(hb, w_ref[a:b, :])

    def qk_norm(y, g_ref):
        y = _rms(y, g_ref[...])
        if rope:
            nxt = pltpu.roll(y, HEAD_DIM - 1, axis=1)
            prv = pltpu.roll(y, 1, axis=1)
            y = y * cos_ref[...] + nxt * se_ref[...] + prv * so_ref[...]
        return y

    gk = proj("gk").astype(BF16)

    q = proj("q_a")
    scale = HEAD_DIM ** -0.5
    for hd in range(N_HEADS):
        sl = slice(hd * HEAD_DIM, (hd + 1) * HEAD_DIM)
        qa_ref[:, sl] = (qk_norm(q[:, sl], gq_ref) * scale).astype(BF16)

    k = proj("k_a")
    v = proj("v_a")
    tb, _, _, ts, _ = k_ref.shape
    for kv in range(N_KV_HEADS):
        sl = slice(kv * HEAD_DIM, (kv + 1) * HEAD_DIM)
        k_ref[:, 0, kv] = qk_norm(k[:, sl], gk_ref).reshape(tb, ts, HEAD_DIM)
        v_ref[:, 0, kv] = v[:, sl].reshape(tb, ts, HEAD_DIM)

    dk = (d // 2) // GLA_HEADS
    qg_ref[...] = (proj("q_g") * (dk ** -0.5)).astype(BF16)
    kg_ref[...] = proj("k_g").astype(BF16)
    vg_ref[...] = proj("v_g").astype(BF16)
    rg_ref[...] = _silu(proj("r_g")).astype(BF16)

    gate_ref[...] = _sigmoid(proj("gates")).astype(BF16)

    lgf_ref[...] = _log_sigmoid(_dot(gk, wgf_ref[...]) + bgf_ref[...]) * (1.0 / GLA_GATE_NORM)
    lgb_ref[...] = _log_sigmoid(_dot(gk, wgb_ref[...]) + bgb_ref[...]) * (1.0 / GLA_GATE_NORM)


def _inproj_call(x2, mod3, mod_row_of_tile, g_pre, w_in_p, g_q, g_k, wgf, bgf, wgb, bgb, rope_tabs,
                 batch, seq):
    n, d = x2.shape
    tm = TOKEN_TILE
    _, dinp = _inproj_layout(d)
    rope = rope_tabs is not None
    gk_w = d // 2
    if seq >= tm:
        tb, ts, per = 1, tm, seq // tm
        kv_map = lambda i: (i // per, 0, 0, i % per, 0)
    else:
        tb, ts, per = tm // seq, seq, 1
        kv_map = lambda i: (i, 0, 0, 0, 0)
    row = lambda i: (i, 0)
    const = lambda i: (0, 0)
    in_specs = [
        pl.BlockSpec((tm, d), row),
        pl.BlockSpec((1, 1, 6 * d), lambda i: (mod_row_of_tile(i, tm), 0, 0)),
        pl.BlockSpec((1, d), const),
        pl.BlockSpec((dinp, d), const, pipeline_mode=pl.Buffered(1)),
        pl.BlockSpec((1, HEAD_DIM), const),
        pl.BlockSpec((1, HEAD_DIM), const),
        pl.BlockSpec((V7X_LANES, gk_w), const),
        pl.BlockSpec((1, gk_w), const),
        pl.BlockSpec((V7X_LANES, gk_w), const),
        pl.BlockSpec((1, gk_w), const),
    ]
    args = [x2, mod3, g_pre, w_in_p, g_q, g_k, wgf, bgf, wgb, bgb]
    if rope:
        tab = pl.BlockSpec((tm, HEAD_DIM), lambda i: (i % per, 0))
        in_specs += [tab, tab, tab]
        args += list(rope_tabs)
    kv_shape = jax.ShapeDtypeStruct((batch, 1, N_KV_HEADS, seq, HEAD_DIM), F32)
    kv_spec = pl.BlockSpec((tb, 1, N_KV_HEADS, ts, HEAD_DIM), kv_map)
    out_shape = (
        jax.ShapeDtypeStruct((n, N_HEADS * HEAD_DIM), BF16), kv_shape, kv_shape,
        jax.ShapeDtypeStruct((n, gk_w), BF16), jax.ShapeDtypeStruct((n, gk_w), BF16),
        jax.ShapeDtypeStruct((n, d), BF16), jax.ShapeDtypeStruct((n, d), BF16),
        jax.ShapeDtypeStruct((n, gk_w), F32), jax.ShapeDtypeStruct((n, gk_w), F32),
        jax.ShapeDtypeStruct((n, 2 * d), BF16),
    )
    out_specs = (
        pl.BlockSpec((tm, N_HEADS * HEAD_DIM), row), kv_spec, kv_spec,
        pl.BlockSpec((tm, gk_w), row), pl.BlockSpec((tm, gk_w), row),
        pl.BlockSpec((tm, d), row), pl.BlockSpec((tm, d), row),
        pl.BlockSpec((tm, gk_w), row), pl.BlockSpec((tm, gk_w), row),
        pl.BlockSpec((tm, 2 * d), row),
    )
    return pl.pallas_call(
        functools.partial(_inproj_kernel, rope=rope, d=d),
        out_shape=out_shape,
        grid=(n // tm,),
        in_specs=in_specs,
        out_specs=out_specs,
        compiler_params=_cparams(("parallel",)),
        name="inproj_lat" if rope else "inproj_ctx",
    )(*args)


def _attn_kernel(*refs, cached):
    if cached:
        q_ref, k_ref, v_ref, ck_ref, cv_ref, o_ref = refs
    else:
        q_ref, k_ref, v_ref, o_ref = refs
    seqs = k_ref.shape[0]
    tq = q_ref.shape[0] // seqs
    grp = N_HEADS // N_KV_HEADS
    chains = [(sq, kv) for sq in range(seqs) for kv in range(N_KV_HEADS)]

    def scores_of(sq, kv):
        rows = pl.ds(sq * tq, tq)
        kk = k_ref[sq, 0, kv].astype(BF16)
        vv = v_ref[sq, 0, kv].astype(BF16)
        if cached:
            kk = jnp.concatenate([ck_ref[sq, 0, kv].astype(BF16), kk], axis=0)
            vv = jnp.concatenate([cv_ref[sq, 0, kv].astype(BF16), vv], axis=0)
        heads = [q_ref[rows, (kv * grp + g) * HEAD_DIM:(kv * grp + g + 1) * HEAD_DIM] for g in range(grp)]
        return _dot_nt(jnp.concatenate(heads, axis=0), kk), vv

    look_ahead = cached
    ahead = scores_of(*chains[0]) if look_ahead else None
    for n, (sq, kv) in enumerate(chains):
        s, vv = ahead if look_ahead else scores_of(sq, kv)
        if look_ahead and n + 1 < len(chains):
            ahead = scores_of(*chains[n + 1])
        rows = pl.ds(sq * tq, tq)
        p = jnp.exp(s - jnp.max(s, axis=-1, keepdims=True))
        l = jnp.sum(p, axis=-1, keepdims=True)
        o = _dot(p.astype(BF16), vv) / l
        for g in range(grp):
            hd = kv * grp + g
            o_ref[rows, hd * HEAD_DIM:(hd + 1) * HEAD_DIM] = o[g * tq:(g + 1) * tq].astype(BF16)


def _attn_call(q_a, k_a, v_a, cache_k, cache_v):
    batch, _, _, seq, _ = k_a.shape
    n, aq = q_a.shape
    tq = min(ATTN_Q_TILE, seq)
    per = seq // tq
    cached = cache_k is not None
    seqs = ATTN_SEQS_PER_STEP if (per == 1 and not cached) else 1
    batch = batch // seqs
    own = pl.BlockSpec((seqs, 1, N_KV_HEADS, seq, HEAD_DIM), lambda b, j: (b, 0, 0, 0, 0))
    tq = tq * seqs
    in_specs = [pl.BlockSpec((tq, aq), lambda b, j: (b * per + j, 0)), own, own]
    args = [q_a, k_a, v_a]
    if cached:
        past = cache_k.shape[3]
        cspec = pl.BlockSpec((1, 1, N_KV_HEADS, past, HEAD_DIM), lambda b, j: (b, 0, 0, 0, 0))
        in_specs += [cspec, cspec]
        args += [cache_k, cache_v]
    return pl.pallas_call(
        functools.partial(_attn_kernel, cached=cached),
        out_shape=jax.ShapeDtypeStruct((n, aq), BF16),
        grid=(batch, per),
        in_specs=in_specs,
        out_specs=pl.BlockSpec((tq, aq), lambda b, j: (b * per + j, 0)),
        compiler_params=_cparams(("parallel", "parallel")),
        name="attn_lat" if cached else "attn_ctx",
    )(*args)


def _gla_kernel(*refs, nblk, heads, seqs, has_state, emit_state):
    refs = list(refs)
    q_ref, k_ref, v_ref, lgf_ref, lgb_ref, rg_ref, gg_ref = refs[:7]
    pos = 7
    if has_state:
        s0f_ref, s0b_ref = refs[pos:pos + 2]
        pos += 2
    og_ref = refs[pos]
    pos += 1
    if emit_state:
        sf_ref, sb_ref = refs[pos:pos + 2]
        pos += 2
    of_scr, ob_scr = refs[pos:pos + 2]

    blk = GLA_BLOCK
    ch = GLA_CHUNK
    nch = blk // ch
    dk = q_ref.shape[1] // heads
    dv = v_ref.shape[1] // heads
    shift = ch.bit_length() - 1
    row_in_chunk = lax.broadcasted_iota(I32, (blk, dk), 0) & (ch - 1)
    ri = lax.broadcasted_iota(I32, (blk, blk), 0)
    ci = lax.broadcasted_iota(I32, (blk, blk), 1)
    same = (ri >> shift) == (ci >> shift)
    mask_f = same & (ci <= ri)
    mask_b = same & (ci >= ri)

    def one_block(b0, hd, reverse):
        rows = pl.ds(b0, blk)
        kcols = slice(hd * dk, (hd + 1) * dk)
        q = q_ref[rows, kcols].astype(F32)
        k = k_ref[rows, kcols].astype(F32)
        v = v_ref[rows, hd * dv:(hd + 1) * dv]
        b = (lgb_ref if reverse else lgf_ref)[rows, kcols]
        mask = mask_b if reverse else mask_f
        s = 1
        while s < ch:
            if reverse:
                sh = pltpu.roll(b, blk - s, axis=0)
                b = b + jnp.where(row_in_chunk < ch - s, sh, 0.0)
            else:
                sh = pltpu.roll(b, s, axis=0)
                b = b + jnp.where(row_in_chunk >= s, sh, 0.0)
            s *= 2
        qe = (q * jnp.exp(b)).astype(BF16)
        ke = (k * jnp.exp(-b)).astype(BF16)
        scores = _dot_nt(qe, ke)
        end_row = [c * ch + (0 if reverse else ch - 1) for c in range(nch)]
        ends = [b[r:r + 1, :] for r in end_row]
        b_end = jnp.concatenate([jnp.broadcast_to(e, (ch, dk)) for e in ends], axis=0)
        kd = (k * jnp.exp(b_end - b)).astype(BF16)
        return dict(scores=scores, mask=mask, qe=qe, kd=kd, v=v, ends=ends, reverse=reverse)

    def chunk_products(w):
        return [lax.dot_general(w["kd"][c * ch:(c + 1) * ch], w["v"][c * ch:(c + 1) * ch],
                                (((0,), (0,)), ((), ())), preferred_element_type=F32) for c in range(nch)]

    def intra_chunk(w):
        return _dot(jnp.where(w["mask"], w["scores"], 0.0).astype(BF16), w["v"])

    def across_chunks(w, kv, state):
        decay = jnp.exp(jnp.concatenate(w["ends"] + [jnp.zeros((dk - nch, dk), F32)], axis=0)).T
        inter = [None] * nch
        for c in (range(nch - 1, -1, -1) if w["reverse"] else range(nch)):
            if state is None:
                inter[c] = jnp.zeros((ch, dv), F32)
                state = kv[c]
            else:
                inter[c] = _dot(w["qe"][c * ch:(c + 1) * ch], state.astype(BF16))
                state = decay[:, c:c + 1] * state + kv[c]
        return jnp.concatenate(inter, axis=0), state

    group = min(GLA_HEADS_PER_STAGE_GROUP, heads)
    for sq in range(seqs):
        srows = pl.ds(sq * nblk * blk, nblk * blk)
        for h0 in range(0, heads, group):
            hds = list(range(h0, h0 + group))
            states = {(hd, rev): ((s0b_ref if rev else s0f_ref)[sq, 0, hd] if has_state else None)
                      for hd in hds for rev in (False, True)}
            for i in range(nblk):
                block_of = {False: sq * nblk + i, True: sq * nblk + nblk - 1 - i}
                chains = [(hd, rev) for hd in hds for rev in (False, True)]
                work = {c: one_block(block_of[c[1]] * blk, c[0], c[1]) for c in chains}
                kvs = {c: chunk_products(work[c]) for c in chains}
                intra = {c: intra_chunk(work[c]) for c in chains}
                for c in chains:
                    hd, rev = c
                    inter, states[c] = across_chunks(work[c], kvs[c], states[c])
                    scr = ob_scr if rev else of_scr
                    scr[pl.ds(block_of[rev] * blk, blk), hd * dv:(hd + 1) * dv] = intra[c] + inter
            for hd in hds:
                vcols = slice(hd * dv, (hd + 1) * dv)
                if emit_state:
                    sf_ref[sq, 0, hd] = states[(hd, False)]
                    sb_ref[sq, 0, hd] = states[(hd, True)]
                o = of_scr[srows, vcols] + ob_scr[srows, vcols]
                og_ref[srows, vcols] = (_rms(o, gg_ref[...]) * rg_ref[srows, vcols].astype(F32)).astype(BF16)


def _gla_call(q_g, k_g, v_g, lg_f, lg_b, r_g, g_gla, state_f, state_b, batch, seq, emit_state, heads, seqs):
    n, gkw = q_g.shape
    d = v_g.shape[1]
    dk, dv = gkw // GLA_HEADS, d // GLA_HEADS
    has_state = state_f is not None
    nblk = seq // GLA_BLOCK
    batch = batch // seqs
    seq = seq * seqs
    kspec = pl.BlockSpec((seq, heads * dk), lambda b, h: (b, h))
    vspec = pl.BlockSpec((seq, heads * dv), lambda b, h: (b, h))
    sspec = pl.BlockSpec((seqs, 1, heads, dk, dv), lambda b, h: (b, 0, h, 0, 0))
    in_specs = [kspec, kspec, vspec, kspec, kspec, vspec, pl.BlockSpec((1, dv), lambda b, h: (0, 0))]
    args = [q_g, k_g, v_g, lg_f, lg_b, r_g, g_gla]
    if has_state:
        in_specs += [sspec, sspec]
        args += [state_f, state_b]
    out_shape = [jax.ShapeDtypeStruct((n, d), BF16)]
    out_specs = [vspec]
    if emit_state:
        st = jax.ShapeDtypeStruct((batch * seqs, 1, GLA_HEADS, dk, dv), F32)
        out_shape += [st, st]
        out_specs += [sspec, sspec]
    return pl.pallas_call(
        functools.partial(_gla_kernel, nblk=nblk, heads=heads, seqs=seqs, has_state=has_state,
                          emit_state=emit_state),
        out_shape=tuple(out_shape),
        grid=(batch, GLA_HEADS // heads),
        in_specs=in_specs,
        out_specs=tuple(out_specs),
        scratch_shapes=[pltpu.VMEM((seq, heads * dv), F32), pltpu.VMEM((seq, heads * dv), F32)],
        compiler_params=_cparams(("parallel", "parallel")),
        name="gla_ctx" if emit_state else "gla_lat",
    )(*args)


def _outproj_kernel(oa_ref, og_ref, gate_ref, x_ref, mod_ref, wpa_ref, wpg_ref, wout_ref, gpm_ref,
                    gpf_ref, wr_ref, x1_ref, h_ref, aff_ref):
    d = x_ref.shape[1]
    m = mod_ref[0]
    sub = x_ref.shape[0] // OUTPROJ_ROW_GROUPS
    groups = [pl.ds(g * sub, sub) for g in range(OUTPROJ_ROW_GROUPS)]
    branch = [(_dot(oa_ref[r, :], wpa_ref[...]), _dot(og_ref[r, :], wpg_ref[...])) for r in groups]
    mo = jnp.concatenate(
        [_dot((gate_ref[r, 0:d].astype(F32) * oa + gate_ref[r, d:2 * d].astype(F32) * og).astype(BF16),
              wout_ref[...]) for r, (oa, og) in zip(groups, branch)], axis=0)
    x1 = x_ref[...] + m[:, 2 * d:3 * d] * _rms(mo, gpm_ref[...])
    x1_ref[...] = x1
    hb = (_rms(x1, gpf_ref[...]) * (1.0 + m[:, 4 * d:5 * d]) + m[:, 3 * d:4 * d]).astype(BF16)
    bits = lax.bitcast_convert_type(hb.astype(F32), jnp.uint32)
    packed = (bits[:, 0:d // 2] >> 16) | (bits[:, d // 2:d] & jnp.uint32(0xFFFF0000))
    h_ref[...] = lax.bitcast_convert_type(packed, I32)
    logits = _dot(hb, wr_ref[...])
    valid = lax.broadcasted_iota(I32, logits.shape, 1) < N_EXPERTS
    mx = jnp.max(jnp.where(valid, logits, -jnp.inf), axis=-1, keepdims=True)
    ex = jnp.where(valid, jnp.exp(logits - mx), 0.0)
    aff_ref[...] = ex / jnp.sum(ex, axis=-1, keepdims=True)


def _outproj_call(o_a, o_g, gates, x2, mod3, mod_row_of_tile, w_pa, w_pg, w_out, g_pm, g_pf, w_r, tag):
    n, d = x2.shape
    tm = OUTPROJ_TILE
    row = lambda i: (i, 0)
    const = lambda i: (0, 0)
    wspec = pl.BlockSpec((d, d), const, pipeline_mode=pl.Buffered(1))
    return pl.pallas_call(
        _outproj_kernel,
        out_shape=(jax.ShapeDtypeStruct((n, d), F32), jax.ShapeDtypeStruct((n, d // 2), I32),
                   jax.ShapeDtypeStruct((n, V7X_LANES), F32)),
        grid=(n // tm,),
        in_specs=[pl.BlockSpec((tm, d), row), pl.BlockSpec((tm, d), row), pl.BlockSpec((tm, 2 * d), row),
                  pl.BlockSpec((tm, d), row),
                  pl.BlockSpec((1, 1, 6 * d), lambda i: (mod_row_of_tile(i, tm), 0, 0)),
                  wspec, wspec, wspec, pl.BlockSpec((1, d), const), pl.BlockSpec((1, d), const),
                  pl.BlockSpec((d, V7X_LANES), const)],
        out_specs=(pl.BlockSpec((tm, d), row), pl.BlockSpec((tm, d // 2), row),
                   pl.BlockSpec((tm, V7X_LANES), row)),
        compiler_params=_cparams(("parallel",)),
        name="outproj_" + tag,
    )(o_a, o_g, gates, x2, mod3, w_pa, w_pg, w_out, g_pm, g_pf, w_r)


def _route_kernel(aff_ref, pos_ref, post_ref, tbl_ref, afft_scr, *, cap):
    n = aff_ref.shape[0]
    rb = ROUTE_BLOCK
    nb = n // rb
    lanes = aff_ref.shape[1]
    lane = lax.broadcasted_iota(I32, (1, lanes), 1)
    expert_lane = lane < N_EXPERTS
    tbl_ref[...] = jnp.zeros(tbl_ref.shape, I32)

    def to_token_lanes(c, carry):
        start = pl.multiple_of(c * rb, rb)
        afft_scr[c] = aff_ref[pl.ds(start, rb), :].T[0:N_EXPERTS, :]
        return carry

    lax.fori_loop(0, nb, to_token_lanes, 0)
    aff_t = afft_scr[...]

    def count(hit):
        return jnp.sum(jnp.sum(hit.astype(I32), axis=0), axis=1, keepdims=True)

    def bit_step(i, lo):
        t = lo | jnp.left_shift(jnp.int32(1), 30 - i)
        ge = aff_t >= lax.bitcast_convert_type(t, F32)[None]
        return jnp.where(count(ge) >= cap, t, lo)

    thr_bits = lax.fori_loop(0, 31, bit_step, jnp.zeros((N_EXPERTS, 1), I32))
    need_t = cap - count(aff_t > lax.bitcast_convert_type(thr_bits, F32)[None])

    def to_expert_lanes(col):
        full = jnp.concatenate([jnp.broadcast_to(col, (N_EXPERTS, lanes)),
                                jnp.zeros((lanes - N_EXPERTS, lanes), I32)], axis=0)
        return full.T[0:1, :]

    thr = lax.bitcast_convert_type(to_expert_lanes(thr_bits), F32)
    need = to_expert_lanes(need_t).astype(F32)
    capf = float(cap)

    r = lax.broadcasted_iota(I32, (rb, rb), 0)
    c_ = lax.broadcasted_iota(I32, (rb, rb), 1)
    tril = jnp.where(c_ <= r, 1.0, 0.0).astype(BF16)

    def blk_step(c, carry):
        eq_before, raw_before = carry
        start = pl.multiple_of(c * rb, rb)
        a = aff_ref[pl.ds(start, rb), :]
        gt = a > thr
        eq = a == thr
        eq_incl = _dot(tril, jnp.where(eq, 1.0, 0.0).astype(BF16)) + eq_before
        raw = (gt | (eq & (eq_incl <= need))) & expert_lane
        raw_incl = _dot(tril, jnp.where(raw, 1.0, 0.0).astype(BF16)) + raw_before
        sel = raw & (raw_incl <= capf)
        self_ = jnp.where(sel, 1.0, 0.0)
        incl = jnp.minimum(raw_incl, capf)
        sel_before = jnp.minimum(raw_before, capf)
        excl = incl - self_
        posb = jnp.where(sel, excl, -1.0).astype(I32)
        pos_ref[pl.ds(start, rb), :] = posb
        post_ref[:, pl.ds(start, rb)] = posb.T[0:N_EXPERTS, :]
        tbl_ref[pl.ds(c, 1), :] = sel_before.astype(I32)
        return (eq_incl[rb - 1:rb, :], raw_incl[rb - 1:rb, :])

    zero = jnp.zeros((1, lanes), F32)
    _, total = lax.fori_loop(0, nb, blk_step, (zero, zero), unroll=2)
    tbl_ref[pl.ds(nb, 1), :] = jnp.minimum(total, capf).astype(I32)


def _route_call(aff, cap, tag):
    n, lanes = aff.shape
    nb = n // ROUTE_BLOCK
    tbl_rows = -(-(nb + 1) // 8) * 8
    full = lambda *shape: pl.BlockSpec(shape, lambda: tuple(0 for _ in shape))
    return pl.pallas_call(
        functools.partial(_route_kernel, cap=cap),
        out_shape=(jax.ShapeDtypeStruct((n, lanes), I32),
                   jax.ShapeDtypeStruct((N_EXPERTS, n), I32),
                   jax.ShapeDtypeStruct((tbl_rows, lanes), I32)),
        in_specs=[full(n, lanes)],
        out_specs=(full(n, lanes), full(N_EXPERTS, n), full(tbl_rows, lanes)),
        scratch_shapes=[pltpu.VMEM((nb, N_EXPERTS, ROUTE_BLOCK), F32)],
        compiler_params=_cparams(()),
        name="route_" + tag,
    )(aff)


def _sc_gather_call(table, post, cap):
    n_exp, n = post.shape
    words = table.shape[1]
    workers = V7X_SC_CORES * V7X_SC_SUBCORES
    parts = workers // n_exp
    chunk = SC_GATHER_CHUNK
    lanes = V7X_SC_LANES
    per_part = cap // parts
    assert parts * n_exp == workers and per_part % chunk == 0 and n % lanes == 0
    mesh = plsc.VectorSubcoreMesh(core_axis_name="c", subcore_axis_name="s",
                                  num_cores=V7X_SC_CORES, num_subcores=V7X_SC_SUBCORES)

    def body(table_hbm, post_hbm, out_hbm, pos_v, idx_v, rows_v, sem):
        wid = lax.axis_index("s") * V7X_SC_CORES + lax.axis_index("c")
        e = wid // parts
        part = wid % parts
        pltpu.sync_copy(post_hbm.at[e], pos_v)
        lane = lax.iota(I32, lanes)

        @pl.loop(0, n, step=lanes)
        def _(t0):
            p = pos_v[pl.ds(t0, lanes)]
            plsc.store_scatter(idx_v, [p], lane + t0, mask=p >= 0)

        @pl.loop(0, per_part // chunk)
        def _(j):
            off = pl.multiple_of(part * per_part + j * chunk, chunk)
            pltpu.async_copy(table_hbm.at[idx_v.at[pl.ds(off, chunk)]], rows_v, sem).wait()
            pltpu.sync_copy(rows_v, out_hbm.at[pl.ds(e * cap + off, chunk)])

    return pl.kernel(
        body,
        out_type=jax.ShapeDtypeStruct((n_exp * cap, words), table.dtype),
        mesh=mesh,
        scratch_types=[pltpu.VMEM((n,), I32), pltpu.VMEM((cap,), I32),
                       pltpu.VMEM((chunk, words), table.dtype), pltpu.SemaphoreType.DMA],
        compiler_params=pltpu.CompilerParams(needs_layout_passes=False),
        name="sc_gather",
    )(table, post)


def _ffn_kernel(*refs, caps):
    ng = len(caps)
    xs_refs = refs[:ng]
    w1_ref, w3_ref, w2_ref = refs[ng:ng + 3]
    ye_refs = refs[ng + 3:2 * ng + 3]
    row_off = [sum(caps[:g]) for g in range(ng)]

    def unpack(words):
        w = lax.bitcast_convert_type(words, jnp.uint32)
        lo = lax.bitcast_convert_type(w << 16, F32).astype(BF16)
        hi = lax.bitcast_convert_type(w & jnp.uint32(0xFFFF0000), F32).astype(BF16)
        return jnp.concatenate([lo, hi], axis=1)

    xs = jnp.concatenate([unpack(r[...]) for r in xs_refs], axis=0)
    hid = _silu(_dot(xs, w1_ref[...].astype(BF16))) * _dot(xs, w3_ref[...].astype(BF16))
    ye = _dot(hid.astype(BF16), w2_ref[...].astype(BF16)).astype(BF16)
    for g in range(ng):
        ye_refs[g][...] = ye[row_off[g]:row_off[g] + caps[g]]


def _ffn_call(groups, w1, w3, w2):
    caps = tuple(g[1] for g in groups)
    n_exp, d, dff = w1.shape
    in_specs = [pl.BlockSpec((cap, d // 2), lambda e: (e, 0)) for cap in caps]
    in_specs += [pl.BlockSpec((None, d, dff), lambda e: (e, 0, 0)),
                 pl.BlockSpec((None, d, dff), lambda e: (e, 0, 0)),
                 pl.BlockSpec((None, dff, d), lambda e: (e, 0, 0))]
    return pl.pallas_call(
        functools.partial(_ffn_kernel, caps=caps),
        out_shape=tuple(jax.ShapeDtypeStruct((n_exp * cap, d), BF16) for cap in caps),
        grid=(n_exp,),
        in_specs=in_specs,
        out_specs=tuple(pl.BlockSpec((cap, d), lambda e: (e, 0)) for cap in caps),
        compiler_params=_cparams(("arbitrary",)),
        name="ffn",
    )(*[g[0] for g in groups], w1, w3, w2)


def _combine_kernel(tbl_ref, ye_hbm, pos_ref, aff_ref, x1_ref, mod_ref, gpo_ref, y_ref, buf, sem, xbuf, xsem,
                    acc_scr, *, cap, blocks_per_tile):
    i = pl.program_id(0)
    nsteps = pl.num_programs(0)
    d = x1_ref.shape[1]
    lanes = pos_ref.shape[1]
    win = COMBINE_WINDOW
    last_start = ye_hbm.shape[0] - win
    slot = i % 2

    def first_row(step, e):
        return tbl_ref[step * blocks_per_tile, e] + e * cap

    def window_start(first, k):
        unclamped = (first // BF16_ROWS_PER_TILE) * BF16_ROWS_PER_TILE + k * win
        return unclamped, jnp.minimum(unclamped, last_start)

    def fetch(step, to_slot, e):
        start = window_start(first_row(step, e), 0)[1]
        return pltpu.make_async_copy(ye_hbm.at[pl.ds(pl.multiple_of(start, BF16_ROWS_PER_TILE), win), :],
                                     buf.at[to_slot, pl.ds(e * win, win), :], sem.at[to_slot, e])

    @pl.when(i == 0)
    def _prime():
        for e in range(N_EXPERTS):
            fetch(0, 0, e).start()

    @pl.when(i + 1 < nsteps)
    def _ahead():
        for e in range(N_EXPERTS):
            fetch(i + 1, 1 - slot, e).start()

    for e in range(N_EXPERTS):
        fetch(i, slot, e).wait()
    lane_row = lax.broadcasted_iota(I32, (1, win), 1)
    pieces = []
    for e in range(N_EXPERTS):
        pcol = pos_ref[:, e:e + 1]
        grow = jnp.where(pcol >= 0, pcol + e * cap, -1)
        start = window_start(first_row(i, e), 0)[1]
        pieces.append(jnp.where(grow == start + lane_row, aff_ref[:, e:e + 1], 0.0).astype(BF16))
    total = None
    for e0 in range(0, N_EXPERTS, COMBINE_EXPERT_GROUP):
        grp = jnp.concatenate(pieces[e0:e0 + COMBINE_EXPERT_GROUP], axis=1)
        part = _dot(grp, buf[slot, pl.ds(e0 * win, COMBINE_EXPERT_GROUP * win), :])
        total = part if total is None else total + part
    acc_scr[...] = total

    def extra_windows(e):
        covered = window_start(first_row(i, e), 1)[0]
        return jnp.maximum(first_row(i + 1, e) - covered + win - 1, 0) // win

    def expert_extra(e, carry):
        first = first_row(i, e)
        extra = extra_windows(e)

        def more(k, c):
            unclamped, start = window_start(first, k)
            cp = pltpu.make_async_copy(ye_hbm.at[pl.ds(pl.multiple_of(start, BF16_ROWS_PER_TILE), win), :],
                                       xbuf, xsem)
            cp.start()
            cp.wait()
            at_e = lax.broadcasted_iota(I32, (1, lanes), 1) == e
            pcol = jnp.sum(jnp.where(at_e, pos_ref[...].astype(F32), 0.0), axis=1, keepdims=True).astype(I32)
            wcol = jnp.sum(jnp.where(at_e, aff_ref[...], 0.0), axis=1, keepdims=True)
            grow = jnp.where(pcol >= 0, pcol + e * cap, -1)
            hit = (grow == start + lane_row) & (grow >= unclamped)
            acc_scr[...] += _dot(jnp.where(hit, wcol, 0.0).astype(BF16), xbuf[...])
            return c

        lax.fori_loop(1, 1 + extra, more, 0)
        return carry

    any_extra = extra_windows(0)
    for e in range(1, N_EXPERTS):
        any_extra = any_extra + extra_windows(e)

    @pl.when(any_extra > 0)
    def _overflow():
        lax.fori_loop(0, N_EXPERTS, expert_extra, 0)

    m = mod_ref[0]
    y_ref[...] = x1_ref[...] + m[:, 5 * d:6 * d] * _rms(acc_scr[...], gpo_ref[...])


def _combine_call(tbl, ye, pos, aff, x1, mod3, mod_row_of_tile, g_po, cap, tag):
    n, d = x1.shape
    tm = COMBINE_TILE
    lanes = pos.shape[1]
    grid_spec = pltpu.PrefetchScalarGridSpec(
        num_scalar_prefetch=1,
        grid=(n // tm,),
        in_specs=[pl.BlockSpec(memory_space=pl.ANY),
                  pl.BlockSpec((tm, lanes), lambda i, t: (i, 0)),
                  pl.BlockSpec((tm, lanes), lambda i, t: (i, 0)),
                  pl.BlockSpec((tm, d), lambda i, t: (i, 0)),
                  pl.BlockSpec((1, 1, 6 * d), lambda i, t: (mod_row_of_tile(i, tm), 0, 0)),
                  pl.BlockSpec((1, d), lambda i, t: (0, 0))],
        out_specs=pl.BlockSpec((tm, d), lambda i, t: (i, 0)),
        scratch_shapes=[pltpu.VMEM((2, N_EXPERTS * COMBINE_WINDOW, d), BF16),
                        pltpu.SemaphoreType.DMA((2, N_EXPERTS)),
                        pltpu.VMEM((COMBINE_WINDOW, d), BF16),
                        pltpu.SemaphoreType.DMA(()),
                        pltpu.VMEM((tm, d), F32)],
    )
    return pl.pallas_call(
        functools.partial(_combine_kernel, cap=cap, blocks_per_tile=tm // ROUTE_BLOCK),
        out_shape=jax.ShapeDtypeStruct((n, d), F32),
        grid_spec=grid_spec,
        compiler_params=_cparams(("arbitrary",)),
        name="combine_" + tag,
    )(tbl, ye, pos, aff, x1, mod3, g_po)


def _rope_tables(seq):
    rows = seq // GRID_W
    r = jnp.repeat(jnp.arange(rows), GRID_W).astype(F32)
    col = jnp.tile(jnp.arange(GRID_W), rows).astype(F32)
    pairs = HEAD_DIM // 4
    freqs = ROPE_THETA ** (-jnp.arange(pairs, dtype=F32) / pairs)
    ang = jnp.concatenate([r[:, None] * freqs, col[:, None] * freqs], axis=-1)
    cos = jnp.repeat(jnp.cos(ang), 2, axis=-1)
    sin = jnp.repeat(jnp.sin(ang), 2, axis=-1)
    even = (jnp.arange(HEAD_DIM) % 2) == 0
    return cos, jnp.where(even, -sin, 0.0), jnp.where(even, 0.0, sin)


def _trunk_to_routing(x, mod3, mod_row_of_tile, rope_tabs, ctx, lw, tag):
    (g_pre_mix, g_post_mix, g_pre_ffn, g_post_ffn, w_in_p, g_q, g_k, wgf, bgf, wgb, bgb, g_gla,
     w_pa, w_pg, w_out, w_r, w1, w3, w2) = lw
    batch, seq, d = x.shape
    n = batch * seq
    x2 = x.reshape(n, d)
    (q_a, k_a, v_a, q_g, k_g, v_g, r_g, lg_f, lg_b, gates) = _inproj_call(
        x2, mod3, mod_row_of_tile, g_pre_mix, w_in_p, g_q, g_k, wgf, bgf, wgb, bgb, rope_tabs, batch, seq)
    if ctx is None:
        o_a = _attn_call(q_a, k_a, v_a, None, None)
        o_g, s_f, s_b = _gla_call(q_g, k_g, v_g, lg_f, lg_b, r_g, g_gla, None, None, batch, seq, True,
                                  GLA_CTX_HEADS_PER_STEP, GLA_CTX_SEQS_PER_STEP)
    else:
        ck, cv, s_f0, s_b0 = ctx
        o_a = _attn_call(q_a, k_a, v_a, ck, cv)
        (o_g,) = _gla_call(q_g, k_g, v_g, lg_f, lg_b, r_g, g_gla, s_f0, s_b0, batch, seq, False,
                           GLA_LAT_HEADS_PER_STEP, 1)
        s_f = s_b = None
    x1, h, aff = _outproj_call(o_a, o_g, gates, x2, mod3, mod_row_of_tile, w_pa, w_pg, w_out,
                               g_post_mix, g_pre_ffn, w_r, tag)
    cap = (EC_CAPACITY_FACTOR * n) // N_EXPERTS
    pos, post, tbl = _route_call(aff, cap, tag)
    xs = _sc_gather_call(h, post, cap)
    return dict(x1=x1, xs=xs, aff=aff, pos=pos, tbl=tbl, cap=cap,
                mod_row_of_tile=mod_row_of_tile, tag=tag, shape=(batch, seq, d)), (k_a, v_a, s_f, s_b)


def _expert_ffn(groups, mod3, g_post_ffn, w1, w3, w2):
    yes = _ffn_call([(g["xs"], g["cap"]) for g in groups], w1, w3, w2)
    outs = []
    for g, ye in zip(groups, yes):
        y = _combine_call(g["tbl"], ye, g["pos"], g["aff"], g["x1"], mod3, g["mod_row_of_tile"], g_post_ffn,
                          g["cap"], g["tag"])
        outs.append(y.reshape(g["shape"]))
    return outs


def kernel(x_prompt, x_sample, cache_k, cache_v, state_gla_fwd, state_gla_bwd, c, c_ctx, g_pre_mix, g_post_mix, g_pre_ffn, g_post_ffn, w_mod, b_mod, w_in, g_q, g_k, w_gk2_f, b_gk_f, w_gk2_b, b_gk_b, g_gla, w_pa, w_pg, w_out, w_router, w1, w3, w2):
    depth = w_in.shape[0]
    assert depth == 1, "single trunk layer"
    d = x_prompt.shape[-1]
    dec_batch, dec_seq, _ = x_sample.shape
    assert dec_batch + 1 <= MOD_ROWS
    l = 0
    rank = GLA_GATE_RANK
    w_in_p = jnp.swapaxes(w_in[l], 0, 1).astype(BF16)
    gkw = w_gk2_f.shape[-1]
    wgf = jnp.zeros((V7X_LANES, gkw), F32).at[0:rank].set(w_gk2_f[l]).astype(BF16)
    wgb = jnp.zeros((V7X_LANES, gkw), F32).at[rank:2 * rank].set(w_gk2_b[l]).astype(BF16)
    w_r = jnp.zeros((d, V7X_LANES), F32).at[:, :N_EXPERTS].set(w_router[l]).astype(BF16)
    row = lambda a: a[l].reshape(1, -1)
    lw = (row(g_pre_mix), row(g_post_mix), row(g_pre_ffn), row(g_post_ffn), w_in_p, row(g_q), row(g_k),
          wgf, row(b_gk_f), wgb, row(b_gk_b), row(g_gla),
          w_pa[l].astype(BF16), w_pg[l].astype(BF16), w_out[l].astype(BF16), w_r, w1[l], w3[l], w2[l])

    cc = jnp.concatenate([c_ctx[None, :], c, jnp.zeros((MOD_ROWS - 1 - dec_batch, d), F32)], axis=0)
    mod = _mod_call(cc, w_mod[l], b_mod[l].reshape(1, -1))
    mod3 = mod.reshape(MOD_ROWS, 1, 6 * d)

    gp, (nk, nv, nsf, nsb) = _trunk_to_routing(x_prompt, mod3, lambda i, tm: 0, None, None, lw, "ctx")
    ctx = (cache_k, cache_v, state_gla_fwd, state_gla_bwd)
    gs, _ = _trunk_to_routing(x_sample, mod3, lambda i, tm: 1 + (i * tm) // dec_seq, _rope_tables(dec_seq),
                              ctx, lw, "lat")
    yp, ys = _expert_ffn([gp, gs], mod3, lw[3], lw[16], lw[17], lw[18])
    return (yp, ys, nk, nv, nsf, nsb)
```

```python
import functools

import jax
import jax.numpy as jnp
from jax import lax
from jax.experimental import pallas as pl
from jax.experimental.pallas import tpu as pltpu
from jax.experimental.pallas import tpu_sc as plsc

F32 = jnp.float32
BF16 = jnp.bfloat16
I32 = jnp.int32

N_HEADS = 8
N_KV_HEADS = 2
HEAD_DIM = 128
GRID_W = 64
ROPE_THETA = 10000.0
GLA_HEADS = 4
GLA_GATE_RANK = 16
GLA_GATE_NORM = 16.0
GLA_CHUNK = 64
N_EXPERTS = 16
EC_CAPACITY_FACTOR = 2
EPS = 1e-6

V7X_LANES = 128
V7X_VMEM_BYTES = 64 * 1024 * 1024
V7X_VMEM_RESERVE_BYTES = 6 * 1024 * 1024
BF16_ROWS_PER_TILE = 16
V7X_SC_CORES = 2
V7X_SC_SUBCORES = 16
V7X_SC_LANES = 16

TOKEN_TILE = 512
OUTPROJ_TILE = 512
OUTPROJ_ROW_GROUPS = 4
ATTN_Q_TILE = 256
ATTN_SEQS_PER_STEP = 4
GLA_BLOCK = 256
GLA_HEADS_PER_STAGE_GROUP = 2
GLA_CTX_HEADS_PER_STEP = 4
GLA_CTX_SEQS_PER_STEP = 4
GLA_LAT_HEADS_PER_STEP = 2
ROUTE_BLOCK = 256
SC_GATHER_CHUNK = 64
COMBINE_TILE = 512
COMBINE_WINDOW = 128
COMBINE_EXPERT_GROUP = 2
MOD_ROWS = 8
MOD_N_TILE = 1536


def _cparams(semantics):
    return pltpu.CompilerParams(dimension_semantics=semantics,
                                vmem_limit_bytes=V7X_VMEM_BYTES - V7X_VMEM_RESERVE_BYTES)


def _sigmoid(x):
    return 0.5 * (jnp.tanh(0.5 * x) + 1.0)


def _silu(x):
    return x * _sigmoid(x)


def _log_sigmoid(x):
    return jnp.minimum(x, 0.0) - jnp.log1p(jnp.exp(-jnp.abs(x)))


def _rms(x, g):
    ms = jnp.mean(x * x, axis=-1, keepdims=True)
    return x * lax.rsqrt(ms + EPS) * g


def _dot(a, b):
    return jnp.dot(a, b, preferred_element_type=F32)


def _dot_nt(a, b):
    return lax.dot_general(a, b, (((1,), (1,)), ((), ())), preferred_element_type=F32)


def _mod_kernel(c_ref, w_ref, b_ref, o_ref):
    s = _silu(c_ref[...]).astype(BF16)
    o_ref[...] = _dot(s, w_ref[...].astype(BF16)) + b_ref[...]


def _mod_call(cc, w_mod, b_mod):
    d, n6 = w_mod.shape
    tn = MOD_N_TILE
    return pl.pallas_call(
        _mod_kernel,
        out_shape=jax.ShapeDtypeStruct((MOD_ROWS, n6), F32),
        grid=(n6 // tn,),
        in_specs=[pl.BlockSpec((MOD_ROWS, d), lambda j: (0, 0)),
                  pl.BlockSpec((d, tn), lambda j: (0, j)),
                  pl.BlockSpec((1, tn), lambda j: (0, j))],
        out_specs=pl.BlockSpec((MOD_ROWS, tn), lambda j: (0, j)),
        compiler_params=_cparams(("arbitrary",)),
        name="mod",
    )(cc, w_mod, b_mod)


def _inproj_layout(d):
    aq, akv = N_HEADS * HEAD_DIM, N_KV_HEADS * HEAD_DIM
    gk, gv = d // 2, d
    names = ("q_a", "k_a", "v_a", "q_g", "k_g", "v_g", "r_g", "gk_f", "gk_b", "gates")
    widths = (aq, akv, akv, gk, gk, gv, gv, GLA_GATE_RANK, GLA_GATE_RANK, 2 * d)
    off, o = {}, 0
    for nme, w in zip(names, widths):
        off[nme] = (o, o + w)
        o += w
    off["gk"] = (off["gk_f"][0], off["gk_f"][0] + V7X_LANES)
    return off, o


def _inproj_kernel(*refs, rope, d):
    if rope:
        (x_ref, mod_ref, gpre_ref, w_ref, gq_ref, gk_ref, wgf_ref, bgf_ref, wgb_ref, bgb_ref,
         cos_ref, se_ref, so_ref, *outs) = refs
    else:
        (x_ref, mod_ref, gpre_ref, w_ref, gq_ref, gk_ref, wgf_ref, bgf_ref, wgb_ref, bgb_ref,
         *outs) = refs
    qa_ref, k_ref, v_ref, qg_ref, kg_ref, vg_ref, rg_ref, lgf_ref, lgb_ref, gate_ref = outs
    off, _ = _inproj_layout(d)
    m = mod_ref[0]
    h = _rms(x_ref[...], gpre_ref[...]) * (1.0 + m[:, d:2 * d]) + m[:, 0:d]
    hb = h.astype(BF16)

    def proj(name):
        a, b = off[name]
        return _dot_nt(hb, w_ref[a:b, :])

    def qk_norm(y, g_ref):
        y = _rms(y, g_ref[...])
        if rope:
            nxt = pltpu.roll(y, HEAD_DIM - 1, axis=1)
            prv = pltpu.roll(y, 1, axis=1)
            y = y * cos_ref[...] + nxt * se_ref[...] + prv * so_ref[...]
        return y

    gk = proj("gk").astype(BF16)

    q = proj("q_a")
    scale = HEAD_DIM ** -0.5
    for hd in range(N_HEADS):
        sl = slice(hd * HEAD_DIM, (hd + 1) * HEAD_DIM)
        qa_ref[:, sl] = (qk_norm(q[:, sl], gq_ref) * scale).astype(BF16)

    k = proj("k_a")
    v = proj("v_a")
    tb, _, _, ts, _ = k_ref.shape
    for kv in range(N_KV_HEADS):
        sl = slice(kv * HEAD_DIM, (kv + 1) * HEAD_DIM)
        k_ref[:, 0, kv] = qk_norm(k[:, sl], gk_ref).reshape(tb, ts, HEAD_DIM)
        v_ref[:, 0, kv] = v[:, sl].reshape(tb, ts, HEAD_DIM)

    dk = (d // 2) // GLA_HEADS
    qg_ref[...] = (proj("q_g") * (dk ** -0.5)).astype(BF16)
    kg_ref[...] = proj("k_g").astype(BF16)
    vg_ref[...] = proj("v_g").astype(BF16)
    rg_ref[...] = _silu(proj("r_g")).astype(BF16)

    gate_ref[...] = _sigmoid(proj("gates")).astype(BF16)

    lgf_ref[...] = _log_sigmoid(_dot(gk, wgf_ref[...]) + bgf_ref[...]) * (1.0 / GLA_GATE_NORM)
    lgb_ref[...] = _log_sigmoid(_dot(gk, wgb_ref[...]) + bgb_ref[...]) * (1.0 / GLA_GATE_NORM)


def _inproj_call(x2, mod3, mod_row_of_tile, g_pre, w_in_p, g_q, g_k, wgf, bgf, wgb, bgb, rope_tabs,
                 batch, seq):
    n, d = x2.shape
    tm = TOKEN_TILE
    _, dinp = _inproj_layout(d)
    rope = rope_tabs is not None
    gk_w = d // 2
    if seq >= tm:
        tb, ts, per = 1, tm, seq // tm
        kv_map = lambda i: (i // per, 0, 0, i % per, 0)
    else:
        tb, ts, per = tm // seq, seq, 1
        kv_map = lambda i: (i, 0, 0, 0, 0)
    row = lambda i: (i, 0)
    const = lambda i: (0, 0)
    in_specs = [
        pl.BlockSpec((tm, d), row),
        pl.BlockSpec((1, 1, 6 * d), lambda i: (mod_row_of_tile(i, tm), 0, 0)),
        pl.BlockSpec((1, d), const),
        pl.BlockSpec((dinp, d), const, pipeline_mode=pl.Buffered(1)),
        pl.BlockSpec((1, HEAD_DIM), const),
        pl.BlockSpec((1, HEAD_DIM), const),
        pl.BlockSpec((V7X_LANES, gk_w), const),
        pl.BlockSpec((1, gk_w), const),
        pl.BlockSpec((V7X_LANES, gk_w), const),
        pl.BlockSpec((1, gk_w), const),
    ]
    args = [x2, mod3, g_pre, w_in_p, g_q, g_k, wgf, bgf, wgb, bgb]
    if rope:
        tab = pl.BlockSpec((tm, HEAD_DIM), lambda i: (i % per, 0))
        in_specs += [tab, tab, tab]
        args += list(rope_tabs)
    kv_shape = jax.ShapeDtypeStruct((batch, 1, N_KV_HEADS, seq, HEAD_DIM), F32)
    kv_spec = pl.BlockSpec((tb, 1, N_KV_HEADS, ts, HEAD_DIM), kv_map)
    out_shape = (
        jax.ShapeDtypeStruct((n, N_HEADS * HEAD_DIM), BF16), kv_shape, kv_shape,
        jax.ShapeDtypeStruct((n, gk_w), BF16), jax.ShapeDtypeStruct((n, gk_w), BF16),
        jax.ShapeDtypeStruct((n, d), BF16), jax.ShapeDtypeStruct((n, d), BF16),
        jax.ShapeDtypeStruct((n, gk_w), F32), jax.ShapeDtypeStruct((n, gk_w), F32),
        jax.ShapeDtypeStruct((n, 2 * d), BF16),
    )
    out_specs = (
        pl.BlockSpec((tm, N_HEADS * HEAD_DIM), row), kv_spec, kv_spec,
        pl.BlockSpec((tm, gk_w), row), pl.BlockSpec((tm, gk_w), row),
        pl.BlockSpec((tm, d), row), pl.BlockSpec((tm, d), row),
        pl.BlockSpec((tm, gk_w), row), pl.BlockSpec((tm, gk_w), row),
        pl.BlockSpec((tm, 2 * d), row),
    )
    return pl.pallas_call(
        functools.partial(_inproj_kernel, rope=rope, d=d),
        out_shape=out_shape,
        grid=(n // tm,),
        in_specs=in_specs,
        out_specs=out_specs,
        compiler_params=_cparams(("parallel",)),
        name="inproj_lat" if rope else "inproj_ctx",
    )(*args)


def _attn_kernel(*refs, cached):
    if cached:
        q_ref, k_ref, v_ref, ck_ref, cv_ref, o_ref = refs
    else:
        q_ref, k_ref, v_ref, o_ref = refs
    seqs = k_ref.shape[0]
    tq = q_ref.shape[0] // seqs
    grp = N_HEADS // N_KV_HEADS
    chains = [(sq, kv) for sq in range(seqs) for kv in range(N_KV_HEADS)]

    def scores_of(sq, kv):
        rows = pl.ds(sq * tq, tq)
        kk = k_ref[sq, 0, kv].astype(BF16)
        vv = v_ref[sq, 0, kv].astype(BF16)
        if cached:
            kk = jnp.concatenate([ck_ref[sq, 0, kv].astype(BF16), kk], axis=0)
            vv = jnp.concatenate([cv_ref[sq, 0, kv].astype(BF16), vv], axis=0)
        heads = [q_ref[rows, (kv * grp + g) * HEAD_DIM:(kv * grp + g + 1) * HEAD_DIM] for g in range(grp)]
        return _dot_nt(jnp.concatenate(heads, axis=0), kk), vv

    look_ahead = cached
    ahead = scores_of(*chains[0]) if look_ahead else None
    for n, (sq, kv) in enumerate(chains):
        s, vv = ahead if look_ahead else scores_of(sq, kv)
        if look_ahead and n + 1 < len(chains):
            ahead = scores_of(*chains[n + 1])
        rows = pl.ds(sq * tq, tq)
        p = jnp.exp(s - jnp.max(s, axis=-1, keepdims=True))
        l = jnp.sum(p, axis=-1, keepdims=True)
        o = _dot(p.astype(BF16), vv) / l
        for g in range(grp):
            hd = kv * grp + g
            o_ref[rows, hd * HEAD_DIM:(hd + 1) * HEAD_DIM] = o[g * tq:(g + 1) * tq].astype(BF16)


def _attn_call(q_a, k_a, v_a, cache_k, cache_v):
    batch, _, _, seq, _ = k_a.shape
    n, aq = q_a.shape
    tq = min(ATTN_Q_TILE, seq)
    per = seq // tq
    cached = cache_k is not None
    seqs = ATTN_SEQS_PER_STEP if (per == 1 and not cached) else 1
    batch = batch // seqs
    own = pl.BlockSpec((seqs, 1, N_KV_HEADS, seq, HEAD_DIM), lambda b, j: (b, 0, 0, 0, 0))
    tq = tq * seqs
    in_specs = [pl.BlockSpec((tq, aq), lambda b, j: (b * per + j, 0)), own, own]
    args = [q_a, k_a, v_a]
    if cached:
        past = cache_k.shape[3]
        cspec = pl.BlockSpec((1, 1, N_KV_HEADS, past, HEAD_DIM), lambda b, j: (b, 0, 0, 0, 0))
        in_specs += [cspec, cspec]
        args += [cache_k, cache_v]
    return pl.pallas_call(
        functools.partial(_attn_kernel, cached=cached),
        out_shape=jax.ShapeDtypeStruct((n, aq), BF16),
        grid=(batch, per),
        in_specs=in_specs,
        out_specs=pl.BlockSpec((tq, aq), lambda b, j: (b * per + j, 0)),
        compiler_params=_cparams(("parallel", "parallel")),
        name="attn_lat" if cached else "attn_ctx",
    )(*args)


def _gla_kernel(*refs, nblk, heads, seqs, has_state, emit_state):
    refs = list(refs)
    q_ref, k_ref, v_ref, lgf_ref, lgb_ref, rg_ref, gg_ref = refs[:7]
    pos = 7
    if has_state:
        s0f_ref, s0b_ref = refs[pos:pos + 2]
        pos += 2
    og_ref = refs[pos]
    pos += 1
    if emit_state:
        sf_ref, sb_ref = refs[pos:pos + 2]
        pos += 2
    of_scr, ob_scr = refs[pos:pos + 2]

    blk = GLA_BLOCK
    ch = GLA_CHUNK
    nch = blk // ch
    dk = q_ref.shape[1] // heads
    dv = v_ref.shape[1] // heads
    shift = ch.bit_length() - 1
    row_in_chunk = lax.broadcasted_iota(I32, (blk, dk), 0) & (ch - 1)
    ri = lax.broadcasted_iota(I32, (blk, blk), 0)
    ci = lax.broadcasted_iota(I32, (blk, blk), 1)
    same = (ri >> shift) == (ci >> shift)
    mask_f = same & (ci <= ri)
    mask_b = same & (ci >= ri)

    def one_block(b0, hd, reverse):
        rows = pl.ds(b0, blk)
        kcols = slice(hd * dk, (hd + 1) * dk)
        q = q_ref[rows, kcols].astype(F32)
        k = k_ref[rows, kcols].astype(F32)
        v = v_ref[rows, hd * dv:(hd + 1) * dv]
        b = (lgb_ref if reverse else lgf_ref)[rows, kcols]
        mask = mask_b if reverse else mask_f
        s = 1
        while s < ch:
            if reverse:
                sh = pltpu.roll(b, blk - s, axis=0)
                b = b + jnp.where(row_in_chunk < ch - s, sh, 0.0)
            else:
                sh = pltpu.roll(b, s, axis=0)
                b = b + jnp.where(row_in_chunk >= s, sh, 0.0)
            s *= 2
        qe = (q * jnp.exp(b)).astype(BF16)
        ke = (k * jnp.exp(-b)).astype(BF16)
        scores = _dot_nt(qe, ke)
        end_row = [c * ch + (0 if reverse else ch - 1) for c in range(nch)]
        ends = [b[r:r + 1, :] for r in end_row]
        b_end = jnp.concatenate([jnp.broadcast_to(e, (ch, dk)) for e in ends], axis=0)
        kd = (k * jnp.exp(b_end - b)).astype(BF16)
        return dict(scores=scores, mask=mask, qe=qe, kd=kd, v=v, ends=ends, reverse=reverse)

    def chunk_products(w):
        return [lax.dot_general(w["kd"][c * ch:(c + 1) * ch], w["v"][c * ch:(c + 1) * ch],
                                (((0,), (0,)), ((), ())), preferred_element_type=F32) for c in range(nch)]

    def intra_chunk(w):
        return _dot(jnp.where(w["mask"], w["scores"], 0.0).astype(BF16), w["v"])

    def across_chunks(w, kv, state):
        decay = jnp.exp(jnp.concatenate(w["ends"] + [jnp.zeros((dk - nch, dk), F32)], axis=0)).T
        inter = [None] * nch
        for c in (range(nch - 1, -1, -1) if w["reverse"] else range(nch)):
            if state is None:
                inter[c] = jnp.zeros((ch, dv), F32)
                state = kv[c]
            else:
                inter[c] = _dot(w["qe"][c * ch:(c + 1) * ch], state.astype(BF16))
                state = decay[:, c:c + 1] * state + kv[c]
        return jnp.concatenate(inter, axis=0), state

    group = min(GLA_HEADS_PER_STAGE_GROUP, heads)
    for sq in range(seqs):
        srows = pl.ds(sq * nblk * blk, nblk * blk)
        for h0 in range(0, heads, group):
            hds = list(range(h0, h0 + group))
            states = {(hd, rev): ((s0b_ref if rev else s0f_ref)[sq, 0, hd] if has_state else None)
                      for hd in hds for rev in (False, True)}
            for i in range(nblk):
                block_of = {False: sq * nblk + i, True: sq * nblk + nblk - 1 - i}
                chains = [(hd, rev) for hd in hds for rev in (False, True)]
                work = {c: one_block(block_of[c[1]] * blk, c[0], c[1]) for c in chains}
                kvs = {c: chunk_products(work[c]) for c in chains}
                intra = {c: intra_chunk(work[c]) for c in chains}
                for c in chains:
                    hd, rev = c
                    inter, states[c] = across_chunks(work[c], kvs[c], states[c])
                    scr = ob_scr if rev else of_scr
                    scr[pl.ds(block_of[rev] * blk, blk), hd * dv:(hd + 1) * dv] = intra[c] + inter
            for hd in hds:
                vcols = slice(hd * dv, (hd + 1) * dv)
                if emit_state:
                    sf_ref[sq, 0, hd] = states[(hd, False)]
                    sb_ref[sq, 0, hd] = states[(hd, True)]
                o = of_scr[srows, vcols] + ob_scr[srows, vcols]
                og_ref[srows, vcols] = (_rms(o, gg_ref[...]) * rg_ref[srows, vcols].astype(F32)).astype(BF16)


def _gla_call(q_g, k_g, v_g, lg_f, lg_b, r_g, g_gla, state_f, state_b, batch, seq, emit_state, heads, seqs):
    n, gkw = q_g.shape
    d = v_g.shape[1]
    dk, dv = gkw // GLA_HEADS, d // GLA_HEADS
    has_state = state_f is not None
    nblk = seq // GLA_BLOCK
    batch = batch // seqs
    seq = seq * seqs
    kspec = pl.BlockSpec((seq, heads * dk), lambda b, h: (b, h))
    vspec = pl.BlockSpec((seq, heads * dv), lambda b, h: (b, h))
    sspec = pl.BlockSpec((seqs, 1, heads, dk, dv), lambda b, h: (b, 0, h, 0, 0))
    in_specs = [kspec, kspec, vspec, kspec, kspec, vspec, pl.BlockSpec((1, dv), lambda b, h: (0, 0))]
    args = [q_g, k_g, v_g, lg_f, lg_b, r_g, g_gla]
    if has_state:
        in_specs += [sspec, sspec]
        args += [state_f, state_b]
    out_shape = [jax.ShapeDtypeStruct((n, d), BF16)]
    out_specs = [vspec]
    if emit_state:
        st = jax.ShapeDtypeStruct((batch * seqs, 1, GLA_HEADS, dk, dv), F32)
        out_shape += [st, st]
        out_specs += [sspec, sspec]
    return pl.pallas_call(
        functools.partial(_gla_kernel, nblk=nblk, heads=heads, seqs=seqs, has_state=has_state,
                          emit_state=emit_state),
        out_shape=tuple(out_shape),
        grid=(batch, GLA_HEADS // heads),
        in_specs=in_specs,
        out_specs=tuple(out_specs),
        scratch_shapes=[pltpu.VMEM((seq, heads * dv), F32), pltpu.VMEM((seq, heads * dv), F32)],
        compiler_params=_cparams(("parallel", "parallel")),
        name="gla_ctx" if emit_state else "gla_lat",
    )(*args)


def _outproj_kernel(oa_ref, og_ref, gate_ref, x_ref, mod_ref, wpa_ref, wpg_ref, wout_ref, gpm_ref,
                    gpf_ref, wr_ref, x1_ref, h_ref, aff_ref):
    d = x_ref.shape[1]
    m = mod_ref[0]
    sub = x_ref.shape[0] // OUTPROJ_ROW_GROUPS
    groups = [pl.ds(g * sub, sub) for g in range(OUTPROJ_ROW_GROUPS)]
    branch = [(_dot(oa_ref[r, :], wpa_ref[...]), _dot(og_ref[r, :], wpg_ref[...])) for r in groups]
    mo = jnp.concatenate(
        [_dot((gate_ref[r, 0:d].astype(F32) * oa + gate_ref[r, d:2 * d].astype(F32) * og).astype(BF16),
              wout_ref[...]) for r, (oa, og) in zip(groups, branch)], axis=0)
    x1 = x_ref[...] + m[:, 2 * d:3 * d] * _rms(mo, gpm_ref[...])
    x1_ref[...] = x1
    h = _rms(x1, gpf_ref[...]) * (1.0 + m[:, 4 * d:5 * d]) + m[:, 3 * d:4 * d]
    h_ref[...] = h
    logits = _dot(h.astype(BF16), wr_ref[...])
    valid = lax.broadcasted_iota(I32, logits.shape, 1) < N_EXPERTS
    mx = jnp.max(jnp.where(valid, logits, -jnp.inf), axis=-1, keepdims=True)
    ex = jnp.where(valid, jnp.exp(logits - mx), 0.0)
    aff_ref[...] = ex / jnp.sum(ex, axis=-1, keepdims=True)


def _outproj_call(o_a, o_g, gates, x2, mod3, mod_row_of_tile, w_pa, w_pg, w_out, g_pm, g_pf, w_r, tag):
    n, d = x2.shape
    tm = OUTPROJ_TILE
    row = lambda i: (i, 0)
    const = lambda i: (0, 0)
    wspec = pl.BlockSpec((d, d), const, pipeline_mode=pl.Buffered(1))
    return pl.pallas_call(
        _outproj_kernel,
        out_shape=(jax.ShapeDtypeStruct((n, d), F32), jax.ShapeDtypeStruct((n, d), F32),
                   jax.ShapeDtypeStruct((n, V7X_LANES), F32)),
        grid=(n // tm,),
        in_specs=[pl.BlockSpec((tm, d), row), pl.BlockSpec((tm, d), row), pl.BlockSpec((tm, 2 * d), row),
                  pl.BlockSpec((tm, d), row),
                  pl.BlockSpec((1, 1, 6 * d), lambda i: (mod_row_of_tile(i, tm), 0, 0)),
                  wspec, wspec, wspec, pl.BlockSpec((1, d), const), pl.BlockSpec((1, d), const),
                  pl.BlockSpec((d, V7X_LANES), const)],
        out_specs=(pl.BlockSpec((tm, d), row), pl.BlockSpec((tm, d), row),
                   pl.BlockSpec((tm, V7X_LANES), row)),
        compiler_params=_cparams(("parallel",)),
        name="outproj_" + tag,
    )(o_a, o_g, gates, x2, mod3, w_pa, w_pg, w_out, g_pm, g_pf, w_r)


def _route_kernel(aff_ref, pos_ref, post_ref, tbl_ref, afft_scr, *, cap):
    n = aff_ref.shape[0]
    rb = ROUTE_BLOCK
    nb = n // rb
    lanes = aff_ref.shape[1]
    lane = lax.broadcasted_iota(I32, (1, lanes), 1)
    expert_lane = lane < N_EXPERTS
    tbl_ref[...] = jnp.zeros(tbl_ref.shape, I32)

    def to_token_lanes(c, carry):
        start = pl.multiple_of(c * rb, rb)
        afft_scr[c] = aff_ref[pl.ds(start, rb), :].T[0:N_EXPERTS, :]
        return carry

    lax.fori_loop(0, nb, to_token_lanes, 0)
    aff_t = afft_scr[...]

    def count(hit):
        return jnp.sum(jnp.sum(hit.astype(I32), axis=0), axis=1, keepdims=True)

    def bit_step(i, lo):
        t = lo | jnp.left_shift(jnp.int32(1), 30 - i)
        ge = aff_t >= lax.bitcast_convert_type(t, F32)[None]
        return jnp.where(count(ge) >= cap, t, lo)

    thr_bits = lax.fori_loop(0, 31, bit_step, jnp.zeros((N_EXPERTS, 1), I32))
    need_t = cap - count(aff_t > lax.bitcast_convert_type(thr_bits, F32)[None])

    def to_expert_lanes(col):
        full = jnp.concatenate([jnp.broadcast_to(col, (N_EXPERTS, lanes)),
                                jnp.zeros((lanes - N_EXPERTS, lanes), I32)], axis=0)
        return full.T[0:1, :]

    thr = lax.bitcast_convert_type(to_expert_lanes(thr_bits), F32)
    need = to_expert_lanes(need_t).astype(F32)
    capf = float(cap)

    r = lax.broadcasted_iota(I32, (rb, rb), 0)
    c_ = lax.broadcasted_iota(I32, (rb, rb), 1)
    tril = jnp.where(c_ <= r, 1.0, 0.0).astype(BF16)

    def blk_step(c, carry):
        eq_before, raw_before = carry
        start = pl.multiple_of(c * rb, rb)
        a = aff_ref[pl.ds(start, rb), :]
        gt = a > thr
        eq = a == thr
        eq_incl = _dot(tril, jnp.where(eq, 1.0, 0.0).astype(BF16)) + eq_before
        raw = (gt | (eq & (eq_incl <= need))) & expert_lane
        raw_incl = _dot(tril, jnp.where(raw, 1.0, 0.0).astype(BF16)) + raw_before
        sel = raw & (raw_incl <= capf)
        self_ = jnp.where(sel, 1.0, 0.0)
        incl = jnp.minimum(raw_incl, capf)
        sel_before = jnp.minimum(raw_before, capf)
        excl = incl - self_
        posb = jnp.where(sel, excl, -1.0).astype(I32)
        pos_ref[pl.ds(start, rb), :] = posb
        post_ref[:, pl.ds(start, rb)] = posb.T[0:N_EXPERTS, :]
        tbl_ref[pl.ds(c, 1), :] = sel_before.astype(I32)
        return (eq_incl[rb - 1:rb, :], raw_incl[rb - 1:rb, :])

    zero = jnp.zeros((1, lanes), F32)
    _, total = lax.fori_loop(0, nb, blk_step, (zero, zero), unroll=2)
    tbl_ref[pl.ds(nb, 1), :] = jnp.minimum(total, capf).astype(I32)


def _route_call(aff, cap, tag):
    n, lanes = aff.shape
    nb = n // ROUTE_BLOCK
    tbl_rows = -(-(nb + 1) // 8) * 8
    full = lambda *shape: pl.BlockSpec(shape, lambda: tuple(0 for _ in shape))
    return pl.pallas_call(
        functools.partial(_route_kernel, cap=cap),
        out_shape=(jax.ShapeDtypeStruct((n, lanes), I32),
                   jax.ShapeDtypeStruct((N_EXPERTS, n), I32),
                   jax.ShapeDtypeStruct((tbl_rows, lanes), I32)),
        in_specs=[full(n, lanes)],
        out_specs=(full(n, lanes), full(N_EXPERTS, n), full(tbl_rows, lanes)),
        scratch_shapes=[pltpu.VMEM((nb, N_EXPERTS, ROUTE_BLOCK), F32)],
        compiler_params=_cparams(()),
        name="route_" + tag,
    )(aff)


def _sc_gather_call(table, post, cap):
    n_exp, n = post.shape
    words = table.shape[1]
    workers = V7X_SC_CORES * V7X_SC_SUBCORES
    parts = workers // n_exp
    chunk = SC_GATHER_CHUNK
    lanes = V7X_SC_LANES
    per_part = cap // parts
    assert parts * n_exp == workers and per_part % chunk == 0 and n % lanes == 0
    mesh = plsc.VectorSubcoreMesh(core_axis_name="c", subcore_axis_name="s",
                                  num_cores=V7X_SC_CORES, num_subcores=V7X_SC_SUBCORES)

    def body(table_hbm, post_hbm, out_hbm, pos_v, idx_v, rows_v, sem):
        wid = lax.axis_index("s") * V7X_SC_CORES + lax.axis_index("c")
        e = wid // parts
        part = wid % parts
        pltpu.sync_copy(post_hbm.at[e], pos_v)
        lane = lax.iota(I32, lanes)

        @pl.loop(0, n, step=lanes)
        def _(t0):
            p = pos_v[pl.ds(t0, lanes)]
            plsc.store_scatter(idx_v, [p], lane + t0, mask=p >= 0)

        @pl.loop(0, per_part // chunk)
        def _(j):
            off = pl.multiple_of(part * per_part + j * chunk, chunk)
            pltpu.async_copy(table_hbm.at[idx_v.at[pl.ds(off, chunk)]], rows_v, sem).wait()
            pltpu.sync_copy(rows_v, out_hbm.at[pl.ds(e * cap + off, chunk)])

    return pl.kernel(
        body,
        out_type=jax.ShapeDtypeStruct((n_exp * cap, words), table.dtype),
        mesh=mesh,
        scratch_types=[pltpu.VMEM((n,), I32), pltpu.VMEM((cap,), I32),
                       pltpu.VMEM((chunk, words), table.dtype), pltpu.SemaphoreType.DMA],
        compiler_params=pltpu.CompilerParams(needs_layout_passes=False),
        name="sc_gather",
    )(table, post)


def _ffn_kernel(*refs, caps):
    ng = len(caps)
    xs_refs = refs[:ng]
    w1_ref, w3_ref, w2_ref = refs[ng:ng + 3]
    ye_refs = refs[ng + 3:2 * ng + 3]
    row_off = [sum(caps[:g]) for g in range(ng)]

    xs = jnp.concatenate([r[...].astype(BF16) for r in xs_refs], axis=0)
    hid = _silu(_dot(xs, w1_ref[...].astype(BF16))) * _dot(xs, w3_ref[...].astype(BF16))
    ye = _dot(hid.astype(BF16), w2_ref[...].astype(BF16)).astype(BF16)
    for g in range(ng):
        ye_refs[g][...] = ye[row_off[g]:row_off[g] + caps[g]]


def _ffn_call(groups, w1, w3, w2):
    caps = tuple(g[1] for g in groups)
    n_exp, d, dff = w1.shape
    in_specs = [pl.BlockSpec((cap, d), lambda e: (e, 0)) for cap in caps]
    in_specs += [pl.BlockSpec((None, d, dff), lambda e: (e, 0, 0)),
                 pl.BlockSpec((None, d, dff), lambda e: (e, 0, 0)),
                 pl.BlockSpec((None, dff, d), lambda e: (e, 0, 0))]
    return pl.pallas_call(
        functools.partial(_ffn_kernel, caps=caps),
        out_shape=tuple(jax.ShapeDtypeStruct((n_exp * cap, d), BF16) for cap in caps),
        grid=(n_exp,),
        in_specs=in_specs,
        out_specs=tuple(pl.BlockSpec((cap, d), lambda e: (e, 0)) for cap in caps),
        compiler_params=_cparams(("arbitrary",)),
        name="ffn",
    )(*[g[0] for g in groups], w1, w3, w2)


def _combine_kernel(tbl_ref, ye_hbm, pos_ref, aff_ref, x1_ref, mod_ref, gpo_ref, y_ref, buf, sem, xbuf, xsem,
                    acc_scr, *, cap, blocks_per_tile):
    i = pl.program_id(0)
    nsteps = pl.num_programs(0)
    d = x1_ref.shape[1]
    lanes = pos_ref.shape[1]
    win = COMBINE_WINDOW
    last_start = ye_hbm.shape[0] - win
    slot = i % 2

    def first_row(step, e):
        return tbl_ref[step * blocks_per_tile, e] + e * cap

    def window_start(first, k):
        unclamped = (first // BF16_ROWS_PER_TILE) * BF16_ROWS_PER_TILE + k * win
        return unclamped, jnp.minimum(unclamped, last_start)

    def fetch(step, to_slot, e):
        start = window_start(first_row(step, e), 0)[1]
        return pltpu.make_async_copy(ye_hbm.at[pl.ds(pl.multiple_of(start, BF16_ROWS_PER_TILE), win), :],
                                     buf.at[to_slot, pl.ds(e * win, win), :], sem.at[to_slot, e])

    @pl.when(i == 0)
    def _prime():
        for e in range(N_EXPERTS):
            fetch(0, 0, e).start()

    @pl.when(i + 1 < nsteps)
    def _ahead():
        for e in range(N_EXPERTS):
            fetch(i + 1, 1 - slot, e).start()

    for e in range(N_EXPERTS):
        fetch(i, slot, e).wait()
    lane_row = lax.broadcasted_iota(I32, (1, win), 1)
    pieces = []
    for e in range(N_EXPERTS):
        pcol = pos_ref[:, e:e + 1]
        grow = jnp.where(pcol >= 0, pcol + e * cap, -1)
        start = window_start(first_row(i, e), 0)[1]
        pieces.append(jnp.where(grow == start + lane_row, aff_ref[:, e:e + 1], 0.0).astype(BF16))
    total = None
    for e0 in range(0, N_EXPERTS, COMBINE_EXPERT_GROUP):
        grp = jnp.concatenate(pieces[e0:e0 + COMBINE_EXPERT_GROUP], axis=1)
        part = _dot(grp, buf[slot, pl.ds(e0 * win, COMBINE_EXPERT_GROUP * win), :])
        total = part if total is None else total + part
    acc_scr[...] = total

    def extra_windows(e):
        covered = window_start(first_row(i, e), 1)[0]
        return jnp.maximum(first_row(i + 1, e) - covered + win - 1, 0) // win

    def expert_extra(e, carry):
        first = first_row(i, e)
        extra = extra_windows(e)

        def more(k, c):
            unclamped, start = window_start(first, k)
            cp = pltpu.make_async_copy(ye_hbm.at[pl.ds(pl.multiple_of(start, BF16_ROWS_PER_TILE), win), :],
                                       xbuf, xsem)
            cp.start()
            cp.wait()
            at_e = lax.broadcasted_iota(I32, (1, lanes), 1) == e
            pcol = jnp.sum(jnp.where(at_e, pos_ref[...].astype(F32), 0.0), axis=1, keepdims=True).astype(I32)
            wcol = jnp.sum(jnp.where(at_e, aff_ref[...], 0.0), axis=1, keepdims=True)
            grow = jnp.where(pcol >= 0, pcol + e * cap, -1)
            hit = (grow == start + lane_row) & (grow >= unclamped)
            acc_scr[...] += _dot(jnp.where(hit, wcol, 0.0).astype(BF16), xbuf[...])
            return c

        lax.fori_loop(1, 1 + extra, more, 0)
        return carry

    any_extra = extra_windows(0)
    for e in range(1, N_EXPERTS):
        any_extra = any_extra + extra_windows(e)

    @pl.when(any_extra > 0)
    def _overflow():
        lax.fori_loop(0, N_EXPERTS, expert_extra, 0)

    m = mod_ref[0]
    y_ref[...] = x1_ref[...] + m[:, 5 * d:6 * d] * _rms(acc_scr[...], gpo_ref[...])


def _combine_call(tbl, ye, pos, aff, x1, mod3, mod_row_of_tile, g_po, cap, tag):
    n, d = x1.shape
    tm = COMBINE_TILE
    lanes = pos.shape[1]
    grid_spec = pltpu.PrefetchScalarGridSpec(
        num_scalar_prefetch=1,
        grid=(n // tm,),
        in_specs=[pl.BlockSpec(memory_space=pl.ANY),
                  pl.BlockSpec((tm, lanes), lambda i, t: (i, 0)),
                  pl.BlockSpec((tm, lanes), lambda i, t: (i, 0)),
                  pl.BlockSpec((tm, d), lambda i, t: (i, 0)),
                  pl.BlockSpec((1, 1, 6 * d), lambda i, t: (mod_row_of_tile(i, tm), 0, 0)),
                  pl.BlockSpec((1, d), lambda i, t: (0, 0))],
        out_specs=pl.BlockSpec((tm, d), lambda i, t: (i, 0)),
        scratch_shapes=[pltpu.VMEM((2, N_EXPERTS * COMBINE_WINDOW, d), BF16),
                        pltpu.SemaphoreType.DMA((2, N_EXPERTS)),
                        pltpu.VMEM((COMBINE_WINDOW, d), BF16),
                        pltpu.SemaphoreType.DMA(()),
                        pltpu.VMEM((tm, d), F32)],
    )
    return pl.pallas_call(
        functools.partial(_combine_kernel, cap=cap, blocks_per_tile=tm // ROUTE_BLOCK),
        out_shape=jax.ShapeDtypeStruct((n, d), F32),
        grid_spec=grid_spec,
        compiler_params=_cparams(("arbitrary",)),
        name="combine_" + tag,
    )(tbl, ye, pos, aff, x1, mod3, g_po)


def _rope_tables(seq):
    rows = seq // GRID_W
    r = jnp.repeat(jnp.arange(rows), GRID_W).astype(F32)
    col = jnp.tile(jnp.arange(GRID_W), rows).astype(F32)
    pairs = HEAD_DIM // 4
    freqs = ROPE_THETA ** (-jnp.arange(pairs, dtype=F32) / pairs)
    ang = jnp.concatenate([r[:, None] * freqs, col[:, None] * freqs], axis=-1)
    cos = jnp.repeat(jnp.cos(ang), 2, axis=-1)
    sin = jnp.repeat(jnp.sin(ang), 2, axis=-1)
    even = (jnp.arange(HEAD_DIM) % 2) == 0
    return cos, jnp.where(even, -sin, 0.0), jnp.where(even, 0.0, sin)


def _trunk_to_routing(x, mod3, mod_row_of_tile, rope_tabs, ctx, lw, tag):
    (g_pre_mix, g_post_mix, g_pre_ffn, g_post_ffn, w_in_p, g_q, g_k, wgf, bgf, wgb, bgb, g_gla,
     w_pa, w_pg, w_out, w_r, w1, w3, w2) = lw
    batch, seq, d = x.shape
    n = batch * seq
    x2 = x.reshape(n, d)
    (q_a, k_a, v_a, q_g, k_g, v_g, r_g, lg_f, lg_b, gates) = _inproj_call(
        x2, mod3, mod_row_of_tile, g_pre_mix, w_in_p, g_q, g_k, wgf, bgf, wgb, bgb, rope_tabs, batch, seq)
    if ctx is None:
        o_a = _attn_call(q_a, k_a, v_a, None, None)
        o_g, s_f, s_b = _gla_call(q_g, k_g, v_g, lg_f, lg_b, r_g, g_gla, None, None, batch, seq, True,
                                  GLA_CTX_HEADS_PER_STEP, GLA_CTX_SEQS_PER_STEP)
    else:
        ck, cv, s_f0, s_b0 = ctx
        o_a = _attn_call(q_a, k_a, v_a, ck, cv)
        (o_g,) = _gla_call(q_g, k_g, v_g, lg_f, lg_b, r_g, g_gla, s_f0, s_b0, batch, seq, False,
                           GLA_LAT_HEADS_PER_STEP, 1)
        s_f = s_b = None
    x1, h, aff = _outproj_call(o_a, o_g, gates, x2, mod3, mod_row_of_tile, w_pa, w_pg, w_out,
                               g_post_mix, g_pre_ffn, w_r, tag)
    cap = (EC_CAPACITY_FACTOR * n) // N_EXPERTS
    pos, post, tbl = _route_call(aff, cap, tag)
    xs = _sc_gather_call(h, post, cap)
    return dict(x1=x1, xs=xs, aff=aff, pos=pos, tbl=tbl, cap=cap,
                mod_row_of_tile=mod_row_of_tile, tag=tag, shape=(batch, seq, d)), (k_a, v_a, s_f, s_b)


def _expert_ffn(groups, mod3, g_post_ffn, w1, w3, w2):
    yes = _ffn_call([(g["xs"], g["cap"]) for g in groups], w1, w3, w2)
    outs = []
    for g, ye in zip(groups, yes):
        y = _combine_call(g["tbl"], ye, g["pos"], g["aff"], g["x1"], mod3, g["mod_row_of_tile"], g_post_ffn,
                          g["cap"], g["tag"])
        outs.append(y.reshape(g["shape"]))
    return outs


def kernel(x_prompt, x_sample, cache_k, cache_v, state_gla_fwd, state_gla_bwd, c, c_ctx, g_pre_mix, g_post_mix, g_pre_ffn, g_post_ffn, w_mod, b_mod, w_in, g_q, g_k, w_gk2_f, b_gk_f, w_gk2_b, b_gk_b, g_gla, w_pa, w_pg, w_out, w_router, w1, w3, w2):
    depth = w_in.shape[0]
    assert depth == 1, "single trunk layer"
    d = x_prompt.shape[-1]
    dec_batch, dec_seq, _ = x_sample.shape
    assert dec_batch + 1 <= MOD_ROWS
    l = 0
    rank = GLA_GATE_RANK
    w_in_p = jnp.swapaxes(w_in[l], 0, 1).astype(BF16)
    gkw = w_gk2_f.shape[-1]
    wgf = jnp.zeros((V7X_LANES, gkw), F32).at[0:rank].set(w_gk2_f[l]).astype(BF16)
    wgb = jnp.zeros((V7X_LANES, gkw), F32).at[rank:2 * rank].set(w_gk2_b[l]).astype(BF16)
    w_r = jnp.zeros((d, V7X_LANES), F32).at[:, :N_EXPERTS].set(w_router[l]).astype(BF16)
    row = lambda a: a[l].reshape(1, -1)
    lw = (row(g_pre_mix), row(g_post_mix), row(g_pre_ffn), row(g_post_ffn), w_in_p, row(g_q), row(g_k),
          wgf, row(b_gk_f), wgb, row(b_gk_b), row(g_gla),
          w_pa[l].astype(BF16), w_pg[l].astype(BF16), w_out[l].astype(BF16), w_r, w1[l], w3[l], w2[l])

    cc = jnp.concatenate([c_ctx[None, :], c, jnp.zeros((MOD_ROWS - 1 - dec_batch, d), F32)], axis=0)
    mod = _mod_call(cc, w_mod[l], b_mod[l].reshape(1, -1))
    mod3 = mod.reshape(MOD_ROWS, 1, 6 * d)

    gp, (nk, nv, nsf, nsb) = _trunk_to_routing(x_prompt, mod3, lambda i, tm: 0, None, None, lw, "ctx")
    ctx = (cache_k, cache_v, state_gla_fwd, state_gla_bwd)
    gs, _ = _trunk_to_routing(x_sample, mod3, lambda i, tm: 1 + (i * tm) // dec_seq, _rope_tables(dec_seq),
                              ctx, lw, "lat")
    yp, ys = _expert_ffn([gp, gs], mod3, lw[3], lw[16], lw[17], lw[18])
    return (yp, ys, nk, nv, nsf, nsb)
```

```python
import functools

import jax
import jax.numpy as jnp
from jax import lax
from jax.experimental import pallas as pl
from jax.experimental.pallas import tpu as pltpu
from jax.experimental.pallas import tpu_sc as plsc

F32 = jnp.float32
BF16 = jnp.bfloat16
I32 = jnp.int32

N_HEADS = 8
N_KV_HEADS = 2
HEAD_DIM = 128
GRID_W = 64
ROPE_THETA = 10000.0
GLA_HEADS = 4
GLA_GATE_RANK = 16
GLA_GATE_NORM = 16.0
GLA_CHUNK = 64
N_EXPERTS = 16
EC_CAPACITY_FACTOR = 2
EPS = 1e-6

V7X_LANES = 128
V7X_VMEM_BYTES = 64 * 1024 * 1024
V7X_VMEM_RESERVE_BYTES = 6 * 1024 * 1024
BF16_ROWS_PER_TILE = 16
V7X_SC_CORES = 2
V7X_SC_SUBCORES = 16
V7X_SC_LANES = 16

TOKEN_TILE = 512
OUTPROJ_TILE = 512
OUTPROJ_ROW_GROUPS = 4
ATTN_Q_TILE = 256
ATTN_SEQS_PER_STEP = 4
GLA_BLOCK = 256
GLA_HEADS_PER_STAGE_GROUP = 2
GLA_CTX_HEADS_PER_STEP = 4
GLA_CTX_SEQS_PER_STEP = 4
GLA_LAT_HEADS_PER_STEP = 2
ROUTE_BLOCK = 256
SC_GATHER_CHUNK = 128
COMBINE_TILE = 512
COMBINE_WINDOW = 128
COMBINE_EXPERT_GROUP = 2
MOD_ROWS = 8
MOD_N_TILE = 1536


def _cparams(semantics):
    return pltpu.CompilerParams(dimension_semantics=semantics,
                                vmem_limit_bytes=V7X_VMEM_BYTES - V7X_VMEM_RESERVE_BYTES)


def _sigmoid(x):
    return 0.5 * (jnp.tanh(0.5 * x) + 1.0)


def _silu(x):
    return x * _sigmoid(x)


def _log_sigmoid(x):
    return jnp.minimum(x, 0.0) - jnp.log1p(jnp.exp(-jnp.abs(x)))


def _rms(x, g):
    ms = jnp.mean(x * x, axis=-1, keepdims=True)
    return x * lax.rsqrt(ms + EPS) * g


def _dot(a, b):
    return jnp.dot(a, b, preferred_element_type=F32)


def _dot_nt(a, b):
    return lax.dot_general(a, b, (((1,), (1,)), ((), ())), preferred_element_type=F32)


def _mod_kernel(c_ref, w_ref, b_ref, o_ref):
    s = _silu(c_ref[...]).astype(BF16)
    o_ref[...] = _dot(s, w_ref[...].astype(BF16)) + b_ref[...]


def _mod_call(cc, w_mod, b_mod):
    d, n6 = w_mod.shape
    tn = MOD_N_TILE
    return pl.pallas_call(
        _mod_kernel,
        out_shape=jax.ShapeDtypeStruct((MOD_ROWS, n6), F32),
        grid=(n6 // tn,),
        in_specs=[pl.BlockSpec((MOD_ROWS, d), lambda j: (0, 0)),
                  pl.BlockSpec((d, tn), lambda j: (0, j)),
                  pl.BlockSpec((1, tn), lambda j: (0, j))],
        out_specs=pl.BlockSpec((MOD_ROWS, tn), lambda j: (0, j)),
        compiler_params=_cparams(("arbitrary",)),
        name="mod",
    )(cc, w_mod, b_mod)


def _inproj_layout(d):
    aq, akv = N_HEADS * HEAD_DIM, N_KV_HEADS * HEAD_DIM
    gk, gv = d // 2, d
    names = ("q_a", "k_a", "v_a", "q_g", "k_g", "v_g", "r_g", "gk_f", "gk_b", "gates")
    widths = (aq, akv, akv, gk, gk, gv, gv, GLA_GATE_RANK, GLA_GATE_RANK, 2 * d)
    off, o = {}, 0
    for nme, w in zip(names, widths):
        off[nme] = (o, o + w)
        o += w
    off["gk"] = (off["gk_f"][0], off["gk_f"][0] + V7X_LANES)
    return off, o


def _inproj_kernel(*refs, rope, d):
    if rope:
        (x_ref, mod_ref, gpre_ref, w_ref, gq_ref, gk_ref, wgf_ref, bgf_ref, wgb_ref, bgb_ref,
         cos_ref, se_ref, so_ref, *outs) = refs
    else:
        (x_ref, mod_ref, gpre_ref, w_ref, gq_ref, gk_ref, wgf_ref, bgf_ref, wgb_ref, bgb_ref,
         *outs) = refs
    qa_ref, k_ref, v_ref, qg_ref, kg_ref, vg_ref, rg_ref, lgf_ref, lgb_ref, gate_ref = outs
    off, _ = _inproj_layout(d)
    m = mod_ref[0]
    h = _rms(x_ref[...], gpre_ref[...]) * (1.0 + m[:, d:2 * d]) + m[:, 0:d]
    hb = h.astype(BF16)

    def proj(name):
        a, b = off[name]
        return _dot_nt(hb, w_ref[a:b, :].astype(BF16))

    def qk_norm(y, g_ref):
        y = _rms(y, g_ref[...])
        if rope:
            nxt = pltpu.roll(y, HEAD_DIM - 1, axis=1)
            prv = pltpu.roll(y, 1, axis=1)
            y = y * cos_ref[...] + nxt * se_ref[...] + prv * so_ref[...]
        return y

    gk = proj("gk").astype(BF16)

    q = proj("q_a")
    scale = HEAD_DIM ** -0.5
    for hd in range(N_HEADS):
        sl = slice(hd * HEAD_DIM, (hd + 1) * HEAD_DIM)
        qa_ref[:, sl] = (qk_norm(q[:, sl], gq_ref) * scale).astype(BF16)

    k = proj("k_a")
    v = proj("v_a")
    tb, _, _, ts, _ = k_ref.shape
    for kv in range(N_KV_HEADS):
        sl = slice(kv * HEAD_DIM, (kv + 1) * HEAD_DIM)
        k_ref[:, 0, kv] = qk_norm(k[:, sl], gk_ref).reshape(tb, ts, HEAD_DIM)
        v_ref[:, 0, kv] = v[:, sl].reshape(tb, ts, HEAD_DIM)

    dk = (d // 2) // GLA_HEADS
    qg_ref[...] = (proj("q_g") * (dk ** -0.5)).astype(BF16)
    kg_ref[...] = proj("k_g").astype(BF16)
    vg_ref[...] = proj("v_g").astype(BF16)
    rg_ref[...] = _silu(proj("r_g")).astype(BF16)

    gate_ref[...] = _sigmoid(proj("gates")).astype(BF16)

    lgf_ref[...] = _log_sigmoid(_dot(gk, wgf_ref[...]) + bgf_ref[...]) * (1.0 / GLA_GATE_NORM)
    lgb_ref[...] = _log_sigmoid(_dot(gk, wgb_ref[...]) + bgb_ref[...]) * (1.0 / GLA_GATE_NORM)


def _inproj_call(x2, mod3, mod_row_of_tile, g_pre, w_in_p, g_q, g_k, wgf, bgf, wgb, bgb, rope_tabs,
                 batch, seq):
    n, d = x2.shape
    tm = TOKEN_TILE
    _, dinp = _inproj_layout(d)
    rope = rope_tabs is not None
    gk_w = d // 2
    if seq >= tm:
        tb, ts, per = 1, tm, seq // tm
        kv_map = lambda i: (i // per, 0, 0, i % per, 0)
    else:
        tb, ts, per = tm // seq, seq, 1
        kv_map = lambda i: (i, 0, 0, 0, 0)
    row = lambda i: (i, 0)
    const = lambda i: (0, 0)
    in_specs = [
        pl.BlockSpec((tm, d), row),
        pl.BlockSpec((1, 1, 6 * d), lambda i: (mod_row_of_tile(i, tm), 0, 0)),
        pl.BlockSpec((1, d), const),
        pl.BlockSpec((dinp, d), const, pipeline_mode=pl.Buffered(1)),
        pl.BlockSpec((1, HEAD_DIM), const),
        pl.BlockSpec((1, HEAD_DIM), const),
        pl.BlockSpec((V7X_LANES, gk_w), const),
        pl.BlockSpec((1, gk_w), const),
        pl.BlockSpec((V7X_LANES, gk_w), const),
        pl.BlockSpec((1, gk_w), const),
    ]
    args = [x2, mod3, g_pre, w_in_p, g_q, g_k, wgf, bgf, wgb, bgb]
    if rope:
        tab = pl.BlockSpec((tm, HEAD_DIM), lambda i: (i % per, 0))
        in_specs += [tab, tab, tab]
        args += list(rope_tabs)
    kv_shape = jax.ShapeDtypeStruct((batch, 1, N_KV_HEADS, seq, HEAD_DIM), F32)
    kv_spec = pl.BlockSpec((tb, 1, N_KV_HEADS, ts, HEAD_DIM), kv_map)
    out_shape = (
        jax.ShapeDtypeStruct((n, N_HEADS * HEAD_DIM), BF16), kv_shape, kv_shape,
        jax.ShapeDtypeStruct((n, gk_w), BF16), jax.ShapeDtypeStruct((n, gk_w), BF16),
        jax.ShapeDtypeStruct((n, d), BF16), jax.ShapeDtypeStruct((n, d), BF16),
        jax.ShapeDtypeStruct((n, gk_w), F32), jax.ShapeDtypeStruct((n, gk_w), F32),
        jax.ShapeDtypeStruct((n, 2 * d), BF16),
    )
    out_specs = (
        pl.BlockSpec((tm, N_HEADS * HEAD_DIM), row), kv_spec, kv_spec,
        pl.BlockSpec((tm, gk_w), row), pl.BlockSpec((tm, gk_w), row),
        pl.BlockSpec((tm, d), row), pl.BlockSpec((tm, d), row),
        pl.BlockSpec((tm, gk_w), row), pl.BlockSpec((tm, gk_w), row),
        pl.BlockSpec((tm, 2 * d), row),
    )
    return pl.pallas_call(
        functools.partial(_inproj_kernel, rope=rope, d=d),
        out_shape=out_shape,
        grid=(n // tm,),
        in_specs=in_specs,
        out_specs=out_specs,
        compiler_params=_cparams(("parallel",)),
        name="inproj_lat" if rope else "inproj_ctx",
    )(*args)


def _attn_kernel(*refs, cached):
    if cached:
        q_ref, k_ref, v_ref, ck_ref, cv_ref, o_ref = refs
    else:
        q_ref, k_ref, v_ref, o_ref = refs
    seqs = k_ref.shape[0]
    tq = q_ref.shape[0] // seqs
    grp = N_HEADS // N_KV_HEADS
    chains = [(sq, kv) for sq in range(seqs) for kv in range(N_KV_HEADS)]

    def scores_of(sq, kv):
        rows = pl.ds(sq * tq, tq)
        kk = k_ref[sq, 0, kv].astype(BF16)
        vv = v_ref[sq, 0, kv].astype(BF16)
        if cached:
            kk = jnp.concatenate([ck_ref[sq, 0, kv].astype(BF16), kk], axis=0)
            vv = jnp.concatenate([cv_ref[sq, 0, kv].astype(BF16), vv], axis=0)
        heads = [q_ref[rows, (kv * grp + g) * HEAD_DIM:(kv * grp + g + 1) * HEAD_DIM] for g in range(grp)]
        return _dot_nt(jnp.concatenate(heads, axis=0), kk), vv

    look_ahead = cached
    ahead = scores_of(*chains[0]) if look_ahead else None
    for n, (sq, kv) in enumerate(chains):
        s, vv = ahead if look_ahead else scores_of(sq, kv)
        if look_ahead and n + 1 < len(chains):
            ahead = scores_of(*chains[n + 1])
        rows = pl.ds(sq * tq, tq)
        p = jnp.exp(s - jnp.max(s, axis=-1, keepdims=True))
        l = jnp.sum(p, axis=-1, keepdims=True)
        o = _dot(p.astype(BF16), vv) / l
        for g in range(grp):
            hd = kv * grp + g
            o_ref[rows, hd * HEAD_DIM:(hd + 1) * HEAD_DIM] = o[g * tq:(g + 1) * tq].astype(BF16)


def _attn_call(q_a, k_a, v_a, cache_k, cache_v):
    batch, _, _, seq, _ = k_a.shape
    n, aq = q_a.shape
    tq = min(ATTN_Q_TILE, seq)
    per = seq // tq
    cached = cache_k is not None
    seqs = ATTN_SEQS_PER_STEP if (per == 1 and not cached) else 1
    batch = batch // seqs
    own = pl.BlockSpec((seqs, 1, N_KV_HEADS, seq, HEAD_DIM), lambda b, j: (b, 0, 0, 0, 0))
    tq = tq * seqs
    in_specs = [pl.BlockSpec((tq, aq), lambda b, j: (b * per + j, 0)), own, own]
    args = [q_a, k_a, v_a]
    if cached:
        past = cache_k.shape[3]
        cspec = pl.BlockSpec((1, 1, N_KV_HEADS, past, HEAD_DIM), lambda b, j: (b, 0, 0, 0, 0))
        in_specs += [cspec, cspec]
        args += [cache_k, cache_v]
    return pl.pallas_call(
        functools.partial(_attn_kernel, cached=cached),
        out_shape=jax.ShapeDtypeStruct((n, aq), BF16),
        grid=(batch, per),
        in_specs=in_specs,
        out_specs=pl.BlockSpec((tq, aq), lambda b, j: (b * per + j, 0)),
        compiler_params=_cparams(("parallel", "parallel")),
        name="attn_lat" if cached else "attn_ctx",
    )(*args)


def _gla_kernel(*refs, nblk, heads, seqs, has_state, emit_state):
    refs = list(refs)
    q_ref, k_ref, v_ref, lgf_ref, lgb_ref, rg_ref, gg_ref = refs[:7]
    pos = 7
    if has_state:
        s0f_ref, s0b_ref = refs[pos:pos + 2]
        pos += 2
    og_ref = refs[pos]
    pos += 1
    if emit_state:
        sf_ref, sb_ref = refs[pos:pos + 2]
        pos += 2
    of_scr, ob_scr = refs[pos:pos + 2]

    blk = GLA_BLOCK
    ch = GLA_CHUNK
    nch = blk // ch
    dk = q_ref.shape[1] // heads
    dv = v_ref.shape[1] // heads
    shift = ch.bit_length() - 1
    row_in_chunk = lax.broadcasted_iota(I32, (blk, dk), 0) & (ch - 1)
    ri = lax.broadcasted_iota(I32, (blk, blk), 0)
    ci = lax.broadcasted_iota(I32, (blk, blk), 1)
    same = (ri >> shift) == (ci >> shift)
    mask_f = same & (ci <= ri)
    mask_b = same & (ci >= ri)

    def one_block(b0, hd, reverse):
        rows = pl.ds(b0, blk)
        kcols = slice(hd * dk, (hd + 1) * dk)
        q = q_ref[rows, kcols].astype(F32)
        k = k_ref[rows, kcols].astype(F32)
        v = v_ref[rows, hd * dv:(hd + 1) * dv]
        b = (lgb_ref if reverse else lgf_ref)[rows, kcols]
        mask = mask_b if reverse else mask_f
        s = 1
        while s < ch:
            if reverse:
                sh = pltpu.roll(b, blk - s, axis=0)
                b = b + jnp.where(row_in_chunk < ch - s, sh, 0.0)
            else:
                sh = pltpu.roll(b, s, axis=0)
                b = b + jnp.where(row_in_chunk >= s, sh, 0.0)
            s *= 2
        qe = (q * jnp.exp(b)).astype(BF16)
        ke = (k * jnp.exp(-b)).astype(BF16)
        scores = _dot_nt(qe, ke)
        end_row = [c * ch + (0 if reverse else ch - 1) for c in range(nch)]
        ends = [b[r:r + 1, :] for r in end_row]
        b_end = jnp.concatenate([jnp.broadcast_to(e, (ch, dk)) for e in ends], axis=0)
        kd = (k * jnp.exp(b_end - b)).astype(BF16)
        return dict(scores=scores, mask=mask, qe=qe, kd=kd, v=v, ends=ends, reverse=reverse)

    def chunk_products(w):
        return [lax.dot_general(w["kd"][c * ch:(c + 1) * ch], w["v"][c * ch:(c + 1) * ch],
                                (((0,), (0,)), ((), ())), preferred_element_type=F32) for c in range(nch)]

    def intra_chunk(w):
        return _dot(jnp.where(w["mask"], w["scores"], 0.0).astype(BF16), w["v"])

    def across_chunks(w, kv, state):
        decay = jnp.exp(jnp.concatenate(w["ends"] + [jnp.zeros((dk - nch, dk), F32)], axis=0)).T
        inter = [None] * nch
        for c in (range(nch - 1, -1, -1) if w["reverse"] else range(nch)):
            if state is None:
                inter[c] = jnp.zeros((ch, dv), F32)
                state = kv[c]
            else:
                inter[c] = _dot(w["qe"][c * ch:(c + 1) * ch], state.astype(BF16))
                state = decay[:, c:c + 1] * state + kv[c]
        return jnp.concatenate(inter, axis=0), state

    group = min(GLA_HEADS_PER_STAGE_GROUP, heads)
    for sq in range(seqs):
        srows = pl.ds(sq * nblk * blk, nblk * blk)
        for h0 in range(0, heads, group):
            hds = list(range(h0, h0 + group))
            states = {(hd, rev): ((s0b_ref if rev else s0f_ref)[sq, 0, hd] if has_state else None)
                      for hd in hds for rev in (False, True)}
            for i in range(nblk):
                block_of = {False: sq * nblk + i, True: sq * nblk + nblk - 1 - i}
                chains = [(hd, rev) for hd in hds for rev in (False, True)]
                work = {c: one_block(block_of[c[1]] * blk, c[0], c[1]) for c in chains}
                kvs = {c: chunk_products(work[c]) for c in chains}
                intra = {c: intra_chunk(work[c]) for c in chains}
                for c in chains:
                    hd, rev = c
                    inter, states[c] = across_chunks(work[c], kvs[c], states[c])
                    scr = ob_scr if rev else of_scr
                    scr[pl.ds(block_of[rev] * blk, blk), hd * dv:(hd + 1) * dv] = intra[c] + inter
            for hd in hds:
                vcols = slice(hd * dv, (hd + 1) * dv)
                if emit_state:
                    sf_ref[sq, 0, hd] = states[(hd, False)]
                    sb_ref[sq, 0, hd] = states[(hd, True)]
                o = of_scr[srows, vcols] + ob_scr[srows, vcols]
                og_ref[srows, vcols] = (_rms(o, gg_ref[...]) * rg_ref[srows, vcols].astype(F32)).astype(BF16)


def _gla_call(q_g, k_g, v_g, lg_f, lg_b, r_g, g_gla, state_f, state_b, batch, seq, emit_state, heads, seqs):
    n, gkw = q_g.shape
    d = v_g.shape[1]
    dk, dv = gkw // GLA_HEADS, d // GLA_HEADS
    has_state = state_f is not None
    nblk = seq // GLA_BLOCK
    batch = batch // seqs
    seq = seq * seqs
    kspec = pl.BlockSpec((seq, heads * dk), lambda b, h: (b, h))
    vspec = pl.BlockSpec((seq, heads * dv), lambda b, h: (b, h))
    sspec = pl.BlockSpec((seqs, 1, heads, dk, dv), lambda b, h: (b, 0, h, 0, 0))
    in_specs = [kspec, kspec, vspec, kspec, kspec, vspec, pl.BlockSpec((1, dv), lambda b, h: (0, 0))]
    args = [q_g, k_g, v_g, lg_f, lg_b, r_g, g_gla]
    if has_state:
        in_specs += [sspec, sspec]
        args += [state_f, state_b]
    out_shape = [jax.ShapeDtypeStruct((n, d), BF16)]
    out_specs = [vspec]
    if emit_state:
        st = jax.ShapeDtypeStruct((batch * seqs, 1, GLA_HEADS, dk, dv), F32)
        out_shape += [st, st]
        out_specs += [sspec, sspec]
    return pl.pallas_call(
        functools.partial(_gla_kernel, nblk=nblk, heads=heads, seqs=seqs, has_state=has_state,
                          emit_state=emit_state),
        out_shape=tuple(out_shape),
        grid=(batch, GLA_HEADS // heads),
        in_specs=in_specs,
        out_specs=tuple(out_specs),
        scratch_shapes=[pltpu.VMEM((seq, heads * dv), F32), pltpu.VMEM((seq, heads * dv), F32)],
        compiler_params=_cparams(("parallel", "parallel")),
        name="gla_ctx" if emit_state else "gla_lat",
    )(*args)


def _outproj_kernel(oa_ref, og_ref, gate_ref, x_ref, mod_ref, wpa_ref, wpg_ref, wout_ref, gpm_ref,
                    gpf_ref, wr_ref, x1_ref, h_ref, aff_ref):
    d = x_ref.shape[1]
    m = mod_ref[0]
    sub = x_ref.shape[0] // OUTPROJ_ROW_GROUPS
    groups = [pl.ds(g * sub, sub) for g in range(OUTPROJ_ROW_GROUPS)]
    branch = [(_dot(oa_ref[r, :], wpa_ref[...]), _dot(og_ref[r, :], wpg_ref[...])) for r in groups]
    mo = jnp.concatenate(
        [_dot((gate_ref[r, 0:d].astype(F32) * oa + gate_ref[r, d:2 * d].astype(F32) * og).astype(BF16),
              wout_ref[...]) for r, (oa, og) in zip(groups, branch)], axis=0)
    x1 = x_ref[...] + m[:, 2 * d:3 * d] * _rms(mo, gpm_ref[...])
    x1_ref[...] = x1
    hb = (_rms(x1, gpf_ref[...]) * (1.0 + m[:, 4 * d:5 * d]) + m[:, 3 * d:4 * d]).astype(BF16)
    bits = lax.bitcast_convert_type(hb.astype(F32), jnp.uint32)
    packed = (bits[:, 0:d // 2] >> 16) | (bits[:, d // 2:d] & jnp.uint32(0xFFFF0000))
    h_ref[...] = lax.bitcast_convert_type(packed, I32)
    logits = _dot(hb, wr_ref[...])
    valid = lax.broadcasted_iota(I32, logits.shape, 1) < N_EXPERTS
    mx = jnp.max(jnp.where(valid, logits, -jnp.inf), axis=-1, keepdims=True)
    ex = jnp.where(valid, jnp.exp(logits - mx), 0.0)
    aff_ref[...] = ex / jnp.sum(ex, axis=-1, keepdims=True)


def _outproj_call(o_a, o_g, gates, x2, mod3, mod_row_of_tile, w_pa, w_pg, w_out, g_pm, g_pf, w_r, tag):
    n, d = x2.shape
    tm = OUTPROJ_TILE
    row = lambda i: (i, 0)
    const = lambda i: (0, 0)
    wspec = pl.BlockSpec((d, d), const, pipeline_mode=pl.Buffered(1))
    return pl.pallas_call(
        _outproj_kernel,
        out_shape=(jax.ShapeDtypeStruct((n, d), F32), jax.ShapeDtypeStruct((n, d // 2), I32),
                   jax.ShapeDtypeStruct((n, V7X_LANES), F32)),
        grid=(n // tm,),
        in_specs=[pl.BlockSpec((tm, d), row), pl.BlockSpec((tm, d), row), pl.BlockSpec((tm, 2 * d), row),
                  pl.BlockSpec((tm, d), row),
                  pl.BlockSpec((1, 1, 6 * d), lambda i: (mod_row_of_tile(i, tm), 0, 0)),
                  wspec, wspec, wspec, pl.BlockSpec((1, d), const), pl.BlockSpec((1, d), const),
                  pl.BlockSpec((d, V7X_LANES), const)],
        out_specs=(pl.BlockSpec((tm, d), row), pl.BlockSpec((tm, d // 2), row),
                   pl.BlockSpec((tm, V7X_LANES), row)),
        compiler_params=_cparams(("parallel",)),
        name="outproj_" + tag,
    )(o_a, o_g, gates, x2, mod3, w_pa, w_pg, w_out, g_pm, g_pf, w_r)


def _route_kernel(aff_ref, pos_ref, post_ref, tbl_ref, afft_scr, *, cap):
    n = aff_ref.shape[0]
    rb = ROUTE_BLOCK
    nb = n // rb
    lanes = aff_ref.shape[1]
    lane = lax.broadcasted_iota(I32, (1, lanes), 1)
    expert_lane = lane < N_EXPERTS
    tbl_ref[...] = jnp.zeros(tbl_ref.shape, I32)

    def to_token_lanes(c, carry):
        start = pl.multiple_of(c * rb, rb)
        afft_scr[c] = aff_ref[pl.ds(start, rb), :].T[0:N_EXPERTS, :]
        return carry

    lax.fori_loop(0, nb, to_token_lanes, 0)
    aff_t = afft_scr[...]

    def count(hit):
        return jnp.sum(jnp.sum(hit.astype(I32), axis=0), axis=1, keepdims=True)

    def bit_step(i, lo):
        t = lo | jnp.left_shift(jnp.int32(1), 30 - i)
        ge = aff_t >= lax.bitcast_convert_type(t, F32)[None]
        return jnp.where(count(ge) >= cap, t, lo)

    thr_bits = lax.fori_loop(0, 31, bit_step, jnp.zeros((N_EXPERTS, 1), I32))
    need_t = cap - count(aff_t > lax.bitcast_convert_type(thr_bits, F32)[None])

    def to_expert_lanes(col):
        full = jnp.concatenate([jnp.broadcast_to(col, (N_EXPERTS, lanes)),
                                jnp.zeros((lanes - N_EXPERTS, lanes), I32)], axis=0)
        return full.T[0:1, :]

    thr = lax.bitcast_convert_type(to_expert_lanes(thr_bits), F32)
    need = to_expert_lanes(need_t).astype(F32)
    capf = float(cap)

    r = lax.broadcasted_iota(I32, (rb, rb), 0)
    c_ = lax.broadcasted_iota(I32, (rb, rb), 1)
    tril = jnp.where(c_ <= r, 1.0, 0.0).astype(BF16)

    def blk_step(c, carry):
        eq_before, raw_before = carry
        start = pl.multiple_of(c * rb, rb)
        a = aff_ref[pl.ds(start, rb), :]
        gt = a > thr
        eq = a == thr
        eq_incl = _dot(tril, jnp.where(eq, 1.0, 0.0).astype(BF16)) + eq_before
        raw = (gt | (eq & (eq_incl <= need))) & expert_lane
        raw_incl = _dot(tril, jnp.where(raw, 1.0, 0.0).astype(BF16)) + raw_before
        sel = raw & (raw_incl <= capf)
        self_ = jnp.where(sel, 1.0, 0.0)
        incl = jnp.minimum(raw_incl, capf)
        sel_before = jnp.minimum(raw_before, capf)
        excl = incl - self_
        posb = jnp.where(sel, excl, -1.0).astype(I32)
        pos_ref[pl.ds(start, rb), :] = posb
        post_ref[:, pl.ds(start, rb)] = posb.T[0:N_EXPERTS, :]
        tbl_ref[pl.ds(c, 1), :] = sel_before.astype(I32)
        return (eq_incl[rb - 1:rb, :], raw_incl[rb - 1:rb, :])

    zero = jnp.zeros((1, lanes), F32)
    _, total = lax.fori_loop(0, nb, blk_step, (zero, zero), unroll=2)
    tbl_ref[pl.ds(nb, 1), :] = jnp.minimum(total, capf).astype(I32)


def _route_call(aff, cap, tag):
    n, lanes = aff.shape
    nb = n // ROUTE_BLOCK
    tbl_rows = -(-(nb + 1) // 8) * 8
    full = lambda *shape: pl.BlockSpec(shape, lambda: tuple(0 for _ in shape))
    return pl.pallas_call(
        functools.partial(_route_kernel, cap=cap),
        out_shape=(jax.ShapeDtypeStruct((n, lanes), I32),
                   jax.ShapeDtypeStruct((N_EXPERTS, n), I32),
                   jax.ShapeDtypeStruct((tbl_rows, lanes), I32)),
        in_specs=[full(n, lanes)],
        out_specs=(full(n, lanes), full(N_EXPERTS, n), full(tbl_rows, lanes)),
        scratch_shapes=[pltpu.VMEM((nb, N_EXPERTS, ROUTE_BLOCK), F32)],
        compiler_params=_cparams(()),
        name="route_" + tag,
    )(aff)


def _sc_gather_call(table, post, cap):
    n_exp, n = post.shape
    words = table.shape[1]
    workers = V7X_SC_CORES * V7X_SC_SUBCORES
    parts = workers // n_exp
    chunk = SC_GATHER_CHUNK
    lanes = V7X_SC_LANES
    per_part = cap // parts
    assert parts * n_exp == workers and per_part % chunk == 0 and n % lanes == 0
    mesh = plsc.VectorSubcoreMesh(core_axis_name="c", subcore_axis_name="s",
                                  num_cores=V7X_SC_CORES, num_subcores=V7X_SC_SUBCORES)

    def body(table_hbm, post_hbm, out_hbm, pos_v, idx_v, rows_v, sem):
        wid = lax.axis_index("s") * V7X_SC_CORES + lax.axis_index("c")
        e = wid // parts
        part = wid % parts
        pltpu.sync_copy(post_hbm.at[e], pos_v)
        lane = lax.iota(I32, lanes)

        @pl.loop(0, n, step=lanes)
        def _(t0):
            p = pos_v[pl.ds(t0, lanes)]
            plsc.store_scatter(idx_v, [p], lane + t0, mask=p >= 0)

        @pl.loop(0, per_part // chunk)
        def _(j):
            off = pl.multiple_of(part * per_part + j * chunk, chunk)
            pltpu.async_copy(table_hbm.at[idx_v.at[pl.ds(off, chunk)]], rows_v, sem).wait()
            pltpu.sync_copy(rows_v, out_hbm.at[pl.ds(e * cap + off, chunk)])

    return pl.kernel(
        body,
        out_type=jax.ShapeDtypeStruct((n_exp * cap, words), table.dtype),
        mesh=mesh,
        scratch_types=[pltpu.VMEM((n,), I32), pltpu.VMEM((cap,), I32),
                       pltpu.VMEM((chunk, words), table.dtype), pltpu.SemaphoreType.DMA],
        compiler_params=pltpu.CompilerParams(needs_layout_passes=False),
        name="sc_gather",
    )(table, post)


def _ffn_kernel(*refs, caps):
    ng = len(caps)
    xs_refs = refs[:ng]
    w1_ref, w3_ref, w2_ref = refs[ng:ng + 3]
    ye_refs = refs[ng + 3:2 * ng + 3]
    row_off = [sum(caps[:g]) for g in range(ng)]

    def unpack(words):
        w = lax.bitcast_convert_type(words, jnp.uint32)
        lo = lax.bitcast_convert_type(w << 16, F32).astype(BF16)
        hi = lax.bitcast_convert_type(w & jnp.uint32(0xFFFF0000), F32).astype(BF16)
        return jnp.concatenate([lo, hi], axis=1)

    xs = jnp.concatenate([unpack(r[...]) for r in xs_refs], axis=0)
    hid = _silu(_dot(xs, w1_ref[...].astype(BF16))) * _dot(xs, w3_ref[...].astype(BF16))
    ye = _dot(hid.astype(BF16), w2_ref[...].astype(BF16)).astype(BF16)
    for g in range(ng):
        ye_refs[g][...] = ye[row_off[g]:row_off[g] + caps[g]]


def _ffn_call(groups, w1, w3, w2):
    caps = tuple(g[1] for g in groups)
    n_exp, d, dff = w1.shape
    in_specs = [pl.BlockSpec((cap, d // 2), lambda e: (e, 0)) for cap in caps]
    in_specs += [pl.BlockSpec((None, d, dff), lambda e: (e, 0, 0)),
                 pl.BlockSpec((None, d, dff), lambda e: (e, 0, 0)),
                 pl.BlockSpec((None, dff, d), lambda e: (e, 0, 0))]
    return pl.pallas_call(
        functools.partial(_ffn_kernel, caps=caps),
        out_shape=tuple(jax.ShapeDtypeStruct((n_exp * cap, d), BF16) for cap in caps),
        grid=(n_exp,),
        in_specs=in_specs,
        out_specs=tuple(pl.BlockSpec((cap, d), lambda e: (e, 0)) for cap in caps),
        compiler_params=_cparams(("arbitrary",)),
        name="ffn",
    )(*[g[0] for g in groups], w1, w3, w2)


def _combine_kernel(tbl_ref, ye_hbm, pos_ref, aff_ref, x1_ref, mod_ref, gpo_ref, y_ref, buf, sem, xbuf, xsem,
                    acc_scr, *, cap, blocks_per_tile):
    i = pl.program_id(0)
    nsteps = pl.num_programs(0)
    d = x1_ref.shape[1]
    lanes = pos_ref.shape[1]
    win = COMBINE_WINDOW
    last_start = ye_hbm.shape[0] - win
    slot = i % 2

    def first_row(step, e):
        return tbl_ref[step * blocks_per_tile, e] + e * cap

    def window_start(first, k):
        unclamped = (first // BF16_ROWS_PER_TILE) * BF16_ROWS_PER_TILE + k * win
        return unclamped, jnp.minimum(unclamped, last_start)

    def fetch(step, to_slot, e):
        start = window_start(first_row(step, e), 0)[1]
        return pltpu.make_async_copy(ye_hbm.at[pl.ds(pl.multiple_of(start, BF16_ROWS_PER_TILE), win), :],
                                     buf.at[to_slot, pl.ds(e * win, win), :], sem.at[to_slot, e])

    @pl.when(i == 0)
    def _prime():
        for e in range(N_EXPERTS):
            fetch(0, 0, e).start()

    @pl.when(i + 1 < nsteps)
    def _ahead():
        for e in range(N_EXPERTS):
            fetch(i + 1, 1 - slot, e).start()

    for e in range(N_EXPERTS):
        fetch(i, slot, e).wait()
    lane_row = lax.broadcasted_iota(I32, (1, win), 1)
    pieces = []
    for e in range(N_EXPERTS):
        pcol = pos_ref[:, e:e + 1]
        grow = jnp.where(pcol >= 0, pcol + e * cap, -1)
        start = window_start(first_row(i, e), 0)[1]
        pieces.append(jnp.where(grow == start + lane_row, aff_ref[:, e:e + 1], 0.0).astype(BF16))
    total = None
    for e0 in range(0, N_EXPERTS, COMBINE_EXPERT_GROUP):
        grp = jnp.concatenate(pieces[e0:e0 + COMBINE_EXPERT_GROUP], axis=1)
        part = _dot(grp, buf[slot, pl.ds(e0 * win, COMBINE_EXPERT_GROUP * win), :])
        total = part if total is None else total + part
    acc_scr[...] = total

    def extra_windows(e):
        covered = window_start(first_row(i, e), 1)[0]
        return jnp.maximum(first_row(i + 1, e) - covered + win - 1, 0) // win

    def expert_extra(e, carry):
        first = first_row(i, e)
        extra = extra_windows(e)

        def more(k, c):
            unclamped, start = window_start(first, k)
            cp = pltpu.make_async_copy(ye_hbm.at[pl.ds(pl.multiple_of(start, BF16_ROWS_PER_TILE), win), :],
                                       xbuf, xsem)
            cp.start()
            cp.wait()
            at_e = lax.broadcasted_iota(I32, (1, lanes), 1) == e
            pcol = jnp.sum(jnp.where(at_e, pos_ref[...].astype(F32), 0.0), axis=1, keepdims=True).astype(I32)
            wcol = jnp.sum(jnp.where(at_e, aff_ref[...], 0.0), axis=1, keepdims=True)
            grow = jnp.where(pcol >= 0, pcol + e * cap, -1)
            hit = (grow == start + lane_row) & (grow >= unclamped)
            acc_scr[...] += _dot(jnp.where(hit, wcol, 0.0).astype(BF16), xbuf[...])
            return c

        lax.fori_loop(1, 1 + extra, more, 0)
        return carry

    any_extra = extra_windows(0)
    for e in range(1, N_EXPERTS):
        any_extra = any_extra + extra_windows(e)

    @pl.when(any_extra > 0)
    def _overflow():
        lax.fori_loop(0, N_EXPERTS, expert_extra, 0)

    m = mod_ref[0]
    y_ref[...] = x1_ref[...] + m[:, 5 * d:6 * d] * _rms(acc_scr[...], gpo_ref[...])


def _combine_call(tbl, ye, pos, aff, x1, mod3, mod_row_of_tile, g_po, cap, tag):
    n, d = x1.shape
    tm = COMBINE_TILE
    lanes = pos.shape[1]
    grid_spec = pltpu.PrefetchScalarGridSpec(
        num_scalar_prefetch=1,
        grid=(n // tm,),
        in_specs=[pl.BlockSpec(memory_space=pl.ANY),
                  pl.BlockSpec((tm, lanes), lambda i, t: (i, 0)),
                  pl.BlockSpec((tm, lanes), lambda i, t: (i, 0)),
                  pl.BlockSpec((tm, d), lambda i, t: (i, 0)),
                  pl.BlockSpec((1, 1, 6 * d), lambda i, t: (mod_row_of_tile(i, tm), 0, 0)),
                  pl.BlockSpec((1, d), lambda i, t: (0, 0))],
        out_specs=pl.BlockSpec((tm, d), lambda i, t: (i, 0)),
        scratch_shapes=[pltpu.VMEM((2, N_EXPERTS * COMBINE_WINDOW, d), BF16),
                        pltpu.SemaphoreType.DMA((2, N_EXPERTS)),
                        pltpu.VMEM((COMBINE_WINDOW, d), BF16),
                        pltpu.SemaphoreType.DMA(()),
                        pltpu.VMEM((tm, d), F32)],
    )
    return pl.pallas_call(
        functools.partial(_combine_kernel, cap=cap, blocks_per_tile=tm // ROUTE_BLOCK),
        out_shape=jax.ShapeDtypeStruct((n, d), F32),
        grid_spec=grid_spec,
        compiler_params=_cparams(("arbitrary",)),
        name="combine_" + tag,
    )(tbl, ye, pos, aff, x1, mod3, g_po)


def _rope_tables(seq):
    rows = seq // GRID_W
    r = jnp.repeat(jnp.arange(rows), GRID_W).astype(F32)
    col = jnp.tile(jnp.arange(GRID_W), rows).astype(F32)
    pairs = HEAD_DIM // 4
    freqs = ROPE_THETA ** (-jnp.arange(pairs, dtype=F32) / pairs)
    ang = jnp.concatenate([r[:, None] * freqs, col[:, None] * freqs], axis=-1)
    cos = jnp.repeat(jnp.cos(ang), 2, axis=-1)
    sin = jnp.repeat(jnp.sin(ang), 2, axis=-1)
    even = (jnp.arange(HEAD_DIM) % 2) == 0
    return cos, jnp.where(even, -sin, 0.0), jnp.where(even, 0.0, sin)


def _trunk_to_routing(x, mod3, mod_row_of_tile, rope_tabs, ctx, lw, tag):
    (g_pre_mix, g_post_mix, g_pre_ffn, g_post_ffn, w_in_p, g_q, g_k, wgf, bgf, wgb, bgb, g_gla,
     w_pa, w_pg, w_out, w_r, w1, w3, w2) = lw
    batch, seq, d = x.shape
    n = batch * seq
    x2 = x.reshape(n, d)
    (q_a, k_a, v_a, q_g, k_g, v_g, r_g, lg_f, lg_b, gates) = _inproj_call(
        x2, mod3, mod_row_of_tile, g_pre_mix, w_in_p, g_q, g_k, wgf, bgf, wgb, bgb, rope_tabs, batch, seq)
    if ctx is None:
        o_a = _attn_call(q_a, k_a, v_a, None, None)
        o_g, s_f, s_b = _gla_call(q_g, k_g, v_g, lg_f, lg_b, r_g, g_gla, None, None, batch, seq, True,
                                  GLA_CTX_HEADS_PER_STEP, GLA_CTX_SEQS_PER_STEP)
    else:
        ck, cv, s_f0, s_b0 = ctx
        o_a = _attn_call(q_a, k_a, v_a, ck, cv)
        (o_g,) = _gla_call(q_g, k_g, v_g, lg_f, lg_b, r_g, g_gla, s_f0, s_b0, batch, seq, False,
                           GLA_LAT_HEADS_PER_STEP, 1)
        s_f = s_b = None
    x1, h, aff = _outproj_call(o_a, o_g, gates, x2, mod3, mod_row_of_tile, w_pa, w_pg, w_out,
                               g_post_mix, g_pre_ffn, w_r, tag)
    cap = (EC_CAPACITY_FACTOR * n) // N_EXPERTS
    pos, post, tbl = _route_call(aff, cap, tag)
    xs = _sc_gather_call(h, post, cap)
    return dict(x1=x1, xs=xs, aff=aff, pos=pos, tbl=tbl, cap=cap,
                mod_row_of_tile=mod_row_of_tile, tag=tag, shape=(batch, seq, d)), (k_a, v_a, s_f, s_b)


def _expert_ffn(groups, mod3, g_post_ffn, w1, w3, w2):
    yes = _ffn_call([(g["xs"], g["cap"]) for g in groups], w1, w3, w2)
    outs = []
    for g, ye in zip(groups, yes):
        y = _combine_call(g["tbl"], ye, g["pos"], g["aff"], g["x1"], mod3, g["mod_row_of_tile"], g_post_ffn,
                          g["cap"], g["tag"])
        outs.append(y.reshape(g["shape"]))
    return outs


def kernel(x_prompt, x_sample, cache_k, cache_v, state_gla_fwd, state_gla_bwd, c, c_ctx, g_pre_mix, g_post_mix, g_pre_ffn, g_post_ffn, w_mod, b_mod, w_in, g_q, g_k, w_gk2_f, b_gk_f, w_gk2_b, b_gk_b, g_gla, w_pa, w_pg, w_out, w_router, w1, w3, w2):
    depth = w_in.shape[0]
    assert depth == 1, "single trunk layer"
    d = x_prompt.shape[-1]
    dec_batch, dec_seq, _ = x_sample.shape
    assert dec_batch + 1 <= MOD_ROWS
    l = 0
    rank = GLA_GATE_RANK
    w_in_p = jnp.swapaxes(w_in[l], 0, 1)
    gkw = w_gk2_f.shape[-1]
    wgf = jnp.zeros((V7X_LANES, gkw), F32).at[0:rank].set(w_gk2_f[l]).astype(BF16)
    wgb = jnp.zeros((V7X_LANES, gkw), F32).at[rank:2 * rank].set(w_gk2_b[l]).astype(BF16)
    w_r = jnp.zeros((d, V7X_LANES), F32).at[:, :N_EXPERTS].set(w_router[l]).astype(BF16)
    row = lambda a: a[l].reshape(1, -1)
    lw = (row(g_pre_mix), row(g_post_mix), row(g_pre_ffn), row(g_post_ffn), w_in_p, row(g_q), row(g_k),
          wgf, row(b_gk_f), wgb, row(b_gk_b), row(g_gla),
          w_pa[l].astype(BF16), w_pg[l].astype(BF16), w_out[l].astype(BF16), w_r, w1[l], w3[l], w2[l])

    cc = jnp.concatenate([c_ctx[None, :], c, jnp.zeros((MOD_ROWS - 1 - dec_batch, d), F32)], axis=0)
    mod = _mod_call(cc, w_mod[l], b_mod[l].reshape(1, -1))
    mod3 = mod.reshape(MOD_ROWS, 1, 6 * d)

    gp, (nk, nv, nsf, nsb) = _trunk_to_routing(x_prompt, mod3, lambda i, tm: 0, None, None, lw, "ctx")
    ctx = (cache_k, cache_v, state_gla_fwd, state_gla_bwd)
    gs, _ = _trunk_to_routing(x_sample, mod3, lambda i, tm: 1 + (i * tm) // dec_seq, _rope_tables(dec_seq),
                              ctx, lw, "lat")
    yp, ys = _expert_ffn([gp, gs], mod3, lw[3], lw[16], lw[17], lw[18])
    return (yp, ys, nk, nv, nsf, nsb)
```

```python
import functools

import jax
import jax.numpy as jnp
from jax import lax
from jax.experimental import pallas as pl
from jax.experimental.pallas import tpu as pltpu
from jax.experimental.pallas import tpu_sc as plsc

F32 = jnp.float32
BF16 = jnp.bfloat16
I32 = jnp.int32

N_HEADS = 8
N_KV_HEADS = 2
HEAD_DIM = 128
GRID_W = 64
ROPE_THETA = 10000.0
GLA_HEADS = 4
GLA_GATE_RANK = 16
GLA_GATE_NORM = 16.0
GLA_CHUNK = 64
N_EXPERTS = 16
EC_CAPACITY_FACTOR = 2
EPS = 1e-6

V7X_LANES = 128
V7X_VMEM_BYTES = 64 * 1024 * 1024
V7X_VMEM_RESERVE_BYTES = 6 * 1024 * 1024
BF16_ROWS_PER_TILE = 16
V7X_SC_CORES = 2
V7X_SC_SUBCORES = 16
V7X_SC_LANES = 16

TOKEN_TILE = 512
OUTPROJ_TILE = 512
OUTPROJ_ROW_GROUPS = 4
ATTN_Q_TILE = 256
ATTN_SEQS_PER_STEP = 4
GLA_BLOCK = 256
GLA_HEADS_PER_STAGE_GROUP = 2
GLA_CTX_HEADS_PER_STEP = 4
GLA_CTX_SEQS_PER_STEP = 4
GLA_LAT_HEADS_PER_STEP = 2
ROUTE_BLOCK = 256
SC_GATHER_CHUNK = 128
COMBINE_TILE = 512
COMBINE_WINDOW = 128
COMBINE_EXPERT_GROUP = 2
MOD_ROWS = 8
MOD_STEPS = 12
MOD_WEIGHT_SLABS = 11


def _cparams(semantics):
    return pltpu.CompilerParams(dimension_semantics=semantics,
                                vmem_limit_bytes=V7X_VMEM_BYTES - V7X_VMEM_RESERVE_BYTES)


def _sigmoid(x):
    return 0.5 * (jnp.tanh(0.5 * x) + 1.0)


def _silu(x):
    return x * _sigmoid(x)


def _log_sigmoid(x):
    return jnp.minimum(x, 0.0) - jnp.log1p(jnp.exp(-jnp.abs(x)))


def _rms(x, g):
    ms = jnp.mean(x * x, axis=-1, keepdims=True)
    return x * lax.rsqrt(ms + EPS) * g


def _dot(a, b):
    return jnp.dot(a, b, preferred_element_type=F32)


def _dot_nt(a, b):
    return lax.dot_general(a, b, (((1,), (1,)), ((), ())), preferred_element_type=F32)


def _mod_kernel(c_ref, w_ref, b_ref, win_ref, o_ref, wout_ref):
    s = _silu(c_ref[...]).astype(BF16)
    o_ref[...] = _dot(s, w_ref[...].astype(BF16)) + b_ref[...]
    wout_ref[...] = win_ref[...].astype(BF16)


def _mod_call(cc, w_mod, b_mod, w_in_t):
    d, n6 = w_mod.shape
    din = w_in_t.shape[0]
    steps = MOD_STEPS
    tn = n6 // steps
    slabs = MOD_WEIGHT_SLABS
    rows = din // slabs
    assert tn * steps == n6 and slabs * rows == din and rows % BF16_ROWS_PER_TILE == 0 and slabs <= steps
    slab = lambda j: (jnp.minimum(j, slabs - 1), 0)
    return pl.pallas_call(
        _mod_kernel,
        out_shape=(jax.ShapeDtypeStruct((MOD_ROWS, n6), F32), jax.ShapeDtypeStruct((din, d), BF16)),
        grid=(steps,),
        in_specs=[pl.BlockSpec((MOD_ROWS, d), lambda j: (0, 0)),
                  pl.BlockSpec((d, tn), lambda j: (0, j)),
                  pl.BlockSpec((1, tn), lambda j: (0, j)),
                  pl.BlockSpec((rows, d), slab)],
        out_specs=(pl.BlockSpec((MOD_ROWS, tn), lambda j: (0, j)), pl.BlockSpec((rows, d), slab)),
        compiler_params=_cparams(("arbitrary",)),
        name="mod",
    )(cc, w_mod, b_mod, w_in_t)


def _inproj_layout(d):
    aq, akv = N_HEADS * HEAD_DIM, N_KV_HEADS * HEAD_DIM
    gk, gv = d // 2, d
    names = ("q_a", "k_a", "v_a", "q_g", "k_g", "v_g", "r_g", "gk_f", "gk_b", "gates")
    widths = (aq, akv, akv, gk, gk, gv, gv, GLA_GATE_RANK, GLA_GATE_RANK, 2 * d)
    off, o = {}, 0
    for nme, w in zip(names, widths):
        off[nme] = (o, o + w)
        o += w
    off["gk"] = (off["gk_f"][0], off["gk_f"][0] + V7X_LANES)
    return off, o


def _inproj_kernel(*refs, rope, d):
    if rope:
        (x_ref, mod_ref, gpre_ref, w_ref, gq_ref, gk_ref, wgf_ref, bgf_ref, wgb_ref, bgb_ref,
         cos_ref, se_ref, so_ref, *outs) = refs
    else:
        (x_ref, mod_ref, gpre_ref, w_ref, gq_ref, gk_ref, wgf_ref, bgf_ref, wgb_ref, bgb_ref,
         *outs) = refs
    qa_ref, k_ref, v_ref, qg_ref, kg_ref, vg_ref, rg_ref, lgf_ref, lgb_ref, gate_ref = outs
    off, _ = _inproj_layout(d)
    m = mod_ref[0]
    h = _rms(x_ref[...], gpre_ref[...]) * (1.0 + m[:, d:2 * d]) + m[:, 0:d]
    hb = h.astype(BF16)

    def proj(name):
        a, b = off[name]
        return _dot_nt(hb, w_ref[a:b, :])

    def qk_norm(y, g_ref):
        y = _rms(y, g_ref[...])
        if rope:
            nxt = pltpu.roll(y, HEAD_DIM - 1, axis=1)
            prv = pltpu.roll(y, 1, axis=1)
            y = y * cos_ref[...] + nxt * se_ref[...] + prv * so_ref[...]
        return y

    gk = proj("gk").astype(BF16)

    q = proj("q_a")
    scale = HEAD_DIM ** -0.5
    for hd in range(N_HEADS):
        sl = slice(hd * HEAD_DIM, (hd + 1) * HEAD_DIM)
        qa_ref[:, sl] = (qk_norm(q[:, sl], gq_ref) * scale).astype(BF16)

    k = proj("k_a")
    v = proj("v_a")
    tb, _, _, ts, _ = k_ref.shape
    for kv in range(N_KV_HEADS):
        sl = slice(kv * HEAD_DIM, (kv + 1) * HEAD_DIM)
        k_ref[:, 0, kv] = qk_norm(k[:, sl], gk_ref).reshape(tb, ts, HEAD_DIM)
        v_ref[:, 0, kv] = v[:, sl].reshape(tb, ts, HEAD_DIM)

    dk = (d // 2) // GLA_HEADS
    qg_ref[...] = (proj("q_g") * (dk ** -0.5)).astype(BF16)
    kg_ref[...] = proj("k_g").astype(BF16)
    vg_ref[...] = proj("v_g").astype(BF16)
    rg_ref[...] = _silu(proj("r_g")).astype(BF16)

    gate_ref[...] = _sigmoid(proj("gates")).astype(BF16)

    lgf_ref[...] = _log_sigmoid(_dot(gk, wgf_ref[...]) + bgf_ref[...]) * (1.0 / GLA_GATE_NORM)
    lgb_ref[...] = _log_sigmoid(_dot(gk, wgb_ref[...]) + bgb_ref[...]) * (1.0 / GLA_GATE_NORM)


def _inproj_call(x2, mod3, mod_row_of_tile, g_pre, w_in_p, g_q, g_k, wgf, bgf, wgb, bgb, rope_tabs,
                 batch, seq):
    n, d = x2.shape
    tm = TOKEN_TILE
    _, dinp = _inproj_layout(d)
    rope = rope_tabs is not None
    gk_w = d // 2
    if seq >= tm:
        tb, ts, per = 1, tm, seq // tm
        kv_map = lambda i: (i // per, 0, 0, i % per, 0)
    else:
        tb, ts, per = tm // seq, seq, 1
        kv_map = lambda i: (i, 0, 0, 0, 0)
    row = lambda i: (i, 0)
    const = lambda i: (0, 0)
    in_specs = [
        pl.BlockSpec((tm, d), row),
        pl.BlockSpec((1, 1, 6 * d), lambda i: (mod_row_of_tile(i, tm), 0, 0)),
        pl.BlockSpec((1, d), const),
        pl.BlockSpec((dinp, d), const, pipeline_mode=pl.Buffered(1)),
        pl.BlockSpec((1, HEAD_DIM), const),
        pl.BlockSpec((1, HEAD_DIM), const),
        pl.BlockSpec((V7X_LANES, gk_w), const),
        pl.BlockSpec((1, gk_w), const),
        pl.BlockSpec((V7X_LANES, gk_w), const),
        pl.BlockSpec((1, gk_w), const),
    ]
    args = [x2, mod3, g_pre, w_in_p, g_q, g_k, wgf, bgf, wgb, bgb]
    if rope:
        tab = pl.BlockSpec((tm, HEAD_DIM), lambda i: (i % per, 0))
        in_specs += [tab, tab, tab]
        args += list(rope_tabs)
    kv_shape = jax.ShapeDtypeStruct((batch, 1, N_KV_HEADS, seq, HEAD_DIM), F32)
    kv_spec = pl.BlockSpec((tb, 1, N_KV_HEADS, ts, HEAD_DIM), kv_map)
    out_shape = (
        jax.ShapeDtypeStruct((n, N_HEADS * HEAD_DIM), BF16), kv_shape, kv_shape,
        jax.ShapeDtypeStruct((n, gk_w), BF16), jax.ShapeDtypeStruct((n, gk_w), BF16),
        jax.ShapeDtypeStruct((n, d), BF16), jax.ShapeDtypeStruct((n, d), BF16),
        jax.ShapeDtypeStruct((n, gk_w), F32), jax.ShapeDtypeStruct((n, gk_w), F32),
        jax.ShapeDtypeStruct((n, 2 * d), BF16),
    )
    out_specs = (
        pl.BlockSpec((tm, N_HEADS * HEAD_DIM), row), kv_spec, kv_spec,
        pl.BlockSpec((tm, gk_w), row), pl.BlockSpec((tm, gk_w), row),
        pl.BlockSpec((tm, d), row), pl.BlockSpec((tm, d), row),
        pl.BlockSpec((tm, gk_w), row), pl.BlockSpec((tm, gk_w), row),
        pl.BlockSpec((tm, 2 * d), row),
    )
    return pl.pallas_call(
        functools.partial(_inproj_kernel, rope=rope, d=d),
        out_shape=out_shape,
        grid=(n // tm,),
        in_specs=in_specs,
        out_specs=out_specs,
        compiler_params=_cparams(("parallel",)),
        name="inproj_lat" if rope else "inproj_ctx",
    )(*args)


def _attn_kernel(*refs, cached):
    if cached:
        q_ref, k_ref, v_ref, ck_ref, cv_ref, o_ref = refs
    else:
        q_ref, k_ref, v_ref, o_ref = refs
    seqs = k_ref.shape[0]
    tq = q_ref.shape[0] // seqs
    grp = N_HEADS // N_KV_HEADS
    chains = [(sq, kv) for sq in range(seqs) for kv in range(N_KV_HEADS)]

    def scores_of(sq, kv):
        rows = pl.ds(sq * tq, tq)
        kk = k_ref[sq, 0, kv].astype(BF16)
        vv = v_ref[sq, 0, kv].astype(BF16)
        if cached:
            kk = jnp.concatenate([ck_ref[sq, 0, kv].astype(BF16), kk], axis=0)
            vv = jnp.concatenate([cv_ref[sq, 0, kv].astype(BF16), vv], axis=0)
        heads = [q_ref[rows, (kv * grp + g) * HEAD_DIM:(kv * grp + g + 1) * HEAD_DIM] for g in range(grp)]
        return _dot_nt(jnp.concatenate(heads, axis=0), kk), vv

    look_ahead = cached
    ahead = scores_of(*chains[0]) if look_ahead else None
    for n, (sq, kv) in enumerate(chains):
        s, vv = ahead if look_ahead else scores_of(sq, kv)
        if look_ahead and n + 1 < len(chains):
            ahead = scores_of(*chains[n + 1])
        rows = pl.ds(sq * tq, tq)
        p = jnp.exp(s - jnp.max(s, axis=-1, keepdims=True))
        l = jnp.sum(p, axis=-1, keepdims=True)
        o = _dot(p.astype(BF16), vv) / l
        for g in range(grp):
            hd = kv * grp + g
            o_ref[rows, hd * HEAD_DIM:(hd + 1) * HEAD_DIM] = o[g * tq:(g + 1) * tq].astype(BF16)


def _attn_call(q_a, k_a, v_a, cache_k, cache_v):
    batch, _, _, seq, _ = k_a.shape
    n, aq = q_a.shape
    tq = min(ATTN_Q_TILE, seq)
    per = seq // tq
    cached = cache_k is not None
    seqs = ATTN_SEQS_PER_STEP if (per == 1 and not cached) else 1
    batch = batch // seqs
    own = pl.BlockSpec((seqs, 1, N_KV_HEADS, seq, HEAD_DIM), lambda b, j: (b, 0, 0, 0, 0))
    tq = tq * seqs
    in_specs = [pl.BlockSpec((tq, aq), lambda b, j: (b * per + j, 0)), own, own]
    args = [q_a, k_a, v_a]
    if cached:
        past = cache_k.shape[3]
        cspec = pl.BlockSpec((1, 1, N_KV_HEADS, past, HEAD_DIM), lambda b, j: (b, 0, 0, 0, 0))
        in_specs += [cspec, cspec]
        args += [cache_k, cache_v]
    return pl.pallas_call(
        functools.partial(_attn_kernel, cached=cached),
        out_shape=jax.ShapeDtypeStruct((n, aq), BF16),
        grid=(batch, per),
        in_specs=in_specs,
        out_specs=pl.BlockSpec((tq, aq), lambda b, j: (b * per + j, 0)),
        compiler_params=_cparams(("parallel", "parallel")),
        name="attn_lat" if cached else "attn_ctx",
    )(*args)


def _gla_kernel(*refs, nblk, heads, seqs, has_state, emit_state):
    refs = list(refs)
    q_ref, k_ref, v_ref, lgf_ref, lgb_ref, rg_ref, gg_ref = refs[:7]
    pos = 7
    if has_state:
        s0f_ref, s0b_ref = refs[pos:pos + 2]
        pos += 2
    og_ref = refs[pos]
    pos += 1
    if emit_state:
        sf_ref, sb_ref = refs[pos:pos + 2]
        pos += 2
    of_scr, ob_scr = refs[pos:pos + 2]

    blk = GLA_BLOCK
    ch = GLA_CHUNK
    nch = blk // ch
    dk = q_ref.shape[1] // heads
    dv = v_ref.shape[1] // heads
    shift = ch.bit_length() - 1
    row_in_chunk = lax.broadcasted_iota(I32, (blk, dk), 0) & (ch - 1)
    ri = lax.broadcasted_iota(I32, (blk, blk), 0)
    ci = lax.broadcasted_iota(I32, (blk, blk), 1)
    same = (ri >> shift) == (ci >> shift)
    mask_f = same & (ci <= ri)
    mask_b = same & (ci >= ri)

    def one_block(b0, hd, reverse):
        rows = pl.ds(b0, blk)
        kcols = slice(hd * dk, (hd + 1) * dk)
        q = q_ref[rows, kcols].astype(F32)
        k = k_ref[rows, kcols].astype(F32)
        v = v_ref[rows, hd * dv:(hd + 1) * dv]
        b = (lgb_ref if reverse else lgf_ref)[rows, kcols]
        mask = mask_b if reverse else mask_f
        s = 1
        while s < ch:
            if reverse:
                sh = pltpu.roll(b, blk - s, axis=0)
                b = b + jnp.where(row_in_chunk < ch - s, sh, 0.0)
            else:
                sh = pltpu.roll(b, s, axis=0)
                b = b + jnp.where(row_in_chunk >= s, sh, 0.0)
            s *= 2
        qe = (q * jnp.exp(b)).astype(BF16)
        ke = (k * jnp.exp(-b)).astype(BF16)
        scores = _dot_nt(qe, ke)
        end_row = [c * ch + (0 if reverse else ch - 1) for c in range(nch)]
        ends = [b[r:r + 1, :] for r in end_row]
        b_end = jnp.concatenate([jnp.broadcast_to(e, (ch, dk)) for e in ends], axis=0)
        kd = (k * jnp.exp(b_end - b)).astype(BF16)
        return dict(scores=scores, mask=mask, qe=qe, kd=kd, v=v, ends=ends, reverse=reverse)

    def chunk_products(w):
        return [lax.dot_general(w["kd"][c * ch:(c + 1) * ch], w["v"][c * ch:(c + 1) * ch],
                                (((0,), (0,)), ((), ())), preferred_element_type=F32) for c in range(nch)]

    def intra_chunk(w):
        return _dot(jnp.where(w["mask"], w["scores"], 0.0).astype(BF16), w["v"])

    def across_chunks(w, kv, state):
        decay = jnp.exp(jnp.concatenate(w["ends"] + [jnp.zeros((dk - nch, dk), F32)], axis=0)).T
        inter = [None] * nch
        for c in (range(nch - 1, -1, -1) if w["reverse"] else range(nch)):
            if state is None:
                inter[c] = jnp.zeros((ch, dv), F32)
                state = kv[c]
            else:
                inter[c] = _dot(w["qe"][c * ch:(c + 1) * ch], state.astype(BF16))
                state = decay[:, c:c + 1] * state + kv[c]
        return jnp.concatenate(inter, axis=0), state

    group = min(GLA_HEADS_PER_STAGE_GROUP, heads)
    for sq in range(seqs):
        srows = pl.ds(sq * nblk * blk, nblk * blk)
        for h0 in range(0, heads, group):
            hds = list(range(h0, h0 + group))
            states = {(hd, rev): ((s0b_ref if rev else s0f_ref)[sq, 0, hd] if has_state else None)
                      for hd in hds for rev in (False, True)}
            for i in range(nblk):
                block_of = {False: sq * nblk + i, True: sq * nblk + nblk - 1 - i}
                chains = [(hd, rev) for hd in hds for rev in (False, True)]
                work = {c: one_block(block_of[c[1]] * blk, c[0], c[1]) for c in chains}
                kvs = {c: chunk_products(work[c]) for c in chains}
                intra = {c: intra_chunk(work[c]) for c in chains}
                for c in chains:
                    hd, rev = c
                    inter, states[c] = across_chunks(work[c], kvs[c], states[c])
                    scr = ob_scr if rev else of_scr
                    scr[pl.ds(block_of[rev] * blk, blk), hd * dv:(hd + 1) * dv] = intra[c] + inter
            for hd in hds:
                vcols = slice(hd * dv, (hd + 1) * dv)
                if emit_state:
                    sf_ref[sq, 0, hd] = states[(hd, False)]
                    sb_ref[sq, 0, hd] = states[(hd, True)]
                o = of_scr[srows, vcols] + ob_scr[srows, vcols]
                og_ref[srows, vcols] = (_rms(o, gg_ref[...]) * rg_ref[srows, vcols].astype(F32)).astype(BF16)


def _gla_call(q_g, k_g, v_g, lg_f, lg_b, r_g, g_gla, state_f, state_b, batch, seq, emit_state, heads, seqs):
    n, gkw = q_g.shape
    d = v_g.shape[1]
    dk, dv = gkw // GLA_HEADS, d // GLA_HEADS
    has_state = state_f is not None
    nblk = seq // GLA_BLOCK
    batch = batch // seqs
    seq = seq * seqs
    kspec = pl.BlockSpec((seq, heads * dk), lambda b, h: (b, h))
    vspec = pl.BlockSpec((seq, heads * dv), lambda b, h: (b, h))
    sspec = pl.BlockSpec((seqs, 1, heads, dk, dv), lambda b, h: (b, 0, h, 0, 0))
    in_specs = [kspec, kspec, vspec, kspec, kspec, vspec, pl.BlockSpec((1, dv), lambda b, h: (0, 0))]
    args = [q_g, k_g, v_g, lg_f, lg_b, r_g, g_gla]
    if has_state:
        in_specs += [sspec, sspec]
        args += [state_f, state_b]
    out_shape = [jax.ShapeDtypeStruct((n, d), BF16)]
    out_specs = [vspec]
    if emit_state:
        st = jax.ShapeDtypeStruct((batch * seqs, 1, GLA_HEADS, dk, dv), F32)
        out_shape += [st, st]
        out_specs += [sspec, sspec]
    return pl.pallas_call(
        functools.partial(_gla_kernel, nblk=nblk, heads=heads, seqs=seqs, has_state=has_state,
                          emit_state=emit_state),
        out_shape=tuple(out_shape),
        grid=(batch, GLA_HEADS // heads),
        in_specs=in_specs,
        out_specs=tuple(out_specs),
        scratch_shapes=[pltpu.VMEM((seq, heads * dv), F32), pltpu.VMEM((seq, heads * dv), F32)],
        compiler_params=_cparams(("parallel", "parallel")),
        name="gla_ctx" if emit_state else "gla_lat",
    )(*args)


def _outproj_kernel(oa_ref, og_ref, gate_ref, x_ref, mod_ref, wpa_ref, wpg_ref, wout_ref, gpm_ref,
                    gpf_ref, wr_ref, x1_ref, h_ref, aff_ref):
    d = x_ref.shape[1]
    m = mod_ref[0]
    sub = x_ref.shape[0] // OUTPROJ_ROW_GROUPS
    groups = [pl.ds(g * sub, sub) for g in range(OUTPROJ_ROW_GROUPS)]
    branch = [(_dot(oa_ref[r, :], wpa_ref[...]), _dot(og_ref[r, :], wpg_ref[...])) for r in groups]
    mo = jnp.concatenate(
        [_dot((gate_ref[r, 0:d].astype(F32) * oa + gate_ref[r, d:2 * d].astype(F32) * og).astype(BF16),
              wout_ref[...]) for r, (oa, og) in zip(groups, branch)], axis=0)
    x1 = x_ref[...] + m[:, 2 * d:3 * d] * _rms(mo, gpm_ref[...])
    x1_ref[...] = x1
    hb = (_rms(x1, gpf_ref[...]) * (1.0 + m[:, 4 * d:5 * d]) + m[:, 3 * d:4 * d]).astype(BF16)
    bits = lax.bitcast_convert_type(hb.astype(F32), jnp.uint32)
    packed = (bits[:, 0:d // 2] >> 16) | (bits[:, d // 2:d] & jnp.uint32(0xFFFF0000))
    h_ref[...] = lax.bitcast_convert_type(packed, I32)
    logits = _dot(hb, wr_ref[...])
    valid = lax.broadcasted_iota(I32, logits.shape, 1) < N_EXPERTS
    mx = jnp.max(jnp.where(valid, logits, -jnp.inf), axis=-1, keepdims=True)
    ex = jnp.where(valid, jnp.exp(logits - mx), 0.0)
    aff_ref[...] = ex / jnp.sum(ex, axis=-1, keepdims=True)


def _outproj_call(o_a, o_g, gates, x2, mod3, mod_row_of_tile, w_pa, w_pg, w_out, g_pm, g_pf, w_r, tag):
    n, d = x2.shape
    tm = OUTPROJ_TILE
    row = lambda i: (i, 0)
    const = lambda i: (0, 0)
    wspec = pl.BlockSpec((d, d), const, pipeline_mode=pl.Buffered(1))
    return pl.pallas_call(
        _outproj_kernel,
        out_shape=(jax.ShapeDtypeStruct((n, d), F32), jax.ShapeDtypeStruct((n, d // 2), I32),
                   jax.ShapeDtypeStruct((n, V7X_LANES), F32)),
        grid=(n // tm,),
        in_specs=[pl.BlockSpec((tm, d), row), pl.BlockSpec((tm, d), row), pl.BlockSpec((tm, 2 * d), row),
                  pl.BlockSpec((tm, d), row),
                  pl.BlockSpec((1, 1, 6 * d), lambda i: (mod_row_of_tile(i, tm), 0, 0)),
                  wspec, wspec, wspec, pl.BlockSpec((1, d), const), pl.BlockSpec((1, d), const),
                  pl.BlockSpec((d, V7X_LANES), const)],
        out_specs=(pl.BlockSpec((tm, d), row), pl.BlockSpec((tm, d // 2), row),
                   pl.BlockSpec((tm, V7X_LANES), row)),
        compiler_params=_cparams(("parallel",)),
        name="outproj_" + tag,
    )(o_a, o_g, gates, x2, mod3, w_pa, w_pg, w_out, g_pm, g_pf, w_r)


def _route_kernel(aff_ref, pos_ref, post_ref, tbl_ref, afft_scr, *, cap):
    n = aff_ref.shape[0]
    rb = ROUTE_BLOCK
    nb = n // rb
    lanes = aff_ref.shape[1]
    lane = lax.broadcasted_iota(I32, (1, lanes), 1)
    expert_lane = lane < N_EXPERTS
    tbl_ref[...] = jnp.zeros(tbl_ref.shape, I32)

    def to_token_lanes(c, carry):
        start = pl.multiple_of(c * rb, rb)
        afft_scr[c] = aff_ref[pl.ds(start, rb), :].T[0:N_EXPERTS, :]
        return carry

    lax.fori_loop(0, nb, to_token_lanes, 0)
    aff_t = afft_scr[...]

    def count(hit):
        return jnp.sum(jnp.sum(hit.astype(I32), axis=0), axis=1, keepdims=True)

    def bit_step(i, lo):
        t = lo | jnp.left_shift(jnp.int32(1), 30 - i)
        ge = aff_t >= lax.bitcast_convert_type(t, F32)[None]
        return jnp.where(count(ge) >= cap, t, lo)

    thr_bits = lax.fori_loop(0, 31, bit_step, jnp.zeros((N_EXPERTS, 1), I32))
    need_t = cap - count(aff_t > lax.bitcast_convert_type(thr_bits, F32)[None])

    def to_expert_lanes(col):
        full = jnp.concatenate([jnp.broadcast_to(col, (N_EXPERTS, lanes)),
                                jnp.zeros((lanes - N_EXPERTS, lanes), I32)], axis=0)
        return full.T[0:1, :]

    thr = lax.bitcast_convert_type(to_expert_lanes(thr_bits), F32)
    need = to_expert_lanes(need_t).astype(F32)
    capf = float(cap)

    r = lax.broadcasted_iota(I32, (rb, rb), 0)
    c_ = lax.broadcasted_iota(I32, (rb, rb), 1)
    tril = jnp.where(c_ <= r, 1.0, 0.0).astype(BF16)

    def blk_step(c, carry):
        eq_before, raw_before = carry
        start = pl.multiple_of(c * rb, rb)
        a = aff_ref[pl.ds(start, rb), :]
        gt = a > thr
        eq = a == thr
        eq_incl = _dot(tril, jnp.where(eq, 1.0, 0.0).astype(BF16)) + eq_before
        raw = (gt | (eq & (eq_incl <= need))) & expert_lane
        raw_incl = _dot(tril, jnp.where(raw, 1.0, 0.0).astype(BF16)) + raw_before
        sel = raw & (raw_incl <= capf)
        self_ = jnp.where(sel, 1.0, 0.0)
        incl = jnp.minimum(raw_incl, capf)
        sel_before = jnp.minimum(raw_before, capf)
        excl = incl - self_
        posb = jnp.where(sel, excl, -1.0).astype(I32)
        pos_ref[pl.ds(start, rb), :] = posb
        post_ref[:, pl.ds(start, rb)] = posb.T[0:N_EXPERTS, :]
        tbl_ref[pl.ds(c, 1), :] = sel_before.astype(I32)
        return (eq_incl[rb - 1:rb, :], raw_incl[rb - 1:rb, :])

    zero = jnp.zeros((1, lanes), F32)
    _, total = lax.fori_loop(0, nb, blk_step, (zero, zero), unroll=2)
    tbl_ref[pl.ds(nb, 1), :] = jnp.minimum(total, capf).astype(I32)


def _route_call(aff, cap, tag):
    n, lanes = aff.shape
    nb = n // ROUTE_BLOCK
    tbl_rows = -(-(nb + 1) // 8) * 8
    full = lambda *shape: pl.BlockSpec(shape, lambda: tuple(0 for _ in shape))
    return pl.pallas_call(
        functools.partial(_route_kernel, cap=cap),
        out_shape=(jax.ShapeDtypeStruct((n, lanes), I32),
                   jax.ShapeDtypeStruct((N_EXPERTS, n), I32),
                   jax.ShapeDtypeStruct((tbl_rows, lanes), I32)),
        in_specs=[full(n, lanes)],
        out_specs=(full(n, lanes), full(N_EXPERTS, n), full(tbl_rows, lanes)),
        scratch_shapes=[pltpu.VMEM((nb, N_EXPERTS, ROUTE_BLOCK), F32)],
        compiler_params=_cparams(()),
        name="route_" + tag,
    )(aff)


def _sc_gather_call(table, post, cap):
    n_exp, n = post.shape
    words = table.shape[1]
    workers = V7X_SC_CORES * V7X_SC_SUBCORES
    parts = workers // n_exp
    chunk = SC_GATHER_CHUNK
    lanes = V7X_SC_LANES
    per_part = cap // parts
    assert parts * n_exp == workers and per_part % chunk == 0 and n % lanes == 0
    mesh = plsc.VectorSubcoreMesh(core_axis_name="c", subcore_axis_name="s",
                                  num_cores=V7X_SC_CORES, num_subcores=V7X_SC_SUBCORES)

    def body(table_hbm, post_hbm, out_hbm, pos_v, idx_v, rows_v, sem):
        wid = lax.axis_index("s") * V7X_SC_CORES + lax.axis_index("c")
        e = wid // parts
        part = wid % parts
        pltpu.sync_copy(post_hbm.at[e], pos_v)
        lane = lax.iota(I32, lanes)

        @pl.loop(0, n, step=lanes)
        def _(t0):
            p = pos_v[pl.ds(t0, lanes)]
            plsc.store_scatter(idx_v, [p], lane + t0, mask=p >= 0)

        @pl.loop(0, per_part // chunk)
        def _(j):
            off = pl.multiple_of(part * per_part + j * chunk, chunk)
            pltpu.async_copy(table_hbm.at[idx_v.at[pl.ds(off, chunk)]], rows_v, sem).wait()
            pltpu.sync_copy(rows_v, out_hbm.at[pl.ds(e * cap + off, chunk)])

    return pl.kernel(
        body,
        out_type=jax.ShapeDtypeStruct((n_exp * cap, words), table.dtype),
        mesh=mesh,
        scratch_types=[pltpu.VMEM((n,), I32), pltpu.VMEM((cap,), I32),
                       pltpu.VMEM((chunk, words), table.dtype), pltpu.SemaphoreType.DMA],
        compiler_params=pltpu.CompilerParams(needs_layout_passes=False),
        name="sc_gather",
    )(table, post)


def _ffn_kernel(*refs, caps):
    ng = len(caps)
    xs_refs = refs[:ng]
    w1_ref, w3_ref, w2_ref = refs[ng:ng + 3]
    ye_refs = refs[ng + 3:2 * ng + 3]
    row_off = [sum(caps[:g]) for g in range(ng)]

    def unpack(words):
        w = lax.bitcast_convert_type(words, jnp.uint32)
        lo = lax.bitcast_convert_type(w << 16, F32).astype(BF16)
        hi = lax.bitcast_convert_type(w & jnp.uint32(0xFFFF0000), F32).astype(BF16)
        return jnp.concatenate([lo, hi], axis=1)

    xs = jnp.concatenate([unpack(r[...]) for r in xs_refs], axis=0)
    hid = _silu(_dot(xs, w1_ref[...].astype(BF16))) * _dot(xs, w3_ref[...].astype(BF16))
    ye = _dot(hid.astype(BF16), w2_ref[...].astype(BF16)).astype(BF16)
    for g in range(ng):
        ye_refs[g][...] = ye[row_off[g]:row_off[g] + caps[g]]


def _ffn_call(groups, w1, w3, w2):
    caps = tuple(g[1] for g in groups)
    n_exp, d, dff = w1.shape
    in_specs = [pl.BlockSpec((cap, d // 2), lambda e: (e, 0)) for cap in caps]
    in_specs += [pl.BlockSpec((None, d, dff), lambda e: (e, 0, 0)),
                 pl.BlockSpec((None, d, dff), lambda e: (e, 0, 0)),
                 pl.BlockSpec((None, dff, d), lambda e: (e, 0, 0))]
    return pl.pallas_call(
        functools.partial(_ffn_kernel, caps=caps),
        out_shape=tuple(jax.ShapeDtypeStruct((n_exp * cap, d), BF16) for cap in caps),
        grid=(n_exp,),
        in_specs=in_specs,
        out_specs=tuple(pl.BlockSpec((cap, d), lambda e: (e, 0)) for cap in caps),
        compiler_params=_cparams(("arbitrary",)),
        name="ffn",
    )(*[g[0] for g in groups], w1, w3, w2)


def _combine_kernel(tbl_ref, ye_hbm, pos_ref, aff_ref, x1_ref, mod_ref, gpo_ref, y_ref, buf, sem, xbuf, xsem,
                    acc_scr, *, cap, blocks_per_tile):
    i = pl.program_id(0)
    nsteps = pl.num_programs(0)
    d = x1_ref.shape[1]
    lanes = pos_ref.shape[1]
    win = COMBINE_WINDOW
    last_start = ye_hbm.shape[0] - win
    slot = i % 2

    def first_row(step, e):
        return tbl_ref[step * blocks_per_tile, e] + e * cap

    def window_start(first, k):
        unclamped = (first // BF16_ROWS_PER_TILE) * BF16_ROWS_PER_TILE + k * win
        return unclamped, jnp.minimum(unclamped, last_start)

    def fetch(step, to_slot, e):
        start = window_start(first_row(step, e), 0)[1]
        return pltpu.make_async_copy(ye_hbm.at[pl.ds(pl.multiple_of(start, BF16_ROWS_PER_TILE), win), :],
                                     buf.at[to_slot, pl.ds(e * win, win), :], sem.at[to_slot, e])

    @pl.when(i == 0)
    def _prime():
        for e in range(N_EXPERTS):
            fetch(0, 0, e).start()

    @pl.when(i + 1 < nsteps)
    def _ahead():
        for e in range(N_EXPERTS):
            fetch(i + 1, 1 - slot, e).start()

    for e in range(N_EXPERTS):
        fetch(i, slot, e).wait()
    lane_row = lax.broadcasted_iota(I32, (1, win), 1)
    pieces = []
    for e in range(N_EXPERTS):
        pcol = pos_ref[:, e:e + 1]
        grow = jnp.where(pcol >= 0, pcol + e * cap, -1)
        start = window_start(first_row(i, e), 0)[1]
        pieces.append(jnp.where(grow == start + lane_row, aff_ref[:, e:e + 1], 0.0).astype(BF16))
    total = None
    for e0 in range(0, N_EXPERTS, COMBINE_EXPERT_GROUP):
        grp = jnp.concatenate(pieces[e0:e0 + COMBINE_EXPERT_GROUP], axis=1)
        part = _dot(grp, buf[slot, pl.ds(e0 * win, COMBINE_EXPERT_GROUP * win), :])
        total = part if total is None else total + part
    acc_scr[...] = total

    def extra_windows(e):
        covered = window_start(first_row(i, e), 1)[0]
        return jnp.maximum(first_row(i + 1, e) - covered + win - 1, 0) // win

    def expert_extra(e, carry):
        first = first_row(i, e)
        extra = extra_windows(e)

        def more(k, c):
            unclamped, start = window_start(first, k)
            cp = pltpu.make_async_copy(ye_hbm.at[pl.ds(pl.multiple_of(start, BF16_ROWS_PER_TILE), win), :],
                                       xbuf, xsem)
            cp.start()
            cp.wait()
            at_e = lax.broadcasted_iota(I32, (1, lanes), 1) == e
            pcol = jnp.sum(jnp.where(at_e, pos_ref[...].astype(F32), 0.0), axis=1, keepdims=True).astype(I32)
            wcol = jnp.sum(jnp.where(at_e, aff_ref[...], 0.0), axis=1, keepdims=True)
            grow = jnp.where(pcol >= 0, pcol + e * cap, -1)
            hit = (grow == start + lane_row) & (grow >= unclamped)
            acc_scr[...] += _dot(jnp.where(hit, wcol, 0.0).astype(BF16), xbuf[...])
            return c

        lax.fori_loop(1, 1 + extra, more, 0)
        return carry

    any_extra = extra_windows(0)
    for e in range(1, N_EXPERTS):
        any_extra = any_extra + extra_windows(e)

    @pl.when(any_extra > 0)
    def _overflow():
        lax.fori_loop(0, N_EXPERTS, expert_extra, 0)

    m = mod_ref[0]
    y_ref[...] = x1_ref[...] + m[:, 5 * d:6 * d] * _rms(acc_scr[...], gpo_ref[...])


def _combine_call(tbl, ye, pos, aff, x1, mod3, mod_row_of_tile, g_po, cap, tag):
    n, d = x1.shape
    tm = COMBINE_TILE
    lanes = pos.shape[1]
    grid_spec = pltpu.PrefetchScalarGridSpec(
        num_scalar_prefetch=1,
        grid=(n // tm,),
        in_specs=[pl.BlockSpec(memory_space=pl.ANY),
                  pl.BlockSpec((tm, lanes), lambda i, t: (i, 0)),
                  pl.BlockSpec((tm, lanes), lambda i, t: (i, 0)),
                  pl.BlockSpec((tm, d), lambda i, t: (i, 0)),
                  pl.BlockSpec((1, 1, 6 * d), lambda i, t: (mod_row_of_tile(i, tm), 0, 0)),
                  pl.BlockSpec((1, d), lambda i, t: (0, 0))],
        out_specs=pl.BlockSpec((tm, d), lambda i, t: (i, 0)),
        scratch_shapes=[pltpu.VMEM((2, N_EXPERTS * COMBINE_WINDOW, d), BF16),
                        pltpu.SemaphoreType.DMA((2, N_EXPERTS)),
                        pltpu.VMEM((COMBINE_WINDOW, d), BF16),
                        pltpu.SemaphoreType.DMA(()),
                        pltpu.VMEM((tm, d), F32)],
    )
    return pl.pallas_call(
        functools.partial(_combine_kernel, cap=cap, blocks_per_tile=tm // ROUTE_BLOCK),
        out_shape=jax.ShapeDtypeStruct((n, d), F32),
        grid_spec=grid_spec,
        compiler_params=_cparams(("arbitrary",)),
        name="combine_" + tag,
    )(tbl, ye, pos, aff, x1, mod3, g_po)


def _rope_tables(seq):
    rows = seq // GRID_W
    r = jnp.repeat(jnp.arange(rows), GRID_W).astype(F32)
    col = jnp.tile(jnp.arange(GRID_W), rows).astype(F32)
    pairs = HEAD_DIM // 4
    freqs = ROPE_THETA ** (-jnp.arange(pairs, dtype=F32) / pairs)
    ang = jnp.concatenate([r[:, None] * freqs, col[:, None] * freqs], axis=-1)
    cos = jnp.repeat(jnp.cos(ang), 2, axis=-1)
    sin = jnp.repeat(jnp.sin(ang), 2, axis=-1)
    even = (jnp.arange(HEAD_DIM) % 2) == 0
    return cos, jnp.where(even, -sin, 0.0), jnp.where(even, 0.0, sin)


def _trunk_to_routing(x, mod3, mod_row_of_tile, rope_tabs, ctx, lw, tag):
    (g_pre_mix, g_post_mix, g_pre_ffn, g_post_ffn, w_in_p, g_q, g_k, wgf, bgf, wgb, bgb, g_gla,
     w_pa, w_pg, w_out, w_r, w1, w3, w2) = lw
    batch, seq, d = x.shape
    n = batch * seq
    x2 = x.reshape(n, d)
    (q_a, k_a, v_a, q_g, k_g, v_g, r_g, lg_f, lg_b, gates) = _inproj_call(
        x2, mod3, mod_row_of_tile, g_pre_mix, w_in_p, g_q, g_k, wgf, bgf, wgb, bgb, rope_tabs, batch, seq)
    if ctx is None:
        o_a = _attn_call(q_a, k_a, v_a, None, None)
        o_g, s_f, s_b = _gla_call(q_g, k_g, v_g, lg_f, lg_b, r_g, g_gla, None, None, batch, seq, True,
                                  GLA_CTX_HEADS_PER_STEP, GLA_CTX_SEQS_PER_STEP)
    else:
        ck, cv, s_f0, s_b0 = ctx
        o_a = _attn_call(q_a, k_a, v_a, ck, cv)
        (o_g,) = _gla_call(q_g, k_g, v_g, lg_f, lg_b, r_g, g_gla, s_f0, s_b0, batch, seq, False,
                           GLA_LAT_HEADS_PER_STEP, 1)
        s_f = s_b = None
    x1, h, aff = _outproj_call(o_a, o_g, gates, x2, mod3, mod_row_of_tile, w_pa, w_pg, w_out,
                               g_post_mix, g_pre_ffn, w_r, tag)
    cap = (EC_CAPACITY_FACTOR * n) // N_EXPERTS
    pos, post, tbl = _route_call(aff, cap, tag)
    xs = _sc_gather_call(h, post, cap)
    return dict(x1=x1, xs=xs, aff=aff, pos=pos, tbl=tbl, cap=cap,
                mod_row_of_tile=mod_row_of_tile, tag=tag, shape=(batch, seq, d)), (k_a, v_a, s_f, s_b)


def _expert_ffn(groups, mod3, g_post_ffn, w1, w3, w2):
    yes = _ffn_call([(g["xs"], g["cap"]) for g in groups], w1, w3, w2)
    outs = []
    for g, ye in zip(groups, yes):
        y = _combine_call(g["tbl"], ye, g["pos"], g["aff"], g["x1"], mod3, g["mod_row_of_tile"], g_post_ffn,
                          g["cap"], g["tag"])
        outs.append(y.reshape(g["shape"]))
    return outs


def kernel(x_prompt, x_sample, cache_k, cache_v, state_gla_fwd, state_gla_bwd, c, c_ctx, g_pre_mix, g_post_mix, g_pre_ffn, g_post_ffn, w_mod, b_mod, w_in, g_q, g_k, w_gk2_f, b_gk_f, w_gk2_b, b_gk_b, g_gla, w_pa, w_pg, w_out, w_router, w1, w3, w2):
    depth = w_in.shape[0]
    assert depth == 1, "single trunk layer"
    d = x_prompt.shape[-1]
    dec_batch, dec_seq, _ = x_sample.shape
    assert dec_batch + 1 <= MOD_ROWS
    l = 0
    rank = GLA_GATE_RANK
    cc = jnp.concatenate([c_ctx[None, :], c, jnp.zeros((MOD_ROWS - 1 - dec_batch, d), F32)], axis=0)
    mod, w_in_p = _mod_call(cc, w_mod[l], b_mod[l].reshape(1, -1), jnp.swapaxes(w_in[l], 0, 1))
    mod3 = mod.reshape(MOD_ROWS, 1, 6 * d)
    gkw = w_gk2_f.shape[-1]
    wgf = jnp.zeros((V7X_LANES, gkw), F32).at[0:rank].set(w_gk2_f[l]).astype(BF16)
    wgb = jnp.zeros((V7X_LANES, gkw), F32).at[rank:2 * rank].set(w_gk2_b[l]).astype(BF16)
    w_r = jnp.zeros((d, V7X_LANES), F32).at[:, :N_EXPERTS].set(w_router[l]).astype(BF16)
    row = lambda a: a[l].reshape(1, -1)
    lw = (row(g_pre_mix), row(g_post_mix), row(g_pre_ffn), row(g_post_ffn), w_in_p, row(g_q), row(g_k),
          wgf, row(b_gk_f), wgb, row(b_gk_b), row(g_gla),
          w_pa[l].astype(BF16), w_pg[l].astype(BF16), w_out[l].astype(BF16), w_r, w1[l], w3[l], w2[l])

    gp, (nk, nv, nsf, nsb) = _trunk_to_routing(x_prompt, mod3, lambda i, tm: 0, None, None, lw, "ctx")
    ctx = (cache_k, cache_v, state_gla_fwd, state_gla_bwd)
    gs, _ = _trunk_to_routing(x_sample, mod3, lambda i, tm: 1 + (i * tm) // dec_seq, _rope_tables(dec_seq),
                              ctx, lw, "lat")
    yp, ys = _expert_ffn([gp, gs], mod3, lw[3], lw[16], lw[17], lw[18])
    return (yp, ys, nk, nv, nsf, nsb)
```

```python
import functools

import jax
import jax.numpy as jnp
from jax import lax
from jax.experimental import pallas as pl
from jax.experimental.pallas import tpu as pltpu
from jax.experimental.pallas import tpu_sc as plsc

F32 = jnp.float32
BF16 = jnp.bfloat16
I32 = jnp.int32

N_HEADS = 8
N_KV_HEADS = 2
HEAD_DIM = 128
GRID_W = 64
ROPE_THETA = 10000.0
GLA_HEADS = 4
GLA_GATE_RANK = 16
GLA_GATE_NORM = 16.0
GLA_CHUNK = 64
N_EXPERTS = 16
EC_CAPACITY_FACTOR = 2
EPS = 1e-6

V7X_LANES = 128
V7X_VMEM_BYTES = 64 * 1024 * 1024
V7X_VMEM_RESERVE_BYTES = 6 * 1024 * 1024
BF16_ROWS_PER_TILE = 16
V7X_SC_CORES = 2
V7X_SC_SUBCORES = 16
V7X_SC_LANES = 16

TOKEN_TILE = 512
OUTPROJ_TILE = 512
OUTPROJ_ROW_GROUPS = 4
ATTN_Q_TILE = 256
ATTN_SEQS_PER_STEP = 4
GLA_BLOCK = 256
GLA_HEADS_PER_STAGE_GROUP = 2
GLA_CTX_HEADS_PER_STEP = 4
GLA_CTX_SEQS_PER_STEP = 4
GLA_LAT_HEADS_PER_STEP = 2
ROUTE_BLOCK = 256
SC_GATHER_CHUNK = 128
COMBINE_TILE = 512
COMBINE_WINDOW = 128
COMBINE_EXPERT_GROUP = 2
MOD_ROWS = 8
MOD_STEPS = 12
MOD_WEIGHT_SLABS = 11


def _cparams(semantics):
    return pltpu.CompilerParams(dimension_semantics=semantics,
                                vmem_limit_bytes=V7X_VMEM_BYTES - V7X_VMEM_RESERVE_BYTES)


def _sigmoid(x):
    return 0.5 * (jnp.tanh(0.5 * x) + 1.0)


def _silu(x):
    return x * _sigmoid(x)


def _log_sigmoid(x):
    return jnp.minimum(x, 0.0) - jnp.log1p(jnp.exp(-jnp.abs(x)))


def _rms(x, g):
    ms = jnp.mean(x * x, axis=-1, keepdims=True)
    return x * lax.rsqrt(ms + EPS) * g


def _dot(a, b):
    return jnp.dot(a, b, preferred_element_type=F32)


def _dot_nt(a, b):
    return lax.dot_general(a, b, (((1,), (1,)), ((), ())), preferred_element_type=F32)


def _mod_kernel(c_ref, w_ref, b_ref, win_ref, o_ref, wout_ref):
    s = _silu(c_ref[...]).astype(BF16)
    o_ref[...] = _dot(s, w_ref[...].astype(BF16)) + b_ref[...]
    wout_ref[...] = win_ref[...].astype(BF16)


def _mod_call(cc, w_mod, b_mod, w_in_t):
    d, n6 = w_mod.shape
    din = w_in_t.shape[0]
    steps = MOD_STEPS
    tn = n6 // steps
    slabs = MOD_WEIGHT_SLABS
    rows = din // slabs
    assert tn * steps == n6 and slabs * rows == din and rows % BF16_ROWS_PER_TILE == 0 and slabs <= steps
    slab = lambda j: (jnp.minimum(j, slabs - 1), 0)
    return pl.pallas_call(
        _mod_kernel,
        out_shape=(jax.ShapeDtypeStruct((MOD_ROWS, n6), F32), jax.ShapeDtypeStruct((din, d), BF16)),
        grid=(steps,),
        in_specs=[pl.BlockSpec((MOD_ROWS, d), lambda j: (0, 0)),
                  pl.BlockSpec((d, tn), lambda j: (0, j)),
                  pl.BlockSpec((1, tn), lambda j: (0, j)),
                  pl.BlockSpec((rows, d), slab)],
        out_specs=(pl.BlockSpec((MOD_ROWS, tn), lambda j: (0, j)), pl.BlockSpec((rows, d), slab)),
        compiler_params=_cparams(("arbitrary",)),
        name="mod",
    )(cc, w_mod, b_mod, w_in_t)


def _inproj_layout(d):
    aq, akv = N_HEADS * HEAD_DIM, N_KV_HEADS * HEAD_DIM
    gk, gv = d // 2, d
    names = ("q_a", "k_a", "v_a", "q_g", "k_g", "v_g", "r_g", "gk_f", "gk_b", "gates")
    widths = (aq, akv, akv, gk, gk, gv, gv, GLA_GATE_RANK, GLA_GATE_RANK, 2 * d)
    off, o = {}, 0
    for nme, w in zip(names, widths):
        off[nme] = (o, o + w)
        o += w
    off["gk"] = (off["gk_f"][0], off["gk_f"][0] + V7X_LANES)
    return off, o


def _inproj_kernel(*refs, rope, d):
    if rope:
        (x_ref, mod_ref, gpre_ref, w_ref, gq_ref, gk_ref, wgf_ref, bgf_ref, wgb_ref, bgb_ref,
         cos_ref, se_ref, so_ref, *outs) = refs
    else:
        (x_ref, mod_ref, gpre_ref, w_ref, gq_ref, gk_ref, wgf_ref, bgf_ref, wgb_ref, bgb_ref,
         *outs) = refs
    qa_ref, k_ref, v_ref, qg_ref, kg_ref, vg_ref, rg_ref, lgf_ref, lgb_ref, gate_ref = outs
    off, _ = _inproj_layout(d)
    m = mod_ref[0]
    h = _rms(x_ref[...], gpre_ref[...]) * (1.0 + m[:, d:2 * d]) + m[:, 0:d]
    hb = h.astype(BF16)

    def proj(name):
        a, b = off[name]
        return _dot_nt(hb, w_ref[a:b, :])

    def qk_norm(y, g_ref):
        y = _rms(y, g_ref[...])
        if rope:
            nxt = pltpu.roll(y, HEAD_DIM - 1, axis=1)
            prv = pltpu.roll(y, 1, axis=1)
            y = y * cos_ref[...] + nxt * se_ref[...] + prv * so_ref[...]
        return y

    gk = proj("gk").astype(BF16)

    q = proj("q_a")
    scale = HEAD_DIM ** -0.5
    for hd in range(N_HEADS):
        sl = slice(hd * HEAD_DIM, (hd + 1) * HEAD_DIM)
        qa_ref[:, sl] = (qk_norm(q[:, sl], gq_ref) * scale).astype(BF16)

    k = proj("k_a")
    v = proj("v_a")
    tb, _, _, ts, _ = k_ref.shape
    for kv in range(N_KV_HEADS):
        sl = slice(kv * HEAD_DIM, (kv + 1) * HEAD_DIM)
        k_ref[:, 0, kv] = qk_norm(k[:, sl], gk_ref).reshape(tb, ts, HEAD_DIM)
        v_ref[:, 0, kv] = v[:, sl].reshape(tb, ts, HEAD_DIM)

    dk = (d // 2) // GLA_HEADS
    qg_ref[...] = (proj("q_g") * (dk ** -0.5)).astype(BF16)
    kg_ref[...] = proj("k_g").astype(BF16)
    vg_ref[...] = proj("v_g").astype(BF16)
    rg_ref[...] = _silu(proj("r_g")).astype(BF16)

    gate_ref[...] = _sigmoid(proj("gates")).astype(BF16)

    lgf_ref[...] = _log_sigmoid(_dot(gk, wgf_ref[...]) + bgf_ref[...]) * (1.0 / GLA_GATE_NORM)
    lgb_ref[...] = _log_sigmoid(_dot(gk, wgb_ref[...]) + bgb_ref[...]) * (1.0 / GLA_GATE_NORM)


def _inproj_call(x2, mod3, mod_row_of_tile, g_pre, w_in_p, g_q, g_k, wgf, bgf, wgb, bgb, rope_tabs,
                 batch, seq):
    n, d = x2.shape
    tm = TOKEN_TILE
    _, dinp = _inproj_layout(d)
    rope = rope_tabs is not None
    gk_w = d // 2
    if seq >= tm:
        tb, ts, per = 1, tm, seq // tm
        kv_map = lambda i: (i // per, 0, 0, i % per, 0)
    else:
        tb, ts, per = tm // seq, seq, 1
        kv_map = lambda i: (i, 0, 0, 0, 0)
    row = lambda i: (i, 0)
    const = lambda i: (0, 0)
    in_specs = [
        pl.BlockSpec((tm, d), row),
        pl.BlockSpec((1, 1, 6 * d), lambda i: (mod_row_of_tile(i, tm), 0, 0)),
        pl.BlockSpec((1, d), const),
        pl.BlockSpec((dinp, d), const, pipeline_mode=pl.Buffered(1)),
        pl.BlockSpec((1, HEAD_DIM), const),
        pl.BlockSpec((1, HEAD_DIM), const),
        pl.BlockSpec((V7X_LANES, gk_w), const),
        pl.BlockSpec((1, gk_w), const),
        pl.BlockSpec((V7X_LANES, gk_w), const),
        pl.BlockSpec((1, gk_w), const),
    ]
    args = [x2, mod3, g_pre, w_in_p, g_q, g_k, wgf, bgf, wgb, bgb]
    if rope:
        tab = pl.BlockSpec((tm, HEAD_DIM), lambda i: (i % per, 0))
        in_specs += [tab, tab, tab]
        args += list(rope_tabs)
    kv_shape = jax.ShapeDtypeStruct((batch, 1, N_KV_HEADS, seq, HEAD_DIM), F32)
    kv_spec = pl.BlockSpec((tb, 1, N_KV_HEADS, ts, HEAD_DIM), kv_map)
    out_shape = (
        jax.ShapeDtypeStruct((n, N_HEADS * HEAD_DIM), BF16), kv_shape, kv_shape,
        jax.ShapeDtypeStruct((n, gk_w), BF16), jax.ShapeDtypeStruct((n, gk_w), BF16),
        jax.ShapeDtypeStruct((n, d), BF16), jax.ShapeDtypeStruct((n, d), BF16),
        jax.ShapeDtypeStruct((n, gk_w), F32), jax.ShapeDtypeStruct((n, gk_w), F32),
        jax.ShapeDtypeStruct((n, 2 * d), BF16),
    )
    out_specs = (
        pl.BlockSpec((tm, N_HEADS * HEAD_DIM), row), kv_spec, kv_spec,
        pl.BlockSpec((tm, gk_w), row), pl.BlockSpec((tm, gk_w), row),
        pl.BlockSpec((tm, d), row), pl.BlockSpec((tm, d), row),
        pl.BlockSpec((tm, gk_w), row), pl.BlockSpec((tm, gk_w), row),
        pl.BlockSpec((tm, 2 * d), row),
    )
    return pl.pallas_call(
        functools.partial(_inproj_kernel, rope=rope, d=d),
        out_shape=out_shape,
        grid=(n // tm,),
        in_specs=in_specs,
        out_specs=out_specs,
        compiler_params=_cparams(("parallel",)),
        name="inproj_lat" if rope else "inproj_ctx",
    )(*args)


def _attn_kernel(*refs, cached):
    if cached:
        q_ref, k_ref, v_ref, ck_ref, cv_ref, o_ref = refs
    else:
        q_ref, k_ref, v_ref, o_ref = refs
    seqs = k_ref.shape[0]
    tq = q_ref.shape[0] // seqs
    grp = N_HEADS // N_KV_HEADS
    chains = [(sq, kv) for sq in range(seqs) for kv in range(N_KV_HEADS)]

    def scores_of(sq, kv):
        rows = pl.ds(sq * tq, tq)
        kk = k_ref[sq, 0, kv].astype(BF16)
        vv = v_ref[sq, 0, kv].astype(BF16)
        if cached:
            kk = jnp.concatenate([ck_ref[sq, 0, kv].astype(BF16), kk], axis=0)
            vv = jnp.concatenate([cv_ref[sq, 0, kv].astype(BF16), vv], axis=0)
        heads = [q_ref[rows, (kv * grp + g) * HEAD_DIM:(kv * grp + g + 1) * HEAD_DIM] for g in range(grp)]
        return _dot_nt(jnp.concatenate(heads, axis=0), kk), vv

    look_ahead = cached
    ahead = scores_of(*chains[0]) if look_ahead else None
    for n, (sq, kv) in enumerate(chains):
        s, vv = ahead if look_ahead else scores_of(sq, kv)
        if look_ahead and n + 1 < len(chains):
            ahead = scores_of(*chains[n + 1])
        rows = pl.ds(sq * tq, tq)
        p = jnp.exp(s - jnp.max(s, axis=-1, keepdims=True))
        l = jnp.sum(p, axis=-1, keepdims=True)
        o = _dot(p.astype(BF16), vv) / l
        for g in range(grp):
            hd = kv * grp + g
            o_ref[rows, hd * HEAD_DIM:(hd + 1) * HEAD_DIM] = o[g * tq:(g + 1) * tq].astype(BF16)


def _attn_call(q_a, k_a, v_a, cache_k, cache_v):
    batch, _, _, seq, _ = k_a.shape
    n, aq = q_a.shape
    tq = min(ATTN_Q_TILE, seq)
    per = seq // tq
    cached = cache_k is not None
    seqs = ATTN_SEQS_PER_STEP if (per == 1 and not cached) else 1
    batch = batch // seqs
    own = pl.BlockSpec((seqs, 1, N_KV_HEADS, seq, HEAD_DIM), lambda b, j: (b, 0, 0, 0, 0))
    tq = tq * seqs
    in_specs = [pl.BlockSpec((tq, aq), lambda b, j: (b * per + j, 0)), own, own]
    args = [q_a, k_a, v_a]
    if cached:
        past = cache_k.shape[3]
        cspec = pl.BlockSpec((1, 1, N_KV_HEADS, past, HEAD_DIM), lambda b, j: (b, 0, 0, 0, 0))
        in_specs += [cspec, cspec]
        args += [cache_k, cache_v]
    return pl.pallas_call(
        functools.partial(_attn_kernel, cached=cached),
        out_shape=jax.ShapeDtypeStruct((n, aq), BF16),
        grid=(batch, per),
        in_specs=in_specs,
        out_specs=pl.BlockSpec((tq, aq), lambda b, j: (b * per + j, 0)),
        compiler_params=_cparams(("parallel", "parallel")),
        name="attn_lat" if cached else "attn_ctx",
    )(*args)


def _gla_kernel(*refs, nblk, heads, seqs, has_state, emit_state):
    refs = list(refs)
    q_ref, k_ref, v_ref, lgf_ref, lgb_ref, rg_ref, gg_ref = refs[:7]
    pos = 7
    if has_state:
        s0f_ref, s0b_ref = refs[pos:pos + 2]
        pos += 2
    og_ref = refs[pos]
    pos += 1
    if emit_state:
        sf_ref, sb_ref = refs[pos:pos + 2]
        pos += 2
    of_scr, ob_scr = refs[pos:pos + 2]

    blk = GLA_BLOCK
    ch = GLA_CHUNK
    nch = blk // ch
    dk = q_ref.shape[1] // heads
    dv = v_ref.shape[1] // heads
    shift = ch.bit_length() - 1
    row_in_chunk = lax.broadcasted_iota(I32, (blk, dk), 0) & (ch - 1)
    ri = lax.broadcasted_iota(I32, (blk, blk), 0)
    ci = lax.broadcasted_iota(I32, (blk, blk), 1)
    same = (ri >> shift) == (ci >> shift)
    mask_f = same & (ci <= ri)
    mask_b = same & (ci >= ri)

    def one_block(b0, hd, reverse):
        rows = pl.ds(b0, blk)
        kcols = slice(hd * dk, (hd + 1) * dk)
        q = q_ref[rows, kcols].astype(F32)
        k = k_ref[rows, kcols].astype(F32)
        v = v_ref[rows, hd * dv:(hd + 1) * dv]
        b = (lgb_ref if reverse else lgf_ref)[rows, kcols]
        mask = mask_b if reverse else mask_f
        s = 1
        while s < ch:
            if reverse:
                sh = pltpu.roll(b, blk - s, axis=0)
                b = b + jnp.where(row_in_chunk < ch - s, sh, 0.0)
            else:
                sh = pltpu.roll(b, s, axis=0)
                b = b + jnp.where(row_in_chunk >= s, sh, 0.0)
            s *= 2
        qe = (q * jnp.exp(b)).astype(BF16)
        ke = (k * jnp.exp(-b)).astype(BF16)
        scores = _dot_nt(qe, ke)
        end_row = [c * ch + (0 if reverse else ch - 1) for c in range(nch)]
        ends = [b[r:r + 1, :] for r in end_row]
        b_end = jnp.concatenate([jnp.broadcast_to(e, (ch, dk)) for e in ends], axis=0)
        kd = (k * jnp.exp(b_end - b)).astype(BF16)
        return dict(scores=scores, mask=mask, qe=qe, kd=kd, v=v, ends=ends, reverse=reverse)

    def chunk_products(w):
        return [lax.dot_general(w["kd"][c * ch:(c + 1) * ch], w["v"][c * ch:(c + 1) * ch],
                                (((0,), (0,)), ((), ())), preferred_element_type=F32) for c in range(nch)]

    def intra_chunk(w):
        return _dot(jnp.where(w["mask"], w["scores"], 0.0).astype(BF16), w["v"])

    def across_chunks(w, kv, state):
        decay = jnp.exp(jnp.concatenate(w["ends"] + [jnp.zeros((dk - nch, dk), F32)], axis=0)).T
        inter = [None] * nch
        for c in (range(nch - 1, -1, -1) if w["reverse"] else range(nch)):
            if state is None:
                inter[c] = jnp.zeros((ch, dv), F32)
                state = kv[c]
            else:
                inter[c] = _dot(w["qe"][c * ch:(c + 1) * ch], state.astype(BF16))
                state = decay[:, c:c + 1] * state + kv[c]
        return jnp.concatenate(inter, axis=0), state

    group = min(GLA_HEADS_PER_STAGE_GROUP, heads)
    for sq in range(seqs):
        srows = pl.ds(sq * nblk * blk, nblk * blk)
        for h0 in range(0, heads, group):
            hds = list(range(h0, h0 + group))
            states = {(hd, rev): ((s0b_ref if rev else s0f_ref)[sq, 0, hd] if has_state else None)
                      for hd in hds for rev in (False, True)}
            for i in range(nblk):
                block_of = {False: sq * nblk + i, True: sq * nblk + nblk - 1 - i}
                chains = [(hd, rev) for hd in hds for rev in (False, True)]
                work = {c: one_block(block_of[c[1]] * blk, c[0], c[1]) for c in chains}
                kvs = {c: chunk_products(work[c]) for c in chains}
                intra = {c: intra_chunk(work[c]) for c in chains}
                for c in chains:
                    hd, rev = c
                    inter, states[c] = across_chunks(work[c], kvs[c], states[c])
                    scr = ob_scr if rev else of_scr
                    scr[pl.ds(block_of[rev] * blk, blk), hd * dv:(hd + 1) * dv] = intra[c] + inter
            for hd in hds:
                vcols = slice(hd * dv, (hd + 1) * dv)
                if emit_state:
                    sf_ref[sq, 0, hd] = states[(hd, False)]
                    sb_ref[sq, 0, hd] = states[(hd, True)]
                o = of_scr[srows, vcols] + ob_scr[srows, vcols]
                og_ref[srows, vcols] = (_rms(o, gg_ref[...]) * rg_ref[srows, vcols].astype(F32)).astype(BF16)


def _gla_call(q_g, k_g, v_g, lg_f, lg_b, r_g, g_gla, state_f, state_b, batch, seq, emit_state, heads, seqs):
    n, gkw = q_g.shape
    d = v_g.shape[1]
    dk, dv = gkw // GLA_HEADS, d // GLA_HEADS
    has_state = state_f is not None
    nblk = seq // GLA_BLOCK
    batch = batch // seqs
    seq = seq * seqs
    kspec = pl.BlockSpec((seq, heads * dk), lambda b, h: (b, h))
    vspec = pl.BlockSpec((seq, heads * dv), lambda b, h: (b, h))
    sspec = pl.BlockSpec((seqs, 1, heads, dk, dv), lambda b, h: (b, 0, h, 0, 0))
    in_specs = [kspec, kspec, vspec, kspec, kspec, vspec, pl.BlockSpec((1, dv), lambda b, h: (0, 0))]
    args = [q_g, k_g, v_g, lg_f, lg_b, r_g, g_gla]
    if has_state:
        in_specs += [sspec, sspec]
        args += [state_f, state_b]
    out_shape = [jax.ShapeDtypeStruct((n, d), BF16)]
    out_specs = [vspec]
    if emit_state:
        st = jax.ShapeDtypeStruct((batch * seqs, 1, GLA_HEADS, dk, dv), F32)
        out_shape += [st, st]
        out_specs += [sspec, sspec]
    return pl.pallas_call(
        functools.partial(_gla_kernel, nblk=nblk, heads=heads, seqs=seqs, has_state=has_state,
                          emit_state=emit_state),
        out_shape=tuple(out_shape),
        grid=(batch, GLA_HEADS // heads),
        in_specs=in_specs,
        out_specs=tuple(out_specs),
        scratch_shapes=[pltpu.VMEM((seq, heads * dv), F32), pltpu.VMEM((seq, heads * dv), F32)],
        compiler_params=_cparams(("parallel", "parallel")),
        name="gla_ctx" if emit_state else "gla_lat",
    )(*args)


def _outproj_kernel(oa_ref, og_ref, gate_ref, x_ref, mod_ref, wpa_ref, wpg_ref, wout_ref, gpm_ref,
                    gpf_ref, wr_ref, x1_ref, h_ref, aff_ref):
    d = x_ref.shape[1]
    m = mod_ref[0]
    sub = x_ref.shape[0] // OUTPROJ_ROW_GROUPS
    groups = [pl.ds(g * sub, sub) for g in range(OUTPROJ_ROW_GROUPS)]
    wpa, wpg, wout = (w[...].astype(BF16) for w in (wpa_ref, wpg_ref, wout_ref))
    branch = [(_dot(oa_ref[r, :], wpa), _dot(og_ref[r, :], wpg)) for r in groups]
    mo = jnp.concatenate(
        [_dot((gate_ref[r, 0:d].astype(F32) * oa + gate_ref[r, d:2 * d].astype(F32) * og).astype(BF16), wout)
         for r, (oa, og) in zip(groups, branch)], axis=0)
    x1 = x_ref[...] + m[:, 2 * d:3 * d] * _rms(mo, gpm_ref[...])
    x1_ref[...] = x1
    hb = (_rms(x1, gpf_ref[...]) * (1.0 + m[:, 4 * d:5 * d]) + m[:, 3 * d:4 * d]).astype(BF16)
    bits = lax.bitcast_convert_type(hb.astype(F32), jnp.uint32)
    packed = (bits[:, 0:d // 2] >> 16) | (bits[:, d // 2:d] & jnp.uint32(0xFFFF0000))
    h_ref[...] = lax.bitcast_convert_type(packed, I32)
    logits = _dot(hb, wr_ref[...])
    valid = lax.broadcasted_iota(I32, logits.shape, 1) < N_EXPERTS
    mx = jnp.max(jnp.where(valid, logits, -jnp.inf), axis=-1, keepdims=True)
    ex = jnp.where(valid, jnp.exp(logits - mx), 0.0)
    aff_ref[...] = ex / jnp.sum(ex, axis=-1, keepdims=True)


def _outproj_call(o_a, o_g, gates, x2, mod3, mod_row_of_tile, w_pa, w_pg, w_out, g_pm, g_pf, w_r, tag):
    n, d = x2.shape
    tm = OUTPROJ_TILE
    row = lambda i: (i, 0)
    const = lambda i: (0, 0)
    wspec = pl.BlockSpec((d, d), const, pipeline_mode=pl.Buffered(1))
    return pl.pallas_call(
        _outproj_kernel,
        out_shape=(jax.ShapeDtypeStruct((n, d), F32), jax.ShapeDtypeStruct((n, d // 2), I32),
                   jax.ShapeDtypeStruct((n, V7X_LANES), F32)),
        grid=(n // tm,),
        in_specs=[pl.BlockSpec((tm, d), row), pl.BlockSpec((tm, d), row), pl.BlockSpec((tm, 2 * d), row),
                  pl.BlockSpec((tm, d), row),
                  pl.BlockSpec((1, 1, 6 * d), lambda i: (mod_row_of_tile(i, tm), 0, 0)),
                  wspec, wspec, wspec, pl.BlockSpec((1, d), const), pl.BlockSpec((1, d), const),
                  pl.BlockSpec((d, V7X_LANES), const)],
        out_specs=(pl.BlockSpec((tm, d), row), pl.BlockSpec((tm, d // 2), row),
                   pl.BlockSpec((tm, V7X_LANES), row)),
        compiler_params=_cparams(("parallel",)),
        name="outproj_" + tag,
    )(o_a, o_g, gates, x2, mod3, w_pa, w_pg, w_out, g_pm, g_pf, w_r)


def _route_kernel(aff_ref, pos_ref, post_ref, tbl_ref, afft_scr, *, cap):
    n = aff_ref.shape[0]
    rb = ROUTE_BLOCK
    nb = n // rb
    lanes = aff_ref.shape[1]
    lane = lax.broadcasted_iota(I32, (1, lanes), 1)
    expert_lane = lane < N_EXPERTS
    tbl_ref[...] = jnp.zeros(tbl_ref.shape, I32)

    def to_token_lanes(c, carry):
        start = pl.multiple_of(c * rb, rb)
        afft_scr[c] = aff_ref[pl.ds(start, rb), :].T[0:N_EXPERTS, :]
        return carry

    lax.fori_loop(0, nb, to_token_lanes, 0)
    aff_t = afft_scr[...]

    def count(hit):
        return jnp.sum(jnp.sum(hit.astype(I32), axis=0), axis=1, keepdims=True)

    def bit_step(i, lo):
        t = lo | jnp.left_shift(jnp.int32(1), 30 - i)
        ge = aff_t >= lax.bitcast_convert_type(t, F32)[None]
        return jnp.where(count(ge) >= cap, t, lo)

    thr_bits = lax.fori_loop(0, 31, bit_step, jnp.zeros((N_EXPERTS, 1), I32))
    need_t = cap - count(aff_t > lax.bitcast_convert_type(thr_bits, F32)[None])

    def to_expert_lanes(col):
        full = jnp.concatenate([jnp.broadcast_to(col, (N_EXPERTS, lanes)),
                                jnp.zeros((lanes - N_EXPERTS, lanes), I32)], axis=0)
        return full.T[0:1, :]

    thr = lax.bitcast_convert_type(to_expert_lanes(thr_bits), F32)
    need = to_expert_lanes(need_t).astype(F32)
    capf = float(cap)

    r = lax.broadcasted_iota(I32, (rb, rb), 0)
    c_ = lax.broadcasted_iota(I32, (rb, rb), 1)
    tril = jnp.where(c_ <= r, 1.0, 0.0).astype(BF16)

    def blk_step(c, carry):
        eq_before, raw_before = carry
        start = pl.multiple_of(c * rb, rb)
        a = aff_ref[pl.ds(start, rb), :]
        gt = a > thr
        eq = a == thr
        eq_incl = _dot(tril, jnp.where(eq, 1.0, 0.0).astype(BF16)) + eq_before
        raw = (gt | (eq & (eq_incl <= need))) & expert_lane
        raw_incl = _dot(tril, jnp.where(raw, 1.0, 0.0).astype(BF16)) + raw_before
        sel = raw & (raw_incl <= capf)
        self_ = jnp.where(sel, 1.0, 0.0)
        incl = jnp.minimum(raw_incl, capf)
        sel_before = jnp.minimum(raw_before, capf)
        excl = incl - self_
        posb = jnp.where(sel, excl, -1.0).astype(I32)
        pos_ref[pl.ds(start, rb), :] = posb
        post_ref[:, pl.ds(start, rb)] = posb.T[0:N_EXPERTS, :]
        tbl_ref[pl.ds(c, 1), :] = sel_before.astype(I32)
        return (eq_incl[rb - 1:rb, :], raw_incl[rb - 1:rb, :])

    zero = jnp.zeros((1, lanes), F32)
    _, total = lax.fori_loop(0, nb, blk_step, (zero, zero), unroll=2)
    tbl_ref[pl.ds(nb, 1), :] = jnp.minimum(total, capf).astype(I32)


def _route_call(aff, cap, tag):
    n, lanes = aff.shape
    nb = n // ROUTE_BLOCK
    tbl_rows = -(-(nb + 1) // 8) * 8
    full = lambda *shape: pl.BlockSpec(shape, lambda: tuple(0 for _ in shape))
    return pl.pallas_call(
        functools.partial(_route_kernel, cap=cap),
        out_shape=(jax.ShapeDtypeStruct((n, lanes), I32),
                   jax.ShapeDtypeStruct((N_EXPERTS, n), I32),
                   jax.ShapeDtypeStruct((tbl_rows, lanes), I32)),
        in_specs=[full(n, lanes)],
        out_specs=(full(n, lanes), full(N_EXPERTS, n), full(tbl_rows, lanes)),
        scratch_shapes=[pltpu.VMEM((nb, N_EXPERTS, ROUTE_BLOCK), F32)],
        compiler_params=_cparams(()),
        name="route_" + tag,
    )(aff)


def _sc_gather_call(table, post, cap):
    n_exp, n = post.shape
    words = table.shape[1]
    workers = V7X_SC_CORES * V7X_SC_SUBCORES
    parts = workers // n_exp
    chunk = SC_GATHER_CHUNK
    lanes = V7X_SC_LANES
    per_part = cap // parts
    assert parts * n_exp == workers and per_part % chunk == 0 and n % lanes == 0
    mesh = plsc.VectorSubcoreMesh(core_axis_name="c", subcore_axis_name="s",
                                  num_cores=V7X_SC_CORES, num_subcores=V7X_SC_SUBCORES)

    def body(table_hbm, post_hbm, out_hbm, pos_v, idx_v, rows_v, sem):
        wid = lax.axis_index("s") * V7X_SC_CORES + lax.axis_index("c")
        e = wid // parts
        part = wid % parts
        pltpu.sync_copy(post_hbm.at[e], pos_v)
        lane = lax.iota(I32, lanes)

        @pl.loop(0, n, step=lanes)
        def _(t0):
            p = pos_v[pl.ds(t0, lanes)]
            plsc.store_scatter(idx_v, [p], lane + t0, mask=p >= 0)

        @pl.loop(0, per_part // chunk)
        def _(j):
            off = pl.multiple_of(part * per_part + j * chunk, chunk)
            pltpu.async_copy(table_hbm.at[idx_v.at[pl.ds(off, chunk)]], rows_v, sem).wait()
            pltpu.sync_copy(rows_v, out_hbm.at[pl.ds(e * cap + off, chunk)])

    return pl.kernel(
        body,
        out_type=jax.ShapeDtypeStruct((n_exp * cap, words), table.dtype),
        mesh=mesh,
        scratch_types=[pltpu.VMEM((n,), I32), pltpu.VMEM((cap,), I32),
                       pltpu.VMEM((chunk, words), table.dtype), pltpu.SemaphoreType.DMA],
        compiler_params=pltpu.CompilerParams(needs_layout_passes=False),
        name="sc_gather",
    )(table, post)


def _ffn_kernel(*refs, caps):
    ng = len(caps)
    xs_refs = refs[:ng]
    w1_ref, w3_ref, w2_ref = refs[ng:ng + 3]
    ye_refs = refs[ng + 3:2 * ng + 3]
    row_off = [sum(caps[:g]) for g in range(ng)]

    def unpack(words):
        w = lax.bitcast_convert_type(words, jnp.uint32)
        lo = lax.bitcast_convert_type(w << 16, F32).astype(BF16)
        hi = lax.bitcast_convert_type(w & jnp.uint32(0xFFFF0000), F32).astype(BF16)
        return jnp.concatenate([lo, hi], axis=1)

    xs = jnp.concatenate([unpack(r[...]) for r in xs_refs], axis=0)
    hid = _silu(_dot(xs, w1_ref[...].astype(BF16))) * _dot(xs, w3_ref[...].astype(BF16))
    ye = _dot(hid.astype(BF16), w2_ref[...].astype(BF16)).astype(BF16)
    for g in range(ng):
        ye_refs[g][...] = ye[row_off[g]:row_off[g] + caps[g]]


def _ffn_call(groups, w1, w3, w2):
    caps = tuple(g[1] for g in groups)
    n_exp, d, dff = w1.shape
    in_specs = [pl.BlockSpec((cap, d // 2), lambda e: (e, 0)) for cap in caps]
    in_specs += [pl.BlockSpec((None, d, dff), lambda e: (e, 0, 0)),
                 pl.BlockSpec((None, d, dff), lambda e: (e, 0, 0)),
                 pl.BlockSpec((None, dff, d), lambda e: (e, 0, 0))]
    return pl.pallas_call(
        functools.partial(_ffn_kernel, caps=caps),
        out_shape=tuple(jax.ShapeDtypeStruct((n_exp * cap, d), BF16) for cap in caps),
        grid=(n_exp,),
        in_specs=in_specs,
        out_specs=tuple(pl.BlockSpec((cap, d), lambda e: (e, 0)) for cap in caps),
        compiler_params=_cparams(("arbitrary",)),
        name="ffn",
    )(*[g[0] for g in groups], w1, w3, w2)


def _combine_kernel(tbl_ref, ye_hbm, pos_ref, aff_ref, x1_ref, mod_ref, gpo_ref, y_ref, buf, sem, xbuf, xsem,
                    acc_scr, *, cap, blocks_per_tile):
    i = pl.program_id(0)
    nsteps = pl.num_programs(0)
    d = x1_ref.shape[1]
    lanes = pos_ref.shape[1]
    win = COMBINE_WINDOW
    last_start = ye_hbm.shape[0] - win
    slot = i % 2

    def first_row(step, e):
        return tbl_ref[step * blocks_per_tile, e] + e * cap

    def window_start(first, k):
        unclamped = (first // BF16_ROWS_PER_TILE) * BF16_ROWS_PER_TILE + k * win
        return unclamped, jnp.minimum(unclamped, last_start)

    def fetch(step, to_slot, e):
        start = window_start(first_row(step, e), 0)[1]
        return pltpu.make_async_copy(ye_hbm.at[pl.ds(pl.multiple_of(start, BF16_ROWS_PER_TILE), win), :],
                                     buf.at[to_slot, pl.ds(e * win, win), :], sem.at[to_slot, e])

    @pl.when(i == 0)
    def _prime():
        for e in range(N_EXPERTS):
            fetch(0, 0, e).start()

    @pl.when(i + 1 < nsteps)
    def _ahead():
        for e in range(N_EXPERTS):
            fetch(i + 1, 1 - slot, e).start()

    for e in range(N_EXPERTS):
        fetch(i, slot, e).wait()
    lane_row = lax.broadcasted_iota(I32, (1, win), 1)
    pieces = []
    for e in range(N_EXPERTS):
        pcol = pos_ref[:, e:e + 1]
        grow = jnp.where(pcol >= 0, pcol + e * cap, -1)
        start = window_start(first_row(i, e), 0)[1]
        pieces.append(jnp.where(grow == start + lane_row, aff_ref[:, e:e + 1], 0.0).astype(BF16))
    total = None
    for e0 in range(0, N_EXPERTS, COMBINE_EXPERT_GROUP):
        grp = jnp.concatenate(pieces[e0:e0 + COMBINE_EXPERT_GROUP], axis=1)
        part = _dot(grp, buf[slot, pl.ds(e0 * win, COMBINE_EXPERT_GROUP * win), :])
        total = part if total is None else total + part
    acc_scr[...] = total

    def extra_windows(e):
        covered = window_start(first_row(i, e), 1)[0]
        return jnp.maximum(first_row(i + 1, e) - covered + win - 1, 0) // win

    def expert_extra(e, carry):
        first = first_row(i, e)
        extra = extra_windows(e)

        def more(k, c):
            unclamped, start = window_start(first, k)
            cp = pltpu.make_async_copy(ye_hbm.at[pl.ds(pl.multiple_of(start, BF16_ROWS_PER_TILE), win), :],
                                       xbuf, xsem)
            cp.start()
            cp.wait()
            at_e = lax.broadcasted_iota(I32, (1, lanes), 1) == e
            pcol = jnp.sum(jnp.where(at_e, pos_ref[...].astype(F32), 0.0), axis=1, keepdims=True).astype(I32)
            wcol = jnp.sum(jnp.where(at_e, aff_ref[...], 0.0), axis=1, keepdims=True)
            grow = jnp.where(pcol >= 0, pcol + e * cap, -1)
            hit = (grow == start + lane_row) & (grow >= unclamped)
            acc_scr[...] += _dot(jnp.where(hit, wcol, 0.0).astype(BF16), xbuf[...])
            return c

        lax.fori_loop(1, 1 + extra, more, 0)
        return carry

    any_extra = extra_windows(0)
    for e in range(1, N_EXPERTS):
        any_extra = any_extra + extra_windows(e)

    @pl.when(any_extra > 0)
    def _overflow():
        lax.fori_loop(0, N_EXPERTS, expert_extra, 0)

    m = mod_ref[0]
    y_ref[...] = x1_ref[...] + m[:, 5 * d:6 * d] * _rms(acc_scr[...], gpo_ref[...])


def _combine_call(tbl, ye, pos, aff, x1, mod3, mod_row_of_tile, g_po, cap, tag):
    n, d = x1.shape
    tm = COMBINE_TILE
    lanes = pos.shape[1]
    grid_spec = pltpu.PrefetchScalarGridSpec(
        num_scalar_prefetch=1,
        grid=(n // tm,),
        in_specs=[pl.BlockSpec(memory_space=pl.ANY),
                  pl.BlockSpec((tm, lanes), lambda i, t: (i, 0)),
                  pl.BlockSpec((tm, lanes), lambda i, t: (i, 0)),
                  pl.BlockSpec((tm, d), lambda i, t: (i, 0)),
                  pl.BlockSpec((1, 1, 6 * d), lambda i, t: (mod_row_of_tile(i, tm), 0, 0)),
                  pl.BlockSpec((1, d), lambda i, t: (0, 0))],
        out_specs=pl.BlockSpec((tm, d), lambda i, t: (i, 0)),
        scratch_shapes=[pltpu.VMEM((2, N_EXPERTS * COMBINE_WINDOW, d), BF16),
                        pltpu.SemaphoreType.DMA((2, N_EXPERTS)),
                        pltpu.VMEM((COMBINE_WINDOW, d), BF16),
                        pltpu.SemaphoreType.DMA(()),
                        pltpu.VMEM((tm, d), F32)],
    )
    return pl.pallas_call(
        functools.partial(_combine_kernel, cap=cap, blocks_per_tile=tm // ROUTE_BLOCK),
        out_shape=jax.ShapeDtypeStruct((n, d), F32),
        grid_spec=grid_spec,
        compiler_params=_cparams(("arbitrary",)),
        name="combine_" + tag,
    )(tbl, ye, pos, aff, x1, mod3, g_po)


def _rope_tables(seq):
    rows = seq // GRID_W
    r = jnp.repeat(jnp.arange(rows), GRID_W).astype(F32)
    col = jnp.tile(jnp.arange(GRID_W), rows).astype(F32)
    pairs = HEAD_DIM // 4
    freqs = ROPE_THETA ** (-jnp.arange(pairs, dtype=F32) / pairs)
    ang = jnp.concatenate([r[:, None] * freqs, col[:, None] * freqs], axis=-1)
    cos = jnp.repeat(jnp.cos(ang), 2, axis=-1)
    sin = jnp.repeat(jnp.sin(ang), 2, axis=-1)
    even = (jnp.arange(HEAD_DIM) % 2) == 0
    return cos, jnp.where(even, -sin, 0.0), jnp.where(even, 0.0, sin)


def _trunk_to_routing(x, mod3, mod_row_of_tile, rope_tabs, ctx, lw, tag):
    (g_pre_mix, g_post_mix, g_pre_ffn, g_post_ffn, w_in_p, g_q, g_k, wgf, bgf, wgb, bgb, g_gla,
     w_pa, w_pg, w_out, w_r, w1, w3, w2) = lw
    batch, seq, d = x.shape
    n = batch * seq
    x2 = x.reshape(n, d)
    (q_a, k_a, v_a, q_g, k_g, v_g, r_g, lg_f, lg_b, gates) = _inproj_call(
        x2, mod3, mod_row_of_tile, g_pre_mix, w_in_p, g_q, g_k, wgf, bgf, wgb, bgb, rope_tabs, batch, seq)
    if ctx is None:
        o_a = _attn_call(q_a, k_a, v_a, None, None)
        o_g, s_f, s_b = _gla_call(q_g, k_g, v_g, lg_f, lg_b, r_g, g_gla, None, None, batch, seq, True,
                                  GLA_CTX_HEADS_PER_STEP, GLA_CTX_SEQS_PER_STEP)
    else:
        ck, cv, s_f0, s_b0 = ctx
        o_a = _attn_call(q_a, k_a, v_a, ck, cv)
        (o_g,) = _gla_call(q_g, k_g, v_g, lg_f, lg_b, r_g, g_gla, s_f0, s_b0, batch, seq, False,
                           GLA_LAT_HEADS_PER_STEP, 1)
        s_f = s_b = None
    x1, h, aff = _outproj_call(o_a, o_g, gates, x2, mod3, mod_row_of_tile, w_pa, w_pg, w_out,
                               g_post_mix, g_pre_ffn, w_r, tag)
    cap = (EC_CAPACITY_FACTOR * n) // N_EXPERTS
    pos, post, tbl = _route_call(aff, cap, tag)
    xs = _sc_gather_call(h, post, cap)
    return dict(x1=x1, xs=xs, aff=aff, pos=pos, tbl=tbl, cap=cap,
                mod_row_of_tile=mod_row_of_tile, tag=tag, shape=(batch, seq, d)), (k_a, v_a, s_f, s_b)


def _expert_ffn(groups, mod3, g_post_ffn, w1, w3, w2):
    yes = _ffn_call([(g["xs"], g["cap"]) for g in groups], w1, w3, w2)
    outs = []
    for g, ye in zip(groups, yes):
        y = _combine_call(g["tbl"], ye, g["pos"], g["aff"], g["x1"], mod3, g["mod_row_of_tile"], g_post_ffn,
                          g["cap"], g["tag"])
        outs.append(y.reshape(g["shape"]))
    return outs


def kernel(x_prompt, x_sample, cache_k, cache_v, state_gla_fwd, state_gla_bwd, c, c_ctx, g_pre_mix, g_post_mix, g_pre_ffn, g_post_ffn, w_mod, b_mod, w_in, g_q, g_k, w_gk2_f, b_gk_f, w_gk2_b, b_gk_b, g_gla, w_pa, w_pg, w_out, w_router, w1, w3, w2):
    depth = w_in.shape[0]
    assert depth == 1, "single trunk layer"
    d = x_prompt.shape[-1]
    dec_batch, dec_seq, _ = x_sample.shape
    assert dec_batch + 1 <= MOD_ROWS
    l = 0
    rank = GLA_GATE_RANK
    cc = jnp.concatenate([c_ctx[None, :], c, jnp.zeros((MOD_ROWS - 1 - dec_batch, d), F32)], axis=0)
    mod, w_in_p = _mod_call(cc, w_mod[l], b_mod[l].reshape(1, -1), jnp.swapaxes(w_in[l], 0, 1))
    mod3 = mod.reshape(MOD_ROWS, 1, 6 * d)
    gkw = w_gk2_f.shape[-1]
    wgf = jnp.zeros((V7X_LANES, gkw), F32).at[0:rank].set(w_gk2_f[l]).astype(BF16)
    wgb = jnp.zeros((V7X_LANES, gkw), F32).at[rank:2 * rank].set(w_gk2_b[l]).astype(BF16)
    w_r = jnp.zeros((d, V7X_LANES), F32).at[:, :N_EXPERTS].set(w_router[l]).astype(BF16)
    row = lambda a: a[l].reshape(1, -1)
    lw = (row(g_pre_mix), row(g_post_mix), row(g_pre_ffn), row(g_post_ffn), w_in_p, row(g_q), row(g_k),
          wgf, row(b_gk_f), wgb, row(b_gk_b), row(g_gla),
          w_pa[l], w_pg[l], w_out[l], w_r, w1[l], w3[l], w2[l])

    gp, (nk, nv, nsf, nsb) = _trunk_to_routing(x_prompt, mod3, lambda i, tm: 0, None, None, lw, "ctx")
    ctx = (cache_k, cache_v, state_gla_fwd, state_gla_bwd)
    gs, _ = _trunk_to_routing(x_sample, mod3, lambda i, tm: 1 + (i * tm) // dec_seq, _rope_tables(dec_seq),
                              ctx, lw, "lat")
    yp, ys = _expert_ffn([gp, gs], mod3, lw[3], lw[16], lw[17], lw[18])
    return (yp, ys, nk, nv, nsf, nsb)
```

```python
import functools

import jax
import jax.numpy as jnp
import numpy as np
from jax import lax
from jax.experimental import pallas as pl
from jax.experimental.pallas import tpu as pltpu
from jax.experimental.pallas import tpu_sc as plsc

F32 = jnp.float32
BF16 = jnp.bfloat16
I32 = jnp.int32

N_HEADS = 8
N_KV_HEADS = 2
HEAD_DIM = 128
GRID_W = 64
ROPE_THETA = 10000.0
GLA_HEADS = 4
GLA_GATE_RANK = 16
GLA_GATE_NORM = 16.0
GLA_CHUNK = 64
N_EXPERTS = 16
EC_CAPACITY_FACTOR = 2
EPS = 1e-6

V7X_LANES = 128
V7X_VMEM_BYTES = 64 * 1024 * 1024
V7X_VMEM_RESERVE_BYTES = 6 * 1024 * 1024
BF16_ROWS_PER_TILE = 16
V7X_SC_CORES = 2
V7X_SC_SUBCORES = 16
V7X_SC_LANES = 16

TOKEN_TILE = 512
OUTPROJ_TILE = 512
OUTPROJ_ROW_GROUPS = 4
ATTN_Q_TILE = 256
ATTN_SEQS_PER_STEP = 4
GLA_BLOCK = 256
GLA_HEADS_PER_STAGE_GROUP = 2
GLA_CTX_HEADS_PER_STEP = 4
GLA_CTX_SEQS_PER_STEP = 4
GLA_LAT_HEADS_PER_STEP = 2
ROUTE_BLOCK = 256
SC_GATHER_CHUNK = 128
COMBINE_TILE = 512
COMBINE_WINDOW = 128
COMBINE_EXPERT_GROUP = 2
MOD_ROWS = 8
MOD_STEPS = 12
MOD_WEIGHT_SLABS = 11


def _cparams(semantics):
    return pltpu.CompilerParams(dimension_semantics=semantics,
                                vmem_limit_bytes=V7X_VMEM_BYTES - V7X_VMEM_RESERVE_BYTES)


def _sigmoid(x):
    return 0.5 * (jnp.tanh(0.5 * x) + 1.0)


def _silu(x):
    return x * _sigmoid(x)


def _log_sigmoid(x):
    return jnp.minimum(x, 0.0) - jnp.log1p(jnp.exp(-jnp.abs(x)))


def _rms(x, g):
    ms = jnp.mean(x * x, axis=-1, keepdims=True)
    return x * lax.rsqrt(ms + EPS) * g


def _dot(a, b):
    return jnp.dot(a, b, preferred_element_type=F32)


def _dot_nt(a, b):
    return lax.dot_general(a, b, (((1,), (1,)), ((), ())), preferred_element_type=F32)


def _mod_kernel(cctx_ref, c_ref, w_ref, b_ref, win_ref, o_ref, wout_ref):
    rows, d = o_ref.shape[0], c_ref.shape[1]
    ridx = lax.broadcasted_iota(I32, (rows, d), 0)
    cc = jnp.where(ridx == 0, jnp.broadcast_to(cctx_ref[...], (rows, d)), 0.0)
    for b in range(c_ref.shape[0]):
        cc = jnp.where(ridx == b + 1, jnp.broadcast_to(c_ref[b:b + 1, :], (rows, d)), cc)
    out = _dot(_silu(cc).astype(BF16), w_ref[...].astype(BF16)) + b_ref[...]
    for r in range(rows):
        o_ref[r] = out[r:r + 1, :]
    wout_ref[...] = win_ref[...].astype(BF16)


def _mod_call(c_ctx, c, w_mod, b_mod, w_in_t):
    d, n6 = w_mod.shape
    dec_batch = c.shape[0]
    din = w_in_t.shape[0]
    steps = MOD_STEPS
    tn = n6 // steps
    slabs = MOD_WEIGHT_SLABS
    rows = din // slabs
    assert tn * steps == n6 and slabs * rows == din and rows % BF16_ROWS_PER_TILE == 0 and slabs <= steps
    slab = lambda j: (jnp.minimum(j, slabs - 1), 0)
    return pl.pallas_call(
        _mod_kernel,
        out_shape=(jax.ShapeDtypeStruct((MOD_ROWS, 1, n6), F32), jax.ShapeDtypeStruct((din, d), BF16)),
        grid=(steps,),
        in_specs=[pl.BlockSpec((1, d), lambda j: (0, 0)),
                  pl.BlockSpec((dec_batch, d), lambda j: (0, 0)),
                  pl.BlockSpec((d, tn), lambda j: (0, j)),
                  pl.BlockSpec((1, tn), lambda j: (0, j)),
                  pl.BlockSpec((rows, d), slab)],
        out_specs=(pl.BlockSpec((MOD_ROWS, 1, tn), lambda j: (0, 0, j)), pl.BlockSpec((rows, d), slab)),
        compiler_params=_cparams(("arbitrary",)),
        name="mod",
    )(c_ctx, c, w_mod, b_mod, w_in_t)


def _inproj_layout(d):
    aq, akv = N_HEADS * HEAD_DIM, N_KV_HEADS * HEAD_DIM
    gk, gv = d // 2, d
    names = ("q_a", "k_a", "v_a", "q_g", "k_g", "v_g", "r_g", "gk_f", "gk_b", "gates")
    widths = (aq, akv, akv, gk, gk, gv, gv, GLA_GATE_RANK, GLA_GATE_RANK, 2 * d)
    off, o = {}, 0
    for nme, w in zip(names, widths):
        off[nme] = (o, o + w)
        o += w
    off["gk"] = (off["gk_f"][0], off["gk_f"][0] + V7X_LANES)
    return off, o


def _inproj_kernel(*refs, rope, d):
    if rope:
        (x_ref, mod_ref, gpre_ref, w_ref, gq_ref, gk_ref, wgf_ref, bgf_ref, wgb_ref, bgb_ref,
         cos_ref, se_ref, so_ref, *outs) = refs
    else:
        (x_ref, mod_ref, gpre_ref, w_ref, gq_ref, gk_ref, wgf_ref, bgf_ref, wgb_ref, bgb_ref,
         *outs) = refs
    qa_ref, k_ref, v_ref, qg_ref, kg_ref, vg_ref, rg_ref, lgf_ref, lgb_ref, gate_ref = outs
    off, _ = _inproj_layout(d)
    m = mod_ref[0]
    h = _rms(x_ref[...], gpre_ref[...]) * (1.0 + m[:, d:2 * d]) + m[:, 0:d]
    hb = h.astype(BF16)

    def proj(name):
        a, b = off[name]
        return _dot_nt(hb, w_ref[a:b, :])

    def qk_norm(y, g_ref):
        y = _rms(y, g_ref[...])
        if rope:
            nxt = pltpu.roll(y, HEAD_DIM - 1, axis=1)
            prv = pltpu.roll(y, 1, axis=1)
            y = y * cos_ref[...] + nxt * se_ref[...] + prv * so_ref[...]
        return y

    gk = proj("gk").astype(BF16)

    q = proj("q_a")
    scale = HEAD_DIM ** -0.5
    for hd in range(N_HEADS):
        sl = slice(hd * HEAD_DIM, (hd + 1) * HEAD_DIM)
        qa_ref[:, sl] = (qk_norm(q[:, sl], gq_ref) * scale).astype(BF16)

    k = proj("k_a")
    v = proj("v_a")
    tb, _, _, ts, _ = k_ref.shape
    for kv in range(N_KV_HEADS):
        sl = slice(kv * HEAD_DIM, (kv + 1) * HEAD_DIM)
        k_ref[:, 0, kv] = qk_norm(k[:, sl], gk_ref).reshape(tb, ts, HEAD_DIM)
        v_ref[:, 0, kv] = v[:, sl].reshape(tb, ts, HEAD_DIM)

    dk = (d // 2) // GLA_HEADS
    qg_ref[...] = (proj("q_g") * (dk ** -0.5)).astype(BF16)
    kg_ref[...] = proj("k_g").astype(BF16)
    vg_ref[...] = proj("v_g").astype(BF16)
    rg_ref[...] = _silu(proj("r_g")).astype(BF16)

    gate_ref[...] = _sigmoid(proj("gates")).astype(BF16)

    def rank_rows(w_ref, at):
        rank, width = w_ref.shape
        parts = [jnp.zeros((at, width), F32)] if at else []
        parts += [w_ref[...], jnp.zeros((V7X_LANES - at - rank, width), F32)]
        return jnp.concatenate(parts, axis=0).astype(BF16)

    lgf_ref[...] = _log_sigmoid(_dot(gk, rank_rows(wgf_ref, 0)) + bgf_ref[...]) * (1.0 / GLA_GATE_NORM)
    lgb_ref[...] = (_log_sigmoid(_dot(gk, rank_rows(wgb_ref, GLA_GATE_RANK)) + bgb_ref[...])
                    * (1.0 / GLA_GATE_NORM))


def _inproj_call(x2, mod3, mod_row_of_tile, g_pre, w_in_p, g_q, g_k, wgf, bgf, wgb, bgb, rope_tabs,
                 batch, seq):
    n, d = x2.shape
    tm = TOKEN_TILE
    _, dinp = _inproj_layout(d)
    rope = rope_tabs is not None
    gk_w = d // 2
    if seq >= tm:
        tb, ts, per = 1, tm, seq // tm
        kv_map = lambda i: (i // per, 0, 0, i % per, 0)
    else:
        tb, ts, per = tm // seq, seq, 1
        kv_map = lambda i: (i, 0, 0, 0, 0)
    row = lambda i: (i, 0)
    const = lambda i: (0, 0)
    in_specs = [
        pl.BlockSpec((tm, d), row),
        pl.BlockSpec((1, 1, 6 * d), lambda i: (mod_row_of_tile(i, tm), 0, 0)),
        pl.BlockSpec((1, d), const),
        pl.BlockSpec((dinp, d), const, pipeline_mode=pl.Buffered(1)),
        pl.BlockSpec((1, HEAD_DIM), const),
        pl.BlockSpec((1, HEAD_DIM), const),
        pl.BlockSpec((GLA_GATE_RANK, gk_w), const),
        pl.BlockSpec((1, gk_w), const),
        pl.BlockSpec((GLA_GATE_RANK, gk_w), const),
        pl.BlockSpec((1, gk_w), const),
    ]
    args = [x2, mod3, g_pre, w_in_p, g_q, g_k, wgf, bgf, wgb, bgb]
    if rope:
        tab = pl.BlockSpec((tm, HEAD_DIM), lambda i: (i % per, 0))
        in_specs += [tab, tab, tab]
        args += list(rope_tabs)
    kv_shape = jax.ShapeDtypeStruct((batch, 1, N_KV_HEADS, seq, HEAD_DIM), F32)
    kv_spec = pl.BlockSpec((tb, 1, N_KV_HEADS, ts, HEAD_DIM), kv_map)
    out_shape = (
        jax.ShapeDtypeStruct((n, N_HEADS * HEAD_DIM), BF16), kv_shape, kv_shape,
        jax.ShapeDtypeStruct((n, gk_w), BF16), jax.ShapeDtypeStruct((n, gk_w), BF16),
        jax.ShapeDtypeStruct((n, d), BF16), jax.ShapeDtypeStruct((n, d), BF16),
        jax.ShapeDtypeStruct((n, gk_w), F32), jax.ShapeDtypeStruct((n, gk_w), F32),
        jax.ShapeDtypeStruct((n, 2 * d), BF16),
    )
    out_specs = (
        pl.BlockSpec((tm, N_HEADS * HEAD_DIM), row), kv_spec, kv_spec,
        pl.BlockSpec((tm, gk_w), row), pl.BlockSpec((tm, gk_w), row),
        pl.BlockSpec((tm, d), row), pl.BlockSpec((tm, d), row),
        pl.BlockSpec((tm, gk_w), row), pl.BlockSpec((tm, gk_w), row),
        pl.BlockSpec((tm, 2 * d), row),
    )
    return pl.pallas_call(
        functools.partial(_inproj_kernel, rope=rope, d=d),
        out_shape=out_shape,
        grid=(n // tm,),
        in_specs=in_specs,
        out_specs=out_specs,
        compiler_params=_cparams(("parallel",)),
        name="inproj_lat" if rope else "inproj_ctx",
    )(*args)


def _attn_kernel(*refs, cached):
    if cached:
        q_ref, k_ref, v_ref, ck_ref, cv_ref, o_ref = refs
    else:
        q_ref, k_ref, v_ref, o_ref = refs
    seqs = k_ref.shape[0]
    tq = q_ref.shape[0] // seqs
    grp = N_HEADS // N_KV_HEADS
    chains = [(sq, kv) for sq in range(seqs) for kv in range(N_KV_HEADS)]

    def scores_of(sq, kv):
        rows = pl.ds(sq * tq, tq)
        kk = k_ref[sq, 0, kv].astype(BF16)
        vv = v_ref[sq, 0, kv].astype(BF16)
        if cached:
            kk = jnp.concatenate([ck_ref[sq, 0, kv].astype(BF16), kk], axis=0)
            vv = jnp.concatenate([cv_ref[sq, 0, kv].astype(BF16), vv], axis=0)
        heads = [q_ref[rows, (kv * grp + g) * HEAD_DIM:(kv * grp + g + 1) * HEAD_DIM] for g in range(grp)]
        return _dot_nt(jnp.concatenate(heads, axis=0), kk), vv

    look_ahead = cached
    ahead = scores_of(*chains[0]) if look_ahead else None
    for n, (sq, kv) in enumerate(chains):
        s, vv = ahead if look_ahead else scores_of(sq, kv)
        if look_ahead and n + 1 < len(chains):
            ahead = scores_of(*chains[n + 1])
        rows = pl.ds(sq * tq, tq)
        p = jnp.exp(s - jnp.max(s, axis=-1, keepdims=True))
        l = jnp.sum(p, axis=-1, keepdims=True)
        o = _dot(p.astype(BF16), vv) / l
        for g in range(grp):
            hd = kv * grp + g
            o_ref[rows, hd * HEAD_DIM:(hd + 1) * HEAD_DIM] = o[g * tq:(g + 1) * tq].astype(BF16)


def _attn_call(q_a, k_a, v_a, cache_k, cache_v):
    batch, _, _, seq, _ = k_a.shape
    n, aq = q_a.shape
    tq = min(ATTN_Q_TILE, seq)
    per = seq // tq
    cached = cache_k is not None
    seqs = ATTN_SEQS_PER_STEP if (per == 1 and not cached) else 1
    batch = batch // seqs
    own = pl.BlockSpec((seqs, 1, N_KV_HEADS, seq, HEAD_DIM), lambda b, j: (b, 0, 0, 0, 0))
    tq = tq * seqs
    in_specs = [pl.BlockSpec((tq, aq), lambda b, j: (b * per + j, 0)), own, own]
    args = [q_a, k_a, v_a]
    if cached:
        past = cache_k.shape[3]
        cspec = pl.BlockSpec((1, 1, N_KV_HEADS, past, HEAD_DIM), lambda b, j: (b, 0, 0, 0, 0))
        in_specs += [cspec, cspec]
        args += [cache_k, cache_v]
    return pl.pallas_call(
        functools.partial(_attn_kernel, cached=cached),
        out_shape=jax.ShapeDtypeStruct((n, aq), BF16),
        grid=(batch, per),
        in_specs=in_specs,
        out_specs=pl.BlockSpec((tq, aq), lambda b, j: (b * per + j, 0)),
        compiler_params=_cparams(("parallel", "parallel")),
        name="attn_lat" if cached else "attn_ctx",
    )(*args)


def _gla_kernel(*refs, nblk, heads, seqs, has_state, emit_state):
    refs = list(refs)
    q_ref, k_ref, v_ref, lgf_ref, lgb_ref, rg_ref, gg_ref = refs[:7]
    pos = 7
    if has_state:
        s0f_ref, s0b_ref = refs[pos:pos + 2]
        pos += 2
    og_ref = refs[pos]
    pos += 1
    if emit_state:
        sf_ref, sb_ref = refs[pos:pos + 2]
        pos += 2
    of_scr, ob_scr = refs[pos:pos + 2]

    blk = GLA_BLOCK
    ch = GLA_CHUNK
    nch = blk // ch
    dk = q_ref.shape[1] // heads
    dv = v_ref.shape[1] // heads
    shift = ch.bit_length() - 1
    row_in_chunk = lax.broadcasted_iota(I32, (blk, dk), 0) & (ch - 1)
    ri = lax.broadcasted_iota(I32, (blk, blk), 0)
    ci = lax.broadcasted_iota(I32, (blk, blk), 1)
    same = (ri >> shift) == (ci >> shift)
    mask_f = same & (ci <= ri)
    mask_b = same & (ci >= ri)

    def one_block(b0, hd, reverse):
        rows = pl.ds(b0, blk)
        kcols = slice(hd * dk, (hd + 1) * dk)
        q = q_ref[rows, kcols].astype(F32)
        k = k_ref[rows, kcols].astype(F32)
        v = v_ref[rows, hd * dv:(hd + 1) * dv]
        b = (lgb_ref if reverse else lgf_ref)[rows, kcols]
        mask = mask_b if reverse else mask_f
        s = 1
        while s < ch:
            if reverse:
                sh = pltpu.roll(b, blk - s, axis=0)
                b = b + jnp.where(row_in_chunk < ch - s, sh, 0.0)
            else:
                sh = pltpu.roll(b, s, axis=0)
                b = b + jnp.where(row_in_chunk >= s, sh, 0.0)
            s *= 2
        qe = (q * jnp.exp(b)).astype(BF16)
        ke = (k * jnp.exp(-b)).astype(BF16)
        scores = _dot_nt(qe, ke)
        end_row = [c * ch + (0 if reverse else ch - 1) for c in range(nch)]
        ends = [b[r:r + 1, :] for r in end_row]
        b_end = jnp.concatenate([jnp.broadcast_to(e, (ch, dk)) for e in ends], axis=0)
        kd = (k * jnp.exp(b_end - b)).astype(BF16)
        return dict(scores=scores, mask=mask, qe=qe, kd=kd, v=v, ends=ends, reverse=reverse)

    def chunk_products(w):
        return [lax.dot_general(w["kd"][c * ch:(c + 1) * ch], w["v"][c * ch:(c + 1) * ch],
                                (((0,), (0,)), ((), ())), preferred_element_type=F32) for c in range(nch)]

    def intra_chunk(w):
        return _dot(jnp.where(w["mask"], w["scores"], 0.0).astype(BF16), w["v"])

    def across_chunks(w, kv, state):
        decay = jnp.exp(jnp.concatenate(w["ends"] + [jnp.zeros((dk - nch, dk), F32)], axis=0)).T
        inter = [None] * nch
        for c in (range(nch - 1, -1, -1) if w["reverse"] else range(nch)):
            if state is None:
                inter[c] = jnp.zeros((ch, dv), F32)
                state = kv[c]
            else:
                inter[c] = _dot(w["qe"][c * ch:(c + 1) * ch], state.astype(BF16))
                state = decay[:, c:c + 1] * state + kv[c]
        return jnp.concatenate(inter, axis=0), state

    group = min(GLA_HEADS_PER_STAGE_GROUP, heads)
    for sq in range(seqs):
        srows = pl.ds(sq * nblk * blk, nblk * blk)
        for h0 in range(0, heads, group):
            hds = list(range(h0, h0 + group))
            states = {(hd, rev): ((s0b_ref if rev else s0f_ref)[sq, 0, hd] if has_state else None)
                      for hd in hds for rev in (False, True)}
            for i in range(nblk):
                block_of = {False: sq * nblk + i, True: sq * nblk + nblk - 1 - i}
                chains = [(hd, rev) for hd in hds for rev in (False, True)]
                work = {c: one_block(block_of[c[1]] * blk, c[0], c[1]) for c in chains}
                kvs = {c: chunk_products(work[c]) for c in chains}
                intra = {c: intra_chunk(work[c]) for c in chains}
                for c in chains:
                    hd, rev = c
                    inter, states[c] = across_chunks(work[c], kvs[c], states[c])
                    scr = ob_scr if rev else of_scr
                    scr[pl.ds(block_of[rev] * blk, blk), hd * dv:(hd + 1) * dv] = intra[c] + inter
            for hd in hds:
                vcols = slice(hd * dv, (hd + 1) * dv)
                if emit_state:
                    sf_ref[sq, 0, hd] = states[(hd, False)]
                    sb_ref[sq, 0, hd] = states[(hd, True)]
                o = of_scr[srows, vcols] + ob_scr[srows, vcols]
                og_ref[srows, vcols] = (_rms(o, gg_ref[...]) * rg_ref[srows, vcols].astype(F32)).astype(BF16)


def _gla_call(q_g, k_g, v_g, lg_f, lg_b, r_g, g_gla, state_f, state_b, batch, seq, emit_state, heads, seqs):
    n, gkw = q_g.shape
    d = v_g.shape[1]
    dk, dv = gkw // GLA_HEADS, d // GLA_HEADS
    has_state = state_f is not None
    nblk = seq // GLA_BLOCK
    batch = batch // seqs
    seq = seq * seqs
    kspec = pl.BlockSpec((seq, heads * dk), lambda b, h: (b, h))
    vspec = pl.BlockSpec((seq, heads * dv), lambda b, h: (b, h))
    sspec = pl.BlockSpec((seqs, 1, heads, dk, dv), lambda b, h: (b, 0, h, 0, 0))
    in_specs = [kspec, kspec, vspec, kspec, kspec, vspec, pl.BlockSpec((1, dv), lambda b, h: (0, 0))]
    args = [q_g, k_g, v_g, lg_f, lg_b, r_g, g_gla]
    if has_state:
        in_specs += [sspec, sspec]
        args += [state_f, state_b]
    out_shape = [jax.ShapeDtypeStruct((n, d), BF16)]
    out_specs = [vspec]
    if emit_state:
        st = jax.ShapeDtypeStruct((batch * seqs, 1, GLA_HEADS, dk, dv), F32)
        out_shape += [st, st]
        out_specs += [sspec, sspec]
    return pl.pallas_call(
        functools.partial(_gla_kernel, nblk=nblk, heads=heads, seqs=seqs, has_state=has_state,
                          emit_state=emit_state),
        out_shape=tuple(out_shape),
        grid=(batch, GLA_HEADS // heads),
        in_specs=in_specs,
        out_specs=tuple(out_specs),
        scratch_shapes=[pltpu.VMEM((seq, heads * dv), F32), pltpu.VMEM((seq, heads * dv), F32)],
        compiler_params=_cparams(("parallel", "parallel")),
        name="gla_ctx" if emit_state else "gla_lat",
    )(*args)


def _outproj_kernel(oa_ref, og_ref, gate_ref, x_ref, mod_ref, wpa_ref, wpg_ref, wout_ref, gpm_ref,
                    gpf_ref, wr_ref, x1_ref, h_ref, aff_ref):
    d = x_ref.shape[1]
    m = mod_ref[0]
    sub = x_ref.shape[0] // OUTPROJ_ROW_GROUPS
    groups = [pl.ds(g * sub, sub) for g in range(OUTPROJ_ROW_GROUPS)]
    wpa, wpg, wout = (w[...].astype(BF16) for w in (wpa_ref, wpg_ref, wout_ref))
    branch = [(_dot(oa_ref[r, :], wpa), _dot(og_ref[r, :], wpg)) for r in groups]
    mo = jnp.concatenate(
        [_dot((gate_ref[r, 0:d].astype(F32) * oa + gate_ref[r, d:2 * d].astype(F32) * og).astype(BF16), wout)
         for r, (oa, og) in zip(groups, branch)], axis=0)
    x1 = x_ref[...] + m[:, 2 * d:3 * d] * _rms(mo, gpm_ref[...])
    x1_ref[...] = x1
    hb = (_rms(x1, gpf_ref[...]) * (1.0 + m[:, 4 * d:5 * d]) + m[:, 3 * d:4 * d]).astype(BF16)
    bits = lax.bitcast_convert_type(hb.astype(F32), jnp.uint32)
    packed = (bits[:, 0:d // 2] >> 16) | (bits[:, d // 2:d] & jnp.uint32(0xFFFF0000))
    h_ref[...] = lax.bitcast_convert_type(packed, I32)
    wr = jnp.concatenate([wr_ref[...], jnp.zeros((V7X_LANES - wr_ref.shape[0], d), F32)], axis=0).astype(BF16)
    logits = _dot_nt(hb, wr)
    valid = lax.broadcasted_iota(I32, logits.shape, 1) < N_EXPERTS
    mx = jnp.max(jnp.where(valid, logits, -jnp.inf), axis=-1, keepdims=True)
    ex = jnp.where(valid, jnp.exp(logits - mx), 0.0)
    aff_ref[...] = ex / jnp.sum(ex, axis=-1, keepdims=True)


def _outproj_call(o_a, o_g, gates, x2, mod3, mod_row_of_tile, w_pa, w_pg, w_out, g_pm, g_pf, w_r, tag):
    n, d = x2.shape
    tm = OUTPROJ_TILE
    row = lambda i: (i, 0)
    const = lambda i: (0, 0)
    wspec = pl.BlockSpec((d, d), const, pipeline_mode=pl.Buffered(1))
    return pl.pallas_call(
        _outproj_kernel,
        out_shape=(jax.ShapeDtypeStruct((n, d), F32), jax.ShapeDtypeStruct((n, d // 2), I32),
                   jax.ShapeDtypeStruct((n, V7X_LANES), F32)),
        grid=(n // tm,),
        in_specs=[pl.BlockSpec((tm, d), row), pl.BlockSpec((tm, d), row), pl.BlockSpec((tm, 2 * d), row),
                  pl.BlockSpec((tm, d), row),
                  pl.BlockSpec((1, 1, 6 * d), lambda i: (mod_row_of_tile(i, tm), 0, 0)),
                  wspec, wspec, wspec, pl.BlockSpec((1, d), const), pl.BlockSpec((1, d), const),
                  pl.BlockSpec((N_EXPERTS, d), const)],
        out_specs=(pl.BlockSpec((tm, d), row), pl.BlockSpec((tm, d // 2), row),
                   pl.BlockSpec((tm, V7X_LANES), row)),
        compiler_params=_cparams(("parallel",)),
        name="outproj_" + tag,
    )(o_a, o_g, gates, x2, mod3, w_pa, w_pg, w_out, g_pm, g_pf, w_r)


def _route_kernel(aff_ref, pos_ref, post_ref, tbl_ref, afft_scr, *, cap):
    n = aff_ref.shape[0]
    rb = ROUTE_BLOCK
    nb = n // rb
    lanes = aff_ref.shape[1]
    lane = lax.broadcasted_iota(I32, (1, lanes), 1)
    expert_lane = lane < N_EXPERTS
    tbl_ref[...] = jnp.zeros(tbl_ref.shape, I32)

    def to_token_lanes(c, carry):
        start = pl.multiple_of(c * rb, rb)
        afft_scr[c] = aff_ref[pl.ds(start, rb), :].T[0:N_EXPERTS, :]
        return carry

    lax.fori_loop(0, nb, to_token_lanes, 0)
    aff_t = afft_scr[...]

    def count(hit):
        return jnp.sum(jnp.sum(hit.astype(I32), axis=0), axis=1, keepdims=True)

    def bit_step(i, lo):
        t = lo | jnp.left_shift(jnp.int32(1), 30 - i)
        ge = aff_t >= lax.bitcast_convert_type(t, F32)[None]
        return jnp.where(count(ge) >= cap, t, lo)

    thr_bits = lax.fori_loop(0, 31, bit_step, jnp.zeros((N_EXPERTS, 1), I32))
    need_t = cap - count(aff_t > lax.bitcast_convert_type(thr_bits, F32)[None])

    def to_expert_lanes(col):
        full = jnp.concatenate([jnp.broadcast_to(col, (N_EXPERTS, lanes)),
                                jnp.zeros((lanes - N_EXPERTS, lanes), I32)], axis=0)
        return full.T[0:1, :]

    thr = lax.bitcast_convert_type(to_expert_lanes(thr_bits), F32)
    need = to_expert_lanes(need_t).astype(F32)
    capf = float(cap)

    r = lax.broadcasted_iota(I32, (rb, rb), 0)
    c_ = lax.broadcasted_iota(I32, (rb, rb), 1)
    tril = jnp.where(c_ <= r, 1.0, 0.0).astype(BF16)

    def blk_step(c, carry):
        eq_before, raw_before = carry
        start = pl.multiple_of(c * rb, rb)
        a = aff_ref[pl.ds(start, rb), :]
        gt = a > thr
        eq = a == thr
        eq_incl = _dot(tril, jnp.where(eq, 1.0, 0.0).astype(BF16)) + eq_before
        raw = (gt | (eq & (eq_incl <= need))) & expert_lane
        raw_incl = _dot(tril, jnp.where(raw, 1.0, 0.0).astype(BF16)) + raw_before
        sel = raw & (raw_incl <= capf)
        self_ = jnp.where(sel, 1.0, 0.0)
        incl = jnp.minimum(raw_incl, capf)
        sel_before = jnp.minimum(raw_before, capf)
        excl = incl - self_
        posb = jnp.where(sel, excl, -1.0).astype(I32)
        pos_ref[pl.ds(start, rb), :] = posb
        post_ref[:, pl.ds(start, rb)] = posb.T[0:N_EXPERTS, :]
        tbl_ref[pl.ds(c, 1), :] = sel_before.astype(I32)
        return (eq_incl[rb - 1:rb, :], raw_incl[rb - 1:rb, :])

    zero = jnp.zeros((1, lanes), F32)
    _, total = lax.fori_loop(0, nb, blk_step, (zero, zero), unroll=2)
    tbl_ref[pl.ds(nb, 1), :] = jnp.minimum(total, capf).astype(I32)


def _route_call(aff, cap, tag):
    n, lanes = aff.shape
    nb = n // ROUTE_BLOCK
    tbl_rows = -(-(nb + 1) // 8) * 8
    full = lambda *shape: pl.BlockSpec(shape, lambda: tuple(0 for _ in shape))
    return pl.pallas_call(
        functools.partial(_route_kernel, cap=cap),
        out_shape=(jax.ShapeDtypeStruct((n, lanes), I32),
                   jax.ShapeDtypeStruct((N_EXPERTS, n), I32),
                   jax.ShapeDtypeStruct((tbl_rows, lanes), I32)),
        in_specs=[full(n, lanes)],
        out_specs=(full(n, lanes), full(N_EXPERTS, n), full(tbl_rows, lanes)),
        scratch_shapes=[pltpu.VMEM((nb, N_EXPERTS, ROUTE_BLOCK), F32)],
        compiler_params=_cparams(()),
        name="route_" + tag,
    )(aff)


def _sc_gather_call(table, post, cap):
    n_exp, n = post.shape
    words = table.shape[1]
    workers = V7X_SC_CORES * V7X_SC_SUBCORES
    parts = workers // n_exp
    chunk = SC_GATHER_CHUNK
    lanes = V7X_SC_LANES
    per_part = cap // parts
    assert parts * n_exp == workers and per_part % chunk == 0 and n % lanes == 0
    mesh = plsc.VectorSubcoreMesh(core_axis_name="c", subcore_axis_name="s",
                                  num_cores=V7X_SC_CORES, num_subcores=V7X_SC_SUBCORES)

    def body(table_hbm, post_hbm, out_hbm, pos_v, idx_v, rows_v, sem):
        wid = lax.axis_index("s") * V7X_SC_CORES + lax.axis_index("c")
        e = wid // parts
        part = wid % parts
        pltpu.sync_copy(post_hbm.at[e], pos_v)
        lane = lax.iota(I32, lanes)

        @pl.loop(0, n, step=lanes)
        def _(t0):
            p = pos_v[pl.ds(t0, lanes)]
            plsc.store_scatter(idx_v, [p], lane + t0, mask=p >= 0)

        @pl.loop(0, per_part // chunk)
        def _(j):
            off = pl.multiple_of(part * per_part + j * chunk, chunk)
            pltpu.async_copy(table_hbm.at[idx_v.at[pl.ds(off, chunk)]], rows_v, sem).wait()
            pltpu.sync_copy(rows_v, out_hbm.at[pl.ds(e * cap + off, chunk)])

    return pl.kernel(
        body,
        out_type=jax.ShapeDtypeStruct((n_exp * cap, words), table.dtype),
        mesh=mesh,
        scratch_types=[pltpu.VMEM((n,), I32), pltpu.VMEM((cap,), I32),
                       pltpu.VMEM((chunk, words), table.dtype), pltpu.SemaphoreType.DMA],
        compiler_params=pltpu.CompilerParams(needs_layout_passes=False),
        name="sc_gather",
    )(table, post)


def _ffn_kernel(*refs, caps):
    ng = len(caps)
    xs_refs = refs[:ng]
    w1_ref, w3_ref, w2_ref = refs[ng:ng + 3]
    ye_refs = refs[ng + 3:2 * ng + 3]
    row_off = [sum(caps[:g]) for g in range(ng)]

    def unpack(words):
        w = lax.bitcast_convert_type(words, jnp.uint32)
        lo = lax.bitcast_convert_type(w << 16, F32).astype(BF16)
        hi = lax.bitcast_convert_type(w & jnp.uint32(0xFFFF0000), F32).astype(BF16)
        return jnp.concatenate([lo, hi], axis=1)

    xs = jnp.concatenate([unpack(r[...]) for r in xs_refs], axis=0)
    hid = _silu(_dot(xs, w1_ref[...].astype(BF16))) * _dot(xs, w3_ref[...].astype(BF16))
    ye = _dot(hid.astype(BF16), w2_ref[...].astype(BF16)).astype(BF16)
    for g in range(ng):
        ye_refs[g][...] = ye[row_off[g]:row_off[g] + caps[g]]


def _ffn_call(groups, w1, w3, w2):
    caps = tuple(g[1] for g in groups)
    n_exp, d, dff = w1.shape
    in_specs = [pl.BlockSpec((cap, d // 2), lambda e: (e, 0)) for cap in caps]
    in_specs += [pl.BlockSpec((None, d, dff), lambda e: (e, 0, 0)),
                 pl.BlockSpec((None, d, dff), lambda e: (e, 0, 0)),
                 pl.BlockSpec((None, dff, d), lambda e: (e, 0, 0))]
    return pl.pallas_call(
        functools.partial(_ffn_kernel, caps=caps),
        out_shape=tuple(jax.ShapeDtypeStruct((n_exp * cap, d), BF16) for cap in caps),
        grid=(n_exp,),
        in_specs=in_specs,
        out_specs=tuple(pl.BlockSpec((cap, d), lambda e: (e, 0)) for cap in caps),
        compiler_params=_cparams(("arbitrary",)),
        name="ffn",
    )(*[g[0] for g in groups], w1, w3, w2)


def _combine_kernel(tbl_ref, ye_hbm, pos_ref, aff_ref, x1_ref, mod_ref, gpo_ref, y_ref, buf, sem, xbuf, xsem,
                    acc_scr, *, cap, blocks_per_tile):
    i = pl.program_id(0)
    nsteps = pl.num_programs(0)
    d = x1_ref.shape[1]
    lanes = pos_ref.shape[1]
    win = COMBINE_WINDOW
    last_start = ye_hbm.shape[0] - win
    slot = i % 2

    def first_row(step, e):
        return tbl_ref[step * blocks_per_tile, e] + e * cap

    def window_start(first, k):
        unclamped = (first // BF16_ROWS_PER_TILE) * BF16_ROWS_PER_TILE + k * win
        return unclamped, jnp.minimum(unclamped, last_start)

    def fetch(step, to_slot, e):
        start = window_start(first_row(step, e), 0)[1]
        return pltpu.make_async_copy(ye_hbm.at[pl.ds(pl.multiple_of(start, BF16_ROWS_PER_TILE), win), :],
                                     buf.at[to_slot, pl.ds(e * win, win), :], sem.at[to_slot, e])

    @pl.when(i == 0)
    def _prime():
        for e in range(N_EXPERTS):
            fetch(0, 0, e).start()

    @pl.when(i + 1 < nsteps)
    def _ahead():
        for e in range(N_EXPERTS):
            fetch(i + 1, 1 - slot, e).start()

    for e in range(N_EXPERTS):
        fetch(i, slot, e).wait()
    lane_row = lax.broadcasted_iota(I32, (1, win), 1)
    pieces = []
    for e in range(N_EXPERTS):
        pcol = pos_ref[:, e:e + 1]
        grow = jnp.where(pcol >= 0, pcol + e * cap, -1)
        start = window_start(first_row(i, e), 0)[1]
        pieces.append(jnp.where(grow == start + lane_row, aff_ref[:, e:e + 1], 0.0).astype(BF16))
    total = None
    for e0 in range(0, N_EXPERTS, COMBINE_EXPERT_GROUP):
        grp = jnp.concatenate(pieces[e0:e0 + COMBINE_EXPERT_GROUP], axis=1)
        part = _dot(grp, buf[slot, pl.ds(e0 * win, COMBINE_EXPERT_GROUP * win), :])
        total = part if total is None else total + part
    acc_scr[...] = total

    def extra_windows(e):
        covered = window_start(first_row(i, e), 1)[0]
        return jnp.maximum(first_row(i + 1, e) - covered + win - 1, 0) // win

    def expert_extra(e, carry):
        first = first_row(i, e)
        extra = extra_windows(e)

        def more(k, c):
            unclamped, start = window_start(first, k)
            cp = pltpu.make_async_copy(ye_hbm.at[pl.ds(pl.multiple_of(start, BF16_ROWS_PER_TILE), win), :],
                                       xbuf, xsem)
            cp.start()
            cp.wait()
            at_e = lax.broadcasted_iota(I32, (1, lanes), 1) == e
            pcol = jnp.sum(jnp.where(at_e, pos_ref[...].astype(F32), 0.0), axis=1, keepdims=True).astype(I32)
            wcol = jnp.sum(jnp.where(at_e, aff_ref[...], 0.0), axis=1, keepdims=True)
            grow = jnp.where(pcol >= 0, pcol + e * cap, -1)
            hit = (grow == start + lane_row) & (grow >= unclamped)
            acc_scr[...] += _dot(jnp.where(hit, wcol, 0.0).astype(BF16), xbuf[...])
            return c

        lax.fori_loop(1, 1 + extra, more, 0)
        return carry

    any_extra = extra_windows(0)
    for e in range(1, N_EXPERTS):
        any_extra = any_extra + extra_windows(e)

    @pl.when(any_extra > 0)
    def _overflow():
        lax.fori_loop(0, N_EXPERTS, expert_extra, 0)

    m = mod_ref[0]
    y_ref[...] = x1_ref[...] + m[:, 5 * d:6 * d] * _rms(acc_scr[...], gpo_ref[...])


def _combine_call(tbl, ye, pos, aff, x1, mod3, mod_row_of_tile, g_po, cap, tag):
    n, d = x1.shape
    tm = COMBINE_TILE
    lanes = pos.shape[1]
    grid_spec = pltpu.PrefetchScalarGridSpec(
        num_scalar_prefetch=1,
        grid=(n // tm,),
        in_specs=[pl.BlockSpec(memory_space=pl.ANY),
                  pl.BlockSpec((tm, lanes), lambda i, t: (i, 0)),
                  pl.BlockSpec((tm, lanes), lambda i, t: (i, 0)),
                  pl.BlockSpec((tm, d), lambda i, t: (i, 0)),
                  pl.BlockSpec((1, 1, 6 * d), lambda i, t: (mod_row_of_tile(i, tm), 0, 0)),
                  pl.BlockSpec((1, d), lambda i, t: (0, 0))],
        out_specs=pl.BlockSpec((tm, d), lambda i, t: (i, 0)),
        scratch_shapes=[pltpu.VMEM((2, N_EXPERTS * COMBINE_WINDOW, d), BF16),
                        pltpu.SemaphoreType.DMA((2, N_EXPERTS)),
                        pltpu.VMEM((COMBINE_WINDOW, d), BF16),
                        pltpu.SemaphoreType.DMA(()),
                        pltpu.VMEM((tm, d), F32)],
    )
    return pl.pallas_call(
        functools.partial(_combine_kernel, cap=cap, blocks_per_tile=tm // ROUTE_BLOCK),
        out_shape=jax.ShapeDtypeStruct((n, d), F32),
        grid_spec=grid_spec,
        compiler_params=_cparams(("arbitrary",)),
        name="combine_" + tag,
    )(tbl, ye, pos, aff, x1, mod3, g_po)


def _rope_tables(seq):
    f32 = np.float32
    rows = seq // GRID_W
    r = np.repeat(np.arange(rows), GRID_W).astype(f32)
    col = np.tile(np.arange(GRID_W), rows).astype(f32)
    pairs = HEAD_DIM // 4
    freqs = np.power(f32(ROPE_THETA), -np.arange(pairs, dtype=f32) / f32(pairs)).astype(f32)
    ang = np.concatenate([r[:, None] * freqs, col[:, None] * freqs], axis=-1).astype(f32)
    cos = np.repeat(np.cos(ang), 2, axis=-1).astype(f32)
    sin = np.repeat(np.sin(ang), 2, axis=-1).astype(f32)
    even = (np.arange(HEAD_DIM) % 2) == 0
    zero = f32(0.0)
    return (jnp.asarray(cos), jnp.asarray(np.where(even, -sin, zero).astype(f32)),
            jnp.asarray(np.where(even, zero, sin).astype(f32)))


def _trunk_to_routing(x, mod3, mod_row_of_tile, rope_tabs, ctx, lw, tag):
    (g_pre_mix, g_post_mix, g_pre_ffn, g_post_ffn, w_in_p, g_q, g_k, wgf, bgf, wgb, bgb, g_gla,
     w_pa, w_pg, w_out, w_r, w1, w3, w2) = lw
    batch, seq, d = x.shape
    n = batch * seq
    x2 = x.reshape(n, d)
    (q_a, k_a, v_a, q_g, k_g, v_g, r_g, lg_f, lg_b, gates) = _inproj_call(
        x2, mod3, mod_row_of_tile, g_pre_mix, w_in_p, g_q, g_k, wgf, bgf, wgb, bgb, rope_tabs, batch, seq)
    if ctx is None:
        o_a = _attn_call(q_a, k_a, v_a, None, None)
        o_g, s_f, s_b = _gla_call(q_g, k_g, v_g, lg_f, lg_b, r_g, g_gla, None, None, batch, seq, True,
                                  GLA_CTX_HEADS_PER_STEP, GLA_CTX_SEQS_PER_STEP)
    else:
        ck, cv, s_f0, s_b0 = ctx
        o_a = _attn_call(q_a, k_a, v_a, ck, cv)
        (o_g,) = _gla_call(q_g, k_g, v_g, lg_f, lg_b, r_g, g_gla, s_f0, s_b0, batch, seq, False,
                           GLA_LAT_HEADS_PER_STEP, 1)
        s_f = s_b = None
    x1, h, aff = _outproj_call(o_a, o_g, gates, x2, mod3, mod_row_of_tile, w_pa, w_pg, w_out,
                               g_post_mix, g_pre_ffn, w_r, tag)
    cap = (EC_CAPACITY_FACTOR * n) // N_EXPERTS
    pos, post, tbl = _route_call(aff, cap, tag)
    xs = _sc_gather_call(h, post, cap)
    return dict(x1=x1, xs=xs, aff=aff, pos=pos, tbl=tbl, cap=cap,
                mod_row_of_tile=mod_row_of_tile, tag=tag, shape=(batch, seq, d)), (k_a, v_a, s_f, s_b)


def _expert_ffn(groups, mod3, g_post_ffn, w1, w3, w2):
    yes = _ffn_call([(g["xs"], g["cap"]) for g in groups], w1, w3, w2)
    outs = []
    for g, ye in zip(groups, yes):
        y = _combine_call(g["tbl"], ye, g["pos"], g["aff"], g["x1"], mod3, g["mod_row_of_tile"], g_post_ffn,
                          g["cap"], g["tag"])
        outs.append(y.reshape(g["shape"]))
    return outs


def kernel(x_prompt, x_sample, cache_k, cache_v, state_gla_fwd, state_gla_bwd, c, c_ctx, g_pre_mix, g_post_mix, g_pre_ffn, g_post_ffn, w_mod, b_mod, w_in, g_q, g_k, w_gk2_f, b_gk_f, w_gk2_b, b_gk_b, g_gla, w_pa, w_pg, w_out, w_router, w1, w3, w2):
    depth = w_in.shape[0]
    assert depth == 1, "single trunk layer"
    d = x_prompt.shape[-1]
    dec_batch, dec_seq, _ = x_sample.shape
    assert dec_batch + 1 <= MOD_ROWS
    l = 0
    mod3, w_in_p = _mod_call(c_ctx.reshape(1, d), c, w_mod[l], b_mod[l].reshape(1, -1), jnp.swapaxes(w_in[l], 0, 1))
    row = lambda a: a[l].reshape(1, -1)
    lw = (row(g_pre_mix), row(g_post_mix), row(g_pre_ffn), row(g_post_ffn), w_in_p, row(g_q), row(g_k),
          w_gk2_f[l], row(b_gk_f), w_gk2_b[l], row(b_gk_b), row(g_gla),
          w_pa[l], w_pg[l], w_out[l], jnp.swapaxes(w_router[l], 0, 1), w1[l], w3[l], w2[l])

    gp, (nk, nv, nsf, nsb) = _trunk_to_routing(x_prompt, mod3, lambda i, tm: 0, None, None, lw, "ctx")
    ctx = (cache_k, cache_v, state_gla_fwd, state_gla_bwd)
    gs, _ = _trunk_to_routing(x_sample, mod3, lambda i, tm: 1 + (i * tm) // dec_seq, _rope_tables(dec_seq),
                              ctx, lw, "lat")
    yp, ys = _expert_ffn([gp, gs], mod3, lw[3], lw[16], lw[17], lw[18])
    return (yp, ys, nk, nv, nsf, nsb)
```

```python
import functools

import jax
import jax.numpy as jnp
import numpy as np
from jax import lax
from jax.experimental import pallas as pl
from jax.experimental.pallas import tpu as pltpu
from jax.experimental.pallas import tpu_sc as plsc

F32 = jnp.float32
BF16 = jnp.bfloat16
I32 = jnp.int32

N_HEADS = 8
N_KV_HEADS = 2
HEAD_DIM = 128
GRID_W = 64
ROPE_THETA = 10000.0
GLA_HEADS = 4
GLA_GATE_RANK = 16
GLA_GATE_NORM = 16.0
GLA_CHUNK = 64
N_EXPERTS = 16
EC_CAPACITY_FACTOR = 2
EPS = 1e-6

V7X_LANES = 128
V7X_VMEM_BYTES = 64 * 1024 * 1024
V7X_VMEM_RESERVE_BYTES = 6 * 1024 * 1024
BF16_ROWS_PER_TILE = 16
V7X_SC_CORES = 2
V7X_SC_SUBCORES = 16
V7X_SC_LANES = 16

TOKEN_TILE = 512
OUTPROJ_TILE = 512
OUTPROJ_ROW_GROUPS = 4
ATTN_Q_TILE = 256
ATTN_SEQS_PER_STEP = 8
GLA_BLOCK = 256
GLA_HEADS_PER_STAGE_GROUP = 2
GLA_CTX_HEADS_PER_STEP = 4
GLA_CTX_SEQS_PER_STEP = 4
GLA_LAT_HEADS_PER_STEP = 4
ROUTE_BLOCK = 256
SC_GATHER_CHUNK = 128
COMBINE_TILE = 512
COMBINE_WINDOW = 128
COMBINE_EXPERT_GROUP = 2
MOD_ROWS = 8
MOD_STEPS = 12
MOD_WEIGHT_SLABS = 11


def _cparams(semantics):
    return pltpu.CompilerParams(dimension_semantics=semantics,
                                vmem_limit_bytes=V7X_VMEM_BYTES - V7X_VMEM_RESERVE_BYTES)


def _sigmoid(x):
    return 0.5 * (jnp.tanh(0.5 * x) + 1.0)


def _silu(x):
    return x * _sigmoid(x)


def _log_sigmoid(x):
    return jnp.minimum(x, 0.0) - jnp.log1p(jnp.exp(-jnp.abs(x)))


def _rms(x, g):
    ms = jnp.mean(x * x, axis=-1, keepdims=True)
    return x * lax.rsqrt(ms + EPS) * g


def _dot(a, b):
    return jnp.dot(a, b, preferred_element_type=F32)


def _dot_nt(a, b):
    return lax.dot_general(a, b, (((1,), (1,)), ((), ())), preferred_element_type=F32)


def _mod_kernel(cctx_ref, c_ref, w_ref, b_ref, win_ref, o_ref, wout_ref):
    rows, d = o_ref.shape[0], c_ref.shape[1]
    ridx = lax.broadcasted_iota(I32, (rows, d), 0)
    cc = jnp.where(ridx == 0, jnp.broadcast_to(cctx_ref[...], (rows, d)), 0.0)
    for b in range(c_ref.shape[0]):
        cc = jnp.where(ridx == b + 1, jnp.broadcast_to(c_ref[b:b + 1, :], (rows, d)), cc)
    out = _dot(_silu(cc).astype(BF16), w_ref[...].astype(BF16)) + b_ref[...]
    for r in range(rows):
        o_ref[r] = out[r:r + 1, :]
    wout_ref[...] = win_ref[...].astype(BF16)


def _mod_call(c_ctx, c, w_mod, b_mod, w_in_t):
    d, n6 = w_mod.shape
    dec_batch = c.shape[0]
    din = w_in_t.shape[0]
    steps = MOD_STEPS
    tn = n6 // steps
    slabs = MOD_WEIGHT_SLABS
    rows = din // slabs
    assert tn * steps == n6 and slabs * rows == din and rows % BF16_ROWS_PER_TILE == 0 and slabs <= steps
    slab = lambda j: (jnp.minimum(j, slabs - 1), 0)
    return pl.pallas_call(
        _mod_kernel,
        out_shape=(jax.ShapeDtypeStruct((MOD_ROWS, 1, n6), F32), jax.ShapeDtypeStruct((din, d), BF16)),
        grid=(steps,),
        in_specs=[pl.BlockSpec((1, d), lambda j: (0, 0)),
                  pl.BlockSpec((dec_batch, d), lambda j: (0, 0)),
                  pl.BlockSpec((d, tn), lambda j: (0, j)),
                  pl.BlockSpec((1, tn), lambda j: (0, j)),
                  pl.BlockSpec((rows, d), slab)],
        out_specs=(pl.BlockSpec((MOD_ROWS, 1, tn), lambda j: (0, 0, j)), pl.BlockSpec((rows, d), slab)),
        compiler_params=_cparams(("arbitrary",)),
        name="mod",
    )(c_ctx, c, w_mod, b_mod, w_in_t)


def _inproj_layout(d):
    aq, akv = N_HEADS * HEAD_DIM, N_KV_HEADS * HEAD_DIM
    gk, gv = d // 2, d
    names = ("q_a", "k_a", "v_a", "q_g", "k_g", "v_g", "r_g", "gk_f", "gk_b", "gates")
    widths = (aq, akv, akv, gk, gk, gv, gv, GLA_GATE_RANK, GLA_GATE_RANK, 2 * d)
    off, o = {}, 0
    for nme, w in zip(names, widths):
        off[nme] = (o, o + w)
        o += w
    off["gk"] = (off["gk_f"][0], off["gk_f"][0] + V7X_LANES)
    return off, o


def _inproj_kernel(*refs, rope, d):
    if rope:
        (x_ref, mod_ref, gpre_ref, w_ref, gq_ref, gk_ref, wgf_ref, bgf_ref, wgb_ref, bgb_ref,
         cos_ref, se_ref, so_ref, *outs) = refs
    else:
        (x_ref, mod_ref, gpre_ref, w_ref, gq_ref, gk_ref, wgf_ref, bgf_ref, wgb_ref, bgb_ref,
         *outs) = refs
    qa_ref, k_ref, v_ref, qg_ref, kg_ref, vg_ref, rg_ref, lgf_ref, lgb_ref, gate_ref = outs
    off, _ = _inproj_layout(d)
    m = mod_ref[0]
    h = _rms(x_ref[...], gpre_ref[...]) * (1.0 + m[:, d:2 * d]) + m[:, 0:d]
    hb = h.astype(BF16)

    def proj(name):
        a, b = off[name]
        return _dot_nt(hb, w_ref[a:b, :])

    def qk_norm(y, g_ref):
        y = _rms(y, g_ref[...])
        if rope:
            nxt = pltpu.roll(y, HEAD_DIM - 1, axis=1)
            prv = pltpu.roll(y, 1, axis=1)
            y = y * cos_ref[...] + nxt * se_ref[...] + prv * so_ref[...]
        return y

    gk = proj("gk").astype(BF16)

    q = proj("q_a")
    scale = HEAD_DIM ** -0.5
    for hd in range(N_HEADS):
        sl = slice(hd * HEAD_DIM, (hd + 1) * HEAD_DIM)
        qa_ref[:, sl] = (qk_norm(q[:, sl], gq_ref) * scale).astype(BF16)

    k = proj("k_a")
    v = proj("v_a")
    tb, _, _, ts, _ = k_ref.shape
    for kv in range(N_KV_HEADS):
        sl = slice(kv * HEAD_DIM, (kv + 1) * HEAD_DIM)
        k_ref[:, 0, kv] = qk_norm(k[:, sl], gk_ref).reshape(tb, ts, HEAD_DIM)
        v_ref[:, 0, kv] = v[:, sl].reshape(tb, ts, HEAD_DIM)

    dk = (d // 2) // GLA_HEADS
    qg_ref[...] = (proj("q_g") * (dk ** -0.5)).astype(BF16)
    kg_ref[...] = proj("k_g").astype(BF16)
    vg_ref[...] = proj("v_g").astype(BF16)
    rg_ref[...] = _silu(proj("r_g")).astype(BF16)

    gate_ref[...] = _sigmoid(proj("gates")).astype(BF16)

    def rank_rows(w_ref, at):
        rank, width = w_ref.shape
        parts = [jnp.zeros((at, width), F32)] if at else []
        parts += [w_ref[...], jnp.zeros((V7X_LANES - at - rank, width), F32)]
        return jnp.concatenate(parts, axis=0).astype(BF16)

    lgf_ref[...] = _log_sigmoid(_dot(gk, rank_rows(wgf_ref, 0)) + bgf_ref[...]) * (1.0 / GLA_GATE_NORM)
    lgb_ref[...] = (_log_sigmoid(_dot(gk, rank_rows(wgb_ref, GLA_GATE_RANK)) + bgb_ref[...])
                    * (1.0 / GLA_GATE_NORM))


def _inproj_call(x2, mod3, mod_row_of_tile, g_pre, w_in_p, g_q, g_k, wgf, bgf, wgb, bgb, rope_tabs,
                 batch, seq):
    n, d = x2.shape
    tm = TOKEN_TILE
    _, dinp = _inproj_layout(d)
    rope = rope_tabs is not None
    gk_w = d // 2
    if seq >= tm:
        tb, ts, per = 1, tm, seq // tm
        kv_map = lambda i: (i // per, 0, 0, i % per, 0)
    else:
        tb, ts, per = tm // seq, seq, 1
        kv_map = lambda i: (i, 0, 0, 0, 0)
    row = lambda i: (i, 0)
    const = lambda i: (0, 0)
    in_specs = [
        pl.BlockSpec((tm, d), row),
        pl.BlockSpec((1, 1, 6 * d), lambda i: (mod_row_of_tile(i, tm), 0, 0)),
        pl.BlockSpec((1, d), const),
        pl.BlockSpec((dinp, d), const, pipeline_mode=pl.Buffered(1)),
        pl.BlockSpec((1, HEAD_DIM), const),
        pl.BlockSpec((1, HEAD_DIM), const),
        pl.BlockSpec((GLA_GATE_RANK, gk_w), const),
        pl.BlockSpec((1, gk_w), const),
        pl.BlockSpec((GLA_GATE_RANK, gk_w), const),
        pl.BlockSpec((1, gk_w), const),
    ]
    args = [x2, mod3, g_pre, w_in_p, g_q, g_k, wgf, bgf, wgb, bgb]
    if rope:
        tab = pl.BlockSpec((tm, HEAD_DIM), lambda i: (i % per, 0))
        in_specs += [tab, tab, tab]
        args += list(rope_tabs)
    kv_shape = jax.ShapeDtypeStruct((batch, 1, N_KV_HEADS, seq, HEAD_DIM), F32)
    kv_spec = pl.BlockSpec((tb, 1, N_KV_HEADS, ts, HEAD_DIM), kv_map)
    out_shape = (
        jax.ShapeDtypeStruct((n, N_HEADS * HEAD_DIM), BF16), kv_shape, kv_shape,
        jax.ShapeDtypeStruct((n, gk_w), BF16), jax.ShapeDtypeStruct((n, gk_w), BF16),
        jax.ShapeDtypeStruct((n, d), BF16), jax.ShapeDtypeStruct((n, d), BF16),
        jax.ShapeDtypeStruct((n, gk_w), F32), jax.ShapeDtypeStruct((n, gk_w), F32),
        jax.ShapeDtypeStruct((n, 2 * d), BF16),
    )
    out_specs = (
        pl.BlockSpec((tm, N_HEADS * HEAD_DIM), row), kv_spec, kv_spec,
        pl.BlockSpec((tm, gk_w), row), pl.BlockSpec((tm, gk_w), row),
        pl.BlockSpec((tm, d), row), pl.BlockSpec((tm, d), row),
        pl.BlockSpec((tm, gk_w), row), pl.BlockSpec((tm, gk_w), row),
        pl.BlockSpec((tm, 2 * d), row),
    )
    return pl.pallas_call(
        functools.partial(_inproj_kernel, rope=rope, d=d),
        out_shape=out_shape,
        grid=(n // tm,),
        in_specs=in_specs,
        out_specs=out_specs,
        compiler_params=_cparams(("parallel",)),
        name="inproj_lat" if rope else "inproj_ctx",
    )(*args)


def _attn_kernel(*refs, cached):
    if cached:
        q_ref, k_ref, v_ref, ck_ref, cv_ref, o_ref = refs
    else:
        q_ref, k_ref, v_ref, o_ref = refs
    seqs = k_ref.shape[0]
    tq = q_ref.shape[0] // seqs
    grp = N_HEADS // N_KV_HEADS
    chains = [(sq, kv) for sq in range(seqs) for kv in range(N_KV_HEADS)]

    def scores_of(sq, kv):
        rows = pl.ds(sq * tq, tq)
        kk = k_ref[sq, 0, kv].astype(BF16)
        vv = v_ref[sq, 0, kv].astype(BF16)
        if cached:
            kk = jnp.concatenate([ck_ref[sq, 0, kv].astype(BF16), kk], axis=0)
            vv = jnp.concatenate([cv_ref[sq, 0, kv].astype(BF16), vv], axis=0)
        heads = [q_ref[rows, (kv * grp + g) * HEAD_DIM:(kv * grp + g + 1) * HEAD_DIM] for g in range(grp)]
        return _dot_nt(jnp.concatenate(heads, axis=0), kk), vv

    look_ahead = cached
    ahead = scores_of(*chains[0]) if look_ahead else None
    for n, (sq, kv) in enumerate(chains):
        s, vv = ahead if look_ahead else scores_of(sq, kv)
        if look_ahead and n + 1 < len(chains):
            ahead = scores_of(*chains[n + 1])
        rows = pl.ds(sq * tq, tq)
        p = jnp.exp(s - jnp.max(s, axis=-1, keepdims=True))
        l = jnp.sum(p, axis=-1, keepdims=True)
        o = _dot(p.astype(BF16), vv) / l
        for g in range(grp):
            hd = kv * grp + g
            o_ref[rows, hd * HEAD_DIM:(hd + 1) * HEAD_DIM] = o[g * tq:(g + 1) * tq].astype(BF16)


def _attn_call(q_a, k_a, v_a, cache_k, cache_v):
    batch, _, _, seq, _ = k_a.shape
    n, aq = q_a.shape
    tq = min(ATTN_Q_TILE, seq)
    per = seq // tq
    cached = cache_k is not None
    seqs = ATTN_SEQS_PER_STEP if (per == 1 and not cached) else 1
    batch = batch // seqs
    own = pl.BlockSpec((seqs, 1, N_KV_HEADS, seq, HEAD_DIM), lambda b, j: (b, 0, 0, 0, 0))
    tq = tq * seqs
    in_specs = [pl.BlockSpec((tq, aq), lambda b, j: (b * per + j, 0)), own, own]
    args = [q_a, k_a, v_a]
    if cached:
        past = cache_k.shape[3]
        cspec = pl.BlockSpec((1, 1, N_KV_HEADS, past, HEAD_DIM), lambda b, j: (b, 0, 0, 0, 0))
        in_specs += [cspec, cspec]
        args += [cache_k, cache_v]
    return pl.pallas_call(
        functools.partial(_attn_kernel, cached=cached),
        out_shape=jax.ShapeDtypeStruct((n, aq), BF16),
        grid=(batch, per),
        in_specs=in_specs,
        out_specs=pl.BlockSpec((tq, aq), lambda b, j: (b * per + j, 0)),
        compiler_params=_cparams(("parallel", "parallel")),
        name="attn_lat" if cached else "attn_ctx",
    )(*args)


def _gla_kernel(*refs, nblk, heads, seqs, has_state, emit_state):
    refs = list(refs)
    q_ref, k_ref, v_ref, lgf_ref, lgb_ref, rg_ref, gg_ref = refs[:7]
    pos = 7
    if has_state:
        s0f_ref, s0b_ref = refs[pos:pos + 2]
        pos += 2
    og_ref = refs[pos]
    pos += 1
    if emit_state:
        sf_ref, sb_ref = refs[pos:pos + 2]
        pos += 2
    of_scr, ob_scr = refs[pos:pos + 2]

    blk = GLA_BLOCK
    ch = GLA_CHUNK
    nch = blk // ch
    dk = q_ref.shape[1] // heads
    dv = v_ref.shape[1] // heads
    shift = ch.bit_length() - 1
    row_in_chunk = lax.broadcasted_iota(I32, (blk, dk), 0) & (ch - 1)
    ri = lax.broadcasted_iota(I32, (blk, blk), 0)
    ci = lax.broadcasted_iota(I32, (blk, blk), 1)
    same = (ri >> shift) == (ci >> shift)
    mask_f = same & (ci <= ri)
    mask_b = same & (ci >= ri)

    def one_block(b0, hd, reverse):
        rows = pl.ds(b0, blk)
        kcols = slice(hd * dk, (hd + 1) * dk)
        q = q_ref[rows, kcols].astype(F32)
        k = k_ref[rows, kcols].astype(F32)
        v = v_ref[rows, hd * dv:(hd + 1) * dv]
        b = (lgb_ref if reverse else lgf_ref)[rows, kcols]
        mask = mask_b if reverse else mask_f
        s = 1
        while s < ch:
            if reverse:
                sh = pltpu.roll(b, blk - s, axis=0)
                b = b + jnp.where(row_in_chunk < ch - s, sh, 0.0)
            else:
                sh = pltpu.roll(b, s, axis=0)
                b = b + jnp.where(row_in_chunk >= s, sh, 0.0)
            s *= 2
        qe = (q * jnp.exp(b)).astype(BF16)
        ke = (k * jnp.exp(-b)).astype(BF16)
        scores = _dot_nt(qe, ke)
        end_row = [c * ch + (0 if reverse else ch - 1) for c in range(nch)]
        ends = [b[r:r + 1, :] for r in end_row]
        b_end = jnp.concatenate([jnp.broadcast_to(e, (ch, dk)) for e in ends], axis=0)
        kd = (k * jnp.exp(b_end - b)).astype(BF16)
        return dict(scores=scores, mask=mask, qe=qe, kd=kd, v=v, ends=ends, reverse=reverse)

    def chunk_products(w):
        return [lax.dot_general(w["kd"][c * ch:(c + 1) * ch], w["v"][c * ch:(c + 1) * ch],
                                (((0,), (0,)), ((), ())), preferred_element_type=F32) for c in range(nch)]

    def intra_chunk(w):
        return _dot(jnp.where(w["mask"], w["scores"], 0.0).astype(BF16), w["v"])

    def across_chunks(w, kv, state):
        decay = jnp.exp(jnp.concatenate(w["ends"] + [jnp.zeros((dk - nch, dk), F32)], axis=0)).T
        inter = [None] * nch
        for c in (range(nch - 1, -1, -1) if w["reverse"] else range(nch)):
            if state is None:
                inter[c] = jnp.zeros((ch, dv), F32)
                state = kv[c]
            else:
                inter[c] = _dot(w["qe"][c * ch:(c + 1) * ch], state.astype(BF16))
                state = decay[:, c:c + 1] * state + kv[c]
        return jnp.concatenate(inter, axis=0), state

    group = min(GLA_HEADS_PER_STAGE_GROUP, heads)
    for sq in range(seqs):
        srows = pl.ds(sq * nblk * blk, nblk * blk)
        for h0 in range(0, heads, group):
            hds = list(range(h0, h0 + group))
            states = {(hd, rev): ((s0b_ref if rev else s0f_ref)[sq, 0, hd] if has_state else None)
                      for hd in hds for rev in (False, True)}
            for i in range(nblk):
                block_of = {False: sq * nblk + i, True: sq * nblk + nblk - 1 - i}
                chains = [(hd, rev) for hd in hds for rev in (False, True)]
                work = {c: one_block(block_of[c[1]] * blk, c[0], c[1]) for c in chains}
                kvs = {c: chunk_products(work[c]) for c in chains}
                intra = {c: intra_chunk(work[c]) for c in chains}
                for c in chains:
                    hd, rev = c
                    inter, states[c] = across_chunks(work[c], kvs[c], states[c])
                    scr = ob_scr if rev else of_scr
                    scr[pl.ds(block_of[rev] * blk, blk), hd * dv:(hd + 1) * dv] = intra[c] + inter
            for hd in hds:
                vcols = slice(hd * dv, (hd + 1) * dv)
                if emit_state:
                    sf_ref[sq, 0, hd] = states[(hd, False)]
                    sb_ref[sq, 0, hd] = states[(hd, True)]
                o = of_scr[srows, vcols] + ob_scr[srows, vcols]
                og_ref[srows, vcols] = (_rms(o, gg_ref[...]) * rg_ref[srows, vcols].astype(F32)).astype(BF16)


def _gla_call(q_g, k_g, v_g, lg_f, lg_b, r_g, g_gla, state_f, state_b, batch, seq, emit_state, heads, seqs):
    n, gkw = q_g.shape
    d = v_g.shape[1]
    dk, dv = gkw // GLA_HEADS, d // GLA_HEADS
    has_state = state_f is not None
    nblk = seq // GLA_BLOCK
    batch = batch // seqs
    seq = seq * seqs
    kspec = pl.BlockSpec((seq, heads * dk), lambda b, h: (b, h))
    vspec = pl.BlockSpec((seq, heads * dv), lambda b, h: (b, h))
    sspec = pl.BlockSpec((seqs, 1, heads, dk, dv), lambda b, h: (b, 0, h, 0, 0))
    in_specs = [kspec, kspec, vspec, kspec, kspec, vspec, pl.BlockSpec((1, dv), lambda b, h: (0, 0))]
    args = [q_g, k_g, v_g, lg_f, lg_b, r_g, g_gla]
    if has_state:
        in_specs += [sspec, sspec]
        args += [state_f, state_b]
    out_shape = [jax.ShapeDtypeStruct((n, d), BF16)]
    out_specs = [vspec]
    if emit_state:
        st = jax.ShapeDtypeStruct((batch * seqs, 1, GLA_HEADS, dk, dv), F32)
        out_shape += [st, st]
        out_specs += [sspec, sspec]
    return pl.pallas_call(
        functools.partial(_gla_kernel, nblk=nblk, heads=heads, seqs=seqs, has_state=has_state,
                          emit_state=emit_state),
        out_shape=tuple(out_shape),
        grid=(batch, GLA_HEADS // heads),
        in_specs=in_specs,
        out_specs=tuple(out_specs),
        scratch_shapes=[pltpu.VMEM((seq, heads * dv), F32), pltpu.VMEM((seq, heads * dv), F32)],
        compiler_params=_cparams(("parallel", "parallel")),
        name="gla_ctx" if emit_state else "gla_lat",
    )(*args)


def _outproj_kernel(oa_ref, og_ref, gate_ref, x_ref, mod_ref, wpa_ref, wpg_ref, wout_ref, gpm_ref,
                    gpf_ref, wr_ref, x1_ref, h_ref, aff_ref):
    d = x_ref.shape[1]
    m = mod_ref[0]
    sub = x_ref.shape[0] // OUTPROJ_ROW_GROUPS
    groups = [pl.ds(g * sub, sub) for g in range(OUTPROJ_ROW_GROUPS)]
    wpa, wpg, wout = (w[...].astype(BF16) for w in (wpa_ref, wpg_ref, wout_ref))
    branch = [(_dot(oa_ref[r, :], wpa), _dot(og_ref[r, :], wpg)) for r in groups]
    mo = jnp.concatenate(
        [_dot((gate_ref[r, 0:d].astype(F32) * oa + gate_ref[r, d:2 * d].astype(F32) * og).astype(BF16), wout)
         for r, (oa, og) in zip(groups, branch)], axis=0)
    x1 = x_ref[...] + m[:, 2 * d:3 * d] * _rms(mo, gpm_ref[...])
    x1_ref[...] = x1
    hb = (_rms(x1, gpf_ref[...]) * (1.0 + m[:, 4 * d:5 * d]) + m[:, 3 * d:4 * d]).astype(BF16)
    bits = lax.bitcast_convert_type(hb.astype(F32), jnp.uint32)
    packed = (bits[:, 0:d // 2] >> 16) | (bits[:, d // 2:d] & jnp.uint32(0xFFFF0000))
    h_ref[...] = lax.bitcast_convert_type(packed, I32)
    wr = jnp.concatenate([wr_ref[...], jnp.zeros((V7X_LANES - wr_ref.shape[0], d), F32)], axis=0).astype(BF16)
    logits = _dot_nt(hb, wr)
    valid = lax.broadcasted_iota(I32, logits.shape, 1) < N_EXPERTS
    mx = jnp.max(jnp.where(valid, logits, -jnp.inf), axis=-1, keepdims=True)
    ex = jnp.where(valid, jnp.exp(logits - mx), 0.0)
    aff_ref[...] = ex / jnp.sum(ex, axis=-1, keepdims=True)


def _outproj_call(o_a, o_g, gates, x2, mod3, mod_row_of_tile, w_pa, w_pg, w_out, g_pm, g_pf, w_r, tag):
    n, d = x2.shape
    tm = OUTPROJ_TILE
    row = lambda i: (i, 0)
    const = lambda i: (0, 0)
    wspec = pl.BlockSpec((d, d), const, pipeline_mode=pl.Buffered(1))
    return pl.pallas_call(
        _outproj_kernel,
        out_shape=(jax.ShapeDtypeStruct((n, d), F32), jax.ShapeDtypeStruct((n, d // 2), I32),
                   jax.ShapeDtypeStruct((n, V7X_LANES), F32)),
        grid=(n // tm,),
        in_specs=[pl.BlockSpec((tm, d), row), pl.BlockSpec((tm, d), row), pl.BlockSpec((tm, 2 * d), row),
                  pl.BlockSpec((tm, d), row),
                  pl.BlockSpec((1, 1, 6 * d), lambda i: (mod_row_of_tile(i, tm), 0, 0)),
                  wspec, wspec, wspec, pl.BlockSpec((1, d), const), pl.BlockSpec((1, d), const),
                  pl.BlockSpec((N_EXPERTS, d), const)],
        out_specs=(pl.BlockSpec((tm, d), row), pl.BlockSpec((tm, d // 2), row),
                   pl.BlockSpec((tm, V7X_LANES), row)),
        compiler_params=_cparams(("parallel",)),
        name="outproj_" + tag,
    )(o_a, o_g, gates, x2, mod3, w_pa, w_pg, w_out, g_pm, g_pf, w_r)


def _route_kernel(aff_ref, pos_ref, post_ref, tbl_ref, afft_scr, *, cap):
    n = aff_ref.shape[0]
    rb = ROUTE_BLOCK
    nb = n // rb
    lanes = aff_ref.shape[1]
    lane = lax.broadcasted_iota(I32, (1, lanes), 1)
    expert_lane = lane < N_EXPERTS
    tbl_ref[...] = jnp.zeros(tbl_ref.shape, I32)

    def to_token_lanes(c, carry):
        start = pl.multiple_of(c * rb, rb)
        afft_scr[c] = aff_ref[pl.ds(start, rb), :].T[0:N_EXPERTS, :]
        return carry

    lax.fori_loop(0, nb, to_token_lanes, 0)
    aff_t = afft_scr[...]

    def count(hit):
        return jnp.sum(jnp.sum(hit.astype(I32), axis=0), axis=1, keepdims=True)

    def bit_step(i, lo):
        t = lo | jnp.left_shift(jnp.int32(1), 30 - i)
        ge = aff_t >= lax.bitcast_convert_type(t, F32)[None]
        return jnp.where(count(ge) >= cap, t, lo)

    thr_bits = lax.fori_loop(0, 31, bit_step, jnp.zeros((N_EXPERTS, 1), I32))
    need_t = cap - count(aff_t > lax.bitcast_convert_type(thr_bits, F32)[None])

    def to_expert_lanes(col):
        full = jnp.concatenate([jnp.broadcast_to(col, (N_EXPERTS, lanes)),
                                jnp.zeros((lanes - N_EXPERTS, lanes), I32)], axis=0)
        return full.T[0:1, :]

    thr = lax.bitcast_convert_type(to_expert_lanes(thr_bits), F32)
    need = to_expert_lanes(need_t).astype(F32)
    capf = float(cap)

    r = lax.broadcasted_iota(I32, (rb, rb), 0)
    c_ = lax.broadcasted_iota(I32, (rb, rb), 1)
    tril = jnp.where(c_ <= r, 1.0, 0.0).astype(BF16)

    def blk_step(c, carry):
        eq_before, raw_before = carry
        start = pl.multiple_of(c * rb, rb)
        a = aff_ref[pl.ds(start, rb), :]
        gt = a > thr
        eq = a == thr
        eq_incl = _dot(tril, jnp.where(eq, 1.0, 0.0).astype(BF16)) + eq_before
        raw = (gt | (eq & (eq_incl <= need))) & expert_lane
        raw_incl = _dot(tril, jnp.where(raw, 1.0, 0.0).astype(BF16)) + raw_before
        sel = raw & (raw_incl <= capf)
        self_ = jnp.where(sel, 1.0, 0.0)
        incl = jnp.minimum(raw_incl, capf)
        sel_before = jnp.minimum(raw_before, capf)
        excl = incl - self_
        posb = jnp.where(sel, excl, -1.0).astype(I32)
        pos_ref[pl.ds(start, rb), :] = posb
        post_ref[:, pl.ds(start, rb)] = posb.T[0:N_EXPERTS, :]
        tbl_ref[pl.ds(c, 1), :] = sel_before.astype(I32)
        return (eq_incl[rb - 1:rb, :], raw_incl[rb - 1:rb, :])

    zero = jnp.zeros((1, lanes), F32)
    _, total = lax.fori_loop(0, nb, blk_step, (zero, zero), unroll=2)
    tbl_ref[pl.ds(nb, 1), :] = jnp.minimum(total, capf).astype(I32)


def _route_call(aff, cap, tag):
    n, lanes = aff.shape
    nb = n // ROUTE_BLOCK
    tbl_rows = -(-(nb + 1) // 8) * 8
    full = lambda *shape: pl.BlockSpec(shape, lambda: tuple(0 for _ in shape))
    return pl.pallas_call(
        functools.partial(_route_kernel, cap=cap),
        out_shape=(jax.ShapeDtypeStruct((n, lanes), I32),
                   jax.ShapeDtypeStruct((N_EXPERTS, n), I32),
                   jax.ShapeDtypeStruct((tbl_rows, lanes), I32)),
        in_specs=[full(n, lanes)],
        out_specs=(full(n, lanes), full(N_EXPERTS, n), full(tbl_rows, lanes)),
        scratch_shapes=[pltpu.VMEM((nb, N_EXPERTS, ROUTE_BLOCK), F32)],
        compiler_params=_cparams(()),
        name="route_" + tag,
    )(aff)


def _sc_gather_call(table, post, cap):
    n_exp, n = post.shape
    words = table.shape[1]
    workers = V7X_SC_CORES * V7X_SC_SUBCORES
    parts = workers // n_exp
    chunk = SC_GATHER_CHUNK
    lanes = V7X_SC_LANES
    per_part = cap // parts
    assert parts * n_exp == workers and per_part % chunk == 0 and n % lanes == 0
    mesh = plsc.VectorSubcoreMesh(core_axis_name="c", subcore_axis_name="s",
                                  num_cores=V7X_SC_CORES, num_subcores=V7X_SC_SUBCORES)

    def body(table_hbm, post_hbm, out_hbm, pos_v, idx_v, rows_v, sem):
        wid = lax.axis_index("s") * V7X_SC_CORES + lax.axis_index("c")
        e = wid // parts
        part = wid % parts
        pltpu.sync_copy(post_hbm.at[e], pos_v)
        lane = lax.iota(I32, lanes)

        @pl.loop(0, n, step=lanes)
        def _(t0):
            p = pos_v[pl.ds(t0, lanes)]
            plsc.store_scatter(idx_v, [p], lane + t0, mask=p >= 0)

        @pl.loop(0, per_part // chunk)
        def _(j):
            off = pl.multiple_of(part * per_part + j * chunk, chunk)
            pltpu.async_copy(table_hbm.at[idx_v.at[pl.ds(off, chunk)]], rows_v, sem).wait()
            pltpu.sync_copy(rows_v, out_hbm.at[pl.ds(e * cap + off, chunk)])

    return pl.kernel(
        body,
        out_type=jax.ShapeDtypeStruct((n_exp * cap, words), table.dtype),
        mesh=mesh,
        scratch_types=[pltpu.VMEM((n,), I32), pltpu.VMEM((cap,), I32),
                       pltpu.VMEM((chunk, words), table.dtype), pltpu.SemaphoreType.DMA],
        compiler_params=pltpu.CompilerParams(needs_layout_passes=False),
        name="sc_gather",
    )(table, post)


def _ffn_kernel(*refs, caps):
    ng = len(caps)
    xs_refs = refs[:ng]
    w1_ref, w3_ref, w2_ref = refs[ng:ng + 3]
    ye_refs = refs[ng + 3:2 * ng + 3]
    row_off = [sum(caps[:g]) for g in range(ng)]

    def unpack(words):
        w = lax.bitcast_convert_type(words, jnp.uint32)
        lo = lax.bitcast_convert_type(w << 16, F32).astype(BF16)
        hi = lax.bitcast_convert_type(w & jnp.uint32(0xFFFF0000), F32).astype(BF16)
        return jnp.concatenate([lo, hi], axis=1)

    xs = jnp.concatenate([unpack(r[...]) for r in xs_refs], axis=0)
    hid = _silu(_dot(xs, w1_ref[...].astype(BF16))) * _dot(xs, w3_ref[...].astype(BF16))
    ye = _dot(hid.astype(BF16), w2_ref[...].astype(BF16)).astype(BF16)
    for g in range(ng):
        ye_refs[g][...] = ye[row_off[g]:row_off[g] + caps[g]]


def _ffn_call(groups, w1, w3, w2):
    caps = tuple(g[1] for g in groups)
    n_exp, d, dff = w1.shape
    in_specs = [pl.BlockSpec((cap, d // 2), lambda e: (e, 0)) for cap in caps]
    in_specs += [pl.BlockSpec((None, d, dff), lambda e: (e, 0, 0)),
                 pl.BlockSpec((None, d, dff), lambda e: (e, 0, 0)),
                 pl.BlockSpec((None, dff, d), lambda e: (e, 0, 0))]
    return pl.pallas_call(
        functools.partial(_ffn_kernel, caps=caps),
        out_shape=tuple(jax.ShapeDtypeStruct((n_exp * cap, d), BF16) for cap in caps),
        grid=(n_exp,),
        in_specs=in_specs,
        out_specs=tuple(pl.BlockSpec((cap, d), lambda e: (e, 0)) for cap in caps),
        compiler_params=_cparams(("arbitrary",)),
        name="ffn",
    )(*[g[0] for g in groups], w1, w3, w2)


def _combine_kernel(tbl_ref, ye_hbm, pos_ref, aff_ref, x1_ref, mod_ref, gpo_ref, y_ref, buf, sem, xbuf, xsem,
                    acc_scr, *, cap, blocks_per_tile):
    i = pl.program_id(0)
    nsteps = pl.num_programs(0)
    d = x1_ref.shape[1]
    lanes = pos_ref.shape[1]
    win = COMBINE_WINDOW
    last_start = ye_hbm.shape[0] - win
    slot = i % 2

    def first_row(step, e):
        return tbl_ref[step * blocks_per_tile, e] + e * cap

    def window_start(first, k):
        unclamped = (first // BF16_ROWS_PER_TILE) * BF16_ROWS_PER_TILE + k * win
        return unclamped, jnp.minimum(unclamped, last_start)

    def fetch(step, to_slot, e):
        start = window_start(first_row(step, e), 0)[1]
        return pltpu.make_async_copy(ye_hbm.at[pl.ds(pl.multiple_of(start, BF16_ROWS_PER_TILE), win), :],
                                     buf.at[to_slot, pl.ds(e * win, win), :], sem.at[to_slot, e])

    @pl.when(i == 0)
    def _prime():
        for e in range(N_EXPERTS):
            fetch(0, 0, e).start()

    @pl.when(i + 1 < nsteps)
    def _ahead():
        for e in range(N_EXPERTS):
            fetch(i + 1, 1 - slot, e).start()

    for e in range(N_EXPERTS):
        fetch(i, slot, e).wait()
    lane_row = lax.broadcasted_iota(I32, (1, win), 1)
    pieces = []
    for e in range(N_EXPERTS):
        pcol = pos_ref[:, e:e + 1]
        grow = jnp.where(pcol >= 0, pcol + e * cap, -1)
        start = window_start(first_row(i, e), 0)[1]
        pieces.append(jnp.where(grow == start + lane_row, aff_ref[:, e:e + 1], 0.0).astype(BF16))
    total = None
    for e0 in range(0, N_EXPERTS, COMBINE_EXPERT_GROUP):
        grp = jnp.concatenate(pieces[e0:e0 + COMBINE_EXPERT_GROUP], axis=1)
        part = _dot(grp, buf[slot, pl.ds(e0 * win, COMBINE_EXPERT_GROUP * win), :])
        total = part if total is None else total + part
    acc_scr[...] = total

    def extra_windows(e):
        covered = window_start(first_row(i, e), 1)[0]
        return jnp.maximum(first_row(i + 1, e) - covered + win - 1, 0) // win

    def expert_extra(e, carry):
        first = first_row(i, e)
        extra = extra_windows(e)

        def more(k, c):
            unclamped, start = window_start(first, k)
            cp = pltpu.make_async_copy(ye_hbm.at[pl.ds(pl.multiple_of(start, BF16_ROWS_PER_TILE), win), :],
                                       xbuf, xsem)
            cp.start()
            cp.wait()
            at_e = lax.broadcasted_iota(I32, (1, lanes), 1) == e
            pcol = jnp.sum(jnp.where(at_e, pos_ref[...].astype(F32), 0.0), axis=1, keepdims=True).astype(I32)
            wcol = jnp.sum(jnp.where(at_e, aff_ref[...], 0.0), axis=1, keepdims=True)
            grow = jnp.where(pcol >= 0, pcol + e * cap, -1)
            hit = (grow == start + lane_row) & (grow >= unclamped)
            acc_scr[...] += _dot(jnp.where(hit, wcol, 0.0).astype(BF16), xbuf[...])
            return c

        lax.fori_loop(1, 1 + extra, more, 0)
        return carry

    any_extra = extra_windows(0)
    for e in range(1, N_EXPERTS):
        any_extra = any_extra + extra_windows(e)

    @pl.when(any_extra > 0)
    def _overflow():
        lax.fori_loop(0, N_EXPERTS, expert_extra, 0)

    m = mod_ref[0]
    y_ref[...] = x1_ref[...] + m[:, 5 * d:6 * d] * _rms(acc_scr[...], gpo_ref[...])


def _combine_call(tbl, ye, pos, aff, x1, mod3, mod_row_of_tile, g_po, cap, tag):
    n, d = x1.shape
    tm = COMBINE_TILE
    lanes = pos.shape[1]
    grid_spec = pltpu.PrefetchScalarGridSpec(
        num_scalar_prefetch=1,
        grid=(n // tm,),
        in_specs=[pl.BlockSpec(memory_space=pl.ANY),
                  pl.BlockSpec((tm, lanes), lambda i, t: (i, 0)),
                  pl.BlockSpec((tm, lanes), lambda i, t: (i, 0)),
                  pl.BlockSpec((tm, d), lambda i, t: (i, 0)),
                  pl.BlockSpec((1, 1, 6 * d), lambda i, t: (mod_row_of_tile(i, tm), 0, 0)),
                  pl.BlockSpec((1, d), lambda i, t: (0, 0))],
        out_specs=pl.BlockSpec((tm, d), lambda i, t: (i, 0)),
        scratch_shapes=[pltpu.VMEM((2, N_EXPERTS * COMBINE_WINDOW, d), BF16),
                        pltpu.SemaphoreType.DMA((2, N_EXPERTS)),
                        pltpu.VMEM((COMBINE_WINDOW, d), BF16),
                        pltpu.SemaphoreType.DMA(()),
                        pltpu.VMEM((tm, d), F32)],
    )
    return pl.pallas_call(
        functools.partial(_combine_kernel, cap=cap, blocks_per_tile=tm // ROUTE_BLOCK),
        out_shape=jax.ShapeDtypeStruct((n, d), F32),
        grid_spec=grid_spec,
        compiler_params=_cparams(("arbitrary",)),
        name="combine_" + tag,
    )(tbl, ye, pos, aff, x1, mod3, g_po)


def _rope_tables(seq):
    f32 = np.float32
    rows = seq // GRID_W
    r = np.repeat(np.arange(rows), GRID_W).astype(f32)
    col = np.tile(np.arange(GRID_W), rows).astype(f32)
    pairs = HEAD_DIM // 4
    freqs = np.power(f32(ROPE_THETA), -np.arange(pairs, dtype=f32) / f32(pairs)).astype(f32)
    ang = np.concatenate([r[:, None] * freqs, col[:, None] * freqs], axis=-1).astype(f32)
    cos = np.repeat(np.cos(ang), 2, axis=-1).astype(f32)
    sin = np.repeat(np.sin(ang), 2, axis=-1).astype(f32)
    even = (np.arange(HEAD_DIM) % 2) == 0
    zero = f32(0.0)
    return (jnp.asarray(cos), jnp.asarray(np.where(even, -sin, zero).astype(f32)),
            jnp.asarray(np.where(even, zero, sin).astype(f32)))


def _trunk_to_routing(x, mod3, mod_row_of_tile, rope_tabs, ctx, lw, tag):
    (g_pre_mix, g_post_mix, g_pre_ffn, g_post_ffn, w_in_p, g_q, g_k, wgf, bgf, wgb, bgb, g_gla,
     w_pa, w_pg, w_out, w_r, w1, w3, w2) = lw
    batch, seq, d = x.shape
    n = batch * seq
    x2 = x.reshape(n, d)
    (q_a, k_a, v_a, q_g, k_g, v_g, r_g, lg_f, lg_b, gates) = _inproj_call(
        x2, mod3, mod_row_of_tile, g_pre_mix, w_in_p, g_q, g_k, wgf, bgf, wgb, bgb, rope_tabs, batch, seq)
    if ctx is None:
        o_a = _attn_call(q_a, k_a, v_a, None, None)
        o_g, s_f, s_b = _gla_call(q_g, k_g, v_g, lg_f, lg_b, r_g, g_gla, None, None, batch, seq, True,
                                  GLA_CTX_HEADS_PER_STEP, GLA_CTX_SEQS_PER_STEP)
    else:
        ck, cv, s_f0, s_b0 = ctx
        o_a = _attn_call(q_a, k_a, v_a, ck, cv)
        (o_g,) = _gla_call(q_g, k_g, v_g, lg_f, lg_b, r_g, g_gla, s_f0, s_b0, batch, seq, False,
                           GLA_LAT_HEADS_PER_STEP, 1)
        s_f = s_b = None
    x1, h, aff = _outproj_call(o_a, o_g, gates, x2, mod3, mod_row_of_tile, w_pa, w_pg, w_out,
                               g_post_mix, g_pre_ffn, w_r, tag)
    cap = (EC_CAPACITY_FACTOR * n) // N_EXPERTS
    pos, post, tbl = _route_call(aff, cap, tag)
    xs = _sc_gather_call(h, post, cap)
    return dict(x1=x1, xs=xs, aff=aff, pos=pos, tbl=tbl, cap=cap,
                mod_row_of_tile=mod_row_of_tile, tag=tag, shape=(batch, seq, d)), (k_a, v_a, s_f, s_b)


def _expert_ffn(groups, mod3, g_post_ffn, w1, w3, w2):
    yes = _ffn_call([(g["xs"], g["cap"]) for g in groups], w1, w3, w2)
    outs = []
    for g, ye in zip(groups, yes):
        y = _combine_call(g["tbl"], ye, g["pos"], g["aff"], g["x1"], mod3, g["mod_row_of_tile"], g_post_ffn,
                          g["cap"], g["tag"])
        outs.append(y.reshape(g["shape"]))
    return outs


def kernel(x_prompt, x_sample, cache_k, cache_v, state_gla_fwd, state_gla_bwd, c, c_ctx, g_pre_mix, g_post_mix, g_pre_ffn, g_post_ffn, w_mod, b_mod, w_in, g_q, g_k, w_gk2_f, b_gk_f, w_gk2_b, b_gk_b, g_gla, w_pa, w_pg, w_out, w_router, w1, w3, w2):
    depth = w_in.shape[0]
    assert depth == 1, "single trunk layer"
    d = x_prompt.shape[-1]
    dec_batch, dec_seq, _ = x_sample.shape
    assert dec_batch + 1 <= MOD_ROWS
    l = 0
    mod3, w_in_p = _mod_call(c_ctx.reshape(1, d), c, w_mod[l], b_mod[l].reshape(1, -1), jnp.swapaxes(w_in[l], 0, 1))
    row = lambda a: a[l].reshape(1, -1)
    lw = (row(g_pre_mix), row(g_post_mix), row(g_pre_ffn), row(g_post_ffn), w_in_p, row(g_q), row(g_k),
          w_gk2_f[l], row(b_gk_f), w_gk2_b[l], row(b_gk_b), row(g_gla),
          w_pa[l], w_pg[l], w_out[l], jnp.swapaxes(w_router[l], 0, 1), w1[l], w3[l], w2[l])

    gp, (nk, nv, nsf, nsb) = _trunk_to_routing(x_prompt, mod3, lambda i, tm: 0, None, None, lw, "ctx")
    ctx = (cache_k, cache_v, state_gla_fwd, state_gla_bwd)
    gs, _ = _trunk_to_routing(x_sample, mod3, lambda i, tm: 1 + (i * tm) // dec_seq, _rope_tables(dec_seq),
                              ctx, lw, "lat")
    yp, ys = _expert_ffn([gp, gs], mod3, lw[3], lw[16], lw[17], lw[18])
    return (yp, ys, nk, nv, nsf, nsb)
```

```python
import functools

import jax
import jax.numpy as jnp
import numpy as np
from jax import lax
from jax.experimental import pallas as pl
from jax.experimental.pallas import tpu as pltpu
from jax.experimental.pallas import tpu_sc as plsc

F32 = jnp.float32
BF16 = jnp.bfloat16
I32 = jnp.int32

N_HEADS = 8
N_KV_HEADS = 2
HEAD_DIM = 128
GRID_W = 64
ROPE_THETA = 10000.0
GLA_HEADS = 4
GLA_GATE_RANK = 16
GLA_GATE_NORM = 16.0
GLA_CHUNK = 64
N_EXPERTS = 16
EC_CAPACITY_FACTOR = 2
EPS = 1e-6
LOG2_E = 1.4426950408889634

V7X_LANES = 128
V7X_VMEM_BYTES = 64 * 1024 * 1024
V7X_VMEM_RESERVE_BYTES = 6 * 1024 * 1024
BF16_ROWS_PER_TILE = 16
V7X_SC_CORES = 2
V7X_SC_SUBCORES = 16
V7X_SC_LANES = 16

TOKEN_TILE = 512
OUTPROJ_TILE = 512
OUTPROJ_ROW_GROUPS = 4
ATTN_Q_TILE = 256
ATTN_SEQS_PER_STEP = 4
ATTN_SOFTMAX_ROWS = 32
ATTN_SOFTMAX_LONG_KEYS = 8 * V7X_LANES
GLA_BLOCK = 256
GLA_HEADS_PER_STAGE_GROUP = 2
GLA_CTX_HEADS_PER_STEP = 4
GLA_CTX_SEQS_PER_STEP = 4
GLA_LAT_HEADS_PER_STEP = 2
ROUTE_BLOCK = 256
SC_GATHER_CHUNK = 128
COMBINE_TILE = 512
COMBINE_WINDOW = 128
COMBINE_EXPERT_GROUP = 2
MOD_ROWS = 8
MOD_STEPS = 12
MOD_WEIGHT_SLABS = 11


def _cparams(semantics):
    return pltpu.CompilerParams(dimension_semantics=semantics,
                                vmem_limit_bytes=V7X_VMEM_BYTES - V7X_VMEM_RESERVE_BYTES)


def _sigmoid(x):
    return 0.5 * (jnp.tanh(0.5 * x) + 1.0)


def _silu(x):
    return x * _sigmoid(x)


def _log_sigmoid(x):
    return jnp.minimum(x, 0.0) - jnp.log1p(jnp.exp(-jnp.abs(x)))


def _rms(x, g):
    ms = jnp.mean(x * x, axis=-1, keepdims=True)
    return x * lax.rsqrt(ms + EPS) * g


def _dot(a, b):
    return jnp.dot(a, b, preferred_element_type=F32)


def _dot_nt(a, b):
    return lax.dot_general(a, b, (((1,), (1,)), ((), ())), preferred_element_type=F32)


def _mod_kernel(cctx_ref, c_ref, w_ref, b_ref, win_ref, o_ref, wout_ref):
    rows, d = o_ref.shape[0], c_ref.shape[1]
    ridx = lax.broadcasted_iota(I32, (rows, d), 0)
    cc = jnp.where(ridx == 0, jnp.broadcast_to(cctx_ref[...], (rows, d)), 0.0)
    for b in range(c_ref.shape[0]):
        cc = jnp.where(ridx == b + 1, jnp.broadcast_to(c_ref[b:b + 1, :], (rows, d)), cc)
    out = _dot(_silu(cc).astype(BF16), w_ref[...].astype(BF16)) + b_ref[...]
    for r in range(rows):
        o_ref[r] = out[r:r + 1, :]
    wout_ref[...] = win_ref[...].astype(BF16)


def _mod_call(c_ctx, c, w_mod, b_mod, w_in_t):
    d, n6 = w_mod.shape
    dec_batch = c.shape[0]
    din = w_in_t.shape[0]
    steps = MOD_STEPS
    tn = n6 // steps
    slabs = MOD_WEIGHT_SLABS
    rows = din // slabs
    assert tn * steps == n6 and slabs * rows == din and rows % BF16_ROWS_PER_TILE == 0 and slabs <= steps
    slab = lambda j: (jnp.minimum(j, slabs - 1), 0)
    return pl.pallas_call(
        _mod_kernel,
        out_shape=(jax.ShapeDtypeStruct((MOD_ROWS, 1, n6), F32), jax.ShapeDtypeStruct((din, d), BF16)),
        grid=(steps,),
        in_specs=[pl.BlockSpec((1, d), lambda j: (0, 0)),
                  pl.BlockSpec((dec_batch, d), lambda j: (0, 0)),
                  pl.BlockSpec((d, tn), lambda j: (0, j)),
                  pl.BlockSpec((1, tn), lambda j: (0, j)),
                  pl.BlockSpec((rows, d), slab)],
        out_specs=(pl.BlockSpec((MOD_ROWS, 1, tn), lambda j: (0, 0, j)), pl.BlockSpec((rows, d), slab)),
        compiler_params=_cparams(("arbitrary",)),
        name="mod",
    )(c_ctx, c, w_mod, b_mod, w_in_t)


def _inproj_layout(d):
    aq, akv = N_HEADS * HEAD_DIM, N_KV_HEADS * HEAD_DIM
    gk, gv = d // 2, d
    names = ("q_a", "k_a", "v_a", "q_g", "k_g", "v_g", "r_g", "gk_f", "gk_b", "gates")
    widths = (aq, akv, akv, gk, gk, gv, gv, GLA_GATE_RANK, GLA_GATE_RANK, 2 * d)
    off, o = {}, 0
    for nme, w in zip(names, widths):
        off[nme] = (o, o + w)
        o += w
    off["gk"] = (off["gk_f"][0], off["gk_f"][0] + V7X_LANES)
    return off, o


def _inproj_kernel(*refs, rope, d):
    if rope:
        (x_ref, mod_ref, gpre_ref, w_ref, gq_ref, gk_ref, wgf_ref, bgf_ref, wgb_ref, bgb_ref,
         cos_ref, se_ref, so_ref, *outs) = refs
    else:
        (x_ref, mod_ref, gpre_ref, w_ref, gq_ref, gk_ref, wgf_ref, bgf_ref, wgb_ref, bgb_ref,
         *outs) = refs
    qa_ref, k_ref, v_ref, qg_ref, kg_ref, vg_ref, rg_ref, lgf_ref, lgb_ref, gate_ref = outs
    off, _ = _inproj_layout(d)
    m = mod_ref[0]
    h = _rms(x_ref[...], gpre_ref[...]) * (1.0 + m[:, d:2 * d]) + m[:, 0:d]
    hb = h.astype(BF16)

    def proj(name):
        a, b = off[name]
        return _dot_nt(hb, w_ref[a:b, :])

    def qk_norm(y, g_ref):
        y = _rms(y, g_ref[...])
        if rope:
            nxt = pltpu.roll(y, HEAD_DIM - 1, axis=1)
            prv = pltpu.roll(y, 1, axis=1)
            y = y * cos_ref[...] + nxt * se_ref[...] + prv * so_ref[...]
        return y

    gk = proj("gk").astype(BF16)

    q = proj("q_a")
    scale = HEAD_DIM ** -0.5 * LOG2_E
    for hd in range(N_HEADS):
        sl = slice(hd * HEAD_DIM, (hd + 1) * HEAD_DIM)
        qa_ref[:, sl] = (qk_norm(q[:, sl], gq_ref) * scale).astype(BF16)

    k = proj("k_a")
    v = proj("v_a")
    tb, _, _, ts, _ = k_ref.shape
    for kv in range(N_KV_HEADS):
        sl = slice(kv * HEAD_DIM, (kv + 1) * HEAD_DIM)
        k_ref[:, 0, kv] = qk_norm(k[:, sl], gk_ref).reshape(tb, ts, HEAD_DIM)
        v_ref[:, 0, kv] = v[:, sl].reshape(tb, ts, HEAD_DIM)

    dk = (d // 2) // GLA_HEADS
    qg_ref[...] = (proj("q_g") * (dk ** -0.5)).astype(BF16)
    kg_ref[...] = proj("k_g").astype(BF16)
    vg_ref[...] = proj("v_g").astype(BF16)
    rg_ref[...] = _silu(proj("r_g")).astype(BF16)

    gate_ref[...] = _sigmoid(proj("gates")).astype(BF16)

    def rank_rows(w_ref, at):
        rank, width = w_ref.shape
        parts = [jnp.zeros((at, width), F32)] if at else []
        parts += [w_ref[...], jnp.zeros((V7X_LANES - at - rank, width), F32)]
        return jnp.concatenate(parts, axis=0).astype(BF16)

    lgf_ref[...] = _log_sigmoid(_dot(gk, rank_rows(wgf_ref, 0)) + bgf_ref[...]) * (1.0 / GLA_GATE_NORM)
    lgb_ref[...] = (_log_sigmoid(_dot(gk, rank_rows(wgb_ref, GLA_GATE_RANK)) + bgb_ref[...])
                    * (1.0 / GLA_GATE_NORM))


def _inproj_call(x2, mod3, mod_row_of_tile, g_pre, w_in_p, g_q, g_k, wgf, bgf, wgb, bgb, rope_tabs,
                 batch, seq):
    n, d = x2.shape
    tm = TOKEN_TILE
    _, dinp = _inproj_layout(d)
    rope = rope_tabs is not None
    gk_w = d // 2
    if seq >= tm:
        tb, ts, per = 1, tm, seq // tm
        kv_map = lambda i: (i // per, 0, 0, i % per, 0)
    else:
        tb, ts, per = tm // seq, seq, 1
        kv_map = lambda i: (i, 0, 0, 0, 0)
    row = lambda i: (i, 0)
    const = lambda i: (0, 0)
    in_specs = [
        pl.BlockSpec((tm, d), row),
        pl.BlockSpec((1, 1, 6 * d), lambda i: (mod_row_of_tile(i, tm), 0, 0)),
        pl.BlockSpec((1, d), const),
        pl.BlockSpec((dinp, d), const, pipeline_mode=pl.Buffered(1)),
        pl.BlockSpec((1, HEAD_DIM), const),
        pl.BlockSpec((1, HEAD_DIM), const),
        pl.BlockSpec((GLA_GATE_RANK, gk_w), const),
        pl.BlockSpec((1, gk_w), const),
        pl.BlockSpec((GLA_GATE_RANK, gk_w), const),
        pl.BlockSpec((1, gk_w), const),
    ]
    args = [x2, mod3, g_pre, w_in_p, g_q, g_k, wgf, bgf, wgb, bgb]
    if rope:
        tab = pl.BlockSpec((tm, HEAD_DIM), lambda i: (i % per, 0))
        in_specs += [tab, tab, tab]
        args += list(rope_tabs)
    kv_shape = jax.ShapeDtypeStruct((batch, 1, N_KV_HEADS, seq, HEAD_DIM), F32)
    kv_spec = pl.BlockSpec((tb, 1, N_KV_HEADS, ts, HEAD_DIM), kv_map)
    out_shape = (
        jax.ShapeDtypeStruct((n, N_HEADS * HEAD_DIM), BF16), kv_shape, kv_shape,
        jax.ShapeDtypeStruct((n, gk_w), BF16), jax.ShapeDtypeStruct((n, gk_w), BF16),
        jax.ShapeDtypeStruct((n, d), BF16), jax.ShapeDtypeStruct((n, d), BF16),
        jax.ShapeDtypeStruct((n, gk_w), F32), jax.ShapeDtypeStruct((n, gk_w), F32),
        jax.ShapeDtypeStruct((n, 2 * d), BF16),
    )
    out_specs = (
        pl.BlockSpec((tm, N_HEADS * HEAD_DIM), row), kv_spec, kv_spec,
        pl.BlockSpec((tm, gk_w), row), pl.BlockSpec((tm, gk_w), row),
        pl.BlockSpec((tm, d), row), pl.BlockSpec((tm, d), row),
        pl.BlockSpec((tm, gk_w), row), pl.BlockSpec((tm, gk_w), row),
        pl.BlockSpec((tm, 2 * d), row),
    )
    return pl.pallas_call(
        functools.partial(_inproj_kernel, rope=rope, d=d),
        out_shape=out_shape,
        grid=(n // tm,),
        in_specs=in_specs,
        out_specs=out_specs,
        compiler_params=_cparams(("parallel",)),
        name="inproj_lat" if rope else "inproj_ctx",
    )(*args)


def _attn_kernel(*refs, cached):
    if cached:
        q_ref, k_ref, v_ref, ck_ref, cv_ref, o_ref = refs
    else:
        q_ref, k_ref, v_ref, o_ref = refs
    seqs = k_ref.shape[0]
    tq = q_ref.shape[0] // seqs
    grp = N_HEADS // N_KV_HEADS
    chains = [(sq, kv) for sq in range(seqs) for kv in range(N_KV_HEADS)]

    def scores_of(sq, kv):
        rows = pl.ds(sq * tq, tq)
        kk = k_ref[sq, 0, kv].astype(BF16)
        vv = v_ref[sq, 0, kv].astype(BF16)
        if cached:
            kk = jnp.concatenate([ck_ref[sq, 0, kv].astype(BF16), kk], axis=0)
            vv = jnp.concatenate([cv_ref[sq, 0, kv].astype(BF16), vv], axis=0)
        heads = [q_ref[rows, (kv * grp + g) * HEAD_DIM:(kv * grp + g + 1) * HEAD_DIM] for g in range(grp)]
        return _dot_nt(jnp.concatenate(heads, axis=0), kk), vv

    look_ahead = cached
    ahead = scores_of(*chains[0]) if look_ahead else None
    for n, (sq, kv) in enumerate(chains):
        s, vv = ahead if look_ahead else scores_of(sq, kv)
        if look_ahead and n + 1 < len(chains):
            ahead = scores_of(*chains[n + 1])
        rows = pl.ds(sq * tq, tq)
        blk = ATTN_SOFTMAX_ROWS if s.shape[1] > ATTN_SOFTMAX_LONG_KEYS else s.shape[0]
        ps, ls = [], []
        for r0 in range(0, s.shape[0], blk):
            sb = s[r0:r0 + blk]
            pb = jnp.exp2(sb - jnp.max(sb, axis=-1, keepdims=True))
            ls.append(jnp.sum(pb, axis=-1, keepdims=True))
            ps.append(pb.astype(BF16))
        l = jnp.concatenate(ls, axis=0)
        o = _dot(jnp.concatenate(ps, axis=0), vv) / l
        for g in range(grp):
            hd = kv * grp + g
            o_ref[rows, hd * HEAD_DIM:(hd + 1) * HEAD_DIM] = o[g * tq:(g + 1) * tq].astype(BF16)


def _attn_call(q_a, k_a, v_a, cache_k, cache_v):
    batch, _, _, seq, _ = k_a.shape
    n, aq = q_a.shape
    tq = min(ATTN_Q_TILE, seq)
    per = seq // tq
    cached = cache_k is not None
    seqs = ATTN_SEQS_PER_STEP if (per == 1 and not cached) else 1
    batch = batch // seqs
    own = pl.BlockSpec((seqs, 1, N_KV_HEADS, seq, HEAD_DIM), lambda b, j: (b, 0, 0, 0, 0))
    tq = tq * seqs
    in_specs = [pl.BlockSpec((tq, aq), lambda b, j: (b * per + j, 0)), own, own]
    args = [q_a, k_a, v_a]
    if cached:
        past = cache_k.shape[3]
        cspec = pl.BlockSpec((1, 1, N_KV_HEADS, past, HEAD_DIM), lambda b, j: (b, 0, 0, 0, 0))
        in_specs += [cspec, cspec]
        args += [cache_k, cache_v]
    return pl.pallas_call(
        functools.partial(_attn_kernel, cached=cached),
        out_shape=jax.ShapeDtypeStruct((n, aq), BF16),
        grid=(batch, per),
        in_specs=in_specs,
        out_specs=pl.BlockSpec((tq, aq), lambda b, j: (b * per + j, 0)),
        compiler_params=_cparams(("parallel", "parallel")),
        name="attn_lat" if cached else "attn_ctx",
    )(*args)


def _gla_kernel(*refs, nblk, heads, seqs, has_state, emit_state):
    refs = list(refs)
    q_ref, k_ref, v_ref, lgf_ref, lgb_ref, rg_ref, gg_ref = refs[:7]
    pos = 7
    if has_state:
        s0f_ref, s0b_ref = refs[pos:pos + 2]
        pos += 2
    og_ref = refs[pos]
    pos += 1
    if emit_state:
        sf_ref, sb_ref = refs[pos:pos + 2]
        pos += 2
    of_scr, ob_scr = refs[pos:pos + 2]

    blk = GLA_BLOCK
    ch = GLA_CHUNK
    nch = blk // ch
    dk = q_ref.shape[1] // heads
    dv = v_ref.shape[1] // heads
    shift = ch.bit_length() - 1
    row_in_chunk = lax.broadcasted_iota(I32, (blk, dk), 0) & (ch - 1)
    ri = lax.broadcasted_iota(I32, (blk, blk), 0)
    ci = lax.broadcasted_iota(I32, (blk, blk), 1)
    same = (ri >> shift) == (ci >> shift)
    mask_f = same & (ci <= ri)
    mask_b = same & (ci >= ri)

    def one_block(b0, hd, reverse):
        rows = pl.ds(b0, blk)
        kcols = slice(hd * dk, (hd + 1) * dk)
        q = q_ref[rows, kcols].astype(F32)
        k = k_ref[rows, kcols].astype(F32)
        v = v_ref[rows, hd * dv:(hd + 1) * dv]
        b = (lgb_ref if reverse else lgf_ref)[rows, kcols]
        mask = mask_b if reverse else mask_f
        s = 1
        while s < ch:
            if reverse:
                sh = pltpu.roll(b, blk - s, axis=0)
                b = b + jnp.where(row_in_chunk < ch - s, sh, 0.0)
            else:
                sh = pltpu.roll(b, s, axis=0)
                b = b + jnp.where(row_in_chunk >= s, sh, 0.0)
            s *= 2
        qe = (q * jnp.exp(b)).astype(BF16)
        ke = (k * jnp.exp(-b)).astype(BF16)
        scores = _dot_nt(qe, ke)
        end_row = [c * ch + (0 if reverse else ch - 1) for c in range(nch)]
        ends = [b[r:r + 1, :] for r in end_row]
        b_end = jnp.concatenate([jnp.broadcast_to(e, (ch, dk)) for e in ends], axis=0)
        kd = (k * jnp.exp(b_end - b)).astype(BF16)
        return dict(scores=scores, mask=mask, qe=qe, kd=kd, v=v, ends=ends, reverse=reverse)

    def chunk_products(w):
        return [lax.dot_general(w["kd"][c * ch:(c + 1) * ch], w["v"][c * ch:(c + 1) * ch],
                                (((0,), (0,)), ((), ())), preferred_element_type=F32) for c in range(nch)]

    def intra_chunk(w):
        return _dot(jnp.where(w["mask"], w["scores"], 0.0).astype(BF16), w["v"])

    def across_chunks(w, kv, state):
        decay = jnp.exp(jnp.concatenate(w["ends"] + [jnp.zeros((dk - nch, dk), F32)], axis=0)).T
        inter = [None] * nch
        for c in (range(nch - 1, -1, -1) if w["reverse"] else range(nch)):
            if state is None:
                inter[c] = jnp.zeros((ch, dv), F32)
                state = kv[c]
            else:
                inter[c] = _dot(w["qe"][c * ch:(c + 1) * ch], state.astype(BF16))
                state = decay[:, c:c + 1] * state + kv[c]
        return jnp.concatenate(inter, axis=0), state

    group = min(GLA_HEADS_PER_STAGE_GROUP, heads)
    for sq in range(seqs):
        srows = pl.ds(sq * nblk * blk, nblk * blk)
        for h0 in range(0, heads, group):
            hds = list(range(h0, h0 + group))
            states = {(hd, rev): ((s0b_ref if rev else s0f_ref)[sq, 0, hd] if has_state else None)
                      for hd in hds for rev in (False, True)}
            for i in range(nblk):
                block_of = {False: sq * nblk + i, True: sq * nblk + nblk - 1 - i}
                chains = [(hd, rev) for hd in hds for rev in (False, True)]
                work = {c: one_block(block_of[c[1]] * blk, c[0], c[1]) for c in chains}
                kvs = {c: chunk_products(work[c]) for c in chains}
                intra = {c: intra_chunk(work[c]) for c in chains}
                for c in chains:
                    hd, rev = c
                    inter, states[c] = across_chunks(work[c], kvs[c], states[c])
                    scr = ob_scr if rev else of_scr
                    scr[pl.ds(block_of[rev] * blk, blk), hd * dv:(hd + 1) * dv] = intra[c] + inter
            for hd in hds:
                vcols = slice(hd * dv, (hd + 1) * dv)
                if emit_state:
                    sf_ref[sq, 0, hd] = states[(hd, False)]
                    sb_ref[sq, 0, hd] = states[(hd, True)]
                o = of_scr[srows, vcols] + ob_scr[srows, vcols]
                og_ref[srows, vcols] = (_rms(o, gg_ref[...]) * rg_ref[srows, vcols].astype(F32)).astype(BF16)


def _gla_call(q_g, k_g, v_g, lg_f, lg_b, r_g, g_gla, state_f, state_b, batch, seq, emit_state, heads, seqs):
    n, gkw = q_g.shape
    d = v_g.shape[1]
    dk, dv = gkw // GLA_HEADS, d // GLA_HEADS
    has_state = state_f is not None
    nblk = seq // GLA_BLOCK
    batch = batch // seqs
    seq = seq * seqs
    kspec = pl.BlockSpec((seq, heads * dk), lambda b, h: (b, h))
    vspec = pl.BlockSpec((seq, heads * dv), lambda b, h: (b, h))
    sspec = pl.BlockSpec((seqs, 1, heads, dk, dv), lambda b, h: (b, 0, h, 0, 0))
    in_specs = [kspec, kspec, vspec, kspec, kspec, vspec, pl.BlockSpec((1, dv), lambda b, h: (0, 0))]
    args = [q_g, k_g, v_g, lg_f, lg_b, r_g, g_gla]
    if has_state:
        in_specs += [sspec, sspec]
        args += [state_f, state_b]
    out_shape = [jax.ShapeDtypeStruct((n, d), BF16)]
    out_specs = [vspec]
    if emit_state:
        st = jax.ShapeDtypeStruct((batch * seqs, 1, GLA_HEADS, dk, dv), F32)
        out_shape += [st, st]
        out_specs += [sspec, sspec]
    return pl.pallas_call(
        functools.partial(_gla_kernel, nblk=nblk, heads=heads, seqs=seqs, has_state=has_state,
                          emit_state=emit_state),
        out_shape=tuple(out_shape),
        grid=(batch, GLA_HEADS // heads),
        in_specs=in_specs,
        out_specs=tuple(out_specs),
        scratch_shapes=[pltpu.VMEM((seq, heads * dv), F32), pltpu.VMEM((seq, heads * dv), F32)],
        compiler_params=_cparams(("parallel", "parallel")),
        name="gla_ctx" if emit_state else "gla_lat",
    )(*args)


def _outproj_kernel(oa_ref, og_ref, gate_ref, x_ref, mod_ref, wpa_ref, wpg_ref, wout_ref, gpm_ref,
                    gpf_ref, wr_ref, x1_ref, h_ref, aff_ref):
    d = x_ref.shape[1]
    m = mod_ref[0]
    sub = x_ref.shape[0] // OUTPROJ_ROW_GROUPS
    groups = [pl.ds(g * sub, sub) for g in range(OUTPROJ_ROW_GROUPS)]
    wpa, wpg, wout = (w[...].astype(BF16) for w in (wpa_ref, wpg_ref, wout_ref))
    branch = [(_dot(oa_ref[r, :], wpa), _dot(og_ref[r, :], wpg)) for r in groups]
    mo = jnp.concatenate(
        [_dot((gate_ref[r, 0:d].astype(F32) * oa + gate_ref[r, d:2 * d].astype(F32) * og).astype(BF16), wout)
         for r, (oa, og) in zip(groups, branch)], axis=0)
    x1 = x_ref[...] + m[:, 2 * d:3 * d] * _rms(mo, gpm_ref[...])
    x1_ref[...] = x1
    hb = (_rms(x1, gpf_ref[...]) * (1.0 + m[:, 4 * d:5 * d]) + m[:, 3 * d:4 * d]).astype(BF16)
    bits = lax.bitcast_convert_type(hb.astype(F32), jnp.uint32)
    packed = (bits[:, 0:d // 2] >> 16) | (bits[:, d // 2:d] & jnp.uint32(0xFFFF0000))
    h_ref[...] = lax.bitcast_convert_type(packed, I32)
    wr = jnp.concatenate([wr_ref[...], jnp.zeros((V7X_LANES - wr_ref.shape[0], d), F32)], axis=0).astype(BF16)
    logits = _dot_nt(hb, wr)
    valid = lax.broadcasted_iota(I32, logits.shape, 1) < N_EXPERTS
    mx = jnp.max(jnp.where(valid, logits, -jnp.inf), axis=-1, keepdims=True)
    ex = jnp.where(valid, jnp.exp(logits - mx), 0.0)
    aff_ref[...] = ex / jnp.sum(ex, axis=-1, keepdims=True)


def _outproj_call(o_a, o_g, gates, x2, mod3, mod_row_of_tile, w_pa, w_pg, w_out, g_pm, g_pf, w_r, tag):
    n, d = x2.shape
    tm = OUTPROJ_TILE
    row = lambda i: (i, 0)
    const = lambda i: (0, 0)
    wspec = pl.BlockSpec((d, d), const, pipeline_mode=pl.Buffered(1))
    return pl.pallas_call(
        _outproj_kernel,
        out_shape=(jax.ShapeDtypeStruct((n, d), F32), jax.ShapeDtypeStruct((n, d // 2), I32),
                   jax.ShapeDtypeStruct((n, V7X_LANES), F32)),
        grid=(n // tm,),
        in_specs=[pl.BlockSpec((tm, d), row), pl.BlockSpec((tm, d), row), pl.BlockSpec((tm, 2 * d), row),
                  pl.BlockSpec((tm, d), row),
                  pl.BlockSpec((1, 1, 6 * d), lambda i: (mod_row_of_tile(i, tm), 0, 0)),
                  wspec, wspec, wspec, pl.BlockSpec((1, d), const), pl.BlockSpec((1, d), const),
                  pl.BlockSpec((N_EXPERTS, d), const)],
        out_specs=(pl.BlockSpec((tm, d), row), pl.BlockSpec((tm, d // 2), row),
                   pl.BlockSpec((tm, V7X_LANES), row)),
        compiler_params=_cparams(("parallel",)),
        name="outproj_" + tag,
    )(o_a, o_g, gates, x2, mod3, w_pa, w_pg, w_out, g_pm, g_pf, w_r)


def _route_kernel(aff_ref, pos_ref, post_ref, tbl_ref, afft_scr, *, cap):
    n = aff_ref.shape[0]
    rb = ROUTE_BLOCK
    nb = n // rb
    lanes = aff_ref.shape[1]
    lane = lax.broadcasted_iota(I32, (1, lanes), 1)
    expert_lane = lane < N_EXPERTS
    tbl_ref[...] = jnp.zeros(tbl_ref.shape, I32)

    def to_token_lanes(c, carry):
        start = pl.multiple_of(c * rb, rb)
        afft_scr[c] = aff_ref[pl.ds(start, rb), :].T[0:N_EXPERTS, :]
        return carry

    lax.fori_loop(0, nb, to_token_lanes, 0)
    aff_t = afft_scr[...]

    def count(hit):
        return jnp.sum(jnp.sum(hit.astype(I32), axis=0), axis=1, keepdims=True)

    def bit_step(i, lo):
        t = lo | jnp.left_shift(jnp.int32(1), 30 - i)
        ge = aff_t >= lax.bitcast_convert_type(t, F32)[None]
        return jnp.where(count(ge) >= cap, t, lo)

    thr_bits = lax.fori_loop(0, 31, bit_step, jnp.zeros((N_EXPERTS, 1), I32))
    need_t = cap - count(aff_t > lax.bitcast_convert_type(thr_bits, F32)[None])

    def to_expert_lanes(col):
        full = jnp.concatenate([jnp.broadcast_to(col, (N_EXPERTS, lanes)),
                                jnp.zeros((lanes - N_EXPERTS, lanes), I32)], axis=0)
        return full.T[0:1, :]

    thr = lax.bitcast_convert_type(to_expert_lanes(thr_bits), F32)
    need = to_expert_lanes(need_t).astype(F32)
    capf = float(cap)

    r = lax.broadcasted_iota(I32, (rb, rb), 0)
    c_ = lax.broadcasted_iota(I32, (rb, rb), 1)
    tril = jnp.where(c_ <= r, 1.0, 0.0).astype(BF16)

    def blk_step(c, carry):
        eq_before, raw_before = carry
        start = pl.multiple_of(c * rb, rb)
        a = aff_ref[pl.ds(start, rb), :]
        gt = a > thr
        eq = a == thr
        eq_incl = _dot(tril, jnp.where(eq, 1.0, 0.0).astype(BF16)) + eq_before
        raw = (gt | (eq & (eq_incl <= need))) & expert_lane
        raw_incl = _dot(tril, jnp.where(raw, 1.0, 0.0).astype(BF16)) + raw_before
        sel = raw & (raw_incl <= capf)
        self_ = jnp.where(sel, 1.0, 0.0)
        incl = jnp.minimum(raw_incl, capf)
        sel_before = jnp.minimum(raw_before, capf)
        excl = incl - self_
        posb = jnp.where(sel, excl, -1.0).astype(I32)
        pos_ref[pl.ds(start, rb), :] = posb
        post_ref[:, pl.ds(start, rb)] = posb.T[0:N_EXPERTS, :]
        tbl_ref[pl.ds(c, 1), :] = sel_before.astype(I32)
        return (eq_incl[rb - 1:rb, :], raw_incl[rb - 1:rb, :])

    zero = jnp.zeros((1, lanes), F32)
    _, total = lax.fori_loop(0, nb, blk_step, (zero, zero), unroll=2)
    tbl_ref[pl.ds(nb, 1), :] = jnp.minimum(total, capf).astype(I32)


def _route_call(aff, cap, tag):
    n, lanes = aff.shape
    nb = n // ROUTE_BLOCK
    tbl_rows = -(-(nb + 1) // 8) * 8
    full = lambda *shape: pl.BlockSpec(shape, lambda: tuple(0 for _ in shape))
    return pl.pallas_call(
        functools.partial(_route_kernel, cap=cap),
        out_shape=(jax.ShapeDtypeStruct((n, lanes), I32),
                   jax.ShapeDtypeStruct((N_EXPERTS, n), I32),
                   jax.ShapeDtypeStruct((tbl_rows, lanes), I32)),
        in_specs=[full(n, lanes)],
        out_specs=(full(n, lanes), full(N_EXPERTS, n), full(tbl_rows, lanes)),
        scratch_shapes=[pltpu.VMEM((nb, N_EXPERTS, ROUTE_BLOCK), F32)],
        compiler_params=_cparams(()),
        name="route_" + tag,
    )(aff)


def _sc_gather_call(table, post, cap):
    n_exp, n = post.shape
    words = table.shape[1]
    workers = V7X_SC_CORES * V7X_SC_SUBCORES
    parts = workers // n_exp
    chunk = SC_GATHER_CHUNK
    lanes = V7X_SC_LANES
    per_part = cap // parts
    assert parts * n_exp == workers and per_part % chunk == 0 and n % lanes == 0
    mesh = plsc.VectorSubcoreMesh(core_axis_name="c", subcore_axis_name="s",
                                  num_cores=V7X_SC_CORES, num_subcores=V7X_SC_SUBCORES)

    def body(table_hbm, post_hbm, out_hbm, pos_v, idx_v, rows_v, sem):
        wid = lax.axis_index("s") * V7X_SC_CORES + lax.axis_index("c")
        e = wid // parts
        part = wid % parts
        pltpu.sync_copy(post_hbm.at[e], pos_v)
        lane = lax.iota(I32, lanes)

        @pl.loop(0, n, step=lanes)
        def _(t0):
            p = pos_v[pl.ds(t0, lanes)]
            plsc.store_scatter(idx_v, [p], lane + t0, mask=p >= 0)

        @pl.loop(0, per_part // chunk)
        def _(j):
            off = pl.multiple_of(part * per_part + j * chunk, chunk)
            pltpu.async_copy(table_hbm.at[idx_v.at[pl.ds(off, chunk)]], rows_v, sem).wait()
            pltpu.sync_copy(rows_v, out_hbm.at[pl.ds(e * cap + off, chunk)])

    return pl.kernel(
        body,
        out_type=jax.ShapeDtypeStruct((n_exp * cap, words), table.dtype),
        mesh=mesh,
        scratch_types=[pltpu.VMEM((n,), I32), pltpu.VMEM((cap,), I32),
                       pltpu.VMEM((chunk, words), table.dtype), pltpu.SemaphoreType.DMA],
        compiler_params=pltpu.CompilerParams(needs_layout_passes=False),
        name="sc_gather",
    )(table, post)


def _ffn_kernel(*refs, caps):
    ng = len(caps)
    xs_refs = refs[:ng]
    w1_ref, w3_ref, w2_ref = refs[ng:ng + 3]
    ye_refs = refs[ng + 3:2 * ng + 3]
    row_off = [sum(caps[:g]) for g in range(ng)]

    def unpack(words):
        w = lax.bitcast_convert_type(words, jnp.uint32)
        lo = lax.bitcast_convert_type(w << 16, F32).astype(BF16)
        hi = lax.bitcast_convert_type(w & jnp.uint32(0xFFFF0000), F32).astype(BF16)
        return jnp.concatenate([lo, hi], axis=1)

    xs = jnp.concatenate([unpack(r[...]) for r in xs_refs], axis=0)
    hid = _silu(_dot(xs, w1_ref[...].astype(BF16))) * _dot(xs, w3_ref[...].astype(BF16))
    ye = _dot(hid.astype(BF16), w2_ref[...].astype(BF16)).astype(BF16)
    for g in range(ng):
        ye_refs[g][...] = ye[row_off[g]:row_off[g] + caps[g]]


def _ffn_call(groups, w1, w3, w2):
    caps = tuple(g[1] for g in groups)
    n_exp, d, dff = w1.shape
    in_specs = [pl.BlockSpec((cap, d // 2), lambda e: (e, 0)) for cap in caps]
    in_specs += [pl.BlockSpec((None, d, dff), lambda e: (e, 0, 0)),
                 pl.BlockSpec((None, d, dff), lambda e: (e, 0, 0)),
                 pl.BlockSpec((None, dff, d), lambda e: (e, 0, 0))]
    return pl.pallas_call(
        functools.partial(_ffn_kernel, caps=caps),
        out_shape=tuple(jax.ShapeDtypeStruct((n_exp * cap, d), BF16) for cap in caps),
        grid=(n_exp,),
        in_specs=in_specs,
        out_specs=tuple(pl.BlockSpec((cap, d), lambda e: (e, 0)) for cap in caps),
        compiler_params=_cparams(("arbitrary",)),
        name="ffn",
    )(*[g[0] for g in groups], w1, w3, w2)


def _combine_kernel(tbl_ref, ye_hbm, pos_ref, aff_ref, x1_ref, mod_ref, gpo_ref, y_ref, buf, sem, xbuf, xsem,
                    acc_scr, *, cap, blocks_per_tile):
    i = pl.program_id(0)
    nsteps = pl.num_programs(0)
    d = x1_ref.shape[1]
    lanes = pos_ref.shape[1]
    win = COMBINE_WINDOW
    last_start = ye_hbm.shape[0] - win
    slot = i % 2

    def first_row(step, e):
        return tbl_ref[step * blocks_per_tile, e] + e * cap

    def window_start(first, k):
        unclamped = (first // BF16_ROWS_PER_TILE) * BF16_ROWS_PER_TILE + k * win
        return unclamped, jnp.minimum(unclamped, last_start)

    def fetch(step, to_slot, e):
        start = window_start(first_row(step, e), 0)[1]
        return pltpu.make_async_copy(ye_hbm.at[pl.ds(pl.multiple_of(start, BF16_ROWS_PER_TILE), win), :],
                                     buf.at[to_slot, pl.ds(e * win, win), :], sem.at[to_slot, e])

    @pl.when(i == 0)
    def _prime():
        for e in range(N_EXPERTS):
            fetch(0, 0, e).start()

    @pl.when(i + 1 < nsteps)
    def _ahead():
        for e in range(N_EXPERTS):
            fetch(i + 1, 1 - slot, e).start()

    for e in range(N_EXPERTS):
        fetch(i, slot, e).wait()
    lane_row = lax.broadcasted_iota(I32, (1, win), 1)
    pieces = []
    for e in range(N_EXPERTS):
        pcol = pos_ref[:, e:e + 1]
        grow = jnp.where(pcol >= 0, pcol + e * cap, -1)
        start = window_start(first_row(i, e), 0)[1]
        pieces.append(jnp.where(grow == start + lane_row, aff_ref[:, e:e + 1], 0.0).astype(BF16))
    total = None
    for e0 in range(0, N_EXPERTS, COMBINE_EXPERT_GROUP):
        grp = jnp.concatenate(pieces[e0:e0 + COMBINE_EXPERT_GROUP], axis=1)
        part = _dot(grp, buf[slot, pl.ds(e0 * win, COMBINE_EXPERT_GROUP * win), :])
        total = part if total is None else total + part
    acc_scr[...] = total

    def extra_windows(e):
        covered = window_start(first_row(i, e), 1)[0]
        return jnp.maximum(first_row(i + 1, e) - covered + win - 1, 0) // win

    def expert_extra(e, carry):
        first = first_row(i, e)
        extra = extra_windows(e)

        def more(k, c):
            unclamped, start = window_start(first, k)
            cp = pltpu.make_async_copy(ye_hbm.at[pl.ds(pl.multiple_of(start, BF16_ROWS_PER_TILE), win), :],
                                       xbuf, xsem)
            cp.start()
            cp.wait()
            at_e = lax.broadcasted_iota(I32, (1, lanes), 1) == e
            pcol = jnp.sum(jnp.where(at_e, pos_ref[...].astype(F32), 0.0), axis=1, keepdims=True).astype(I32)
            wcol = jnp.sum(jnp.where(at_e, aff_ref[...], 0.0), axis=1, keepdims=True)
            grow = jnp.where(pcol >= 0, pcol + e * cap, -1)
            hit = (grow == start + lane_row) & (grow >= unclamped)
            acc_scr[...] += _dot(jnp.where(hit, wcol, 0.0).astype(BF16), xbuf[...])
            return c

        lax.fori_loop(1, 1 + extra, more, 0)
        return carry

    any_extra = extra_windows(0)
    for e in range(1, N_EXPERTS):
        any_extra = any_extra + extra_windows(e)

    @pl.when(any_extra > 0)
    def _overflow():
        lax.fori_loop(0, N_EXPERTS, expert_extra, 0)

    m = mod_ref[0]
    y_ref[...] = x1_ref[...] + m[:, 5 * d:6 * d] * _rms(acc_scr[...], gpo_ref[...])


def _combine_call(tbl, ye, pos, aff, x1, mod3, mod_row_of_tile, g_po, cap, tag):
    n, d = x1.shape
    tm = COMBINE_TILE
    lanes = pos.shape[1]
    grid_spec = pltpu.PrefetchScalarGridSpec(
        num_scalar_prefetch=1,
        grid=(n // tm,),
        in_specs=[pl.BlockSpec(memory_space=pl.ANY),
                  pl.BlockSpec((tm, lanes), lambda i, t: (i, 0)),
                  pl.BlockSpec((tm, lanes), lambda i, t: (i, 0)),
                  pl.BlockSpec((tm, d), lambda i, t: (i, 0)),
                  pl.BlockSpec((1, 1, 6 * d), lambda i, t: (mod_row_of_tile(i, tm), 0, 0)),
                  pl.BlockSpec((1, d), lambda i, t: (0, 0))],
        out_specs=pl.BlockSpec((tm, d), lambda i, t: (i, 0)),
        scratch_shapes=[pltpu.VMEM((2, N_EXPERTS * COMBINE_WINDOW, d), BF16),
                        pltpu.SemaphoreType.DMA((2, N_EXPERTS)),
                        pltpu.VMEM((COMBINE_WINDOW, d), BF16),
                        pltpu.SemaphoreType.DMA(()),
                        pltpu.VMEM((tm, d), F32)],
    )
    return pl.pallas_call(
        functools.partial(_combine_kernel, cap=cap, blocks_per_tile=tm // ROUTE_BLOCK),
        out_shape=jax.ShapeDtypeStruct((n, d), F32),
        grid_spec=grid_spec,
        compiler_params=_cparams(("arbitrary",)),
        name="combine_" + tag,
    )(tbl, ye, pos, aff, x1, mod3, g_po)


def _rope_tables(seq):
    f32 = np.float32
    rows = seq // GRID_W
    r = np.repeat(np.arange(rows), GRID_W).astype(f32)
    col = np.tile(np.arange(GRID_W), rows).astype(f32)
    pairs = HEAD_DIM // 4
    freqs = np.power(f32(ROPE_THETA), -np.arange(pairs, dtype=f32) / f32(pairs)).astype(f32)
    ang = np.concatenate([r[:, None] * freqs, col[:, None] * freqs], axis=-1).astype(f32)
    cos = np.repeat(np.cos(ang), 2, axis=-1).astype(f32)
    sin = np.repeat(np.sin(ang), 2, axis=-1).astype(f32)
    even = (np.arange(HEAD_DIM) % 2) == 0
    zero = f32(0.0)
    return (jnp.asarray(cos), jnp.asarray(np.where(even, -sin, zero).astype(f32)),
            jnp.asarray(np.where(even, zero, sin).astype(f32)))


def _trunk_to_routing(x, mod3, mod_row_of_tile, rope_tabs, ctx, lw, tag):
    (g_pre_mix, g_post_mix, g_pre_ffn, g_post_ffn, w_in_p, g_q, g_k, wgf, bgf, wgb, bgb, g_gla,
     w_pa, w_pg, w_out, w_r, w1, w3, w2) = lw
    batch, seq, d = x.shape
    n = batch * seq
    x2 = x.reshape(n, d)
    (q_a, k_a, v_a, q_g, k_g, v_g, r_g, lg_f, lg_b, gates) = _inproj_call(
        x2, mod3, mod_row_of_tile, g_pre_mix, w_in_p, g_q, g_k, wgf, bgf, wgb, bgb, rope_tabs, batch, seq)
    if ctx is None:
        o_a = _attn_call(q_a, k_a, v_a, None, None)
        o_g, s_f, s_b = _gla_call(q_g, k_g, v_g, lg_f, lg_b, r_g, g_gla, None, None, batch, seq, True,
                                  GLA_CTX_HEADS_PER_STEP, GLA_CTX_SEQS_PER_STEP)
    else:
        ck, cv, s_f0, s_b0 = ctx
        o_a = _attn_call(q_a, k_a, v_a, ck, cv)
        (o_g,) = _gla_call(q_g, k_g, v_g, lg_f, lg_b, r_g, g_gla, s_f0, s_b0, batch, seq, False,
                           GLA_LAT_HEADS_PER_STEP, 1)
        s_f = s_b = None
    x1, h, aff = _outproj_call(o_a, o_g, gates, x2, mod3, mod_row_of_tile, w_pa, w_pg, w_out,
                               g_post_mix, g_pre_ffn, w_r, tag)
    cap = (EC_CAPACITY_FACTOR * n) // N_EXPERTS
    pos, post, tbl = _route_call(aff, cap, tag)
    xs = _sc_gather_call(h, post, cap)
    return dict(x1=x1, xs=xs, aff=aff, pos=pos, tbl=tbl, cap=cap,
                mod_row_of_tile=mod_row_of_tile, tag=tag, shape=(batch, seq, d)), (k_a, v_a, s_f, s_b)


def _expert_ffn(groups, mod3, g_post_ffn, w1, w3, w2):
    yes = _ffn_call([(g["xs"], g["cap"]) for g in groups], w1, w3, w2)
    outs = []
    for g, ye in zip(groups, yes):
        y = _combine_call(g["tbl"], ye, g["pos"], g["aff"], g["x1"], mod3, g["mod_row_of_tile"], g_post_ffn,
                          g["cap"], g["tag"])
        outs.append(y.reshape(g["shape"]))
    return outs


def kernel(x_prompt, x_sample, cache_k, cache_v, state_gla_fwd, state_gla_bwd, c, c_ctx, g_pre_mix, g_post_mix, g_pre_ffn, g_post_ffn, w_mod, b_mod, w_in, g_q, g_k, w_gk2_f, b_gk_f, w_gk2_b, b_gk_b, g_gla, w_pa, w_pg, w_out, w_router, w1, w3, w2):
    depth = w_in.shape[0]
    assert depth == 1, "single trunk layer"
    d = x_prompt.shape[-1]
    dec_batch, dec_seq, _ = x_sample.shape
    assert dec_batch + 1 <= MOD_ROWS
    l = 0
    mod3, w_in_p = _mod_call(c_ctx.reshape(1, d), c, w_mod[l], b_mod[l].reshape(1, -1), jnp.swapaxes(w_in[l], 0, 1))
    row = lambda a: a[l].reshape(1, -1)
    lw = (row(g_pre_mix), row(g_post_mix), row(g_pre_ffn), row(g_post_ffn), w_in_p, row(g_q), row(g_k),
          w_gk2_f[l], row(b_gk_f), w_gk2_b[l], row(b_gk_b), row(g_gla),
          w_pa[l], w_pg[l], w_out[l], jnp.swapaxes(w_router[l], 0, 1), w1[l], w3[l], w2[l])

    gp, (nk, nv, nsf, nsb) = _trunk_to_routing(x_prompt, mod3, lambda i, tm: 0, None, None, lw, "ctx")
    ctx = (cache_k, cache_v, state_gla_fwd, state_gla_bwd)
    gs, _ = _trunk_to_routing(x_sample, mod3, lambda i, tm: 1 + (i * tm) // dec_seq, _rope_tables(dec_seq),
                              ctx, lw, "lat")
    yp, ys = _expert_ffn([gp, gs], mod3, lw[3], lw[16], lw[17], lw[18])
    return (yp, ys, nk, nv, nsf, nsb)
```

```python
import functools

import jax
import jax.numpy as jnp
import numpy as np
from jax import lax
from jax.experimental import pallas as pl
from jax.experimental.pallas import tpu as pltpu
from jax.experimental.pallas import tpu_sc as plsc

F32 = jnp.float32
BF16 = jnp.bfloat16
I32 = jnp.int32

N_HEADS = 8
N_KV_HEADS = 2
HEAD_DIM = 128
GRID_W = 64
ROPE_THETA = 10000.0
GLA_HEADS = 4
GLA_GATE_RANK = 16
GLA_GATE_NORM = 16.0
GLA_CHUNK = 64
N_EXPERTS = 16
EC_CAPACITY_FACTOR = 2
EPS = 1e-6
LOG2_E = 1.4426950408889634

V7X_LANES = 128
V7X_VMEM_BYTES = 64 * 1024 * 1024
V7X_VMEM_RESERVE_BYTES = 6 * 1024 * 1024
BF16_ROWS_PER_TILE = 16
V7X_SC_CORES = 2
V7X_SC_SUBCORES = 16
V7X_SC_LANES = 16

TOKEN_TILE = 512
OUTPROJ_TILE = 512
OUTPROJ_ROW_GROUPS = 4
ATTN_Q_TILE = 256
ATTN_SEQS_PER_STEP = 4
GLA_BLOCK = 256
GLA_HEADS_PER_STAGE_GROUP = 2
GLA_CTX_HEADS_PER_STEP = 4
GLA_CTX_SEQS_PER_STEP = 4
GLA_LAT_HEADS_PER_STEP = 2
ROUTE_BLOCK = 256
SC_GATHER_CHUNK = 128
COMBINE_TILE = 512
COMBINE_WINDOW = 128
COMBINE_EXPERT_GROUP = 2
MOD_ROWS = 8
MOD_STEPS = 12
MOD_WEIGHT_SLABS = 11


def _cparams(semantics):
    return pltpu.CompilerParams(dimension_semantics=semantics,
                                vmem_limit_bytes=V7X_VMEM_BYTES - V7X_VMEM_RESERVE_BYTES)


def _sigmoid(x):
    return 0.5 * (jnp.tanh(0.5 * x) + 1.0)


def _silu(x):
    return x * _sigmoid(x)


def _log_sigmoid(x):
    return jnp.minimum(x, 0.0) - jnp.log1p(jnp.exp(-jnp.abs(x)))


def _rms(x, g):
    ms = jnp.mean(x * x, axis=-1, keepdims=True)
    return x * lax.rsqrt(ms + EPS) * g


def _dot(a, b):
    return jnp.dot(a, b, preferred_element_type=F32)


def _dot_nt(a, b):
    return lax.dot_general(a, b, (((1,), (1,)), ((), ())), preferred_element_type=F32)


def _mod_kernel(cctx_ref, c_ref, w_ref, b_ref, win_ref, o_ref, wout_ref):
    rows, d = o_ref.shape[0], c_ref.shape[1]
    ridx = lax.broadcasted_iota(I32, (rows, d), 0)
    cc = jnp.where(ridx == 0, jnp.broadcast_to(cctx_ref[...], (rows, d)), 0.0)
    for b in range(c_ref.shape[0]):
        cc = jnp.where(ridx == b + 1, jnp.broadcast_to(c_ref[b:b + 1, :], (rows, d)), cc)
    out = _dot(_silu(cc).astype(BF16), w_ref[...].astype(BF16)) + b_ref[...]
    for r in range(rows):
        o_ref[r] = out[r:r + 1, :]
    wout_ref[...] = win_ref[...].astype(BF16)


def _mod_call(c_ctx, c, w_mod, b_mod, w_in_t):
    d, n6 = w_mod.shape
    dec_batch = c.shape[0]
    din = w_in_t.shape[0]
    steps = MOD_STEPS
    tn = n6 // steps
    slabs = MOD_WEIGHT_SLABS
    rows = din // slabs
    assert tn * steps == n6 and slabs * rows == din and rows % BF16_ROWS_PER_TILE == 0 and slabs <= steps
    slab = lambda j: (jnp.minimum(j, slabs - 1), 0)
    return pl.pallas_call(
        _mod_kernel,
        out_shape=(jax.ShapeDtypeStruct((MOD_ROWS, 1, n6), F32), jax.ShapeDtypeStruct((din, d), BF16)),
        grid=(steps,),
        in_specs=[pl.BlockSpec((1, d), lambda j: (0, 0)),
                  pl.BlockSpec((dec_batch, d), lambda j: (0, 0)),
                  pl.BlockSpec((d, tn), lambda j: (0, j)),
                  pl.BlockSpec((1, tn), lambda j: (0, j)),
                  pl.BlockSpec((rows, d), slab)],
        out_specs=(pl.BlockSpec((MOD_ROWS, 1, tn), lambda j: (0, 0, j)), pl.BlockSpec((rows, d), slab)),
        compiler_params=_cparams(("arbitrary",)),
        name="mod",
    )(c_ctx, c, w_mod, b_mod, w_in_t)


def _inproj_layout(d):
    aq, akv = N_HEADS * HEAD_DIM, N_KV_HEADS * HEAD_DIM
    gk, gv = d // 2, d
    names = ("q_a", "k_a", "v_a", "q_g", "k_g", "v_g", "r_g", "gk_f", "gk_b", "gates")
    widths = (aq, akv, akv, gk, gk, gv, gv, GLA_GATE_RANK, GLA_GATE_RANK, 2 * d)
    off, o = {}, 0
    for nme, w in zip(names, widths):
        off[nme] = (o, o + w)
        o += w
    off["gk"] = (off["gk_f"][0], off["gk_f"][0] + V7X_LANES)
    return off, o


def _inproj_kernel(*refs, rope, d):
    if rope:
        (x_ref, mod_ref, gpre_ref, w_ref, gq_ref, gk_ref, wgf_ref, bgf_ref, wgb_ref, bgb_ref,
         cos_ref, se_ref, so_ref, *outs) = refs
    else:
        (x_ref, mod_ref, gpre_ref, w_ref, gq_ref, gk_ref, wgf_ref, bgf_ref, wgb_ref, bgb_ref,
         *outs) = refs
    qa_ref, k_ref, v_ref, qg_ref, kg_ref, vg_ref, rg_ref, lgf_ref, lgb_ref, gate_ref = outs
    off, _ = _inproj_layout(d)
    m = mod_ref[0]
    h = _rms(x_ref[...], gpre_ref[...]) * (1.0 + m[:, d:2 * d]) + m[:, 0:d]
    hb = h.astype(BF16)

    def proj(name):
        a, b = off[name]
        return _dot_nt(hb, w_ref[a:b, :])

    def qk_norm(y, g_ref):
        y = _rms(y, g_ref[...])
        if rope:
            nxt = pltpu.roll(y, HEAD_DIM - 1, axis=1)
            prv = pltpu.roll(y, 1, axis=1)
            y = y * cos_ref[...] + nxt * se_ref[...] + prv * so_ref[...]
        return y

    gk = proj("gk").astype(BF16)

    q = proj("q_a")
    scale = HEAD_DIM ** -0.5
    for hd in range(N_HEADS):
        sl = slice(hd * HEAD_DIM, (hd + 1) * HEAD_DIM)
        qa_ref[:, sl] = (qk_norm(q[:, sl], gq_ref) * scale).astype(BF16)

    k = proj("k_a")
    v = proj("v_a")
    tb, _, _, ts, _ = k_ref.shape
    for kv in range(N_KV_HEADS):
        sl = slice(kv * HEAD_DIM, (kv + 1) * HEAD_DIM)
        k_ref[:, 0, kv] = qk_norm(k[:, sl], gk_ref).reshape(tb, ts, HEAD_DIM)
        v_ref[:, 0, kv] = v[:, sl].reshape(tb, ts, HEAD_DIM)

    dk = (d // 2) // GLA_HEADS
    qg_ref[...] = (proj("q_g") * (dk ** -0.5)).astype(BF16)
    kg_ref[...] = proj("k_g").astype(BF16)
    vg_ref[...] = proj("v_g").astype(BF16)
    rg_ref[...] = _silu(proj("r_g")).astype(BF16)

    gate_ref[...] = _sigmoid(proj("gates")).astype(BF16)

    def rank_rows(w_ref, at):
        rank, width = w_ref.shape
        parts = [jnp.zeros((at, width), F32)] if at else []
        parts += [w_ref[...], jnp.zeros((V7X_LANES - at - rank, width), F32)]
        return jnp.concatenate(parts, axis=0).astype(BF16)

    lgf_ref[...] = _log_sigmoid(_dot(gk, rank_rows(wgf_ref, 0)) + bgf_ref[...]) * (LOG2_E / GLA_GATE_NORM)
    lgb_ref[...] = (_log_sigmoid(_dot(gk, rank_rows(wgb_ref, GLA_GATE_RANK)) + bgb_ref[...])
                    * (LOG2_E / GLA_GATE_NORM))


def _inproj_call(x2, mod3, mod_row_of_tile, g_pre, w_in_p, g_q, g_k, wgf, bgf, wgb, bgb, rope_tabs,
                 batch, seq):
    n, d = x2.shape
    tm = TOKEN_TILE
    _, dinp = _inproj_layout(d)
    rope = rope_tabs is not None
    gk_w = d // 2
    if seq >= tm:
        tb, ts, per = 1, tm, seq // tm
        kv_map = lambda i: (i // per, 0, 0, i % per, 0)
    else:
        tb, ts, per = tm // seq, seq, 1
        kv_map = lambda i: (i, 0, 0, 0, 0)
    row = lambda i: (i, 0)
    const = lambda i: (0, 0)
    in_specs = [
        pl.BlockSpec((tm, d), row),
        pl.BlockSpec((1, 1, 6 * d), lambda i: (mod_row_of_tile(i, tm), 0, 0)),
        pl.BlockSpec((1, d), const),
        pl.BlockSpec((dinp, d), const, pipeline_mode=pl.Buffered(1)),
        pl.BlockSpec((1, HEAD_DIM), const),
        pl.BlockSpec((1, HEAD_DIM), const),
        pl.BlockSpec((GLA_GATE_RANK, gk_w), const),
        pl.BlockSpec((1, gk_w), const),
        pl.BlockSpec((GLA_GATE_RANK, gk_w), const),
        pl.BlockSpec((1, gk_w), const),
    ]
    args = [x2, mod3, g_pre, w_in_p, g_q, g_k, wgf, bgf, wgb, bgb]
    if rope:
        tab = pl.BlockSpec((tm, HEAD_DIM), lambda i: (i % per, 0))
        in_specs += [tab, tab, tab]
        args += list(rope_tabs)
    kv_shape = jax.ShapeDtypeStruct((batch, 1, N_KV_HEADS, seq, HEAD_DIM), F32)
    kv_spec = pl.BlockSpec((tb, 1, N_KV_HEADS, ts, HEAD_DIM), kv_map)
    out_shape = (
        jax.ShapeDtypeStruct((n, N_HEADS * HEAD_DIM), BF16), kv_shape, kv_shape,
        jax.ShapeDtypeStruct((n, gk_w), BF16), jax.ShapeDtypeStruct((n, gk_w), BF16),
        jax.ShapeDtypeStruct((n, d), BF16), jax.ShapeDtypeStruct((n, d), BF16),
        jax.ShapeDtypeStruct((n, gk_w), F32), jax.ShapeDtypeStruct((n, gk_w), F32),
        jax.ShapeDtypeStruct((n, 2 * d), BF16),
    )
    out_specs = (
        pl.BlockSpec((tm, N_HEADS * HEAD_DIM), row), kv_spec, kv_spec,
        pl.BlockSpec((tm, gk_w), row), pl.BlockSpec((tm, gk_w), row),
        pl.BlockSpec((tm, d), row), pl.BlockSpec((tm, d), row),
        pl.BlockSpec((tm, gk_w), row), pl.BlockSpec((tm, gk_w), row),
        pl.BlockSpec((tm, 2 * d), row),
    )
    return pl.pallas_call(
        functools.partial(_inproj_kernel, rope=rope, d=d),
        out_shape=out_shape,
        grid=(n // tm,),
        in_specs=in_specs,
        out_specs=out_specs,
        compiler_params=_cparams(("parallel",)),
        name="inproj_lat" if rope else "inproj_ctx",
    )(*args)


def _attn_kernel(*refs, cached):
    if cached:
        q_ref, k_ref, v_ref, ck_ref, cv_ref, o_ref = refs
    else:
        q_ref, k_ref, v_ref, o_ref = refs
    seqs = k_ref.shape[0]
    tq = q_ref.shape[0] // seqs
    grp = N_HEADS // N_KV_HEADS
    chains = [(sq, kv) for sq in range(seqs) for kv in range(N_KV_HEADS)]

    def scores_of(sq, kv):
        rows = pl.ds(sq * tq, tq)
        kk = k_ref[sq, 0, kv].astype(BF16)
        vv = v_ref[sq, 0, kv].astype(BF16)
        if cached:
            kk = jnp.concatenate([ck_ref[sq, 0, kv].astype(BF16), kk], axis=0)
            vv = jnp.concatenate([cv_ref[sq, 0, kv].astype(BF16), vv], axis=0)
        heads = [q_ref[rows, (kv * grp + g) * HEAD_DIM:(kv * grp + g + 1) * HEAD_DIM] for g in range(grp)]
        return _dot_nt(jnp.concatenate(heads, axis=0), kk), vv

    look_ahead = cached
    ahead = scores_of(*chains[0]) if look_ahead else None
    for n, (sq, kv) in enumerate(chains):
        s, vv = ahead if look_ahead else scores_of(sq, kv)
        if look_ahead and n + 1 < len(chains):
            ahead = scores_of(*chains[n + 1])
        rows = pl.ds(sq * tq, tq)
        p = jnp.exp(s - jnp.max(s, axis=-1, keepdims=True))
        l = jnp.sum(p, axis=-1, keepdims=True)
        o = _dot(p.astype(BF16), vv) / l
        for g in range(grp):
            hd = kv * grp + g
            o_ref[rows, hd * HEAD_DIM:(hd + 1) * HEAD_DIM] = o[g * tq:(g + 1) * tq].astype(BF16)


def _attn_call(q_a, k_a, v_a, cache_k, cache_v):
    batch, _, _, seq, _ = k_a.shape
    n, aq = q_a.shape
    tq = min(ATTN_Q_TILE, seq)
    per = seq // tq
    cached = cache_k is not None
    seqs = ATTN_SEQS_PER_STEP if (per == 1 and not cached) else 1
    batch = batch // seqs
    own = pl.BlockSpec((seqs, 1, N_KV_HEADS, seq, HEAD_DIM), lambda b, j: (b, 0, 0, 0, 0))
    tq = tq * seqs
    in_specs = [pl.BlockSpec((tq, aq), lambda b, j: (b * per + j, 0)), own, own]
    args = [q_a, k_a, v_a]
    if cached:
        past = cache_k.shape[3]
        cspec = pl.BlockSpec((1, 1, N_KV_HEADS, past, HEAD_DIM), lambda b, j: (b, 0, 0, 0, 0))
        in_specs += [cspec, cspec]
        args += [cache_k, cache_v]
    return pl.pallas_call(
        functools.partial(_attn_kernel, cached=cached),
        out_shape=jax.ShapeDtypeStruct((n, aq), BF16),
        grid=(batch, per),
        in_specs=in_specs,
        out_specs=pl.BlockSpec((tq, aq), lambda b, j: (b * per + j, 0)),
        compiler_params=_cparams(("parallel", "parallel")),
        name="attn_lat" if cached else "attn_ctx",
    )(*args)


def _gla_kernel(*refs, nblk, heads, seqs, has_state, emit_state):
    refs = list(refs)
    q_ref, k_ref, v_ref, lgf_ref, lgb_ref, rg_ref, gg_ref = refs[:7]
    pos = 7
    if has_state:
        s0f_ref, s0b_ref = refs[pos:pos + 2]
        pos += 2
    og_ref = refs[pos]
    pos += 1
    if emit_state:
        sf_ref, sb_ref = refs[pos:pos + 2]
        pos += 2
    of_scr, ob_scr = refs[pos:pos + 2]

    blk = GLA_BLOCK
    ch = GLA_CHUNK
    nch = blk // ch
    dk = q_ref.shape[1] // heads
    dv = v_ref.shape[1] // heads
    shift = ch.bit_length() - 1
    row_in_chunk = lax.broadcasted_iota(I32, (blk, dk), 0) & (ch - 1)
    ri = lax.broadcasted_iota(I32, (blk, blk), 0)
    ci = lax.broadcasted_iota(I32, (blk, blk), 1)
    same = (ri >> shift) == (ci >> shift)
    mask_f = same & (ci <= ri)
    mask_b = same & (ci >= ri)

    def one_block(b0, hd, reverse):
        rows = pl.ds(b0, blk)
        kcols = slice(hd * dk, (hd + 1) * dk)
        q = q_ref[rows, kcols].astype(F32)
        k = k_ref[rows, kcols].astype(F32)
        v = v_ref[rows, hd * dv:(hd + 1) * dv]
        b = (lgb_ref if reverse else lgf_ref)[rows, kcols]
        mask = mask_b if reverse else mask_f
        s = 1
        while s < ch:
            if reverse:
                sh = pltpu.roll(b, blk - s, axis=0)
                b = b + jnp.where(row_in_chunk < ch - s, sh, 0.0)
            else:
                sh = pltpu.roll(b, s, axis=0)
                b = b + jnp.where(row_in_chunk >= s, sh, 0.0)
            s *= 2
        qe = (q * jnp.exp2(b)).astype(BF16)
        ke = (k * jnp.exp2(-b)).astype(BF16)
        scores = _dot_nt(qe, ke)
        end_row = [c * ch + (0 if reverse else ch - 1) for c in range(nch)]
        ends = [b[r:r + 1, :] for r in end_row]
        b_end = jnp.concatenate([jnp.broadcast_to(e, (ch, dk)) for e in ends], axis=0)
        kd = (k * jnp.exp2(b_end - b)).astype(BF16)
        return dict(scores=scores, mask=mask, qe=qe, kd=kd, v=v, ends=ends, reverse=reverse)

    def chunk_products(w):
        return [lax.dot_general(w["kd"][c * ch:(c + 1) * ch], w["v"][c * ch:(c + 1) * ch],
                                (((0,), (0,)), ((), ())), preferred_element_type=F32) for c in range(nch)]

    def intra_chunk(w):
        return _dot(jnp.where(w["mask"], w["scores"], 0.0).astype(BF16), w["v"])

    def across_chunks(w, kv, state):
        decay = jnp.exp2(jnp.concatenate(w["ends"] + [jnp.zeros((dk - nch, dk), F32)], axis=0)).T
        inter = [None] * nch
        for c in (range(nch - 1, -1, -1) if w["reverse"] else range(nch)):
            if state is None:
                inter[c] = jnp.zeros((ch, dv), F32)
                state = kv[c]
            else:
                inter[c] = _dot(w["qe"][c * ch:(c + 1) * ch], state.astype(BF16))
                state = decay[:, c:c + 1] * state + kv[c]
        return jnp.concatenate(inter, axis=0), state

    group = min(GLA_HEADS_PER_STAGE_GROUP, heads)
    for sq in range(seqs):
        srows = pl.ds(sq * nblk * blk, nblk * blk)
        for h0 in range(0, heads, group):
            hds = list(range(h0, h0 + group))
            states = {(hd, rev): ((s0b_ref if rev else s0f_ref)[sq, 0, hd] if has_state else None)
                      for hd in hds for rev in (False, True)}
            for i in range(nblk):
                block_of = {False: sq * nblk + i, True: sq * nblk + nblk - 1 - i}
                chains = [(hd, rev) for hd in hds for rev in (False, True)]
                work = {c: one_block(block_of[c[1]] * blk, c[0], c[1]) for c in chains}
                kvs = {c: chunk_products(work[c]) for c in chains}
                intra = {c: intra_chunk(work[c]) for c in chains}
                for c in chains:
                    hd, rev = c
                    inter, states[c] = across_chunks(work[c], kvs[c], states[c])
                    scr = ob_scr if rev else of_scr
                    scr[pl.ds(block_of[rev] * blk, blk), hd * dv:(hd + 1) * dv] = intra[c] + inter
            for hd in hds:
                vcols = slice(hd * dv, (hd + 1) * dv)
                if emit_state:
                    sf_ref[sq, 0, hd] = states[(hd, False)]
                    sb_ref[sq, 0, hd] = states[(hd, True)]
                o = of_scr[srows, vcols] + ob_scr[srows, vcols]
                og_ref[srows, vcols] = (_rms(o, gg_ref[...]) * rg_ref[srows, vcols].astype(F32)).astype(BF16)


def _gla_call(q_g, k_g, v_g, lg_f, lg_b, r_g, g_gla, state_f, state_b, batch, seq, emit_state, heads, seqs):
    n, gkw = q_g.shape
    d = v_g.shape[1]
    dk, dv = gkw // GLA_HEADS, d // GLA_HEADS
    has_state = state_f is not None
    nblk = seq // GLA_BLOCK
    batch = batch // seqs
    seq = seq * seqs
    kspec = pl.BlockSpec((seq, heads * dk), lambda b, h: (b, h))
    vspec = pl.BlockSpec((seq, heads * dv), lambda b, h: (b, h))
    sspec = pl.BlockSpec((seqs, 1, heads, dk, dv), lambda b, h: (b, 0, h, 0, 0))
    in_specs = [kspec, kspec, vspec, kspec, kspec, vspec, pl.BlockSpec((1, dv), lambda b, h: (0, 0))]
    args = [q_g, k_g, v_g, lg_f, lg_b, r_g, g_gla]
    if has_state:
        in_specs += [sspec, sspec]
        args += [state_f, state_b]
    out_shape = [jax.ShapeDtypeStruct((n, d), BF16)]
    out_specs = [vspec]
    if emit_state:
        st = jax.ShapeDtypeStruct((batch * seqs, 1, GLA_HEADS, dk, dv), F32)
        out_shape += [st, st]
        out_specs += [sspec, sspec]
    return pl.pallas_call(
        functools.partial(_gla_kernel, nblk=nblk, heads=heads, seqs=seqs, has_state=has_state,
                          emit_state=emit_state),
        out_shape=tuple(out_shape),
        grid=(batch, GLA_HEADS // heads),
        in_specs=in_specs,
        out_specs=tuple(out_specs),
        scratch_shapes=[pltpu.VMEM((seq, heads * dv), F32), pltpu.VMEM((seq, heads * dv), F32)],
        compiler_params=_cparams(("parallel", "parallel")),
        name="gla_ctx" if emit_state else "gla_lat",
    )(*args)


def _outproj_kernel(oa_ref, og_ref, gate_ref, x_ref, mod_ref, wpa_ref, wpg_ref, wout_ref, gpm_ref,
                    gpf_ref, wr_ref, x1_ref, h_ref, aff_ref):
    d = x_ref.shape[1]
    m = mod_ref[0]
    sub = x_ref.shape[0] // OUTPROJ_ROW_GROUPS
    groups = [pl.ds(g * sub, sub) for g in range(OUTPROJ_ROW_GROUPS)]
    wpa, wpg, wout = (w[...].astype(BF16) for w in (wpa_ref, wpg_ref, wout_ref))
    branch = [(_dot(oa_ref[r, :], wpa), _dot(og_ref[r, :], wpg)) for r in groups]
    mo = jnp.concatenate(
        [_dot((gate_ref[r, 0:d].astype(F32) * oa + gate_ref[r, d:2 * d].astype(F32) * og).astype(BF16), wout)
         for r, (oa, og) in zip(groups, branch)], axis=0)
    x1 = x_ref[...] + m[:, 2 * d:3 * d] * _rms(mo, gpm_ref[...])
    x1_ref[...] = x1
    hb = (_rms(x1, gpf_ref[...]) * (1.0 + m[:, 4 * d:5 * d]) + m[:, 3 * d:4 * d]).astype(BF16)
    bits = lax.bitcast_convert_type(hb.astype(F32), jnp.uint32)
    packed = (bits[:, 0:d // 2] >> 16) | (bits[:, d // 2:d] & jnp.uint32(0xFFFF0000))
    h_ref[...] = lax.bitcast_convert_type(packed, I32)
    wr = jnp.concatenate([wr_ref[...], jnp.zeros((V7X_LANES - wr_ref.shape[0], d), F32)], axis=0).astype(BF16)
    logits = _dot_nt(hb, wr)
    valid = lax.broadcasted_iota(I32, logits.shape, 1) < N_EXPERTS
    mx = jnp.max(jnp.where(valid, logits, -jnp.inf), axis=-1, keepdims=True)
    ex = jnp.where(valid, jnp.exp(logits - mx), 0.0)
    aff_ref[...] = ex / jnp.sum(ex, axis=-1, keepdims=True)


def _outproj_call(o_a, o_g, gates, x2, mod3, mod_row_of_tile, w_pa, w_pg, w_out, g_pm, g_pf, w_r, tag):
    n, d = x2.shape
    tm = OUTPROJ_TILE
    row = lambda i: (i, 0)
    const = lambda i: (0, 0)
    wspec = pl.BlockSpec((d, d), const, pipeline_mode=pl.Buffered(1))
    return pl.pallas_call(
        _outproj_kernel,
        out_shape=(jax.ShapeDtypeStruct((n, d), F32), jax.ShapeDtypeStruct((n, d // 2), I32),
                   jax.ShapeDtypeStruct((n, V7X_LANES), F32)),
        grid=(n // tm,),
        in_specs=[pl.BlockSpec((tm, d), row), pl.BlockSpec((tm, d), row), pl.BlockSpec((tm, 2 * d), row),
                  pl.BlockSpec((tm, d), row),
                  pl.BlockSpec((1, 1, 6 * d), lambda i: (mod_row_of_tile(i, tm), 0, 0)),
                  wspec, wspec, wspec, pl.BlockSpec((1, d), const), pl.BlockSpec((1, d), const),
                  pl.BlockSpec((N_EXPERTS, d), const)],
        out_specs=(pl.BlockSpec((tm, d), row), pl.BlockSpec((tm, d // 2), row),
                   pl.BlockSpec((tm, V7X_LANES), row)),
        compiler_params=_cparams(("parallel",)),
        name="outproj_" + tag,
    )(o_a, o_g, gates, x2, mod3, w_pa, w_pg, w_out, g_pm, g_pf, w_r)


def _route_kernel(aff_ref, pos_ref, post_ref, tbl_ref, afft_scr, *, cap):
    n = aff_ref.shape[0]
    rb = ROUTE_BLOCK
    nb = n // rb
    lanes = aff_ref.shape[1]
    lane = lax.broadcasted_iota(I32, (1, lanes), 1)
    expert_lane = lane < N_EXPERTS
    tbl_ref[...] = jnp.zeros(tbl_ref.shape, I32)

    def to_token_lanes(c, carry):
        start = pl.multiple_of(c * rb, rb)
        afft_scr[c] = aff_ref[pl.ds(start, rb), :].T[0:N_EXPERTS, :]
        return carry

    lax.fori_loop(0, nb, to_token_lanes, 0)
    aff_t = afft_scr[...]

    def count(hit):
        return jnp.sum(jnp.sum(hit.astype(I32), axis=0), axis=1, keepdims=True)

    def bit_step(i, lo):
        t = lo | jnp.left_shift(jnp.int32(1), 30 - i)
        ge = aff_t >= lax.bitcast_convert_type(t, F32)[None]
        return jnp.where(count(ge) >= cap, t, lo)

    thr_bits = lax.fori_loop(0, 31, bit_step, jnp.zeros((N_EXPERTS, 1), I32))
    need_t = cap - count(aff_t > lax.bitcast_convert_type(thr_bits, F32)[None])

    def to_expert_lanes(col):
        full = jnp.concatenate([jnp.broadcast_to(col, (N_EXPERTS, lanes)),
                                jnp.zeros((lanes - N_EXPERTS, lanes), I32)], axis=0)
        return full.T[0:1, :]

    thr = lax.bitcast_convert_type(to_expert_lanes(thr_bits), F32)
    need = to_expert_lanes(need_t).astype(F32)
    capf = float(cap)

    r = lax.broadcasted_iota(I32, (rb, rb), 0)
    c_ = lax.broadcasted_iota(I32, (rb, rb), 1)
    tril = jnp.where(c_ <= r, 1.0, 0.0).astype(BF16)

    def blk_step(c, carry):
        eq_before, raw_before = carry
        start = pl.multiple_of(c * rb, rb)
        a = aff_ref[pl.ds(start, rb), :]
        gt = a > thr
        eq = a == thr
        eq_incl = _dot(tril, jnp.where(eq, 1.0, 0.0).astype(BF16)) + eq_before
        raw = (gt | (eq & (eq_incl <= need))) & expert_lane
        raw_incl = _dot(tril, jnp.where(raw, 1.0, 0.0).astype(BF16)) + raw_before
        sel = raw & (raw_incl <= capf)
        self_ = jnp.where(sel, 1.0, 0.0)
        incl = jnp.minimum(raw_incl, capf)
        sel_before = jnp.minimum(raw_before, capf)
        excl = incl - self_
        posb = jnp.where(sel, excl, -1.0).astype(I32)
        pos_ref[pl.ds(start, rb), :] = posb
        post_ref[:, pl.ds(start, rb)] = posb.T[0:N_EXPERTS, :]
        tbl_ref[pl.ds(c, 1), :] = sel_before.astype(I32)
        return (eq_incl[rb - 1:rb, :], raw_incl[rb - 1:rb, :])

    zero = jnp.zeros((1, lanes), F32)
    _, total = lax.fori_loop(0, nb, blk_step, (zero, zero), unroll=2)
    tbl_ref[pl.ds(nb, 1), :] = jnp.minimum(total, capf).astype(I32)


def _route_call(aff, cap, tag):
    n, lanes = aff.shape
    nb = n // ROUTE_BLOCK
    tbl_rows = -(-(nb + 1) // 8) * 8
    full = lambda *shape: pl.BlockSpec(shape, lambda: tuple(0 for _ in shape))
    return pl.pallas_call(
        functools.partial(_route_kernel, cap=cap),
        out_shape=(jax.ShapeDtypeStruct((n, lanes), I32),
                   jax.ShapeDtypeStruct((N_EXPERTS, n), I32),
                   jax.ShapeDtypeStruct((tbl_rows, lanes), I32)),
        in_specs=[full(n, lanes)],
        out_specs=(full(n, lanes), full(N_EXPERTS, n), full(tbl_rows, lanes)),
        scratch_shapes=[pltpu.VMEM((nb, N_EXPERTS, ROUTE_BLOCK), F32)],
        compiler_params=_cparams(()),
        name="route_" + tag,
    )(aff)


def _sc_gather_call(table, post, cap):
    n_exp, n = post.shape
    words = table.shape[1]
    workers = V7X_SC_CORES * V7X_SC_SUBCORES
    parts = workers // n_exp
    chunk = SC_GATHER_CHUNK
    lanes = V7X_SC_LANES
    per_part = cap // parts
    assert parts * n_exp == workers and per_part % chunk == 0 and n % lanes == 0
    mesh = plsc.VectorSubcoreMesh(core_axis_name="c", subcore_axis_name="s",
                                  num_cores=V7X_SC_CORES, num_subcores=V7X_SC_SUBCORES)

    def body(table_hbm, post_hbm, out_hbm, pos_v, idx_v, rows_v, sem):
        wid = lax.axis_index("s") * V7X_SC_CORES + lax.axis_index("c")
        e = wid // parts
        part = wid % parts
        pltpu.sync_copy(post_hbm.at[e], pos_v)
        lane = lax.iota(I32, lanes)

        @pl.loop(0, n, step=lanes)
        def _(t0):
            p = pos_v[pl.ds(t0, lanes)]
            plsc.store_scatter(idx_v, [p], lane + t0, mask=p >= 0)

        @pl.loop(0, per_part // chunk)
        def _(j):
            off = pl.multiple_of(part * per_part + j * chunk, chunk)
            pltpu.async_copy(table_hbm.at[idx_v.at[pl.ds(off, chunk)]], rows_v, sem).wait()
            pltpu.sync_copy(rows_v, out_hbm.at[pl.ds(e * cap + off, chunk)])

    return pl.kernel(
        body,
        out_type=jax.ShapeDtypeStruct((n_exp * cap, words), table.dtype),
        mesh=mesh,
        scratch_types=[pltpu.VMEM((n,), I32), pltpu.VMEM((cap,), I32),
                       pltpu.VMEM((chunk, words), table.dtype), pltpu.SemaphoreType.DMA],
        compiler_params=pltpu.CompilerParams(needs_layout_passes=False),
        name="sc_gather",
    )(table, post)


def _ffn_kernel(*refs, caps):
    ng = len(caps)
    xs_refs = refs[:ng]
    w1_ref, w3_ref, w2_ref = refs[ng:ng + 3]
    ye_refs = refs[ng + 3:2 * ng + 3]
    row_off = [sum(caps[:g]) for g in range(ng)]

    def unpack(words):
        w = lax.bitcast_convert_type(words, jnp.uint32)
        lo = lax.bitcast_convert_type(w << 16, F32).astype(BF16)
        hi = lax.bitcast_convert_type(w & jnp.uint32(0xFFFF0000), F32).astype(BF16)
        return jnp.concatenate([lo, hi], axis=1)

    xs = jnp.concatenate([unpack(r[...]) for r in xs_refs], axis=0)
    hid = _silu(_dot(xs, w1_ref[...].astype(BF16))) * _dot(xs, w3_ref[...].astype(BF16))
    ye = _dot(hid.astype(BF16), w2_ref[...].astype(BF16)).astype(BF16)
    for g in range(ng):
        ye_refs[g][...] = ye[row_off[g]:row_off[g] + caps[g]]


def _ffn_call(groups, w1, w3, w2):
    caps = tuple(g[1] for g in groups)
    n_exp, d, dff = w1.shape
    in_specs = [pl.BlockSpec((cap, d // 2), lambda e: (e, 0)) for cap in caps]
    in_specs += [pl.BlockSpec((None, d, dff), lambda e: (e, 0, 0)),
                 pl.BlockSpec((None, d, dff), lambda e: (e, 0, 0)),
                 pl.BlockSpec((None, dff, d), lambda e: (e, 0, 0))]
    return pl.pallas_call(
        functools.partial(_ffn_kernel, caps=caps),
        out_shape=tuple(jax.ShapeDtypeStruct((n_exp * cap, d), BF16) for cap in caps),
        grid=(n_exp,),
        in_specs=in_specs,
        out_specs=tuple(pl.BlockSpec((cap, d), lambda e: (e, 0)) for cap in caps),
        compiler_params=_cparams(("arbitrary",)),
        name="ffn",
    )(*[g[0] for g in groups], w1, w3, w2)


def _combine_kernel(tbl_ref, ye_hbm, pos_ref, aff_ref, x1_ref, mod_ref, gpo_ref, y_ref, buf, sem, xbuf, xsem,
                    acc_scr, *, cap, blocks_per_tile):
    i = pl.program_id(0)
    nsteps = pl.num_programs(0)
    d = x1_ref.shape[1]
    lanes = pos_ref.shape[1]
    win = COMBINE_WINDOW
    last_start = ye_hbm.shape[0] - win
    slot = i % 2

    def first_row(step, e):
        return tbl_ref[step * blocks_per_tile, e] + e * cap

    def window_start(first, k):
        unclamped = (first // BF16_ROWS_PER_TILE) * BF16_ROWS_PER_TILE + k * win
        return unclamped, jnp.minimum(unclamped, last_start)

    def fetch(step, to_slot, e):
        start = window_start(first_row(step, e), 0)[1]
        return pltpu.make_async_copy(ye_hbm.at[pl.ds(pl.multiple_of(start, BF16_ROWS_PER_TILE), win), :],
                                     buf.at[to_slot, pl.ds(e * win, win), :], sem.at[to_slot, e])

    @pl.when(i == 0)
    def _prime():
        for e in range(N_EXPERTS):
            fetch(0, 0, e).start()

    @pl.when(i + 1 < nsteps)
    def _ahead():
        for e in range(N_EXPERTS):
            fetch(i + 1, 1 - slot, e).start()

    for e in range(N_EXPERTS):
        fetch(i, slot, e).wait()
    lane_row = lax.broadcasted_iota(I32, (1, win), 1)
    pieces = []
    for e in range(N_EXPERTS):
        pcol = pos_ref[:, e:e + 1]
        grow = jnp.where(pcol >= 0, pcol + e * cap, -1)
        start = window_start(first_row(i, e), 0)[1]
        pieces.append(jnp.where(grow == start + lane_row, aff_ref[:, e:e + 1], 0.0).astype(BF16))
    total = None
    for e0 in range(0, N_EXPERTS, COMBINE_EXPERT_GROUP):
        grp = jnp.concatenate(pieces[e0:e0 + COMBINE_EXPERT_GROUP], axis=1)
        part = _dot(grp, buf[slot, pl.ds(e0 * win, COMBINE_EXPERT_GROUP * win), :])
        total = part if total is None else total + part
    acc_scr[...] = total

    def extra_windows(e):
        covered = window_start(first_row(i, e), 1)[0]
        return jnp.maximum(first_row(i + 1, e) - covered + win - 1, 0) // win

    def expert_extra(e, carry):
        first = first_row(i, e)
        extra = extra_windows(e)

        def more(k, c):
            unclamped, start = window_start(first, k)
            cp = pltpu.make_async_copy(ye_hbm.at[pl.ds(pl.multiple_of(start, BF16_ROWS_PER_TILE), win), :],
                                       xbuf, xsem)
            cp.start()
            cp.wait()
            at_e = lax.broadcasted_iota(I32, (1, lanes), 1) == e
            pcol = jnp.sum(jnp.where(at_e, pos_ref[...].astype(F32), 0.0), axis=1, keepdims=True).astype(I32)
            wcol = jnp.sum(jnp.where(at_e, aff_ref[...], 0.0), axis=1, keepdims=True)
            grow = jnp.where(pcol >= 0, pcol + e * cap, -1)
            hit = (grow == start + lane_row) & (grow >= unclamped)
            acc_scr[...] += _dot(jnp.where(hit, wcol, 0.0).astype(BF16), xbuf[...])
            return c

        lax.fori_loop(1, 1 + extra, more, 0)
        return carry

    any_extra = extra_windows(0)
    for e in range(1, N_EXPERTS):
        any_extra = any_extra + extra_windows(e)

    @pl.when(any_extra > 0)
    def _overflow():
        lax.fori_loop(0, N_EXPERTS, expert_extra, 0)

    m = mod_ref[0]
    y_ref[...] = x1_ref[...] + m[:, 5 * d:6 * d] * _rms(acc_scr[...], gpo_ref[...])


def _combine_call(tbl, ye, pos, aff, x1, mod3, mod_row_of_tile, g_po, cap, tag):
    n, d = x1.shape
    tm = COMBINE_TILE
    lanes = pos.shape[1]
    grid_spec = pltpu.PrefetchScalarGridSpec(
        num_scalar_prefetch=1,
        grid=(n // tm,),
        in_specs=[pl.BlockSpec(memory_space=pl.ANY),
                  pl.BlockSpec((tm, lanes), lambda i, t: (i, 0)),
                  pl.BlockSpec((tm, lanes), lambda i, t: (i, 0)),
                  pl.BlockSpec((tm, d), lambda i, t: (i, 0)),
                  pl.BlockSpec((1, 1, 6 * d), lambda i, t: (mod_row_of_tile(i, tm), 0, 0)),
                  pl.BlockSpec((1, d), lambda i, t: (0, 0))],
        out_specs=pl.BlockSpec((tm, d), lambda i, t: (i, 0)),
        scratch_shapes=[pltpu.VMEM((2, N_EXPERTS * COMBINE_WINDOW, d), BF16),
                        pltpu.SemaphoreType.DMA((2, N_EXPERTS)),
                        pltpu.VMEM((COMBINE_WINDOW, d), BF16),
                        pltpu.SemaphoreType.DMA(()),
                        pltpu.VMEM((tm, d), F32)],
    )
    return pl.pallas_call(
        functools.partial(_combine_kernel, cap=cap, blocks_per_tile=tm // ROUTE_BLOCK),
        out_shape=jax.ShapeDtypeStruct((n, d), F32),
        grid_spec=grid_spec,
        compiler_params=_cparams(("arbitrary",)),
        name="combine_" + tag,
    )(tbl, ye, pos, aff, x1, mod3, g_po)


def _rope_tables(seq):
    f32 = np.float32
    rows = seq // GRID_W
    r = np.repeat(np.arange(rows), GRID_W).astype(f32)
    col = np.tile(np.arange(GRID_W), rows).astype(f32)
    pairs = HEAD_DIM // 4
    freqs = np.power(f32(ROPE_THETA), -np.arange(pairs, dtype=f32) / f32(pairs)).astype(f32)
    ang = np.concatenate([r[:, None] * freqs, col[:, None] * freqs], axis=-1).astype(f32)
    cos = np.repeat(np.cos(ang), 2, axis=-1).astype(f32)
    sin = np.repeat(np.sin(ang), 2, axis=-1).astype(f32)
    even = (np.arange(HEAD_DIM) % 2) == 0
    zero = f32(0.0)
    return (jnp.asarray(cos), jnp.asarray(np.where(even, -sin, zero).astype(f32)),
            jnp.asarray(np.where(even, zero, sin).astype(f32)))


def _trunk_to_routing(x, mod3, mod_row_of_tile, rope_tabs, ctx, lw, tag):
    (g_pre_mix, g_post_mix, g_pre_ffn, g_post_ffn, w_in_p, g_q, g_k, wgf, bgf, wgb, bgb, g_gla,
     w_pa, w_pg, w_out, w_r, w1, w3, w2) = lw
    batch, seq, d = x.shape
    n = batch * seq
    x2 = x.reshape(n, d)
    (q_a, k_a, v_a, q_g, k_g, v_g, r_g, lg_f, lg_b, gates) = _inproj_call(
        x2, mod3, mod_row_of_tile, g_pre_mix, w_in_p, g_q, g_k, wgf, bgf, wgb, bgb, rope_tabs, batch, seq)
    if ctx is None:
        o_a = _attn_call(q_a, k_a, v_a, None, None)
        o_g, s_f, s_b = _gla_call(q_g, k_g, v_g, lg_f, lg_b, r_g, g_gla, None, None, batch, seq, True,
                                  GLA_CTX_HEADS_PER_STEP, GLA_CTX_SEQS_PER_STEP)
    else:
        ck, cv, s_f0, s_b0 = ctx
        o_a = _attn_call(q_a, k_a, v_a, ck, cv)
        (o_g,) = _gla_call(q_g, k_g, v_g, lg_f, lg_b, r_g, g_gla, s_f0, s_b0, batch, seq, False,
                           GLA_LAT_HEADS_PER_STEP, 1)
        s_f = s_b = None
    x1, h, aff = _outproj_call(o_a, o_g, gates, x2, mod3, mod_row_of_tile, w_pa, w_pg, w_out,
                               g_post_mix, g_pre_ffn, w_r, tag)
    cap = (EC_CAPACITY_FACTOR * n) // N_EXPERTS
    pos, post, tbl = _route_call(aff, cap, tag)
    xs = _sc_gather_call(h, post, cap)
    return dict(x1=x1, xs=xs, aff=aff, pos=pos, tbl=tbl, cap=cap,
                mod_row_of_tile=mod_row_of_tile, tag=tag, shape=(batch, seq, d)), (k_a, v_a, s_f, s_b)


def _expert_ffn(groups, mod3, g_post_ffn, w1, w3, w2):
    yes = _ffn_call([(g["xs"], g["cap"]) for g in groups], w1, w3, w2)
    outs = []
    for g, ye in zip(groups, yes):
        y = _combine_call(g["tbl"], ye, g["pos"], g["aff"], g["x1"], mod3, g["mod_row_of_tile"], g_post_ffn,
                          g["cap"], g["tag"])
        outs.append(y.reshape(g["shape"]))
    return outs


def kernel(x_prompt, x_sample, cache_k, cache_v, state_gla_fwd, state_gla_bwd, c, c_ctx, g_pre_mix, g_post_mix, g_pre_ffn, g_post_ffn, w_mod, b_mod, w_in, g_q, g_k, w_gk2_f, b_gk_f, w_gk2_b, b_gk_b, g_gla, w_pa, w_pg, w_out, w_router, w1, w3, w2):
    depth = w_in.shape[0]
    assert depth == 1, "single trunk layer"
    d = x_prompt.shape[-1]
    dec_batch, dec_seq, _ = x_sample.shape
    assert dec_batch + 1 <= MOD_ROWS
    l = 0
    mod3, w_in_p = _mod_call(c_ctx.reshape(1, d), c, w_mod[l], b_mod[l].reshape(1, -1), jnp.swapaxes(w_in[l], 0, 1))
    row = lambda a: a[l].reshape(1, -1)
    lw = (row(g_pre_mix), row(g_post_mix), row(g_pre_ffn), row(g_post_ffn), w_in_p, row(g_q), row(g_k),
          w_gk2_f[l], row(b_gk_f), w_gk2_b[l], row(b_gk_b), row(g_gla),
          w_pa[l], w_pg[l], w_out[l], jnp.swapaxes(w_router[l], 0, 1), w1[l], w3[l], w2[l])

    gp, (nk, nv, nsf, nsb) = _trunk_to_routing(x_prompt, mod3, lambda i, tm: 0, None, None, lw, "ctx")
    ctx = (cache_k, cache_v, state_gla_fwd, state_gla_bwd)
    gs, _ = _trunk_to_routing(x_sample, mod3, lambda i, tm: 1 + (i * tm) // dec_seq, _rope_tables(dec_seq),
                              ctx, lw, "lat")
    yp, ys = _expert_ffn([gp, gs], mod3, lw[3], lw[16], lw[17], lw[18])
    return (yp, ys, nk, nv, nsf, nsb)
```
